```python
import jax, jax.numpy as jnp
from jax import lax
import numpy as np

D_MODEL = 1024
BATCH = 8
SEQ = 8192
DEPTH = 1

HEAD_DIM = 64
N_Q_HEADS = 8
N_KV_HEADS = 2
GROUP = N_Q_HEADS // N_KV_HEADS
ATTN_WIDTH = N_Q_HEADS * HEAD_DIM
KV_WIDTH = N_KV_HEADS * HEAD_DIM
WINDOW = 128
BLOCK = 128
ROT_DIM = HEAD_DIM // 4
ROPE_THETA = 500000.0
CONV_WIDTH = D_MODEL - ATTN_WIDTH
CONV_K = 3
MIX_WIDTH = ATTN_WIDTH + CONV_WIDTH
IN_SPLITS = [ATTN_WIDTH, KV_WIDTH, KV_WIDTH, ATTN_WIDTH,
             CONV_WIDTH, CONV_WIDTH, CONV_WIDTH, CONV_WIDTH]
IN_WIDTH = sum(IN_SPLITS)
EPS = 1e-5

kernel_name = "hybrid_swa_sink_shortconv_block"


def rms_norm(x, g):
    xf = x.astype(jnp.float32)
    y = xf * lax.rsqrt(jnp.mean(xf * xf, axis=-1, keepdims=True) + EPS)
    return (y * g.astype(jnp.float32)).astype(x.dtype)


def partial_rope(t, pos):
    half = ROT_DIM // 2
    inv_freq = ROPE_THETA ** (-jnp.arange(0, ROT_DIM, 2, dtype=jnp.float32) / ROT_DIM)
    ang = pos.astype(jnp.float32)[:, None] * inv_freq[None, :]
    cos = jnp.cos(ang)[None, :, None, :]
    sin = jnp.sin(ang)[None, :, None, :]
    rot = t[..., :ROT_DIM].astype(jnp.float32)
    x1, x2 = rot[..., :half], rot[..., half:]
    r = jnp.concatenate([x1 * cos - x2 * sin, x2 * cos + x1 * sin], axis=-1)
    return jnp.concatenate([r.astype(t.dtype), t[..., ROT_DIM:]], axis=-1)


def sliding_window_sink_attention(q, k, v, sinks):
    B, S, _, D = q.shape
    nb = S // BLOCK
    qb = q.reshape(B, nb, BLOCK, N_KV_HEADS, GROUP, D)
    kb = k.reshape(B, nb, BLOCK, N_KV_HEADS, D)
    vb = v.reshape(B, nb, BLOCK, N_KV_HEADS, D)
    pad = ((0, 0), (1, 0), (0, 0), (0, 0), (0, 0))
    kk = jnp.concatenate([jnp.pad(kb, pad)[:, :-1], kb], axis=2)
    vv = jnp.concatenate([jnp.pad(vb, pad)[:, :-1], vb], axis=2)
    scale = 1.0 / np.sqrt(D)
    s = jnp.einsum('bnqkgd,bnskd->bnkgqs', qb, kk,
                   preferred_element_type=jnp.float32) * scale
    qi = jnp.arange(BLOCK)[:, None]
    kj = jnp.arange(2 * BLOCK)[None, :]
    delta = qi + BLOCK - kj
    band = (delta >= 0) & (delta < WINDOW)
    valid = (jnp.arange(nb)[:, None] * BLOCK + kj - BLOCK) >= 0
    mask = (band[None] & valid[:, None, :])[None, :, None, None]
    s = jnp.where(mask, s, -jnp.inf)
    sink = sinks.astype(jnp.float32).reshape(N_KV_HEADS, GROUP)[None, None, :, :, None, None]
    m = jnp.maximum(jnp.max(s, axis=-1, keepdims=True), sink)
    p = jnp.exp(s - m)
    denom = jnp.sum(p, axis=-1, keepdims=True) + jnp.exp(sink - m)
    prob = (p / denom).astype(v.dtype)
    o = jnp.einsum('bnkgqs,bnskd->bnqkgd', prob, vv,
                   preferred_element_type=jnp.float32)
    return o.reshape(B, S, N_Q_HEADS * D).astype(q.dtype)


def causal_short_conv(u, w):
    S = u.shape[1]
    up = jnp.pad(u, ((0, 0), (CONV_K - 1, 0), (0, 0)))
    y = w[0] * up[:, 0:S]
    for j in range(1, CONV_K):
        y = y + w[j] * up[:, j:j + S]
    return y


def _fwd_setup_inputs(seed: int = 0) -> dict:
    key = jax.random.key(seed)
    ks = jax.random.split(key, 8)
    f32 = jnp.float32
    x = jax.random.normal(ks[0], (BATCH, SEQ, D_MODEL), f32)
    norm_g = 1.0 + 0.02 * jax.random.normal(ks[1], (D_MODEL,), f32)
    w_in = jax.random.normal(ks[2], (D_MODEL, IN_WIDTH), f32) * D_MODEL ** -0.5
    sinks = 0.5 * jax.random.normal(ks[3], (N_Q_HEADS,), f32)
    conv_w = jax.random.normal(ks[4], (CONV_K, CONV_WIDTH), f32) * CONV_K ** -0.5
    w_out = jax.random.normal(ks[5], (MIX_WIDTH, D_MODEL), f32) * MIX_WIDTH ** -0.5
    final_g = 1.0 + 0.02 * jax.random.normal(ks[6], (D_MODEL,), f32)
    return {"x": x, "norm_g": norm_g, "w_in": w_in, "sinks": sinks,
            "conv_w": conv_w, "w_out": w_out, "final_g": final_g}


def _fwd_reference(x, norm_g, w_in, sinks, conv_w, w_out, final_g):
    B, S, _ = x.shape
    pos = jnp.arange(S, dtype=jnp.int32)
    h = x
    for _ in range(DEPTH):
        xn = rms_norm(h, norm_g)
        proj = jnp.einsum('bsd,de->bse', xn, w_in)
        offs = list(np.cumsum(IN_SPLITS)[:-1])
        q, k, v, g_attn, b_gate, c_gate, h_in, g_conv = jnp.split(proj, offs, axis=-1)
        q = partial_rope(q.reshape(B, S, N_Q_HEADS, HEAD_DIM), pos)
        k = partial_rope(k.reshape(B, S, N_KV_HEADS, HEAD_DIM), pos)
        v = v.reshape(B, S, N_KV_HEADS, HEAD_DIM)
        attn = sliding_window_sink_attention(q, k, v, sinks)
        y_attn = attn * jax.nn.silu(g_attn)
        y_conv = b_gate * causal_short_conv(c_gate * h_in, conv_w)
        y_conv = y_conv * jax.nn.silu(g_conv)
        mix = jnp.concatenate([y_attn, y_conv], axis=-1)
        h = h + jnp.einsum('bse,ed->bsd', mix, w_out)
    return rms_norm(h, final_g)


import jax as _jax
import jax.numpy as _jnp

TWIN_FORMAT = 'train_step'
FWD_PARAMS = ['x', 'norm_g', 'w_in', 'sinks', 'conv_w', 'w_out', 'final_g']
TWIN_WEIGHTS = ['norm_g', 'w_in', 'sinks', 'conv_w', 'w_out', 'final_g']
TWIN_DIFF_INPUT = 'x'
TWIN_INPUTS = ['x', 'norm_g', 'w_in', 'sinks', 'conv_w', 'w_out', 'final_g', 'loss_target', 'm_norm_g', 'm_w_in', 'm_sinks', 'm_conv_w', 'm_w_out', 'm_final_g', 'v_norm_g', 'v_w_in', 'v_sinks', 'v_conv_w', 'v_w_out', 'v_final_g']
TWIN_OUTPUTS = ['loss', 'grad_x', 'grad_norm_g', 'grad_w_in', 'grad_sinks', 'grad_conv_w', 'grad_w_out', 'grad_final_g', 'delta_norm_g', 'delta_w_in', 'delta_sinks', 'delta_conv_w', 'delta_w_out', 'delta_final_g', 'new_m_norm_g', 'new_m_w_in', 'new_m_sinks', 'new_m_conv_w', 'new_m_w_out', 'new_m_final_g', 'new_v_norm_g', 'new_v_w_in', 'new_v_sinks', 'new_v_conv_w', 'new_v_w_out', 'new_v_final_g']
TWIN_LEAF_KINDS = {'loss': 'loss', 'grad_x': 'grad_x', 'grad_norm_g': 'grad_w', 'grad_w_in': 'grad_w', 'grad_sinks': 'grad_w', 'grad_conv_w': 'grad_w', 'grad_w_out': 'grad_w', 'grad_final_g': 'grad_w', 'delta_norm_g': 'delta_w', 'delta_w_in': 'delta_w', 'delta_sinks': 'delta_w', 'delta_conv_w': 'delta_w', 'delta_w_out': 'delta_w', 'delta_final_g': 'delta_w', 'new_m_norm_g': 'new_m', 'new_m_w_in': 'new_m', 'new_m_sinks': 'new_m', 'new_m_conv_w': 'new_m', 'new_m_w_out': 'new_m', 'new_m_final_g': 'new_m', 'new_v_norm_g': 'new_v', 'new_v_w_in': 'new_v', 'new_v_sinks': 'new_v', 'new_v_conv_w': 'new_v', 'new_v_w_out': 'new_v', 'new_v_final_g': 'new_v'}


def _forward(args):
    return _fwd_reference(*[args[k] for k in FWD_PARAMS])


def _output_shape():
    out = _jax.eval_shape(lambda: _forward(_fwd_setup_inputs(0)))
    return out.shape, out.dtype

N_MICROBATCH = 1
ADAM_LR = 0.001
ADAM_B1 = 0.9
ADAM_B2 = 0.999
ADAM_EPS = 1e-08
ADAM_WD = 0.01
ADAM_STEP = 10
PER_EXAMPLE_BATCH_AXIS = {'x': 0, 'loss_target': 0}
SHARED_INPUTS = []
_WEIGHT_DTYPES = {'norm_g': _jnp.float32, 'w_in': _jnp.float32, 'sinks': _jnp.float32, 'conv_w': _jnp.float32, 'w_out': _jnp.float32, 'final_g': _jnp.float32}
MOMENT_SCALE = {'norm_g': 1.904663e-01, 'w_in': 1.079141e-01, 'sinks': 2.239798e-02, 'conv_w': 1.368956e-01, 'w_out': 9.561797e-02, 'final_g': 6.391215e+01}


def _to_microbatches(a, axis):
    t = _jnp.moveaxis(a, axis, 0)
    t = t.reshape((N_MICROBATCH, t.shape[0] // N_MICROBATCH) + t.shape[1:])
    return _jnp.moveaxis(t, 1, axis + 1)


def setup_inputs(seed: int = 0) -> dict:
    inp = _fwd_setup_inputs(seed)
    key = _jax.random.fold_in(_jax.random.key(seed), 7919)
    shape, _ = _output_shape()
    out = dict(inp)
    out["loss_target"] = _jax.random.normal(_jax.random.fold_in(key, 0), shape, _jnp.float32)
    for i, name in enumerate(TWIN_WEIGHTS):
        w = inp[name].astype(_jnp.float32)
        if MOMENT_SCALE is None:
            s = _jnp.sqrt(_jnp.mean(_jnp.square(w)) + 1e-30)
        else:
            s = MOMENT_SCALE[name]
        km, kv = _jax.random.split(_jax.random.fold_in(key, i + 1))
        out[name] = w
        out["m_" + name] = s * _jax.random.normal(km, w.shape, _jnp.float32)
        out["v_" + name] = (s * s) * _jax.random.uniform(kv, w.shape, _jnp.float32, 0.5, 1.5)
    if N_MICROBATCH > 1:
        for name, axis in PER_EXAMPLE_BATCH_AXIS.items():
            out[name] = _to_microbatches(out[name], axis)
    return {'x': out['x'], 'norm_g': out['norm_g'], 'w_in': out['w_in'], 'sinks': out['sinks'], 'conv_w': out['conv_w'], 'w_out': out['w_out'], 'final_g': out['final_g'], 'loss_target': out['loss_target'], 'm_norm_g': out['m_norm_g'], 'm_w_in': out['m_w_in'], 'm_sinks': out['m_sinks'], 'm_conv_w': out['m_conv_w'], 'm_w_out': out['m_w_out'], 'm_final_g': out['m_final_g'], 'v_norm_g': out['v_norm_g'], 'v_w_in': out['v_w_in'], 'v_sinks': out['v_sinks'], 'v_conv_w': out['v_conv_w'], 'v_w_out': out['v_w_out'], 'v_final_g': out['v_final_g']}


def _loss(weights, diff, rest, loss_target):
    with _jax.named_scope("forward"):
        args = {**rest, TWIN_DIFF_INPUT: diff, **{k: w.astype(_WEIGHT_DTYPES[k]) for k, w in weights.items()}}
        y = _forward(args)
    with _jax.named_scope("loss_head"):
        err = _jnp.square(y.astype(_jnp.float32) - loss_target)
        return 0.5 * _jnp.sum(_jnp.mean(err, axis=-1)) if err.ndim else 0.5 * err


def _adamw(w, g, m, v):
    m = ADAM_B1 * m + (1.0 - ADAM_B1) * g
    v = ADAM_B2 * v + (1.0 - ADAM_B2) * _jnp.square(g)
    m_hat = m / (1.0 - ADAM_B1 ** ADAM_STEP)
    v_hat = v / (1.0 - ADAM_B2 ** ADAM_STEP)
    delta = -ADAM_LR * (m_hat / (_jnp.sqrt(v_hat) + ADAM_EPS) + ADAM_WD * w)
    return delta, m, v


def reference(x, norm_g, w_in, sinks, conv_w, w_out, final_g, loss_target, m_norm_g, m_w_in, m_sinks, m_conv_w, m_w_out, m_final_g, v_norm_g, v_w_in, v_sinks, v_conv_w, v_w_out, v_final_g):
    given = dict(x=x, norm_g=norm_g, w_in=w_in, sinks=sinks, conv_w=conv_w, w_out=w_out, final_g=final_g, loss_target=loss_target, m_norm_g=m_norm_g, m_w_in=m_w_in, m_sinks=m_sinks, m_conv_w=m_conv_w, m_w_out=m_w_out, m_final_g=m_final_g, v_norm_g=v_norm_g, v_w_in=v_w_in, v_sinks=v_sinks, v_conv_w=v_conv_w, v_w_out=v_w_out, v_final_g=v_final_g)
    weights = {n: given[n] for n in TWIN_WEIGHTS}
    shared = {n: given[n] for n in SHARED_INPUTS}
    per_example = {n: given[n] for n in ['x']}
    grad_fn = _jax.value_and_grad(_loss, argnums=(0, 1))

    def one_microbatch(ex, loss_target):
        ex = dict(ex)
        diff = ex.pop(TWIN_DIFF_INPUT)
        return grad_fn(weights, diff, {**shared, **ex}, loss_target)

    if N_MICROBATCH == 1:
        loss, (grad_w, grad_x) = one_microbatch(per_example, given["loss_target"])
    else:
        def body(carry, xs):
            loss_sum, grad_sum = carry
            l_k, (gw_k, gx_k) = one_microbatch(xs[0], xs[1])
            with _jax.named_scope("update"):
                return (loss_sum + l_k, _jax.tree.map(_jnp.add, grad_sum, gw_k)), gx_k

        init = (_jnp.zeros((), _jnp.float32), _jax.tree.map(_jnp.zeros_like, weights))
        (loss, grad_w), grad_x = _jax.lax.scan(body, init, (per_example, given["loss_target"]))
    with _jax.named_scope("update"):
        delta_w, new_m, new_v = {}, {}, {}
        for n in TWIN_WEIGHTS:
            delta_w[n], new_m[n], new_v[n] = _adamw(weights[n], grad_w[n], given["m_" + n], given["v_" + n])
    return (loss, grad_x, *[grad_w[n] for n in TWIN_WEIGHTS], *[delta_w[n] for n in TWIN_WEIGHTS],
            *[new_m[n] for n in TWIN_WEIGHTS], *[new_v[n] for n in TWIN_WEIGHTS])
```

```python
import functools

import jax
import jax.numpy as jnp
from jax import lax
from jax.experimental import pallas as pl
from jax.experimental.pallas import tpu as pltpu

F32 = jnp.float32
BF16 = jnp.bfloat16

D_MODEL = 1024
HEAD_DIM = 64
ATTN_W = 512
KV_W = 128
CONV_W = 512
IN_W = 3328
REST_W = IN_W - ATTN_W - 2 * KV_W
BLOCK = 128
ROT_DIM = 16
ROPE_THETA = 500000.0
EPS = 1e-5
SCALE = 0.125
NEG = -1e30

N_CHIPS = 4
W_IN_BLK = IN_W // N_CHIPS
W_OUT_BLK = D_MODEL // N_CHIPS

ADAM_LR = 0.001
ADAM_B1 = 0.9
ADAM_B2 = 0.999
ADAM_EPS = 1e-08
ADAM_WD = 0.01
ADAM_STEP = 10

VMEM_LIMIT = 56 * 1024 * 1024
T_PROJ = 512
T_MIX = 256
SMALL_ROWS = 8
MESH = pl.DeviceIdType.MESH

_NT = (((1,), (1,)), ((), ()))
_TN = (((0,), (0,)), ((), ()))


def _params(sem=None):
    kw = dict(vmem_limit_bytes=VMEM_LIMIT)
    if sem is not None:
        kw["dimension_semantics"] = sem
    return pltpu.CompilerParams(**kw)


def _sigmoid(t):
    return 1.0 / (1.0 + jnp.exp(-t))


def _shift_down(t, prev8, k):
    rolled = pltpu.roll(t, k, 0)
    row = lax.broadcasted_iota(jnp.int32, t.shape, 0)
    for j in range(k):
        rolled = jnp.where(row == j, prev8[8 - k + j:8 - k + j + 1, :], rolled)
    return rolled


def _shift_up(t, next8, k):
    n = t.shape[0]
    rolled = pltpu.roll(t, n - k, 0)
    row = lax.broadcasted_iota(jnp.int32, t.shape, 0)
    for j in range(k):
        rolled = jnp.where(row == n - k + j, next8[j:j + 1, :], rolled)
    return rolled


def _rope(t, c, a, b):
    w = t.shape[1]
    reps = w // 128
    if reps > 1:
        c, a, b = (jnp.concatenate([z] * reps, axis=1) for z in (c, a, b))
    return t * c + pltpu.roll(t, w - 8, 1) * a + pltpu.roll(t, 8, 1) * b


def _rope_tables(seq):
    pos = jnp.arange(seq, dtype=jnp.int32)
    inv_freq = ROPE_THETA ** (-jnp.arange(0, ROT_DIM, 2, dtype=F32) / ROT_DIM)
    ang = pos.astype(F32)[:, None] * inv_freq[None, :]
    cos, sin = jnp.cos(ang), jnp.sin(ang)
    ones = jnp.ones((seq, HEAD_DIM - ROT_DIM), F32)
    zeros8 = jnp.zeros((seq, 8), F32)
    zeros = jnp.zeros((seq, HEAD_DIM - ROT_DIM), F32)
    c = jnp.concatenate([cos, cos, ones], axis=1)
    a = jnp.concatenate([-sin, zeros8, zeros], axis=1)
    b = jnp.concatenate([zeros8, sin, zeros], axis=1)
    return tuple(jnp.concatenate([z, z], axis=1) for z in (c, a, b))


def _lane_lo(shape):
    return lax.broadcasted_iota(jnp.int32, shape, 1) < HEAD_DIM


def _stack_heads(t, g):
    lo = _lane_lo((BLOCK, 128))
    parts = []
    for hh in range(4):
        pair = t[:, 256 * g + 128 * (hh // 2):256 * g + 128 * (hh // 2) + 128]
        keep = lo if hh % 2 == 0 else jnp.logical_not(lo)
        parts.append(jnp.where(keep, pair, jnp.zeros_like(pair)))
    return jnp.concatenate(parts, axis=0)


def _unstack_pair(o, pp):
    lo = _lane_lo((BLOCK, 128))
    return jnp.where(lo, o[256 * pp:256 * pp + 128], o[256 * pp + 128:256 * pp + 256])


def _band_mask(has_prev):
    r = lax.broadcasted_iota(jnp.int32, (4 * BLOCK, 2 * BLOCK), 0) % BLOCK
    kj = lax.broadcasted_iota(jnp.int32, (4 * BLOCK, 2 * BLOCK), 1)
    cur = (kj >= BLOCK) & (kj - BLOCK <= r)
    prev = (kj < BLOCK) & (kj > r)
    if has_prev is not True:
        prev = prev & has_prev
    return cur | prev


def _sink_col(sinks_ref, g):
    r = lax.broadcasted_iota(jnp.int32, (4 * BLOCK, 1), 0) // BLOCK
    col = jnp.full((4 * BLOCK, 1), sinks_ref[4 * g + 3], F32)
    for hh in range(3):
        col = jnp.where(r == hh, sinks_ref[4 * g + hh], col)
    return col


def _probs(qs, kd, sink_col, mask):
    s = lax.dot_general(qs, kd, _NT, preferred_element_type=F32)
    s = jnp.where(mask, s, NEG)
    m = jnp.maximum(jnp.max(s, axis=-1, keepdims=True), sink_col)
    p = jnp.exp(s - m)
    es = jnp.exp(sink_col - m)
    inv = 1.0 / (jnp.sum(p, axis=-1, keepdims=True) + es)
    return p * inv, es * inv


def _gather_weights(w_in, w_out, conv_w8):
    hi, ho = D_MODEL // 2, W_OUT_BLK // 2

    def body(wi_ref, wo_ref, cw_ref, wi_all, wo_all, cw_all, send_sems, recv_sems):
        x, y, c = lax.axis_index("x"), lax.axis_index("y"), lax.axis_index("c")
        me = 2 * x + y
        sibling = (x, y, 1 - c)
        chips = [(1 - x, y), (x, 1 - y), (1 - x, 1 - y)]

        wi_all[me] = wi_ref[...].astype(BF16)
        wo_all[me] = wo_ref[...].astype(BF16)
        cw_all[me] = cw_ref[...]

        def copies(k, chip, half, to):
            j = 2 * chip[0] + chip[1]
            refs = (wi_all.at[j, pl.ds(half * hi, hi)], wo_all.at[j, pl.ds(half * ho, ho)])
            return [pltpu.make_async_remote_copy(src_ref=r, dst_ref=r, send_sem=send_sems.at[2 * k + n],
                                                 recv_sem=recv_sems.at[2 * k + n], device_id=to,
                                                 device_id_type=MESH) for n, r in enumerate(refs)]

        def conv_copy(k, chip, to):
            r = cw_all.at[2 * chip[0] + chip[1]]
            return pltpu.make_async_remote_copy(src_ref=r, dst_ref=r, send_sem=send_sems.at[12 + k],
                                                recv_sem=recv_sems.at[12 + k], device_id=to, device_id_type=MESH)

        first = [cp for k, chip in enumerate(chips) for cp in copies(k, (x, y), c, (*chip, c))]
        first += [conv_copy(k, (x, y), (*chip, c)) for k, chip in enumerate(chips)]
        for cp in first:
            cp.start()
        passed = []
        for k, chip in enumerate(chips):
            for cp in copies(k, chip, c, (x, y, c)):
                cp.wait_recv()
            fwd = copies(3 + k, chip, c, sibling)
            for cp in fwd:
                cp.start()
            passed += fwd
        for k, chip in enumerate(chips):
            for cp in copies(3 + k, chip, 1 - c, (x, y, c)):
                cp.wait_recv()
            conv_copy(k, chip, (x, y, c)).wait_recv()
        for cp in first + passed:
            cp.wait_send()

    vmem = pl.BlockSpec(memory_space=pltpu.VMEM)
    return pl.pallas_call(
        body, name="gather_weights",
        out_shape=(jax.ShapeDtypeStruct((N_CHIPS, D_MODEL, W_IN_BLK), BF16),
                   jax.ShapeDtypeStruct((N_CHIPS, W_OUT_BLK, D_MODEL), BF16),
                   jax.ShapeDtypeStruct((N_CHIPS, 8, 128), F32)),
        in_specs=[vmem, vmem, vmem], out_specs=(vmem, vmem, vmem),
        scratch_shapes=[pltpu.SemaphoreType.DMA((15,)), pltpu.SemaphoreType.DMA((15,))],
        compiler_params=_params(),
    )(w_in, w_out, conv_w8)


def _reduce_grads(g_in, g_out, small):
    hi, ho = D_MODEL // 2, W_OUT_BLK // 2

    def body(gi_hbm, go_hbm, small_ref, gi_out, go_out, small_out,
             mine_i, mine_o, sib_i, sib_o, ici_i, ici_o, small_in, send_sems, recv_sems, local_sems):
        x, y, c = lax.axis_index("x"), lax.axis_index("y"), lax.axis_index("c")
        me = 2 * x + y
        my_dev = 4 * x + 2 * y + c
        sibling = (x, y, 1 - c)
        chips = [(1 - x, y), (x, 1 - y), (1 - x, 1 - y)]

        def remote(k, src, dst, to):
            return pltpu.make_async_remote_copy(src_ref=src, dst_ref=dst, send_sem=send_sems.at[k],
                                                recv_sem=recv_sems.at[k], device_id=to, device_id_type=MESH)

        small_cps = []
        for f in range(1, 8):
            fx, fy, fc = f >> 2, (f >> 1) & 1, f & 1
            to = (x ^ fx, y ^ fy, c ^ fc)
            small_cps.append(remote(10 + f, small_ref, small_in.at[f - 1], to))
        for cp in small_cps:
            cp.start()

        own = [pltpu.make_async_copy(gi_hbm.at[:, pl.ds(c * hi, hi)], mine_i, local_sems.at[0]),
               pltpu.make_async_copy(go_hbm.at[:, pl.ds(c * ho, ho)], mine_o, local_sems.at[1])]
        for cp in own:
            cp.start()
        to_sib = [remote(0, gi_hbm.at[:, pl.ds((1 - c) * hi, hi)], sib_i, sibling),
                  remote(1, go_hbm.at[:, pl.ds((1 - c) * ho, ho)], sib_o, sibling)]
        for cp in to_sib:
            cp.start()
        for cp in own:
            cp.wait()
        for cp in to_sib:
            cp.wait_recv()

        ici = []
        for k, chip in enumerate(chips):
            j = 2 * chip[0] + chip[1]
            mine_i[j] = mine_i[j] + sib_i[j]
            mine_o[j] = mine_o[j] + sib_o[j]
            ici += [remote(2 + 2 * k, mine_i.at[j], ici_i.at[k], (*chip, c)),
                    remote(3 + 2 * k, mine_o.at[j], ici_o.at[k], (*chip, c))]
            ici[-2].start()
            ici[-1].start()
        tot_i = mine_i[me] + sib_i[me]
        tot_o = mine_o[me] + sib_o[me]
        for k in range(3):
            ici[2 * k].wait_recv()
            ici[2 * k + 1].wait_recv()
            tot_i = tot_i + ici_i[k]
            tot_o = tot_o + ici_o[k]
        gi_out[pl.ds(c * hi, hi), :] = tot_i
        go_out[pl.ds(c * ho, ho), :] = tot_o

        swap = [remote(8, gi_out.at[pl.ds(c * hi, hi)], gi_out.at[pl.ds(c * hi, hi)], sibling),
                remote(9, go_out.at[pl.ds(c * ho, ho)], go_out.at[pl.ds(c * ho, ho)], sibling)]
        for cp in swap:
            cp.start()

        for cp in small_cps:
            cp.wait_recv()
        total = jnp.zeros((SMALL_ROWS, D_MODEL), F32)
        for d in range(8):
            slot = jnp.maximum((d ^ my_dev) - 1, 0)
            total = total + jnp.where(d == my_dev, small_ref[...], small_in[slot])
        small_out[...] = total

        recv_swap = [remote(8, gi_out.at[pl.ds((1 - c) * hi, hi)], gi_out.at[pl.ds((1 - c) * hi, hi)], sibling),
                     remote(9, go_out.at[pl.ds((1 - c) * ho, ho)], go_out.at[pl.ds((1 - c) * ho, ho)], sibling)]
        for cp in recv_swap:
            cp.wait_recv()
        for cp in to_sib + ici + swap + small_cps:
            cp.wait_send()

    vmem = pl.BlockSpec(memory_space=pltpu.VMEM)
    anyspace = pl.BlockSpec(memory_space=pl.ANY)
    return pl.pallas_call(
        body, name="reduce_grads",
        out_shape=(jax.ShapeDtypeStruct((D_MODEL, W_IN_BLK), F32),
                   jax.ShapeDtypeStruct((W_OUT_BLK, D_MODEL), F32),
                   jax.ShapeDtypeStruct((SMALL_ROWS, D_MODEL), F32)),
        in_specs=[anyspace, anyspace, vmem], out_specs=(vmem, vmem, vmem),
        scratch_shapes=[pltpu.VMEM((N_CHIPS, hi, W_IN_BLK), F32), pltpu.VMEM((N_CHIPS, ho, D_MODEL), F32),
                        pltpu.VMEM((N_CHIPS, hi, W_IN_BLK), F32), pltpu.VMEM((N_CHIPS, ho, D_MODEL), F32),
                        pltpu.VMEM((3, hi, W_IN_BLK), F32), pltpu.VMEM((3, ho, D_MODEL), F32),
                        pltpu.VMEM((7, SMALL_ROWS, D_MODEL), F32),
                        pltpu.SemaphoreType.DMA((18,)), pltpu.SemaphoreType.DMA((18,)),
                        pltpu.SemaphoreType.DMA((2,))],
        compiler_params=_params(),
    )(g_in, g_out, small)


def _fwd_proj(x, norm_g, w_in, rope_c, rope_a, rope_b):
    seq = x.shape[0]
    nt = seq // T_PROJ

    def body(x_ref, g_ref, w_ref, c_ref, a_ref, b_ref, q_ref, kd_ref, vd_ref, rest_ref):
        xf = x_ref[...]
        r1 = lax.rsqrt(jnp.mean(xf * xf, axis=-1, keepdims=True) + EPS)
        xn = (xf * r1 * g_ref[...]).astype(BF16)
        c, a, b = c_ref[...], a_ref[...], b_ref[...]
        q = jnp.dot(xn, w_ref[:, 0:ATTN_W], preferred_element_type=F32)
        q_ref[...] = (_rope(q, c, a, b) * SCALE).astype(BF16)
        kv = jnp.dot(xn, w_ref[:, ATTN_W:ATTN_W + 2 * KV_W], preferred_element_type=F32)
        k = _rope(kv[:, 0:KV_W], c, a, b)
        v = kv[:, KV_W:2 * KV_W]
        lo = _lane_lo(k.shape)
        for t, ref in ((k, kd_ref), (v, vd_ref)):
            sw = pltpu.roll(t, HEAD_DIM, 1)
            ref[:, 0:128] = jnp.where(lo, t, sw).astype(BF16)
            ref[:, 128:256] = jnp.where(lo, sw, t).astype(BF16)
        for n in range(REST_W // 512):
            lo_c = ATTN_W + 2 * KV_W + 512 * n
            rest_ref[:, 512 * n:512 * (n + 1)] = jnp.dot(xn, w_ref[:, lo_c:lo_c + 512], preferred_element_type=F32)

    tile = lambda w: pl.BlockSpec((T_PROJ, w), lambda i: (i, 0))
    whole = lambda r, w: pl.BlockSpec((r, w), lambda i: (0, 0))
    return pl.pallas_call(
        body, name="fwd_proj", grid=(nt,),
        out_shape=(jax.ShapeDtypeStruct((seq, ATTN_W), BF16), jax.ShapeDtypeStruct((seq, 2 * KV_W), BF16),
                   jax.ShapeDtypeStruct((seq, 2 * KV_W), BF16), jax.ShapeDtypeStruct((seq, REST_W), F32)),
        in_specs=[tile(D_MODEL), whole(1, D_MODEL), whole(D_MODEL, IN_W), tile(128), tile(128), tile(128)],
        out_specs=(tile(ATTN_W), tile(2 * KV_W), tile(2 * KV_W), tile(REST_W)),
        compiler_params=_params(("arbitrary",)),
    )(x, norm_g, w_in, rope_c, rope_a, rope_b)


def _conv_parts(rest_ref, prev_ref, cw_ref, first):
    u = rest_ref[:, 1024:1536] * rest_ref[:, 1536:2048]
    up = prev_ref[:, 1024:1536] * prev_ref[:, 1536:2048]
    up = jnp.where(first, jnp.zeros_like(up), up)
    um1 = _shift_down(u, up, 1)
    um2 = _shift_down(u, up, 2)
    cv = cw_ref[0:1, :] * um2 + cw_ref[1:2, :] * um1 + cw_ref[2:3, :] * u
    return u, um1, um2, cv


def _fwd_mix(x, q, kd, vd, rest, sinks, conv_w, w_out, final_g, target):
    seq = x.shape[0]
    nt = seq // T_MIX
    nsub = T_MIX // BLOCK

    def body(sinks_ref, x_ref, q_ref, kd_ref, vd_ref, kdp_ref, vdp_ref, rest_ref, restp_ref, cw_ref, wo_ref,
             fg_ref, tgt_ref, attn_ref, mix_ref, dh2_ref, small_ref):
        i = pl.program_id(0)

        @pl.when(i == 0)
        def _():
            small_ref[...] = jnp.zeros_like(small_ref)

        for sb in range(nsub):
            rows = slice(BLOCK * sb, BLOCK * (sb + 1))
            if sb == 0:
                kk = jnp.concatenate([kdp_ref[...], kd_ref[rows, :]], axis=0)
                vv = jnp.concatenate([vdp_ref[...], vd_ref[rows, :]], axis=0)
                mask = _band_mask(i > 0)
            else:
                both = slice(BLOCK * (sb - 1), BLOCK * (sb + 1))
                kk, vv = kd_ref[both, :], vd_ref[both, :]
                mask = _band_mask(True)
            qt = q_ref[rows, :]
            for g in range(2):
                prob, _ = _probs(_stack_heads(qt, g), kk[:, 128 * g:128 * (g + 1)], _sink_col(sinks_ref, g), mask)
                o = jnp.dot(prob.astype(BF16), vv[:, 128 * g:128 * (g + 1)], preferred_element_type=F32)
                for pp in range(2):
                    lanes = slice(256 * g + 128 * pp, 256 * g + 128 * (pp + 1))
                    attn_ref[rows, lanes] = _unstack_pair(o, pp)

        ga = rest_ref[:, 0:512]
        mix_ref[:, 0:ATTN_W] = (attn_ref[...] * (ga * _sigmoid(ga))).astype(BF16)
        _, _, _, cv = _conv_parts(rest_ref, restp_ref, cw_ref, i == 0)
        gc = rest_ref[:, 2048:2560]
        mix_ref[:, ATTN_W:] = (rest_ref[:, 512:1024] * cv * (gc * _sigmoid(gc))).astype(BF16)

        h2 = x_ref[...] + jnp.dot(mix_ref[...], wo_ref[...], preferred_element_type=F32)
        r2 = lax.rsqrt(jnp.mean(h2 * h2, axis=-1, keepdims=True) + EPS)
        n2 = h2 * r2
        err = n2 * fg_ref[...] - tgt_ref[...]
        dy = err * (1.0 / D_MODEL)
        small_ref[6:7, :] += jnp.sum(err * err, axis=0, keepdims=True) * (0.5 / D_MODEL)
        small_ref[1:2, :] += jnp.sum(dy * n2, axis=0, keepdims=True)
        dn = dy * fg_ref[...]
        dh2_ref[...] = r2 * (dn - n2 * jnp.mean(dn * n2, axis=-1, keepdims=True))

    tile = lambda w: pl.BlockSpec((T_MIX, w), lambda i: (i, 0))
    whole = lambda r, w: pl.BlockSpec((r, w), lambda i: (0, 0))
    prev_blk = pl.BlockSpec((BLOCK, 2 * KV_W), lambda i: (jnp.maximum(i * nsub - 1, 0), 0))
    prev8 = pl.BlockSpec((8, REST_W), lambda i: (jnp.maximum(i * (T_MIX // 8) - 1, 0), 0))
    return pl.pallas_call(
        body, name="fwd_mix", grid=(nt,),
        out_shape=(jax.ShapeDtypeStruct((seq, ATTN_W), F32), jax.ShapeDtypeStruct((seq, D_MODEL), BF16),
                   jax.ShapeDtypeStruct((seq, D_MODEL), F32), jax.ShapeDtypeStruct((SMALL_ROWS, D_MODEL), F32)),
        in_specs=[pl.BlockSpec(memory_space=pltpu.SMEM), tile(D_MODEL), tile(ATTN_W), tile(2 * KV_W), tile(2 * KV_W),
                  prev_blk, prev_blk, tile(REST_W), prev8, whole(8, CONV_W), whole(D_MODEL, D_MODEL),
                  whole(1, D_MODEL), tile(D_MODEL)],
        out_specs=(tile(ATTN_W), tile(D_MODEL), tile(D_MODEL), whole(SMALL_ROWS, D_MODEL)),
        compiler_params=_params(("arbitrary",)),
    )(sinks, x, q, kd, vd, kd, vd, rest, rest, conv_w, w_out, final_g, target)


def _bwd_mix(dh2, q, kd, vd, attn, rest, mix, sinks, conv_w, w_out, rope_c, rope_a, rope_b):
    seq = dh2.shape[0]
    nt = seq // T_MIX
    nsub = T_MIX // BLOCK

    def body(sinks_ref, dh2_ref, q_ref, kd_ref, vd_ref, kdp_ref, vdp_ref, attn_ref, rest_ref, restp_ref, mix_ref,
             cw_ref, wo_ref, c_ref, a_ref, b_ref,
             dq_ref, dk_ref, dv_ref, dkh_ref, dvh_ref, dga_ref, db_ref, dgc_ref, dcv_ref, gwo_ref, small_ref,
             dmix_ref, dsink_ref):
        i = pl.program_id(0)

        @pl.when(i == 0)
        def _():
            small_ref[...] = jnp.zeros_like(small_ref)
            gwo_ref[...] = jnp.zeros_like(gwo_ref)
            dsink_ref[...] = jnp.zeros_like(dsink_ref)

        dh2b = dh2_ref[...].astype(BF16)
        gwo_ref[...] += lax.dot_general(mix_ref[...], dh2b, _TN, preferred_element_type=F32)
        dmix_ref[...] = lax.dot_general(dh2b, wo_ref[...], _NT, preferred_element_type=F32)

        ga = rest_ref[:, 0:512]
        sg = _sigmoid(ga)
        dma = dmix_ref[:, 0:ATTN_W]
        dga_ref[...] = (dma * attn_ref[...] * (sg * (1.0 + ga * (1.0 - sg)))).astype(BF16)
        dmix_ref[:, 0:ATTN_W] = dma * (ga * sg)

        u, um1, um2, cv = _conv_parts(rest_ref, restp_ref, cw_ref, i == 0)
        gc = rest_ref[:, 2048:2560]
        sc = _sigmoid(gc)
        bg = rest_ref[:, 512:1024]
        dmc = dmix_ref[:, ATTN_W:]
        t1 = dmc * (gc * sc)
        db_ref[...] = (t1 * cv).astype(BF16)
        dcv = t1 * bg
        dcv_ref[...] = dcv
        dgc_ref[...] = (dmc * (bg * cv) * (sc * (1.0 + gc * (1.0 - sc)))).astype(BF16)
        small_ref[2:3, 0:CONV_W] += jnp.sum(dcv * um2, axis=0, keepdims=True)
        small_ref[3:4, 0:CONV_W] += jnp.sum(dcv * um1, axis=0, keepdims=True)
        small_ref[4:5, 0:CONV_W] += jnp.sum(dcv * u, axis=0, keepdims=True)

        lo = _lane_lo((2 * BLOCK, 128))
        dk_blocks = [None] * (nsub + 1)
        dv_blocks = [None] * (nsub + 1)

        def add(lst, n, val):
            lst[n] = val if lst[n] is None else lst[n] + val

        for sb in range(nsub):
            rows = slice(BLOCK * sb, BLOCK * (sb + 1))
            if sb == 0:
                kk = jnp.concatenate([kdp_ref[...], kd_ref[rows, :]], axis=0)
                vv = jnp.concatenate([vdp_ref[...], vd_ref[rows, :]], axis=0)
                mask = _band_mask(i > 0)
            else:
                both = slice(BLOCK * (sb - 1), BLOCK * (sb + 1))
                kk, vv = kd_ref[both, :], vd_ref[both, :]
                mask = _band_mask(True)
            qt = q_ref[rows, :]
            dot = dmix_ref[rows, 0:ATTN_W].astype(BF16)
            c, a, b = c_ref[rows, :], a_ref[rows, :], b_ref[rows, :]
            dk_g, dv_g = [], []
            for g in range(2):
                qs = _stack_heads(qt, g)
                dos = _stack_heads(dot, g)
                kg, vg = kk[:, 128 * g:128 * (g + 1)], vv[:, 128 * g:128 * (g + 1)]
                prob, psink = _probs(qs, kg, _sink_col(sinks_ref, g), mask)
                dp = lax.dot_general(dos, vg, _NT, preferred_element_type=F32)
                rs = jnp.sum(prob * dp, axis=-1, keepdims=True)
                ds = (prob * (dp - rs)).astype(BF16)
                dsink_ref[g] += -psink * rs
                dqs = jnp.dot(ds, kg, preferred_element_type=F32) * SCALE
                for pp in range(2):
                    lanes = slice(256 * g + 128 * pp, 256 * g + 128 * (pp + 1))
                    dq_ref[rows, lanes] = _rope(_unstack_pair(dqs, pp), c, -a, -b).astype(BF16)
                dkd = lax.dot_general(ds, qs, _TN, preferred_element_type=F32)
                dvd = lax.dot_general(prob.astype(BF16), dos, _TN, preferred_element_type=F32)
                dk_g.append(dkd + pltpu.roll(dkd, HEAD_DIM, 1))
                dv_g.append(dvd + pltpu.roll(dvd, HEAD_DIM, 1))
            dk2 = jnp.where(lo, dk_g[0], dk_g[1])
            dv2 = jnp.where(lo, dv_g[0], dv_g[1])
            add(dk_blocks, sb, dk2[0:BLOCK])
            add(dk_blocks, sb + 1, dk2[BLOCK:])
            add(dv_blocks, sb, dv2[0:BLOCK])
            add(dv_blocks, sb + 1, dv2[BLOCK:])
        dkh_ref[0] = dk_blocks[0]
        dvh_ref[0] = dv_blocks[0]
        for sb in range(nsub):
            dk_ref[BLOCK * sb:BLOCK * (sb + 1), :] = dk_blocks[sb + 1]
            dv_ref[BLOCK * sb:BLOCK * (sb + 1), :] = dv_blocks[sb + 1]

        @pl.when(i == nt - 1)
        def _():
            for h in range(8):
                tot = jnp.sum(dsink_ref[h // 4, BLOCK * (h % 4):BLOCK * (h % 4 + 1), :], axis=0, keepdims=True)
                small_ref[5:6, h:h + 1] = tot

    tile = lambda w: pl.BlockSpec((T_MIX, w), lambda i: (i, 0))
    whole = lambda r, w: pl.BlockSpec((r, w), lambda i: (0, 0))
    prev_blk = pl.BlockSpec((BLOCK, 2 * KV_W), lambda i: (jnp.maximum(i * nsub - 1, 0), 0))
    prev8 = pl.BlockSpec((8, REST_W), lambda i: (jnp.maximum(i * (T_MIX // 8) - 1, 0), 0))
    halo = pl.BlockSpec((1, BLOCK, KV_W), lambda i: (i, 0, 0))
    bf = lambda w: jax.ShapeDtypeStruct((seq, w), BF16)
    f32 = lambda w: jax.ShapeDtypeStruct((seq, w), F32)
    return pl.pallas_call(
        body, name="bwd_mix", grid=(nt,),
        out_shape=(bf(ATTN_W), f32(KV_W), f32(KV_W), jax.ShapeDtypeStruct((nt, BLOCK, KV_W), F32),
                   jax.ShapeDtypeStruct((nt, BLOCK, KV_W), F32), bf(ATTN_W), bf(CONV_W), bf(CONV_W), f32(CONV_W),
                   jax.ShapeDtypeStruct((D_MODEL, D_MODEL), F32), jax.ShapeDtypeStruct((SMALL_ROWS, D_MODEL), F32)),
        in_specs=[pl.BlockSpec(memory_space=pltpu.SMEM), tile(D_MODEL), tile(ATTN_W), tile(2 * KV_W), tile(2 * KV_W),
                  prev_blk, prev_blk, tile(ATTN_W), tile(REST_W), prev8, tile(D_MODEL), whole(8, CONV_W),
                  whole(D_MODEL, D_MODEL), tile(128), tile(128), tile(128)],
        out_specs=(tile(ATTN_W), tile(KV_W), tile(KV_W), halo, halo, tile(ATTN_W), tile(CONV_W), tile(CONV_W),
                   tile(CONV_W), whole(D_MODEL, D_MODEL), whole(SMALL_ROWS, D_MODEL)),
        scratch_shapes=[pltpu.VMEM((T_MIX, D_MODEL), F32), pltpu.VMEM((2, 4 * BLOCK, 1), F32)],
        compiler_params=_params(("arbitrary",)),
    )(sinks, dh2, q, kd, vd, kd, vd, attn, rest, rest, mix, conv_w, w_out, rope_c, rope_a, rope_b)


def _bwd_proj(x, norm_g, dh2, dq, dk, dv, dkh, dvh, dga, db, dgc, dcv, rest, conv_w, w_in, rope_c, rope_a, rope_b):
    seq = x.shape[0]
    tb = T_MIX
    nt = seq // tb

    def body(x_ref, g_ref, dh2_ref, dq_ref, dk_ref, dv_ref, dkh_ref, dvh_ref, dga_ref, db_ref, dgc_ref, dcv_ref,
             dcvn_ref, ch_ref, cw_ref, w_ref, c_ref, a_ref, b_ref, gx_ref, gw_hbm, small_ref, dp_ref, acc_ref):
        i = pl.program_id(0)

        @pl.when(i == 0)
        def _():
            small_ref[...] = jnp.zeros_like(small_ref)
            acc_ref[...] = jnp.zeros_like(acc_ref)

        last = i == nt - 1
        keep = jnp.where(last, 0.0, 1.0)
        pad = jnp.zeros((tb - BLOCK, KV_W), F32)
        dk = dk_ref[...] + jnp.concatenate([pad, dkh_ref[0] * keep], axis=0)
        dv = dv_ref[...] + jnp.concatenate([pad, dvh_ref[0] * keep], axis=0)
        dp_ref[:, 0:ATTN_W] = dq_ref[...]
        dp_ref[:, ATTN_W:ATTN_W + KV_W] = _rope(dk, c_ref[...], -a_ref[...], -b_ref[...]).astype(BF16)
        dp_ref[:, ATTN_W + KV_W:ATTN_W + 2 * KV_W] = dv.astype(BF16)
        base = ATTN_W + 2 * KV_W
        dp_ref[:, base:base + 512] = dga_ref[...]
        dp_ref[:, base + 512:base + 1024] = db_ref[...]
        dcv = dcv_ref[...]
        nxt = dcvn_ref[...] * keep
        du = cw_ref[2:3, :] * dcv + cw_ref[1:2, :] * _shift_up(dcv, nxt, 1) + cw_ref[0:1, :] * _shift_up(dcv, nxt, 2)
        dp_ref[:, base + 1024:base + 1536] = (du * ch_ref[:, 512:1024]).astype(BF16)
        dp_ref[:, base + 1536:base + 2048] = (du * ch_ref[:, 0:512]).astype(BF16)
        dp_ref[:, base + 2048:base + 2560] = dgc_ref[...]

        xf = x_ref[...]
        r1 = lax.rsqrt(jnp.mean(xf * xf, axis=-1, keepdims=True) + EPS)
        n1 = xf * r1
        xn = (n1 * g_ref[...]).astype(BF16)
        for n in range(IN_W // 256):
            cols = slice(256 * n, 256 * (n + 1))
            acc_ref[:, cols] += lax.dot_general(xn, dp_ref[:, cols], _TN, preferred_element_type=F32)
        dxn = lax.dot_general(dp_ref[...], w_ref[...], _NT, preferred_element_type=F32)
        small_ref[0:1, :] += jnp.sum(dxn * n1, axis=0, keepdims=True)
        dxg = dxn * g_ref[...]
        gx_ref[...] = r1 * (dxg - n1 * jnp.mean(dxg * n1, axis=-1, keepdims=True)) + dh2_ref[...]

        @pl.when(last)
        def _():
            pltpu.sync_copy(acc_ref, gw_hbm)

    tile = lambda w: pl.BlockSpec((tb, w), lambda i: (i, 0))
    whole = lambda r, w: pl.BlockSpec((r, w), lambda i: (0, 0))
    halo = pl.BlockSpec((1, BLOCK, KV_W), lambda i: (jnp.minimum(i + 1, nt - 1), 0, 0))
    next8 = pl.BlockSpec((8, CONV_W), lambda i: (jnp.minimum((i + 1) * (tb // 8), seq // 8 - 1), 0))
    ch = pl.BlockSpec((tb, 1024), lambda i: (i, 1))
    return pl.pallas_call(
        body, name="bwd_proj", grid=(nt,),
        out_shape=(jax.ShapeDtypeStruct((seq, D_MODEL), F32), jax.ShapeDtypeStruct((D_MODEL, IN_W), F32),
                   jax.ShapeDtypeStruct((SMALL_ROWS, D_MODEL), F32)),
        in_specs=[tile(D_MODEL), whole(1, D_MODEL), tile(D_MODEL), tile(ATTN_W), tile(KV_W), tile(KV_W), halo, halo,
                  tile(ATTN_W), tile(CONV_W), tile(CONV_W), tile(CONV_W), next8, ch, whole(8, CONV_W),
                  pl.BlockSpec((D_MODEL, IN_W), lambda i: (0, 0), pipeline_mode=pl.Buffered(1)),
                  tile(128), tile(128), tile(128)],
        out_specs=(tile(D_MODEL), pl.BlockSpec(memory_space=pl.ANY), whole(SMALL_ROWS, D_MODEL)),
        scratch_shapes=[pltpu.VMEM((tb, IN_W), BF16), pltpu.VMEM((D_MODEL, IN_W), F32)],
        compiler_params=_params(("arbitrary",)),
    )(x, norm_g, dh2, dq, dk, dv, dkh, dvh, dga, db, dgc, dcv, dcv, rest, conv_w, w_in, rope_c, rope_a, rope_b)


def _adamw(w, g, m, v, name):
    rows, cols = w.shape
    tr = min(rows, 128)

    def body(w_ref, g_ref, m_ref, v_ref, d_ref, nm_ref, nv_ref):
        gg = g_ref[...]
        m2 = ADAM_B1 * m_ref[...] + (1.0 - ADAM_B1) * gg
        v2 = ADAM_B2 * v_ref[...] + (1.0 - ADAM_B2) * jnp.square(gg)
        m_hat = m2 / (1.0 - ADAM_B1 ** ADAM_STEP)
        v_hat = v2 / (1.0 - ADAM_B2 ** ADAM_STEP)
        d_ref[...] = -ADAM_LR * (m_hat / (jnp.sqrt(v_hat) + ADAM_EPS) + ADAM_WD * w_ref[...])
        nm_ref[...] = m2
        nv_ref[...] = v2

    spec = pl.BlockSpec((tr, cols), lambda i: (i, 0))
    shp = jax.ShapeDtypeStruct((rows, cols), F32)
    return pl.pallas_call(
        body, name=name, grid=(rows // tr,), out_shape=(shp, shp, shp),
        in_specs=[spec] * 4, out_specs=(spec,) * 3, compiler_params=_params(("arbitrary",)),
    )(w, g, m, v)


def _pack_small(norm_g, final_g, conv_w, sinks):
    out = jnp.zeros((SMALL_ROWS, D_MODEL), F32)
    out = out.at[0].set(norm_g).at[1].set(final_g)
    out = out.at[2:5, 0:128].set(conv_w).at[5, 0:8].set(sinks)
    return out


def kernel(x, norm_g, w_in, sinks, conv_w, w_out, final_g, loss_target, m_norm_g, m_w_in, m_sinks, m_conv_w, m_w_out, m_final_g, v_norm_g, v_w_in, v_sinks, v_conv_w, v_w_out, v_final_g):
    seq = x.shape[1]
    x2 = x.reshape(seq, D_MODEL)
    tgt = loss_target.reshape(seq, D_MODEL)
    ng = norm_g.reshape(1, D_MODEL)
    fg = final_g.reshape(1, D_MODEL)
    chip = 2 * lax.axis_index("x") + lax.axis_index("y")

    conv_w8 = jnp.zeros((8, 128), F32).at[0:3].set(conv_w)
    wi_all, wo_all, cw_all = _gather_weights(w_in, w_out, conv_w8)
    w_in_full = jnp.concatenate([wi_all[j] for j in range(N_CHIPS)], axis=1)
    w_out_full = wo_all.reshape(D_MODEL, D_MODEL)
    conv_full = jnp.concatenate([cw_all[j] for j in range(N_CHIPS)], axis=1)
    rope_c, rope_a, rope_b = _rope_tables(seq)

    q, kd, vd, rest = _fwd_proj(x2, ng, w_in_full, rope_c, rope_a, rope_b)
    attn, mix, dh2, small_f = _fwd_mix(x2, q, kd, vd, rest, sinks, conv_full, w_out_full, fg, tgt)
    dq, dk, dv, dkh, dvh, dga, db, dgc, dcv, g_wo, small_m = _bwd_mix(
        dh2, q, kd, vd, attn, rest, mix, sinks, conv_full, w_out_full, rope_c, rope_a, rope_b)
    grad_x, g_wi, small_p = _bwd_proj(x2, ng, dh2, dq, dk, dv, dkh, dvh, dga, db, dgc, dcv, rest, conv_full,
                                      w_in_full, rope_c, rope_a, rope_b)

    g_in_blocks = jnp.stack([g_wi[:, W_IN_BLK * j:W_IN_BLK * (j + 1)] for j in range(N_CHIPS)])
    g_out_blocks = g_wo.reshape(N_CHIPS, W_OUT_BLK, D_MODEL)
    grad_w_in, grad_w_out, small = _reduce_grads(g_in_blocks, g_out_blocks, small_f + small_m + small_p)

    loss = jnp.sum(small[6])
    grad_norm_g, grad_final_g = small[0], small[1]
    grad_conv_full = small[2:5, 0:CONV_W]
    grad_conv_w = lax.dynamic_slice(grad_conv_full, (0, 128 * chip), (3, 128))
    grad_sinks = small[5, 0:8]

    d_wi, nm_wi, nv_wi = _adamw(w_in, grad_w_in, m_w_in, v_w_in, "adamw_w_in")
    d_wo, nm_wo, nv_wo = _adamw(w_out, grad_w_out, m_w_out, v_w_out, "adamw_w_out")
    d_s, nm_s, nv_s = _adamw(_pack_small(norm_g, final_g, conv_w, sinks),
                             _pack_small(grad_norm_g, grad_final_g, grad_conv_w, grad_sinks),
                             _pack_small(m_norm_g, m_final_g, m_conv_w, m_sinks),
                             _pack_small(v_norm_g, v_final_g, v_conv_w, v_sinks), "adamw_small")

    def unpack(p):
        return p[0], p[2:5, 0:128], p[5, 0:8], p[1]

    out = [loss, grad_x.reshape(1, seq, D_MODEL), grad_norm_g, grad_w_in, grad_sinks, grad_conv_w, grad_w_out,
           grad_final_g]
    for p_small, p_wi, p_wo in ((d_s, d_wi, d_wo), (nm_s, nm_wi, nm_wo), (nv_s, nv_wi, nv_wo)):
        n_g, c_w, s_k, f_g = unpack(p_small)
        out += [n_g, p_wi, s_k, c_w, p_wo, f_g]
    return tuple(out)
```

```python
import functools

import jax
import jax.numpy as jnp
from jax import lax
from jax.experimental import pallas as pl
from jax.experimental.pallas import tpu as pltpu

F32 = jnp.float32
BF16 = jnp.bfloat16

D_MODEL = 1024
HEAD_DIM = 64
ATTN_W = 512
KV_W = 128
CONV_W = 512
IN_W = 3328
REST_W = IN_W - ATTN_W - 2 * KV_W
BLOCK = 128
ROT_DIM = 16
ROPE_THETA = 500000.0
EPS = 1e-5
SCALE = 0.125
NEG = -1e30

N_CHIPS = 4
W_IN_BLK = IN_W // N_CHIPS
W_OUT_BLK = D_MODEL // N_CHIPS

ADAM_LR = 0.001
ADAM_B1 = 0.9
ADAM_B2 = 0.999
ADAM_EPS = 1e-08
ADAM_WD = 0.01
ADAM_STEP = 10

VMEM_LIMIT = 56 * 1024 * 1024
T_PROJ = 512
T_MIX = 256
SMALL_ROWS = 8
MESH = pl.DeviceIdType.MESH

_NT = (((1,), (1,)), ((), ()))
_TN = (((0,), (0,)), ((), ()))


def _params(sem=None):
    kw = dict(vmem_limit_bytes=VMEM_LIMIT)
    if sem is not None:
        kw["dimension_semantics"] = sem
    return pltpu.CompilerParams(**kw)


def _sigmoid(t):
    return 1.0 / (1.0 + jnp.exp(-t))


def _shift_down(t, prev8, k):
    rolled = pltpu.roll(t, k, 0)
    row = lax.broadcasted_iota(jnp.int32, t.shape, 0)
    for j in range(k):
        rolled = jnp.where(row == j, prev8[8 - k + j:8 - k + j + 1, :], rolled)
    return rolled


def _shift_up(t, next8, k):
    n = t.shape[0]
    rolled = pltpu.roll(t, n - k, 0)
    row = lax.broadcasted_iota(jnp.int32, t.shape, 0)
    for j in range(k):
        rolled = jnp.where(row == n - k + j, next8[j:j + 1, :], rolled)
    return rolled


def _rope(t, c, a, b):
    w = t.shape[1]
    reps = w // 128
    if reps > 1:
        c, a, b = (jnp.concatenate([z] * reps, axis=1) for z in (c, a, b))
    return t * c + pltpu.roll(t, w - 8, 1) * a + pltpu.roll(t, 8, 1) * b


def _rope_tables(seq):
    pos = jnp.arange(seq, dtype=jnp.int32)
    j = jnp.arange(128, dtype=jnp.int32) % HEAD_DIM
    inv_freq = ROPE_THETA ** (-(2 * (j % 8)).astype(F32) / ROT_DIM)
    ang = pos.astype(F32)[:, None] * inv_freq[None, :]
    cos, sin = jnp.cos(ang), jnp.sin(ang)
    c = jnp.where(j < ROT_DIM, cos, 1.0)
    a = jnp.where(j < 8, -sin, 0.0)
    b = jnp.where((j >= 8) & (j < ROT_DIM), sin, 0.0)
    return c, a, b


def _lane_lo(shape):
    return lax.broadcasted_iota(jnp.int32, shape, 1) < HEAD_DIM


def _stack_heads(t, g):
    lo = _lane_lo((BLOCK, 128))
    parts = []
    for hh in range(4):
        pair = t[:, 256 * g + 128 * (hh // 2):256 * g + 128 * (hh // 2) + 128]
        keep = lo if hh % 2 == 0 else jnp.logical_not(lo)
        parts.append(jnp.where(keep, pair, jnp.zeros_like(pair)))
    return jnp.concatenate(parts, axis=0)


def _unstack_pair(o, pp):
    lo = _lane_lo((BLOCK, 128))
    return jnp.where(lo, o[256 * pp:256 * pp + 128], o[256 * pp + 128:256 * pp + 256])


def _band_mask(has_prev):
    r = lax.broadcasted_iota(jnp.int32, (4 * BLOCK, 2 * BLOCK), 0) % BLOCK
    kj = lax.broadcasted_iota(jnp.int32, (4 * BLOCK, 2 * BLOCK), 1)
    cur = (kj >= BLOCK) & (kj - BLOCK <= r)
    prev = (kj < BLOCK) & (kj > r)
    if has_prev is not True:
        prev = prev & has_prev
    return cur | prev


def _sink_col(sinks_ref, g):
    r = lax.broadcasted_iota(jnp.int32, (4 * BLOCK, 1), 0) // BLOCK
    col = jnp.full((4 * BLOCK, 1), sinks_ref[4 * g + 3], F32)
    for hh in range(3):
        col = jnp.where(r == hh, sinks_ref[4 * g + hh], col)
    return col


def _probs(qs, kd, sink_col, mask):
    s = lax.dot_general(qs, kd, _NT, preferred_element_type=F32)
    s = jnp.where(mask, s, NEG)
    m = jnp.maximum(jnp.max(s, axis=-1, keepdims=True), sink_col)
    p = jnp.exp(s - m)
    es = jnp.exp(sink_col - m)
    inv = 1.0 / (jnp.sum(p, axis=-1, keepdims=True) + es)
    return p * inv, es * inv


def _gather_weights(w_in_t, w_out, conv_w8):
    hi, ho = W_IN_BLK // 2, W_OUT_BLK // 2

    def body(wi_ref, wo_ref, cw_ref, wi_all, wo_all, cw_all, send_sems, recv_sems):
        x, y, c = lax.axis_index("x"), lax.axis_index("y"), lax.axis_index("c")
        me = 2 * x + y
        sibling = (x, y, 1 - c)
        chips = [(1 - x, y), (x, 1 - y), (1 - x, 1 - y)]

        wi_all[me] = wi_ref[...].astype(BF16)
        wo_all[me] = wo_ref[...].astype(BF16)
        cw_all[me] = cw_ref[...]

        def copies(k, chip, half, to):
            j = 2 * chip[0] + chip[1]
            refs = (wi_all.at[j, pl.ds(half * hi, hi)], wo_all.at[j, pl.ds(half * ho, ho)])
            return [pltpu.make_async_remote_copy(src_ref=r, dst_ref=r, send_sem=send_sems.at[2 * k + n],
                                                 recv_sem=recv_sems.at[2 * k + n], device_id=to,
                                                 device_id_type=MESH) for n, r in enumerate(refs)]

        def conv_copy(k, chip, to):
            r = cw_all.at[2 * chip[0] + chip[1]]
            return pltpu.make_async_remote_copy(src_ref=r, dst_ref=r, send_sem=send_sems.at[12 + k],
                                                recv_sem=recv_sems.at[12 + k], device_id=to, device_id_type=MESH)

        first = [cp for k, chip in enumerate(chips) for cp in copies(k, (x, y), c, (*chip, c))]
        first += [conv_copy(k, (x, y), (*chip, c)) for k, chip in enumerate(chips)]
        for cp in first:
            cp.start()
        passed = []
        for k, chip in enumerate(chips):
            for cp in copies(k, chip, c, (x, y, c)):
                cp.wait_recv()
            fwd = copies(3 + k, chip, c, sibling)
            for cp in fwd:
                cp.start()
            passed += fwd
        for k, chip in enumerate(chips):
            for cp in copies(3 + k, chip, 1 - c, (x, y, c)):
                cp.wait_recv()
            conv_copy(k, chip, (x, y, c)).wait_recv()
        for cp in first + passed:
            cp.wait_send()

    vmem = pl.BlockSpec(memory_space=pltpu.VMEM)
    return pl.pallas_call(
        body, name="gather_weights",
        out_shape=(jax.ShapeDtypeStruct((N_CHIPS, W_IN_BLK, D_MODEL), BF16),
                   jax.ShapeDtypeStruct((N_CHIPS, W_OUT_BLK, D_MODEL), BF16),
                   jax.ShapeDtypeStruct((N_CHIPS, 8, 128), F32)),
        in_specs=[vmem, vmem, vmem], out_specs=(vmem, vmem, vmem),
        scratch_shapes=[pltpu.SemaphoreType.DMA((15,)), pltpu.SemaphoreType.DMA((15,))],
        compiler_params=_params(),
    )(w_in_t, w_out, conv_w8)


def _reduce_grads(g_in, g_out, small):
    hi, ho = W_IN_BLK // 2, W_OUT_BLK // 2

    def body(gi_hbm, go_hbm, small_ref, gi_out, go_out, small_out,
             mine_i, mine_o, sib_i, sib_o, out_i, out_o, ici_i, ici_o, small_in, send_sems, recv_sems, local_sems):
        x, y, c = lax.axis_index("x"), lax.axis_index("y"), lax.axis_index("c")
        me = 2 * x + y
        my_dev = 4 * x + 2 * y + c
        sibling = (x, y, 1 - c)
        chips = [(1 - x, y), (x, 1 - y), (1 - x, 1 - y)]

        def remote(k, src, dst, to):
            return pltpu.make_async_remote_copy(src_ref=src, dst_ref=dst, send_sem=send_sems.at[k],
                                                recv_sem=recv_sems.at[k], device_id=to, device_id_type=MESH)

        small_cps = []
        for f in range(1, 8):
            fx, fy, fc = f >> 2, (f >> 1) & 1, f & 1
            to = (x ^ fx, y ^ fy, c ^ fc)
            small_cps.append(remote(10 + f, small_ref, small_in.at[f - 1], to))
        for cp in small_cps:
            cp.start()

        own = [pltpu.make_async_copy(gi_hbm.at[:, pl.ds(c * hi, hi)], mine_i, local_sems.at[0]),
               pltpu.make_async_copy(go_hbm.at[:, pl.ds(c * ho, ho)], mine_o, local_sems.at[1])]
        for cp in own:
            cp.start()
        to_sib = [remote(0, gi_hbm.at[:, pl.ds((1 - c) * hi, hi)], sib_i, sibling),
                  remote(1, go_hbm.at[:, pl.ds((1 - c) * ho, ho)], sib_o, sibling)]
        for cp in to_sib:
            cp.start()
        for cp in own:
            cp.wait()
        for cp in to_sib:
            cp.wait_recv()

        ici = []
        for k, chip in enumerate(chips):
            j = 2 * chip[0] + chip[1]
            out_i[k] = (mine_i[j] + sib_i[j]).astype(BF16)
            out_o[k] = (mine_o[j] + sib_o[j]).astype(BF16)
            ici += [remote(2 + 2 * k, out_i.at[k], ici_i.at[k], (*chip, c)),
                    remote(3 + 2 * k, out_o.at[k], ici_o.at[k], (*chip, c))]
            ici[-2].start()
            ici[-1].start()
        tot_i = mine_i[me] + sib_i[me]
        tot_o = mine_o[me] + sib_o[me]
        for k in range(3):
            ici[2 * k].wait_recv()
            ici[2 * k + 1].wait_recv()
            tot_i = tot_i + ici_i[k].astype(F32)
            tot_o = tot_o + ici_o[k].astype(F32)
        gi_out[pl.ds(c * hi, hi), :] = tot_i
        go_out[pl.ds(c * ho, ho), :] = tot_o

        swap = [remote(8, gi_out.at[pl.ds(c * hi, hi)], gi_out.at[pl.ds(c * hi, hi)], sibling),
                remote(9, go_out.at[pl.ds(c * ho, ho)], go_out.at[pl.ds(c * ho, ho)], sibling)]
        for cp in swap:
            cp.start()

        for cp in small_cps:
            cp.wait_recv()
        total = jnp.zeros((SMALL_ROWS, D_MODEL), F32)
        for d in range(8):
            slot = jnp.maximum((d ^ my_dev) - 1, 0)
            total = total + jnp.where(d == my_dev, small_ref[...], small_in[slot])
        small_out[...] = total

        recv_swap = [remote(8, gi_out.at[pl.ds((1 - c) * hi, hi)], gi_out.at[pl.ds((1 - c) * hi, hi)], sibling),
                     remote(9, go_out.at[pl.ds((1 - c) * ho, ho)], go_out.at[pl.ds((1 - c) * ho, ho)], sibling)]
        for cp in recv_swap:
            cp.wait_recv()
        for cp in to_sib + ici + swap + small_cps:
            cp.wait_send()

    vmem = pl.BlockSpec(memory_space=pltpu.VMEM)
    anyspace = pl.BlockSpec(memory_space=pl.ANY)
    return pl.pallas_call(
        body, name="reduce_grads",
        out_shape=(jax.ShapeDtypeStruct((W_IN_BLK, D_MODEL), F32),
                   jax.ShapeDtypeStruct((W_OUT_BLK, D_MODEL), F32),
                   jax.ShapeDtypeStruct((SMALL_ROWS, D_MODEL), F32)),
        in_specs=[anyspace, anyspace, vmem], out_specs=(vmem, vmem, vmem),
        scratch_shapes=[pltpu.VMEM((N_CHIPS, hi, D_MODEL), F32), pltpu.VMEM((N_CHIPS, ho, D_MODEL), F32),
                        pltpu.VMEM((N_CHIPS, hi, D_MODEL), F32), pltpu.VMEM((N_CHIPS, ho, D_MODEL), F32),
                        pltpu.VMEM((3, hi, D_MODEL), BF16), pltpu.VMEM((3, ho, D_MODEL), BF16),
                        pltpu.VMEM((3, hi, D_MODEL), BF16), pltpu.VMEM((3, ho, D_MODEL), BF16),
                        pltpu.VMEM((7, SMALL_ROWS, D_MODEL), F32),
                        pltpu.SemaphoreType.DMA((18,)), pltpu.SemaphoreType.DMA((18,)),
                        pltpu.SemaphoreType.DMA((2,))],
        compiler_params=_params(),
    )(g_in, g_out, small)


def _fwd_proj(x, norm_g, w_in_t, rope_c, rope_a, rope_b):
    seq = x.shape[0]
    nt = seq // T_PROJ

    def body(x_ref, g_ref, w_ref, c_ref, a_ref, b_ref, q_ref, kd_ref, vd_ref, rest_ref):
        xf = x_ref[...]
        r1 = lax.rsqrt(jnp.mean(xf * xf, axis=-1, keepdims=True) + EPS)
        xn = (xf * r1 * g_ref[...]).astype(BF16)
        c, a, b = c_ref[...], a_ref[...], b_ref[...]
        proj = lambda lo_c, w: lax.dot_general(xn, w_ref[lo_c:lo_c + w, :], _NT, preferred_element_type=F32)
        q_ref[...] = (_rope(proj(0, ATTN_W), c, a, b) * SCALE).astype(BF16)
        kv = proj(ATTN_W, 2 * KV_W)
        k = _rope(kv[:, 0:KV_W], c, a, b)
        v = kv[:, KV_W:2 * KV_W]
        lo = _lane_lo(k.shape)
        for t, ref in ((k, kd_ref), (v, vd_ref)):
            sw = pltpu.roll(t, HEAD_DIM, 1)
            ref[:, 0:128] = jnp.where(lo, t, sw).astype(BF16)
            ref[:, 128:256] = jnp.where(lo, sw, t).astype(BF16)
        for n in range(REST_W // 512):
            rest_ref[:, 512 * n:512 * (n + 1)] = proj(ATTN_W + 2 * KV_W + 512 * n, 512)

    tile = lambda w: pl.BlockSpec((T_PROJ, w), lambda i: (i, 0))
    whole = lambda r, w: pl.BlockSpec((r, w), lambda i: (0, 0))
    return pl.pallas_call(
        body, name="fwd_proj", grid=(nt,),
        out_shape=(jax.ShapeDtypeStruct((seq, ATTN_W), BF16), jax.ShapeDtypeStruct((seq, 2 * KV_W), BF16),
                   jax.ShapeDtypeStruct((seq, 2 * KV_W), BF16), jax.ShapeDtypeStruct((seq, REST_W), F32)),
        in_specs=[tile(D_MODEL), whole(1, D_MODEL), whole(IN_W, D_MODEL), tile(128), tile(128), tile(128)],
        out_specs=(tile(ATTN_W), tile(2 * KV_W), tile(2 * KV_W), tile(REST_W)),
        compiler_params=_params(("arbitrary",)),
    )(x, norm_g, w_in_t, rope_c, rope_a, rope_b)


def _conv_parts(rest_ref, prev_ref, cw_ref, first):
    u = rest_ref[:, 1024:1536] * rest_ref[:, 1536:2048]
    up = prev_ref[:, 1024:1536] * prev_ref[:, 1536:2048]
    up = jnp.where(first, jnp.zeros_like(up), up)
    um1 = _shift_down(u, up, 1)
    um2 = _shift_down(u, up, 2)
    cv = cw_ref[0:1, :] * um2 + cw_ref[1:2, :] * um1 + cw_ref[2:3, :] * u
    return u, um1, um2, cv


def _fwd_mix(x, q, kd, vd, rest, sinks, conv_w, w_out, final_g, target):
    seq = x.shape[0]
    nt = seq // T_MIX
    nsub = T_MIX // BLOCK

    def body(sinks_ref, x_ref, q_ref, kd_ref, vd_ref, kdp_ref, vdp_ref, rest_ref, restp_ref, cw_ref, wo_ref,
             fg_ref, tgt_ref, attn_ref, mix_ref, dh2_ref, small_ref):
        i = pl.program_id(0)

        @pl.when(i == 0)
        def _():
            small_ref[...] = jnp.zeros_like(small_ref)

        for sb in range(nsub):
            rows = slice(BLOCK * sb, BLOCK * (sb + 1))
            if sb == 0:
                kk = jnp.concatenate([kdp_ref[...], kd_ref[rows, :]], axis=0)
                vv = jnp.concatenate([vdp_ref[...], vd_ref[rows, :]], axis=0)
                mask = _band_mask(i > 0)
            else:
                both = slice(BLOCK * (sb - 1), BLOCK * (sb + 1))
                kk, vv = kd_ref[both, :], vd_ref[both, :]
                mask = _band_mask(True)
            qt = q_ref[rows, :]
            for g in range(2):
                prob, _ = _probs(_stack_heads(qt, g), kk[:, 128 * g:128 * (g + 1)], _sink_col(sinks_ref, g), mask)
                o = jnp.dot(prob.astype(BF16), vv[:, 128 * g:128 * (g + 1)], preferred_element_type=F32)
                for pp in range(2):
                    lanes = slice(256 * g + 128 * pp, 256 * g + 128 * (pp + 1))
                    attn_ref[rows, lanes] = _unstack_pair(o, pp)

        ga = rest_ref[:, 0:512]
        mix_ref[:, 0:ATTN_W] = (attn_ref[...] * (ga * _sigmoid(ga))).astype(BF16)
        _, _, _, cv = _conv_parts(rest_ref, restp_ref, cw_ref, i == 0)
        gc = rest_ref[:, 2048:2560]
        mix_ref[:, ATTN_W:] = (rest_ref[:, 512:1024] * cv * (gc * _sigmoid(gc))).astype(BF16)

        h2 = x_ref[...] + jnp.dot(mix_ref[...], wo_ref[...], preferred_element_type=F32)
        r2 = lax.rsqrt(jnp.mean(h2 * h2, axis=-1, keepdims=True) + EPS)
        n2 = h2 * r2
        err = n2 * fg_ref[...] - tgt_ref[...]
        dy = err * (1.0 / D_MODEL)
        small_ref[6:7, :] += jnp.sum(err * err, axis=0, keepdims=True) * (0.5 / D_MODEL)
        small_ref[1:2, :] += jnp.sum(dy * n2, axis=0, keepdims=True)
        dn = dy * fg_ref[...]
        dh2_ref[...] = r2 * (dn - n2 * jnp.mean(dn * n2, axis=-1, keepdims=True))

    tile = lambda w: pl.BlockSpec((T_MIX, w), lambda i: (i, 0))
    whole = lambda r, w: pl.BlockSpec((r, w), lambda i: (0, 0))
    prev_blk = pl.BlockSpec((BLOCK, 2 * KV_W), lambda i: (jnp.maximum(i * nsub - 1, 0), 0))
    prev8 = pl.BlockSpec((8, REST_W), lambda i: (jnp.maximum(i * (T_MIX // 8) - 1, 0), 0))
    return pl.pallas_call(
        body, name="fwd_mix", grid=(nt,),
        out_shape=(jax.ShapeDtypeStruct((seq, ATTN_W), F32), jax.ShapeDtypeStruct((seq, D_MODEL), BF16),
                   jax.ShapeDtypeStruct((seq, D_MODEL), F32), jax.ShapeDtypeStruct((SMALL_ROWS, D_MODEL), F32)),
        in_specs=[pl.BlockSpec(memory_space=pltpu.SMEM), tile(D_MODEL), tile(ATTN_W), tile(2 * KV_W), tile(2 * KV_W),
                  prev_blk, prev_blk, tile(REST_W), prev8, whole(8, CONV_W), whole(D_MODEL, D_MODEL),
                  whole(1, D_MODEL), tile(D_MODEL)],
        out_specs=(tile(ATTN_W), tile(D_MODEL), tile(D_MODEL), whole(SMALL_ROWS, D_MODEL)),
        compiler_params=_params(("arbitrary",)),
    )(sinks, x, q, kd, vd, kd, vd, rest, rest, conv_w, w_out, final_g, target)


def _bwd_mix(dh2, q, kd, vd, attn, rest, mix, sinks, conv_w, w_out, rope_c, rope_a, rope_b):
    seq = dh2.shape[0]
    nt = seq // T_MIX
    nsub = T_MIX // BLOCK

    def body(sinks_ref, dh2_ref, q_ref, kd_ref, vd_ref, kdp_ref, vdp_ref, attn_ref, rest_ref, restp_ref, mix_ref,
             cw_ref, wo_ref, c_ref, a_ref, b_ref,
             dq_ref, dk_ref, dv_ref, dkh_ref, dvh_ref, dga_ref, db_ref, dgc_ref, dcv_ref, gwo_ref, small_ref,
             dmix_ref, dsink_ref):
        i = pl.program_id(0)

        @pl.when(i == 0)
        def _():
            small_ref[...] = jnp.zeros_like(small_ref)
            gwo_ref[...] = jnp.zeros_like(gwo_ref)
            dsink_ref[...] = jnp.zeros_like(dsink_ref)

        dh2b = dh2_ref[...].astype(BF16)
        gwo_ref[...] += lax.dot_general(mix_ref[...], dh2b, _TN, preferred_element_type=F32)
        dmix_ref[...] = lax.dot_general(dh2b, wo_ref[...], _NT, preferred_element_type=F32)

        ga = rest_ref[:, 0:512]
        sg = _sigmoid(ga)
        dma = dmix_ref[:, 0:ATTN_W]
        dga_ref[...] = (dma * attn_ref[...] * (sg * (1.0 + ga * (1.0 - sg)))).astype(BF16)
        dmix_ref[:, 0:ATTN_W] = dma * (ga * sg)

        u, um1, um2, cv = _conv_parts(rest_ref, restp_ref, cw_ref, i == 0)
        gc = rest_ref[:, 2048:2560]
        sc = _sigmoid(gc)
        bg = rest_ref[:, 512:1024]
        dmc = dmix_ref[:, ATTN_W:]
        t1 = dmc * (gc * sc)
        db_ref[...] = (t1 * cv).astype(BF16)
        dcv = t1 * bg
        dcv_ref[...] = dcv
        dgc_ref[...] = (dmc * (bg * cv) * (sc * (1.0 + gc * (1.0 - sc)))).astype(BF16)
        small_ref[2:3, 0:CONV_W] += jnp.sum(dcv * um2, axis=0, keepdims=True)
        small_ref[3:4, 0:CONV_W] += jnp.sum(dcv * um1, axis=0, keepdims=True)
        small_ref[4:5, 0:CONV_W] += jnp.sum(dcv * u, axis=0, keepdims=True)

        lo = _lane_lo((2 * BLOCK, 128))
        dk_blocks = [None] * (nsub + 1)
        dv_blocks = [None] * (nsub + 1)

        def add(lst, n, val):
            lst[n] = val if lst[n] is None else lst[n] + val

        for sb in range(nsub):
            rows = slice(BLOCK * sb, BLOCK * (sb + 1))
            if sb == 0:
                kk = jnp.concatenate([kdp_ref[...], kd_ref[rows, :]], axis=0)
                vv = jnp.concatenate([vdp_ref[...], vd_ref[rows, :]], axis=0)
                mask = _band_mask(i > 0)
            else:
                both = slice(BLOCK * (sb - 1), BLOCK * (sb + 1))
                kk, vv = kd_ref[both, :], vd_ref[both, :]
                mask = _band_mask(True)
            qt = q_ref[rows, :]
            dot = dmix_ref[rows, 0:ATTN_W].astype(BF16)
            c, a, b = c_ref[rows, :], a_ref[rows, :], b_ref[rows, :]
            dk_g, dv_g = [], []
            for g in range(2):
                qs = _stack_heads(qt, g)
                dos = _stack_heads(dot, g)
                kg, vg = kk[:, 128 * g:128 * (g + 1)], vv[:, 128 * g:128 * (g + 1)]
                prob, psink = _probs(qs, kg, _sink_col(sinks_ref, g), mask)
                dp = lax.dot_general(dos, vg, _NT, preferred_element_type=F32)
                rs = jnp.sum(prob * dp, axis=-1, keepdims=True)
                ds = (prob * (dp - rs)).astype(BF16)
                dsink_ref[g] += -psink * rs
                dqs = jnp.dot(ds, kg, preferred_element_type=F32) * SCALE
                for pp in range(2):
                    lanes = slice(256 * g + 128 * pp, 256 * g + 128 * (pp + 1))
                    dq_ref[rows, lanes] = _rope(_unstack_pair(dqs, pp), c, -a, -b).astype(BF16)
                dkd = lax.dot_general(ds, qs, _TN, preferred_element_type=F32)
                dvd = lax.dot_general(prob.astype(BF16), dos, _TN, preferred_element_type=F32)
                dk_g.append(dkd + pltpu.roll(dkd, HEAD_DIM, 1))
                dv_g.append(dvd + pltpu.roll(dvd, HEAD_DIM, 1))
            dk2 = jnp.where(lo, dk_g[0], dk_g[1])
            dv2 = jnp.where(lo, dv_g[0], dv_g[1])
            add(dk_blocks, sb, dk2[0:BLOCK])
            add(dk_blocks, sb + 1, dk2[BLOCK:])
            add(dv_blocks, sb, dv2[0:BLOCK])
            add(dv_blocks, sb + 1, dv2[BLOCK:])
        dkh_ref[0] = dk_blocks[0]
        dvh_ref[0] = dv_blocks[0]
        for sb in range(nsub):
            dk_ref[BLOCK * sb:BLOCK * (sb + 1), :] = dk_blocks[sb + 1]
            dv_ref[BLOCK * sb:BLOCK * (sb + 1), :] = dv_blocks[sb + 1]

        @pl.when(i == nt - 1)
        def _():
            for h in range(8):
                tot = jnp.sum(dsink_ref[h // 4, BLOCK * (h % 4):BLOCK * (h % 4 + 1), :], axis=0, keepdims=True)
                small_ref[5:6, h:h + 1] = tot

    tile = lambda w: pl.BlockSpec((T_MIX, w), lambda i: (i, 0))
    whole = lambda r, w: pl.BlockSpec((r, w), lambda i: (0, 0))
    prev_blk = pl.BlockSpec((BLOCK, 2 * KV_W), lambda i: (jnp.maximum(i * nsub - 1, 0), 0))
    prev8 = pl.BlockSpec((8, REST_W), lambda i: (jnp.maximum(i * (T_MIX // 8) - 1, 0), 0))
    halo = pl.BlockSpec((1, BLOCK, KV_W), lambda i: (i, 0, 0))
    bf = lambda w: jax.ShapeDtypeStruct((seq, w), BF16)
    f32 = lambda w: jax.ShapeDtypeStruct((seq, w), F32)
    return pl.pallas_call(
        body, name="bwd_mix", grid=(nt,),
        out_shape=(bf(ATTN_W), f32(KV_W), f32(KV_W), jax.ShapeDtypeStruct((nt, BLOCK, KV_W), F32),
                   jax.ShapeDtypeStruct((nt, BLOCK, KV_W), F32), bf(ATTN_W), bf(CONV_W), bf(CONV_W), f32(CONV_W),
                   jax.ShapeDtypeStruct((D_MODEL, D_MODEL), F32), jax.ShapeDtypeStruct((SMALL_ROWS, D_MODEL), F32)),
        in_specs=[pl.BlockSpec(memory_space=pltpu.SMEM), tile(D_MODEL), tile(ATTN_W), tile(2 * KV_W), tile(2 * KV_W),
                  prev_blk, prev_blk, tile(ATTN_W), tile(REST_W), prev8, tile(D_MODEL), whole(8, CONV_W),
                  whole(D_MODEL, D_MODEL), tile(128), tile(128), tile(128)],
        out_specs=(tile(ATTN_W), tile(KV_W), tile(KV_W), halo, halo, tile(ATTN_W), tile(CONV_W), tile(CONV_W),
                   tile(CONV_W), whole(D_MODEL, D_MODEL), whole(SMALL_ROWS, D_MODEL)),
        scratch_shapes=[pltpu.VMEM((T_MIX, D_MODEL), F32), pltpu.VMEM((2, 4 * BLOCK, 1), F32)],
        compiler_params=_params(("arbitrary",)),
    )(sinks, dh2, q, kd, vd, kd, vd, attn, rest, rest, mix, conv_w, w_out, rope_c, rope_a, rope_b)


def _bwd_proj(x, norm_g, dh2, dq, dk, dv, dkh, dvh, dga, db, dgc, dcv, rest, conv_w, w_in_t, rope_c, rope_a, rope_b):
    seq = x.shape[0]
    tb = T_MIX
    nt = seq // tb

    def body(x_ref, g_ref, dh2_ref, dq_ref, dk_ref, dv_ref, dkh_ref, dvh_ref, dga_ref, db_ref, dgc_ref, dcv_ref,
             dcvn_ref, ch_ref, cw_ref, w_ref, c_ref, a_ref, b_ref, gx_ref, gw_hbm, small_ref, dp_ref, acc_ref):
        i = pl.program_id(0)

        @pl.when(i == 0)
        def _():
            small_ref[...] = jnp.zeros_like(small_ref)
            acc_ref[...] = jnp.zeros_like(acc_ref)

        last = i == nt - 1
        keep = jnp.where(last, 0.0, 1.0)
        pad = jnp.zeros((tb - BLOCK, KV_W), F32)
        dk = dk_ref[...] + jnp.concatenate([pad, dkh_ref[0] * keep], axis=0)
        dv = dv_ref[...] + jnp.concatenate([pad, dvh_ref[0] * keep], axis=0)
        dp_ref[:, 0:ATTN_W] = dq_ref[...]
        dp_ref[:, ATTN_W:ATTN_W + KV_W] = _rope(dk, c_ref[...], -a_ref[...], -b_ref[...]).astype(BF16)
        dp_ref[:, ATTN_W + KV_W:ATTN_W + 2 * KV_W] = dv.astype(BF16)
        base = ATTN_W + 2 * KV_W
        dp_ref[:, base:base + 512] = dga_ref[...]
        dp_ref[:, base + 512:base + 1024] = db_ref[...]
        dcv = dcv_ref[...]
        nxt = dcvn_ref[...] * keep
        du = cw_ref[2:3, :] * dcv + cw_ref[1:2, :] * _shift_up(dcv, nxt, 1) + cw_ref[0:1, :] * _shift_up(dcv, nxt, 2)
        dp_ref[:, base + 1024:base + 1536] = (du * ch_ref[:, 512:1024]).astype(BF16)
        dp_ref[:, base + 1536:base + 2048] = (du * ch_ref[:, 0:512]).astype(BF16)
        dp_ref[:, base + 2048:base + 2560] = dgc_ref[...]

        xf = x_ref[...]
        r1 = lax.rsqrt(jnp.mean(xf * xf, axis=-1, keepdims=True) + EPS)
        n1 = xf * r1
        xn = (n1 * g_ref[...]).astype(BF16)
        for n in range(IN_W // 256):
            cols = slice(256 * n, 256 * (n + 1))
            acc_ref[cols, :] += lax.dot_general(dp_ref[:, cols], xn, _TN, preferred_element_type=F32)
        dxn = jnp.dot(dp_ref[...], w_ref[...], preferred_element_type=F32)
        small_ref[0:1, :] += jnp.sum(dxn * n1, axis=0, keepdims=True)
        dxg = dxn * g_ref[...]
        gx_ref[...] = r1 * (dxg - n1 * jnp.mean(dxg * n1, axis=-1, keepdims=True)) + dh2_ref[...]

        @pl.when(last)
        def _():
            pltpu.sync_copy(acc_ref, gw_hbm)

    tile = lambda w: pl.BlockSpec((tb, w), lambda i: (i, 0))
    whole = lambda r, w: pl.BlockSpec((r, w), lambda i: (0, 0))
    halo = pl.BlockSpec((1, BLOCK, KV_W), lambda i: (jnp.minimum(i + 1, nt - 1), 0, 0))
    next8 = pl.BlockSpec((8, CONV_W), lambda i: (jnp.minimum((i + 1) * (tb // 8), seq // 8 - 1), 0))
    ch = pl.BlockSpec((tb, 1024), lambda i: (i, 1))
    return pl.pallas_call(
        body, name="bwd_proj", grid=(nt,),
        out_shape=(jax.ShapeDtypeStruct((seq, D_MODEL), F32), jax.ShapeDtypeStruct((IN_W, D_MODEL), F32),
                   jax.ShapeDtypeStruct((SMALL_ROWS, D_MODEL), F32)),
        in_specs=[tile(D_MODEL), whole(1, D_MODEL), tile(D_MODEL), tile(ATTN_W), tile(KV_W), tile(KV_W), halo, halo,
                  tile(ATTN_W), tile(CONV_W), tile(CONV_W), tile(CONV_W), next8, ch, whole(8, CONV_W),
                  pl.BlockSpec((IN_W, D_MODEL), lambda i: (0, 0), pipeline_mode=pl.Buffered(1)),
                  tile(128), tile(128), tile(128)],
        out_specs=(tile(D_MODEL), pl.BlockSpec(memory_space=pl.ANY), whole(SMALL_ROWS, D_MODEL)),
        scratch_shapes=[pltpu.VMEM((tb, IN_W), BF16), pltpu.VMEM((IN_W, D_MODEL), F32)],
        compiler_params=_params(("arbitrary",)),
    )(x, norm_g, dh2, dq, dk, dv, dkh, dvh, dga, db, dgc, dcv, dcv, rest, conv_w, w_in_t, rope_c, rope_a, rope_b)


def _adamw(w, g, m, v, name):
    rows, cols = w.shape
    tr = min(rows, 128)

    def body(w_ref, g_ref, m_ref, v_ref, d_ref, nm_ref, nv_ref):
        gg = g_ref[...]
        m2 = ADAM_B1 * m_ref[...] + (1.0 - ADAM_B1) * gg
        v2 = ADAM_B2 * v_ref[...] + (1.0 - ADAM_B2) * jnp.square(gg)
        m_hat = m2 / (1.0 - ADAM_B1 ** ADAM_STEP)
        v_hat = v2 / (1.0 - ADAM_B2 ** ADAM_STEP)
        d_ref[...] = -ADAM_LR * (m_hat / (jnp.sqrt(v_hat) + ADAM_EPS) + ADAM_WD * w_ref[...])
        nm_ref[...] = m2
        nv_ref[...] = v2

    spec = pl.BlockSpec((tr, cols), lambda i: (i, 0))
    shp = jax.ShapeDtypeStruct((rows, cols), F32)
    return pl.pallas_call(
        body, name=name, grid=(rows // tr,), out_shape=(shp, shp, shp),
        in_specs=[spec] * 4, out_specs=(spec,) * 3, compiler_params=_params(("arbitrary",)),
    )(w, g, m, v)


def _pack_small(norm_g, final_g, conv_w, sinks):
    out = jnp.zeros((SMALL_ROWS, D_MODEL), F32)
    out = out.at[0].set(norm_g).at[1].set(final_g)
    out = out.at[2:5, 0:128].set(conv_w).at[5, 0:8].set(sinks)
    return out


def kernel(x, norm_g, w_in, sinks, conv_w, w_out, final_g, loss_target, m_norm_g, m_w_in, m_sinks, m_conv_w, m_w_out, m_final_g, v_norm_g, v_w_in, v_sinks, v_conv_w, v_w_out, v_final_g):
    seq = x.shape[1]
    x2 = x.reshape(seq, D_MODEL)
    tgt = loss_target.reshape(seq, D_MODEL)
    ng = norm_g.reshape(1, D_MODEL)
    fg = final_g.reshape(1, D_MODEL)
    chip = 2 * lax.axis_index("x") + lax.axis_index("y")

    conv_w8 = jnp.zeros((8, 128), F32).at[0:3].set(conv_w)
    wi_all, wo_all, cw_all = _gather_weights(w_in.T, w_out, conv_w8)
    w_in_full = wi_all.reshape(IN_W, D_MODEL)
    w_out_full = wo_all.reshape(D_MODEL, D_MODEL)
    conv_full = jnp.concatenate([cw_all[j] for j in range(N_CHIPS)], axis=1)
    rope_c, rope_a, rope_b = _rope_tables(seq)

    q, kd, vd, rest = _fwd_proj(x2, ng, w_in_full, rope_c, rope_a, rope_b)
    attn, mix, dh2, small_f = _fwd_mix(x2, q, kd, vd, rest, sinks, conv_full, w_out_full, fg, tgt)
    dq, dk, dv, dkh, dvh, dga, db, dgc, dcv, g_wo, small_m = _bwd_mix(
        dh2, q, kd, vd, attn, rest, mix, sinks, conv_full, w_out_full, rope_c, rope_a, rope_b)
    grad_x, g_wi, small_p = _bwd_proj(x2, ng, dh2, dq, dk, dv, dkh, dvh, dga, db, dgc, dcv, rest, conv_full,
                                      w_in_full, rope_c, rope_a, rope_b)

    g_in_blocks = g_wi.reshape(N_CHIPS, W_IN_BLK, D_MODEL)
    g_out_blocks = g_wo.reshape(N_CHIPS, W_OUT_BLK, D_MODEL)
    grad_w_in_t, grad_w_out, small = _reduce_grads(g_in_blocks, g_out_blocks, small_f + small_m + small_p)
    grad_w_in = grad_w_in_t.T

    loss = jnp.sum(small[6])
    grad_norm_g, grad_final_g = small[0], small[1]
    grad_conv_full = small[2:5, 0:CONV_W]
    grad_conv_w = lax.dynamic_slice(grad_conv_full, (0, 128 * chip), (3, 128))
    grad_sinks = small[5, 0:8]

    d_wi, nm_wi, nv_wi = _adamw(w_in, grad_w_in, m_w_in, v_w_in, "adamw_w_in")
    d_wo, nm_wo, nv_wo = _adamw(w_out, grad_w_out, m_w_out, v_w_out, "adamw_w_out")
    d_s, nm_s, nv_s = _adamw(_pack_small(norm_g, final_g, conv_w, sinks),
                             _pack_small(grad_norm_g, grad_final_g, grad_conv_w, grad_sinks),
                             _pack_small(m_norm_g, m_final_g, m_conv_w, m_sinks),
                             _pack_small(v_norm_g, v_final_g, v_conv_w, v_sinks), "adamw_small")

    def unpack(p):
        return p[0], p[2:5, 0:128], p[5, 0:8], p[1]

    out = [loss, grad_x.reshape(1, seq, D_MODEL), grad_norm_g, grad_w_in, grad_sinks, grad_conv_w, grad_w_out,
           grad_final_g]
    for p_small, p_wi, p_wo in ((d_s, d_wi, d_wo), (nm_s, nm_wi, nm_wo), (nv_s, nv_wi, nv_wo)):
        n_g, c_w, s_k, f_g = unpack(p_small)
        out += [n_g, p_wi, s_k, c_w, p_wo, f_g]
    return tuple(out)
```

```python
import functools

import jax
import jax.numpy as jnp
from jax import lax
from jax.experimental import pallas as pl
from jax.experimental.pallas import tpu as pltpu

F32 = jnp.float32
BF16 = jnp.bfloat16

D_MODEL = 1024
HEAD_DIM = 64
ATTN_W = 512
KV_W = 128
CONV_W = 512
IN_W = 3328
REST_W = IN_W - ATTN_W - 2 * KV_W
BLOCK = 128
ROT_DIM = 16
ROPE_THETA = 500000.0
EPS = 1e-5
SCALE = 0.125
NEG = -1e30

N_CHIPS = 4
W_IN_BLK = IN_W // N_CHIPS
W_OUT_BLK = D_MODEL // N_CHIPS

ADAM_LR = 0.001
ADAM_B1 = 0.9
ADAM_B2 = 0.999
ADAM_EPS = 1e-08
ADAM_WD = 0.01
ADAM_STEP = 10

VMEM_LIMIT = 56 * 1024 * 1024
T_PROJ = 512
T_FMIX = 512
T_MIX = 256
SMALL_ROWS = 8
MESH = pl.DeviceIdType.MESH

_NT = (((1,), (1,)), ((), ()))
_TN = (((0,), (0,)), ((), ()))


def _params(sem=None):
    kw = dict(vmem_limit_bytes=VMEM_LIMIT)
    if sem is not None:
        kw["dimension_semantics"] = sem
    return pltpu.CompilerParams(**kw)


def _sigmoid(t):
    return 1.0 / (1.0 + jnp.exp(-t))


def _shift_down(t, prev8, k):
    rolled = pltpu.roll(t, k, 0)
    row = lax.broadcasted_iota(jnp.int32, t.shape, 0)
    for j in range(k):
        rolled = jnp.where(row == j, prev8[8 - k + j:8 - k + j + 1, :], rolled)
    return rolled


def _shift_up(t, next8, k):
    n = t.shape[0]
    rolled = pltpu.roll(t, n - k, 0)
    row = lax.broadcasted_iota(jnp.int32, t.shape, 0)
    for j in range(k):
        rolled = jnp.where(row == n - k + j, next8[j:j + 1, :], rolled)
    return rolled


def _rope(t, c, a, b):
    w = t.shape[1]
    reps = w // 128
    if reps > 1:
        c, a, b = (jnp.concatenate([z] * reps, axis=1) for z in (c, a, b))
    return t * c + pltpu.roll(t, w - 8, 1) * a + pltpu.roll(t, 8, 1) * b


def _rope_tables(seq):
    pos = jnp.arange(seq, dtype=jnp.int32)
    j = jnp.arange(128, dtype=jnp.int32) % HEAD_DIM
    inv_freq = ROPE_THETA ** (-(2 * (j % 8)).astype(F32) / ROT_DIM)
    ang = pos.astype(F32)[:, None] * inv_freq[None, :]
    cos, sin = jnp.cos(ang), jnp.sin(ang)
    c = jnp.where(j < ROT_DIM, cos, 1.0)
    a = jnp.where(j < 8, -sin, 0.0)
    b = jnp.where((j >= 8) & (j < ROT_DIM), sin, 0.0)
    return c, a, b


def _lane_lo(shape):
    return lax.broadcasted_iota(jnp.int32, shape, 1) < HEAD_DIM


def _stack_heads(t, g):
    lo = _lane_lo((BLOCK, 128))
    parts = []
    for hh in range(4):
        pair = t[:, 256 * g + 128 * (hh // 2):256 * g + 128 * (hh // 2) + 128]
        keep = lo if hh % 2 == 0 else jnp.logical_not(lo)
        parts.append(jnp.where(keep, pair, jnp.zeros_like(pair)))
    return jnp.concatenate(parts, axis=0)


def _unstack_pair(o, pp):
    lo = _lane_lo((BLOCK, 128))
    return jnp.where(lo, o[256 * pp:256 * pp + 128], o[256 * pp + 128:256 * pp + 256])


def _band_mask(has_prev):
    r = lax.broadcasted_iota(jnp.int32, (4 * BLOCK, 2 * BLOCK), 0) % BLOCK
    kj = lax.broadcasted_iota(jnp.int32, (4 * BLOCK, 2 * BLOCK), 1)
    cur = (kj >= BLOCK) & (kj - BLOCK <= r)
    prev = (kj < BLOCK) & (kj > r)
    if has_prev is not True:
        prev = prev & has_prev
    return cur | prev


def _sink_col(sinks_ref, g):
    r = lax.broadcasted_iota(jnp.int32, (4 * BLOCK, 1), 0) // BLOCK
    col = jnp.full((4 * BLOCK, 1), sinks_ref[4 * g + 3], F32)
    for hh in range(3):
        col = jnp.where(r == hh, sinks_ref[4 * g + hh], col)
    return col


def _probs(qs, kd, sink_col, mask):
    s = lax.dot_general(qs, kd, _NT, preferred_element_type=F32)
    s = jnp.where(mask, s, NEG)
    m = jnp.maximum(jnp.max(s, axis=-1, keepdims=True), sink_col)
    p = jnp.exp(s - m)
    es = jnp.exp(sink_col - m)
    inv = 1.0 / (jnp.sum(p, axis=-1, keepdims=True) + es)
    return p * inv, es * inv


def _gather_weights(w_in_t, w_out, conv_w8):
    hi, ho = W_IN_BLK // 2, W_OUT_BLK // 2

    def body(wi_ref, wo_ref, cw_ref, wi_all, wo_all, cw_all, send_sems, recv_sems):
        x, y, c = lax.axis_index("x"), lax.axis_index("y"), lax.axis_index("c")
        me = 2 * x + y
        sibling = (x, y, 1 - c)
        chips = [(1 - x, y), (x, 1 - y), (1 - x, 1 - y)]

        wi_all[me] = wi_ref[...].astype(BF16)
        wo_all[me] = wo_ref[...].astype(BF16)
        cw_all[me] = cw_ref[...]

        def copies(k, chip, half, to):
            j = 2 * chip[0] + chip[1]
            refs = (wi_all.at[j, pl.ds(half * hi, hi)], wo_all.at[j, pl.ds(half * ho, ho)])
            return [pltpu.make_async_remote_copy(src_ref=r, dst_ref=r, send_sem=send_sems.at[2 * k + n],
                                                 recv_sem=recv_sems.at[2 * k + n], device_id=to,
                                                 device_id_type=MESH) for n, r in enumerate(refs)]

        def conv_copy(k, chip, to):
            r = cw_all.at[2 * chip[0] + chip[1]]
            return pltpu.make_async_remote_copy(src_ref=r, dst_ref=r, send_sem=send_sems.at[12 + k],
                                                recv_sem=recv_sems.at[12 + k], device_id=to, device_id_type=MESH)

        first = [cp for k, chip in enumerate(chips) for cp in copies(k, (x, y), c, (*chip, c))]
        first += [conv_copy(k, (x, y), (*chip, c)) for k, chip in enumerate(chips)]
        for cp in first:
            cp.start()
        passed = []
        for k, chip in enumerate(chips):
            for cp in copies(k, chip, c, (x, y, c)):
                cp.wait_recv()
            fwd = copies(3 + k, chip, c, sibling)
            for cp in fwd:
                cp.start()
            passed += fwd
        for k, chip in enumerate(chips):
            for cp in copies(3 + k, chip, 1 - c, (x, y, c)):
                cp.wait_recv()
            conv_copy(k, chip, (x, y, c)).wait_recv()
        for cp in first + passed:
            cp.wait_send()

    vmem = pl.BlockSpec(memory_space=pltpu.VMEM)
    return pl.pallas_call(
        body, name="gather_weights",
        out_shape=(jax.ShapeDtypeStruct((N_CHIPS, W_IN_BLK, D_MODEL), BF16),
                   jax.ShapeDtypeStruct((N_CHIPS, W_OUT_BLK, D_MODEL), BF16),
                   jax.ShapeDtypeStruct((N_CHIPS, 8, 128), F32)),
        in_specs=[vmem, vmem, vmem], out_specs=(vmem, vmem, vmem),
        scratch_shapes=[pltpu.SemaphoreType.DMA((15,)), pltpu.SemaphoreType.DMA((15,))],
        compiler_params=_params(),
    )(w_in_t, w_out, conv_w8)


def _reduce_grads(g_in, g_out, small):
    hi, ho = W_IN_BLK // 2, W_OUT_BLK // 2

    def body(gi_hbm, go_hbm, small_ref, gi_out, go_out, small_out,
             mine_i, mine_o, sib_i, sib_o, out_i, out_o, ici_i, ici_o, small_in, send_sems, recv_sems, local_sems):
        x, y, c = lax.axis_index("x"), lax.axis_index("y"), lax.axis_index("c")
        me = 2 * x + y
        my_dev = 4 * x + 2 * y + c
        sibling = (x, y, 1 - c)
        chips = [(1 - x, y), (x, 1 - y), (1 - x, 1 - y)]

        def remote(k, src, dst, to):
            return pltpu.make_async_remote_copy(src_ref=src, dst_ref=dst, send_sem=send_sems.at[k],
                                                recv_sem=recv_sems.at[k], device_id=to, device_id_type=MESH)

        small_cps = []
        for f in range(1, 8):
            fx, fy, fc = f >> 2, (f >> 1) & 1, f & 1
            to = (x ^ fx, y ^ fy, c ^ fc)
            small_cps.append(remote(10 + f, small_ref, small_in.at[f - 1], to))
        for cp in small_cps:
            cp.start()

        own = [pltpu.make_async_copy(gi_hbm.at[:, pl.ds(c * hi, hi)], mine_i, local_sems.at[0]),
               pltpu.make_async_copy(go_hbm.at[:, pl.ds(c * ho, ho)], mine_o, local_sems.at[1])]
        for cp in own:
            cp.start()
        to_sib = [remote(0, gi_hbm.at[:, pl.ds((1 - c) * hi, hi)], sib_i, sibling),
                  remote(1, go_hbm.at[:, pl.ds((1 - c) * ho, ho)], sib_o, sibling)]
        for cp in to_sib:
            cp.start()
        for cp in own:
            cp.wait()
        for cp in to_sib:
            cp.wait_recv()

        ici = []
        for k, chip in enumerate(chips):
            j = 2 * chip[0] + chip[1]
            out_i[k] = (mine_i[j] + sib_i[j]).astype(BF16)
            out_o[k] = (mine_o[j] + sib_o[j]).astype(BF16)
            ici += [remote(2 + 2 * k, out_i.at[k], ici_i.at[k], (*chip, c)),
                    remote(3 + 2 * k, out_o.at[k], ici_o.at[k], (*chip, c))]
            ici[-2].start()
            ici[-1].start()
        tot_i = mine_i[me] + sib_i[me]
        tot_o = mine_o[me] + sib_o[me]
        for k in range(3):
            ici[2 * k].wait_recv()
            ici[2 * k + 1].wait_recv()
            tot_i = tot_i + ici_i[k].astype(F32)
            tot_o = tot_o + ici_o[k].astype(F32)
        gi_out[pl.ds(c * hi, hi), :] = tot_i
        go_out[pl.ds(c * ho, ho), :] = tot_o

        swap = [remote(8, gi_out.at[pl.ds(c * hi, hi)], gi_out.at[pl.ds(c * hi, hi)], sibling),
                remote(9, go_out.at[pl.ds(c * ho, ho)], go_out.at[pl.ds(c * ho, ho)], sibling)]
        for cp in swap:
            cp.start()

        for cp in small_cps:
            cp.wait_recv()
        total = jnp.zeros((SMALL_ROWS, D_MODEL), F32)
        for d in range(8):
            slot = jnp.maximum((d ^ my_dev) - 1, 0)
            total = total + jnp.where(d == my_dev, small_ref[...], small_in[slot])
        small_out[...] = total

        recv_swap = [remote(8, gi_out.at[pl.ds((1 - c) * hi, hi)], gi_out.at[pl.ds((1 - c) * hi, hi)], sibling),
                     remote(9, go_out.at[pl.ds((1 - c) * ho, ho)], go_out.at[pl.ds((1 - c) * ho, ho)], sibling)]
        for cp in recv_swap:
            cp.wait_recv()
        for cp in to_sib + ici + swap + small_cps:
            cp.wait_send()

    vmem = pl.BlockSpec(memory_space=pltpu.VMEM)
    anyspace = pl.BlockSpec(memory_space=pl.ANY)
    return pl.pallas_call(
        body, name="reduce_grads",
        out_shape=(jax.ShapeDtypeStruct((W_IN_BLK, D_MODEL), F32),
                   jax.ShapeDtypeStruct((W_OUT_BLK, D_MODEL), F32),
                   jax.ShapeDtypeStruct((SMALL_ROWS, D_MODEL), F32)),
        in_specs=[anyspace, anyspace, vmem], out_specs=(vmem, vmem, vmem),
        scratch_shapes=[pltpu.VMEM((N_CHIPS, hi, D_MODEL), F32), pltpu.VMEM((N_CHIPS, ho, D_MODEL), F32),
                        pltpu.VMEM((N_CHIPS, hi, D_MODEL), F32), pltpu.VMEM((N_CHIPS, ho, D_MODEL), F32),
                        pltpu.VMEM((3, hi, D_MODEL), BF16), pltpu.VMEM((3, ho, D_MODEL), BF16),
                        pltpu.VMEM((3, hi, D_MODEL), BF16), pltpu.VMEM((3, ho, D_MODEL), BF16),
                        pltpu.VMEM((7, SMALL_ROWS, D_MODEL), F32),
                        pltpu.SemaphoreType.DMA((18,)), pltpu.SemaphoreType.DMA((18,)),
                        pltpu.SemaphoreType.DMA((2,))],
        compiler_params=_params(),
    )(g_in, g_out, small)


def _fwd_proj(x, norm_g, w_in_t, rope_c, rope_a, rope_b):
    seq = x.shape[0]
    nt = seq // T_PROJ

    def body(x_ref, g_ref, w_ref, c_ref, a_ref, b_ref, q_ref, kd_ref, vd_ref, rest_ref):
        xf = x_ref[...]
        r1 = lax.rsqrt(jnp.mean(xf * xf, axis=-1, keepdims=True) + EPS)
        xn = (xf * r1 * g_ref[...]).astype(BF16)
        c, a, b = c_ref[...], a_ref[...], b_ref[...]
        proj = lambda lo_c, w: lax.dot_general(xn, w_ref[lo_c:lo_c + w, :], _NT, preferred_element_type=F32)
        q_ref[...] = (_rope(proj(0, ATTN_W), c, a, b) * SCALE).astype(BF16)
        kv = proj(ATTN_W, 2 * KV_W)
        k = _rope(kv[:, 0:KV_W], c, a, b)
        v = kv[:, KV_W:2 * KV_W]
        lo = _lane_lo(k.shape)
        for t, ref in ((k, kd_ref), (v, vd_ref)):
            sw = pltpu.roll(t, HEAD_DIM, 1)
            ref[:, 0:128] = jnp.where(lo, t, sw).astype(BF16)
            ref[:, 128:256] = jnp.where(lo, sw, t).astype(BF16)
        for n in range(REST_W // 512):
            rest_ref[:, 512 * n:512 * (n + 1)] = proj(ATTN_W + 2 * KV_W + 512 * n, 512)

    tile = lambda w: pl.BlockSpec((T_PROJ, w), lambda i: (i, 0))
    whole = lambda r, w: pl.BlockSpec((r, w), lambda i: (0, 0))
    return pl.pallas_call(
        body, name="fwd_proj", grid=(nt,),
        out_shape=(jax.ShapeDtypeStruct((seq, ATTN_W), BF16), jax.ShapeDtypeStruct((seq, 2 * KV_W), BF16),
                   jax.ShapeDtypeStruct((seq, 2 * KV_W), BF16), jax.ShapeDtypeStruct((seq, REST_W), F32)),
        in_specs=[tile(D_MODEL), whole(1, D_MODEL), whole(IN_W, D_MODEL), tile(128), tile(128), tile(128)],
        out_specs=(tile(ATTN_W), tile(2 * KV_W), tile(2 * KV_W), tile(REST_W)),
        compiler_params=_params(("arbitrary",)),
    )(x, norm_g, w_in_t, rope_c, rope_a, rope_b)


def _conv_parts(rest_ref, prev_ref, cw_ref, first):
    u = rest_ref[:, 1024:1536] * rest_ref[:, 1536:2048]
    up = prev_ref[:, 1024:1536] * prev_ref[:, 1536:2048]
    up = jnp.where(first, jnp.zeros_like(up), up)
    um1 = _shift_down(u, up, 1)
    um2 = _shift_down(u, up, 2)
    cv = cw_ref[0:1, :] * um2 + cw_ref[1:2, :] * um1 + cw_ref[2:3, :] * u
    return u, um1, um2, cv


def _fwd_mix(x, q, kd, vd, rest, sinks, conv_w, w_out, final_g, target):
    seq = x.shape[0]
    nt = seq // T_FMIX
    nsub = T_FMIX // BLOCK

    def body(sinks_ref, x_ref, q_ref, kd_ref, vd_ref, kdp_ref, vdp_ref, rest_ref, restp_ref, cw_ref, wo_ref,
             fg_ref, tgt_ref, attn_ref, mix_ref, dh2_ref, small_ref):
        i = pl.program_id(0)

        @pl.when(i == 0)
        def _():
            small_ref[...] = jnp.zeros_like(small_ref)

        for sb in range(nsub):
            rows = slice(BLOCK * sb, BLOCK * (sb + 1))
            if sb == 0:
                kk = jnp.concatenate([kdp_ref[...], kd_ref[rows, :]], axis=0)
                vv = jnp.concatenate([vdp_ref[...], vd_ref[rows, :]], axis=0)
                mask = _band_mask(i > 0)
            else:
                both = slice(BLOCK * (sb - 1), BLOCK * (sb + 1))
                kk, vv = kd_ref[both, :], vd_ref[both, :]
                mask = _band_mask(True)
            qt = q_ref[rows, :]
            for g in range(2):
                prob, _ = _probs(_stack_heads(qt, g), kk[:, 128 * g:128 * (g + 1)], _sink_col(sinks_ref, g), mask)
                o = jnp.dot(prob.astype(BF16), vv[:, 128 * g:128 * (g + 1)], preferred_element_type=F32)
                for pp in range(2):
                    lanes = slice(256 * g + 128 * pp, 256 * g + 128 * (pp + 1))
                    attn_ref[rows, lanes] = _unstack_pair(o, pp)

        ga = rest_ref[:, 0:512]
        mix_ref[:, 0:ATTN_W] = (attn_ref[...] * (ga * _sigmoid(ga))).astype(BF16)
        _, _, _, cv = _conv_parts(rest_ref, restp_ref, cw_ref, i == 0)
        gc = rest_ref[:, 2048:2560]
        mix_ref[:, ATTN_W:] = (rest_ref[:, 512:1024] * cv * (gc * _sigmoid(gc))).astype(BF16)

        h2 = x_ref[...] + jnp.dot(mix_ref[...], wo_ref[...], preferred_element_type=F32)
        r2 = lax.rsqrt(jnp.mean(h2 * h2, axis=-1, keepdims=True) + EPS)
        n2 = h2 * r2
        err = n2 * fg_ref[...] - tgt_ref[...]
        dy = err * (1.0 / D_MODEL)
        small_ref[6:7, :] += jnp.sum(err * err, axis=0, keepdims=True) * (0.5 / D_MODEL)
        small_ref[1:2, :] += jnp.sum(dy * n2, axis=0, keepdims=True)
        dn = dy * fg_ref[...]
        dh2_ref[...] = r2 * (dn - n2 * jnp.mean(dn * n2, axis=-1, keepdims=True))

    tile = lambda w: pl.BlockSpec((T_FMIX, w), lambda i: (i, 0))
    whole = lambda r, w: pl.BlockSpec((r, w), lambda i: (0, 0))
    prev_blk = pl.BlockSpec((BLOCK, 2 * KV_W), lambda i: (jnp.maximum(i * nsub - 1, 0), 0))
    prev8 = pl.BlockSpec((8, REST_W), lambda i: (jnp.maximum(i * (T_FMIX // 8) - 1, 0), 0))
    return pl.pallas_call(
        body, name="fwd_mix", grid=(nt,),
        out_shape=(jax.ShapeDtypeStruct((seq, ATTN_W), F32), jax.ShapeDtypeStruct((seq, D_MODEL), BF16),
                   jax.ShapeDtypeStruct((seq, D_MODEL), F32), jax.ShapeDtypeStruct((SMALL_ROWS, D_MODEL), F32)),
        in_specs=[pl.BlockSpec(memory_space=pltpu.SMEM), tile(D_MODEL), tile(ATTN_W), tile(2 * KV_W), tile(2 * KV_W),
                  prev_blk, prev_blk, tile(REST_W), prev8, whole(8, CONV_W), whole(D_MODEL, D_MODEL),
                  whole(1, D_MODEL), tile(D_MODEL)],
        out_specs=(tile(ATTN_W), tile(D_MODEL), tile(D_MODEL), whole(SMALL_ROWS, D_MODEL)),
        compiler_params=_params(("arbitrary",)),
    )(sinks, x, q, kd, vd, kd, vd, rest, rest, conv_w, w_out, final_g, target)


def _bwd_mix(dh2, q, kd, vd, attn, rest, mix, sinks, conv_w, w_out, rope_c, rope_a, rope_b):
    seq = dh2.shape[0]
    nt = seq // T_MIX
    nsub = T_MIX // BLOCK

    def body(sinks_ref, dh2_ref, q_ref, kd_ref, vd_ref, kdp_ref, vdp_ref, attn_ref, rest_ref, restp_ref, mix_ref,
             cw_ref, wo_ref, c_ref, a_ref, b_ref,
             dq_ref, dk_ref, dv_ref, dkh_ref, dvh_ref, dga_ref, db_ref, dgc_ref, dcv_ref, gwo_ref, small_ref,
             dmix_ref, dsink_ref):
        i = pl.program_id(0)

        @pl.when(i == 0)
        def _():
            small_ref[...] = jnp.zeros_like(small_ref)
            gwo_ref[...] = jnp.zeros_like(gwo_ref)
            dsink_ref[...] = jnp.zeros_like(dsink_ref)

        dh2b = dh2_ref[...].astype(BF16)
        gwo_ref[...] += lax.dot_general(mix_ref[...], dh2b, _TN, preferred_element_type=F32)
        dmix_ref[...] = lax.dot_general(dh2b, wo_ref[...], _NT, preferred_element_type=F32)

        ga = rest_ref[:, 0:512]
        sg = _sigmoid(ga)
        dma = dmix_ref[:, 0:ATTN_W]
        dga_ref[...] = (dma * attn_ref[...] * (sg * (1.0 + ga * (1.0 - sg)))).astype(BF16)
        dmix_ref[:, 0:ATTN_W] = dma * (ga * sg)

        u, um1, um2, cv = _conv_parts(rest_ref, restp_ref, cw_ref, i == 0)
        gc = rest_ref[:, 2048:2560]
        sc = _sigmoid(gc)
        bg = rest_ref[:, 512:1024]
        dmc = dmix_ref[:, ATTN_W:]
        t1 = dmc * (gc * sc)
        db_ref[...] = (t1 * cv).astype(BF16)
        dcv = t1 * bg
        dcv_ref[...] = dcv
        dgc_ref[...] = (dmc * (bg * cv) * (sc * (1.0 + gc * (1.0 - sc)))).astype(BF16)
        small_ref[2:3, 0:CONV_W] += jnp.sum(dcv * um2, axis=0, keepdims=True)
        small_ref[3:4, 0:CONV_W] += jnp.sum(dcv * um1, axis=0, keepdims=True)
        small_ref[4:5, 0:CONV_W] += jnp.sum(dcv * u, axis=0, keepdims=True)

        lo = _lane_lo((2 * BLOCK, 128))
        dk_blocks = [None] * (nsub + 1)
        dv_blocks = [None] * (nsub + 1)

        def add(lst, n, val):
            lst[n] = val if lst[n] is None else lst[n] + val

        for sb in range(nsub):
            rows = slice(BLOCK * sb, BLOCK * (sb + 1))
            if sb == 0:
                kk = jnp.concatenate([kdp_ref[...], kd_ref[rows, :]], axis=0)
                vv = jnp.concatenate([vdp_ref[...], vd_ref[rows, :]], axis=0)
                mask = _band_mask(i > 0)
            else:
                both = slice(BLOCK * (sb - 1), BLOCK * (sb + 1))
                kk, vv = kd_ref[both, :], vd_ref[both, :]
                mask = _band_mask(True)
            qt = q_ref[rows, :]
            dot = dmix_ref[rows, 0:ATTN_W].astype(BF16)
            c, a, b = c_ref[rows, :], a_ref[rows, :], b_ref[rows, :]
            dk_g, dv_g = [], []
            for g in range(2):
                qs = _stack_heads(qt, g)
                dos = _stack_heads(dot, g)
                kg, vg = kk[:, 128 * g:128 * (g + 1)], vv[:, 128 * g:128 * (g + 1)]
                prob, psink = _probs(qs, kg, _sink_col(sinks_ref, g), mask)
                dp = lax.dot_general(dos, vg, _NT, preferred_element_type=F32)
                rs = jnp.sum(prob * dp, axis=-1, keepdims=True)
                ds = (prob * (dp - rs)).astype(BF16)
                dsink_ref[g] += -psink * rs
                dqs = jnp.dot(ds, kg, preferred_element_type=F32) * SCALE
                for pp in range(2):
                    lanes = slice(256 * g + 128 * pp, 256 * g + 128 * (pp + 1))
                    dq_ref[rows, lanes] = _rope(_unstack_pair(dqs, pp), c, -a, -b).astype(BF16)
                dkd = lax.dot_general(ds, qs, _TN, preferred_element_type=F32)
                dvd = lax.dot_general(prob.astype(BF16), dos, _TN, preferred_element_type=F32)
                dk_g.append(dkd + pltpu.roll(dkd, HEAD_DIM, 1))
                dv_g.append(dvd + pltpu.roll(dvd, HEAD_DIM, 1))
            dk2 = jnp.where(lo, dk_g[0], dk_g[1])
            dv2 = jnp.where(lo, dv_g[0], dv_g[1])
            add(dk_blocks, sb, dk2[0:BLOCK])
            add(dk_blocks, sb + 1, dk2[BLOCK:])
            add(dv_blocks, sb, dv2[0:BLOCK])
            add(dv_blocks, sb + 1, dv2[BLOCK:])
        dkh_ref[0] = dk_blocks[0]
        dvh_ref[0] = dv_blocks[0]
        for sb in range(nsub):
            dk_ref[BLOCK * sb:BLOCK * (sb + 1), :] = dk_blocks[sb + 1]
            dv_ref[BLOCK * sb:BLOCK * (sb + 1), :] = dv_blocks[sb + 1]

        @pl.when(i == nt - 1)
        def _():
            for h in range(8):
                tot = jnp.sum(dsink_ref[h // 4, BLOCK * (h % 4):BLOCK * (h % 4 + 1), :], axis=0, keepdims=True)
                small_ref[5:6, h:h + 1] = tot

    tile = lambda w: pl.BlockSpec((T_MIX, w), lambda i: (i, 0))
    whole = lambda r, w: pl.BlockSpec((r, w), lambda i: (0, 0))
    prev_blk = pl.BlockSpec((BLOCK, 2 * KV_W), lambda i: (jnp.maximum(i * nsub - 1, 0), 0))
    prev8 = pl.BlockSpec((8, REST_W), lambda i: (jnp.maximum(i * (T_MIX // 8) - 1, 0), 0))
    halo = pl.BlockSpec((1, BLOCK, KV_W), lambda i: (i, 0, 0))
    bf = lambda w: jax.ShapeDtypeStruct((seq, w), BF16)
    f32 = lambda w: jax.ShapeDtypeStruct((seq, w), F32)
    return pl.pallas_call(
        body, name="bwd_mix", grid=(nt,),
        out_shape=(bf(ATTN_W), f32(KV_W), f32(KV_W), jax.ShapeDtypeStruct((nt, BLOCK, KV_W), F32),
                   jax.ShapeDtypeStruct((nt, BLOCK, KV_W), F32), bf(ATTN_W), bf(CONV_W), bf(CONV_W), f32(CONV_W),
                   jax.ShapeDtypeStruct((D_MODEL, D_MODEL), F32), jax.ShapeDtypeStruct((SMALL_ROWS, D_MODEL), F32)),
        in_specs=[pl.BlockSpec(memory_space=pltpu.SMEM), tile(D_MODEL), tile(ATTN_W), tile(2 * KV_W), tile(2 * KV_W),
                  prev_blk, prev_blk, tile(ATTN_W), tile(REST_W), prev8, tile(D_MODEL), whole(8, CONV_W),
                  whole(D_MODEL, D_MODEL), tile(128), tile(128), tile(128)],
        out_specs=(tile(ATTN_W), tile(KV_W), tile(KV_W), halo, halo, tile(ATTN_W), tile(CONV_W), tile(CONV_W),
                   tile(CONV_W), whole(D_MODEL, D_MODEL), whole(SMALL_ROWS, D_MODEL)),
        scratch_shapes=[pltpu.VMEM((T_MIX, D_MODEL), F32), pltpu.VMEM((2, 4 * BLOCK, 1), F32)],
        compiler_params=_params(("arbitrary",)),
    )(sinks, dh2, q, kd, vd, kd, vd, attn, rest, rest, mix, conv_w, w_out, rope_c, rope_a, rope_b)


def _bwd_proj(x, norm_g, dh2, dq, dk, dv, dkh, dvh, dga, db, dgc, dcv, rest, conv_w, w_in_t, rope_c, rope_a, rope_b):
    seq = x.shape[0]
    tb = T_PROJ
    per = tb // T_MIX
    nt = seq // tb

    def body(x_ref, g_ref, dh2_ref, dq_ref, dk_ref, dv_ref, dkh_ref, dvh_ref, dkn_ref, dvn_ref, dga_ref, db_ref,
             dgc_ref, dcv_ref, dcvn_ref, ch_ref, cw_ref, w_ref, c_ref, a_ref, b_ref, gx_ref, gw_hbm, small_ref,
             dp_ref, acc_ref):
        i = pl.program_id(0)

        @pl.when(i == 0)
        def _():
            small_ref[...] = jnp.zeros_like(small_ref)
            acc_ref[...] = jnp.zeros_like(acc_ref)

        last = i == nt - 1
        keep = jnp.where(last, 0.0, 1.0)
        pad = jnp.zeros((T_MIX - BLOCK, KV_W), F32)

        def with_halos(main_ref, halo_ref, next_ref):
            parts = []
            for m in range(1, per + 1):
                parts += [pad, halo_ref[m] if m < per else next_ref[0] * keep]
            return main_ref[...] + jnp.concatenate(parts, axis=0)

        dk = with_halos(dk_ref, dkh_ref, dkn_ref)
        dv = with_halos(dv_ref, dvh_ref, dvn_ref)
        dp_ref[:, 0:ATTN_W] = dq_ref[...]
        dp_ref[:, ATTN_W:ATTN_W + KV_W] = _rope(dk, c_ref[...], -a_ref[...], -b_ref[...]).astype(BF16)
        dp_ref[:, ATTN_W + KV_W:ATTN_W + 2 * KV_W] = dv.astype(BF16)
        base = ATTN_W + 2 * KV_W
        dp_ref[:, base:base + 512] = dga_ref[...]
        dp_ref[:, base + 512:base + 1024] = db_ref[...]
        dcv = dcv_ref[...]
        nxt = dcvn_ref[...] * keep
        du = cw_ref[2:3, :] * dcv + cw_ref[1:2, :] * _shift_up(dcv, nxt, 1) + cw_ref[0:1, :] * _shift_up(dcv, nxt, 2)
        dp_ref[:, base + 1024:base + 1536] = (du * ch_ref[:, 512:1024]).astype(BF16)
        dp_ref[:, base + 1536:base + 2048] = (du * ch_ref[:, 0:512]).astype(BF16)
        dp_ref[:, base + 2048:base + 2560] = dgc_ref[...]

        xf = x_ref[...]
        r1 = lax.rsqrt(jnp.mean(xf * xf, axis=-1, keepdims=True) + EPS)
        n1 = xf * r1
        xn = (n1 * g_ref[...]).astype(BF16)
        for n in range(IN_W // 256):
            cols = slice(256 * n, 256 * (n + 1))
            acc_ref[cols, :] += lax.dot_general(dp_ref[:, cols], xn, _TN, preferred_element_type=F32)
        dxn = jnp.dot(dp_ref[...], w_ref[...], preferred_element_type=F32)
        small_ref[0:1, :] += jnp.sum(dxn * n1, axis=0, keepdims=True)
        dxg = dxn * g_ref[...]
        gx_ref[...] = r1 * (dxg - n1 * jnp.mean(dxg * n1, axis=-1, keepdims=True)) + dh2_ref[...]

        @pl.when(last)
        def _():
            pltpu.sync_copy(acc_ref, gw_hbm)

    tile = lambda w: pl.BlockSpec((tb, w), lambda i: (i, 0))
    whole = lambda r, w: pl.BlockSpec((r, w), lambda i: (0, 0))
    halo = pl.BlockSpec((per, BLOCK, KV_W), lambda i: (i, 0, 0))
    halo_next = pl.BlockSpec((1, BLOCK, KV_W), lambda i: (jnp.minimum((i + 1) * per, seq // T_MIX - 1), 0, 0))
    next8 = pl.BlockSpec((8, CONV_W), lambda i: (jnp.minimum((i + 1) * (tb // 8), seq // 8 - 1), 0))
    ch = pl.BlockSpec((tb, 1024), lambda i: (i, 1))
    return pl.pallas_call(
        body, name="bwd_proj", grid=(nt,),
        out_shape=(jax.ShapeDtypeStruct((seq, D_MODEL), F32), jax.ShapeDtypeStruct((IN_W, D_MODEL), F32),
                   jax.ShapeDtypeStruct((SMALL_ROWS, D_MODEL), F32)),
        in_specs=[tile(D_MODEL), whole(1, D_MODEL), tile(D_MODEL), tile(ATTN_W), tile(KV_W), tile(KV_W), halo, halo,
                  halo_next, halo_next,
                  tile(ATTN_W), tile(CONV_W), tile(CONV_W), tile(CONV_W), next8, ch, whole(8, CONV_W),
                  pl.BlockSpec((IN_W, D_MODEL), lambda i: (0, 0), pipeline_mode=pl.Buffered(1)),
                  tile(128), tile(128), tile(128)],
        out_specs=(tile(D_MODEL), pl.BlockSpec(memory_space=pl.ANY), whole(SMALL_ROWS, D_MODEL)),
        scratch_shapes=[pltpu.VMEM((tb, IN_W), BF16), pltpu.VMEM((IN_W, D_MODEL), F32)],
        compiler_params=_params(("arbitrary",)),
    )(x, norm_g, dh2, dq, dk, dv, dkh, dvh, dkh, dvh, dga, db, dgc, dcv, dcv, rest, conv_w, w_in_t,
      rope_c, rope_a, rope_b)


def _adamw(w, g, m, v, name):
    rows, cols = w.shape
    tr = max(t for t in range(8, min(rows, 256) + 1, 8) if rows % t == 0)

    def body(w_ref, g_ref, m_ref, v_ref, d_ref, nm_ref, nv_ref):
        gg = g_ref[...]
        m2 = ADAM_B1 * m_ref[...] + (1.0 - ADAM_B1) * gg
        v2 = ADAM_B2 * v_ref[...] + (1.0 - ADAM_B2) * jnp.square(gg)
        m_hat = m2 / (1.0 - ADAM_B1 ** ADAM_STEP)
        v_hat = v2 / (1.0 - ADAM_B2 ** ADAM_STEP)
        d_ref[...] = -ADAM_LR * (m_hat / (jnp.sqrt(v_hat) + ADAM_EPS) + ADAM_WD * w_ref[...])
        nm_ref[...] = m2
        nv_ref[...] = v2

    spec = pl.BlockSpec((tr, cols), lambda i: (i, 0))
    shp = jax.ShapeDtypeStruct((rows, cols), F32)
    return pl.pallas_call(
        body, name=name, grid=(rows // tr,), out_shape=(shp, shp, shp),
        in_specs=[spec] * 4, out_specs=(spec,) * 3, compiler_params=_params(("arbitrary",)),
    )(w, g, m, v)


def _pack_small(norm_g, final_g, conv_w, sinks):
    out = jnp.zeros((SMALL_ROWS, D_MODEL), F32)
    out = out.at[0].set(norm_g).at[1].set(final_g)
    out = out.at[2:5, 0:128].set(conv_w).at[5, 0:8].set(sinks)
    return out


def kernel(x, norm_g, w_in, sinks, conv_w, w_out, final_g, loss_target, m_norm_g, m_w_in, m_sinks, m_conv_w, m_w_out, m_final_g, v_norm_g, v_w_in, v_sinks, v_conv_w, v_w_out, v_final_g):
    seq = x.shape[1]
    x2 = x.reshape(seq, D_MODEL)
    tgt = loss_target.reshape(seq, D_MODEL)
    ng = norm_g.reshape(1, D_MODEL)
    fg = final_g.reshape(1, D_MODEL)
    chip = 2 * lax.axis_index("x") + lax.axis_index("y")

    conv_w8 = jnp.zeros((8, 128), F32).at[0:3].set(conv_w)
    wi_all, wo_all, cw_all = _gather_weights(w_in.T, w_out, conv_w8)
    w_in_full = wi_all.reshape(IN_W, D_MODEL)
    w_out_full = wo_all.reshape(D_MODEL, D_MODEL)
    conv_full = jnp.concatenate([cw_all[j] for j in range(N_CHIPS)], axis=1)
    rope_c, rope_a, rope_b = _rope_tables(seq)

    q, kd, vd, rest = _fwd_proj(x2, ng, w_in_full, rope_c, rope_a, rope_b)
    attn, mix, dh2, small_f = _fwd_mix(x2, q, kd, vd, rest, sinks, conv_full, w_out_full, fg, tgt)
    dq, dk, dv, dkh, dvh, dga, db, dgc, dcv, g_wo, small_m = _bwd_mix(
        dh2, q, kd, vd, attn, rest, mix, sinks, conv_full, w_out_full, rope_c, rope_a, rope_b)
    grad_x, g_wi, small_p = _bwd_proj(x2, ng, dh2, dq, dk, dv, dkh, dvh, dga, db, dgc, dcv, rest, conv_full,
                                      w_in_full, rope_c, rope_a, rope_b)

    g_in_blocks = g_wi.reshape(N_CHIPS, W_IN_BLK, D_MODEL)
    g_out_blocks = g_wo.reshape(N_CHIPS, W_OUT_BLK, D_MODEL)
    grad_w_in_t, grad_w_out, small = _reduce_grads(g_in_blocks, g_out_blocks, small_f + small_m + small_p)
    grad_w_in = grad_w_in_t.T

    loss = jnp.sum(small[6])
    grad_norm_g, grad_final_g = small[0], small[1]
    grad_conv_full = small[2:5, 0:CONV_W]
    grad_conv_w = lax.dynamic_slice(grad_conv_full, (0, 128 * chip), (3, 128))
    grad_sinks = small[5, 0:8]

    d_wi, nm_wi, nv_wi = (t.T for t in _adamw(w_in.T, grad_w_in_t, m_w_in.T, v_w_in.T, "adamw_w_in"))
    d_wo, nm_wo, nv_wo = _adamw(w_out, grad_w_out, m_w_out, v_w_out, "adamw_w_out")
    d_s, nm_s, nv_s = _adamw(_pack_small(norm_g, final_g, conv_w, sinks),
                             _pack_small(grad_norm_g, grad_final_g, grad_conv_w, grad_sinks),
                             _pack_small(m_norm_g, m_final_g, m_conv_w, m_sinks),
                             _pack_small(v_norm_g, v_final_g, v_conv_w, v_sinks), "adamw_small")

    def unpack(p):
        return p[0], p[2:5, 0:128], p[5, 0:8], p[1]

    out = [loss, grad_x.reshape(1, seq, D_MODEL), grad_norm_g, grad_w_in, grad_sinks, grad_conv_w, grad_w_out,
           grad_final_g]
    for p_small, p_wi, p_wo in ((d_s, d_wi, d_wo), (nm_s, nm_wi, nm_wo), (nv_s, nv_wi, nv_wo)):
        n_g, c_w, s_k, f_g = unpack(p_small)
        out += [n_g, p_wi, s_k, c_w, p_wo, f_g]
    return tuple(out)
```

```python
import functools

import jax
import jax.numpy as jnp
from jax import lax
from jax.experimental import pallas as pl
from jax.experimental.pallas import tpu as pltpu

F32 = jnp.float32
BF16 = jnp.bfloat16

D_MODEL = 1024
HEAD_DIM = 64
ATTN_W = 512
KV_W = 128
CONV_W = 512
IN_W = 3328
REST_W = IN_W - ATTN_W - 2 * KV_W
BLOCK = 128
ROT_DIM = 16
ROPE_THETA = 500000.0
EPS = 1e-5
SCALE = 0.125
NEG = -1e30

N_CHIPS = 4
W_IN_BLK = IN_W // N_CHIPS
W_OUT_BLK = D_MODEL // N_CHIPS

ADAM_LR = 0.001
ADAM_B1 = 0.9
ADAM_B2 = 0.999
ADAM_EPS = 1e-08
ADAM_WD = 0.01
ADAM_STEP = 10

VMEM_LIMIT = 56 * 1024 * 1024
T_PROJ = 512
T_FMIX = 512
T_MIX = 512
SMALL_ROWS = 8
MESH = pl.DeviceIdType.MESH

_NT = (((1,), (1,)), ((), ()))
_TN = (((0,), (0,)), ((), ()))


def _params(sem=None):
    kw = dict(vmem_limit_bytes=VMEM_LIMIT)
    if sem is not None:
        kw["dimension_semantics"] = sem
    return pltpu.CompilerParams(**kw)


def _sigmoid(t):
    return 1.0 / (1.0 + jnp.exp(-t))


def _shift_down(t, prev8, k):
    rolled = pltpu.roll(t, k, 0)
    row = lax.broadcasted_iota(jnp.int32, t.shape, 0)
    for j in range(k):
        rolled = jnp.where(row == j, prev8[8 - k + j:8 - k + j + 1, :], rolled)
    return rolled


def _shift_up(t, next8, k):
    n = t.shape[0]
    rolled = pltpu.roll(t, n - k, 0)
    row = lax.broadcasted_iota(jnp.int32, t.shape, 0)
    for j in range(k):
        rolled = jnp.where(row == n - k + j, next8[j:j + 1, :], rolled)
    return rolled


def _rope(t, c, a, b):
    w = t.shape[1]
    reps = w // 128
    if reps > 1:
        c, a, b = (jnp.concatenate([z] * reps, axis=1) for z in (c, a, b))
    return t * c + pltpu.roll(t, w - 8, 1) * a + pltpu.roll(t, 8, 1) * b


def _rope_tables(seq):
    pos = jnp.arange(seq, dtype=jnp.int32)
    j = jnp.arange(128, dtype=jnp.int32) % HEAD_DIM
    inv_freq = ROPE_THETA ** (-(2 * (j % 8)).astype(F32) / ROT_DIM)
    ang = pos.astype(F32)[:, None] * inv_freq[None, :]
    cos, sin = jnp.cos(ang), jnp.sin(ang)
    c = jnp.where(j < ROT_DIM, cos, 1.0)
    a = jnp.where(j < 8, -sin, 0.0)
    b = jnp.where((j >= 8) & (j < ROT_DIM), sin, 0.0)
    return c, a, b


def _lane_lo(shape):
    return lax.broadcasted_iota(jnp.int32, shape, 1) < HEAD_DIM


def _stack_heads(t, g):
    lo = _lane_lo((BLOCK, 128))
    parts = []
    for hh in range(4):
        pair = t[:, 256 * g + 128 * (hh // 2):256 * g + 128 * (hh // 2) + 128]
        keep = lo if hh % 2 == 0 else jnp.logical_not(lo)
        parts.append(jnp.where(keep, pair, jnp.zeros_like(pair)))
    return jnp.concatenate(parts, axis=0)


def _unstack_pair(o, pp):
    lo = _lane_lo((BLOCK, 128))
    return jnp.where(lo, o[256 * pp:256 * pp + 128], o[256 * pp + 128:256 * pp + 256])


def _band_mask(has_prev):
    r = lax.broadcasted_iota(jnp.int32, (4 * BLOCK, 2 * BLOCK), 0) % BLOCK
    kj = lax.broadcasted_iota(jnp.int32, (4 * BLOCK, 2 * BLOCK), 1)
    cur = (kj >= BLOCK) & (kj - BLOCK <= r)
    prev = (kj < BLOCK) & (kj > r)
    if has_prev is not True:
        prev = prev & has_prev
    return cur | prev


def _sink_col(sinks_ref, g):
    r = lax.broadcasted_iota(jnp.int32, (4 * BLOCK, 1), 0) // BLOCK
    col = jnp.full((4 * BLOCK, 1), sinks_ref[4 * g + 3], F32)
    for hh in range(3):
        col = jnp.where(r == hh, sinks_ref[4 * g + hh], col)
    return col


def _probs(qs, kd, sink_col, mask):
    return _softmax(lax.dot_general(qs, kd, _NT, preferred_element_type=F32), sink_col, mask)


def _softmax(s, sink_col, mask):
    s = jnp.where(mask, s, NEG)
    m = jnp.maximum(jnp.max(s, axis=-1, keepdims=True), sink_col)
    p = jnp.exp(s - m)
    es = jnp.exp(sink_col - m)
    inv = 1.0 / (jnp.sum(p, axis=-1, keepdims=True) + es)
    return p * inv, es * inv


def _gather_weights(w_in_t, w_out, conv_w8):
    hi, ho = W_IN_BLK // 2, W_OUT_BLK // 2

    def body(wi_ref, wo_ref, cw_ref, wi_all, wo_all, cw_all, send_sems, recv_sems):
        x, y, c = lax.axis_index("x"), lax.axis_index("y"), lax.axis_index("c")
        me = 2 * x + y
        sibling = (x, y, 1 - c)
        chips = [(1 - x, y), (x, 1 - y), (1 - x, 1 - y)]

        wi_all[me] = wi_ref[...].astype(BF16)
        wo_all[me] = wo_ref[...].astype(BF16)
        cw_all[me] = cw_ref[...]

        def copies(k, chip, half, to):
            j = 2 * chip[0] + chip[1]
            refs = (wi_all.at[j, pl.ds(half * hi, hi)], wo_all.at[j, pl.ds(half * ho, ho)])
            return [pltpu.make_async_remote_copy(src_ref=r, dst_ref=r, send_sem=send_sems.at[2 * k + n],
                                                 recv_sem=recv_sems.at[2 * k + n], device_id=to,
                                                 device_id_type=MESH) for n, r in enumerate(refs)]

        def conv_copy(k, chip, to):
            r = cw_all.at[2 * chip[0] + chip[1]]
            return pltpu.make_async_remote_copy(src_ref=r, dst_ref=r, send_sem=send_sems.at[12 + k],
                                                recv_sem=recv_sems.at[12 + k], device_id=to, device_id_type=MESH)

        first = [cp for k, chip in enumerate(chips) for cp in copies(k, (x, y), c, (*chip, c))]
        first += [conv_copy(k, (x, y), (*chip, c)) for k, chip in enumerate(chips)]
        for cp in first:
            cp.start()
        passed = []
        for k, chip in enumerate(chips):
            for cp in copies(k, chip, c, (x, y, c)):
                cp.wait_recv()
            fwd = copies(3 + k, chip, c, sibling)
            for cp in fwd:
                cp.start()
            passed += fwd
        for k, chip in enumerate(chips):
            for cp in copies(3 + k, chip, 1 - c, (x, y, c)):
                cp.wait_recv()
            conv_copy(k, chip, (x, y, c)).wait_recv()
        for cp in first + passed:
            cp.wait_send()

    vmem = pl.BlockSpec(memory_space=pltpu.VMEM)
    return pl.pallas_call(
        body, name="gather_weights",
        out_shape=(jax.ShapeDtypeStruct((N_CHIPS, W_IN_BLK, D_MODEL), BF16),
                   jax.ShapeDtypeStruct((N_CHIPS, W_OUT_BLK, D_MODEL), BF16),
                   jax.ShapeDtypeStruct((N_CHIPS, 8, 128), F32)),
        in_specs=[vmem, vmem, vmem], out_specs=(vmem, vmem, vmem),
        scratch_shapes=[pltpu.SemaphoreType.DMA((15,)), pltpu.SemaphoreType.DMA((15,))],
        compiler_params=_params(),
    )(w_in_t, w_out, conv_w8)


def _reduce_grads(g_in, g_out, small):
    hi, ho = W_IN_BLK // 2, W_OUT_BLK // 2

    def body(gi_hbm, go_hbm, small_ref, gi_out, go_out, small_out,
             mine_i, mine_o, sib_i, sib_o, out_i, out_o, ici_i, ici_o, small_in, send_sems, recv_sems, local_sems):
        x, y, c = lax.axis_index("x"), lax.axis_index("y"), lax.axis_index("c")
        me = 2 * x + y
        my_dev = 4 * x + 2 * y + c
        sibling = (x, y, 1 - c)
        chips = [(1 - x, y), (x, 1 - y), (1 - x, 1 - y)]

        def remote(k, src, dst, to):
            return pltpu.make_async_remote_copy(src_ref=src, dst_ref=dst, send_sem=send_sems.at[k],
                                                recv_sem=recv_sems.at[k], device_id=to, device_id_type=MESH)

        small_cps = []
        for f in range(1, 8):
            fx, fy, fc = f >> 2, (f >> 1) & 1, f & 1
            to = (x ^ fx, y ^ fy, c ^ fc)
            small_cps.append(remote(10 + f, small_ref, small_in.at[f - 1], to))
        for cp in small_cps:
            cp.start()

        own = [pltpu.make_async_copy(gi_hbm.at[:, pl.ds(c * hi, hi)], mine_i, local_sems.at[0]),
               pltpu.make_async_copy(go_hbm.at[:, pl.ds(c * ho, ho)], mine_o, local_sems.at[1])]
        for cp in own:
            cp.start()
        to_sib = [remote(0, gi_hbm.at[:, pl.ds((1 - c) * hi, hi)], sib_i, sibling),
                  remote(1, go_hbm.at[:, pl.ds((1 - c) * ho, ho)], sib_o, sibling)]
        for cp in to_sib:
            cp.start()
        for cp in own:
            cp.wait()
        for cp in to_sib:
            cp.wait_recv()

        ici = []
        for k, chip in enumerate(chips):
            j = 2 * chip[0] + chip[1]
            out_i[k] = (mine_i[j] + sib_i[j]).astype(BF16)
            out_o[k] = (mine_o[j] + sib_o[j]).astype(BF16)
            ici += [remote(2 + 2 * k, out_i.at[k], ici_i.at[k], (*chip, c)),
                    remote(3 + 2 * k, out_o.at[k], ici_o.at[k], (*chip, c))]
            ici[-2].start()
            ici[-1].start()
        tot_i = mine_i[me] + sib_i[me]
        tot_o = mine_o[me] + sib_o[me]
        for k in range(3):
            ici[2 * k].wait_recv()
            ici[2 * k + 1].wait_recv()
            tot_i = tot_i + ici_i[k].astype(F32)
            tot_o = tot_o + ici_o[k].astype(F32)
        gi_out[pl.ds(c * hi, hi), :] = tot_i
        go_out[pl.ds(c * ho, ho), :] = tot_o

        swap = [remote(8, gi_out.at[pl.ds(c * hi, hi)], gi_out.at[pl.ds(c * hi, hi)], sibling),
                remote(9, go_out.at[pl.ds(c * ho, ho)], go_out.at[pl.ds(c * ho, ho)], sibling)]
        for cp in swap:
            cp.start()

        for cp in small_cps:
            cp.wait_recv()
        total = jnp.zeros((SMALL_ROWS, D_MODEL), F32)
        for d in range(8):
            slot = jnp.maximum((d ^ my_dev) - 1, 0)
            total = total + jnp.where(d == my_dev, small_ref[...], small_in[slot])
        small_out[...] = total

        recv_swap = [remote(8, gi_out.at[pl.ds((1 - c) * hi, hi)], gi_out.at[pl.ds((1 - c) * hi, hi)], sibling),
                     remote(9, go_out.at[pl.ds((1 - c) * ho, ho)], go_out.at[pl.ds((1 - c) * ho, ho)], sibling)]
        for cp in recv_swap:
            cp.wait_recv()
        for cp in to_sib + ici + swap + small_cps:
            cp.wait_send()

    vmem = pl.BlockSpec(memory_space=pltpu.VMEM)
    anyspace = pl.BlockSpec(memory_space=pl.ANY)
    return pl.pallas_call(
        body, name="reduce_grads",
        out_shape=(jax.ShapeDtypeStruct((W_IN_BLK, D_MODEL), F32),
                   jax.ShapeDtypeStruct((W_OUT_BLK, D_MODEL), F32),
                   jax.ShapeDtypeStruct((SMALL_ROWS, D_MODEL), F32)),
        in_specs=[anyspace, anyspace, vmem], out_specs=(vmem, vmem, vmem),
        scratch_shapes=[pltpu.VMEM((N_CHIPS, hi, D_MODEL), F32), pltpu.VMEM((N_CHIPS, ho, D_MODEL), F32),
                        pltpu.VMEM((N_CHIPS, hi, D_MODEL), F32), pltpu.VMEM((N_CHIPS, ho, D_MODEL), F32),
                        pltpu.VMEM((3, hi, D_MODEL), BF16), pltpu.VMEM((3, ho, D_MODEL), BF16),
                        pltpu.VMEM((3, hi, D_MODEL), BF16), pltpu.VMEM((3, ho, D_MODEL), BF16),
                        pltpu.VMEM((7, SMALL_ROWS, D_MODEL), F32),
                        pltpu.SemaphoreType.DMA((18,)), pltpu.SemaphoreType.DMA((18,)),
                        pltpu.SemaphoreType.DMA((2,))],
        compiler_params=_params(),
    )(g_in, g_out, small)


def _fwd_proj(x, norm_g, w_in_t, rope_c, rope_a, rope_b):
    seq = x.shape[0]
    nt = seq // T_PROJ

    def body(x_ref, g_ref, w_ref, c_ref, a_ref, b_ref, q_ref, kd_ref, vd_ref, rest_ref):
        xf = x_ref[...]
        r1 = lax.rsqrt(jnp.mean(xf * xf, axis=-1, keepdims=True) + EPS)
        xn = (xf * r1 * g_ref[...]).astype(BF16)
        c, a, b = c_ref[...], a_ref[...], b_ref[...]
        proj = lambda lo_c, w: lax.dot_general(xn, w_ref[lo_c:lo_c + w, :], _NT, preferred_element_type=F32)
        q_ref[...] = (_rope(proj(0, ATTN_W), c, a, b) * SCALE).astype(BF16)
        kv = proj(ATTN_W, 2 * KV_W)
        k = _rope(kv[:, 0:KV_W], c, a, b)
        v = kv[:, KV_W:2 * KV_W]
        lo = _lane_lo(k.shape)
        for t, ref in ((k, kd_ref), (v, vd_ref)):
            sw = pltpu.roll(t, HEAD_DIM, 1)
            ref[:, 0:128] = jnp.where(lo, t, sw).astype(BF16)
            ref[:, 128:256] = jnp.where(lo, sw, t).astype(BF16)
        for n in range(REST_W // 512):
            rest_ref[:, 512 * n:512 * (n + 1)] = proj(ATTN_W + 2 * KV_W + 512 * n, 512)

    tile = lambda w: pl.BlockSpec((T_PROJ, w), lambda i: (i, 0))
    whole = lambda r, w: pl.BlockSpec((r, w), lambda i: (0, 0))
    return pl.pallas_call(
        body, name="fwd_proj", grid=(nt,),
        out_shape=(jax.ShapeDtypeStruct((seq, ATTN_W), BF16), jax.ShapeDtypeStruct((seq, 2 * KV_W), BF16),
                   jax.ShapeDtypeStruct((seq, 2 * KV_W), BF16), jax.ShapeDtypeStruct((seq, REST_W), F32)),
        in_specs=[tile(D_MODEL), whole(1, D_MODEL), whole(IN_W, D_MODEL), tile(128), tile(128), tile(128)],
        out_specs=(tile(ATTN_W), tile(2 * KV_W), tile(2 * KV_W), tile(REST_W)),
        compiler_params=_params(("arbitrary",)),
    )(x, norm_g, w_in_t, rope_c, rope_a, rope_b)


def _conv_parts(rest_ref, prev_ref, cw_ref, first):
    u = rest_ref[:, 1024:1536] * rest_ref[:, 1536:2048]
    up = prev_ref[:, 1024:1536] * prev_ref[:, 1536:2048]
    up = jnp.where(first, jnp.zeros_like(up), up)
    um1 = _shift_down(u, up, 1)
    um2 = _shift_down(u, up, 2)
    cv = cw_ref[0:1, :] * um2 + cw_ref[1:2, :] * um1 + cw_ref[2:3, :] * u
    return u, um1, um2, cv


def _fwd_mix(x, q, kd, vd, rest, sinks, conv_w, w_out, final_g, target):
    seq = x.shape[0]
    nt = seq // T_FMIX
    nsub = T_FMIX // BLOCK

    def body(sinks_ref, x_ref, q_ref, kd_ref, vd_ref, kdp_ref, vdp_ref, rest_ref, restp_ref, cw_ref, wo_ref,
             fg_ref, tgt_ref, attn_ref, dh2_ref, gwo_ref, small_ref, mix_ref, pmix_ref, pdh2_ref):
        i = pl.program_id(0)

        @pl.when(i == 0)
        def _():
            small_ref[...] = jnp.zeros_like(small_ref)
            gwo_ref[...] = jnp.zeros_like(gwo_ref)
            pmix_ref[...] = jnp.zeros_like(pmix_ref)
            pdh2_ref[...] = jnp.zeros_like(pdh2_ref)

        gwo_ref[...] += lax.dot_general(pmix_ref[...], pdh2_ref[...], _TN, preferred_element_type=F32)

        chains = []
        for sb in range(nsub):
            rows = slice(BLOCK * sb, BLOCK * (sb + 1))
            if sb == 0:
                kk = jnp.concatenate([kdp_ref[...], kd_ref[rows, :]], axis=0)
                vv = jnp.concatenate([vdp_ref[...], vd_ref[rows, :]], axis=0)
                mask = _band_mask(i > 0)
            else:
                both = slice(BLOCK * (sb - 1), BLOCK * (sb + 1))
                kk, vv = kd_ref[both, :], vd_ref[both, :]
                mask = _band_mask(True)
            qt = q_ref[rows, :]
            for g in range(2):
                kg = kk[:, 128 * g:128 * (g + 1)]
                chains.append(dict(g=g, rows=rows, mask=mask, vg=vv[:, 128 * g:128 * (g + 1)],
                                   s=lax.dot_general(_stack_heads(qt, g), kg, _NT, preferred_element_type=F32)))
        for ch in chains:
            ch["prob"], _ = _softmax(ch.pop("s"), _sink_col(sinks_ref, ch["g"]), ch["mask"])
        for ch in chains:
            o = jnp.dot(ch["prob"].astype(BF16), ch["vg"], preferred_element_type=F32)
            for pp in range(2):
                lanes = slice(256 * ch["g"] + 128 * pp, 256 * ch["g"] + 128 * (pp + 1))
                attn_ref[ch["rows"], lanes] = _unstack_pair(o, pp)

        ga = rest_ref[:, 0:512]
        mix_ref[:, 0:ATTN_W] = (attn_ref[...] * (ga * _sigmoid(ga))).astype(BF16)
        _, _, _, cv = _conv_parts(rest_ref, restp_ref, cw_ref, i == 0)
        gc = rest_ref[:, 2048:2560]
        mix_ref[:, ATTN_W:] = (rest_ref[:, 512:1024] * cv * (gc * _sigmoid(gc))).astype(BF16)

        h2 = x_ref[...] + jnp.dot(mix_ref[...], wo_ref[...], preferred_element_type=F32)
        r2 = lax.rsqrt(jnp.mean(h2 * h2, axis=-1, keepdims=True) + EPS)
        n2 = h2 * r2
        err = n2 * fg_ref[...] - tgt_ref[...]
        dy = err * (1.0 / D_MODEL)
        small_ref[6:7, :] += jnp.sum(err * err, axis=0, keepdims=True) * (0.5 / D_MODEL)
        small_ref[1:2, :] += jnp.sum(dy * n2, axis=0, keepdims=True)
        dn = dy * fg_ref[...]
        dh2 = r2 * (dn - n2 * jnp.mean(dn * n2, axis=-1, keepdims=True))
        dh2_ref[...] = dh2
        pmix_ref[...] = mix_ref[...]
        pdh2_ref[...] = dh2.astype(BF16)

        @pl.when(i == nt - 1)
        def _():
            gwo_ref[...] += lax.dot_general(pmix_ref[...], pdh2_ref[...], _TN, preferred_element_type=F32)

    tile = lambda w: pl.BlockSpec((T_FMIX, w), lambda i: (i, 0))
    whole = lambda r, w: pl.BlockSpec((r, w), lambda i: (0, 0))
    prev_blk = pl.BlockSpec((BLOCK, 2 * KV_W), lambda i: (jnp.maximum(i * nsub - 1, 0), 0))
    prev8 = pl.BlockSpec((8, REST_W), lambda i: (jnp.maximum(i * (T_FMIX // 8) - 1, 0), 0))
    return pl.pallas_call(
        body, name="fwd_mix", grid=(nt,),
        out_shape=(jax.ShapeDtypeStruct((seq, ATTN_W), F32), jax.ShapeDtypeStruct((seq, D_MODEL), F32),
                   jax.ShapeDtypeStruct((D_MODEL, D_MODEL), F32), jax.ShapeDtypeStruct((SMALL_ROWS, D_MODEL), F32)),
        in_specs=[pl.BlockSpec(memory_space=pltpu.SMEM), tile(D_MODEL), tile(ATTN_W), tile(2 * KV_W), tile(2 * KV_W),
                  prev_blk, prev_blk, tile(REST_W), prev8, whole(8, CONV_W), whole(D_MODEL, D_MODEL),
                  whole(1, D_MODEL), tile(D_MODEL)],
        out_specs=(tile(ATTN_W), tile(D_MODEL), whole(D_MODEL, D_MODEL), whole(SMALL_ROWS, D_MODEL)),
        scratch_shapes=[pltpu.VMEM((T_FMIX, D_MODEL), BF16)] * 3,
        compiler_params=_params(("arbitrary",)),
    )(sinks, x, q, kd, vd, kd, vd, rest, rest, conv_w, w_out, final_g, target)


def _bwd_mix(dh2, q, kd, vd, attn, rest, sinks, conv_w, w_out, rope_c, rope_a, rope_b):
    seq = dh2.shape[0]
    nt = seq // T_MIX
    nsub = T_MIX // BLOCK

    def body(sinks_ref, dh2_ref, q_ref, kd_ref, vd_ref, kdp_ref, vdp_ref, attn_ref, rest_ref, restp_ref,
             cw_ref, wo_ref, c_ref, a_ref, b_ref,
             dq_ref, dk_ref, dv_ref, dkh_ref, dvh_ref, dga_ref, db_ref, dgc_ref, dcv_ref, small_ref,
             dmix_ref, dsink_ref):
        i = pl.program_id(0)

        @pl.when(i == 0)
        def _():
            small_ref[...] = jnp.zeros_like(small_ref)
            dsink_ref[...] = jnp.zeros_like(dsink_ref)

        dmix_ref[...] = lax.dot_general(dh2_ref[...].astype(BF16), wo_ref[...], _NT, preferred_element_type=F32)

        ga = rest_ref[:, 0:512]
        sg = _sigmoid(ga)
        dma = dmix_ref[:, 0:ATTN_W]
        dga_ref[...] = (dma * attn_ref[...] * (sg * (1.0 + ga * (1.0 - sg)))).astype(BF16)
        dmix_ref[:, 0:ATTN_W] = dma * (ga * sg)

        u, um1, um2, cv = _conv_parts(rest_ref, restp_ref, cw_ref, i == 0)
        gc = rest_ref[:, 2048:2560]
        sc = _sigmoid(gc)
        bg = rest_ref[:, 512:1024]
        dmc = dmix_ref[:, ATTN_W:]
        t1 = dmc * (gc * sc)
        db_ref[...] = (t1 * cv).astype(BF16)
        dcv = t1 * bg
        dcv_ref[...] = dcv
        dgc_ref[...] = (dmc * (bg * cv) * (sc * (1.0 + gc * (1.0 - sc)))).astype(BF16)
        small_ref[2:3, 0:CONV_W] += jnp.sum(dcv * um2, axis=0, keepdims=True)
        small_ref[3:4, 0:CONV_W] += jnp.sum(dcv * um1, axis=0, keepdims=True)
        small_ref[4:5, 0:CONV_W] += jnp.sum(dcv * u, axis=0, keepdims=True)

        lo = _lane_lo((2 * BLOCK, 128))
        dk_blocks = [None] * (nsub + 1)
        dv_blocks = [None] * (nsub + 1)

        def add(lst, n, val):
            lst[n] = val if lst[n] is None else lst[n] + val

        chains = []
        for sb in range(nsub):
            rows = slice(BLOCK * sb, BLOCK * (sb + 1))
            if sb == 0:
                kk = jnp.concatenate([kdp_ref[...], kd_ref[rows, :]], axis=0)
                vv = jnp.concatenate([vdp_ref[...], vd_ref[rows, :]], axis=0)
                mask = _band_mask(i > 0)
            else:
                both = slice(BLOCK * (sb - 1), BLOCK * (sb + 1))
                kk, vv = kd_ref[both, :], vd_ref[both, :]
                mask = _band_mask(True)
            qt = q_ref[rows, :]
            dot = dmix_ref[rows, 0:ATTN_W].astype(BF16)
            for g in range(2):
                chains.append(dict(sb=sb, g=g, rows=rows, mask=mask, qs=_stack_heads(qt, g), dos=_stack_heads(dot, g),
                                   kg=kk[:, 128 * g:128 * (g + 1)], vg=vv[:, 128 * g:128 * (g + 1)]))
        for ch in chains:
            ch["prob"], ch["psink"] = _probs(ch["qs"], ch["kg"], _sink_col(sinks_ref, ch["g"]), ch["mask"])
        for ch in chains:
            ch["dp"] = lax.dot_general(ch["dos"], ch["vg"], _NT, preferred_element_type=F32)
        for ch in chains:
            rs = jnp.sum(ch["prob"] * ch["dp"], axis=-1, keepdims=True)
            ch["ds"] = (ch["prob"] * (ch["dp"] - rs)).astype(BF16)
            dsink_ref[ch["g"]] += -ch["psink"] * rs
        for ch in chains:
            dqs = jnp.dot(ch["ds"], ch["kg"], preferred_element_type=F32) * SCALE
            c, a, b = c_ref[ch["rows"], :], a_ref[ch["rows"], :], b_ref[ch["rows"], :]
            for pp in range(2):
                lanes = slice(256 * ch["g"] + 128 * pp, 256 * ch["g"] + 128 * (pp + 1))
                dq_ref[ch["rows"], lanes] = _rope(_unstack_pair(dqs, pp), c, -a, -b).astype(BF16)
            dkd = lax.dot_general(ch["ds"], ch["qs"], _TN, preferred_element_type=F32)
            dvd = lax.dot_general(ch["prob"].astype(BF16), ch["dos"], _TN, preferred_element_type=F32)
            ch["dk"] = dkd + pltpu.roll(dkd, HEAD_DIM, 1)
            ch["dv"] = dvd + pltpu.roll(dvd, HEAD_DIM, 1)
        for sb in range(nsub):
            dk2 = jnp.where(lo, chains[2 * sb]["dk"], chains[2 * sb + 1]["dk"])
            dv2 = jnp.where(lo, chains[2 * sb]["dv"], chains[2 * sb + 1]["dv"])
            add(dk_blocks, sb, dk2[0:BLOCK])
            add(dk_blocks, sb + 1, dk2[BLOCK:])
            add(dv_blocks, sb, dv2[0:BLOCK])
            add(dv_blocks, sb + 1, dv2[BLOCK:])
        dkh_ref[0] = dk_blocks[0]
        dvh_ref[0] = dv_blocks[0]
        for sb in range(nsub):
            dk_ref[BLOCK * sb:BLOCK * (sb + 1), :] = dk_blocks[sb + 1]
            dv_ref[BLOCK * sb:BLOCK * (sb + 1), :] = dv_blocks[sb + 1]

        @pl.when(i == nt - 1)
        def _():
            for h in range(8):
                tot = jnp.sum(dsink_ref[h // 4, BLOCK * (h % 4):BLOCK * (h % 4 + 1), :], axis=0, keepdims=True)
                small_ref[5:6, h:h + 1] = tot

    tile = lambda w: pl.BlockSpec((T_MIX, w), lambda i: (i, 0))
    whole = lambda r, w: pl.BlockSpec((r, w), lambda i: (0, 0))
    prev_blk = pl.BlockSpec((BLOCK, 2 * KV_W), lambda i: (jnp.maximum(i * nsub - 1, 0), 0))
    prev8 = pl.BlockSpec((8, REST_W), lambda i: (jnp.maximum(i * (T_MIX // 8) - 1, 0), 0))
    halo = pl.BlockSpec((1, BLOCK, KV_W), lambda i: (i, 0, 0))
    bf = lambda w: jax.ShapeDtypeStruct((seq, w), BF16)
    f32 = lambda w: jax.ShapeDtypeStruct((seq, w), F32)
    return pl.pallas_call(
        body, name="bwd_mix", grid=(nt,),
        out_shape=(bf(ATTN_W), f32(KV_W), f32(KV_W), jax.ShapeDtypeStruct((nt, BLOCK, KV_W), F32),
                   jax.ShapeDtypeStruct((nt, BLOCK, KV_W), F32), bf(ATTN_W), bf(CONV_W), bf(CONV_W), f32(CONV_W),
                   jax.ShapeDtypeStruct((SMALL_ROWS, D_MODEL), F32)),
        in_specs=[pl.BlockSpec(memory_space=pltpu.SMEM), tile(D_MODEL), tile(ATTN_W), tile(2 * KV_W), tile(2 * KV_W),
                  prev_blk, prev_blk, tile(ATTN_W), tile(REST_W), prev8, whole(8, CONV_W),
                  whole(D_MODEL, D_MODEL), tile(128), tile(128), tile(128)],
        out_specs=(tile(ATTN_W), tile(KV_W), tile(KV_W), halo, halo, tile(ATTN_W), tile(CONV_W), tile(CONV_W),
                   tile(CONV_W), whole(SMALL_ROWS, D_MODEL)),
        scratch_shapes=[pltpu.VMEM((T_MIX, D_MODEL), F32), pltpu.VMEM((2, 4 * BLOCK, 1), F32)],
        compiler_params=_params(("arbitrary",)),
    )(sinks, dh2, q, kd, vd, kd, vd, attn, rest, rest, conv_w, w_out, rope_c, rope_a, rope_b)


def _bwd_proj(x, norm_g, dh2, dq, dk, dv, dkh, dvh, dga, db, dgc, dcv, rest, conv_w, w_in_t, rope_c, rope_a, rope_b):
    seq = x.shape[0]
    tb = T_PROJ
    per = tb // T_MIX
    nt = seq // tb

    def body(x_ref, g_ref, dh2_ref, dq_ref, dk_ref, dv_ref, dkh_ref, dvh_ref, dkn_ref, dvn_ref, dga_ref, db_ref,
             dgc_ref, dcv_ref, dcvn_ref, ch_ref, cw_ref, w_ref, c_ref, a_ref, b_ref, gx_ref, gw_hbm, small_ref,
             dp_ref, acc_ref):
        i = pl.program_id(0)

        @pl.when(i == 0)
        def _():
            small_ref[...] = jnp.zeros_like(small_ref)
            acc_ref[...] = jnp.zeros_like(acc_ref)

        last = i == nt - 1
        keep = jnp.where(last, 0.0, 1.0)
        pad = jnp.zeros((T_MIX - BLOCK, KV_W), F32)

        def with_halos(main_ref, halo_ref, next_ref):
            parts = []
            for m in range(1, per + 1):
                parts += [pad, halo_ref[m] if m < per else next_ref[0] * keep]
            return main_ref[...] + jnp.concatenate(parts, axis=0)

        dk = with_halos(dk_ref, dkh_ref, dkn_ref)
        dv = with_halos(dv_ref, dvh_ref, dvn_ref)
        dp_ref[:, 0:ATTN_W] = dq_ref[...]
        dp_ref[:, ATTN_W:ATTN_W + KV_W] = _rope(dk, c_ref[...], -a_ref[...], -b_ref[...]).astype(BF16)
        dp_ref[:, ATTN_W + KV_W:ATTN_W + 2 * KV_W] = dv.astype(BF16)
        base = ATTN_W + 2 * KV_W
        dp_ref[:, base:base + 512] = dga_ref[...]
        dp_ref[:, base + 512:base + 1024] = db_ref[...]
        dcv = dcv_ref[...]
        nxt = dcvn_ref[...] * keep
        du = cw_ref[2:3, :] * dcv + cw_ref[1:2, :] * _shift_up(dcv, nxt, 1) + cw_ref[0:1, :] * _shift_up(dcv, nxt, 2)
        dp_ref[:, base + 1024:base + 1536] = (du * ch_ref[:, 512:1024]).astype(BF16)
        dp_ref[:, base + 1536:base + 2048] = (du * ch_ref[:, 0:512]).astype(BF16)
        dp_ref[:, base + 2048:base + 2560] = dgc_ref[...]

        xf = x_ref[...]
        r1 = lax.rsqrt(jnp.mean(xf * xf, axis=-1, keepdims=True) + EPS)
        n1 = xf * r1
        xn = (n1 * g_ref[...]).astype(BF16)
        for n in range(IN_W // 256):
            cols = slice(256 * n, 256 * (n + 1))
            acc_ref[cols, :] += lax.dot_general(dp_ref[:, cols], xn, _TN, preferred_element_type=F32)
        dxn = jnp.dot(dp_ref[...], w_ref[...], preferred_element_type=F32)
        small_ref[0:1, :] += jnp.sum(dxn * n1, axis=0, keepdims=True)
        dxg = dxn * g_ref[...]
        gx_ref[...] = r1 * (dxg - n1 * jnp.mean(dxg * n1, axis=-1, keepdims=True)) + dh2_ref[...]

        @pl.when(last)
        def _():
            pltpu.sync_copy(acc_ref, gw_hbm)

    tile = lambda w: pl.BlockSpec((tb, w), lambda i: (i, 0))
    whole = lambda r, w: pl.BlockSpec((r, w), lambda i: (0, 0))
    halo = pl.BlockSpec((per, BLOCK, KV_W), lambda i: (i, 0, 0))
    halo_next = pl.BlockSpec((1, BLOCK, KV_W), lambda i: (jnp.minimum((i + 1) * per, seq // T_MIX - 1), 0, 0))
    next8 = pl.BlockSpec((8, CONV_W), lambda i: (jnp.minimum((i + 1) * (tb // 8), seq // 8 - 1), 0))
    ch = pl.BlockSpec((tb, 1024), lambda i: (i, 1))
    return pl.pallas_call(
        body, name="bwd_proj", grid=(nt,),
        out_shape=(jax.ShapeDtypeStruct((seq, D_MODEL), F32), jax.ShapeDtypeStruct((IN_W, D_MODEL), F32),
                   jax.ShapeDtypeStruct((SMALL_ROWS, D_MODEL), F32)),
        in_specs=[tile(D_MODEL), whole(1, D_MODEL), tile(D_MODEL), tile(ATTN_W), tile(KV_W), tile(KV_W), halo, halo,
                  halo_next, halo_next,
                  tile(ATTN_W), tile(CONV_W), tile(CONV_W), tile(CONV_W), next8, ch, whole(8, CONV_W),
                  pl.BlockSpec((IN_W, D_MODEL), lambda i: (0, 0), pipeline_mode=pl.Buffered(1)),
                  tile(128), tile(128), tile(128)],
        out_specs=(tile(D_MODEL), pl.BlockSpec(memory_space=pl.ANY), whole(SMALL_ROWS, D_MODEL)),
        scratch_shapes=[pltpu.VMEM((tb, IN_W), BF16), pltpu.VMEM((IN_W, D_MODEL), F32)],
        compiler_params=_params(("arbitrary",)),
    )(x, norm_g, dh2, dq, dk, dv, dkh, dvh, dkh, dvh, dga, db, dgc, dcv, dcv, rest, conv_w, w_in_t,
      rope_c, rope_a, rope_b)


def _adamw(w, g, m, v, name):
    rows, cols = w.shape
    tr = max(t for t in range(8, min(rows, 256) + 1, 8) if rows % t == 0)

    def body(w_ref, g_ref, m_ref, v_ref, d_ref, nm_ref, nv_ref):
        gg = g_ref[...]
        m2 = ADAM_B1 * m_ref[...] + (1.0 - ADAM_B1) * gg
        v2 = ADAM_B2 * v_ref[...] + (1.0 - ADAM_B2) * jnp.square(gg)
        m_hat = m2 / (1.0 - ADAM_B1 ** ADAM_STEP)
        v_hat = v2 / (1.0 - ADAM_B2 ** ADAM_STEP)
        d_ref[...] = -ADAM_LR * (m_hat / (jnp.sqrt(v_hat) + ADAM_EPS) + ADAM_WD * w_ref[...])
        nm_ref[...] = m2
        nv_ref[...] = v2

    spec = pl.BlockSpec((tr, cols), lambda i: (i, 0))
    shp = jax.ShapeDtypeStruct((rows, cols), F32)
    return pl.pallas_call(
        body, name=name, grid=(rows // tr,), out_shape=(shp, shp, shp),
        in_specs=[spec] * 4, out_specs=(spec,) * 3, compiler_params=_params(("arbitrary",)),
    )(w, g, m, v)


def _pack_small(norm_g, final_g, conv_w, sinks):
    out = jnp.zeros((SMALL_ROWS, D_MODEL), F32)
    out = out.at[0].set(norm_g).at[1].set(final_g)
    out = out.at[2:5, 0:128].set(conv_w).at[5, 0:8].set(sinks)
    return out


def kernel(x, norm_g, w_in, sinks, conv_w, w_out, final_g, loss_target, m_norm_g, m_w_in, m_sinks, m_conv_w, m_w_out, m_final_g, v_norm_g, v_w_in, v_sinks, v_conv_w, v_w_out, v_final_g):
    seq = x.shape[1]
    x2 = x.reshape(seq, D_MODEL)
    tgt = loss_target.reshape(seq, D_MODEL)
    ng = norm_g.reshape(1, D_MODEL)
    fg = final_g.reshape(1, D_MODEL)
    chip = 2 * lax.axis_index("x") + lax.axis_index("y")

    conv_w8 = jnp.zeros((8, 128), F32).at[0:3].set(conv_w)
    wi_all, wo_all, cw_all = _gather_weights(w_in.T, w_out, conv_w8)
    w_in_full = wi_all.reshape(IN_W, D_MODEL)
    w_out_full = wo_all.reshape(D_MODEL, D_MODEL)
    conv_full = jnp.concatenate([cw_all[j] for j in range(N_CHIPS)], axis=1)
    rope_c, rope_a, rope_b = _rope_tables(seq)

    q, kd, vd, rest = _fwd_proj(x2, ng, w_in_full, rope_c, rope_a, rope_b)
    attn, dh2, g_wo, small_f = _fwd_mix(x2, q, kd, vd, rest, sinks, conv_full, w_out_full, fg, tgt)
    dq, dk, dv, dkh, dvh, dga, db, dgc, dcv, small_m = _bwd_mix(
        dh2, q, kd, vd, attn, rest, sinks, conv_full, w_out_full, rope_c, rope_a, rope_b)
    grad_x, g_wi, small_p = _bwd_proj(x2, ng, dh2, dq, dk, dv, dkh, dvh, dga, db, dgc, dcv, rest, conv_full,
                                      w_in_full, rope_c, rope_a, rope_b)

    g_in_blocks = g_wi.reshape(N_CHIPS, W_IN_BLK, D_MODEL)
    g_out_blocks = g_wo.reshape(N_CHIPS, W_OUT_BLK, D_MODEL)
    grad_w_in_t, grad_w_out, small = _reduce_grads(g_in_blocks, g_out_blocks, small_f + small_m + small_p)
    grad_w_in = grad_w_in_t.T

    loss = jnp.sum(small[6])
    grad_norm_g, grad_final_g = small[0], small[1]
    grad_conv_full = small[2:5, 0:CONV_W]
    grad_conv_w = lax.dynamic_slice(grad_conv_full, (0, 128 * chip), (3, 128))
    grad_sinks = small[5, 0:8]

    d_wi, nm_wi, nv_wi = (t.T for t in _adamw(w_in.T, grad_w_in_t, m_w_in.T, v_w_in.T, "adamw_w_in"))
    d_wo, nm_wo, nv_wo = _adamw(w_out, grad_w_out, m_w_out, v_w_out, "adamw_w_out")
    d_s, nm_s, nv_s = _adamw(_pack_small(norm_g, final_g, conv_w, sinks),
                             _pack_small(grad_norm_g, grad_final_g, grad_conv_w, grad_sinks),
                             _pack_small(m_norm_g, m_final_g, m_conv_w, m_sinks),
                             _pack_small(v_norm_g, v_final_g, v_conv_w, v_sinks), "adamw_small")

    def unpack(p):
        return p[0], p[2:5, 0:128], p[5, 0:8], p[1]

    out = [loss, grad_x.reshape(1, seq, D_MODEL), grad_norm_g, grad_w_in, grad_sinks, grad_conv_w, grad_w_out,
           grad_final_g]
    for p_small, p_wi, p_wo in ((d_s, d_wi, d_wo), (nm_s, nm_wi, nm_wo), (nv_s, nv_wi, nv_wo)):
        n_g, c_w, s_k, f_g = unpack(p_small)
        out += [n_g, p_wi, s_k, c_w, p_wo, f_g]
    return tuple(out)
```

```python
import functools

import jax
import jax.numpy as jnp
from jax import lax
from jax.experimental import pallas as pl
from jax.experimental.pallas import tpu as pltpu

F32 = jnp.float32
BF16 = jnp.bfloat16

D_MODEL = 1024
HEAD_DIM = 64
ATTN_W = 512
KV_W = 128
CONV_W = 512
IN_W = 3328
REST_W = IN_W - ATTN_W - 2 * KV_W
BLOCK = 128
ROT_DIM = 16
ROPE_THETA = 500000.0
EPS = 1e-5
SCALE = 0.125
NEG = -1e30

N_CHIPS = 4
W_IN_BLK = IN_W // N_CHIPS
W_OUT_BLK = D_MODEL // N_CHIPS

ADAM_LR = 0.001
ADAM_B1 = 0.9
ADAM_B2 = 0.999
ADAM_EPS = 1e-08
ADAM_WD = 0.01
ADAM_STEP = 10

VMEM_LIMIT = 56 * 1024 * 1024
T_PROJ = 512
T_FMIX = 512
T_MIX = 512
SMALL_ROWS = 8
MESH = pl.DeviceIdType.MESH

_NT = (((1,), (1,)), ((), ()))
_TN = (((0,), (0,)), ((), ()))


def _params(sem=None):
    kw = dict(vmem_limit_bytes=VMEM_LIMIT)
    if sem is not None:
        kw["dimension_semantics"] = sem
    return pltpu.CompilerParams(**kw)


def _sigmoid(t):
    return 1.0 / (1.0 + jnp.exp(-t))


def _shift_down(t, prev8, k):
    rolled = pltpu.roll(t, k, 0)
    row = lax.broadcasted_iota(jnp.int32, t.shape, 0)
    for j in range(k):
        rolled = jnp.where(row == j, prev8[8 - k + j:8 - k + j + 1, :], rolled)
    return rolled


def _shift_up(t, next8, k):
    n = t.shape[0]
    rolled = pltpu.roll(t, n - k, 0)
    row = lax.broadcasted_iota(jnp.int32, t.shape, 0)
    for j in range(k):
        rolled = jnp.where(row == n - k + j, next8[j:j + 1, :], rolled)
    return rolled


def _rope(t, c, a, b):
    w = t.shape[1]
    reps = w // 128
    if reps > 1:
        c, a, b = (jnp.concatenate([z] * reps, axis=1) for z in (c, a, b))
    return t * c + pltpu.roll(t, w - 8, 1) * a + pltpu.roll(t, 8, 1) * b


def _rope_tables(seq):
    pos = jnp.arange(seq, dtype=jnp.int32)
    inv_freq = ROPE_THETA ** (-jnp.arange(0, ROT_DIM, 2, dtype=F32) / ROT_DIM)
    ang = pos.astype(F32)[:, None] * inv_freq[None, :]
    cos8, sin8 = jnp.cos(ang), jnp.sin(ang)
    j = jnp.arange(128, dtype=jnp.int32) % HEAD_DIM
    cos = sum(cos8[:, k:k + 1] * ((j % 8 == k) & (j < ROT_DIM)).astype(F32)[None, :] for k in range(8))
    sin = sum(sin8[:, k:k + 1] * ((j % 8 == k) & (j < ROT_DIM)).astype(F32)[None, :] for k in range(8))
    c = cos + (j >= ROT_DIM).astype(F32)[None, :]
    a = -sin * (j < 8).astype(F32)[None, :]
    b = sin * (j >= 8).astype(F32)[None, :]
    return c, a, b


def _lane_lo(shape):
    return lax.broadcasted_iota(jnp.int32, shape, 1) < HEAD_DIM


def _stack_heads(t, g):
    lo = _lane_lo((BLOCK, 128))
    parts = []
    for hh in range(4):
        pair = t[:, 256 * g + 128 * (hh // 2):256 * g + 128 * (hh // 2) + 128]
        keep = lo if hh % 2 == 0 else jnp.logical_not(lo)
        parts.append(jnp.where(keep, pair, jnp.zeros_like(pair)))
    return jnp.concatenate(parts, axis=0)


def _unstack_pair(o, pp):
    lo = _lane_lo((BLOCK, 128))
    return jnp.where(lo, o[256 * pp:256 * pp + 128], o[256 * pp + 128:256 * pp + 256])


def _band_masks(has_prev):
    r = lax.broadcasted_iota(jnp.int32, (4 * BLOCK, 2 * BLOCK), 0) % BLOCK
    kj = lax.broadcasted_iota(jnp.int32, (4 * BLOCK, 2 * BLOCK), 1)
    cur = (kj >= BLOCK) & (kj - BLOCK <= r)
    prev = (kj < BLOCK) & (kj > r)
    return cur | (prev & has_prev), cur | prev


def _sink_col(sinks_ref, g):
    r = lax.broadcasted_iota(jnp.int32, (4 * BLOCK, 1), 0) // BLOCK
    col = jnp.full((4 * BLOCK, 1), sinks_ref[4 * g + 3], F32)
    for hh in range(3):
        col = jnp.where(r == hh, sinks_ref[4 * g + hh], col)
    return col


def _probs(qs, kd, sink_col, mask):
    return _softmax(lax.dot_general(qs, kd, _NT, preferred_element_type=F32), sink_col, mask)


def _softmax(s, sink_col, mask):
    s = jnp.where(mask, s, NEG)
    m = jnp.maximum(jnp.max(s, axis=-1, keepdims=True), sink_col)
    p = jnp.exp(s - m)
    es = jnp.exp(sink_col - m)
    inv = 1.0 / (jnp.sum(p, axis=-1, keepdims=True) + es)
    return p * inv, es * inv


def _gather_weights(w_in_t, w_out, conv_w8):
    hi, ho = W_IN_BLK // 2, W_OUT_BLK // 2

    def body(wi_ref, wo_ref, cw_ref, wi_all, wo_all, cw_all, send_sems, recv_sems):
        x, y, c = lax.axis_index("x"), lax.axis_index("y"), lax.axis_index("c")
        me = 2 * x + y
        sibling = (x, y, 1 - c)
        chips = [(1 - x, y), (x, 1 - y), (1 - x, 1 - y)]

        wi_all[me] = wi_ref[...].astype(BF16)
        wo_all[me] = wo_ref[...].astype(BF16)
        cw_all[me] = cw_ref[...]

        def copies(k, chip, half, to):
            j = 2 * chip[0] + chip[1]
            refs = (wi_all.at[j, pl.ds(half * hi, hi)], wo_all.at[j, pl.ds(half * ho, ho)])
            return [pltpu.make_async_remote_copy(src_ref=r, dst_ref=r, send_sem=send_sems.at[2 * k + n],
                                                 recv_sem=recv_sems.at[2 * k + n], device_id=to,
                                                 device_id_type=MESH) for n, r in enumerate(refs)]

        def conv_copy(k, chip, to):
            r = cw_all.at[2 * chip[0] + chip[1]]
            return pltpu.make_async_remote_copy(src_ref=r, dst_ref=r, send_sem=send_sems.at[12 + k],
                                                recv_sem=recv_sems.at[12 + k], device_id=to, device_id_type=MESH)

        first = [cp for k, chip in enumerate(chips) for cp in copies(k, (x, y), c, (*chip, c))]
        first += [conv_copy(k, (x, y), (*chip, c)) for k, chip in enumerate(chips)]
        for cp in first:
            cp.start()
        passed = []
        for k, chip in enumerate(chips):
            for cp in copies(k, chip, c, (x, y, c)):
                cp.wait_recv()
            fwd = copies(3 + k, chip, c, sibling)
            for cp in fwd:
                cp.start()
            passed += fwd
        for k, chip in enumerate(chips):
            for cp in copies(3 + k, chip, 1 - c, (x, y, c)):
                cp.wait_recv()
            conv_copy(k, chip, (x, y, c)).wait_recv()
        for cp in first + passed:
            cp.wait_send()

    vmem = pl.BlockSpec(memory_space=pltpu.VMEM)
    return pl.pallas_call(
        body, name="gather_weights",
        out_shape=(jax.ShapeDtypeStruct((N_CHIPS, W_IN_BLK, D_MODEL), BF16),
                   jax.ShapeDtypeStruct((N_CHIPS, W_OUT_BLK, D_MODEL), BF16),
                   jax.ShapeDtypeStruct((N_CHIPS, 8, 128), F32)),
        in_specs=[vmem, vmem, vmem], out_specs=(vmem, vmem, vmem),
        scratch_shapes=[pltpu.SemaphoreType.DMA((15,)), pltpu.SemaphoreType.DMA((15,))],
        compiler_params=_params(),
    )(w_in_t, w_out, conv_w8)


def _fwd_proj(x, norm_g, w_in_t, rope_c, rope_a, rope_b):
    seq = x.shape[0]
    nt = seq // T_PROJ

    def body(x_ref, g_ref, w_ref, c_ref, a_ref, b_ref, q_ref, kd_ref, vd_ref, rest_ref):
        xf = x_ref[...]
        r1 = lax.rsqrt(jnp.mean(xf * xf, axis=-1, keepdims=True) + EPS)
        xn = (xf * r1 * g_ref[...]).astype(BF16)
        c, a, b = c_ref[...], a_ref[...], b_ref[...]
        proj = lambda lo_c, w: lax.dot_general(xn, w_ref[lo_c:lo_c + w, :], _NT, preferred_element_type=F32)
        q_ref[...] = (_rope(proj(0, ATTN_W), c, a, b) * SCALE).astype(BF16)
        kv = proj(ATTN_W, 2 * KV_W)
        k = _rope(kv[:, 0:KV_W], c, a, b)
        v = kv[:, KV_W:2 * KV_W]
        lo = _lane_lo(k.shape)
        for t, ref in ((k, kd_ref), (v, vd_ref)):
            sw = pltpu.roll(t, HEAD_DIM, 1)
            ref[:, 0:128] = jnp.where(lo, t, sw).astype(BF16)
            ref[:, 128:256] = jnp.where(lo, sw, t).astype(BF16)
        for n in range(REST_W // 512):
            rest_ref[:, 512 * n:512 * (n + 1)] = proj(ATTN_W + 2 * KV_W + 512 * n, 512)

    tile = lambda w: pl.BlockSpec((T_PROJ, w), lambda i: (i, 0))
    whole = lambda r, w: pl.BlockSpec((r, w), lambda i: (0, 0))
    return pl.pallas_call(
        body, name="fwd_proj", grid=(nt,),
        out_shape=(jax.ShapeDtypeStruct((seq, ATTN_W), BF16), jax.ShapeDtypeStruct((seq, 2 * KV_W), BF16),
                   jax.ShapeDtypeStruct((seq, 2 * KV_W), BF16), jax.ShapeDtypeStruct((seq, REST_W), F32)),
        in_specs=[tile(D_MODEL), whole(1, D_MODEL), whole(IN_W, D_MODEL), tile(128), tile(128), tile(128)],
        out_specs=(tile(ATTN_W), tile(2 * KV_W), tile(2 * KV_W), tile(REST_W)),
        compiler_params=_params(("arbitrary",)),
    )(x, norm_g, w_in_t, rope_c, rope_a, rope_b)


def _conv_parts(rest_ref, prev_ref, cw_ref, first):
    u = rest_ref[:, 1024:1536] * rest_ref[:, 1536:2048]
    up = prev_ref[:, 1024:1536] * prev_ref[:, 1536:2048]
    up = jnp.where(first, jnp.zeros_like(up), up)
    um1 = _shift_down(u, up, 1)
    um2 = _shift_down(u, up, 2)
    cv = cw_ref[0:1, :] * um2 + cw_ref[1:2, :] * um1 + cw_ref[2:3, :] * u
    return u, um1, um2, cv


def _fwd_mix(x, q, kd, vd, rest, sinks, conv_w, w_out, final_g, target):
    seq = x.shape[0]
    nt = seq // T_FMIX
    nsub = T_FMIX // BLOCK

    def body(sinks_ref, x_ref, q_ref, kd_ref, vd_ref, kdp_ref, vdp_ref, rest_ref, restp_ref, cw_ref, wo_ref,
             fg_ref, tgt_ref, attn_ref, dh2_ref, gwo_ref, gwob_ref, small_ref, mix_ref, pmix_ref, pdh2_ref):
        i = pl.program_id(0)

        @pl.when(i == 0)
        def _():
            small_ref[...] = jnp.zeros_like(small_ref)
            gwo_ref[...] = jnp.zeros_like(gwo_ref)
            pmix_ref[...] = jnp.zeros_like(pmix_ref)
            pdh2_ref[...] = jnp.zeros_like(pdh2_ref)

        gwo_ref[...] += lax.dot_general(pmix_ref[...], pdh2_ref[...], _TN, preferred_element_type=F32)

        chains = []
        mask_first, mask_rest = _band_masks(i > 0)
        for sb in range(nsub):
            rows = slice(BLOCK * sb, BLOCK * (sb + 1))
            if sb == 0:
                kk = jnp.concatenate([kdp_ref[...], kd_ref[rows, :]], axis=0)
                vv = jnp.concatenate([vdp_ref[...], vd_ref[rows, :]], axis=0)
                mask = mask_first
            else:
                both = slice(BLOCK * (sb - 1), BLOCK * (sb + 1))
                kk, vv = kd_ref[both, :], vd_ref[both, :]
                mask = mask_rest
            qt = q_ref[rows, :]
            for g in range(2):
                kg = kk[:, 128 * g:128 * (g + 1)]
                chains.append(dict(g=g, rows=rows, mask=mask, vg=vv[:, 128 * g:128 * (g + 1)],
                                   s=lax.dot_general(_stack_heads(qt, g), kg, _NT, preferred_element_type=F32)))
        for ch in chains:
            ch["prob"], _ = _softmax(ch.pop("s"), _sink_col(sinks_ref, ch["g"]), ch["mask"])
        for ch in chains:
            o = jnp.dot(ch["prob"].astype(BF16), ch["vg"], preferred_element_type=F32)
            for pp in range(2):
                lanes = slice(256 * ch["g"] + 128 * pp, 256 * ch["g"] + 128 * (pp + 1))
                attn_ref[ch["rows"], lanes] = _unstack_pair(o, pp)

        ga = rest_ref[:, 0:512]
        mix_ref[:, 0:ATTN_W] = (attn_ref[...] * (ga * _sigmoid(ga))).astype(BF16)
        _, _, _, cv = _conv_parts(rest_ref, restp_ref, cw_ref, i == 0)
        gc = rest_ref[:, 2048:2560]
        mix_ref[:, ATTN_W:] = (rest_ref[:, 512:1024] * cv * (gc * _sigmoid(gc))).astype(BF16)

        h2 = x_ref[...] + jnp.dot(mix_ref[...], wo_ref[...], preferred_element_type=F32)
        r2 = lax.rsqrt(jnp.mean(h2 * h2, axis=-1, keepdims=True) + EPS)
        n2 = h2 * r2
        err = n2 * fg_ref[...] - tgt_ref[...]
        dy = err * (1.0 / D_MODEL)
        small_ref[6:7, :] += jnp.sum(err * err, axis=0, keepdims=True) * (0.5 / D_MODEL)
        small_ref[1:2, :] += jnp.sum(dy * n2, axis=0, keepdims=True)
        dn = dy * fg_ref[...]
        dh2 = r2 * (dn - n2 * jnp.mean(dn * n2, axis=-1, keepdims=True))
        dh2_ref[...] = dh2
        pmix_ref[...] = mix_ref[...]
        pdh2_ref[...] = dh2.astype(BF16)

        @pl.when(i == nt - 1)
        def _():
            gwo_ref[...] += lax.dot_general(pmix_ref[...], pdh2_ref[...], _TN, preferred_element_type=F32)
            gwob_ref[...] = gwo_ref[...].astype(BF16)

    tile = lambda w: pl.BlockSpec((T_FMIX, w), lambda i: (i, 0))
    whole = lambda r, w: pl.BlockSpec((r, w), lambda i: (0, 0))
    prev_blk = pl.BlockSpec((BLOCK, 2 * KV_W), lambda i: (jnp.maximum(i * nsub - 1, 0), 0))
    prev8 = pl.BlockSpec((8, REST_W), lambda i: (jnp.maximum(i * (T_FMIX // 8) - 1, 0), 0))
    return pl.pallas_call(
        body, name="fwd_mix", grid=(nt,),
        out_shape=(jax.ShapeDtypeStruct((seq, ATTN_W), F32), jax.ShapeDtypeStruct((seq, D_MODEL), F32),
                   jax.ShapeDtypeStruct((D_MODEL, D_MODEL), F32), jax.ShapeDtypeStruct((D_MODEL, D_MODEL), BF16),
                   jax.ShapeDtypeStruct((SMALL_ROWS, D_MODEL), F32)),
        in_specs=[pl.BlockSpec(memory_space=pltpu.SMEM), tile(D_MODEL), tile(ATTN_W), tile(2 * KV_W), tile(2 * KV_W),
                  prev_blk, prev_blk, tile(REST_W), prev8, whole(8, CONV_W), whole(D_MODEL, D_MODEL),
                  whole(1, D_MODEL), tile(D_MODEL)],
        out_specs=(tile(ATTN_W), tile(D_MODEL), whole(D_MODEL, D_MODEL), whole(D_MODEL, D_MODEL),
                   whole(SMALL_ROWS, D_MODEL)),
        scratch_shapes=[pltpu.VMEM((T_FMIX, D_MODEL), BF16)] * 3,
        compiler_params=_params(("arbitrary",)),
    )(sinks, x, q, kd, vd, kd, vd, rest, rest, conv_w, w_out, final_g, target)


def _bwd_mix(dh2, q, kd, vd, attn, rest, sinks, conv_w, w_out, rope_c, rope_a, rope_b):
    seq = dh2.shape[0]
    nt = seq // T_MIX
    nsub = T_MIX // BLOCK

    def body(sinks_ref, dh2_ref, q_ref, kd_ref, vd_ref, kdp_ref, vdp_ref, attn_ref, rest_ref, restp_ref,
             cw_ref, wo_ref, c_ref, a_ref, b_ref,
             dq_ref, dk_ref, dv_ref, dkh_ref, dvh_ref, dga_ref, db_ref, dgc_ref, dcv_ref, small_ref,
             dmix_ref, dsink_ref):
        i = pl.program_id(0)

        @pl.when(i == 0)
        def _():
            small_ref[...] = jnp.zeros_like(small_ref)
            dsink_ref[...] = jnp.zeros_like(dsink_ref)

        dmix_ref[...] = lax.dot_general(dh2_ref[...].astype(BF16), wo_ref[...], _NT, preferred_element_type=F32)

        ga = rest_ref[:, 0:512]
        sg = _sigmoid(ga)
        dma = dmix_ref[:, 0:ATTN_W]
        dga_ref[...] = (dma * attn_ref[...] * (sg * (1.0 + ga * (1.0 - sg)))).astype(BF16)
        dmix_ref[:, 0:ATTN_W] = dma * (ga * sg)

        u, um1, um2, cv = _conv_parts(rest_ref, restp_ref, cw_ref, i == 0)
        gc = rest_ref[:, 2048:2560]
        sc = _sigmoid(gc)
        bg = rest_ref[:, 512:1024]
        dmc = dmix_ref[:, ATTN_W:]
        t1 = dmc * (gc * sc)
        db_ref[...] = (t1 * cv).astype(BF16)
        dcv = t1 * bg
        dcv_ref[...] = dcv
        dgc_ref[...] = (dmc * (bg * cv) * (sc * (1.0 + gc * (1.0 - sc)))).astype(BF16)
        small_ref[2:3, 0:CONV_W] += jnp.sum(dcv * um2, axis=0, keepdims=True)
        small_ref[3:4, 0:CONV_W] += jnp.sum(dcv * um1, axis=0, keepdims=True)
        small_ref[4:5, 0:CONV_W] += jnp.sum(dcv * u, axis=0, keepdims=True)

        lo = _lane_lo((2 * BLOCK, 128))
        dk_blocks = [None] * (nsub + 1)
        dv_blocks = [None] * (nsub + 1)

        def add(lst, n, val):
            lst[n] = val if lst[n] is None else lst[n] + val

        chains = []
        mask_first, mask_rest = _band_masks(i > 0)
        for sb in range(nsub):
            rows = slice(BLOCK * sb, BLOCK * (sb + 1))
            if sb == 0:
                kk = jnp.concatenate([kdp_ref[...], kd_ref[rows, :]], axis=0)
                vv = jnp.concatenate([vdp_ref[...], vd_ref[rows, :]], axis=0)
                mask = mask_first
            else:
                both = slice(BLOCK * (sb - 1), BLOCK * (sb + 1))
                kk, vv = kd_ref[both, :], vd_ref[both, :]
                mask = mask_rest
            qt = q_ref[rows, :]
            dot = dmix_ref[rows, 0:ATTN_W].astype(BF16)
            for g in range(2):
                chains.append(dict(sb=sb, g=g, rows=rows, mask=mask, qs=_stack_heads(qt, g), dos=_stack_heads(dot, g),
                                   kg=kk[:, 128 * g:128 * (g + 1)], vg=vv[:, 128 * g:128 * (g + 1)]))
        for ch in chains:
            ch["prob"], ch["psink"] = _probs(ch["qs"], ch["kg"], _sink_col(sinks_ref, ch["g"]), ch["mask"])
        for ch in chains:
            ch["dp"] = lax.dot_general(ch["dos"], ch["vg"], _NT, preferred_element_type=F32)
        for ch in chains:
            rs = jnp.sum(ch["prob"] * ch["dp"], axis=-1, keepdims=True)
            ch["ds"] = (ch["prob"] * (ch["dp"] - rs)).astype(BF16)
            dsink_ref[ch["g"]] += -ch["psink"] * rs
        for ch in chains:
            dqs = jnp.dot(ch["ds"], ch["kg"], preferred_element_type=F32) * SCALE
            c, a, b = c_ref[ch["rows"], :], a_ref[ch["rows"], :], b_ref[ch["rows"], :]
            for pp in range(2):
                lanes = slice(256 * ch["g"] + 128 * pp, 256 * ch["g"] + 128 * (pp + 1))
                dq_ref[ch["rows"], lanes] = _rope(_unstack_pair(dqs, pp), c, -a, -b).astype(BF16)
            dkd = lax.dot_general(ch["ds"], ch["qs"], _TN, preferred_element_type=F32)
            dvd = lax.dot_general(ch["prob"].astype(BF16), ch["dos"], _TN, preferred_element_type=F32)
            ch["dk"] = dkd + pltpu.roll(dkd, HEAD_DIM, 1)
            ch["dv"] = dvd + pltpu.roll(dvd, HEAD_DIM, 1)
        for sb in range(nsub):
            dk2 = jnp.where(lo, chains[2 * sb]["dk"], chains[2 * sb + 1]["dk"])
            dv2 = jnp.where(lo, chains[2 * sb]["dv"], chains[2 * sb + 1]["dv"])
            add(dk_blocks, sb, dk2[0:BLOCK])
            add(dk_blocks, sb + 1, dk2[BLOCK:])
            add(dv_blocks, sb, dv2[0:BLOCK])
            add(dv_blocks, sb + 1, dv2[BLOCK:])
        dkh_ref[0] = dk_blocks[0]
        dvh_ref[0] = dv_blocks[0]
        for sb in range(nsub):
            dk_ref[BLOCK * sb:BLOCK * (sb + 1), :] = dk_blocks[sb + 1]
            dv_ref[BLOCK * sb:BLOCK * (sb + 1), :] = dv_blocks[sb + 1]

        @pl.when(i == nt - 1)
        def _():
            for h in range(8):
                tot = jnp.sum(dsink_ref[h // 4, BLOCK * (h % 4):BLOCK * (h % 4 + 1), :], axis=0, keepdims=True)
                small_ref[5:6, h:h + 1] = tot

    tile = lambda w: pl.BlockSpec((T_MIX, w), lambda i: (i, 0))
    whole = lambda r, w: pl.BlockSpec((r, w), lambda i: (0, 0))
    prev_blk = pl.BlockSpec((BLOCK, 2 * KV_W), lambda i: (jnp.maximum(i * nsub - 1, 0), 0))
    prev8 = pl.BlockSpec((8, REST_W), lambda i: (jnp.maximum(i * (T_MIX // 8) - 1, 0), 0))
    halo = pl.BlockSpec((1, BLOCK, KV_W), lambda i: (i, 0, 0))
    bf = lambda w: jax.ShapeDtypeStruct((seq, w), BF16)
    f32 = lambda w: jax.ShapeDtypeStruct((seq, w), F32)
    return pl.pallas_call(
        body, name="bwd_mix", grid=(nt,),
        out_shape=(bf(ATTN_W), f32(KV_W), f32(KV_W), jax.ShapeDtypeStruct((nt, BLOCK, KV_W), F32),
                   jax.ShapeDtypeStruct((nt, BLOCK, KV_W), F32), bf(ATTN_W), bf(CONV_W), bf(CONV_W), f32(CONV_W),
                   jax.ShapeDtypeStruct((SMALL_ROWS, D_MODEL), F32)),
        in_specs=[pl.BlockSpec(memory_space=pltpu.SMEM), tile(D_MODEL), tile(ATTN_W), tile(2 * KV_W), tile(2 * KV_W),
                  prev_blk, prev_blk, tile(ATTN_W), tile(REST_W), prev8, whole(8, CONV_W),
                  whole(D_MODEL, D_MODEL), tile(128), tile(128), tile(128)],
        out_specs=(tile(ATTN_W), tile(KV_W), tile(KV_W), halo, halo, tile(ATTN_W), tile(CONV_W), tile(CONV_W),
                   tile(CONV_W), whole(SMALL_ROWS, D_MODEL)),
        scratch_shapes=[pltpu.VMEM((T_MIX, D_MODEL), F32), pltpu.VMEM((2, 4 * BLOCK, 1), F32)],
        compiler_params=_params(("arbitrary",)),
    )(sinks, dh2, q, kd, vd, kd, vd, attn, rest, rest, conv_w, w_out, rope_c, rope_a, rope_b)


def _bwd_dw(x, norm_g, dq, dk, dv, dkh, dvh, dga, db, dgc, dcv, rest, conv_w, rope_c, rope_a, rope_b):
    seq = x.shape[0]
    tb = T_PROJ
    per = tb // T_MIX
    nt = seq // tb
    stage_rows = 256

    def body(x_ref, g_ref, dq_ref, dk_ref, dv_ref, dkh_ref, dvh_ref, dkn_ref, dvn_ref, dga_ref, db_ref,
             dgc_ref, dcv_ref, dcvn_ref, ch_ref, cw_ref, c_ref, a_ref, b_ref, dp_ref, gw_hbm, gwb_hbm,
             acc_ref, stage_ref):
        i = pl.program_id(0)

        @pl.when(i == 0)
        def _():
            acc_ref[...] = jnp.zeros_like(acc_ref)

        last = i == nt - 1
        keep = jnp.where(last, 0.0, 1.0)
        pad = jnp.zeros((T_MIX - BLOCK, KV_W), F32)

        def with_halos(main_ref, halo_ref, next_ref):
            parts = []
            for m in range(1, per + 1):
                parts += [pad, halo_ref[m] if m < per else next_ref[0] * keep]
            return main_ref[...] + jnp.concatenate(parts, axis=0)

        dk = with_halos(dk_ref, dkh_ref, dkn_ref)
        dv = with_halos(dv_ref, dvh_ref, dvn_ref)
        dp_ref[:, 0:ATTN_W] = dq_ref[...]
        dp_ref[:, ATTN_W:ATTN_W + KV_W] = _rope(dk, c_ref[...], -a_ref[...], -b_ref[...]).astype(BF16)
        dp_ref[:, ATTN_W + KV_W:ATTN_W + 2 * KV_W] = dv.astype(BF16)
        base = ATTN_W + 2 * KV_W
        dp_ref[:, base:base + 512] = dga_ref[...]
        dp_ref[:, base + 512:base + 1024] = db_ref[...]
        dcv = dcv_ref[...]
        nxt = dcvn_ref[...] * keep
        du = cw_ref[2:3, :] * dcv + cw_ref[1:2, :] * _shift_up(dcv, nxt, 1) + cw_ref[0:1, :] * _shift_up(dcv, nxt, 2)
        dp_ref[:, base + 1024:base + 1536] = (du * ch_ref[:, 512:1024]).astype(BF16)
        dp_ref[:, base + 1536:base + 2048] = (du * ch_ref[:, 0:512]).astype(BF16)
        dp_ref[:, base + 2048:base + 2560] = dgc_ref[...]

        xf = x_ref[...]
        r1 = lax.rsqrt(jnp.mean(xf * xf, axis=-1, keepdims=True) + EPS)
        xn = (xf * r1 * g_ref[...]).astype(BF16)
        for n in range(IN_W // 256):
            cols = slice(256 * n, 256 * (n + 1))
            acc_ref[cols, :] += lax.dot_general(dp_ref[:, cols], xn, _TN, preferred_element_type=F32)

        @pl.when(last)
        def _():
            pltpu.sync_copy(acc_ref, gw_hbm)
            for n in range(IN_W // stage_rows):
                rows = slice(stage_rows * n, stage_rows * (n + 1))
                stage_ref[...] = acc_ref[rows, :].astype(BF16)
                pltpu.sync_copy(stage_ref, gwb_hbm.at[rows])

    tile = lambda w: pl.BlockSpec((tb, w), lambda i: (i, 0))
    whole = lambda r, w: pl.BlockSpec((r, w), lambda i: (0, 0))
    halo = pl.BlockSpec((per, BLOCK, KV_W), lambda i: (i, 0, 0))
    halo_next = pl.BlockSpec((1, BLOCK, KV_W), lambda i: (jnp.minimum((i + 1) * per, seq // T_MIX - 1), 0, 0))
    next8 = pl.BlockSpec((8, CONV_W), lambda i: (jnp.minimum((i + 1) * (tb // 8), seq // 8 - 1), 0))
    ch = pl.BlockSpec((tb, 1024), lambda i: (i, 1))
    hbm = pl.BlockSpec(memory_space=pl.ANY)
    return pl.pallas_call(
        body, name="bwd_dw", grid=(nt,),
        out_shape=(jax.ShapeDtypeStruct((seq, IN_W), BF16), jax.ShapeDtypeStruct((IN_W, D_MODEL), F32),
                   jax.ShapeDtypeStruct((IN_W, D_MODEL), BF16)),
        in_specs=[tile(D_MODEL), whole(1, D_MODEL), tile(ATTN_W), tile(KV_W), tile(KV_W), halo, halo,
                  halo_next, halo_next,
                  tile(ATTN_W), tile(CONV_W), tile(CONV_W), tile(CONV_W), next8, ch, whole(8, CONV_W),
                  tile(128), tile(128), tile(128)],
        out_specs=(tile(IN_W), hbm, hbm),
        scratch_shapes=[pltpu.VMEM((IN_W, D_MODEL), F32), pltpu.VMEM((stage_rows, D_MODEL), BF16)],
        compiler_params=_params(("arbitrary",)),
    )(x, norm_g, dq, dk, dv, dkh, dvh, dkh, dvh, dga, db, dgc, dcv, dcv, rest, conv_w, rope_c, rope_a, rope_b)


def _bwd_dx(x, norm_g, dh2, dproj, w_in_t, g_in, g_in_b, g_out, g_out_b, small):
    seq = x.shape[0]
    tb = T_PROJ
    nt = seq // tb
    hi, ho = W_IN_BLK // 2, W_OUT_BLK // 2

    def body(x_ref, g_ref, dh2_ref, dp_ref, w_ref, gi_hbm, gib_hbm, go_hbm, gob_hbm, small_ref,
             gx_ref, gi_out, go_out, small_out,
             mine_i, mine_o, land_i, land_o, small_in, small_mine, gng_ref, send_sems, recv_sems, local_sems):
        i = pl.program_id(0)
        x_, y_, c_ = lax.axis_index("x"), lax.axis_index("y"), lax.axis_index("c")
        me = 2 * x_ + y_
        my_dev = 4 * x_ + 2 * y_ + c_
        sibling = (x_, y_, 1 - c_)

        def remote(k, src, dst, to):
            return pltpu.make_async_remote_copy(src_ref=src, dst_ref=dst, send_sem=send_sems.at[k],
                                                recv_sem=recv_sems.at[k], device_id=to, device_id_type=MESH)

        def exchange():
            cps = []
            for f in range(1, 8):
                fx, fy, fc = f >> 2, (f >> 1) & 1, f & 1
                to = (x_ ^ fx, y_ ^ fy, c_ ^ fc)
                j, h = 2 * to[0] + to[1], to[2]
                cps += [remote(2 * (f - 1), gib_hbm.at[j, pl.ds(h * hi, hi)], land_i.at[f - 1], to),
                        remote(2 * (f - 1) + 1, gob_hbm.at[j, pl.ds(h * ho, ho)], land_o.at[f - 1], to)]
            own = [pltpu.make_async_copy(gi_hbm.at[me, pl.ds(c_ * hi, hi)], mine_i, local_sems.at[0]),
                   pltpu.make_async_copy(go_hbm.at[me, pl.ds(c_ * ho, ho)], mine_o, local_sems.at[1])]
            return cps, own

        @pl.when(i == 0)
        def _():
            gng_ref[...] = jnp.zeros_like(gng_ref)
            cps, own = exchange()
            for cp in cps + own:
                cp.start()

        xf = x_ref[...]
        r1 = lax.rsqrt(jnp.mean(xf * xf, axis=-1, keepdims=True) + EPS)
        n1 = xf * r1
        dxn = jnp.dot(dp_ref[...], w_ref[...], preferred_element_type=F32)
        gng_ref[...] += jnp.sum(dxn * n1, axis=0, keepdims=True)
        dxg = dxn * g_ref[...]
        gx_ref[...] = r1 * (dxg - n1 * jnp.mean(dxg * n1, axis=-1, keepdims=True)) + dh2_ref[...]

        @pl.when(i == nt - 1)
        def _():
            cps, own = exchange()
            for cp in own:
                cp.wait()
            tot_i, tot_o = mine_i[...], mine_o[...]
            for f in range(1, 8):
                cps[2 * (f - 1)].wait_recv()
                cps[2 * (f - 1) + 1].wait_recv()
                tot_i = tot_i + land_i[f - 1].astype(F32)
                tot_o = tot_o + land_o[f - 1].astype(F32)
            mine_i[...] = tot_i
            mine_o[...] = tot_o
            keep = [pltpu.make_async_copy(mine_i, gi_out.at[pl.ds(c_ * hi, hi)], local_sems.at[0]),
                    pltpu.make_async_copy(mine_o, go_out.at[pl.ds(c_ * ho, ho)], local_sems.at[1])]
            swap = [remote(21, mine_i, gi_out.at[pl.ds(c_ * hi, hi)], sibling),
                    remote(22, mine_o, go_out.at[pl.ds(c_ * ho, ho)], sibling)]
            for cp in keep + swap:
                cp.start()

            row = lax.broadcasted_iota(jnp.int32, (SMALL_ROWS, D_MODEL), 0)
            small_mine[...] = small_ref[...] + jnp.where(row == 0, gng_ref[...], 0.0)
            sm = [remote(14 + f - 1, small_mine, small_in.at[f - 1], (x_ ^ (f >> 2), y_ ^ ((f >> 1) & 1), c_ ^ (f & 1)))
                  for f in range(1, 8)]
            for cp in sm:
                cp.start()
            for cp in sm:
                cp.wait_recv()
            total = jnp.zeros((SMALL_ROWS, D_MODEL), F32)
            for d in range(8):
                slot = jnp.maximum((d ^ my_dev) - 1, 0)
                total = total + jnp.where(d == my_dev, small_mine[...], small_in[slot])
            small_out[...] = total
            for cp in sm:
                cp.wait_send()

            other = [remote(21, mine_i, gi_out.at[pl.ds((1 - c_) * hi, hi)], sibling),
                     remote(22, mine_o, go_out.at[pl.ds((1 - c_) * ho, ho)], sibling)]
            for cp in other:
                cp.wait_recv()
            for cp in keep:
                cp.wait()
            for cp in cps + swap:
                cp.wait_send()

    tile = lambda w: pl.BlockSpec((tb, w), lambda i: (i, 0))
    whole = lambda r, w: pl.BlockSpec((r, w), lambda i: (0, 0))
    hbm = pl.BlockSpec(memory_space=pl.ANY)
    vmem = pl.BlockSpec(memory_space=pltpu.VMEM)
    return pl.pallas_call(
        body, name="bwd_dx", grid=(nt,),
        out_shape=(jax.ShapeDtypeStruct((seq, D_MODEL), F32), jax.ShapeDtypeStruct((W_IN_BLK, D_MODEL), F32),
                   jax.ShapeDtypeStruct((W_OUT_BLK, D_MODEL), F32), jax.ShapeDtypeStruct((SMALL_ROWS, D_MODEL), F32)),
        in_specs=[tile(D_MODEL), whole(1, D_MODEL), tile(D_MODEL), tile(IN_W),
                  pl.BlockSpec((IN_W, D_MODEL), lambda i: (0, 0), pipeline_mode=pl.Buffered(1)),
                  hbm, hbm, hbm, hbm, vmem],
        out_specs=(tile(D_MODEL), hbm, hbm, vmem),
        scratch_shapes=[pltpu.VMEM((hi, D_MODEL), F32), pltpu.VMEM((ho, D_MODEL), F32),
                        pltpu.VMEM((7, hi, D_MODEL), BF16), pltpu.VMEM((7, ho, D_MODEL), BF16),
                        pltpu.VMEM((7, SMALL_ROWS, D_MODEL), F32), pltpu.VMEM((SMALL_ROWS, D_MODEL), F32),
                        pltpu.VMEM((1, D_MODEL), F32),
                        pltpu.SemaphoreType.DMA((23,)), pltpu.SemaphoreType.DMA((23,)),
                        pltpu.SemaphoreType.DMA((2,))],
        compiler_params=_params(("arbitrary",)),
    )(x, norm_g, dh2, dproj, w_in_t, g_in, g_in_b, g_out, g_out_b, small)


def _adamw(w, g, m, v, name):
    rows, cols = w.shape
    tr = max(t for t in range(8, min(rows, 256) + 1, 8) if rows % t == 0)

    def body(w_ref, g_ref, m_ref, v_ref, d_ref, nm_ref, nv_ref):
        gg = g_ref[...]
        m2 = ADAM_B1 * m_ref[...] + (1.0 - ADAM_B1) * gg
        v2 = ADAM_B2 * v_ref[...] + (1.0 - ADAM_B2) * jnp.square(gg)
        m_hat = m2 / (1.0 - ADAM_B1 ** ADAM_STEP)
        v_hat = v2 / (1.0 - ADAM_B2 ** ADAM_STEP)
        d_ref[...] = -ADAM_LR * (m_hat / (jnp.sqrt(v_hat) + ADAM_EPS) + ADAM_WD * w_ref[...])
        nm_ref[...] = m2
        nv_ref[...] = v2

    spec = pl.BlockSpec((tr, cols), lambda i: (i, 0))
    shp = jax.ShapeDtypeStruct((rows, cols), F32)
    return pl.pallas_call(
        body, name=name, grid=(rows // tr,), out_shape=(shp, shp, shp),
        in_specs=[spec] * 4, out_specs=(spec,) * 3, compiler_params=_params(("arbitrary",)),
    )(w, g, m, v)


def _pack_small(norm_g, final_g, conv_w, sinks):
    out = jnp.zeros((SMALL_ROWS, D_MODEL), F32)
    out = out.at[0].set(norm_g).at[1].set(final_g)
    out = out.at[2:5, 0:128].set(conv_w).at[5, 0:8].set(sinks)
    return out


def kernel(x, norm_g, w_in, sinks, conv_w, w_out, final_g, loss_target, m_norm_g, m_w_in, m_sinks, m_conv_w, m_w_out, m_final_g, v_norm_g, v_w_in, v_sinks, v_conv_w, v_w_out, v_final_g):
    seq = x.shape[1]
    x2 = x.reshape(seq, D_MODEL)
    tgt = loss_target.reshape(seq, D_MODEL)
    ng = norm_g.reshape(1, D_MODEL)
    fg = final_g.reshape(1, D_MODEL)
    chip = 2 * lax.axis_index("x") + lax.axis_index("y")

    conv_w8 = jnp.zeros((8, 128), F32).at[0:3].set(conv_w)
    wi_all, wo_all, cw_all = _gather_weights(w_in.T, w_out, conv_w8)
    w_in_full = wi_all.reshape(IN_W, D_MODEL)
    w_out_full = wo_all.reshape(D_MODEL, D_MODEL)
    conv_full = jnp.concatenate([cw_all[j] for j in range(N_CHIPS)], axis=1)
    rope_c, rope_a, rope_b = _rope_tables(seq)

    q, kd, vd, rest = _fwd_proj(x2, ng, w_in_full, rope_c, rope_a, rope_b)
    attn, dh2, g_wo, g_wo_b, small_f = _fwd_mix(x2, q, kd, vd, rest, sinks, conv_full, w_out_full, fg, tgt)
    dq, dk, dv, dkh, dvh, dga, db, dgc, dcv, small_m = _bwd_mix(
        dh2, q, kd, vd, attn, rest, sinks, conv_full, w_out_full, rope_c, rope_a, rope_b)
    dproj, g_wi, g_wi_b = _bwd_dw(x2, ng, dq, dk, dv, dkh, dvh, dga, db, dgc, dcv, rest, conv_full,
                                  rope_c, rope_a, rope_b)
    in_blocks = lambda t: t.reshape(N_CHIPS, W_IN_BLK, D_MODEL)
    out_blocks = lambda t: t.reshape(N_CHIPS, W_OUT_BLK, D_MODEL)
    grad_x, grad_w_in_t, grad_w_out, small = _bwd_dx(
        x2, ng, dh2, dproj, w_in_full, in_blocks(g_wi), in_blocks(g_wi_b), out_blocks(g_wo), out_blocks(g_wo_b),
        small_f + small_m)
    grad_w_in = grad_w_in_t.T

    loss = jnp.sum(small[6])
    grad_norm_g, grad_final_g = small[0], small[1]
    grad_conv_full = small[2:5, 0:CONV_W]
    grad_conv_w = lax.dynamic_slice(grad_conv_full, (0, 128 * chip), (3, 128))
    grad_sinks = small[5, 0:8]

    d_wi, nm_wi, nv_wi = (t.T for t in _adamw(w_in.T, grad_w_in_t, m_w_in.T, v_w_in.T, "adamw_w_in"))
    d_wo, nm_wo, nv_wo = _adamw(w_out, grad_w_out, m_w_out, v_w_out, "adamw_w_out")
    d_s, nm_s, nv_s = _adamw(_pack_small(norm_g, final_g, conv_w, sinks),
                             _pack_small(grad_norm_g, grad_final_g, grad_conv_w, grad_sinks),
                             _pack_small(m_norm_g, m_final_g, m_conv_w, m_sinks),
                             _pack_small(v_norm_g, v_final_g, v_conv_w, v_sinks), "adamw_small")

    def unpack(p):
        return p[0], p[2:5, 0:128], p[5, 0:8], p[1]

    out = [loss, grad_x.reshape(1, seq, D_MODEL), grad_norm_g, grad_w_in, grad_sinks, grad_conv_w, grad_w_out,
           grad_final_g]
    for p_small, p_wi, p_wo in ((d_s, d_wi, d_wo), (nm_s, nm_wi, nm_wo), (nv_s, nv_wi, nv_wo)):
        n_g, c_w, s_k, f_g = unpack(p_small)
        out += [n_g, p_wi, s_k, c_w, p_wo, f_g]
    return tuple(out)
```

```python
import jax
import jax.numpy as jnp
from jax import lax
from jax.experimental import pallas as pl
from jax.experimental.pallas import tpu as pltpu

F32 = jnp.float32
BF16 = jnp.bfloat16

D_MODEL = 1024
HEAD_DIM = 64
ATTN_W = 512
KV_W = 128
CONV_W = 512
IN_W = 3328
REST_W = IN_W - ATTN_W - 2 * KV_W
BLOCK = 128
ROT_DIM = 16
ROPE_THETA = 500000.0
EPS = 1e-5
SCALE = 0.125
NEG = -1e30

N_CHIPS = 4
W_IN_BLK = IN_W // N_CHIPS
W_OUT_BLK = D_MODEL // N_CHIPS

ADAM_LR = 0.001
ADAM_B1 = 0.9
ADAM_B2 = 0.999
ADAM_EPS = 1e-08
ADAM_WD = 0.01
ADAM_STEP = 10

VMEM_LIMIT = 56 * 1024 * 1024
T_PROJ = 512
T_FMIX = 512
T_MIX = 512
SMALL_ROWS = 8
MESH = pl.DeviceIdType.MESH

_NT = (((1,), (1,)), ((), ()))
_TN = (((0,), (0,)), ((), ()))


def _params(sem=None):
    kw = dict(vmem_limit_bytes=VMEM_LIMIT)
    if sem is not None:
        kw["dimension_semantics"] = sem
    return pltpu.CompilerParams(**kw)


def _sigmoid(t):
    return 1.0 / (1.0 + jnp.exp(-t))


def _shift_down(t, prev8, k):
    rolled = pltpu.roll(t, k, 0)
    row = lax.broadcasted_iota(jnp.int32, t.shape, 0)
    for j in range(k):
        rolled = jnp.where(row == j, prev8[8 - k + j:8 - k + j + 1, :], rolled)
    return rolled


def _shift_up(t, next8, k):
    n = t.shape[0]
    rolled = pltpu.roll(t, n - k, 0)
    row = lax.broadcasted_iota(jnp.int32, t.shape, 0)
    for j in range(k):
        rolled = jnp.where(row == n - k + j, next8[j:j + 1, :], rolled)
    return rolled


def _rope(t, c, a, b):
    w = t.shape[1]
    reps = w // 128
    if reps > 1:
        c, a, b = (jnp.concatenate([z] * reps, axis=1) for z in (c, a, b))
    return t * c + pltpu.roll(t, w - 8, 1) * a + pltpu.roll(t, 8, 1) * b


def _lane_lo(shape):
    return lax.broadcasted_iota(jnp.int32, shape, 1) < HEAD_DIM


def _stack_heads(t, g):
    lo = _lane_lo((BLOCK, 128))
    parts = []
    for hh in range(4):
        pair = t[:, 256 * g + 128 * (hh // 2):256 * g + 128 * (hh // 2) + 128]
        keep = lo if hh % 2 == 0 else jnp.logical_not(lo)
        parts.append(jnp.where(keep, pair, jnp.zeros_like(pair)))
    return jnp.concatenate(parts, axis=0)


def _unstack_pair(o, pp):
    lo = _lane_lo((BLOCK, 128))
    return jnp.where(lo, o[256 * pp:256 * pp + 128], o[256 * pp + 128:256 * pp + 256])


def _band_masks(has_prev):
    r = lax.broadcasted_iota(jnp.int32, (4 * BLOCK, 2 * BLOCK), 0) % BLOCK
    kj = lax.broadcasted_iota(jnp.int32, (4 * BLOCK, 2 * BLOCK), 1)
    cur = (kj >= BLOCK) & (kj - BLOCK <= r)
    prev = (kj < BLOCK) & (kj > r)
    return cur | (prev & has_prev), cur | prev


def _sink_col(sinks_ref, g):
    r = lax.broadcasted_iota(jnp.int32, (4 * BLOCK, 1), 0) // BLOCK
    col = jnp.full((4 * BLOCK, 1), sinks_ref[4 * g + 3], F32)
    for hh in range(3):
        col = jnp.where(r == hh, sinks_ref[4 * g + hh], col)
    return col


def _probs(qs, kd, sink_col, mask):
    return _softmax(lax.dot_general(qs, kd, _NT, preferred_element_type=F32), sink_col, mask)


def _softmax(s, sink_col, mask):
    s = jnp.where(mask, s, NEG)
    m = jnp.maximum(jnp.max(s, axis=-1, keepdims=True), sink_col)
    p = jnp.exp(s - m)
    es = jnp.exp(sink_col - m)
    inv = 1.0 / (jnp.sum(p, axis=-1, keepdims=True) + es)
    return p * inv, es * inv


def _key_windows(i, nsub, kd_ref, vd_ref, kdp_ref, vdp_ref):
    mask_first, mask_rest = _band_masks(i > 0)
    out = []
    for sb in range(nsub):
        rows = slice(BLOCK * sb, BLOCK * (sb + 1))
        if sb == 0:
            kk = jnp.concatenate([kdp_ref[...], kd_ref[rows, :]], axis=0)
            vv = jnp.concatenate([vdp_ref[...], vd_ref[rows, :]], axis=0)
            out.append((rows, kk, vv, mask_first))
        else:
            both = slice(BLOCK * (sb - 1), BLOCK * (sb + 1))
            out.append((rows, kd_ref[both, :], vd_ref[both, :], mask_rest))
    return out


def _gather_weights(w_in_t, w_out, conv_w8):
    hi, ho = W_IN_BLK // 2, W_OUT_BLK // 2

    def body(wi_ref, wo_ref, cw_ref, wi_all, wo_all, cw_all, send_sems, recv_sems):
        x, y, c = lax.axis_index("x"), lax.axis_index("y"), lax.axis_index("c")
        me = 2 * x + y
        sibling = (x, y, 1 - c)
        chips = [(1 - x, y), (x, 1 - y), (1 - x, 1 - y)]

        wi_all[me] = wi_ref[...].astype(BF16)
        wo_all[me] = wo_ref[...].astype(BF16)
        cw_all[me] = cw_ref[...]

        def copies(k, chip, half, to):
            j = 2 * chip[0] + chip[1]
            refs = (wi_all.at[j, pl.ds(half * hi, hi)], wo_all.at[j, pl.ds(half * ho, ho)])
            return [pltpu.make_async_remote_copy(src_ref=r, dst_ref=r, send_sem=send_sems.at[2 * k + n],
                                                 recv_sem=recv_sems.at[2 * k + n], device_id=to,
                                                 device_id_type=MESH) for n, r in enumerate(refs)]

        def conv_copy(k, chip, to):
            r = cw_all.at[2 * chip[0] + chip[1]]
            return pltpu.make_async_remote_copy(src_ref=r, dst_ref=r, send_sem=send_sems.at[12 + k],
                                                recv_sem=recv_sems.at[12 + k], device_id=to, device_id_type=MESH)

        first = [cp for k, chip in enumerate(chips) for cp in copies(k, (x, y), c, (*chip, c))]
        first += [conv_copy(k, (x, y), (*chip, c)) for k, chip in enumerate(chips)]
        for cp in first:
            cp.start()
        passed = []
        for k, chip in enumerate(chips):
            for cp in copies(k, chip, c, (x, y, c)):
                cp.wait_recv()
            fwd = copies(3 + k, chip, c, sibling)
            for cp in fwd:
                cp.start()
            passed += fwd
        for k, chip in enumerate(chips):
            for cp in copies(3 + k, chip, 1 - c, (x, y, c)):
                cp.wait_recv()
            conv_copy(k, chip, (x, y, c)).wait_recv()
        for cp in first + passed:
            cp.wait_send()

    vmem = pl.BlockSpec(memory_space=pltpu.VMEM)
    return pl.pallas_call(
        body, name="gather_weights",
        out_shape=(jax.ShapeDtypeStruct((N_CHIPS, W_IN_BLK, D_MODEL), BF16),
                   jax.ShapeDtypeStruct((N_CHIPS, W_OUT_BLK, D_MODEL), BF16),
                   jax.ShapeDtypeStruct((N_CHIPS, 8, 128), F32)),
        in_specs=[vmem, vmem, vmem], out_specs=(vmem, vmem, vmem),
        scratch_shapes=[pltpu.SemaphoreType.DMA((15,)), pltpu.SemaphoreType.DMA((15,))],
        compiler_params=_params(),
    )(w_in_t, w_out, conv_w8)


def _reduce_grads(g_in, g_out, small):
    hi, ho = W_IN_BLK // 2, W_OUT_BLK // 2

    def body(gi_hbm, go_hbm, small_ref, gi_out, go_out, small_out,
             mine_i, mine_o, sib_i, sib_o, out_i, out_o, ici_i, ici_o, small_in, send_sems, recv_sems, local_sems):
        x, y, c = lax.axis_index("x"), lax.axis_index("y"), lax.axis_index("c")
        my_dev = 4 * x + 2 * y + c
        sibling = (x, y, 1 - c)
        chips = [(1 - x, y), (x, 1 - y), (1 - x, 1 - y)]
        order = chips + [(x, y)]

        def remote(k, src, dst, to):
            return pltpu.make_async_remote_copy(src_ref=src, dst_ref=dst, send_sem=send_sems.at[k],
                                                recv_sem=recv_sems.at[k], device_id=to, device_id_type=MESH)

        small_cps = []
        for f in range(1, 8):
            fx, fy, fc = f >> 2, (f >> 1) & 1, f & 1
            small_cps.append(remote(16 + f - 1, small_ref, small_in.at[f - 1], (x ^ fx, y ^ fy, c ^ fc)))
        for cp in small_cps:
            cp.start()

        own, to_sib = [], []
        for n, chip in enumerate(order):
            j = 2 * chip[0] + chip[1]
            own += [pltpu.make_async_copy(gi_hbm.at[j, pl.ds(c * hi, hi)], mine_i.at[n], local_sems.at[2 * n]),
                    pltpu.make_async_copy(go_hbm.at[j, pl.ds(c * ho, ho)], mine_o.at[n], local_sems.at[2 * n + 1])]
            to_sib += [remote(2 * n, gi_hbm.at[j, pl.ds((1 - c) * hi, hi)], sib_i.at[n], sibling),
                       remote(2 * n + 1, go_hbm.at[j, pl.ds((1 - c) * ho, ho)], sib_o.at[n], sibling)]
            for cp in own[-2:] + to_sib[-2:]:
                cp.start()

        ici = []
        for k, chip in enumerate(chips):
            for cp in own[2 * k:2 * k + 2]:
                cp.wait()
            for cp in to_sib[2 * k:2 * k + 2]:
                cp.wait_recv()
            out_i[k] = (mine_i[k] + sib_i[k]).astype(BF16)
            out_o[k] = (mine_o[k] + sib_o[k]).astype(BF16)
            ici += [remote(8 + 2 * k, out_i.at[k], ici_i.at[k], (*chip, c)),
                    remote(9 + 2 * k, out_o.at[k], ici_o.at[k], (*chip, c))]
            ici[-2].start()
            ici[-1].start()
        for cp in own[6:8]:
            cp.wait()
        for cp in to_sib[6:8]:
            cp.wait_recv()
        tot_i = mine_i[3] + sib_i[3]
        tot_o = mine_o[3] + sib_o[3]
        for k in range(3):
            ici[2 * k].wait_recv()
            ici[2 * k + 1].wait_recv()
            tot_i = tot_i + ici_i[k].astype(F32)
            tot_o = tot_o + ici_o[k].astype(F32)
        gi_out[pl.ds(c * hi, hi), :] = tot_i
        go_out[pl.ds(c * ho, ho), :] = tot_o

        swap = [remote(14, gi_out.at[pl.ds(c * hi, hi)], gi_out.at[pl.ds(c * hi, hi)], sibling),
                remote(15, go_out.at[pl.ds(c * ho, ho)], go_out.at[pl.ds(c * ho, ho)], sibling)]
        for cp in swap:
            cp.start()

        for cp in small_cps:
            cp.wait_recv()
        total = jnp.zeros((SMALL_ROWS, D_MODEL), F32)
        for d in range(8):
            slot = jnp.maximum((d ^ my_dev) - 1, 0)
            total = total + jnp.where(d == my_dev, small_ref[...], small_in[slot])
        small_out[...] = total

        recv_swap = [remote(14, gi_out.at[pl.ds((1 - c) * hi, hi)], gi_out.at[pl.ds((1 - c) * hi, hi)], sibling),
                     remote(15, go_out.at[pl.ds((1 - c) * ho, ho)], go_out.at[pl.ds((1 - c) * ho, ho)], sibling)]
        for cp in recv_swap:
            cp.wait_recv()
        for cp in to_sib + ici + swap + small_cps:
            cp.wait_send()

    vmem = pl.BlockSpec(memory_space=pltpu.VMEM)
    anyspace = pl.BlockSpec(memory_space=pl.ANY)
    return pl.pallas_call(
        body, name="reduce_grads",
        out_shape=(jax.ShapeDtypeStruct((W_IN_BLK, D_MODEL), F32),
                   jax.ShapeDtypeStruct((W_OUT_BLK, D_MODEL), F32),
                   jax.ShapeDtypeStruct((SMALL_ROWS, D_MODEL), F32)),
        in_specs=[anyspace, anyspace, vmem], out_specs=(vmem, vmem, vmem),
        scratch_shapes=[pltpu.VMEM((N_CHIPS, hi, D_MODEL), F32), pltpu.VMEM((N_CHIPS, ho, D_MODEL), F32),
                        pltpu.VMEM((N_CHIPS, hi, D_MODEL), F32), pltpu.VMEM((N_CHIPS, ho, D_MODEL), F32),
                        pltpu.VMEM((3, hi, D_MODEL), BF16), pltpu.VMEM((3, ho, D_MODEL), BF16),
                        pltpu.VMEM((3, hi, D_MODEL), BF16), pltpu.VMEM((3, ho, D_MODEL), BF16),
                        pltpu.VMEM((7, SMALL_ROWS, D_MODEL), F32),
                        pltpu.SemaphoreType.DMA((23,)), pltpu.SemaphoreType.DMA((23,)),
                        pltpu.SemaphoreType.DMA((8,))],
        compiler_params=_params(),
    )(g_in, g_out, small)


def _fwd_proj(x, norm_g, w_in_t):
    seq = x.shape[0]
    nt = seq // T_PROJ
    lane = jnp.arange(128, dtype=jnp.int32) % HEAD_DIM
    inv_freq = ROPE_THETA ** (-(2 * (lane % 8)).astype(F32) / ROT_DIM)
    inv_freq = jnp.where(lane < ROT_DIM, inv_freq, 0.0).reshape(1, 128)
    in_tile = jnp.arange(T_PROJ, dtype=jnp.int32).astype(F32)[:, None] * inv_freq
    cos_in, sin_in = jnp.cos(in_tile), jnp.sin(in_tile)
    start = jnp.repeat((jnp.arange(nt, dtype=jnp.int32) * T_PROJ).astype(F32), 8)[:, None] * inv_freq
    cos_st, sin_st = jnp.cos(start), jnp.sin(start)

    def body(x_ref, g_ref, w_ref, cs_ref, ss_ref, ci_ref, si_ref, q_ref, kd_ref, vd_ref, rest_ref, c_ref, a_ref, b_ref):
        xf = x_ref[...]
        r1 = lax.rsqrt(jnp.mean(xf * xf, axis=-1, keepdims=True) + EPS)
        xn = (xf * r1 * g_ref[...]).astype(BF16)
        cs, ss = cs_ref[0:1, :], ss_ref[0:1, :]
        c = cs * ci_ref[...] - ss * si_ref[...]
        sin = ss * ci_ref[...] + cs * si_ref[...]
        j = lax.broadcasted_iota(jnp.int32, (T_PROJ, 128), 1) % HEAD_DIM
        a = jnp.where(j < 8, -sin, 0.0)
        b = jnp.where(j >= 8, sin, 0.0)
        c_ref[...], a_ref[...], b_ref[...] = c, a, b
        proj = lambda lo_c, w: lax.dot_general(xn, w_ref[lo_c:lo_c + w, :], _NT, preferred_element_type=F32)
        q_ref[...] = (_rope(proj(0, ATTN_W), c, a, b) * SCALE).astype(BF16)
        kv = proj(ATTN_W, 2 * KV_W)
        k = _rope(kv[:, 0:KV_W], c, a, b)
        v = kv[:, KV_W:2 * KV_W]
        lo = _lane_lo(k.shape)
        for t, ref in ((k, kd_ref), (v, vd_ref)):
            sw = pltpu.roll(t, HEAD_DIM, 1)
            ref[:, 0:128] = jnp.where(lo, t, sw).astype(BF16)
            ref[:, 128:256] = jnp.where(lo, sw, t).astype(BF16)
        for n in range(REST_W // 512):
            rest_ref[:, 512 * n:512 * (n + 1)] = proj(ATTN_W + 2 * KV_W + 512 * n, 512)

    tile = lambda w: pl.BlockSpec((T_PROJ, w), lambda i: (i, 0))
    whole = lambda r, w: pl.BlockSpec((r, w), lambda i: (0, 0))
    return pl.pallas_call(
        body, name="fwd_proj", grid=(nt,),
        out_shape=(jax.ShapeDtypeStruct((seq, ATTN_W), BF16), jax.ShapeDtypeStruct((seq, 2 * KV_W), BF16),
                   jax.ShapeDtypeStruct((seq, 2 * KV_W), BF16), jax.ShapeDtypeStruct((seq, REST_W), F32))
        + (jax.ShapeDtypeStruct((seq, 128), F32),) * 3,
        in_specs=[tile(D_MODEL), whole(1, D_MODEL), whole(IN_W, D_MODEL), pl.BlockSpec((8, 128), lambda i: (i, 0)),
                  pl.BlockSpec((8, 128), lambda i: (i, 0)), whole(T_PROJ, 128), whole(T_PROJ, 128)],
        out_specs=(tile(ATTN_W), tile(2 * KV_W), tile(2 * KV_W), tile(REST_W), tile(128), tile(128), tile(128)),
        compiler_params=_params(("arbitrary",)),
    )(x, norm_g, w_in_t, cos_st, sin_st, cos_in, sin_in)


def _conv_parts(rest_ref, prev_ref, cw_ref, first):
    u = rest_ref[:, 1024:1536] * rest_ref[:, 1536:2048]
    up = prev_ref[:, 1024:1536] * prev_ref[:, 1536:2048]
    up = jnp.where(first, jnp.zeros_like(up), up)
    um1 = _shift_down(u, up, 1)
    um2 = _shift_down(u, up, 2)
    cv = cw_ref[0:1, :] * um2 + cw_ref[1:2, :] * um1 + cw_ref[2:3, :] * u
    return u, um1, um2, cv


def _fwd_mix(x, q, kd, vd, rest, sinks, conv_w, w_out, final_g, target):
    seq = x.shape[0]
    nt = seq // T_FMIX
    nsub = T_FMIX // BLOCK

    def body(sinks_ref, x_ref, q_ref, kd_ref, vd_ref, kdp_ref, vdp_ref, rest_ref, restp_ref, cw_ref, wo_ref,
             fg_ref, tgt_ref, attn_ref, dh2_ref, gwo_ref, small_ref, mix_ref, pmix_ref, pdh2_ref):
        i = pl.program_id(0)

        @pl.when(i == 0)
        def _():
            small_ref[...] = jnp.zeros_like(small_ref)
            gwo_ref[...] = jnp.zeros_like(gwo_ref)
            pmix_ref[...] = jnp.zeros_like(pmix_ref)
            pdh2_ref[...] = jnp.zeros_like(pdh2_ref)

        gwo_ref[...] += lax.dot_general(pmix_ref[...], pdh2_ref[...], _TN, preferred_element_type=F32)

        chains = []
        for rows, kk, vv, mask in _key_windows(i, nsub, kd_ref, vd_ref, kdp_ref, vdp_ref):
            qt = q_ref[rows, :]
            for g in range(2):
                kg = kk[:, 128 * g:128 * (g + 1)]
                chains.append(dict(g=g, rows=rows, mask=mask, vg=vv[:, 128 * g:128 * (g + 1)],
                                   s=lax.dot_general(_stack_heads(qt, g), kg, _NT, preferred_element_type=F32)))
        for ch in chains:
            ch["prob"], _ = _softmax(ch.pop("s"), _sink_col(sinks_ref, ch["g"]), ch["mask"])
        for ch in chains:
            o = jnp.dot(ch["prob"].astype(BF16), ch["vg"], preferred_element_type=F32)
            for pp in range(2):
                lanes = slice(256 * ch["g"] + 128 * pp, 256 * ch["g"] + 128 * (pp + 1))
                attn_ref[ch["rows"], lanes] = _unstack_pair(o, pp)

        ga = rest_ref[:, 0:512]
        mix_ref[:, 0:ATTN_W] = (attn_ref[...] * (ga * _sigmoid(ga))).astype(BF16)
        _, _, _, cv = _conv_parts(rest_ref, restp_ref, cw_ref, i == 0)
        gc = rest_ref[:, 2048:2560]
        mix_ref[:, ATTN_W:] = (rest_ref[:, 512:1024] * cv * (gc * _sigmoid(gc))).astype(BF16)

        h2 = x_ref[...] + jnp.dot(mix_ref[...], wo_ref[...], preferred_element_type=F32)
        r2 = lax.rsqrt(jnp.mean(h2 * h2, axis=-1, keepdims=True) + EPS)
        n2 = h2 * r2
        err = n2 * fg_ref[...] - tgt_ref[...]
        dy = err * (1.0 / D_MODEL)
        small_ref[6:7, :] += jnp.sum(err * err, axis=0, keepdims=True) * (0.5 / D_MODEL)
        small_ref[1:2, :] += jnp.sum(dy * n2, axis=0, keepdims=True)
        dn = dy * fg_ref[...]
        dh2 = r2 * (dn - n2 * jnp.mean(dn * n2, axis=-1, keepdims=True))
        dh2_ref[...] = dh2
        pmix_ref[...] = mix_ref[...]
        pdh2_ref[...] = dh2.astype(BF16)

        @pl.when(i == nt - 1)
        def _():
            gwo_ref[...] += lax.dot_general(pmix_ref[...], pdh2_ref[...], _TN, preferred_element_type=F32)

    tile = lambda w: pl.BlockSpec((T_FMIX, w), lambda i: (i, 0))
    whole = lambda r, w: pl.BlockSpec((r, w), lambda i: (0, 0))
    prev_blk = pl.BlockSpec((BLOCK, 2 * KV_W), lambda i: (jnp.maximum(i * nsub - 1, 0), 0))
    prev8 = pl.BlockSpec((8, REST_W), lambda i: (jnp.maximum(i * (T_FMIX // 8) - 1, 0), 0))
    return pl.pallas_call(
        body, name="fwd_mix", grid=(nt,),
        out_shape=(jax.ShapeDtypeStruct((seq, ATTN_W), F32), jax.ShapeDtypeStruct((seq, D_MODEL), F32),
                   jax.ShapeDtypeStruct((D_MODEL, D_MODEL), F32), jax.ShapeDtypeStruct((SMALL_ROWS, D_MODEL), F32)),
        in_specs=[pl.BlockSpec(memory_space=pltpu.SMEM), tile(D_MODEL), tile(ATTN_W), tile(2 * KV_W), tile(2 * KV_W),
                  prev_blk, prev_blk, tile(REST_W), prev8, whole(8, CONV_W), whole(D_MODEL, D_MODEL),
                  whole(1, D_MODEL), tile(D_MODEL)],
        out_specs=(tile(ATTN_W), tile(D_MODEL), whole(D_MODEL, D_MODEL), whole(SMALL_ROWS, D_MODEL)),
        scratch_shapes=[pltpu.VMEM((T_FMIX, D_MODEL), BF16)] * 3,
        compiler_params=_params(("arbitrary",)),
    )(sinks, x, q, kd, vd, kd, vd, rest, rest, conv_w, w_out, final_g, target)


def _bwd_mix(dh2, q, kd, vd, attn, rest, sinks, conv_w, w_out, rope_c, rope_a, rope_b):
    seq = dh2.shape[0]
    nt = seq // T_MIX
    nsub = T_MIX // BLOCK

    def body(sinks_ref, dh2_ref, q_ref, kd_ref, vd_ref, kdp_ref, vdp_ref, attn_ref, rest_ref, restp_ref,
             cw_ref, wo_ref, c_ref, a_ref, b_ref,
             dq_ref, dk_ref, dv_ref, dkh_ref, dvh_ref, dga_ref, db_ref, dgc_ref, dcv_ref, small_ref,
             dmix_ref, dsink_ref):
        i = pl.program_id(0)

        @pl.when(i == 0)
        def _():
            small_ref[...] = jnp.zeros_like(small_ref)
            dsink_ref[...] = jnp.zeros_like(dsink_ref)

        dmix_ref[...] = lax.dot_general(dh2_ref[...].astype(BF16), wo_ref[...], _NT, preferred_element_type=F32)

        ga = rest_ref[:, 0:512]
        sg = _sigmoid(ga)
        dma = dmix_ref[:, 0:ATTN_W]
        dga_ref[...] = (dma * attn_ref[...] * (sg * (1.0 + ga * (1.0 - sg)))).astype(BF16)
        dmix_ref[:, 0:ATTN_W] = dma * (ga * sg)

        u, um1, um2, cv = _conv_parts(rest_ref, restp_ref, cw_ref, i == 0)
        gc = rest_ref[:, 2048:2560]
        sc = _sigmoid(gc)
        bg = rest_ref[:, 512:1024]
        dmc = dmix_ref[:, ATTN_W:]
        t1 = dmc * (gc * sc)
        db_ref[...] = (t1 * cv).astype(BF16)
        dcv = t1 * bg
        dcv_ref[...] = dcv
        dgc_ref[...] = (dmc * (bg * cv) * (sc * (1.0 + gc * (1.0 - sc)))).astype(BF16)
        small_ref[2:3, 0:CONV_W] += jnp.sum(dcv * um2, axis=0, keepdims=True)
        small_ref[3:4, 0:CONV_W] += jnp.sum(dcv * um1, axis=0, keepdims=True)
        small_ref[4:5, 0:CONV_W] += jnp.sum(dcv * u, axis=0, keepdims=True)

        lo = _lane_lo((2 * BLOCK, 128))
        dk_blocks = [None] * (nsub + 1)
        dv_blocks = [None] * (nsub + 1)

        def add(lst, n, val):
            lst[n] = val if lst[n] is None else lst[n] + val

        chains = []
        for rows, kk, vv, mask in _key_windows(i, nsub, kd_ref, vd_ref, kdp_ref, vdp_ref):
            qt = q_ref[rows, :]
            dot = dmix_ref[rows, 0:ATTN_W].astype(BF16)
            for g in range(2):
                chains.append(dict(g=g, rows=rows, mask=mask, qs=_stack_heads(qt, g), dos=_stack_heads(dot, g),
                                   kg=kk[:, 128 * g:128 * (g + 1)], vg=vv[:, 128 * g:128 * (g + 1)]))
        for ch in chains:
            ch["prob"], ch["psink"] = _probs(ch["qs"], ch["kg"], _sink_col(sinks_ref, ch["g"]), ch["mask"])
        for ch in chains:
            ch["dp"] = lax.dot_general(ch["dos"], ch["vg"], _NT, preferred_element_type=F32)
        for ch in chains:
            rs = jnp.sum(ch["prob"] * ch["dp"], axis=-1, keepdims=True)
            ch["ds"] = (ch["prob"] * (ch["dp"] - rs)).astype(BF16)
            dsink_ref[ch["g"]] += -ch["psink"] * rs
        for ch in chains:
            dqs = jnp.dot(ch["ds"], ch["kg"], preferred_element_type=F32) * SCALE
            c, a, b = c_ref[ch["rows"], :], a_ref[ch["rows"], :], b_ref[ch["rows"], :]
            for pp in range(2):
                lanes = slice(256 * ch["g"] + 128 * pp, 256 * ch["g"] + 128 * (pp + 1))
                dq_ref[ch["rows"], lanes] = _rope(_unstack_pair(dqs, pp), c, -a, -b).astype(BF16)
            dkd = lax.dot_general(ch["ds"], ch["qs"], _TN, preferred_element_type=F32)
            dvd = lax.dot_general(ch["prob"].astype(BF16), ch["dos"], _TN, preferred_element_type=F32)
            ch["dk"] = dkd + pltpu.roll(dkd, HEAD_DIM, 1)
            ch["dv"] = dvd + pltpu.roll(dvd, HEAD_DIM, 1)
        for sb in range(nsub):
            dk2 = jnp.where(lo, chains[2 * sb]["dk"], chains[2 * sb + 1]["dk"])
            dv2 = jnp.where(lo, chains[2 * sb]["dv"], chains[2 * sb + 1]["dv"])
            add(dk_blocks, sb, dk2[0:BLOCK])
            add(dk_blocks, sb + 1, dk2[BLOCK:])
            add(dv_blocks, sb, dv2[0:BLOCK])
            add(dv_blocks, sb + 1, dv2[BLOCK:])
        dkh_ref[0] = dk_blocks[0]
        dvh_ref[0] = dv_blocks[0]
        for sb in range(nsub):
            dk_ref[BLOCK * sb:BLOCK * (sb + 1), :] = dk_blocks[sb + 1]
            dv_ref[BLOCK * sb:BLOCK * (sb + 1), :] = dv_blocks[sb + 1]

        @pl.when(i == nt - 1)
        def _():
            for h in range(8):
                tot = jnp.sum(dsink_ref[h // 4, BLOCK * (h % 4):BLOCK * (h % 4 + 1), :], axis=0, keepdims=True)
                small_ref[5:6, h:h + 1] = tot

    tile = lambda w: pl.BlockSpec((T_MIX, w), lambda i: (i, 0))
    whole = lambda r, w: pl.BlockSpec((r, w), lambda i: (0, 0))
    prev_blk = pl.BlockSpec((BLOCK, 2 * KV_W), lambda i: (jnp.maximum(i * nsub - 1, 0), 0))
    prev8 = pl.BlockSpec((8, REST_W), lambda i: (jnp.maximum(i * (T_MIX // 8) - 1, 0), 0))
    halo = pl.BlockSpec((1, BLOCK, KV_W), lambda i: (i, 0, 0))
    bf = lambda w: jax.ShapeDtypeStruct((seq, w), BF16)
    f32 = lambda w: jax.ShapeDtypeStruct((seq, w), F32)
    return pl.pallas_call(
        body, name="bwd_mix", grid=(nt,),
        out_shape=(bf(ATTN_W), f32(KV_W), f32(KV_W), jax.ShapeDtypeStruct((nt, BLOCK, KV_W), F32),
                   jax.ShapeDtypeStruct((nt, BLOCK, KV_W), F32), bf(ATTN_W), bf(CONV_W), bf(CONV_W), f32(CONV_W),
                   jax.ShapeDtypeStruct((SMALL_ROWS, D_MODEL), F32)),
        in_specs=[pl.BlockSpec(memory_space=pltpu.SMEM), tile(D_MODEL), tile(ATTN_W), tile(2 * KV_W), tile(2 * KV_W),
                  prev_blk, prev_blk, tile(ATTN_W), tile(REST_W), prev8, whole(8, CONV_W),
                  whole(D_MODEL, D_MODEL), tile(128), tile(128), tile(128)],
        out_specs=(tile(ATTN_W), tile(KV_W), tile(KV_W), halo, halo, tile(ATTN_W), tile(CONV_W), tile(CONV_W),
                   tile(CONV_W), whole(SMALL_ROWS, D_MODEL)),
        scratch_shapes=[pltpu.VMEM((T_MIX, D_MODEL), F32), pltpu.VMEM((2, 4 * BLOCK, 1), F32)],
        compiler_params=_params(("arbitrary",)),
    )(sinks, dh2, q, kd, vd, kd, vd, attn, rest, rest, conv_w, w_out, rope_c, rope_a, rope_b)


def _bwd_proj(x, norm_g, dh2, dq, dk, dv, dkh, dvh, dga, db, dgc, dcv, rest, conv_w, w_in_t, rope_c, rope_a, rope_b):
    seq = x.shape[0]
    tb = T_PROJ
    per = tb // T_MIX
    nt = seq // tb

    def body(x_ref, g_ref, dh2_ref, dq_ref, dk_ref, dv_ref, dkh_ref, dvh_ref, dkn_ref, dvn_ref, dga_ref, db_ref,
             dgc_ref, dcv_ref, dcvn_ref, ch_ref, cw_ref, w_ref, c_ref, a_ref, b_ref, gx_ref, gw_hbm, small_ref,
             dp_ref, acc_ref):
        i = pl.program_id(0)

        @pl.when(i == 0)
        def _():
            small_ref[...] = jnp.zeros_like(small_ref)
            acc_ref[...] = jnp.zeros_like(acc_ref)

        last = i == nt - 1
        keep = jnp.where(last, 0.0, 1.0)
        pad = jnp.zeros((T_MIX - BLOCK, KV_W), F32)

        def with_halos(main_ref, halo_ref, next_ref):
            parts = []
            for m in range(1, per + 1):
                parts += [pad, halo_ref[m] if m < per else next_ref[0] * keep]
            return main_ref[...] + jnp.concatenate(parts, axis=0)

        dk = with_halos(dk_ref, dkh_ref, dkn_ref)
        dv = with_halos(dv_ref, dvh_ref, dvn_ref)
        dp_ref[:, 0:ATTN_W] = dq_ref[...]
        dp_ref[:, ATTN_W:ATTN_W + KV_W] = _rope(dk, c_ref[...], -a_ref[...], -b_ref[...]).astype(BF16)
        dp_ref[:, ATTN_W + KV_W:ATTN_W + 2 * KV_W] = dv.astype(BF16)
        base = ATTN_W + 2 * KV_W
        dp_ref[:, base:base + 512] = dga_ref[...]
        dp_ref[:, base + 512:base + 1024] = db_ref[...]
        dcv = dcv_ref[...]
        nxt = dcvn_ref[...] * keep
        du = cw_ref[2:3, :] * dcv + cw_ref[1:2, :] * _shift_up(dcv, nxt, 1) + cw_ref[0:1, :] * _shift_up(dcv, nxt, 2)
        dp_ref[:, base + 1024:base + 1536] = (du * ch_ref[:, 512:1024]).astype(BF16)
        dp_ref[:, base + 1536:base + 2048] = (du * ch_ref[:, 0:512]).astype(BF16)
        dp_ref[:, base + 2048:base + 2560] = dgc_ref[...]

        xf = x_ref[...]
        r1 = lax.rsqrt(jnp.mean(xf * xf, axis=-1, keepdims=True) + EPS)
        n1 = xf * r1
        xn = (n1 * g_ref[...]).astype(BF16)
        for n in range(IN_W // 256):
            cols = slice(256 * n, 256 * (n + 1))
            acc_ref[cols, :] += lax.dot_general(dp_ref[:, cols], xn, _TN, preferred_element_type=F32)
        dxn = jnp.dot(dp_ref[...], w_ref[...], preferred_element_type=F32)
        small_ref[0:1, :] += jnp.sum(dxn * n1, axis=0, keepdims=True)
        dxg = dxn * g_ref[...]
        gx_ref[...] = r1 * (dxg - n1 * jnp.mean(dxg * n1, axis=-1, keepdims=True)) + dh2_ref[...]

        @pl.when(last)
        def _():
            pltpu.sync_copy(acc_ref, gw_hbm)

    tile = lambda w: pl.BlockSpec((tb, w), lambda i: (i, 0))
    whole = lambda r, w: pl.BlockSpec((r, w), lambda i: (0, 0))
    halo = pl.BlockSpec((per, BLOCK, KV_W), lambda i: (i, 0, 0))
    halo_next = pl.BlockSpec((1, BLOCK, KV_W), lambda i: (jnp.minimum((i + 1) * per, seq // T_MIX - 1), 0, 0))
    next8 = pl.BlockSpec((8, CONV_W), lambda i: (jnp.minimum((i + 1) * (tb // 8), seq // 8 - 1), 0))
    ch = pl.BlockSpec((tb, 1024), lambda i: (i, 1))
    return pl.pallas_call(
        body, name="bwd_proj", grid=(nt,),
        out_shape=(jax.ShapeDtypeStruct((seq, D_MODEL), F32), jax.ShapeDtypeStruct((IN_W, D_MODEL), F32),
                   jax.ShapeDtypeStruct((SMALL_ROWS, D_MODEL), F32)),
        in_specs=[tile(D_MODEL), whole(1, D_MODEL), tile(D_MODEL), tile(ATTN_W), tile(KV_W), tile(KV_W), halo, halo,
                  halo_next, halo_next,
                  tile(ATTN_W), tile(CONV_W), tile(CONV_W), tile(CONV_W), next8, ch, whole(8, CONV_W),
                  pl.BlockSpec((IN_W, D_MODEL), lambda i: (0, 0), pipeline_mode=pl.Buffered(1)),
                  tile(128), tile(128), tile(128)],
        out_specs=(tile(D_MODEL), pl.BlockSpec(memory_space=pl.ANY), whole(SMALL_ROWS, D_MODEL)),
        scratch_shapes=[pltpu.VMEM((tb, IN_W), BF16), pltpu.VMEM((IN_W, D_MODEL), F32)],
        compiler_params=_params(("arbitrary",)),
    )(x, norm_g, dh2, dq, dk, dv, dkh, dvh, dkh, dvh, dga, db, dgc, dcv, dcv, rest, conv_w, w_in_t,
      rope_c, rope_a, rope_b)


def _adamw(w, g, m, v, name):
    rows, cols = w.shape
    tr = max(t for t in range(8, min(rows, 256) + 1, 8) if rows % t == 0)

    def body(w_ref, g_ref, m_ref, v_ref, d_ref, nm_ref, nv_ref):
        gg = g_ref[...]
        m2 = ADAM_B1 * m_ref[...] + (1.0 - ADAM_B1) * gg
        v2 = ADAM_B2 * v_ref[...] + (1.0 - ADAM_B2) * jnp.square(gg)
        m_hat = m2 / (1.0 - ADAM_B1 ** ADAM_STEP)
        v_hat = v2 / (1.0 - ADAM_B2 ** ADAM_STEP)
        d_ref[...] = -ADAM_LR * (m_hat / (jnp.sqrt(v_hat) + ADAM_EPS) + ADAM_WD * w_ref[...])
        nm_ref[...] = m2
        nv_ref[...] = v2

    spec = pl.BlockSpec((tr, cols), lambda i: (i, 0))
    shp = jax.ShapeDtypeStruct((rows, cols), F32)
    return pl.pallas_call(
        body, name=name, grid=(rows // tr,), out_shape=(shp, shp, shp),
        in_specs=[spec] * 4, out_specs=(spec,) * 3, compiler_params=_params(("arbitrary",)),
    )(w, g, m, v)


def _pack_small(norm_g, final_g, conv_w, sinks):
    out = jnp.zeros((SMALL_ROWS, D_MODEL), F32)
    out = out.at[0].set(norm_g).at[1].set(final_g)
    out = out.at[2:5, 0:128].set(conv_w).at[5, 0:8].set(sinks)
    return out


def kernel(x, norm_g, w_in, sinks, conv_w, w_out, final_g, loss_target, m_norm_g, m_w_in, m_sinks, m_conv_w, m_w_out, m_final_g, v_norm_g, v_w_in, v_sinks, v_conv_w, v_w_out, v_final_g):
    seq = x.shape[1]
    x2 = x.reshape(seq, D_MODEL)
    tgt = loss_target.reshape(seq, D_MODEL)
    ng = norm_g.reshape(1, D_MODEL)
    fg = final_g.reshape(1, D_MODEL)
    chip = 2 * lax.axis_index("x") + lax.axis_index("y")

    conv_w8 = jnp.zeros((8, 128), F32).at[0:3].set(conv_w)
    wi_all, wo_all, cw_all = _gather_weights(w_in.T, w_out, conv_w8)
    w_in_full = wi_all.reshape(IN_W, D_MODEL)
    w_out_full = wo_all.reshape(D_MODEL, D_MODEL)
    conv_full = jnp.concatenate([cw_all[j] for j in range(N_CHIPS)], axis=1)

    q, kd, vd, rest, rope_c, rope_a, rope_b = _fwd_proj(x2, ng, w_in_full)
    attn, dh2, g_wo, small_f = _fwd_mix(x2, q, kd, vd, rest, sinks, conv_full, w_out_full, fg, tgt)
    dq, dk, dv, dkh, dvh, dga, db, dgc, dcv, small_m = _bwd_mix(
        dh2, q, kd, vd, attn, rest, sinks, conv_full, w_out_full, rope_c, rope_a, rope_b)
    grad_x, g_wi, small_p = _bwd_proj(x2, ng, dh2, dq, dk, dv, dkh, dvh, dga, db, dgc, dcv, rest, conv_full,
                                      w_in_full, rope_c, rope_a, rope_b)

    g_in_blocks = g_wi.reshape(N_CHIPS, W_IN_BLK, D_MODEL)
    g_out_blocks = g_wo.reshape(N_CHIPS, W_OUT_BLK, D_MODEL)
    grad_w_in_t, grad_w_out, small = _reduce_grads(g_in_blocks, g_out_blocks, small_f + small_m + small_p)
    grad_w_in = grad_w_in_t.T

    loss = jnp.sum(small[6])
    grad_norm_g, grad_final_g = small[0], small[1]
    grad_conv_full = small[2:5, 0:CONV_W]
    grad_conv_w = lax.dynamic_slice(grad_conv_full, (0, 128 * chip), (3, 128))
    grad_sinks = small[5, 0:8]

    d_wi, nm_wi, nv_wi = (t.T for t in _adamw(w_in.T, grad_w_in_t, m_w_in.T, v_w_in.T, "adamw_w_in"))
    d_wo, nm_wo, nv_wo = _adamw(w_out, grad_w_out, m_w_out, v_w_out, "adamw_w_out")
    d_s, nm_s, nv_s = _adamw(_pack_small(norm_g, final_g, conv_w, sinks),
                             _pack_small(grad_norm_g, grad_final_g, grad_conv_w, grad_sinks),
                             _pack_small(m_norm_g, m_final_g, m_conv_w, m_sinks),
                             _pack_small(v_norm_g, v_final_g, v_conv_w, v_sinks), "adamw_small")

    def unpack(p):
        return p[0], p[2:5, 0:128], p[5, 0:8], p[1]

    out = [loss, grad_x.reshape(1, seq, D_MODEL), grad_norm_g, grad_w_in, grad_sinks, grad_conv_w, grad_w_out,
           grad_final_g]
    for p_small, p_wi, p_wo in ((d_s, d_wi, d_wo), (nm_s, nm_wi, nm_wo), (nv_s, nv_wi, nv_wo)):
        n_g, c_w, s_k, f_g = unpack(p_small)
        out += [n_g, p_wi, s_k, c_w, p_wo, f_g]
    return tuple(out)
```

```python
import jax
import jax.numpy as jnp
from jax import lax
from jax.experimental import pallas as pl
from jax.experimental.pallas import tpu as pltpu

F32 = jnp.float32
BF16 = jnp.bfloat16

D_MODEL = 1024
HEAD_DIM = 64
ATTN_W = 512
KV_W = 128
CONV_W = 512
IN_W = 3328
REST_W = IN_W - ATTN_W - 2 * KV_W
BLOCK = 128
ROT_DIM = 16
ROPE_THETA = 500000.0
EPS = 1e-5
SCALE = 0.125
NEG = -1e30

N_CHIPS = 4
W_IN_BLK = IN_W // N_CHIPS
W_OUT_BLK = D_MODEL // N_CHIPS

ADAM_LR = 0.001
ADAM_B1 = 0.9
ADAM_B2 = 0.999
ADAM_EPS = 1e-08
ADAM_WD = 0.01
ADAM_STEP = 10

VMEM_LIMIT = 56 * 1024 * 1024
T_PROJ = 512
T_FMIX = 512
T_MIX = 512
SMALL_ROWS = 8
MESH = pl.DeviceIdType.MESH

_NT = (((1,), (1,)), ((), ()))
_TN = (((0,), (0,)), ((), ()))


def _params(sem=None):
    kw = dict(vmem_limit_bytes=VMEM_LIMIT)
    if sem is not None:
        kw["dimension_semantics"] = sem
    return pltpu.CompilerParams(**kw)


def _sigmoid(t):
    return 1.0 / (1.0 + jnp.exp(-t))


def _shift_down(t, prev8, k):
    rolled = pltpu.roll(t, k, 0)
    row = lax.broadcasted_iota(jnp.int32, t.shape, 0)
    for j in range(k):
        rolled = jnp.where(row == j, prev8[8 - k + j:8 - k + j + 1, :], rolled)
    return rolled


def _shift_up(t, next8, k):
    n = t.shape[0]
    rolled = pltpu.roll(t, n - k, 0)
    row = lax.broadcasted_iota(jnp.int32, t.shape, 0)
    for j in range(k):
        rolled = jnp.where(row == n - k + j, next8[j:j + 1, :], rolled)
    return rolled


def _rope(t, c, a, b):
    w = t.shape[1]
    reps = w // 128
    if reps > 1:
        c, a, b = (jnp.concatenate([z] * reps, axis=1) for z in (c, a, b))
    return t * c + pltpu.roll(t, w - 8, 1) * a + pltpu.roll(t, 8, 1) * b


def _lane_lo(shape):
    return lax.broadcasted_iota(jnp.int32, shape, 1) < HEAD_DIM


def _stack_heads(t, g):
    lo = _lane_lo((BLOCK, 128))
    parts = []
    for hh in range(4):
        pair = t[:, 256 * g + 128 * (hh // 2):256 * g + 128 * (hh // 2) + 128]
        keep = lo if hh % 2 == 0 else jnp.logical_not(lo)
        parts.append(jnp.where(keep, pair, jnp.zeros_like(pair)))
    return jnp.concatenate(parts, axis=0)


def _unstack_pair(o, pp):
    lo = _lane_lo((BLOCK, 128))
    return jnp.where(lo, o[256 * pp:256 * pp + 128], o[256 * pp + 128:256 * pp + 256])


def _band_masks(has_prev):
    r = lax.broadcasted_iota(jnp.int32, (4 * BLOCK, 2 * BLOCK), 0) % BLOCK
    kj = lax.broadcasted_iota(jnp.int32, (4 * BLOCK, 2 * BLOCK), 1)
    cur = (kj >= BLOCK) & (kj - BLOCK <= r)
    prev = (kj < BLOCK) & (kj > r)
    return cur | (prev & has_prev), cur | prev


def _sink_col(sinks_ref, g):
    r = lax.broadcasted_iota(jnp.int32, (4 * BLOCK, 1), 0) // BLOCK
    col = jnp.full((4 * BLOCK, 1), sinks_ref[4 * g + 3], F32)
    for hh in range(3):
        col = jnp.where(r == hh, sinks_ref[4 * g + hh], col)
    return col


def _probs(qs, kd, sink_col, mask):
    return _softmax(lax.dot_general(qs, kd, _NT, preferred_element_type=F32), sink_col, mask)


def _softmax(s, sink_col, mask):
    s = jnp.where(mask, s, NEG)
    m = jnp.maximum(jnp.max(s, axis=-1, keepdims=True), sink_col)
    p = jnp.exp(s - m)
    es = jnp.exp(sink_col - m)
    inv = 1.0 / (jnp.sum(p, axis=-1, keepdims=True) + es)
    return p * inv, es * inv


def _key_windows(i, nsub, kd_ref, vd_ref, kdp_ref, vdp_ref):
    mask_first, mask_rest = _band_masks(i > 0)
    out = []
    for sb in range(nsub):
        rows = slice(BLOCK * sb, BLOCK * (sb + 1))
        if sb == 0:
            kk = jnp.concatenate([kdp_ref[...], kd_ref[rows, :]], axis=0)
            vv = jnp.concatenate([vdp_ref[...], vd_ref[rows, :]], axis=0)
            out.append((rows, kk, vv, mask_first))
        else:
            both = slice(BLOCK * (sb - 1), BLOCK * (sb + 1))
            out.append((rows, kd_ref[both, :], vd_ref[both, :], mask_rest))
    return out


def _gather_weights(w_in_t, w_out, conv_w8):
    hi, ho = W_IN_BLK // 2, W_OUT_BLK // 2

    def body(wi_ref, wo_ref, cw_ref, wi_all, wo_all, cw_all, send_sems, recv_sems):
        x, y, c = lax.axis_index("x"), lax.axis_index("y"), lax.axis_index("c")
        me = 2 * x + y
        sibling = (x, y, 1 - c)
        chips = [(1 - x, y), (x, 1 - y), (1 - x, 1 - y)]

        wi_all[me] = wi_ref[...].astype(BF16)
        wo_all[me] = wo_ref[...].astype(BF16)
        cw_all[me] = cw_ref[...]

        def copies(k, chip, half, to):
            j = 2 * chip[0] + chip[1]
            refs = (wi_all.at[j, pl.ds(half * hi, hi)], wo_all.at[j, pl.ds(half * ho, ho)])
            return [pltpu.make_async_remote_copy(src_ref=r, dst_ref=r, send_sem=send_sems.at[2 * k + n],
                                                 recv_sem=recv_sems.at[2 * k + n], device_id=to,
                                                 device_id_type=MESH) for n, r in enumerate(refs)]

        def conv_copy(k, chip, to):
            r = cw_all.at[2 * chip[0] + chip[1]]
            return pltpu.make_async_remote_copy(src_ref=r, dst_ref=r, send_sem=send_sems.at[12 + k],
                                                recv_sem=recv_sems.at[12 + k], device_id=to, device_id_type=MESH)

        first = [cp for k, chip in enumerate(chips) for cp in copies(k, (x, y), c, (*chip, c))]
        first += [conv_copy(k, (x, y), (*chip, c)) for k, chip in enumerate(chips)]
        for cp in first:
            cp.start()
        passed = []
        for k, chip in enumerate(chips):
            for cp in copies(k, chip, c, (x, y, c)):
                cp.wait_recv()
            fwd = copies(3 + k, chip, c, sibling)
            for cp in fwd:
                cp.start()
            passed += fwd
        for k, chip in enumerate(chips):
            for cp in copies(3 + k, chip, 1 - c, (x, y, c)):
                cp.wait_recv()
            conv_copy(k, chip, (x, y, c)).wait_recv()
        for cp in first + passed:
            cp.wait_send()

    vmem = pl.BlockSpec(memory_space=pltpu.VMEM)
    return pl.pallas_call(
        body, name="gather_weights",
        out_shape=(jax.ShapeDtypeStruct((N_CHIPS, W_IN_BLK, D_MODEL), BF16),
                   jax.ShapeDtypeStruct((N_CHIPS, W_OUT_BLK, D_MODEL), BF16),
                   jax.ShapeDtypeStruct((N_CHIPS, 8, 128), F32)),
        in_specs=[vmem, vmem, vmem], out_specs=(vmem, vmem, vmem),
        scratch_shapes=[pltpu.SemaphoreType.DMA((15,)), pltpu.SemaphoreType.DMA((15,))],
        compiler_params=_params(),
    )(w_in_t, w_out, conv_w8)


def _reduce_grads(g_in, g_out, *smalls):
    hi, ho = W_IN_BLK // 2, W_OUT_BLK // 2

    def body(gi_hbm, go_hbm, s0_ref, s1_ref, s2_ref, gi_out, go_out, small_out,
             mine_i, mine_o, sib_i, sib_o, out_i, out_o, ici_i, ici_o, small_in, small_ref, send_sems, recv_sems,
             local_sems):
        x, y, c = lax.axis_index("x"), lax.axis_index("y"), lax.axis_index("c")
        my_dev = 4 * x + 2 * y + c
        sibling = (x, y, 1 - c)
        chips = [(1 - x, y), (x, 1 - y), (1 - x, 1 - y)]
        order = chips + [(x, y)]

        def remote(k, src, dst, to):
            return pltpu.make_async_remote_copy(src_ref=src, dst_ref=dst, send_sem=send_sems.at[k],
                                                recv_sem=recv_sems.at[k], device_id=to, device_id_type=MESH)

        small_ref[...] = s0_ref[...] + s1_ref[...] + s2_ref[...]
        small_cps = []
        for f in range(1, 8):
            fx, fy, fc = f >> 2, (f >> 1) & 1, f & 1
            small_cps.append(remote(16 + f - 1, small_ref, small_in.at[f - 1], (x ^ fx, y ^ fy, c ^ fc)))
        for cp in small_cps:
            cp.start()

        own, to_sib = [], []
        for n, chip in enumerate(order):
            j = 2 * chip[0] + chip[1]
            own += [pltpu.make_async_copy(gi_hbm.at[j, pl.ds(c * hi, hi)], mine_i.at[n], local_sems.at[2 * n]),
                    pltpu.make_async_copy(go_hbm.at[j, pl.ds(c * ho, ho)], mine_o.at[n], local_sems.at[2 * n + 1])]
            to_sib += [remote(2 * n, gi_hbm.at[j, pl.ds((1 - c) * hi, hi)], sib_i.at[n], sibling),
                       remote(2 * n + 1, go_hbm.at[j, pl.ds((1 - c) * ho, ho)], sib_o.at[n], sibling)]
            for cp in own[-2:] + to_sib[-2:]:
                cp.start()

        ici = []
        for k, chip in enumerate(chips):
            for cp in own[2 * k:2 * k + 2]:
                cp.wait()
            for cp in to_sib[2 * k:2 * k + 2]:
                cp.wait_recv()
            out_i[k] = (mine_i[k] + sib_i[k]).astype(BF16)
            out_o[k] = (mine_o[k] + sib_o[k]).astype(BF16)
            ici += [remote(8 + 2 * k, out_i.at[k], ici_i.at[k], (*chip, c)),
                    remote(9 + 2 * k, out_o.at[k], ici_o.at[k], (*chip, c))]
            ici[-2].start()
            ici[-1].start()
        for cp in own[6:8]:
            cp.wait()
        for cp in to_sib[6:8]:
            cp.wait_recv()
        tot_i = mine_i[3] + sib_i[3]
        tot_o = mine_o[3] + sib_o[3]
        for k in range(3):
            ici[2 * k].wait_recv()
            ici[2 * k + 1].wait_recv()
            tot_i = tot_i + ici_i[k].astype(F32)
            tot_o = tot_o + ici_o[k].astype(F32)
        gi_out[pl.ds(c * hi, hi), :] = tot_i
        go_out[pl.ds(c * ho, ho), :] = tot_o

        swap = [remote(14, gi_out.at[pl.ds(c * hi, hi)], gi_out.at[pl.ds(c * hi, hi)], sibling),
                remote(15, go_out.at[pl.ds(c * ho, ho)], go_out.at[pl.ds(c * ho, ho)], sibling)]
        for cp in swap:
            cp.start()

        for cp in small_cps:
            cp.wait_recv()
        total = jnp.zeros((SMALL_ROWS, D_MODEL), F32)
        for d in range(8):
            slot = jnp.maximum((d ^ my_dev) - 1, 0)
            total = total + jnp.where(d == my_dev, small_ref[...], small_in[slot])
        small_out[...] = total

        recv_swap = [remote(14, gi_out.at[pl.ds((1 - c) * hi, hi)], gi_out.at[pl.ds((1 - c) * hi, hi)], sibling),
                     remote(15, go_out.at[pl.ds((1 - c) * ho, ho)], go_out.at[pl.ds((1 - c) * ho, ho)], sibling)]
        for cp in recv_swap:
            cp.wait_recv()
        for cp in to_sib + ici + swap + small_cps:
            cp.wait_send()

    vmem = pl.BlockSpec(memory_space=pltpu.VMEM)
    anyspace = pl.BlockSpec(memory_space=pl.ANY)
    return pl.pallas_call(
        body, name="reduce_grads",
        out_shape=(jax.ShapeDtypeStruct((W_IN_BLK, D_MODEL), F32),
                   jax.ShapeDtypeStruct((W_OUT_BLK, D_MODEL), F32),
                   jax.ShapeDtypeStruct((SMALL_ROWS, D_MODEL), F32)),
        in_specs=[anyspace, anyspace, vmem, vmem, vmem], out_specs=(vmem, vmem, vmem),
        scratch_shapes=[pltpu.VMEM((N_CHIPS, hi, D_MODEL), F32), pltpu.VMEM((N_CHIPS, ho, D_MODEL), F32),
                        pltpu.VMEM((N_CHIPS, hi, D_MODEL), F32), pltpu.VMEM((N_CHIPS, ho, D_MODEL), F32),
                        pltpu.VMEM((3, hi, D_MODEL), BF16), pltpu.VMEM((3, ho, D_MODEL), BF16),
                        pltpu.VMEM((3, hi, D_MODEL), BF16), pltpu.VMEM((3, ho, D_MODEL), BF16),
                        pltpu.VMEM((7, SMALL_ROWS, D_MODEL), F32), pltpu.VMEM((SMALL_ROWS, D_MODEL), F32),
                        pltpu.SemaphoreType.DMA((23,)), pltpu.SemaphoreType.DMA((23,)),
                        pltpu.SemaphoreType.DMA((8,))],
        compiler_params=_params(),
    )(g_in, g_out, *smalls)


def _fwd_proj(x, norm_g, w_in_t):
    seq = x.shape[0]
    nt = seq // T_PROJ
    lane = jnp.arange(128, dtype=jnp.int32) % HEAD_DIM
    inv_freq = ROPE_THETA ** (-(2 * (lane % 8)).astype(F32) / ROT_DIM)
    inv_freq = jnp.where(lane < ROT_DIM, inv_freq, 0.0).reshape(1, 128)
    in_tile = jnp.arange(T_PROJ, dtype=jnp.int32).astype(F32)[:, None] * inv_freq
    cos_in, sin_in = jnp.cos(in_tile), jnp.sin(in_tile)
    start = jnp.repeat((jnp.arange(nt, dtype=jnp.int32) * T_PROJ).astype(F32), 8)[:, None] * inv_freq
    cos_st, sin_st = jnp.cos(start), jnp.sin(start)

    def body(x_ref, g_ref, w_ref, cs_ref, ss_ref, ci_ref, si_ref, q_ref, kd_ref, vd_ref, rest_ref, c_ref, a_ref, b_ref):
        xf = x_ref[...]
        r1 = lax.rsqrt(jnp.mean(xf * xf, axis=-1, keepdims=True) + EPS)
        xn = (xf * r1 * g_ref[...]).astype(BF16)
        cs, ss = cs_ref[0:1, :], ss_ref[0:1, :]
        c = cs * ci_ref[...] - ss * si_ref[...]
        sin = ss * ci_ref[...] + cs * si_ref[...]
        j = lax.broadcasted_iota(jnp.int32, (T_PROJ, 128), 1) % HEAD_DIM
        a = jnp.where(j < 8, -sin, 0.0)
        b = jnp.where(j >= 8, sin, 0.0)
        c_ref[...], a_ref[...], b_ref[...] = c, a, b
        proj = lambda lo_c, w: lax.dot_general(xn, w_ref[lo_c:lo_c + w, :], _NT, preferred_element_type=F32)
        q_ref[...] = (_rope(proj(0, ATTN_W), c, a, b) * SCALE).astype(BF16)
        kv = proj(ATTN_W, 2 * KV_W)
        k = _rope(kv[:, 0:KV_W], c, a, b)
        v = kv[:, KV_W:2 * KV_W]
        lo = _lane_lo(k.shape)
        for t, ref in ((k, kd_ref), (v, vd_ref)):
            sw = pltpu.roll(t, HEAD_DIM, 1)
            ref[:, 0:128] = jnp.where(lo, t, sw).astype(BF16)
            ref[:, 128:256] = jnp.where(lo, sw, t).astype(BF16)
        for n in range(REST_W // 512):
            rest_ref[:, 512 * n:512 * (n + 1)] = proj(ATTN_W + 2 * KV_W + 512 * n, 512)

    tile = lambda w: pl.BlockSpec((T_PROJ, w), lambda i: (i, 0))
    whole = lambda r, w: pl.BlockSpec((r, w), lambda i: (0, 0))
    return pl.pallas_call(
        body, name="fwd_proj", grid=(nt,),
        out_shape=(jax.ShapeDtypeStruct((seq, ATTN_W), BF16), jax.ShapeDtypeStruct((seq, 2 * KV_W), BF16),
                   jax.ShapeDtypeStruct((seq, 2 * KV_W), BF16), jax.ShapeDtypeStruct((seq, REST_W), F32))
        + (jax.ShapeDtypeStruct((seq, 128), F32),) * 3,
        in_specs=[tile(D_MODEL), whole(1, D_MODEL), whole(IN_W, D_MODEL), pl.BlockSpec((8, 128), lambda i: (i, 0)),
                  pl.BlockSpec((8, 128), lambda i: (i, 0)), whole(T_PROJ, 128), whole(T_PROJ, 128)],
        out_specs=(tile(ATTN_W), tile(2 * KV_W), tile(2 * KV_W), tile(REST_W), tile(128), tile(128), tile(128)),
        compiler_params=_params(("arbitrary",)),
    )(x, norm_g, w_in_t, cos_st, sin_st, cos_in, sin_in)


CONV_SPEC = pl.BlockSpec((N_CHIPS, 8, 128), lambda i: (0, 0, 0))


def _conv_rows(cw_ref):
    return jnp.concatenate([cw_ref[j] for j in range(N_CHIPS)], axis=1)


def _conv_parts(rest_ref, prev_ref, cw_ref, first):
    u = rest_ref[:, 1024:1536] * rest_ref[:, 1536:2048]
    up = prev_ref[:, 1024:1536] * prev_ref[:, 1536:2048]
    up = jnp.where(first, jnp.zeros_like(up), up)
    um1 = _shift_down(u, up, 1)
    um2 = _shift_down(u, up, 2)
    cw = _conv_rows(cw_ref)
    cv = cw[0:1, :] * um2 + cw[1:2, :] * um1 + cw[2:3, :] * u
    return u, um1, um2, cv


def _fwd_mix(x, q, kd, vd, rest, sinks, conv_w, w_out, final_g, target):
    seq = x.shape[0]
    nt = seq // T_FMIX
    nsub = T_FMIX // BLOCK

    def body(sinks_ref, x_ref, q_ref, kd_ref, vd_ref, kdp_ref, vdp_ref, rest_ref, restp_ref, cw_ref, wo_ref,
             fg_ref, tgt_ref, attn_ref, dh2_ref, gwo_ref, small_ref, mix_ref, pmix_ref, pdh2_ref):
        i = pl.program_id(0)

        @pl.when(i == 0)
        def _():
            small_ref[...] = jnp.zeros_like(small_ref)
            gwo_ref[...] = jnp.zeros_like(gwo_ref)
            pmix_ref[...] = jnp.zeros_like(pmix_ref)
            pdh2_ref[...] = jnp.zeros_like(pdh2_ref)

        gwo_ref[...] += lax.dot_general(pmix_ref[...], pdh2_ref[...], _TN, preferred_element_type=F32)

        chains = []
        for rows, kk, vv, mask in _key_windows(i, nsub, kd_ref, vd_ref, kdp_ref, vdp_ref):
            qt = q_ref[rows, :]
            for g in range(2):
                kg = kk[:, 128 * g:128 * (g + 1)]
                chains.append(dict(g=g, rows=rows, mask=mask, vg=vv[:, 128 * g:128 * (g + 1)],
                                   s=lax.dot_general(_stack_heads(qt, g), kg, _NT, preferred_element_type=F32)))
        for ch in chains:
            ch["prob"], _ = _softmax(ch.pop("s"), _sink_col(sinks_ref, ch["g"]), ch["mask"])
        for ch in chains:
            o = jnp.dot(ch["prob"].astype(BF16), ch["vg"], preferred_element_type=F32)
            for pp in range(2):
                lanes = slice(256 * ch["g"] + 128 * pp, 256 * ch["g"] + 128 * (pp + 1))
                attn_ref[ch["rows"], lanes] = _unstack_pair(o, pp)

        ga = rest_ref[:, 0:512]
        mix_ref[:, 0:ATTN_W] = (attn_ref[...] * (ga * _sigmoid(ga))).astype(BF16)
        _, _, _, cv = _conv_parts(rest_ref, restp_ref, cw_ref, i == 0)
        gc = rest_ref[:, 2048:2560]
        mix_ref[:, ATTN_W:] = (rest_ref[:, 512:1024] * cv * (gc * _sigmoid(gc))).astype(BF16)

        h2 = x_ref[...] + jnp.dot(mix_ref[...], wo_ref[...], preferred_element_type=F32)
        r2 = lax.rsqrt(jnp.mean(h2 * h2, axis=-1, keepdims=True) + EPS)
        n2 = h2 * r2
        err = n2 * fg_ref[...] - tgt_ref[...]
        dy = err * (1.0 / D_MODEL)
        small_ref[6:7, :] += jnp.sum(err * err, axis=0, keepdims=True) * (0.5 / D_MODEL)
        small_ref[1:2, :] += jnp.sum(dy * n2, axis=0, keepdims=True)
        dn = dy * fg_ref[...]
        dh2 = r2 * (dn - n2 * jnp.mean(dn * n2, axis=-1, keepdims=True))
        dh2_ref[...] = dh2
        pmix_ref[...] = mix_ref[...]
        pdh2_ref[...] = dh2.astype(BF16)

        @pl.when(i == nt - 1)
        def _():
            gwo_ref[...] += lax.dot_general(pmix_ref[...], pdh2_ref[...], _TN, preferred_element_type=F32)

    tile = lambda w: pl.BlockSpec((T_FMIX, w), lambda i: (i, 0))
    whole = lambda r, w: pl.BlockSpec((r, w), lambda i: (0, 0))
    prev_blk = pl.BlockSpec((BLOCK, 2 * KV_W), lambda i: (jnp.maximum(i * nsub - 1, 0), 0))
    prev8 = pl.BlockSpec((8, REST_W), lambda i: (jnp.maximum(i * (T_FMIX // 8) - 1, 0), 0))
    return pl.pallas_call(
        body, name="fwd_mix", grid=(nt,),
        out_shape=(jax.ShapeDtypeStruct((seq, ATTN_W), F32), jax.ShapeDtypeStruct((seq, D_MODEL), F32),
                   jax.ShapeDtypeStruct((D_MODEL, D_MODEL), F32), jax.ShapeDtypeStruct((SMALL_ROWS, D_MODEL), F32)),
        in_specs=[pl.BlockSpec(memory_space=pltpu.SMEM), tile(D_MODEL), tile(ATTN_W), tile(2 * KV_W), tile(2 * KV_W),
                  prev_blk, prev_blk, tile(REST_W), prev8, CONV_SPEC, whole(D_MODEL, D_MODEL),
                  whole(1, D_MODEL), tile(D_MODEL)],
        out_specs=(tile(ATTN_W), tile(D_MODEL), whole(D_MODEL, D_MODEL), whole(SMALL_ROWS, D_MODEL)),
        scratch_shapes=[pltpu.VMEM((T_FMIX, D_MODEL), BF16)] * 3,
        compiler_params=_params(("arbitrary",)),
    )(sinks, x, q, kd, vd, kd, vd, rest, rest, conv_w, w_out, final_g, target)


def _bwd_mix(dh2, q, kd, vd, attn, rest, sinks, conv_w, w_out, rope_c, rope_a, rope_b):
    seq = dh2.shape[0]
    nt = seq // T_MIX
    nsub = T_MIX // BLOCK

    def body(sinks_ref, dh2_ref, q_ref, kd_ref, vd_ref, kdp_ref, vdp_ref, attn_ref, rest_ref, restp_ref,
             cw_ref, wo_ref, c_ref, a_ref, b_ref,
             dq_ref, dk_ref, dv_ref, dkh_ref, dvh_ref, dga_ref, db_ref, dgc_ref, dcv_ref, small_ref,
             dmix_ref, dsink_ref):
        i = pl.program_id(0)

        @pl.when(i == 0)
        def _():
            small_ref[...] = jnp.zeros_like(small_ref)
            dsink_ref[...] = jnp.zeros_like(dsink_ref)

        dmix_ref[...] = lax.dot_general(dh2_ref[...].astype(BF16), wo_ref[...], _NT, preferred_element_type=F32)

        ga = rest_ref[:, 0:512]
        sg = _sigmoid(ga)
        dma = dmix_ref[:, 0:ATTN_W]
        dga_ref[...] = (dma * attn_ref[...] * (sg * (1.0 + ga * (1.0 - sg)))).astype(BF16)
        dmix_ref[:, 0:ATTN_W] = dma * (ga * sg)

        u, um1, um2, cv = _conv_parts(rest_ref, restp_ref, cw_ref, i == 0)
        gc = rest_ref[:, 2048:2560]
        sc = _sigmoid(gc)
        bg = rest_ref[:, 512:1024]
        dmc = dmix_ref[:, ATTN_W:]
        t1 = dmc * (gc * sc)
        db_ref[...] = (t1 * cv).astype(BF16)
        dcv = t1 * bg
        dcv_ref[...] = dcv
        dgc_ref[...] = (dmc * (bg * cv) * (sc * (1.0 + gc * (1.0 - sc)))).astype(BF16)
        small_ref[2:3, 0:CONV_W] += jnp.sum(dcv * um2, axis=0, keepdims=True)
        small_ref[3:4, 0:CONV_W] += jnp.sum(dcv * um1, axis=0, keepdims=True)
        small_ref[4:5, 0:CONV_W] += jnp.sum(dcv * u, axis=0, keepdims=True)

        lo = _lane_lo((2 * BLOCK, 128))
        dk_blocks = [None] * (nsub + 1)
        dv_blocks = [None] * (nsub + 1)

        def add(lst, n, val):
            lst[n] = val if lst[n] is None else lst[n] + val

        chains = []
        for rows, kk, vv, mask in _key_windows(i, nsub, kd_ref, vd_ref, kdp_ref, vdp_ref):
            qt = q_ref[rows, :]
            dot = dmix_ref[rows, 0:ATTN_W].astype(BF16)
            for g in range(2):
                chains.append(dict(g=g, rows=rows, mask=mask, qs=_stack_heads(qt, g), dos=_stack_heads(dot, g),
                                   kg=kk[:, 128 * g:128 * (g + 1)], vg=vv[:, 128 * g:128 * (g + 1)]))
        for ch in chains:
            ch["prob"], ch["psink"] = _probs(ch["qs"], ch["kg"], _sink_col(sinks_ref, ch["g"]), ch["mask"])
        for ch in chains:
            ch["dp"] = lax.dot_general(ch["dos"], ch["vg"], _NT, preferred_element_type=F32)
        for ch in chains:
            rs = jnp.sum(ch["prob"] * ch["dp"], axis=-1, keepdims=True)
            ch["ds"] = (ch["prob"] * (ch["dp"] - rs)).astype(BF16)
            dsink_ref[ch["g"]] += -ch["psink"] * rs
        for ch in chains:
            dqs = jnp.dot(ch["ds"], ch["kg"], preferred_element_type=F32) * SCALE
            c, a, b = c_ref[ch["rows"], :], a_ref[ch["rows"], :], b_ref[ch["rows"], :]
            for pp in range(2):
                lanes = slice(256 * ch["g"] + 128 * pp, 256 * ch["g"] + 128 * (pp + 1))
                dq_ref[ch["rows"], lanes] = _rope(_unstack_pair(dqs, pp), c, -a, -b).astype(BF16)
            dkd = lax.dot_general(ch["ds"], ch["qs"], _TN, preferred_element_type=F32)
            dvd = lax.dot_general(ch["prob"].astype(BF16), ch["dos"], _TN, preferred_element_type=F32)
            ch["dk"] = dkd + pltpu.roll(dkd, HEAD_DIM, 1)
            ch["dv"] = dvd + pltpu.roll(dvd, HEAD_DIM, 1)
        for sb in range(nsub):
            dk2 = jnp.where(lo, chains[2 * sb]["dk"], chains[2 * sb + 1]["dk"])
            dv2 = jnp.where(lo, chains[2 * sb]["dv"], chains[2 * sb + 1]["dv"])
            add(dk_blocks, sb, dk2[0:BLOCK])
            add(dk_blocks, sb + 1, dk2[BLOCK:])
            add(dv_blocks, sb, dv2[0:BLOCK])
            add(dv_blocks, sb + 1, dv2[BLOCK:])
        dkh_ref[0] = dk_blocks[0]
        dvh_ref[0] = dv_blocks[0]
        for sb in range(nsub):
            dk_ref[BLOCK * sb:BLOCK * (sb + 1), :] = dk_blocks[sb + 1]
            dv_ref[BLOCK * sb:BLOCK * (sb + 1), :] = dv_blocks[sb + 1]

        @pl.when(i == nt - 1)
        def _():
            for h in range(8):
                tot = jnp.sum(dsink_ref[h // 4, BLOCK * (h % 4):BLOCK * (h % 4 + 1), :], axis=0, keepdims=True)
                small_ref[5:6, h:h + 1] = tot

    tile = lambda w: pl.BlockSpec((T_MIX, w), lambda i: (i, 0))
    whole = lambda r, w: pl.BlockSpec((r, w), lambda i: (0, 0))
    prev_blk = pl.BlockSpec((BLOCK, 2 * KV_W), lambda i: (jnp.maximum(i * nsub - 1, 0), 0))
    prev8 = pl.BlockSpec((8, REST_W), lambda i: (jnp.maximum(i * (T_MIX // 8) - 1, 0), 0))
    halo = pl.BlockSpec((1, BLOCK, KV_W), lambda i: (i, 0, 0))
    bf = lambda w: jax.ShapeDtypeStruct((seq, w), BF16)
    f32 = lambda w: jax.ShapeDtypeStruct((seq, w), F32)
    return pl.pallas_call(
        body, name="bwd_mix", grid=(nt,),
        out_shape=(bf(ATTN_W), f32(KV_W), f32(KV_W), jax.ShapeDtypeStruct((nt, BLOCK, KV_W), F32),
                   jax.ShapeDtypeStruct((nt, BLOCK, KV_W), F32), bf(ATTN_W), bf(CONV_W), bf(CONV_W), f32(CONV_W),
                   jax.ShapeDtypeStruct((SMALL_ROWS, D_MODEL), F32)),
        in_specs=[pl.BlockSpec(memory_space=pltpu.SMEM), tile(D_MODEL), tile(ATTN_W), tile(2 * KV_W), tile(2 * KV_W),
                  prev_blk, prev_blk, tile(ATTN_W), tile(REST_W), prev8, CONV_SPEC,
                  whole(D_MODEL, D_MODEL), tile(128), tile(128), tile(128)],
        out_specs=(tile(ATTN_W), tile(KV_W), tile(KV_W), halo, halo, tile(ATTN_W), tile(CONV_W), tile(CONV_W),
                   tile(CONV_W), whole(SMALL_ROWS, D_MODEL)),
        scratch_shapes=[pltpu.VMEM((T_MIX, D_MODEL), F32), pltpu.VMEM((2, 4 * BLOCK, 1), F32)],
        compiler_params=_params(("arbitrary",)),
    )(sinks, dh2, q, kd, vd, kd, vd, attn, rest, rest, conv_w, w_out, rope_c, rope_a, rope_b)


def _bwd_proj(x, norm_g, dh2, dq, dk, dv, dkh, dvh, dga, db, dgc, dcv, rest, conv_w, w_in_t, rope_c, rope_a, rope_b):
    seq = x.shape[0]
    tb = T_PROJ
    per = tb // T_MIX
    nt = seq // tb

    def body(x_ref, g_ref, dh2_ref, dq_ref, dk_ref, dv_ref, dkh_ref, dvh_ref, dkn_ref, dvn_ref, dga_ref, db_ref,
             dgc_ref, dcv_ref, dcvn_ref, ch_ref, cw_ref, w_ref, c_ref, a_ref, b_ref, gx_ref, gw_hbm, small_ref,
             dp_ref, acc_ref):
        i = pl.program_id(0)

        @pl.when(i == 0)
        def _():
            small_ref[...] = jnp.zeros_like(small_ref)
            acc_ref[...] = jnp.zeros_like(acc_ref)

        last = i == nt - 1
        keep = jnp.where(last, 0.0, 1.0)
        pad = jnp.zeros((T_MIX - BLOCK, KV_W), F32)

        def with_halos(main_ref, halo_ref, next_ref):
            parts = []
            for m in range(1, per + 1):
                parts += [pad, halo_ref[m] if m < per else next_ref[0] * keep]
            return main_ref[...] + jnp.concatenate(parts, axis=0)

        dk = with_halos(dk_ref, dkh_ref, dkn_ref)
        dv = with_halos(dv_ref, dvh_ref, dvn_ref)
        dp_ref[:, 0:ATTN_W] = dq_ref[...]
        dp_ref[:, ATTN_W:ATTN_W + KV_W] = _rope(dk, c_ref[...], -a_ref[...], -b_ref[...]).astype(BF16)
        dp_ref[:, ATTN_W + KV_W:ATTN_W + 2 * KV_W] = dv.astype(BF16)
        base = ATTN_W + 2 * KV_W
        dp_ref[:, base:base + 512] = dga_ref[...]
        dp_ref[:, base + 512:base + 1024] = db_ref[...]
        dcv = dcv_ref[...]
        nxt = dcvn_ref[...] * keep
        cw = _conv_rows(cw_ref)
        du = cw[2:3, :] * dcv + cw[1:2, :] * _shift_up(dcv, nxt, 1) + cw[0:1, :] * _shift_up(dcv, nxt, 2)
        dp_ref[:, base + 1024:base + 1536] = (du * ch_ref[:, 512:1024]).astype(BF16)
        dp_ref[:, base + 1536:base + 2048] = (du * ch_ref[:, 0:512]).astype(BF16)
        dp_ref[:, base + 2048:base + 2560] = dgc_ref[...]

        xf = x_ref[...]
        r1 = lax.rsqrt(jnp.mean(xf * xf, axis=-1, keepdims=True) + EPS)
        n1 = xf * r1
        xn = (n1 * g_ref[...]).astype(BF16)
        for n in range(IN_W // 256):
            cols = slice(256 * n, 256 * (n + 1))
            acc_ref[cols, :] += lax.dot_general(dp_ref[:, cols], xn, _TN, preferred_element_type=F32)
        dxn = jnp.dot(dp_ref[...], w_ref[...], preferred_element_type=F32)
        small_ref[0:1, :] += jnp.sum(dxn * n1, axis=0, keepdims=True)
        dxg = dxn * g_ref[...]
        gx_ref[...] = r1 * (dxg - n1 * jnp.mean(dxg * n1, axis=-1, keepdims=True)) + dh2_ref[...]

        @pl.when(last)
        def _():
            pltpu.sync_copy(acc_ref, gw_hbm)

    tile = lambda w: pl.BlockSpec((tb, w), lambda i: (i, 0))
    whole = lambda r, w: pl.BlockSpec((r, w), lambda i: (0, 0))
    halo = pl.BlockSpec((per, BLOCK, KV_W), lambda i: (i, 0, 0))
    halo_next = pl.BlockSpec((1, BLOCK, KV_W), lambda i: (jnp.minimum((i + 1) * per, seq // T_MIX - 1), 0, 0))
    next8 = pl.BlockSpec((8, CONV_W), lambda i: (jnp.minimum((i + 1) * (tb // 8), seq // 8 - 1), 0))
    ch = pl.BlockSpec((tb, 1024), lambda i: (i, 1))
    return pl.pallas_call(
        body, name="bwd_proj", grid=(nt,),
        out_shape=(jax.ShapeDtypeStruct((seq, D_MODEL), F32), jax.ShapeDtypeStruct((IN_W, D_MODEL), F32),
                   jax.ShapeDtypeStruct((SMALL_ROWS, D_MODEL), F32)),
        in_specs=[tile(D_MODEL), whole(1, D_MODEL), tile(D_MODEL), tile(ATTN_W), tile(KV_W), tile(KV_W), halo, halo,
                  halo_next, halo_next,
                  tile(ATTN_W), tile(CONV_W), tile(CONV_W), tile(CONV_W), next8, ch, CONV_SPEC,
                  pl.BlockSpec((IN_W, D_MODEL), lambda i: (0, 0), pipeline_mode=pl.Buffered(1)),
                  tile(128), tile(128), tile(128)],
        out_specs=(tile(D_MODEL), pl.BlockSpec(memory_space=pl.ANY), whole(SMALL_ROWS, D_MODEL)),
        scratch_shapes=[pltpu.VMEM((tb, IN_W), BF16), pltpu.VMEM((IN_W, D_MODEL), F32)],
        compiler_params=_params(("arbitrary",)),
    )(x, norm_g, dh2, dq, dk, dv, dkh, dvh, dkh, dvh, dga, db, dgc, dcv, dcv, rest, conv_w, w_in_t,
      rope_c, rope_a, rope_b)


def _adamw_step(w, g, m, v):
    m2 = ADAM_B1 * m + (1.0 - ADAM_B1) * g
    v2 = ADAM_B2 * v + (1.0 - ADAM_B2) * jnp.square(g)
    m_hat = m2 / (1.0 - ADAM_B1 ** ADAM_STEP)
    v_hat = v2 / (1.0 - ADAM_B2 ** ADAM_STEP)
    return -ADAM_LR * (m_hat / (jnp.sqrt(v_hat) + ADAM_EPS) + ADAM_WD * w), m2, v2


def _adamw_weights(groups):
    steps = 4

    def body(*refs):
        ins, outs = refs[:4 * len(groups)], refs[4 * len(groups):]
        for k in range(len(groups)):
            res = _adamw_step(*(r[...] for r in ins[4 * k:4 * k + 4]))
            for o_ref, val in zip(outs[3 * k:3 * k + 3], res):
                o_ref[...] = val

    in_specs, out_specs, out_shape = [], [], []
    for w, _, _, _ in groups:
        rows, cols = w.shape
        spec = pl.BlockSpec((rows // steps, cols), lambda i: (i, 0))
        in_specs += [spec] * 4
        out_specs += [spec] * 3
        out_shape += [jax.ShapeDtypeStruct((rows, cols), F32)] * 3
    flat = pl.pallas_call(
        body, name="adamw_weights", grid=(steps,), out_shape=tuple(out_shape), in_specs=in_specs,
        out_specs=tuple(out_specs), compiler_params=_params(("arbitrary",)),
    )(*[a for grp in groups for a in grp])
    return [flat[3 * k:3 * k + 3] for k in range(len(groups))]


def _adamw_small(chip, small, params, m, v):
    def body(chip_ref, small_ref, conv_ref, *refs):
        ins, outs = refs[:12], refs[12:]
        outs[0][...] = jnp.sum(small_ref[6:7, :], axis=-1, keepdims=True)
        grads = (small_ref[0:1, :], small_ref[1:2, :], conv_ref[2:5, :], small_ref[5:6, 0:8])
        for k, g in enumerate(grads):
            outs[1 + k][...] = g
            res = _adamw_step(ins[k][...], g, ins[4 + k][...], ins[8 + k][...])
            for n, val in enumerate(res):
                outs[5 + 4 * n + k][...] = val

    full = lambda a: pl.BlockSpec(a.shape, lambda i, c: (0,) * len(a.shape))
    shapes = [jax.ShapeDtypeStruct(p.shape, F32) for p in params]
    outs = [jax.ShapeDtypeStruct((1, 1), F32)] + shapes * 4
    flat = pl.pallas_call(
        body, name="adamw_small",
        grid_spec=pltpu.PrefetchScalarGridSpec(
            num_scalar_prefetch=1, grid=(1,),
            in_specs=[full(small), pl.BlockSpec((SMALL_ROWS, 128), lambda i, c: (0, c[0]))]
            + [full(a) for a in (*params, *m, *v)],
            out_specs=tuple(full(s) for s in outs)),
        out_shape=tuple(outs), compiler_params=_params(("arbitrary",)),
    )(chip, small, small, *params, *m, *v)
    return flat[0], flat[1:5], [flat[5 + 4 * n:9 + 4 * n] for n in range(3)]


def kernel(x, norm_g, w_in, sinks, conv_w, w_out, final_g, loss_target, m_norm_g, m_w_in, m_sinks, m_conv_w, m_w_out, m_final_g, v_norm_g, v_w_in, v_sinks, v_conv_w, v_w_out, v_final_g):
    seq = x.shape[1]
    x2 = x.reshape(seq, D_MODEL)
    tgt = loss_target.reshape(seq, D_MODEL)
    ng = norm_g.reshape(1, D_MODEL)
    fg = final_g.reshape(1, D_MODEL)
    chip = 2 * lax.axis_index("x") + lax.axis_index("y")

    conv_w8 = jnp.zeros((8, 128), F32).at[0:3].set(conv_w)
    wi_all, wo_all, cw_all = _gather_weights(w_in.T, w_out, conv_w8)
    w_in_full = wi_all.reshape(IN_W, D_MODEL)
    w_out_full = wo_all.reshape(D_MODEL, D_MODEL)

    q, kd, vd, rest, rope_c, rope_a, rope_b = _fwd_proj(x2, ng, w_in_full)
    attn, dh2, g_wo, small_f = _fwd_mix(x2, q, kd, vd, rest, sinks, cw_all, w_out_full, fg, tgt)
    dq, dk, dv, dkh, dvh, dga, db, dgc, dcv, small_m = _bwd_mix(
        dh2, q, kd, vd, attn, rest, sinks, cw_all, w_out_full, rope_c, rope_a, rope_b)
    grad_x, g_wi, small_p = _bwd_proj(x2, ng, dh2, dq, dk, dv, dkh, dvh, dga, db, dgc, dcv, rest, cw_all,
                                      w_in_full, rope_c, rope_a, rope_b)

    g_in_blocks = g_wi.reshape(N_CHIPS, W_IN_BLK, D_MODEL)
    g_out_blocks = g_wo.reshape(N_CHIPS, W_OUT_BLK, D_MODEL)
    grad_w_in_t, grad_w_out, small = _reduce_grads(g_in_blocks, g_out_blocks, small_f, small_m, small_p)

    (upd_wi, upd_wo) = _adamw_weights([(w_in.T, grad_w_in_t, m_w_in.T, v_w_in.T),
                                       (w_out, grad_w_out, m_w_out, v_w_out)])
    row = lambda t: t.reshape(1, -1)
    loss, grads_s, upd_s = _adamw_small(
        chip.reshape(1), small, (ng, fg, conv_w, row(sinks)),
        (row(m_norm_g), row(m_final_g), m_conv_w, row(m_sinks)),
        (row(v_norm_g), row(v_final_g), v_conv_w, row(v_sinks)))

    def named(ng_, fg_, cw_, sk_, wi_t, wo_):
        return [ng_.reshape(D_MODEL), wi_t.T, sk_.reshape(8), cw_, wo_, fg_.reshape(D_MODEL)]

    g_named = named(*grads_s, grad_w_in_t, grad_w_out)
    out = [loss.reshape(()), grad_x.reshape(1, seq, D_MODEL)] + g_named
    for n in range(3):
        out += named(*upd_s[n], upd_wi[n], upd_wo[n])
    return tuple(out)
```

```python
import jax
import jax.numpy as jnp
from jax import lax
from jax.experimental import pallas as pl
from jax.experimental.pallas import tpu as pltpu

F32 = jnp.float32
BF16 = jnp.bfloat16

D_MODEL = 1024
HEAD_DIM = 64
ATTN_W = 512
KV_W = 128
CONV_W = 512
IN_W = 3328
REST_W = IN_W - ATTN_W - 2 * KV_W
BLOCK = 128
ROT_DIM = 16
ROPE_THETA = 500000.0
EPS = 1e-5
SCALE = 0.125
NEG = -1e30

N_CHIPS = 4
W_IN_BLK = IN_W // N_CHIPS
W_OUT_BLK = D_MODEL // N_CHIPS

ADAM_LR = 0.001
ADAM_B1 = 0.9
ADAM_B2 = 0.999
ADAM_EPS = 1e-08
ADAM_WD = 0.01
ADAM_STEP = 10

VMEM_LIMIT = 56 * 1024 * 1024
T_PROJ = 512
T_FMIX = 512
T_MIX = 512
SMALL_ROWS = 8
MESH = pl.DeviceIdType.MESH

_NT = (((1,), (1,)), ((), ()))
_TN = (((0,), (0,)), ((), ()))


def _params(sem=None):
    kw = dict(vmem_limit_bytes=VMEM_LIMIT)
    if sem is not None:
        kw["dimension_semantics"] = sem
    return pltpu.CompilerParams(**kw)


def _sigmoid(t):
    return 1.0 / (1.0 + jnp.exp(-t))


def _shift_down(t, prev8, k):
    rolled = pltpu.roll(t, k, 0)
    row = lax.broadcasted_iota(jnp.int32, t.shape, 0)
    for j in range(k):
        rolled = jnp.where(row == j, prev8[8 - k + j:8 - k + j + 1, :], rolled)
    return rolled


def _shift_up(t, next8, k):
    n = t.shape[0]
    rolled = pltpu.roll(t, n - k, 0)
    row = lax.broadcasted_iota(jnp.int32, t.shape, 0)
    for j in range(k):
        rolled = jnp.where(row == n - k + j, next8[j:j + 1, :], rolled)
    return rolled


def _rope(t, c, a, b):
    w = t.shape[1]
    reps = w // 128
    if reps > 1:
        c, a, b = (jnp.concatenate([z] * reps, axis=1) for z in (c, a, b))
    return t * c + pltpu.roll(t, w - 8, 1) * a + pltpu.roll(t, 8, 1) * b


def _lane_lo(shape):
    return lax.broadcasted_iota(jnp.int32, shape, 1) < HEAD_DIM


def _stack_heads(t, g):
    lo = _lane_lo((BLOCK, 128))
    parts = []
    for hh in range(4):
        pair = t[:, 256 * g + 128 * (hh // 2):256 * g + 128 * (hh // 2) + 128]
        keep = lo if hh % 2 == 0 else jnp.logical_not(lo)
        parts.append(jnp.where(keep, pair, jnp.zeros_like(pair)))
    return jnp.concatenate(parts, axis=0)


def _unstack_pair(o, pp):
    lo = _lane_lo((BLOCK, 128))
    return jnp.where(lo, o[256 * pp:256 * pp + 128], o[256 * pp + 128:256 * pp + 256])


def _band_masks(has_prev):
    r = lax.broadcasted_iota(jnp.int32, (4 * BLOCK, 2 * BLOCK), 0) % BLOCK
    kj = lax.broadcasted_iota(jnp.int32, (4 * BLOCK, 2 * BLOCK), 1)
    cur = (kj >= BLOCK) & (kj - BLOCK <= r)
    prev = (kj < BLOCK) & (kj > r)
    return cur | (prev & has_prev), cur | prev


def _sink_col(sinks_ref, g):
    r = lax.broadcasted_iota(jnp.int32, (4 * BLOCK, 1), 0) // BLOCK
    col = jnp.full((4 * BLOCK, 1), sinks_ref[4 * g + 3], F32)
    for hh in range(3):
        col = jnp.where(r == hh, sinks_ref[4 * g + hh], col)
    return col


def _probs(qs, kd, sink_col, mask):
    return _softmax(lax.dot_general(qs, kd, _NT, preferred_element_type=F32), sink_col, mask)


def _softmax(s, sink_col, mask):
    s = jnp.where(mask, s, NEG)
    m = jnp.maximum(jnp.max(s, axis=-1, keepdims=True), sink_col)
    p = jnp.exp(s - m)
    es = jnp.exp(sink_col - m)
    inv = 1.0 / (jnp.sum(p, axis=-1, keepdims=True) + es)
    return p * inv, es * inv


def _key_windows(i, nsub, kd_ref, vd_ref, kdp_ref, vdp_ref):
    mask_first, mask_rest = _band_masks(i > 0)
    out = []
    for sb in range(nsub):
        rows = slice(BLOCK * sb, BLOCK * (sb + 1))
        if sb == 0:
            kk = jnp.concatenate([kdp_ref[...], kd_ref[rows, :]], axis=0)
            vv = jnp.concatenate([vdp_ref[...], vd_ref[rows, :]], axis=0)
            out.append((rows, kk, vv, mask_first))
        else:
            both = slice(BLOCK * (sb - 1), BLOCK * (sb + 1))
            out.append((rows, kd_ref[both, :], vd_ref[both, :], mask_rest))
    return out


def _gather_w_in(w_in_t):
    hi = W_IN_BLK // 2

    def body(wi_ref, wi_all, send_sems, recv_sems):
        x, y, c = lax.axis_index("x"), lax.axis_index("y"), lax.axis_index("c")
        sibling = (x, y, 1 - c)
        chips = [(1 - x, y), (x, 1 - y), (1 - x, 1 - y)]

        wi_all[2 * x + y] = wi_ref[...].astype(BF16)

        def copy(k, chip, half, to):
            r = wi_all.at[2 * chip[0] + chip[1], pl.ds(half * hi, hi)]
            return pltpu.make_async_remote_copy(src_ref=r, dst_ref=r, send_sem=send_sems.at[k],
                                                recv_sem=recv_sems.at[k], device_id=to, device_id_type=MESH)

        first = [copy(k, (x, y), c, (*chip, c)) for k, chip in enumerate(chips)]
        for cp in first:
            cp.start()
        passed = []
        for k, chip in enumerate(chips):
            copy(k, chip, c, (x, y, c)).wait_recv()
            passed.append(copy(3 + k, chip, c, sibling))
            passed[-1].start()
        for k, chip in enumerate(chips):
            copy(3 + k, chip, 1 - c, (x, y, c)).wait_recv()
        for cp in first + passed:
            cp.wait_send()

    vmem = pl.BlockSpec(memory_space=pltpu.VMEM)
    return pl.pallas_call(
        body, name="gather_w_in",
        out_shape=jax.ShapeDtypeStruct((N_CHIPS, W_IN_BLK, D_MODEL), BF16),
        in_specs=[vmem], out_specs=vmem,
        scratch_shapes=[pltpu.SemaphoreType.DMA((6,)), pltpu.SemaphoreType.DMA((6,))],
        compiler_params=_params(),
    )(w_in_t)


def _reduce_grads(g_in, *smalls):
    hi = W_IN_BLK // 2

    def body(gi_hbm, s0_ref, s1_ref, s2_ref, gi_out, small_out,
             mine_i, sib_i, out_i, ici_i, small_in, small_ref, send_sems, recv_sems, local_sems):
        x, y, c = lax.axis_index("x"), lax.axis_index("y"), lax.axis_index("c")
        my_dev = 4 * x + 2 * y + c
        sibling = (x, y, 1 - c)
        chips = [(1 - x, y), (x, 1 - y), (1 - x, 1 - y)]
        order = chips + [(x, y)]

        def remote(k, src, dst, to):
            return pltpu.make_async_remote_copy(src_ref=src, dst_ref=dst, send_sem=send_sems.at[k],
                                                recv_sem=recv_sems.at[k], device_id=to, device_id_type=MESH)

        small_ref[...] = s0_ref[...] + s1_ref[...] + s2_ref[...]
        small_cps = []
        for f in range(1, 8):
            fx, fy, fc = f >> 2, (f >> 1) & 1, f & 1
            small_cps.append(remote(8 + f - 1, small_ref, small_in.at[f - 1], (x ^ fx, y ^ fy, c ^ fc)))
        for cp in small_cps:
            cp.start()

        own, to_sib = [], []
        for n, chip in enumerate(order):
            j = 2 * chip[0] + chip[1]
            own.append(pltpu.make_async_copy(gi_hbm.at[j, pl.ds(c * hi, hi)], mine_i.at[n], local_sems.at[n]))
            to_sib.append(remote(n, gi_hbm.at[j, pl.ds((1 - c) * hi, hi)], sib_i.at[n], sibling))
            own[-1].start()
            to_sib[-1].start()

        ici = []
        for k, chip in enumerate(chips):
            own[k].wait()
            to_sib[k].wait_recv()
            out_i[k] = (mine_i[k] + sib_i[k]).astype(BF16)
            ici.append(remote(4 + k, out_i.at[k], ici_i.at[k], (*chip, c)))
            ici[-1].start()
        own[3].wait()
        to_sib[3].wait_recv()
        tot_i = mine_i[3] + sib_i[3]
        for k in range(3):
            ici[k].wait_recv()
            tot_i = tot_i + ici_i[k].astype(F32)
        gi_out[pl.ds(c * hi, hi), :] = tot_i

        swap = [remote(7, gi_out.at[pl.ds(c * hi, hi)], gi_out.at[pl.ds(c * hi, hi)], sibling)]
        for cp in swap:
            cp.start()

        for cp in small_cps:
            cp.wait_recv()
        total = jnp.zeros((SMALL_ROWS, D_MODEL), F32)
        for d in range(8):
            slot = jnp.maximum((d ^ my_dev) - 1, 0)
            total = total + jnp.where(d == my_dev, small_ref[...], small_in[slot])
        small_out[...] = total

        remote(7, gi_out.at[pl.ds((1 - c) * hi, hi)], gi_out.at[pl.ds((1 - c) * hi, hi)], sibling).wait_recv()
        for cp in to_sib + ici + swap + small_cps:
            cp.wait_send()

    vmem = pl.BlockSpec(memory_space=pltpu.VMEM)
    anyspace = pl.BlockSpec(memory_space=pl.ANY)
    return pl.pallas_call(
        body, name="reduce_grads",
        out_shape=(jax.ShapeDtypeStruct((W_IN_BLK, D_MODEL), F32), jax.ShapeDtypeStruct((SMALL_ROWS, D_MODEL), F32)),
        in_specs=[anyspace, vmem, vmem, vmem], out_specs=(vmem, vmem),
        scratch_shapes=[pltpu.VMEM((N_CHIPS, hi, D_MODEL), F32), pltpu.VMEM((N_CHIPS, hi, D_MODEL), F32),
                        pltpu.VMEM((3, hi, D_MODEL), BF16), pltpu.VMEM((3, hi, D_MODEL), BF16),
                        pltpu.VMEM((7, SMALL_ROWS, D_MODEL), F32), pltpu.VMEM((SMALL_ROWS, D_MODEL), F32),
                        pltpu.SemaphoreType.DMA((15,)), pltpu.SemaphoreType.DMA((15,)),
                        pltpu.SemaphoreType.DMA((4,))],
        compiler_params=_params(),
    )(g_in, *smalls)


def _fwd_proj(x, norm_g, w_in_t, w_out, conv_w8):
    seq = x.shape[0]
    nt = seq // T_PROJ
    lane = jnp.arange(128, dtype=jnp.int32) % HEAD_DIM
    inv_freq = ROPE_THETA ** (-(2 * (lane % 8)).astype(F32) / ROT_DIM)
    inv_freq = jnp.where(lane < ROT_DIM, inv_freq, 0.0).reshape(1, 128)
    in_tile = jnp.arange(T_PROJ, dtype=jnp.int32).astype(F32)[:, None] * inv_freq
    cos_in, sin_in = jnp.cos(in_tile), jnp.sin(in_tile)
    start = jnp.repeat((jnp.arange(nt, dtype=jnp.int32) * T_PROJ).astype(F32), 8)[:, None] * inv_freq
    cos_st, sin_st = jnp.cos(start), jnp.sin(start)

    def body(x_ref, g_ref, w_ref, cs_ref, ss_ref, ci_ref, si_ref, wo_ref, cw_ref,
             q_ref, kd_ref, vd_ref, rest_ref, c_ref, a_ref, b_ref, wo_all, cw_all,
             wo_stage, send_sems, recv_sems, local_sems):
        i = pl.program_id(0)
        mx, my, mc = lax.axis_index("x"), lax.axis_index("y"), lax.axis_index("c")
        chips = [(1 - mx, my), (mx, 1 - my), (1 - mx, 1 - my)]

        def gather(blocks):
            cps = []
            for k, chip in enumerate(chips):
                for n, (src, dst) in enumerate(((wo_stage, wo_all), (cw_ref, cw_all))):
                    cps.append(pltpu.make_async_remote_copy(
                        src_ref=src, dst_ref=dst.at[blocks[k]], send_sem=send_sems.at[2 * k + n],
                        recv_sem=recv_sems.at[2 * k + n], device_id=(*chip, mc), device_id_type=MESH))
            return cps

        me = 2 * mx + my
        own = [pltpu.make_async_copy(wo_stage, wo_all.at[me], local_sems.at[0]),
               pltpu.make_async_copy(cw_ref, cw_all.at[me], local_sems.at[1])]

        @pl.when(i == 0)
        def _():
            wo_stage[...] = wo_ref[...].astype(BF16)
            for cp in own + gather([me] * 3):
                cp.start()

        xf = x_ref[...]
        r1 = lax.rsqrt(jnp.mean(xf * xf, axis=-1, keepdims=True) + EPS)
        xn = (xf * r1 * g_ref[...]).astype(BF16)
        cs, ss = cs_ref[0:1, :], ss_ref[0:1, :]
        c = cs * ci_ref[...] - ss * si_ref[...]
        sin = ss * ci_ref[...] + cs * si_ref[...]
        j = lax.broadcasted_iota(jnp.int32, (T_PROJ, 128), 1) % HEAD_DIM
        a = jnp.where(j < 8, -sin, 0.0)
        b = jnp.where(j >= 8, sin, 0.0)
        c_ref[...], a_ref[...], b_ref[...] = c, a, b
        proj = lambda lo_c, w: lax.dot_general(xn, w_ref[lo_c:lo_c + w, :], _NT, preferred_element_type=F32)
        q_ref[...] = (_rope(proj(0, ATTN_W), c, a, b) * SCALE).astype(BF16)
        kv = proj(ATTN_W, 2 * KV_W)
        k = _rope(kv[:, 0:KV_W], c, a, b)
        v = kv[:, KV_W:2 * KV_W]
        lo = _lane_lo(k.shape)
        for t, ref in ((k, kd_ref), (v, vd_ref)):
            sw = pltpu.roll(t, HEAD_DIM, 1)
            ref[:, 0:128] = jnp.where(lo, t, sw).astype(BF16)
            ref[:, 128:256] = jnp.where(lo, sw, t).astype(BF16)
        for n in range(REST_W // 512):
            rest_ref[:, 512 * n:512 * (n + 1)] = proj(ATTN_W + 2 * KV_W + 512 * n, 512)

        @pl.when(i == nt - 1)
        def _():
            sent = gather([me] * 3)
            for cp in gather([2 * chip[0] + chip[1] for chip in chips]):
                cp.wait_recv()
            for cp in sent:
                cp.wait_send()
            for cp in own:
                cp.wait()

    tile = lambda w: pl.BlockSpec((T_PROJ, w), lambda i: (i, 0))
    whole = lambda r, w: pl.BlockSpec((r, w), lambda i: (0, 0))
    vmem = pl.BlockSpec(memory_space=pltpu.VMEM)
    hbm = pl.BlockSpec(memory_space=pl.ANY)
    return pl.pallas_call(
        body, name="fwd_proj", grid=(nt,),
        out_shape=(jax.ShapeDtypeStruct((seq, ATTN_W), BF16), jax.ShapeDtypeStruct((seq, 2 * KV_W), BF16),
                   jax.ShapeDtypeStruct((seq, 2 * KV_W), BF16), jax.ShapeDtypeStruct((seq, REST_W), F32))
        + (jax.ShapeDtypeStruct((seq, 128), F32),) * 3
        + (jax.ShapeDtypeStruct((N_CHIPS, W_OUT_BLK, D_MODEL), BF16), jax.ShapeDtypeStruct((N_CHIPS, 8, 128), F32)),
        in_specs=[tile(D_MODEL), whole(1, D_MODEL), whole(IN_W, D_MODEL), pl.BlockSpec((8, 128), lambda i: (i, 0)),
                  pl.BlockSpec((8, 128), lambda i: (i, 0)), whole(T_PROJ, 128), whole(T_PROJ, 128), vmem, vmem],
        out_specs=(tile(ATTN_W), tile(2 * KV_W), tile(2 * KV_W), tile(REST_W), tile(128), tile(128), tile(128),
                   hbm, hbm),
        scratch_shapes=[pltpu.VMEM((W_OUT_BLK, D_MODEL), BF16), pltpu.SemaphoreType.DMA((6,)),
                        pltpu.SemaphoreType.DMA((6,)), pltpu.SemaphoreType.DMA((2,))],
        compiler_params=_params(("arbitrary",)),
    )(x, norm_g, w_in_t, cos_st, sin_st, cos_in, sin_in, w_out, conv_w8)


CONV_SPEC = pl.BlockSpec((N_CHIPS, 8, 128), lambda i: (0, 0, 0))


def _conv_rows(cw_ref):
    return jnp.concatenate([cw_ref[j] for j in range(N_CHIPS)], axis=1)


def _conv_parts(rest_ref, prev_ref, cw_ref, first):
    u = rest_ref[:, 1024:1536] * rest_ref[:, 1536:2048]
    up = prev_ref[:, 1024:1536] * prev_ref[:, 1536:2048]
    up = jnp.where(first, jnp.zeros_like(up), up)
    um1 = _shift_down(u, up, 1)
    um2 = _shift_down(u, up, 2)
    cw = _conv_rows(cw_ref)
    cv = cw[0:1, :] * um2 + cw[1:2, :] * um1 + cw[2:3, :] * u
    return u, um1, um2, cv


def _fwd_mix(x, q, kd, vd, rest, sinks, conv_w, w_out, final_g, target):
    seq = x.shape[0]
    nt = seq // T_FMIX
    nsub = T_FMIX // BLOCK

    def body(sinks_ref, x_ref, q_ref, kd_ref, vd_ref, kdp_ref, vdp_ref, rest_ref, restp_ref, cw_ref, wo_ref,
             fg_ref, tgt_ref, attn_ref, dh2_ref, gwo_ref, gwob_ref, small_ref, mix_ref, pmix_ref, pdh2_ref):
        i = pl.program_id(0)

        @pl.when(i == 0)
        def _():
            small_ref[...] = jnp.zeros_like(small_ref)
            gwo_ref[...] = jnp.zeros_like(gwo_ref)
            pmix_ref[...] = jnp.zeros_like(pmix_ref)
            pdh2_ref[...] = jnp.zeros_like(pdh2_ref)

        gwo_ref[...] += lax.dot_general(pmix_ref[...], pdh2_ref[...], _TN, preferred_element_type=F32)

        chains = []
        for rows, kk, vv, mask in _key_windows(i, nsub, kd_ref, vd_ref, kdp_ref, vdp_ref):
            qt = q_ref[rows, :]
            for g in range(2):
                kg = kk[:, 128 * g:128 * (g + 1)]
                chains.append(dict(g=g, rows=rows, mask=mask, vg=vv[:, 128 * g:128 * (g + 1)],
                                   s=lax.dot_general(_stack_heads(qt, g), kg, _NT, preferred_element_type=F32)))
        for ch in chains:
            ch["prob"], _ = _softmax(ch.pop("s"), _sink_col(sinks_ref, ch["g"]), ch["mask"])
        for ch in chains:
            o = jnp.dot(ch["prob"].astype(BF16), ch["vg"], preferred_element_type=F32)
            for pp in range(2):
                lanes = slice(256 * ch["g"] + 128 * pp, 256 * ch["g"] + 128 * (pp + 1))
                attn_ref[ch["rows"], lanes] = _unstack_pair(o, pp)

        ga = rest_ref[:, 0:512]
        mix_ref[:, 0:ATTN_W] = (attn_ref[...] * (ga * _sigmoid(ga))).astype(BF16)
        _, _, _, cv = _conv_parts(rest_ref, restp_ref, cw_ref, i == 0)
        gc = rest_ref[:, 2048:2560]
        mix_ref[:, ATTN_W:] = (rest_ref[:, 512:1024] * cv * (gc * _sigmoid(gc))).astype(BF16)

        h2 = x_ref[...] + jnp.dot(mix_ref[...], wo_ref[...], preferred_element_type=F32)
        r2 = lax.rsqrt(jnp.mean(h2 * h2, axis=-1, keepdims=True) + EPS)
        n2 = h2 * r2
        err = n2 * fg_ref[...] - tgt_ref[...]
        dy = err * (1.0 / D_MODEL)
        small_ref[6:7, :] += jnp.sum(err * err, axis=0, keepdims=True) * (0.5 / D_MODEL)
        small_ref[1:2, :] += jnp.sum(dy * n2, axis=0, keepdims=True)
        dn = dy * fg_ref[...]
        dh2 = r2 * (dn - n2 * jnp.mean(dn * n2, axis=-1, keepdims=True))
        dh2_ref[...] = dh2
        pmix_ref[...] = mix_ref[...]
        pdh2_ref[...] = dh2.astype(BF16)

        @pl.when(i == nt - 1)
        def _():
            gwo_ref[...] += lax.dot_general(pmix_ref[...], pdh2_ref[...], _TN, preferred_element_type=F32)
            gwob_ref[...] = gwo_ref[...].astype(BF16)

    tile = lambda w: pl.BlockSpec((T_FMIX, w), lambda i: (i, 0))
    whole = lambda r, w: pl.BlockSpec((r, w), lambda i: (0, 0))
    prev_blk = pl.BlockSpec((BLOCK, 2 * KV_W), lambda i: (jnp.maximum(i * nsub - 1, 0), 0))
    prev8 = pl.BlockSpec((8, REST_W), lambda i: (jnp.maximum(i * (T_FMIX // 8) - 1, 0), 0))
    return pl.pallas_call(
        body, name="fwd_mix", grid=(nt,),
        out_shape=(jax.ShapeDtypeStruct((seq, ATTN_W), F32), jax.ShapeDtypeStruct((seq, D_MODEL), F32),
                   jax.ShapeDtypeStruct((D_MODEL, D_MODEL), F32), jax.ShapeDtypeStruct((D_MODEL, D_MODEL), BF16),
                   jax.ShapeDtypeStruct((SMALL_ROWS, D_MODEL), F32)),
        in_specs=[pl.BlockSpec(memory_space=pltpu.SMEM), tile(D_MODEL), tile(ATTN_W), tile(2 * KV_W), tile(2 * KV_W),
                  prev_blk, prev_blk, tile(REST_W), prev8, CONV_SPEC, whole(D_MODEL, D_MODEL),
                  whole(1, D_MODEL), tile(D_MODEL)],
        out_specs=(tile(ATTN_W), tile(D_MODEL), whole(D_MODEL, D_MODEL), whole(D_MODEL, D_MODEL),
                   whole(SMALL_ROWS, D_MODEL)),
        scratch_shapes=[pltpu.VMEM((T_FMIX, D_MODEL), BF16)] * 3,
        compiler_params=_params(("arbitrary",)),
    )(sinks, x, q, kd, vd, kd, vd, rest, rest, conv_w, w_out, final_g, target)


def _scatter_copies(g_hbm, gb_hbm, mine, land, send_sems, recv_sems, local_sem, half):
    x, y, c = lax.axis_index("x"), lax.axis_index("y"), lax.axis_index("c")
    cps = []
    for f in range(1, 8):
        to = (x ^ (f >> 2), y ^ ((f >> 1) & 1), c ^ (f & 1))
        src = gb_hbm.at[2 * to[0] + to[1], pl.ds(to[2] * half, half)]
        cps.append(pltpu.make_async_remote_copy(src_ref=src, dst_ref=land.at[f - 1], send_sem=send_sems.at[f - 1],
                                                recv_sem=recv_sems.at[f - 1], device_id=to, device_id_type=MESH))
    own = pltpu.make_async_copy(g_hbm.at[2 * x + y, pl.ds(c * half, half)], mine, local_sem)
    return cps, own


def _scatter_finish(cps, own, mine, land, out_hbm, send_sems, recv_sems, local_sem, half):
    x, y, c = lax.axis_index("x"), lax.axis_index("y"), lax.axis_index("c")
    own.wait()
    tot = mine[...]
    for f in range(1, 8):
        cps[f - 1].wait_recv()
        tot = tot + land[f - 1].astype(F32)
    mine[...] = tot

    def swap(rows_of):
        return pltpu.make_async_remote_copy(src_ref=mine, dst_ref=out_hbm.at[pl.ds(rows_of * half, half)],
                                            send_sem=send_sems.at[7], recv_sem=recv_sems.at[7],
                                            device_id=(x, y, 1 - c), device_id_type=MESH)

    keep = pltpu.make_async_copy(mine, out_hbm.at[pl.ds(c * half, half)], local_sem)
    keep.start()
    swap(c).start()
    swap(1 - c).wait_recv()
    keep.wait()
    for cp in cps:
        cp.wait_send()
    swap(c).wait_send()


def _bwd_mix(dh2, q, kd, vd, attn, rest, sinks, conv_w, w_out, rope_c, rope_a, rope_b, g_out, g_out_b):
    seq = dh2.shape[0]
    nt = seq // T_MIX
    nsub = T_MIX // BLOCK
    ho = W_OUT_BLK // 2

    def body(sinks_ref, dh2_ref, q_ref, kd_ref, vd_ref, kdp_ref, vdp_ref, attn_ref, rest_ref, restp_ref,
             cw_ref, wo_ref, c_ref, a_ref, b_ref, go_hbm, gob_hbm,
             dq_ref, dk_ref, dv_ref, dkh_ref, dvh_ref, dga_ref, db_ref, dgc_ref, dcv_ref, small_ref, go_out,
             dmix_ref, dsink_ref, mine_o, land_o, send_sems, recv_sems, local_sems):
        i = pl.program_id(0)
        scatter = (mine_o, land_o, send_sems, recv_sems, local_sems.at[0], ho)

        @pl.when(i == 0)
        def _():
            small_ref[...] = jnp.zeros_like(small_ref)
            dsink_ref[...] = jnp.zeros_like(dsink_ref)
            cps, own = _scatter_copies(go_hbm, gob_hbm, *scatter)
            for cp in cps + [own]:
                cp.start()

        dmix_ref[...] = lax.dot_general(dh2_ref[...].astype(BF16), wo_ref[...], _NT, preferred_element_type=F32)

        ga = rest_ref[:, 0:512]
        sg = _sigmoid(ga)
        dma = dmix_ref[:, 0:ATTN_W]
        dga_ref[...] = (dma * attn_ref[...] * (sg * (1.0 + ga * (1.0 - sg)))).astype(BF16)
        dmix_ref[:, 0:ATTN_W] = dma * (ga * sg)

        u, um1, um2, cv = _conv_parts(rest_ref, restp_ref, cw_ref, i == 0)
        gc = rest_ref[:, 2048:2560]
        sc = _sigmoid(gc)
        bg = rest_ref[:, 512:1024]
        dmc = dmix_ref[:, ATTN_W:]
        t1 = dmc * (gc * sc)
        db_ref[...] = (t1 * cv).astype(BF16)
        dcv = t1 * bg
        dcv_ref[...] = dcv
        dgc_ref[...] = (dmc * (bg * cv) * (sc * (1.0 + gc * (1.0 - sc)))).astype(BF16)
        small_ref[2:3, 0:CONV_W] += jnp.sum(dcv * um2, axis=0, keepdims=True)
        small_ref[3:4, 0:CONV_W] += jnp.sum(dcv * um1, axis=0, keepdims=True)
        small_ref[4:5, 0:CONV_W] += jnp.sum(dcv * u, axis=0, keepdims=True)

        lo = _lane_lo((2 * BLOCK, 128))
        dk_blocks = [None] * (nsub + 1)
        dv_blocks = [None] * (nsub + 1)

        def add(lst, n, val):
            lst[n] = val if lst[n] is None else lst[n] + val

        chains = []
        for rows, kk, vv, mask in _key_windows(i, nsub, kd_ref, vd_ref, kdp_ref, vdp_ref):
            qt = q_ref[rows, :]
            dot = dmix_ref[rows, 0:ATTN_W].astype(BF16)
            for g in range(2):
                chains.append(dict(g=g, rows=rows, mask=mask, qs=_stack_heads(qt, g), dos=_stack_heads(dot, g),
                                   kg=kk[:, 128 * g:128 * (g + 1)], vg=vv[:, 128 * g:128 * (g + 1)]))
        for ch in chains:
            ch["prob"], ch["psink"] = _probs(ch["qs"], ch["kg"], _sink_col(sinks_ref, ch["g"]), ch["mask"])
        for ch in chains:
            ch["dp"] = lax.dot_general(ch["dos"], ch["vg"], _NT, preferred_element_type=F32)
        for ch in chains:
            rs = jnp.sum(ch["prob"] * ch["dp"], axis=-1, keepdims=True)
            ch["ds"] = (ch["prob"] * (ch["dp"] - rs)).astype(BF16)
            dsink_ref[ch["g"]] += -ch["psink"] * rs
        for ch in chains:
            dqs = jnp.dot(ch["ds"], ch["kg"], preferred_element_type=F32) * SCALE
            c, a, b = c_ref[ch["rows"], :], a_ref[ch["rows"], :], b_ref[ch["rows"], :]
            for pp in range(2):
                lanes = slice(256 * ch["g"] + 128 * pp, 256 * ch["g"] + 128 * (pp + 1))
                dq_ref[ch["rows"], lanes] = _rope(_unstack_pair(dqs, pp), c, -a, -b).astype(BF16)
            dkd = lax.dot_general(ch["ds"], ch["qs"], _TN, preferred_element_type=F32)
            dvd = lax.dot_general(ch["prob"].astype(BF16), ch["dos"], _TN, preferred_element_type=F32)
            ch["dk"] = dkd + pltpu.roll(dkd, HEAD_DIM, 1)
            ch["dv"] = dvd + pltpu.roll(dvd, HEAD_DIM, 1)
        for sb in range(nsub):
            dk2 = jnp.where(lo, chains[2 * sb]["dk"], chains[2 * sb + 1]["dk"])
            dv2 = jnp.where(lo, chains[2 * sb]["dv"], chains[2 * sb + 1]["dv"])
            add(dk_blocks, sb, dk2[0:BLOCK])
            add(dk_blocks, sb + 1, dk2[BLOCK:])
            add(dv_blocks, sb, dv2[0:BLOCK])
            add(dv_blocks, sb + 1, dv2[BLOCK:])
        dkh_ref[0] = dk_blocks[0]
        dvh_ref[0] = dv_blocks[0]
        for sb in range(nsub):
            dk_ref[BLOCK * sb:BLOCK * (sb + 1), :] = dk_blocks[sb + 1]
            dv_ref[BLOCK * sb:BLOCK * (sb + 1), :] = dv_blocks[sb + 1]

        @pl.when(i == nt - 1)
        def _():
            for h in range(8):
                tot = jnp.sum(dsink_ref[h // 4, BLOCK * (h % 4):BLOCK * (h % 4 + 1), :], axis=0, keepdims=True)
                small_ref[5:6, h:h + 1] = tot
            cps, own = _scatter_copies(go_hbm, gob_hbm, *scatter)
            _scatter_finish(cps, own, mine_o, land_o, go_out, send_sems, recv_sems, local_sems.at[1], ho)

    tile = lambda w: pl.BlockSpec((T_MIX, w), lambda i: (i, 0))
    whole = lambda r, w: pl.BlockSpec((r, w), lambda i: (0, 0))
    prev_blk = pl.BlockSpec((BLOCK, 2 * KV_W), lambda i: (jnp.maximum(i * nsub - 1, 0), 0))
    prev8 = pl.BlockSpec((8, REST_W), lambda i: (jnp.maximum(i * (T_MIX // 8) - 1, 0), 0))
    halo = pl.BlockSpec((1, BLOCK, KV_W), lambda i: (i, 0, 0))
    hbm = pl.BlockSpec(memory_space=pl.ANY)
    bf = lambda w: jax.ShapeDtypeStruct((seq, w), BF16)
    f32 = lambda w: jax.ShapeDtypeStruct((seq, w), F32)
    return pl.pallas_call(
        body, name="bwd_mix", grid=(nt,),
        out_shape=(bf(ATTN_W), f32(KV_W), f32(KV_W), jax.ShapeDtypeStruct((nt, BLOCK, KV_W), F32),
                   jax.ShapeDtypeStruct((nt, BLOCK, KV_W), F32), bf(ATTN_W), bf(CONV_W), bf(CONV_W), f32(CONV_W),
                   jax.ShapeDtypeStruct((SMALL_ROWS, D_MODEL), F32), jax.ShapeDtypeStruct((W_OUT_BLK, D_MODEL), F32)),
        in_specs=[pl.BlockSpec(memory_space=pltpu.SMEM), tile(D_MODEL), tile(ATTN_W), tile(2 * KV_W), tile(2 * KV_W),
                  prev_blk, prev_blk, tile(ATTN_W), tile(REST_W), prev8, CONV_SPEC,
                  whole(D_MODEL, D_MODEL), tile(128), tile(128), tile(128), hbm, hbm],
        out_specs=(tile(ATTN_W), tile(KV_W), tile(KV_W), halo, halo, tile(ATTN_W), tile(CONV_W), tile(CONV_W),
                   tile(CONV_W), whole(SMALL_ROWS, D_MODEL), hbm),
        scratch_shapes=[pltpu.VMEM((T_MIX, D_MODEL), F32), pltpu.VMEM((2, 4 * BLOCK, 1), F32),
                        pltpu.VMEM((ho, D_MODEL), F32), pltpu.VMEM((7, ho, D_MODEL), BF16),
                        pltpu.SemaphoreType.DMA((8,)), pltpu.SemaphoreType.DMA((8,)), pltpu.SemaphoreType.DMA((2,))],
        compiler_params=_params(("arbitrary",)),
    )(sinks, dh2, q, kd, vd, kd, vd, attn, rest, rest, conv_w, w_out, rope_c, rope_a, rope_b, g_out, g_out_b)


def _bwd_proj(x, norm_g, dh2, dq, dk, dv, dkh, dvh, dga, db, dgc, dcv, rest, conv_w, w_in_t, rope_c, rope_a, rope_b):
    seq = x.shape[0]
    tb = T_PROJ
    per = tb // T_MIX
    nt = seq // tb

    def body(x_ref, g_ref, dh2_ref, dq_ref, dk_ref, dv_ref, dkh_ref, dvh_ref, dkn_ref, dvn_ref, dga_ref, db_ref,
             dgc_ref, dcv_ref, dcvn_ref, ch_ref, cw_ref, w_ref, c_ref, a_ref, b_ref, gx_ref, gw_hbm, small_ref,
             dp_ref, acc_ref):
        i = pl.program_id(0)

        @pl.when(i == 0)
        def _():
            small_ref[...] = jnp.zeros_like(small_ref)
            acc_ref[...] = jnp.zeros_like(acc_ref)

        last = i == nt - 1
        keep = jnp.where(last, 0.0, 1.0)
        pad = jnp.zeros((T_MIX - BLOCK, KV_W), F32)

        def with_halos(main_ref, halo_ref, next_ref):
            parts = []
            for m in range(1, per + 1):
                parts += [pad, halo_ref[m] if m < per else next_ref[0] * keep]
            return main_ref[...] + jnp.concatenate(parts, axis=0)

        dk = with_halos(dk_ref, dkh_ref, dkn_ref)
        dv = with_halos(dv_ref, dvh_ref, dvn_ref)
        dp_ref[:, 0:ATTN_W] = dq_ref[...]
        dp_ref[:, ATTN_W:ATTN_W + KV_W] = _rope(dk, c_ref[...], -a_ref[...], -b_ref[...]).astype(BF16)
        dp_ref[:, ATTN_W + KV_W:ATTN_W + 2 * KV_W] = dv.astype(BF16)
        base = ATTN_W + 2 * KV_W
        dp_ref[:, base:base + 512] = dga_ref[...]
        dp_ref[:, base + 512:base + 1024] = db_ref[...]
        dcv = dcv_ref[...]
        nxt = dcvn_ref[...] * keep
        cw = _conv_rows(cw_ref)
        du = cw[2:3, :] * dcv + cw[1:2, :] * _shift_up(dcv, nxt, 1) + cw[0:1, :] * _shift_up(dcv, nxt, 2)
        dp_ref[:, base + 1024:base + 1536] = (du * ch_ref[:, 512:1024]).astype(BF16)
        dp_ref[:, base + 1536:base + 2048] = (du * ch_ref[:, 0:512]).astype(BF16)
        dp_ref[:, base + 2048:base + 2560] = dgc_ref[...]

        xf = x_ref[...]
        r1 = lax.rsqrt(jnp.mean(xf * xf, axis=-1, keepdims=True) + EPS)
        n1 = xf * r1
        xn = (n1 * g_ref[...]).astype(BF16)
        for n in range(IN_W // 256):
            cols = slice(256 * n, 256 * (n + 1))
            acc_ref[cols, :] += lax.dot_general(dp_ref[:, cols], xn, _TN, preferred_element_type=F32)
        dxn = jnp.dot(dp_ref[...], w_ref[...], preferred_element_type=F32)
        small_ref[0:1, :] += jnp.sum(dxn * n1, axis=0, keepdims=True)
        dxg = dxn * g_ref[...]
        gx_ref[...] = r1 * (dxg - n1 * jnp.mean(dxg * n1, axis=-1, keepdims=True)) + dh2_ref[...]

        @pl.when(last)
        def _():
            pltpu.sync_copy(acc_ref, gw_hbm)

    tile = lambda w: pl.BlockSpec((tb, w), lambda i: (i, 0))
    whole = lambda r, w: pl.BlockSpec((r, w), lambda i: (0, 0))
    halo = pl.BlockSpec((per, BLOCK, KV_W), lambda i: (i, 0, 0))
    halo_next = pl.BlockSpec((1, BLOCK, KV_W), lambda i: (jnp.minimum((i + 1) * per, seq // T_MIX - 1), 0, 0))
    next8 = pl.BlockSpec((8, CONV_W), lambda i: (jnp.minimum((i + 1) * (tb // 8), seq // 8 - 1), 0))
    ch = pl.BlockSpec((tb, 1024), lambda i: (i, 1))
    return pl.pallas_call(
        body, name="bwd_proj", grid=(nt,),
        out_shape=(jax.ShapeDtypeStruct((seq, D_MODEL), F32), jax.ShapeDtypeStruct((IN_W, D_MODEL), F32),
                   jax.ShapeDtypeStruct((SMALL_ROWS, D_MODEL), F32)),
        in_specs=[tile(D_MODEL), whole(1, D_MODEL), tile(D_MODEL), tile(ATTN_W), tile(KV_W), tile(KV_W), halo, halo,
                  halo_next, halo_next,
                  tile(ATTN_W), tile(CONV_W), tile(CONV_W), tile(CONV_W), next8, ch, CONV_SPEC,
                  pl.BlockSpec((IN_W, D_MODEL), lambda i: (0, 0), pipeline_mode=pl.Buffered(1)),
                  tile(128), tile(128), tile(128)],
        out_specs=(tile(D_MODEL), pl.BlockSpec(memory_space=pl.ANY), whole(SMALL_ROWS, D_MODEL)),
        scratch_shapes=[pltpu.VMEM((tb, IN_W), BF16), pltpu.VMEM((IN_W, D_MODEL), F32)],
        compiler_params=_params(("arbitrary",)),
    )(x, norm_g, dh2, dq, dk, dv, dkh, dvh, dkh, dvh, dga, db, dgc, dcv, dcv, rest, conv_w, w_in_t,
      rope_c, rope_a, rope_b)


def _adamw_step(w, g, m, v):
    m2 = ADAM_B1 * m + (1.0 - ADAM_B1) * g
    v2 = ADAM_B2 * v + (1.0 - ADAM_B2) * jnp.square(g)
    m_hat = m2 / (1.0 - ADAM_B1 ** ADAM_STEP)
    v_hat = v2 / (1.0 - ADAM_B2 ** ADAM_STEP)
    return -ADAM_LR * (m_hat / (jnp.sqrt(v_hat) + ADAM_EPS) + ADAM_WD * w), m2, v2


def _adamw_weights(groups):
    steps = 4

    def body(*refs):
        ins, outs = refs[:4 * len(groups)], refs[4 * len(groups):]
        for k in range(len(groups)):
            res = _adamw_step(*(r[...] for r in ins[4 * k:4 * k + 4]))
            for o_ref, val in zip(outs[3 * k:3 * k + 3], res):
                o_ref[...] = val

    in_specs, out_specs, out_shape = [], [], []
    for w, _, _, _ in groups:
        rows, cols = w.shape
        spec = pl.BlockSpec((rows // steps, cols), lambda i: (i, 0))
        in_specs += [spec] * 4
        out_specs += [spec] * 3
        out_shape += [jax.ShapeDtypeStruct((rows, cols), F32)] * 3
    flat = pl.pallas_call(
        body, name="adamw_weights", grid=(steps,), out_shape=tuple(out_shape), in_specs=in_specs,
        out_specs=tuple(out_specs), compiler_params=_params(("arbitrary",)),
    )(*[a for grp in groups for a in grp])
    return [flat[3 * k:3 * k + 3] for k in range(len(groups))]


def _adamw_small(chip, small, params, m, v):
    def body(chip_ref, small_ref, conv_ref, *refs):
        ins, outs = refs[:12], refs[12:]
        outs[0][...] = jnp.sum(small_ref[6:7, :], axis=-1, keepdims=True)
        grads = (small_ref[0:1, :], small_ref[1:2, :], conv_ref[2:5, :], small_ref[5:6, 0:8])
        for k, g in enumerate(grads):
            outs[1 + k][...] = g
            res = _adamw_step(ins[k][...], g, ins[4 + k][...], ins[8 + k][...])
            for n, val in enumerate(res):
                outs[5 + 4 * n + k][...] = val

    full = lambda a: pl.BlockSpec(a.shape, lambda i, c: (0,) * len(a.shape))
    shapes = [jax.ShapeDtypeStruct(p.shape, F32) for p in params]
    outs = [jax.ShapeDtypeStruct((1, 1), F32)] + shapes * 4
    flat = pl.pallas_call(
        body, name="adamw_small",
        grid_spec=pltpu.PrefetchScalarGridSpec(
            num_scalar_prefetch=1, grid=(1,),
            in_specs=[full(small), pl.BlockSpec((SMALL_ROWS, 128), lambda i, c: (0, c[0]))]
            + [full(a) for a in (*params, *m, *v)],
            out_specs=tuple(full(s) for s in outs)),
        out_shape=tuple(outs), compiler_params=_params(("arbitrary",)),
    )(chip, small, small, *params, *m, *v)
    return flat[0], flat[1:5], [flat[5 + 4 * n:9 + 4 * n] for n in range(3)]


def kernel(x, norm_g, w_in, sinks, conv_w, w_out, final_g, loss_target, m_norm_g, m_w_in, m_sinks, m_conv_w, m_w_out, m_final_g, v_norm_g, v_w_in, v_sinks, v_conv_w, v_w_out, v_final_g):
    seq = x.shape[1]
    x2 = x.reshape(seq, D_MODEL)
    tgt = loss_target.reshape(seq, D_MODEL)
    ng = norm_g.reshape(1, D_MODEL)
    fg = final_g.reshape(1, D_MODEL)
    chip = 2 * lax.axis_index("x") + lax.axis_index("y")

    conv_w8 = jnp.zeros((8, 128), F32).at[0:3].set(conv_w)
    w_in_full = _gather_w_in(w_in.T).reshape(IN_W, D_MODEL)

    q, kd, vd, rest, rope_c, rope_a, rope_b, wo_all, cw_all = _fwd_proj(x2, ng, w_in_full, w_out, conv_w8)
    w_out_full = wo_all.reshape(D_MODEL, D_MODEL)
    attn, dh2, g_wo, g_wo_b, small_f = _fwd_mix(x2, q, kd, vd, rest, sinks, cw_all, w_out_full, fg, tgt)
    out_blocks = lambda t: t.reshape(N_CHIPS, W_OUT_BLK, D_MODEL)
    dq, dk, dv, dkh, dvh, dga, db, dgc, dcv, small_m, grad_w_out = _bwd_mix(
        dh2, q, kd, vd, attn, rest, sinks, cw_all, w_out_full, rope_c, rope_a, rope_b,
        out_blocks(g_wo), out_blocks(g_wo_b))
    grad_x, g_wi, small_p = _bwd_proj(x2, ng, dh2, dq, dk, dv, dkh, dvh, dga, db, dgc, dcv, rest, cw_all,
                                      w_in_full, rope_c, rope_a, rope_b)

    g_in_blocks = g_wi.reshape(N_CHIPS, W_IN_BLK, D_MODEL)
    grad_w_in_t, small = _reduce_grads(g_in_blocks, small_f, small_m, small_p)

    (upd_wi, upd_wo) = _adamw_weights([(w_in.T, grad_w_in_t, m_w_in.T, v_w_in.T),
                                       (w_out, grad_w_out, m_w_out, v_w_out)])
    row = lambda t: t.reshape(1, -1)
    loss, grads_s, upd_s = _adamw_small(
        chip.reshape(1), small, (ng, fg, conv_w, row(sinks)),
        (row(m_norm_g), row(m_final_g), m_conv_w, row(m_sinks)),
        (row(v_norm_g), row(v_final_g), v_conv_w, row(v_sinks)))

    def named(ng_, fg_, cw_, sk_, wi_t, wo_):
        return [ng_.reshape(D_MODEL), wi_t.T, sk_.reshape(8), cw_, wo_, fg_.reshape(D_MODEL)]

    g_named = named(*grads_s, grad_w_in_t, grad_w_out)
    out = [loss.reshape(()), grad_x.reshape(1, seq, D_MODEL)] + g_named
    for n in range(3):
        out += named(*upd_s[n], upd_wi[n], upd_wo[n])
    return tuple(out)
```

```python
import jax
import jax.numpy as jnp
from jax import lax
from jax.experimental import pallas as pl
from jax.experimental.pallas import tpu as pltpu

F32 = jnp.float32
BF16 = jnp.bfloat16

D_MODEL = 1024
HEAD_DIM = 64
ATTN_W = 512
KV_W = 128
CONV_W = 512
IN_W = 3328
REST_W = IN_W - ATTN_W - 2 * KV_W
BLOCK = 128
ROT_DIM = 16
ROPE_THETA = 500000.0
EPS = 1e-5
SCALE = 0.125
NEG = -1e30

N_CHIPS = 4
W_IN_BLK = IN_W // N_CHIPS
W_OUT_BLK = D_MODEL // N_CHIPS

ADAM_LR = 0.001
ADAM_B1 = 0.9
ADAM_B2 = 0.999
ADAM_EPS = 1e-08
ADAM_WD = 0.01
ADAM_STEP = 10

VMEM_LIMIT = 56 * 1024 * 1024
T_PROJ = 512
T_FMIX = 512
T_MIX = 512
SMALL_ROWS = 8
MESH = pl.DeviceIdType.MESH

_NT = (((1,), (1,)), ((), ()))
_TN = (((0,), (0,)), ((), ()))


def _params(sem=None):
    kw = dict(vmem_limit_bytes=VMEM_LIMIT)
    if sem is not None:
        kw["dimension_semantics"] = sem
    return pltpu.CompilerParams(**kw)


def _sigmoid(t):
    return 1.0 / (1.0 + jnp.exp(-t))


def _shift_down(t, prev8, k):
    rolled = pltpu.roll(t, k, 0)
    row = lax.broadcasted_iota(jnp.int32, t.shape, 0)
    for j in range(k):
        rolled = jnp.where(row == j, prev8[8 - k + j:8 - k + j + 1, :], rolled)
    return rolled


def _shift_up(t, next8, k):
    n = t.shape[0]
    rolled = pltpu.roll(t, n - k, 0)
    row = lax.broadcasted_iota(jnp.int32, t.shape, 0)
    for j in range(k):
        rolled = jnp.where(row == n - k + j, next8[j:j + 1, :], rolled)
    return rolled


def _rope(t, c, a, b):
    w = t.shape[1]
    reps = w // 128
    if reps > 1:
        c, a, b = (jnp.concatenate([z] * reps, axis=1) for z in (c, a, b))
    return t * c + pltpu.roll(t, w - 8, 1) * a + pltpu.roll(t, 8, 1) * b


def _lane_lo(shape):
    return lax.broadcasted_iota(jnp.int32, shape, 1) < HEAD_DIM


def _stack_heads(t, g):
    lo = _lane_lo((BLOCK, 128))
    parts = []
    for hh in range(4):
        pair = t[:, 256 * g + 128 * (hh // 2):256 * g + 128 * (hh // 2) + 128]
        keep = lo if hh % 2 == 0 else jnp.logical_not(lo)
        parts.append(jnp.where(keep, pair, jnp.zeros_like(pair)))
    return jnp.concatenate(parts, axis=0)


def _unstack_pair(o, pp):
    lo = _lane_lo((BLOCK, 128))
    return jnp.where(lo, o[256 * pp:256 * pp + 128], o[256 * pp + 128:256 * pp + 256])


def _band_masks(has_prev):
    r = lax.broadcasted_iota(jnp.int32, (4 * BLOCK, 2 * BLOCK), 0) % BLOCK
    kj = lax.broadcasted_iota(jnp.int32, (4 * BLOCK, 2 * BLOCK), 1)
    cur = (kj >= BLOCK) & (kj - BLOCK <= r)
    prev = (kj < BLOCK) & (kj > r)
    return cur | (prev & has_prev), cur | prev


def _sink_col(sinks_ref, g):
    r = lax.broadcasted_iota(jnp.int32, (4 * BLOCK, 1), 0) // BLOCK
    col = jnp.full((4 * BLOCK, 1), sinks_ref[4 * g + 3], F32)
    for hh in range(3):
        col = jnp.where(r == hh, sinks_ref[4 * g + hh], col)
    return col


def _probs(qs, kd, sink_col, mask):
    return _softmax(lax.dot_general(qs, kd, _NT, preferred_element_type=F32), sink_col, mask)


def _softmax(s, sink_col, mask):
    s = jnp.where(mask, s, NEG)
    m = jnp.maximum(jnp.max(s, axis=-1, keepdims=True), sink_col)
    p = jnp.exp(s - m)
    es = jnp.exp(sink_col - m)
    inv = 1.0 / (jnp.sum(p, axis=-1, keepdims=True) + es)
    return p * inv, es * inv


def _key_windows(i, nsub, kd_ref, vd_ref, kdp_ref, vdp_ref):
    mask_first, mask_rest = _band_masks(i > 0)
    out = []
    for sb in range(nsub):
        rows = slice(BLOCK * sb, BLOCK * (sb + 1))
        if sb == 0:
            kk = jnp.concatenate([kdp_ref[...], kd_ref[rows, :]], axis=0)
            vv = jnp.concatenate([vdp_ref[...], vd_ref[rows, :]], axis=0)
            out.append((rows, kk, vv, mask_first))
        else:
            both = slice(BLOCK * (sb - 1), BLOCK * (sb + 1))
            out.append((rows, kd_ref[both, :], vd_ref[both, :], mask_rest))
    return out


def _gather_w_in(w_in_t):
    hi = W_IN_BLK // 2
    qr = hi // 2

    def body(wi_ref, wi_all, send_sems, recv_sems):
        x, y, c = lax.axis_index("x"), lax.axis_index("y"), lax.axis_index("c")
        me, sibling = (x, y), (x, y, 1 - c)
        xnb, ynb, diag = (1 - x, y), (x, 1 - y), (1 - x, 1 - y)

        wi_all[2 * x + y] = wi_ref[...].astype(BF16)

        def copy(k, chip, half, quarter, to):
            r = wi_all.at[2 * chip[0] + chip[1], pl.ds(half * hi + quarter * qr, qr)]
            return pltpu.make_async_remote_copy(src_ref=r, dst_ref=r, send_sem=send_sems.at[k],
                                                recv_sem=recv_sems.at[k], device_id=to, device_id_type=MESH)

        plan = [(0, xnb, 0), (1, ynb, 1), (2, xnb, 1), (3, ynb, 0)]
        sent = [copy(k, me, c, quarter, (*nb, c)) for k, nb, quarter in plan]
        for cp in sent:
            cp.start()
        arrivals = [(0, xnb, 0), (1, ynb, 1), (2, xnb, 1), (3, ynb, 0), (4, diag, 0), (5, diag, 1)]
        relay = {0: (4, ynb), 1: (5, xnb)}
        for k, chip, quarter in arrivals:
            copy(k, chip, c, quarter, (x, y, c)).wait_recv()
            if k in relay:
                sent.append(copy(relay[k][0], chip, c, quarter, (*relay[k][1], c)))
                sent[-1].start()
            sent.append(copy(6 + k, chip, c, quarter, sibling))
            sent[-1].start()
        for k, chip, quarter in arrivals:
            copy(6 + k, chip, 1 - c, quarter, (x, y, c)).wait_recv()
        for cp in sent:
            cp.wait_send()

    vmem = pl.BlockSpec(memory_space=pltpu.VMEM)
    return pl.pallas_call(
        body, name="gather_w_in",
        out_shape=jax.ShapeDtypeStruct((N_CHIPS, W_IN_BLK, D_MODEL), BF16),
        in_specs=[vmem], out_specs=vmem,
        scratch_shapes=[pltpu.SemaphoreType.DMA((12,)), pltpu.SemaphoreType.DMA((12,))],
        compiler_params=_params(),
    )(w_in_t)


def _reduce_grads(g_in, *smalls):
    hi = W_IN_BLK // 2

    def body(gi_hbm, s0_ref, s1_ref, s2_ref, gi_out, small_out,
             mine_i, sib_i, out_i, ici_i, small_in, small_ref, send_sems, recv_sems, local_sems):
        x, y, c = lax.axis_index("x"), lax.axis_index("y"), lax.axis_index("c")
        my_dev = 4 * x + 2 * y + c
        sibling = (x, y, 1 - c)
        chips = [(1 - x, y), (x, 1 - y), (1 - x, 1 - y)]
        order = chips + [(x, y)]

        def remote(k, src, dst, to):
            return pltpu.make_async_remote_copy(src_ref=src, dst_ref=dst, send_sem=send_sems.at[k],
                                                recv_sem=recv_sems.at[k], device_id=to, device_id_type=MESH)

        small_ref[...] = s0_ref[...] + s1_ref[...] + s2_ref[...]
        small_cps = []
        for f in range(1, 8):
            fx, fy, fc = f >> 2, (f >> 1) & 1, f & 1
            small_cps.append(remote(8 + f - 1, small_ref, small_in.at[f - 1], (x ^ fx, y ^ fy, c ^ fc)))
        for cp in small_cps:
            cp.start()

        own, to_sib = [], []
        for n, chip in enumerate(order):
            j = 2 * chip[0] + chip[1]
            own.append(pltpu.make_async_copy(gi_hbm.at[j, pl.ds(c * hi, hi)], mine_i.at[n], local_sems.at[n]))
            to_sib.append(remote(n, gi_hbm.at[j, pl.ds((1 - c) * hi, hi)], sib_i.at[n], sibling))
            own[-1].start()
            to_sib[-1].start()

        ici = []
        for k, chip in enumerate(chips):
            own[k].wait()
            to_sib[k].wait_recv()
            out_i[k] = (mine_i[k] + sib_i[k]).astype(BF16)
            ici.append(remote(4 + k, out_i.at[k], ici_i.at[k], (*chip, c)))
            ici[-1].start()
        own[3].wait()
        to_sib[3].wait_recv()
        tot_i = mine_i[3] + sib_i[3]
        for k in range(3):
            ici[k].wait_recv()
            tot_i = tot_i + ici_i[k].astype(F32)
        gi_out[pl.ds(c * hi, hi), :] = tot_i

        swap = [remote(7, gi_out.at[pl.ds(c * hi, hi)], gi_out.at[pl.ds(c * hi, hi)], sibling)]
        for cp in swap:
            cp.start()

        for cp in small_cps:
            cp.wait_recv()
        total = jnp.zeros((SMALL_ROWS, D_MODEL), F32)
        for d in range(8):
            slot = jnp.maximum((d ^ my_dev) - 1, 0)
            total = total + jnp.where(d == my_dev, small_ref[...], small_in[slot])
        small_out[...] = total

        remote(7, gi_out.at[pl.ds((1 - c) * hi, hi)], gi_out.at[pl.ds((1 - c) * hi, hi)], sibling).wait_recv()
        for cp in to_sib + ici + swap + small_cps:
            cp.wait_send()

    vmem = pl.BlockSpec(memory_space=pltpu.VMEM)
    anyspace = pl.BlockSpec(memory_space=pl.ANY)
    return pl.pallas_call(
        body, name="reduce_grads",
        out_shape=(jax.ShapeDtypeStruct((W_IN_BLK, D_MODEL), F32), jax.ShapeDtypeStruct((SMALL_ROWS, D_MODEL), F32)),
        in_specs=[anyspace, vmem, vmem, vmem], out_specs=(vmem, vmem),
        scratch_shapes=[pltpu.VMEM((N_CHIPS, hi, D_MODEL), F32), pltpu.VMEM((N_CHIPS, hi, D_MODEL), F32),
                        pltpu.VMEM((3, hi, D_MODEL), BF16), pltpu.VMEM((3, hi, D_MODEL), BF16),
                        pltpu.VMEM((7, SMALL_ROWS, D_MODEL), F32), pltpu.VMEM((SMALL_ROWS, D_MODEL), F32),
                        pltpu.SemaphoreType.DMA((15,)), pltpu.SemaphoreType.DMA((15,)),
                        pltpu.SemaphoreType.DMA((4,))],
        compiler_params=_params(),
    )(g_in, *smalls)


def _fwd_proj(x, norm_g, w_in_t, w_out, conv_w8):
    seq = x.shape[0]
    nt = seq // T_PROJ
    lane = jnp.arange(128, dtype=jnp.int32) % HEAD_DIM
    inv_freq = ROPE_THETA ** (-(2 * (lane % 8)).astype(F32) / ROT_DIM)
    inv_freq = jnp.where(lane < ROT_DIM, inv_freq, 0.0).reshape(1, 128)
    in_tile = jnp.arange(T_PROJ, dtype=jnp.int32).astype(F32)[:, None] * inv_freq
    cos_in, sin_in = jnp.cos(in_tile), jnp.sin(in_tile)
    start = jnp.repeat((jnp.arange(nt, dtype=jnp.int32) * T_PROJ).astype(F32), 8)[:, None] * inv_freq
    cos_st, sin_st = jnp.cos(start), jnp.sin(start)

    def body(x_ref, g_ref, w_ref, cs_ref, ss_ref, ci_ref, si_ref, wo_ref, cw_ref,
             q_ref, kd_ref, vd_ref, rest_ref, c_ref, a_ref, b_ref, wo_all, cw_all,
             wo_stage, send_sems, recv_sems, local_sems):
        i = pl.program_id(0)
        mx, my, mc = lax.axis_index("x"), lax.axis_index("y"), lax.axis_index("c")
        chips = [(1 - mx, my), (mx, 1 - my), (1 - mx, 1 - my)]

        def gather(blocks):
            cps = []
            for k, chip in enumerate(chips):
                for n, (src, dst) in enumerate(((wo_stage, wo_all), (cw_ref, cw_all))):
                    cps.append(pltpu.make_async_remote_copy(
                        src_ref=src, dst_ref=dst.at[blocks[k]], send_sem=send_sems.at[2 * k + n],
                        recv_sem=recv_sems.at[2 * k + n], device_id=(*chip, mc), device_id_type=MESH))
            return cps

        me = 2 * mx + my
        own = [pltpu.make_async_copy(wo_stage, wo_all.at[me], local_sems.at[0]),
               pltpu.make_async_copy(cw_ref, cw_all.at[me], local_sems.at[1])]

        @pl.when(i == 0)
        def _():
            wo_stage[...] = wo_ref[...].astype(BF16)
            for cp in own + gather([me] * 3):
                cp.start()

        xf = x_ref[...]
        r1 = lax.rsqrt(jnp.mean(xf * xf, axis=-1, keepdims=True) + EPS)
        xn = (xf * r1 * g_ref[...]).astype(BF16)
        cs, ss = cs_ref[0:1, :], ss_ref[0:1, :]
        c = cs * ci_ref[...] - ss * si_ref[...]
        sin = ss * ci_ref[...] + cs * si_ref[...]
        j = lax.broadcasted_iota(jnp.int32, (T_PROJ, 128), 1) % HEAD_DIM
        a = jnp.where(j < 8, -sin, 0.0)
        b = jnp.where(j >= 8, sin, 0.0)
        c_ref[...], a_ref[...], b_ref[...] = c, a, b
        proj = lambda lo_c, w: lax.dot_general(xn, w_ref[lo_c:lo_c + w, :], _NT, preferred_element_type=F32)
        q_ref[...] = (_rope(proj(0, ATTN_W), c, a, b) * SCALE).astype(BF16)
        kv = proj(ATTN_W, 2 * KV_W)
        k = _rope(kv[:, 0:KV_W], c, a, b)
        v = kv[:, KV_W:2 * KV_W]
        lo = _lane_lo(k.shape)
        for t, ref in ((k, kd_ref), (v, vd_ref)):
            sw = pltpu.roll(t, HEAD_DIM, 1)
            ref[:, 0:128] = jnp.where(lo, t, sw).astype(BF16)
            ref[:, 128:256] = jnp.where(lo, sw, t).astype(BF16)
        for n in range(REST_W // 512):
            rest_ref[:, 512 * n:512 * (n + 1)] = proj(ATTN_W + 2 * KV_W + 512 * n, 512)

        @pl.when(i == nt - 1)
        def _():
            sent = gather([me] * 3)
            for cp in gather([2 * chip[0] + chip[1] for chip in chips]):
                cp.wait_recv()
            for cp in sent:
                cp.wait_send()
            for cp in own:
                cp.wait()

    tile = lambda w: pl.BlockSpec((T_PROJ, w), lambda i: (i, 0))
    whole = lambda r, w: pl.BlockSpec((r, w), lambda i: (0, 0))
    vmem = pl.BlockSpec(memory_space=pltpu.VMEM)
    hbm = pl.BlockSpec(memory_space=pl.ANY)
    return pl.pallas_call(
        body, name="fwd_proj", grid=(nt,),
        out_shape=(jax.ShapeDtypeStruct((seq, ATTN_W), BF16), jax.ShapeDtypeStruct((seq, 2 * KV_W), BF16),
                   jax.ShapeDtypeStruct((seq, 2 * KV_W), BF16), jax.ShapeDtypeStruct((seq, REST_W), F32))
        + (jax.ShapeDtypeStruct((seq, 128), F32),) * 3
        + (jax.ShapeDtypeStruct((N_CHIPS, W_OUT_BLK, D_MODEL), BF16), jax.ShapeDtypeStruct((N_CHIPS, 8, 128), F32)),
        in_specs=[tile(D_MODEL), whole(1, D_MODEL), whole(IN_W, D_MODEL), pl.BlockSpec((8, 128), lambda i: (i, 0)),
                  pl.BlockSpec((8, 128), lambda i: (i, 0)), whole(T_PROJ, 128), whole(T_PROJ, 128), vmem, vmem],
        out_specs=(tile(ATTN_W), tile(2 * KV_W), tile(2 * KV_W), tile(REST_W), tile(128), tile(128), tile(128),
                   hbm, hbm),
        scratch_shapes=[pltpu.VMEM((W_OUT_BLK, D_MODEL), BF16), pltpu.SemaphoreType.DMA((6,)),
                        pltpu.SemaphoreType.DMA((6,)), pltpu.SemaphoreType.DMA((2,))],
        compiler_params=_params(("arbitrary",)),
    )(x, norm_g, w_in_t, cos_st, sin_st, cos_in, sin_in, w_out, conv_w8)


CONV_SPEC = pl.BlockSpec((N_CHIPS, 8, 128), lambda i: (0, 0, 0))


def _conv_rows(cw_ref):
    return jnp.concatenate([cw_ref[j] for j in range(N_CHIPS)], axis=1)


def _conv_parts(rest_ref, prev_ref, cw_ref, first):
    u = rest_ref[:, 1024:1536] * rest_ref[:, 1536:2048]
    up = prev_ref[:, 1024:1536] * prev_ref[:, 1536:2048]
    up = jnp.where(first, jnp.zeros_like(up), up)
    um1 = _shift_down(u, up, 1)
    um2 = _shift_down(u, up, 2)
    cw = _conv_rows(cw_ref)
    cv = cw[0:1, :] * um2 + cw[1:2, :] * um1 + cw[2:3, :] * u
    return u, um1, um2, cv


def _fwd_mix(x, q, kd, vd, rest, sinks, conv_w, w_out, final_g, target):
    seq = x.shape[0]
    nt = seq // T_FMIX
    nsub = T_FMIX // BLOCK

    def body(sinks_ref, x_ref, q_ref, kd_ref, vd_ref, kdp_ref, vdp_ref, rest_ref, restp_ref, cw_ref, wo_ref,
             fg_ref, tgt_ref, attn_ref, dh2_ref, gwo_ref, gwob_ref, small_ref, mix_ref, pmix_ref, pdh2_ref):
        i = pl.program_id(0)

        @pl.when(i == 0)
        def _():
            small_ref[...] = jnp.zeros_like(small_ref)
            gwo_ref[...] = jnp.zeros_like(gwo_ref)
            pmix_ref[...] = jnp.zeros_like(pmix_ref)
            pdh2_ref[...] = jnp.zeros_like(pdh2_ref)

        gwo_ref[...] += lax.dot_general(pmix_ref[...], pdh2_ref[...], _TN, preferred_element_type=F32)

        chains = []
        for rows, kk, vv, mask in _key_windows(i, nsub, kd_ref, vd_ref, kdp_ref, vdp_ref):
            qt = q_ref[rows, :]
            for g in range(2):
                kg = kk[:, 128 * g:128 * (g + 1)]
                chains.append(dict(g=g, rows=rows, mask=mask, vg=vv[:, 128 * g:128 * (g + 1)],
                                   s=lax.dot_general(_stack_heads(qt, g), kg, _NT, preferred_element_type=F32)))
        for ch in chains:
            ch["prob"], _ = _softmax(ch.pop("s"), _sink_col(sinks_ref, ch["g"]), ch["mask"])
        for ch in chains:
            o = jnp.dot(ch["prob"].astype(BF16), ch["vg"], preferred_element_type=F32)
            for pp in range(2):
                lanes = slice(256 * ch["g"] + 128 * pp, 256 * ch["g"] + 128 * (pp + 1))
                attn_ref[ch["rows"], lanes] = _unstack_pair(o, pp)

        ga = rest_ref[:, 0:512]
        mix_ref[:, 0:ATTN_W] = (attn_ref[...] * (ga * _sigmoid(ga))).astype(BF16)
        _, _, _, cv = _conv_parts(rest_ref, restp_ref, cw_ref, i == 0)
        gc = rest_ref[:, 2048:2560]
        mix_ref[:, ATTN_W:] = (rest_ref[:, 512:1024] * cv * (gc * _sigmoid(gc))).astype(BF16)

        h2 = x_ref[...] + jnp.dot(mix_ref[...], wo_ref[...], preferred_element_type=F32)
        r2 = lax.rsqrt(jnp.mean(h2 * h2, axis=-1, keepdims=True) + EPS)
        n2 = h2 * r2
        err = n2 * fg_ref[...] - tgt_ref[...]
        dy = err * (1.0 / D_MODEL)
        small_ref[6:7, :] += jnp.sum(err * err, axis=0, keepdims=True) * (0.5 / D_MODEL)
        small_ref[1:2, :] += jnp.sum(dy * n2, axis=0, keepdims=True)
        dn = dy * fg_ref[...]
        dh2 = r2 * (dn - n2 * jnp.mean(dn * n2, axis=-1, keepdims=True))
        dh2_ref[...] = dh2
        pmix_ref[...] = mix_ref[...]
        pdh2_ref[...] = dh2.astype(BF16)

        @pl.when(i == nt - 1)
        def _():
            gwo_ref[...] += lax.dot_general(pmix_ref[...], pdh2_ref[...], _TN, preferred_element_type=F32)
            gwob_ref[...] = gwo_ref[...].astype(BF16)

    tile = lambda w: pl.BlockSpec((T_FMIX, w), lambda i: (i, 0))
    whole = lambda r, w: pl.BlockSpec((r, w), lambda i: (0, 0))
    prev_blk = pl.BlockSpec((BLOCK, 2 * KV_W), lambda i: (jnp.maximum(i * nsub - 1, 0), 0))
    prev8 = pl.BlockSpec((8, REST_W), lambda i: (jnp.maximum(i * (T_FMIX // 8) - 1, 0), 0))
    return pl.pallas_call(
        body, name="fwd_mix", grid=(nt,),
        out_shape=(jax.ShapeDtypeStruct((seq, ATTN_W), F32), jax.ShapeDtypeStruct((seq, D_MODEL), F32),
                   jax.ShapeDtypeStruct((D_MODEL, D_MODEL), F32), jax.ShapeDtypeStruct((D_MODEL, D_MODEL), BF16),
                   jax.ShapeDtypeStruct((SMALL_ROWS, D_MODEL), F32)),
        in_specs=[pl.BlockSpec(memory_space=pltpu.SMEM), tile(D_MODEL), tile(ATTN_W), tile(2 * KV_W), tile(2 * KV_W),
                  prev_blk, prev_blk, tile(REST_W), prev8, CONV_SPEC, whole(D_MODEL, D_MODEL),
                  whole(1, D_MODEL), tile(D_MODEL)],
        out_specs=(tile(ATTN_W), tile(D_MODEL), whole(D_MODEL, D_MODEL), whole(D_MODEL, D_MODEL),
                   whole(SMALL_ROWS, D_MODEL)),
        scratch_shapes=[pltpu.VMEM((T_FMIX, D_MODEL), BF16)] * 3,
        compiler_params=_params(("arbitrary",)),
    )(sinks, x, q, kd, vd, kd, vd, rest, rest, conv_w, w_out, final_g, target)


def _scatter_copies(g_hbm, gb_hbm, mine, land, send_sems, recv_sems, local_sem, half):
    x, y, c = lax.axis_index("x"), lax.axis_index("y"), lax.axis_index("c")
    cps = []
    for f in range(1, 8):
        to = (x ^ (f >> 2), y ^ ((f >> 1) & 1), c ^ (f & 1))
        src = gb_hbm.at[2 * to[0] + to[1], pl.ds(to[2] * half, half)]
        cps.append(pltpu.make_async_remote_copy(src_ref=src, dst_ref=land.at[f - 1], send_sem=send_sems.at[f - 1],
                                                recv_sem=recv_sems.at[f - 1], device_id=to, device_id_type=MESH))
    own = pltpu.make_async_copy(g_hbm.at[2 * x + y, pl.ds(c * half, half)], mine, local_sem)
    return cps, own


def _scatter_finish(cps, own, mine, land, out_hbm, send_sems, recv_sems, local_sem, half):
    x, y, c = lax.axis_index("x"), lax.axis_index("y"), lax.axis_index("c")
    own.wait()
    tot = mine[...]
    for f in range(1, 8):
        cps[f - 1].wait_recv()
        tot = tot + land[f - 1].astype(F32)
    mine[...] = tot

    def swap(rows_of):
        return pltpu.make_async_remote_copy(src_ref=mine, dst_ref=out_hbm.at[pl.ds(rows_of * half, half)],
                                            send_sem=send_sems.at[7], recv_sem=recv_sems.at[7],
                                            device_id=(x, y, 1 - c), device_id_type=MESH)

    keep = pltpu.make_async_copy(mine, out_hbm.at[pl.ds(c * half, half)], local_sem)
    keep.start()
    swap(c).start()
    swap(1 - c).wait_recv()
    keep.wait()
    for cp in cps:
        cp.wait_send()
    swap(c).wait_send()


def _bwd_mix(dh2, q, kd, vd, attn, rest, sinks, conv_w, w_out, rope_c, rope_a, rope_b, g_out, g_out_b):
    seq = dh2.shape[0]
    nt = seq // T_MIX
    nsub = T_MIX // BLOCK
    ho = W_OUT_BLK // 2

    def body(sinks_ref, dh2_ref, q_ref, kd_ref, vd_ref, kdp_ref, vdp_ref, attn_ref, rest_ref, restp_ref,
             cw_ref, wo_ref, c_ref, a_ref, b_ref, go_hbm, gob_hbm,
             dq_ref, dk_ref, dv_ref, dkh_ref, dvh_ref, dga_ref, db_ref, dgc_ref, dcv_ref, small_ref, go_out,
             dmix_ref, dsink_ref, mine_o, land_o, send_sems, recv_sems, local_sems):
        i = pl.program_id(0)
        scatter = (mine_o, land_o, send_sems, recv_sems, local_sems.at[0], ho)

        @pl.when(i == 0)
        def _():
            small_ref[...] = jnp.zeros_like(small_ref)
            dsink_ref[...] = jnp.zeros_like(dsink_ref)
            cps, own = _scatter_copies(go_hbm, gob_hbm, *scatter)
            for cp in cps + [own]:
                cp.start()

        dmix_ref[...] = lax.dot_general(dh2_ref[...].astype(BF16), wo_ref[...], _NT, preferred_element_type=F32)

        ga = rest_ref[:, 0:512]
        sg = _sigmoid(ga)
        dma = dmix_ref[:, 0:ATTN_W]
        dga_ref[...] = (dma * attn_ref[...] * (sg * (1.0 + ga * (1.0 - sg)))).astype(BF16)
        dmix_ref[:, 0:ATTN_W] = dma * (ga * sg)

        u, um1, um2, cv = _conv_parts(rest_ref, restp_ref, cw_ref, i == 0)
        gc = rest_ref[:, 2048:2560]
        sc = _sigmoid(gc)
        bg = rest_ref[:, 512:1024]
        dmc = dmix_ref[:, ATTN_W:]
        t1 = dmc * (gc * sc)
        db_ref[...] = (t1 * cv).astype(BF16)
        dcv = t1 * bg
        dcv_ref[...] = dcv
        dgc_ref[...] = (dmc * (bg * cv) * (sc * (1.0 + gc * (1.0 - sc)))).astype(BF16)
        small_ref[2:3, 0:CONV_W] += jnp.sum(dcv * um2, axis=0, keepdims=True)
        small_ref[3:4, 0:CONV_W] += jnp.sum(dcv * um1, axis=0, keepdims=True)
        small_ref[4:5, 0:CONV_W] += jnp.sum(dcv * u, axis=0, keepdims=True)

        lo = _lane_lo((2 * BLOCK, 128))
        dk_blocks = [None] * (nsub + 1)
        dv_blocks = [None] * (nsub + 1)

        def add(lst, n, val):
            lst[n] = val if lst[n] is None else lst[n] + val

        chains = []
        for rows, kk, vv, mask in _key_windows(i, nsub, kd_ref, vd_ref, kdp_ref, vdp_ref):
            qt = q_ref[rows, :]
            dot = dmix_ref[rows, 0:ATTN_W].astype(BF16)
            for g in range(2):
                chains.append(dict(g=g, rows=rows, mask=mask, qs=_stack_heads(qt, g), dos=_stack_heads(dot, g),
                                   kg=kk[:, 128 * g:128 * (g + 1)], vg=vv[:, 128 * g:128 * (g + 1)]))
        for ch in chains:
            ch["prob"], ch["psink"] = _probs(ch["qs"], ch["kg"], _sink_col(sinks_ref, ch["g"]), ch["mask"])
        for ch in chains:
            ch["dp"] = lax.dot_general(ch["dos"], ch["vg"], _NT, preferred_element_type=F32)
        for ch in chains:
            rs = jnp.sum(ch["prob"] * ch["dp"], axis=-1, keepdims=True)
            ch["ds"] = (ch["prob"] * (ch["dp"] - rs)).astype(BF16)
            dsink_ref[ch["g"]] += -ch["psink"] * rs
        for ch in chains:
            dqs = jnp.dot(ch["ds"], ch["kg"], preferred_element_type=F32) * SCALE
            c, a, b = c_ref[ch["rows"], :], a_ref[ch["rows"], :], b_ref[ch["rows"], :]
            for pp in range(2):
                lanes = slice(256 * ch["g"] + 128 * pp, 256 * ch["g"] + 128 * (pp + 1))
                dq_ref[ch["rows"], lanes] = _rope(_unstack_pair(dqs, pp), c, -a, -b).astype(BF16)
            dkd = lax.dot_general(ch["ds"], ch["qs"], _TN, preferred_element_type=F32)
            dvd = lax.dot_general(ch["prob"].astype(BF16), ch["dos"], _TN, preferred_element_type=F32)
            ch["dk"] = dkd + pltpu.roll(dkd, HEAD_DIM, 1)
            ch["dv"] = dvd + pltpu.roll(dvd, HEAD_DIM, 1)
        for sb in range(nsub):
            dk2 = jnp.where(lo, chains[2 * sb]["dk"], chains[2 * sb + 1]["dk"])
            dv2 = jnp.where(lo, chains[2 * sb]["dv"], chains[2 * sb + 1]["dv"])
            add(dk_blocks, sb, dk2[0:BLOCK])
            add(dk_blocks, sb + 1, dk2[BLOCK:])
            add(dv_blocks, sb, dv2[0:BLOCK])
            add(dv_blocks, sb + 1, dv2[BLOCK:])
        dkh_ref[0] = dk_blocks[0]
        dvh_ref[0] = dv_blocks[0]
        for sb in range(nsub):
            dk_ref[BLOCK * sb:BLOCK * (sb + 1), :] = dk_blocks[sb + 1]
            dv_ref[BLOCK * sb:BLOCK * (sb + 1), :] = dv_blocks[sb + 1]

        @pl.when(i == nt - 1)
        def _():
            for h in range(8):
                tot = jnp.sum(dsink_ref[h // 4, BLOCK * (h % 4):BLOCK * (h % 4 + 1), :], axis=0, keepdims=True)
                small_ref[5:6, h:h + 1] = tot
            cps, own = _scatter_copies(go_hbm, gob_hbm, *scatter)
            _scatter_finish(cps, own, mine_o, land_o, go_out, send_sems, recv_sems, local_sems.at[1], ho)

    tile = lambda w: pl.BlockSpec((T_MIX, w), lambda i: (i, 0))
    whole = lambda r, w: pl.BlockSpec((r, w), lambda i: (0, 0))
    prev_blk = pl.BlockSpec((BLOCK, 2 * KV_W), lambda i: (jnp.maximum(i * nsub - 1, 0), 0))
    prev8 = pl.BlockSpec((8, REST_W), lambda i: (jnp.maximum(i * (T_MIX // 8) - 1, 0), 0))
    halo = pl.BlockSpec((1, BLOCK, KV_W), lambda i: (i, 0, 0))
    hbm = pl.BlockSpec(memory_space=pl.ANY)
    bf = lambda w: jax.ShapeDtypeStruct((seq, w), BF16)
    f32 = lambda w: jax.ShapeDtypeStruct((seq, w), F32)
    return pl.pallas_call(
        body, name="bwd_mix", grid=(nt,),
        out_shape=(bf(ATTN_W), f32(KV_W), f32(KV_W), jax.ShapeDtypeStruct((nt, BLOCK, KV_W), F32),
                   jax.ShapeDtypeStruct((nt, BLOCK, KV_W), F32), bf(ATTN_W), bf(CONV_W), bf(CONV_W), f32(CONV_W),
                   jax.ShapeDtypeStruct((SMALL_ROWS, D_MODEL), F32), jax.ShapeDtypeStruct((W_OUT_BLK, D_MODEL), F32)),
        in_specs=[pl.BlockSpec(memory_space=pltpu.SMEM), tile(D_MODEL), tile(ATTN_W), tile(2 * KV_W), tile(2 * KV_W),
                  prev_blk, prev_blk, tile(ATTN_W), tile(REST_W), prev8, CONV_SPEC,
                  whole(D_MODEL, D_MODEL), tile(128), tile(128), tile(128), hbm, hbm],
        out_specs=(tile(ATTN_W), tile(KV_W), tile(KV_W), halo, halo, tile(ATTN_W), tile(CONV_W), tile(CONV_W),
                   tile(CONV_W), whole(SMALL_ROWS, D_MODEL), hbm),
        scratch_shapes=[pltpu.VMEM((T_MIX, D_MODEL), F32), pltpu.VMEM((2, 4 * BLOCK, 1), F32),
                        pltpu.VMEM((ho, D_MODEL), F32), pltpu.VMEM((7, ho, D_MODEL), BF16),
                        pltpu.SemaphoreType.DMA((8,)), pltpu.SemaphoreType.DMA((8,)), pltpu.SemaphoreType.DMA((2,))],
        compiler_params=_params(("arbitrary",)),
    )(sinks, dh2, q, kd, vd, kd, vd, attn, rest, rest, conv_w, w_out, rope_c, rope_a, rope_b, g_out, g_out_b)


def _bwd_proj(x, norm_g, dh2, dq, dk, dv, dkh, dvh, dga, db, dgc, dcv, rest, conv_w, w_in_t, rope_c, rope_a, rope_b):
    seq = x.shape[0]
    tb = T_PROJ
    per = tb // T_MIX
    nt = seq // tb

    def body(x_ref, g_ref, dh2_ref, dq_ref, dk_ref, dv_ref, dkh_ref, dvh_ref, dkn_ref, dvn_ref, dga_ref, db_ref,
             dgc_ref, dcv_ref, dcvn_ref, ch_ref, cw_ref, w_ref, c_ref, a_ref, b_ref, gx_ref, gw_hbm, small_ref,
             dp_ref, acc_ref):
        i = pl.program_id(0)

        @pl.when(i == 0)
        def _():
            small_ref[...] = jnp.zeros_like(small_ref)
            acc_ref[...] = jnp.zeros_like(acc_ref)

        last = i == nt - 1
        keep = jnp.where(last, 0.0, 1.0)
        pad = jnp.zeros((T_MIX - BLOCK, KV_W), F32)

        def with_halos(main_ref, halo_ref, next_ref):
            parts = []
            for m in range(1, per + 1):
                parts += [pad, halo_ref[m] if m < per else next_ref[0] * keep]
            return main_ref[...] + jnp.concatenate(parts, axis=0)

        dk = with_halos(dk_ref, dkh_ref, dkn_ref)
        dv = with_halos(dv_ref, dvh_ref, dvn_ref)
        dp_ref[:, 0:ATTN_W] = dq_ref[...]
        dp_ref[:, ATTN_W:ATTN_W + KV_W] = _rope(dk, c_ref[...], -a_ref[...], -b_ref[...]).astype(BF16)
        dp_ref[:, ATTN_W + KV_W:ATTN_W + 2 * KV_W] = dv.astype(BF16)
        base = ATTN_W + 2 * KV_W
        dp_ref[:, base:base + 512] = dga_ref[...]
        dp_ref[:, base + 512:base + 1024] = db_ref[...]
        dcv = dcv_ref[...]
        nxt = dcvn_ref[...] * keep
        cw = _conv_rows(cw_ref)
        du = cw[2:3, :] * dcv + cw[1:2, :] * _shift_up(dcv, nxt, 1) + cw[0:1, :] * _shift_up(dcv, nxt, 2)
        dp_ref[:, base + 1024:base + 1536] = (du * ch_ref[:, 512:1024]).astype(BF16)
        dp_ref[:, base + 1536:base + 2048] = (du * ch_ref[:, 0:512]).astype(BF16)
        dp_ref[:, base + 2048:base + 2560] = dgc_ref[...]

        xf = x_ref[...]
        r1 = lax.rsqrt(jnp.mean(xf * xf, axis=-1, keepdims=True) + EPS)
        n1 = xf * r1
        xn = (n1 * g_ref[...]).astype(BF16)
        for n in range(IN_W // 256):
            cols = slice(256 * n, 256 * (n + 1))
            acc_ref[cols, :] += lax.dot_general(dp_ref[:, cols], xn, _TN, preferred_element_type=F32)
        dxn = jnp.dot(dp_ref[...], w_ref[...], preferred_element_type=F32)
        small_ref[0:1, :] += jnp.sum(dxn * n1, axis=0, keepdims=True)
        dxg = dxn * g_ref[...]
        gx_ref[...] = r1 * (dxg - n1 * jnp.mean(dxg * n1, axis=-1, keepdims=True)) + dh2_ref[...]

        @pl.when(last)
        def _():
            pltpu.sync_copy(acc_ref, gw_hbm)

    tile = lambda w: pl.BlockSpec((tb, w), lambda i: (i, 0))
    whole = lambda r, w: pl.BlockSpec((r, w), lambda i: (0, 0))
    halo = pl.BlockSpec((per, BLOCK, KV_W), lambda i: (i, 0, 0))
    halo_next = pl.BlockSpec((1, BLOCK, KV_W), lambda i: (jnp.minimum((i + 1) * per, seq // T_MIX - 1), 0, 0))
    next8 = pl.BlockSpec((8, CONV_W), lambda i: (jnp.minimum((i + 1) * (tb // 8), seq // 8 - 1), 0))
    ch = pl.BlockSpec((tb, 1024), lambda i: (i, 1))
    return pl.pallas_call(
        body, name="bwd_proj", grid=(nt,),
        out_shape=(jax.ShapeDtypeStruct((seq, D_MODEL), F32), jax.ShapeDtypeStruct((IN_W, D_MODEL), F32),
                   jax.ShapeDtypeStruct((SMALL_ROWS, D_MODEL), F32)),
        in_specs=[tile(D_MODEL), whole(1, D_MODEL), tile(D_MODEL), tile(ATTN_W), tile(KV_W), tile(KV_W), halo, halo,
                  halo_next, halo_next,
                  tile(ATTN_W), tile(CONV_W), tile(CONV_W), tile(CONV_W), next8, ch, CONV_SPEC,
                  pl.BlockSpec((IN_W, D_MODEL), lambda i: (0, 0), pipeline_mode=pl.Buffered(1)),
                  tile(128), tile(128), tile(128)],
        out_specs=(tile(D_MODEL), pl.BlockSpec(memory_space=pl.ANY), whole(SMALL_ROWS, D_MODEL)),
        scratch_shapes=[pltpu.VMEM((tb, IN_W), BF16), pltpu.VMEM((IN_W, D_MODEL), F32)],
        compiler_params=_params(("arbitrary",)),
    )(x, norm_g, dh2, dq, dk, dv, dkh, dvh, dkh, dvh, dga, db, dgc, dcv, dcv, rest, conv_w, w_in_t,
      rope_c, rope_a, rope_b)


def _adamw_step(w, g, m, v):
    m2 = ADAM_B1 * m + (1.0 - ADAM_B1) * g
    v2 = ADAM_B2 * v + (1.0 - ADAM_B2) * jnp.square(g)
    m_hat = m2 / (1.0 - ADAM_B1 ** ADAM_STEP)
    v_hat = v2 / (1.0 - ADAM_B2 ** ADAM_STEP)
    return -ADAM_LR * (m_hat / (jnp.sqrt(v_hat) + ADAM_EPS) + ADAM_WD * w), m2, v2


def _adamw_weights(groups):
    steps = 4

    def body(*refs):
        ins, outs = refs[:4 * len(groups)], refs[4 * len(groups):]
        for k in range(len(groups)):
            res = _adamw_step(*(r[...] for r in ins[4 * k:4 * k + 4]))
            for o_ref, val in zip(outs[3 * k:3 * k + 3], res):
                o_ref[...] = val

    in_specs, out_specs, out_shape = [], [], []
    for w, _, _, _ in groups:
        rows, cols = w.shape
        spec = pl.BlockSpec((rows // steps, cols), lambda i: (i, 0))
        in_specs += [spec] * 4
        out_specs += [spec] * 3
        out_shape += [jax.ShapeDtypeStruct((rows, cols), F32)] * 3
    flat = pl.pallas_call(
        body, name="adamw_weights", grid=(steps,), out_shape=tuple(out_shape), in_specs=in_specs,
        out_specs=tuple(out_specs), compiler_params=_params(("arbitrary",)),
    )(*[a for grp in groups for a in grp])
    return [flat[3 * k:3 * k + 3] for k in range(len(groups))]


def _adamw_small(chip, small, params, m, v):
    def body(chip_ref, small_ref, conv_ref, *refs):
        ins, outs = refs[:12], refs[12:]
        outs[0][...] = jnp.sum(small_ref[6:7, :], axis=-1, keepdims=True)
        grads = (small_ref[0:1, :], small_ref[1:2, :], conv_ref[2:5, :], small_ref[5:6, 0:8])
        for k, g in enumerate(grads):
            outs[1 + k][...] = g
            res = _adamw_step(ins[k][...], g, ins[4 + k][...], ins[8 + k][...])
            for n, val in enumerate(res):
                outs[5 + 4 * n + k][...] = val

    full = lambda a: pl.BlockSpec(a.shape, lambda i, c: (0,) * len(a.shape))
    shapes = [jax.ShapeDtypeStruct(p.shape, F32) for p in params]
    outs = [jax.ShapeDtypeStruct((1, 1), F32)] + shapes * 4
    flat = pl.pallas_call(
        body, name="adamw_small",
        grid_spec=pltpu.PrefetchScalarGridSpec(
            num_scalar_prefetch=1, grid=(1,),
            in_specs=[full(small), pl.BlockSpec((SMALL_ROWS, 128), lambda i, c: (0, c[0]))]
            + [full(a) for a in (*params, *m, *v)],
            out_specs=tuple(full(s) for s in outs)),
        out_shape=tuple(outs), compiler_params=_params(("arbitrary",)),
    )(chip, small, small, *params, *m, *v)
    return flat[0], flat[1:5], [flat[5 + 4 * n:9 + 4 * n] for n in range(3)]


def kernel(x, norm_g, w_in, sinks, conv_w, w_out, final_g, loss_target, m_norm_g, m_w_in, m_sinks, m_conv_w, m_w_out, m_final_g, v_norm_g, v_w_in, v_sinks, v_conv_w, v_w_out, v_final_g):
    seq = x.shape[1]
    x2 = x.reshape(seq, D_MODEL)
    tgt = loss_target.reshape(seq, D_MODEL)
    ng = norm_g.reshape(1, D_MODEL)
    fg = final_g.reshape(1, D_MODEL)
    chip = 2 * lax.axis_index("x") + lax.axis_index("y")

    conv_w8 = jnp.zeros((8, 128), F32).at[0:3].set(conv_w)
    w_in_full = _gather_w_in(w_in.T).reshape(IN_W, D_MODEL)

    q, kd, vd, rest, rope_c, rope_a, rope_b, wo_all, cw_all = _fwd_proj(x2, ng, w_in_full, w_out, conv_w8)
    w_out_full = wo_all.reshape(D_MODEL, D_MODEL)
    attn, dh2, g_wo, g_wo_b, small_f = _fwd_mix(x2, q, kd, vd, rest, sinks, cw_all, w_out_full, fg, tgt)
    out_blocks = lambda t: t.reshape(N_CHIPS, W_OUT_BLK, D_MODEL)
    dq, dk, dv, dkh, dvh, dga, db, dgc, dcv, small_m, grad_w_out = _bwd_mix(
        dh2, q, kd, vd, attn, rest, sinks, cw_all, w_out_full, rope_c, rope_a, rope_b,
        out_blocks(g_wo), out_blocks(g_wo_b))
    grad_x, g_wi, small_p = _bwd_proj(x2, ng, dh2, dq, dk, dv, dkh, dvh, dga, db, dgc, dcv, rest, cw_all,
                                      w_in_full, rope_c, rope_a, rope_b)

    g_in_blocks = g_wi.reshape(N_CHIPS, W_IN_BLK, D_MODEL)
    grad_w_in_t, small = _reduce_grads(g_in_blocks, small_f, small_m, small_p)

    (upd_wi, upd_wo) = _adamw_weights([(w_in.T, grad_w_in_t, m_w_in.T, v_w_in.T),
                                       (w_out, grad_w_out, m_w_out, v_w_out)])
    row = lambda t: t.reshape(1, -1)
    loss, grads_s, upd_s = _adamw_small(
        chip.reshape(1), small, (ng, fg, conv_w, row(sinks)),
        (row(m_norm_g), row(m_final_g), m_conv_w, row(m_sinks)),
        (row(v_norm_g), row(v_final_g), v_conv_w, row(v_sinks)))

    def named(ng_, fg_, cw_, sk_, wi_t, wo_):
        return [ng_.reshape(D_MODEL), wi_t.T, sk_.reshape(8), cw_, wo_, fg_.reshape(D_MODEL)]

    g_named = named(*grads_s, grad_w_in_t, grad_w_out)
    out = [loss.reshape(()), grad_x.reshape(1, seq, D_MODEL)] + g_named
    for n in range(3):
        out += named(*upd_s[n], upd_wi[n], upd_wo[n])
    return tuple(out)
```

```python
import jax
import jax.numpy as jnp
from jax import lax
from jax.experimental import pallas as pl
from jax.experimental.pallas import tpu as pltpu

F32 = jnp.float32
BF16 = jnp.bfloat16

D_MODEL = 1024
HEAD_DIM = 64
ATTN_W = 512
KV_W = 128
CONV_W = 512
IN_W = 3328
REST_W = IN_W - ATTN_W - 2 * KV_W
BLOCK = 128
ROT_DIM = 16
ROPE_THETA = 500000.0
EPS = 1e-5
SCALE = 0.125
NEG = -1e30

N_CHIPS = 4
W_IN_BLK = IN_W // N_CHIPS
W_OUT_BLK = D_MODEL // N_CHIPS

ADAM_LR = 0.001
ADAM_B1 = 0.9
ADAM_B2 = 0.999
ADAM_EPS = 1e-08
ADAM_WD = 0.01
ADAM_STEP = 10

VMEM_LIMIT = 56 * 1024 * 1024
T_PROJ = 512
T_FMIX = 512
T_MIX = 512
SMALL_ROWS = 8
MESH = pl.DeviceIdType.MESH

_NT = (((1,), (1,)), ((), ()))
_TN = (((0,), (0,)), ((), ()))


def _params(sem=None):
    kw = dict(vmem_limit_bytes=VMEM_LIMIT)
    if sem is not None:
        kw["dimension_semantics"] = sem
    return pltpu.CompilerParams(**kw)


def _sigmoid(t):
    return 1.0 / (1.0 + jnp.exp(-t))


def _shift_down(t, prev8, k):
    rolled = pltpu.roll(t, k, 0)
    row = lax.broadcasted_iota(jnp.int32, t.shape, 0)
    for j in range(k):
        rolled = jnp.where(row == j, prev8[8 - k + j:8 - k + j + 1, :], rolled)
    return rolled


def _shift_up(t, next8, k):
    n = t.shape[0]
    rolled = pltpu.roll(t, n - k, 0)
    row = lax.broadcasted_iota(jnp.int32, t.shape, 0)
    for j in range(k):
        rolled = jnp.where(row == n - k + j, next8[j:j + 1, :], rolled)
    return rolled


def _rope(t, c, a, b):
    w = t.shape[1]
    reps = w // 128
    if reps > 1:
        c, a, b = (jnp.concatenate([z] * reps, axis=1) for z in (c, a, b))
    return t * c + pltpu.roll(t, w - 8, 1) * a + pltpu.roll(t, 8, 1) * b


def _lane_lo(shape):
    return lax.broadcasted_iota(jnp.int32, shape, 1) < HEAD_DIM


def _stack_heads(t, g):
    lo = _lane_lo((BLOCK, 128))
    parts = []
    for hh in range(4):
        pair = t[:, 256 * g + 128 * (hh // 2):256 * g + 128 * (hh // 2) + 128]
        keep = lo if hh % 2 == 0 else jnp.logical_not(lo)
        parts.append(jnp.where(keep, pair, jnp.zeros_like(pair)))
    return jnp.concatenate(parts, axis=0)


def _unstack_pair(o, pp):
    lo = _lane_lo((BLOCK, 128))
    return jnp.where(lo, o[256 * pp:256 * pp + 128], o[256 * pp + 128:256 * pp + 256])


def _band_masks(has_prev):
    r = lax.broadcasted_iota(jnp.int32, (4 * BLOCK, 2 * BLOCK), 0) % BLOCK
    kj = lax.broadcasted_iota(jnp.int32, (4 * BLOCK, 2 * BLOCK), 1)
    cur = (kj >= BLOCK) & (kj - BLOCK <= r)
    prev = (kj < BLOCK) & (kj > r)
    return cur | (prev & has_prev), cur | prev


def _sink_col(sinks_ref, g):
    r = lax.broadcasted_iota(jnp.int32, (4 * BLOCK, 1), 0) // BLOCK
    col = jnp.full((4 * BLOCK, 1), sinks_ref[4 * g + 3], F32)
    for hh in range(3):
        col = jnp.where(r == hh, sinks_ref[4 * g + hh], col)
    return col


def _probs(qs, kd, sink_col, mask):
    return _softmax(lax.dot_general(qs, kd, _NT, preferred_element_type=F32), sink_col, mask)


def _softmax(s, sink_col, mask):
    s = jnp.where(mask, s, NEG)
    m = jnp.maximum(jnp.max(s, axis=-1, keepdims=True), sink_col)
    p = jnp.exp(s - m)
    es = jnp.exp(sink_col - m)
    inv = 1.0 / (jnp.sum(p, axis=-1, keepdims=True) + es)
    return p * inv, es * inv


def _key_windows(i, nsub, kd_ref, vd_ref, kdp_ref, vdp_ref):
    mask_first, mask_rest = _band_masks(i > 0)
    out = []
    for sb in range(nsub):
        rows = slice(BLOCK * sb, BLOCK * (sb + 1))
        if sb == 0:
            kk = jnp.concatenate([kdp_ref[...], kd_ref[rows, :]], axis=0)
            vv = jnp.concatenate([vdp_ref[...], vd_ref[rows, :]], axis=0)
            out.append((rows, kk, vv, mask_first))
        else:
            both = slice(BLOCK * (sb - 1), BLOCK * (sb + 1))
            out.append((rows, kd_ref[both, :], vd_ref[both, :], mask_rest))
    return out


def _gather_w_in(w_in_t):
    hi = W_IN_BLK // 2
    qr = hi // 2

    def body(wi_ref, wi_all, send_sems, recv_sems):
        x, y, c = lax.axis_index("x"), lax.axis_index("y"), lax.axis_index("c")
        me, sibling = (x, y), (x, y, 1 - c)
        xnb, ynb, diag = (1 - x, y), (x, 1 - y), (1 - x, 1 - y)

        wi_all[2 * x + y] = wi_ref[...].astype(BF16)

        def copy(k, chip, half, quarter, to):
            r = wi_all.at[2 * chip[0] + chip[1], pl.ds(half * hi + quarter * qr, qr)]
            return pltpu.make_async_remote_copy(src_ref=r, dst_ref=r, send_sem=send_sems.at[k],
                                                recv_sem=recv_sems.at[k], device_id=to, device_id_type=MESH)

        plan = [(0, xnb, 0), (1, ynb, 1), (2, xnb, 1), (3, ynb, 0)]
        sent = [copy(k, me, c, quarter, (*nb, c)) for k, nb, quarter in plan]
        for cp in sent:
            cp.start()
        arrivals = [(0, xnb, 0), (1, ynb, 1), (2, xnb, 1), (3, ynb, 0), (4, diag, 0), (5, diag, 1)]
        relay = {0: (4, ynb), 1: (5, xnb)}
        for k, chip, quarter in arrivals:
            copy(k, chip, c, quarter, (x, y, c)).wait_recv()
            if k in relay:
                sent.append(copy(relay[k][0], chip, c, quarter, (*relay[k][1], c)))
                sent[-1].start()
            sent.append(copy(6 + k, chip, c, quarter, sibling))
            sent[-1].start()
        for k, chip, quarter in arrivals:
            copy(6 + k, chip, 1 - c, quarter, (x, y, c)).wait_recv()
        for cp in sent:
            cp.wait_send()

    vmem = pl.BlockSpec(memory_space=pltpu.VMEM)
    return pl.pallas_call(
        body, name="gather_w_in",
        out_shape=jax.ShapeDtypeStruct((N_CHIPS, W_IN_BLK, D_MODEL), BF16),
        in_specs=[vmem], out_specs=vmem,
        scratch_shapes=[pltpu.SemaphoreType.DMA((12,)), pltpu.SemaphoreType.DMA((12,))],
        compiler_params=_params(),
    )(w_in_t)


def _reduce_grads(g_in, *smalls):
    hi = W_IN_BLK // 2
    qr = hi // 2
    q0, q1 = slice(0, qr), slice(qr, hi)

    def body(gi_hbm, s0_ref, s1_ref, s2_ref, gi_out, small_out,
             mine_i, sib_i, out_i, ici_i, small_in, small_ref, send_sems, recv_sems, local_sems):
        x, y, c = lax.axis_index("x"), lax.axis_index("y"), lax.axis_index("c")
        my_dev = 4 * x + 2 * y + c
        sibling = (x, y, 1 - c)
        xnb, ynb = (1 - x, y, c), (x, 1 - y, c)
        order = [(1 - x, 1 - y), (1 - x, y), (x, 1 - y), (x, y)]

        def remote(k, src, dst, to):
            return pltpu.make_async_remote_copy(src_ref=src, dst_ref=dst, send_sem=send_sems.at[k],
                                                recv_sem=recv_sems.at[k], device_id=to, device_id_type=MESH)

        small_ref[...] = s0_ref[...] + s1_ref[...] + s2_ref[...]
        small_cps = []
        for f in range(1, 8):
            fx, fy, fc = f >> 2, (f >> 1) & 1, f & 1
            small_cps.append(remote(11 + f - 1, small_ref, small_in.at[f - 1], (x ^ fx, y ^ fy, c ^ fc)))
        for cp in small_cps:
            cp.start()

        own, to_sib = [], []
        for n, chip in enumerate(order):
            j = 2 * chip[0] + chip[1]
            own.append(pltpu.make_async_copy(gi_hbm.at[j, pl.ds(c * hi, hi)], mine_i.at[n], local_sems.at[n]))
            to_sib.append(remote(6 + n, gi_hbm.at[j, pl.ds((1 - c) * hi, hi)], sib_i.at[n], sibling))
            own[-1].start()
            to_sib[-1].start()

        def pair_sum(n):
            own[n].wait()
            to_sib[n].wait_recv()
            return mine_i[n] + sib_i[n]

        ici = [remote(k, out_i.at[k], ici_i.at[k], xnb if k % 2 == 0 else ynb) for k in range(6)]

        def send(k, rows_f32):
            out_i[k] = rows_f32.astype(BF16)
            ici[k].start()

        p_diag = pair_sum(0)
        send(0, p_diag[q0])
        send(1, p_diag[q1])
        p_x = pair_sum(1)
        send(2, p_x[q0])
        p_y = pair_sum(2)
        send(3, p_y[q1])
        ici[0].wait_recv()
        send(5, p_y[q0] + ici_i[0].astype(F32))
        ici[1].wait_recv()
        send(4, p_x[q1] + ici_i[1].astype(F32))
        p_mine = pair_sum(3)
        for k in range(2, 6):
            ici[k].wait_recv()
        gi_out[pl.ds(c * hi, qr), :] = p_mine[q0] + ici_i[2].astype(F32) + ici_i[5].astype(F32)
        gi_out[pl.ds(c * hi + qr, qr), :] = p_mine[q1] + ici_i[4].astype(F32) + ici_i[3].astype(F32)

        swap = [remote(10, gi_out.at[pl.ds(c * hi, hi)], gi_out.at[pl.ds(c * hi, hi)], sibling)]
        for cp in swap:
            cp.start()

        for cp in small_cps:
            cp.wait_recv()
        total = jnp.zeros((SMALL_ROWS, D_MODEL), F32)
        for d in range(8):
            slot = jnp.maximum((d ^ my_dev) - 1, 0)
            total = total + jnp.where(d == my_dev, small_ref[...], small_in[slot])
        small_out[...] = total

        remote(10, gi_out.at[pl.ds((1 - c) * hi, hi)], gi_out.at[pl.ds((1 - c) * hi, hi)], sibling).wait_recv()
        for cp in to_sib + ici + swap + small_cps:
            cp.wait_send()

    vmem = pl.BlockSpec(memory_space=pltpu.VMEM)
    anyspace = pl.BlockSpec(memory_space=pl.ANY)
    return pl.pallas_call(
        body, name="reduce_grads",
        out_shape=(jax.ShapeDtypeStruct((W_IN_BLK, D_MODEL), F32), jax.ShapeDtypeStruct((SMALL_ROWS, D_MODEL), F32)),
        in_specs=[anyspace, vmem, vmem, vmem], out_specs=(vmem, vmem),
        scratch_shapes=[pltpu.VMEM((N_CHIPS, hi, D_MODEL), F32), pltpu.VMEM((N_CHIPS, hi, D_MODEL), F32),
                        pltpu.VMEM((6, qr, D_MODEL), BF16), pltpu.VMEM((6, qr, D_MODEL), BF16),
                        pltpu.VMEM((7, SMALL_ROWS, D_MODEL), F32), pltpu.VMEM((SMALL_ROWS, D_MODEL), F32),
                        pltpu.SemaphoreType.DMA((18,)), pltpu.SemaphoreType.DMA((18,)),
                        pltpu.SemaphoreType.DMA((4,))],
        compiler_params=_params(),
    )(g_in, *smalls)


def _fwd_proj(x, norm_g, w_in_t, w_out, conv_w8):
    seq = x.shape[0]
    nt = seq // T_PROJ
    lane = jnp.arange(128, dtype=jnp.int32) % HEAD_DIM
    inv_freq = ROPE_THETA ** (-(2 * (lane % 8)).astype(F32) / ROT_DIM)
    inv_freq = jnp.where(lane < ROT_DIM, inv_freq, 0.0).reshape(1, 128)
    in_tile = jnp.arange(T_PROJ, dtype=jnp.int32).astype(F32)[:, None] * inv_freq
    cos_in, sin_in = jnp.cos(in_tile), jnp.sin(in_tile)
    start = jnp.repeat((jnp.arange(nt, dtype=jnp.int32) * T_PROJ).astype(F32), 8)[:, None] * inv_freq
    cos_st, sin_st = jnp.cos(start), jnp.sin(start)

    def body(x_ref, g_ref, w_ref, cs_ref, ss_ref, ci_ref, si_ref, wo_ref, cw_ref,
             q_ref, kd_ref, vd_ref, rest_ref, c_ref, a_ref, b_ref, wo_all, cw_all,
             wo_stage, send_sems, recv_sems, local_sems):
        i = pl.program_id(0)
        mx, my, mc = lax.axis_index("x"), lax.axis_index("y"), lax.axis_index("c")
        chips = [(1 - mx, my), (mx, 1 - my), (1 - mx, 1 - my)]

        def gather(blocks):
            cps = []
            for k, chip in enumerate(chips):
                for n, (src, dst) in enumerate(((wo_stage, wo_all), (cw_ref, cw_all))):
                    cps.append(pltpu.make_async_remote_copy(
                        src_ref=src, dst_ref=dst.at[blocks[k]], send_sem=send_sems.at[2 * k + n],
                        recv_sem=recv_sems.at[2 * k + n], device_id=(*chip, mc), device_id_type=MESH))
            return cps

        me = 2 * mx + my
        own = [pltpu.make_async_copy(wo_stage, wo_all.at[me], local_sems.at[0]),
               pltpu.make_async_copy(cw_ref, cw_all.at[me], local_sems.at[1])]

        @pl.when(i == 0)
        def _():
            wo_stage[...] = wo_ref[...].astype(BF16)
            for cp in own + gather([me] * 3):
                cp.start()

        xf = x_ref[...]
        r1 = lax.rsqrt(jnp.mean(xf * xf, axis=-1, keepdims=True) + EPS)
        xn = (xf * r1 * g_ref[...]).astype(BF16)
        cs, ss = cs_ref[0:1, :], ss_ref[0:1, :]
        c = cs * ci_ref[...] - ss * si_ref[...]
        sin = ss * ci_ref[...] + cs * si_ref[...]
        j = lax.broadcasted_iota(jnp.int32, (T_PROJ, 128), 1) % HEAD_DIM
        a = jnp.where(j < 8, -sin, 0.0)
        b = jnp.where(j >= 8, sin, 0.0)
        c_ref[...], a_ref[...], b_ref[...] = c, a, b
        proj = lambda lo_c, w: lax.dot_general(xn, w_ref[lo_c:lo_c + w, :], _NT, preferred_element_type=F32)
        q_ref[...] = (_rope(proj(0, ATTN_W), c, a, b) * SCALE).astype(BF16)
        kv = proj(ATTN_W, 2 * KV_W)
        k = _rope(kv[:, 0:KV_W], c, a, b)
        v = kv[:, KV_W:2 * KV_W]
        lo = _lane_lo(k.shape)
        for t, ref in ((k, kd_ref), (v, vd_ref)):
            sw = pltpu.roll(t, HEAD_DIM, 1)
            ref[:, 0:128] = jnp.where(lo, t, sw).astype(BF16)
            ref[:, 128:256] = jnp.where(lo, sw, t).astype(BF16)
        for n in range(REST_W // 512):
            rest_ref[:, 512 * n:512 * (n + 1)] = proj(ATTN_W + 2 * KV_W + 512 * n, 512)

        @pl.when(i == nt - 1)
        def _():
            sent = gather([me] * 3)
            for cp in gather([2 * chip[0] + chip[1] for chip in chips]):
                cp.wait_recv()
            for cp in sent:
                cp.wait_send()
            for cp in own:
                cp.wait()

    tile = lambda w: pl.BlockSpec((T_PROJ, w), lambda i: (i, 0))
    whole = lambda r, w: pl.BlockSpec((r, w), lambda i: (0, 0))
    vmem = pl.BlockSpec(memory_space=pltpu.VMEM)
    hbm = pl.BlockSpec(memory_space=pl.ANY)
    return pl.pallas_call(
        body, name="fwd_proj", grid=(nt,),
        out_shape=(jax.ShapeDtypeStruct((seq, ATTN_W), BF16), jax.ShapeDtypeStruct((seq, 2 * KV_W), BF16),
                   jax.ShapeDtypeStruct((seq, 2 * KV_W), BF16), jax.ShapeDtypeStruct((seq, REST_W), F32))
        + (jax.ShapeDtypeStruct((seq, 128), F32),) * 3
        + (jax.ShapeDtypeStruct((N_CHIPS, W_OUT_BLK, D_MODEL), BF16), jax.ShapeDtypeStruct((N_CHIPS, 8, 128), F32)),
        in_specs=[tile(D_MODEL), whole(1, D_MODEL), whole(IN_W, D_MODEL), pl.BlockSpec((8, 128), lambda i: (i, 0)),
                  pl.BlockSpec((8, 128), lambda i: (i, 0)), whole(T_PROJ, 128), whole(T_PROJ, 128), vmem, vmem],
        out_specs=(tile(ATTN_W), tile(2 * KV_W), tile(2 * KV_W), tile(REST_W), tile(128), tile(128), tile(128),
                   hbm, hbm),
        scratch_shapes=[pltpu.VMEM((W_OUT_BLK, D_MODEL), BF16), pltpu.SemaphoreType.DMA((6,)),
                        pltpu.SemaphoreType.DMA((6,)), pltpu.SemaphoreType.DMA((2,))],
        compiler_params=_params(("arbitrary",)),
    )(x, norm_g, w_in_t, cos_st, sin_st, cos_in, sin_in, w_out, conv_w8)


CONV_SPEC = pl.BlockSpec((N_CHIPS, 8, 128), lambda i: (0, 0, 0))


def _conv_rows(cw_ref):
    return jnp.concatenate([cw_ref[j] for j in range(N_CHIPS)], axis=1)


def _conv_parts(rest_ref, prev_ref, cw_ref, first):
    u = rest_ref[:, 1024:1536] * rest_ref[:, 1536:2048]
    up = prev_ref[:, 1024:1536] * prev_ref[:, 1536:2048]
    up = jnp.where(first, jnp.zeros_like(up), up)
    um1 = _shift_down(u, up, 1)
    um2 = _shift_down(u, up, 2)
    cw = _conv_rows(cw_ref)
    cv = cw[0:1, :] * um2 + cw[1:2, :] * um1 + cw[2:3, :] * u
    return u, um1, um2, cv


def _fwd_mix(x, q, kd, vd, rest, sinks, conv_w, w_out, final_g, target):
    seq = x.shape[0]
    nt = seq // T_FMIX
    nsub = T_FMIX // BLOCK

    def body(sinks_ref, x_ref, q_ref, kd_ref, vd_ref, kdp_ref, vdp_ref, rest_ref, restp_ref, cw_ref, wo_ref,
             fg_ref, tgt_ref, attn_ref, dh2_ref, gwo_ref, gwob_ref, small_ref, mix_ref, pmix_ref, pdh2_ref):
        i = pl.program_id(0)

        @pl.when(i == 0)
        def _():
            small_ref[...] = jnp.zeros_like(small_ref)
            gwo_ref[...] = jnp.zeros_like(gwo_ref)
            pmix_ref[...] = jnp.zeros_like(pmix_ref)
            pdh2_ref[...] = jnp.zeros_like(pdh2_ref)

        gwo_ref[...] += lax.dot_general(pmix_ref[...], pdh2_ref[...], _TN, preferred_element_type=F32)

        chains = []
        for rows, kk, vv, mask in _key_windows(i, nsub, kd_ref, vd_ref, kdp_ref, vdp_ref):
            qt = q_ref[rows, :]
            for g in range(2):
                kg = kk[:, 128 * g:128 * (g + 1)]
                chains.append(dict(g=g, rows=rows, mask=mask, vg=vv[:, 128 * g:128 * (g + 1)],
                                   s=lax.dot_general(_stack_heads(qt, g), kg, _NT, preferred_element_type=F32)))
        for ch in chains:
            ch["prob"], _ = _softmax(ch.pop("s"), _sink_col(sinks_ref, ch["g"]), ch["mask"])
        for ch in chains:
            o = jnp.dot(ch["prob"].astype(BF16), ch["vg"], preferred_element_type=F32)
            for pp in range(2):
                lanes = slice(256 * ch["g"] + 128 * pp, 256 * ch["g"] + 128 * (pp + 1))
                attn_ref[ch["rows"], lanes] = _unstack_pair(o, pp)

        ga = rest_ref[:, 0:512]
        mix_ref[:, 0:ATTN_W] = (attn_ref[...] * (ga * _sigmoid(ga))).astype(BF16)
        _, _, _, cv = _conv_parts(rest_ref, restp_ref, cw_ref, i == 0)
        gc = rest_ref[:, 2048:2560]
        mix_ref[:, ATTN_W:] = (rest_ref[:, 512:1024] * cv * (gc * _sigmoid(gc))).astype(BF16)

        h2 = x_ref[...] + jnp.dot(mix_ref[...], wo_ref[...], preferred_element_type=F32)
        r2 = lax.rsqrt(jnp.mean(h2 * h2, axis=-1, keepdims=True) + EPS)
        n2 = h2 * r2
        err = n2 * fg_ref[...] - tgt_ref[...]
        dy = err * (1.0 / D_MODEL)
        small_ref[6:7, :] += jnp.sum(err * err, axis=0, keepdims=True) * (0.5 / D_MODEL)
        small_ref[1:2, :] += jnp.sum(dy * n2, axis=0, keepdims=True)
        dn = dy * fg_ref[...]
        dh2 = r2 * (dn - n2 * jnp.mean(dn * n2, axis=-1, keepdims=True))
        dh2_ref[...] = dh2
        pmix_ref[...] = mix_ref[...]
        pdh2_ref[...] = dh2.astype(BF16)

        @pl.when(i == nt - 1)
        def _():
            gwo_ref[...] += lax.dot_general(pmix_ref[...], pdh2_ref[...], _TN, preferred_element_type=F32)
            gwob_ref[...] = gwo_ref[...].astype(BF16)

    tile = lambda w: pl.BlockSpec((T_FMIX, w), lambda i: (i, 0))
    whole = lambda r, w: pl.BlockSpec((r, w), lambda i: (0, 0))
    prev_blk = pl.BlockSpec((BLOCK, 2 * KV_W), lambda i: (jnp.maximum(i * nsub - 1, 0), 0))
    prev8 = pl.BlockSpec((8, REST_W), lambda i: (jnp.maximum(i * (T_FMIX // 8) - 1, 0), 0))
    return pl.pallas_call(
        body, name="fwd_mix", grid=(nt,),
        out_shape=(jax.ShapeDtypeStruct((seq, ATTN_W), F32), jax.ShapeDtypeStruct((seq, D_MODEL), F32),
                   jax.ShapeDtypeStruct((D_MODEL, D_MODEL), F32), jax.ShapeDtypeStruct((D_MODEL, D_MODEL), BF16),
                   jax.ShapeDtypeStruct((SMALL_ROWS, D_MODEL), F32)),
        in_specs=[pl.BlockSpec(memory_space=pltpu.SMEM), tile(D_MODEL), tile(ATTN_W), tile(2 * KV_W), tile(2 * KV_W),
                  prev_blk, prev_blk, tile(REST_W), prev8, CONV_SPEC, whole(D_MODEL, D_MODEL),
                  whole(1, D_MODEL), tile(D_MODEL)],
        out_specs=(tile(ATTN_W), tile(D_MODEL), whole(D_MODEL, D_MODEL), whole(D_MODEL, D_MODEL),
                   whole(SMALL_ROWS, D_MODEL)),
        scratch_shapes=[pltpu.VMEM((T_FMIX, D_MODEL), BF16)] * 3,
        compiler_params=_params(("arbitrary",)),
    )(sinks, x, q, kd, vd, kd, vd, rest, rest, conv_w, w_out, final_g, target)


def _scatter_copies(g_hbm, gb_hbm, mine, land, send_sems, recv_sems, local_sem, half):
    x, y, c = lax.axis_index("x"), lax.axis_index("y"), lax.axis_index("c")
    cps = []
    for f in range(1, 8):
        to = (x ^ (f >> 2), y ^ ((f >> 1) & 1), c ^ (f & 1))
        src = gb_hbm.at[2 * to[0] + to[1], pl.ds(to[2] * half, half)]
        cps.append(pltpu.make_async_remote_copy(src_ref=src, dst_ref=land.at[f - 1], send_sem=send_sems.at[f - 1],
                                                recv_sem=recv_sems.at[f - 1], device_id=to, device_id_type=MESH))
    own = pltpu.make_async_copy(g_hbm.at[2 * x + y, pl.ds(c * half, half)], mine, local_sem)
    return cps, own


def _scatter_finish(cps, own, mine, land, out_hbm, send_sems, recv_sems, local_sem, half):
    x, y, c = lax.axis_index("x"), lax.axis_index("y"), lax.axis_index("c")
    own.wait()
    tot = mine[...]
    for f in range(1, 8):
        cps[f - 1].wait_recv()
        tot = tot + land[f - 1].astype(F32)
    mine[...] = tot

    def swap(rows_of):
        return pltpu.make_async_remote_copy(src_ref=mine, dst_ref=out_hbm.at[pl.ds(rows_of * half, half)],
                                            send_sem=send_sems.at[7], recv_sem=recv_sems.at[7],
                                            device_id=(x, y, 1 - c), device_id_type=MESH)

    keep = pltpu.make_async_copy(mine, out_hbm.at[pl.ds(c * half, half)], local_sem)
    keep.start()
    swap(c).start()
    swap(1 - c).wait_recv()
    keep.wait()
    for cp in cps:
        cp.wait_send()
    swap(c).wait_send()


def _bwd_mix(dh2, q, kd, vd, attn, rest, sinks, conv_w, w_out, rope_c, rope_a, rope_b, g_out, g_out_b):
    seq = dh2.shape[0]
    nt = seq // T_MIX
    nsub = T_MIX // BLOCK
    ho = W_OUT_BLK // 2

    def body(sinks_ref, dh2_ref, q_ref, kd_ref, vd_ref, kdp_ref, vdp_ref, attn_ref, rest_ref, restp_ref,
             cw_ref, wo_ref, c_ref, a_ref, b_ref, go_hbm, gob_hbm,
             dq_ref, dk_ref, dv_ref, dkh_ref, dvh_ref, dga_ref, db_ref, dgc_ref, dcv_ref, small_ref, go_out,
             dmix_ref, dsink_ref, mine_o, land_o, send_sems, recv_sems, local_sems):
        i = pl.program_id(0)
        scatter = (mine_o, land_o, send_sems, recv_sems, local_sems.at[0], ho)

        @pl.when(i == 0)
        def _():
            small_ref[...] = jnp.zeros_like(small_ref)
            dsink_ref[...] = jnp.zeros_like(dsink_ref)
            cps, own = _scatter_copies(go_hbm, gob_hbm, *scatter)
            for cp in cps + [own]:
                cp.start()

        dmix_ref[...] = lax.dot_general(dh2_ref[...].astype(BF16), wo_ref[...], _NT, preferred_element_type=F32)

        ga = rest_ref[:, 0:512]
        sg = _sigmoid(ga)
        dma = dmix_ref[:, 0:ATTN_W]
        dga_ref[...] = (dma * attn_ref[...] * (sg * (1.0 + ga * (1.0 - sg)))).astype(BF16)
        dmix_ref[:, 0:ATTN_W] = dma * (ga * sg)

        u, um1, um2, cv = _conv_parts(rest_ref, restp_ref, cw_ref, i == 0)
        gc = rest_ref[:, 2048:2560]
        sc = _sigmoid(gc)
        bg = rest_ref[:, 512:1024]
        dmc = dmix_ref[:, ATTN_W:]
        t1 = dmc * (gc * sc)
        db_ref[...] = (t1 * cv).astype(BF16)
        dcv = t1 * bg
        dcv_ref[...] = dcv
        dgc_ref[...] = (dmc * (bg * cv) * (sc * (1.0 + gc * (1.0 - sc)))).astype(BF16)
        small_ref[2:3, 0:CONV_W] += jnp.sum(dcv * um2, axis=0, keepdims=True)
        small_ref[3:4, 0:CONV_W] += jnp.sum(dcv * um1, axis=0, keepdims=True)
        small_ref[4:5, 0:CONV_W] += jnp.sum(dcv * u, axis=0, keepdims=True)

        lo = _lane_lo((2 * BLOCK, 128))
        dk_blocks = [None] * (nsub + 1)
        dv_blocks = [None] * (nsub + 1)

        def add(lst, n, val):
            lst[n] = val if lst[n] is None else lst[n] + val

        chains = []
        for rows, kk, vv, mask in _key_windows(i, nsub, kd_ref, vd_ref, kdp_ref, vdp_ref):
            qt = q_ref[rows, :]
            dot = dmix_ref[rows, 0:ATTN_W].astype(BF16)
            for g in range(2):
                chains.append(dict(g=g, rows=rows, mask=mask, qs=_stack_heads(qt, g), dos=_stack_heads(dot, g),
                                   kg=kk[:, 128 * g:128 * (g + 1)], vg=vv[:, 128 * g:128 * (g + 1)]))
        for ch in chains:
            ch["prob"], ch["psink"] = _probs(ch["qs"], ch["kg"], _sink_col(sinks_ref, ch["g"]), ch["mask"])
        for ch in chains:
            ch["dp"] = lax.dot_general(ch["dos"], ch["vg"], _NT, preferred_element_type=F32)
        for ch in chains:
            rs = jnp.sum(ch["prob"] * ch["dp"], axis=-1, keepdims=True)
            ch["ds"] = (ch["prob"] * (ch["dp"] - rs)).astype(BF16)
            dsink_ref[ch["g"]] += -ch["psink"] * rs
        for ch in chains:
            dqs = jnp.dot(ch["ds"], ch["kg"], preferred_element_type=F32) * SCALE
            c, a, b = c_ref[ch["rows"], :], a_ref[ch["rows"], :], b_ref[ch["rows"], :]
            for pp in range(2):
                lanes = slice(256 * ch["g"] + 128 * pp, 256 * ch["g"] + 128 * (pp + 1))
                dq_ref[ch["rows"], lanes] = _rope(_unstack_pair(dqs, pp), c, -a, -b).astype(BF16)
            dkd = lax.dot_general(ch["ds"], ch["qs"], _TN, preferred_element_type=F32)
            dvd = lax.dot_general(ch["prob"].astype(BF16), ch["dos"], _TN, preferred_element_type=F32)
            ch["dk"] = dkd + pltpu.roll(dkd, HEAD_DIM, 1)
            ch["dv"] = dvd + pltpu.roll(dvd, HEAD_DIM, 1)
        for sb in range(nsub):
            dk2 = jnp.where(lo, chains[2 * sb]["dk"], chains[2 * sb + 1]["dk"])
            dv2 = jnp.where(lo, chains[2 * sb]["dv"], chains[2 * sb + 1]["dv"])
            add(dk_blocks, sb, dk2[0:BLOCK])
            add(dk_blocks, sb + 1, dk2[BLOCK:])
            add(dv_blocks, sb, dv2[0:BLOCK])
            add(dv_blocks, sb + 1, dv2[BLOCK:])
        dkh_ref[0] = dk_blocks[0]
        dvh_ref[0] = dv_blocks[0]
        for sb in range(nsub):
            dk_ref[BLOCK * sb:BLOCK * (sb + 1), :] = dk_blocks[sb + 1]
            dv_ref[BLOCK * sb:BLOCK * (sb + 1), :] = dv_blocks[sb + 1]

        @pl.when(i == nt - 1)
        def _():
            for h in range(8):
                tot = jnp.sum(dsink_ref[h // 4, BLOCK * (h % 4):BLOCK * (h % 4 + 1), :], axis=0, keepdims=True)
                small_ref[5:6, h:h + 1] = tot
            cps, own = _scatter_copies(go_hbm, gob_hbm, *scatter)
            _scatter_finish(cps, own, mine_o, land_o, go_out, send_sems, recv_sems, local_sems.at[1], ho)

    tile = lambda w: pl.BlockSpec((T_MIX, w), lambda i: (i, 0))
    whole = lambda r, w: pl.BlockSpec((r, w), lambda i: (0, 0))
    prev_blk = pl.BlockSpec((BLOCK, 2 * KV_W), lambda i: (jnp.maximum(i * nsub - 1, 0), 0))
    prev8 = pl.BlockSpec((8, REST_W), lambda i: (jnp.maximum(i * (T_MIX // 8) - 1, 0), 0))
    halo = pl.BlockSpec((1, BLOCK, KV_W), lambda i: (i, 0, 0))
    hbm = pl.BlockSpec(memory_space=pl.ANY)
    bf = lambda w: jax.ShapeDtypeStruct((seq, w), BF16)
    f32 = lambda w: jax.ShapeDtypeStruct((seq, w), F32)
    return pl.pallas_call(
        body, name="bwd_mix", grid=(nt,),
        out_shape=(bf(ATTN_W), f32(KV_W), f32(KV_W), jax.ShapeDtypeStruct((nt, BLOCK, KV_W), F32),
                   jax.ShapeDtypeStruct((nt, BLOCK, KV_W), F32), bf(ATTN_W), bf(CONV_W), bf(CONV_W), f32(CONV_W),
                   jax.ShapeDtypeStruct((SMALL_ROWS, D_MODEL), F32), jax.ShapeDtypeStruct((W_OUT_BLK, D_MODEL), F32)),
        in_specs=[pl.BlockSpec(memory_space=pltpu.SMEM), tile(D_MODEL), tile(ATTN_W), tile(2 * KV_W), tile(2 * KV_W),
                  prev_blk, prev_blk, tile(ATTN_W), tile(REST_W), prev8, CONV_SPEC,
                  whole(D_MODEL, D_MODEL), tile(128), tile(128), tile(128), hbm, hbm],
        out_specs=(tile(ATTN_W), tile(KV_W), tile(KV_W), halo, halo, tile(ATTN_W), tile(CONV_W), tile(CONV_W),
                   tile(CONV_W), whole(SMALL_ROWS, D_MODEL), hbm),
        scratch_shapes=[pltpu.VMEM((T_MIX, D_MODEL), F32), pltpu.VMEM((2, 4 * BLOCK, 1), F32),
                        pltpu.VMEM((ho, D_MODEL), F32), pltpu.VMEM((7, ho, D_MODEL), BF16),
                        pltpu.SemaphoreType.DMA((8,)), pltpu.SemaphoreType.DMA((8,)), pltpu.SemaphoreType.DMA((2,))],
        compiler_params=_params(("arbitrary",)),
    )(sinks, dh2, q, kd, vd, kd, vd, attn, rest, rest, conv_w, w_out, rope_c, rope_a, rope_b, g_out, g_out_b)


def _bwd_proj(x, norm_g, dh2, dq, dk, dv, dkh, dvh, dga, db, dgc, dcv, rest, conv_w, w_in_t, rope_c, rope_a, rope_b):
    seq = x.shape[0]
    tb = T_PROJ
    per = tb // T_MIX
    nt = seq // tb

    def body(x_ref, g_ref, dh2_ref, dq_ref, dk_ref, dv_ref, dkh_ref, dvh_ref, dkn_ref, dvn_ref, dga_ref, db_ref,
             dgc_ref, dcv_ref, dcvn_ref, ch_ref, cw_ref, w_ref, c_ref, a_ref, b_ref, gx_ref, gw_hbm, small_ref,
             dp_ref, acc_ref):
        i = pl.program_id(0)

        @pl.when(i == 0)
        def _():
            small_ref[...] = jnp.zeros_like(small_ref)
            acc_ref[...] = jnp.zeros_like(acc_ref)

        last = i == nt - 1
        keep = jnp.where(last, 0.0, 1.0)
        pad = jnp.zeros((T_MIX - BLOCK, KV_W), F32)

        def with_halos(main_ref, halo_ref, next_ref):
            parts = []
            for m in range(1, per + 1):
                parts += [pad, halo_ref[m] if m < per else next_ref[0] * keep]
            return main_ref[...] + jnp.concatenate(parts, axis=0)

        dk = with_halos(dk_ref, dkh_ref, dkn_ref)
        dv = with_halos(dv_ref, dvh_ref, dvn_ref)
        dp_ref[:, 0:ATTN_W] = dq_ref[...]
        dp_ref[:, ATTN_W:ATTN_W + KV_W] = _rope(dk, c_ref[...], -a_ref[...], -b_ref[...]).astype(BF16)
        dp_ref[:, ATTN_W + KV_W:ATTN_W + 2 * KV_W] = dv.astype(BF16)
        base = ATTN_W + 2 * KV_W
        dp_ref[:, base:base + 512] = dga_ref[...]
        dp_ref[:, base + 512:base + 1024] = db_ref[...]
        dcv = dcv_ref[...]
        nxt = dcvn_ref[...] * keep
        cw = _conv_rows(cw_ref)
        du = cw[2:3, :] * dcv + cw[1:2, :] * _shift_up(dcv, nxt, 1) + cw[0:1, :] * _shift_up(dcv, nxt, 2)
        dp_ref[:, base + 1024:base + 1536] = (du * ch_ref[:, 512:1024]).astype(BF16)
        dp_ref[:, base + 1536:base + 2048] = (du * ch_ref[:, 0:512]).astype(BF16)
        dp_ref[:, base + 2048:base + 2560] = dgc_ref[...]

        xf = x_ref[...]
        r1 = lax.rsqrt(jnp.mean(xf * xf, axis=-1, keepdims=True) + EPS)
        n1 = xf * r1
        xn = (n1 * g_ref[...]).astype(BF16)
        for n in range(IN_W // 256):
            cols = slice(256 * n, 256 * (n + 1))
            acc_ref[cols, :] += lax.dot_general(dp_ref[:, cols], xn, _TN, preferred_element_type=F32)
        dxn = jnp.dot(dp_ref[...], w_ref[...], preferred_element_type=F32)
        small_ref[0:1, :] += jnp.sum(dxn * n1, axis=0, keepdims=True)
        dxg = dxn * g_ref[...]
        gx_ref[...] = r1 * (dxg - n1 * jnp.mean(dxg * n1, axis=-1, keepdims=True)) + dh2_ref[...]

        @pl.when(last)
        def _():
            pltpu.sync_copy(acc_ref, gw_hbm)

    tile = lambda w: pl.BlockSpec((tb, w), lambda i: (i, 0))
    whole = lambda r, w: pl.BlockSpec((r, w), lambda i: (0, 0))
    halo = pl.BlockSpec((per, BLOCK, KV_W), lambda i: (i, 0, 0))
    halo_next = pl.BlockSpec((1, BLOCK, KV_W), lambda i: (jnp.minimum((i + 1) * per, seq // T_MIX - 1), 0, 0))
    next8 = pl.BlockSpec((8, CONV_W), lambda i: (jnp.minimum((i + 1) * (tb // 8), seq // 8 - 1), 0))
    ch = pl.BlockSpec((tb, 1024), lambda i: (i, 1))
    return pl.pallas_call(
        body, name="bwd_proj", grid=(nt,),
        out_shape=(jax.ShapeDtypeStruct((seq, D_MODEL), F32), jax.ShapeDtypeStruct((IN_W, D_MODEL), F32),
                   jax.ShapeDtypeStruct((SMALL_ROWS, D_MODEL), F32)),
        in_specs=[tile(D_MODEL), whole(1, D_MODEL), tile(D_MODEL), tile(ATTN_W), tile(KV_W), tile(KV_W), halo, halo,
                  halo_next, halo_next,
                  tile(ATTN_W), tile(CONV_W), tile(CONV_W), tile(CONV_W), next8, ch, CONV_SPEC,
                  pl.BlockSpec((IN_W, D_MODEL), lambda i: (0, 0), pipeline_mode=pl.Buffered(1)),
                  tile(128), tile(128), tile(128)],
        out_specs=(tile(D_MODEL), pl.BlockSpec(memory_space=pl.ANY), whole(SMALL_ROWS, D_MODEL)),
        scratch_shapes=[pltpu.VMEM((tb, IN_W), BF16), pltpu.VMEM((IN_W, D_MODEL), F32)],
        compiler_params=_params(("arbitrary",)),
    )(x, norm_g, dh2, dq, dk, dv, dkh, dvh, dkh, dvh, dga, db, dgc, dcv, dcv, rest, conv_w, w_in_t,
      rope_c, rope_a, rope_b)


def _adamw_step(w, g, m, v):
    m2 = ADAM_B1 * m + (1.0 - ADAM_B1) * g
    v2 = ADAM_B2 * v + (1.0 - ADAM_B2) * jnp.square(g)
    m_hat = m2 / (1.0 - ADAM_B1 ** ADAM_STEP)
    v_hat = v2 / (1.0 - ADAM_B2 ** ADAM_STEP)
    return -ADAM_LR * (m_hat / (jnp.sqrt(v_hat) + ADAM_EPS) + ADAM_WD * w), m2, v2


def _adamw_weights(groups):
    steps = 4

    def body(*refs):
        ins, outs = refs[:4 * len(groups)], refs[4 * len(groups):]
        for k in range(len(groups)):
            res = _adamw_step(*(r[...] for r in ins[4 * k:4 * k + 4]))
            for o_ref, val in zip(outs[3 * k:3 * k + 3], res):
                o_ref[...] = val

    in_specs, out_specs, out_shape = [], [], []
    for w, _, _, _ in groups:
        rows, cols = w.shape
        spec = pl.BlockSpec((rows // steps, cols), lambda i: (i, 0))
        in_specs += [spec] * 4
        out_specs += [spec] * 3
        out_shape += [jax.ShapeDtypeStruct((rows, cols), F32)] * 3
    flat = pl.pallas_call(
        body, name="adamw_weights", grid=(steps,), out_shape=tuple(out_shape), in_specs=in_specs,
        out_specs=tuple(out_specs), compiler_params=_params(("arbitrary",)),
    )(*[a for grp in groups for a in grp])
    return [flat[3 * k:3 * k + 3] for k in range(len(groups))]


def _adamw_small(chip, small, params, m, v):
    def body(chip_ref, small_ref, conv_ref, *refs):
        ins, outs = refs[:12], refs[12:]
        outs[0][...] = jnp.sum(small_ref[6:7, :], axis=-1, keepdims=True)
        grads = (small_ref[0:1, :], small_ref[1:2, :], conv_ref[2:5, :], small_ref[5:6, 0:8])
        for k, g in enumerate(grads):
            outs[1 + k][...] = g
            res = _adamw_step(ins[k][...], g, ins[4 + k][...], ins[8 + k][...])
            for n, val in enumerate(res):
                outs[5 + 4 * n + k][...] = val

    full = lambda a: pl.BlockSpec(a.shape, lambda i, c: (0,) * len(a.shape))
    shapes = [jax.ShapeDtypeStruct(p.shape, F32) for p in params]
    outs = [jax.ShapeDtypeStruct((1, 1), F32)] + shapes * 4
    flat = pl.pallas_call(
        body, name="adamw_small",
        grid_spec=pltpu.PrefetchScalarGridSpec(
            num_scalar_prefetch=1, grid=(1,),
            in_specs=[full(small), pl.BlockSpec((SMALL_ROWS, 128), lambda i, c: (0, c[0]))]
            + [full(a) for a in (*params, *m, *v)],
            out_specs=tuple(full(s) for s in outs)),
        out_shape=tuple(outs), compiler_params=_params(("arbitrary",)),
    )(chip, small, small, *params, *m, *v)
    return flat[0], flat[1:5], [flat[5 + 4 * n:9 + 4 * n] for n in range(3)]


def kernel(x, norm_g, w_in, sinks, conv_w, w_out, final_g, loss_target, m_norm_g, m_w_in, m_sinks, m_conv_w, m_w_out, m_final_g, v_norm_g, v_w_in, v_sinks, v_conv_w, v_w_out, v_final_g):
    seq = x.shape[1]
    x2 = x.reshape(seq, D_MODEL)
    tgt = loss_target.reshape(seq, D_MODEL)
    ng = norm_g.reshape(1, D_MODEL)
    fg = final_g.reshape(1, D_MODEL)
    chip = 2 * lax.axis_index("x") + lax.axis_index("y")

    conv_w8 = jnp.zeros((8, 128), F32).at[0:3].set(conv_w)
    w_in_full = _gather_w_in(w_in.T).reshape(IN_W, D_MODEL)

    q, kd, vd, rest, rope_c, rope_a, rope_b, wo_all, cw_all = _fwd_proj(x2, ng, w_in_full, w_out, conv_w8)
    w_out_full = wo_all.reshape(D_MODEL, D_MODEL)
    attn, dh2, g_wo, g_wo_b, small_f = _fwd_mix(x2, q, kd, vd, rest, sinks, cw_all, w_out_full, fg, tgt)
    out_blocks = lambda t: t.reshape(N_CHIPS, W_OUT_BLK, D_MODEL)
    dq, dk, dv, dkh, dvh, dga, db, dgc, dcv, small_m, grad_w_out = _bwd_mix(
        dh2, q, kd, vd, attn, rest, sinks, cw_all, w_out_full, rope_c, rope_a, rope_b,
        out_blocks(g_wo), out_blocks(g_wo_b))
    grad_x, g_wi, small_p = _bwd_proj(x2, ng, dh2, dq, dk, dv, dkh, dvh, dga, db, dgc, dcv, rest, cw_all,
                                      w_in_full, rope_c, rope_a, rope_b)

    g_in_blocks = g_wi.reshape(N_CHIPS, W_IN_BLK, D_MODEL)
    grad_w_in_t, small = _reduce_grads(g_in_blocks, small_f, small_m, small_p)

    (upd_wi, upd_wo) = _adamw_weights([(w_in.T, grad_w_in_t, m_w_in.T, v_w_in.T),
                                       (w_out, grad_w_out, m_w_out, v_w_out)])
    row = lambda t: t.reshape(1, -1)
    loss, grads_s, upd_s = _adamw_small(
        chip.reshape(1), small, (ng, fg, conv_w, row(sinks)),
        (row(m_norm_g), row(m_final_g), m_conv_w, row(m_sinks)),
        (row(v_norm_g), row(v_final_g), v_conv_w, row(v_sinks)))

    def named(ng_, fg_, cw_, sk_, wi_t, wo_):
        return [ng_.reshape(D_MODEL), wi_t.T, sk_.reshape(8), cw_, wo_, fg_.reshape(D_MODEL)]

    g_named = named(*grads_s, grad_w_in_t, grad_w_out)
    out = [loss.reshape(()), grad_x.reshape(1, seq, D_MODEL)] + g_named
    for n in range(3):
        out += named(*upd_s[n], upd_wi[n], upd_wo[n])
    return tuple(out)
```

```python
import jax
import jax.numpy as jnp
from jax import lax
from jax.experimental import pallas as pl
from jax.experimental.pallas import tpu as pltpu

F32 = jnp.float32
BF16 = jnp.bfloat16

D_MODEL = 1024
HEAD_DIM = 64
ATTN_W = 512
KV_W = 128
CONV_W = 512
IN_W = 3328
REST_W = IN_W - ATTN_W - 2 * KV_W
BLOCK = 128
ROT_DIM = 16
ROPE_THETA = 500000.0
EPS = 1e-5
SCALE = 0.125
NEG = -1e30

N_CHIPS = 4
W_IN_BLK = IN_W // N_CHIPS
W_OUT_BLK = D_MODEL // N_CHIPS

ADAM_LR = 0.001
ADAM_B1 = 0.9
ADAM_B2 = 0.999
ADAM_EPS = 1e-08
ADAM_WD = 0.01
ADAM_STEP = 10

VMEM_LIMIT = 56 * 1024 * 1024
T_PROJ = 512
T_FMIX = 512
T_MIX = 512
SMALL_ROWS = 8
MESH = pl.DeviceIdType.MESH

_NT = (((1,), (1,)), ((), ()))
_TN = (((0,), (0,)), ((), ()))


def _params(sem=None):
    kw = dict(vmem_limit_bytes=VMEM_LIMIT)
    if sem is not None:
        kw["dimension_semantics"] = sem
    return pltpu.CompilerParams(**kw)


def _sigmoid(t):
    return 1.0 / (1.0 + jnp.exp(-t))


def _shift_down(t, prev8, k):
    rolled = pltpu.roll(t, k, 0)
    row = lax.broadcasted_iota(jnp.int32, t.shape, 0)
    for j in range(k):
        rolled = jnp.where(row == j, prev8[8 - k + j:8 - k + j + 1, :], rolled)
    return rolled


def _shift_up(t, next8, k):
    n = t.shape[0]
    rolled = pltpu.roll(t, n - k, 0)
    row = lax.broadcasted_iota(jnp.int32, t.shape, 0)
    for j in range(k):
        rolled = jnp.where(row == n - k + j, next8[j:j + 1, :], rolled)
    return rolled


def _rope(t, c, a, b):
    w = t.shape[1]
    reps = w // 128
    if reps > 1:
        c, a, b = (jnp.concatenate([z] * reps, axis=1) for z in (c, a, b))
    return t * c + pltpu.roll(t, w - 8, 1) * a + pltpu.roll(t, 8, 1) * b


def _lane_lo(shape):
    return lax.broadcasted_iota(jnp.int32, shape, 1) < HEAD_DIM


def _stack_heads(t, g):
    lo = _lane_lo((BLOCK, 128))
    parts = []
    for hh in range(4):
        pair = t[:, 256 * g + 128 * (hh // 2):256 * g + 128 * (hh // 2) + 128]
        keep = lo if hh % 2 == 0 else jnp.logical_not(lo)
        parts.append(jnp.where(keep, pair, jnp.zeros_like(pair)))
    return jnp.concatenate(parts, axis=0)


def _unstack_pair(o, pp):
    lo = _lane_lo((BLOCK, 128))
    return jnp.where(lo, o[256 * pp:256 * pp + 128], o[256 * pp + 128:256 * pp + 256])


def _band_masks(has_prev):
    r = lax.broadcasted_iota(jnp.int32, (4 * BLOCK, 2 * BLOCK), 0) % BLOCK
    kj = lax.broadcasted_iota(jnp.int32, (4 * BLOCK, 2 * BLOCK), 1)
    cur = (kj >= BLOCK) & (kj - BLOCK <= r)
    prev = (kj < BLOCK) & (kj > r)
    return cur | (prev & has_prev), cur | prev


def _sink_col(sinks_ref, g):
    r = lax.broadcasted_iota(jnp.int32, (4 * BLOCK, 1), 0) // BLOCK
    col = jnp.full((4 * BLOCK, 1), sinks_ref[4 * g + 3], F32)
    for hh in range(3):
        col = jnp.where(r == hh, sinks_ref[4 * g + hh], col)
    return col


def _probs(qs, kd, sink_col, mask):
    return _softmax(lax.dot_general(qs, kd, _NT, preferred_element_type=F32), sink_col, mask)


def _softmax(s, sink_col, mask):
    s = jnp.where(mask, s, NEG)
    m = jnp.maximum(jnp.max(s, axis=-1, keepdims=True), sink_col)
    p = jnp.exp(s - m)
    es = jnp.exp(sink_col - m)
    inv = 1.0 / (jnp.sum(p, axis=-1, keepdims=True) + es)
    return p * inv, es * inv


def _key_windows(i, nsub, kd_ref, vd_ref, kdp_ref, vdp_ref):
    mask_first, mask_rest = _band_masks(i > 0)
    out = []
    for sb in range(nsub):
        rows = slice(BLOCK * sb, BLOCK * (sb + 1))
        if sb == 0:
            kk = jnp.concatenate([kdp_ref[...], kd_ref[rows, :]], axis=0)
            vv = jnp.concatenate([vdp_ref[...], vd_ref[rows, :]], axis=0)
            out.append((rows, kk, vv, mask_first))
        else:
            both = slice(BLOCK * (sb - 1), BLOCK * (sb + 1))
            out.append((rows, kd_ref[both, :], vd_ref[both, :], mask_rest))
    return out


def _gather_w_in(w_in_t):
    hi = W_IN_BLK // 2
    qr = hi // 2

    def body(wi_ref, wi_all, send_sems, recv_sems):
        x, y, c = lax.axis_index("x"), lax.axis_index("y"), lax.axis_index("c")
        me, sibling = (x, y), (x, y, 1 - c)
        xnb, ynb, diag = (1 - x, y), (x, 1 - y), (1 - x, 1 - y)

        wi_all[2 * x + y] = wi_ref[...].astype(BF16)

        def copy(k, chip, half, quarter, to):
            r = wi_all.at[2 * chip[0] + chip[1], pl.ds(half * hi + quarter * qr, qr)]
            return pltpu.make_async_remote_copy(src_ref=r, dst_ref=r, send_sem=send_sems.at[k],
                                                recv_sem=recv_sems.at[k], device_id=to, device_id_type=MESH)

        plan = [(0, xnb, 0), (1, ynb, 1), (2, xnb, 1), (3, ynb, 0)]
        sent = [copy(k, me, c, quarter, (*nb, c)) for k, nb, quarter in plan]
        for cp in sent:
            cp.start()
        arrivals = [(0, xnb, 0), (1, ynb, 1), (2, xnb, 1), (3, ynb, 0), (4, diag, 0), (5, diag, 1)]
        relay = {0: (4, ynb), 1: (5, xnb)}
        for k, chip, quarter in arrivals:
            copy(k, chip, c, quarter, (x, y, c)).wait_recv()
            if k in relay:
                sent.append(copy(relay[k][0], chip, c, quarter, (*relay[k][1], c)))
                sent[-1].start()
            sent.append(copy(6 + k, chip, c, quarter, sibling))
            sent[-1].start()
        for k, chip, quarter in arrivals:
            copy(6 + k, chip, 1 - c, quarter, (x, y, c)).wait_recv()
        for cp in sent:
            cp.wait_send()

    vmem = pl.BlockSpec(memory_space=pltpu.VMEM)
    return pl.pallas_call(
        body, name="gather_w_in",
        out_shape=jax.ShapeDtypeStruct((N_CHIPS, W_IN_BLK, D_MODEL), BF16),
        in_specs=[vmem], out_specs=vmem,
        scratch_shapes=[pltpu.SemaphoreType.DMA((12,)), pltpu.SemaphoreType.DMA((12,))],
        compiler_params=_params(),
    )(w_in_t)


def _reduce_grads(g_in, *smalls):
    hi = W_IN_BLK // 2
    qr = hi // 2
    q0, q1 = slice(0, qr), slice(qr, hi)

    def body(gi_hbm, s0_ref, s1_ref, s2_ref, gi_out, small_out,
             mine_i, sib_i, out_i, ici_i, small_in, small_ref, send_sems, recv_sems, local_sems):
        x, y, c = lax.axis_index("x"), lax.axis_index("y"), lax.axis_index("c")
        my_dev = 4 * x + 2 * y + c
        sibling = (x, y, 1 - c)
        xnb, ynb = (1 - x, y, c), (x, 1 - y, c)
        order = [(1 - x, 1 - y), (1 - x, y), (x, 1 - y), (x, y)]

        def remote(k, src, dst, to):
            return pltpu.make_async_remote_copy(src_ref=src, dst_ref=dst, send_sem=send_sems.at[k],
                                                recv_sem=recv_sems.at[k], device_id=to, device_id_type=MESH)

        small_ref[...] = s0_ref[...] + s1_ref[...] + s2_ref[...]
        small_cps = []
        for f in range(1, 8):
            fx, fy, fc = f >> 2, (f >> 1) & 1, f & 1
            small_cps.append(remote(11 + f - 1, small_ref, small_in.at[f - 1], (x ^ fx, y ^ fy, c ^ fc)))
        for cp in small_cps:
            cp.start()

        own, to_sib = [], []
        for n, chip in enumerate(order):
            j = 2 * chip[0] + chip[1]
            own.append(pltpu.make_async_copy(gi_hbm.at[j, pl.ds(c * hi, hi)], mine_i.at[n], local_sems.at[n]))
            to_sib.append(remote(6 + n, gi_hbm.at[j, pl.ds((1 - c) * hi, hi)], sib_i.at[n], sibling))
            own[-1].start()
            to_sib[-1].start()

        def pair_sum(n):
            own[n].wait()
            to_sib[n].wait_recv()
            return mine_i[n] + sib_i[n]

        ici = [remote(k, out_i.at[k], ici_i.at[k], xnb if k % 2 == 0 else ynb) for k in range(6)]

        def send(k, rows_f32):
            out_i[k] = rows_f32.astype(BF16)
            ici[k].start()

        p_diag = pair_sum(0)
        send(0, p_diag[q0])
        send(1, p_diag[q1])
        p_x = pair_sum(1)
        send(2, p_x[q0])
        p_y = pair_sum(2)
        send(3, p_y[q1])
        ici[0].wait_recv()
        send(5, p_y[q0] + ici_i[0].astype(F32))
        ici[1].wait_recv()
        send(4, p_x[q1] + ici_i[1].astype(F32))
        p_mine = pair_sum(3)
        for k in range(2, 6):
            ici[k].wait_recv()
        gi_out[pl.ds(c * hi, qr), :] = p_mine[q0] + ici_i[2].astype(F32) + ici_i[5].astype(F32)
        gi_out[pl.ds(c * hi + qr, qr), :] = p_mine[q1] + ici_i[4].astype(F32) + ici_i[3].astype(F32)

        swap = [remote(10, gi_out.at[pl.ds(c * hi, hi)], gi_out.at[pl.ds(c * hi, hi)], sibling)]
        for cp in swap:
            cp.start()

        for cp in small_cps:
            cp.wait_recv()
        total = jnp.zeros((SMALL_ROWS, D_MODEL), F32)
        for d in range(8):
            slot = jnp.maximum((d ^ my_dev) - 1, 0)
            total = total + jnp.where(d == my_dev, small_ref[...], small_in[slot])
        small_out[...] = total

        remote(10, gi_out.at[pl.ds((1 - c) * hi, hi)], gi_out.at[pl.ds((1 - c) * hi, hi)], sibling).wait_recv()
        for cp in to_sib + ici + swap + small_cps:
            cp.wait_send()

    vmem = pl.BlockSpec(memory_space=pltpu.VMEM)
    anyspace = pl.BlockSpec(memory_space=pl.ANY)
    return pl.pallas_call(
        body, name="reduce_grads",
        out_shape=(jax.ShapeDtypeStruct((W_IN_BLK, D_MODEL), F32), jax.ShapeDtypeStruct((SMALL_ROWS, D_MODEL), F32)),
        in_specs=[anyspace, vmem, vmem, vmem], out_specs=(vmem, vmem),
        scratch_shapes=[pltpu.VMEM((N_CHIPS, hi, D_MODEL), F32), pltpu.VMEM((N_CHIPS, hi, D_MODEL), F32),
                        pltpu.VMEM((6, qr, D_MODEL), BF16), pltpu.VMEM((6, qr, D_MODEL), BF16),
                        pltpu.VMEM((7, SMALL_ROWS, D_MODEL), F32), pltpu.VMEM((SMALL_ROWS, D_MODEL), F32),
                        pltpu.SemaphoreType.DMA((18,)), pltpu.SemaphoreType.DMA((18,)),
                        pltpu.SemaphoreType.DMA((4,))],
        compiler_params=_params(),
    )(g_in, *smalls)


def _fwd_proj(x, norm_g, w_in_t, w_out, conv_w8):
    seq = x.shape[0]
    nt = seq // T_PROJ
    lane = jnp.arange(128, dtype=jnp.int32) % HEAD_DIM
    inv_freq = ROPE_THETA ** (-(2 * (lane % 8)).astype(F32) / ROT_DIM)
    inv_freq = jnp.where(lane < ROT_DIM, inv_freq, 0.0).reshape(1, 128)
    in_tile = jnp.arange(T_PROJ, dtype=jnp.int32).astype(F32)[:, None] * inv_freq
    cos_in, sin_in = jnp.cos(in_tile), jnp.sin(in_tile)
    start = jnp.repeat((jnp.arange(nt, dtype=jnp.int32) * T_PROJ).astype(F32), 8)[:, None] * inv_freq
    cos_st, sin_st = jnp.cos(start), jnp.sin(start)

    def body(x_ref, g_ref, w_ref, cs_ref, ss_ref, ci_ref, si_ref, wo_ref, cw_ref,
             q_ref, kd_ref, vd_ref, rest_ref, c_ref, a_ref, b_ref, wo_all, cw_all,
             wo_stage, send_sems, recv_sems, local_sems):
        i = pl.program_id(0)
        mx, my, mc = lax.axis_index("x"), lax.axis_index("y"), lax.axis_index("c")
        chips = [(1 - mx, my), (mx, 1 - my), (1 - mx, 1 - my)]

        def gather(blocks):
            cps = []
            for k, chip in enumerate(chips):
                for n, (src, dst) in enumerate(((wo_stage, wo_all), (cw_ref, cw_all))):
                    cps.append(pltpu.make_async_remote_copy(
                        src_ref=src, dst_ref=dst.at[blocks[k]], send_sem=send_sems.at[2 * k + n],
                        recv_sem=recv_sems.at[2 * k + n], device_id=(*chip, mc), device_id_type=MESH))
            return cps

        me = 2 * mx + my
        own = [pltpu.make_async_copy(wo_stage, wo_all.at[me], local_sems.at[0]),
               pltpu.make_async_copy(cw_ref, cw_all.at[me], local_sems.at[1])]

        @pl.when(i == 0)
        def _():
            wo_stage[...] = wo_ref[...].astype(BF16)
            for cp in own + gather([me] * 3):
                cp.start()

        xf = x_ref[...]
        r1 = lax.rsqrt(jnp.mean(xf * xf, axis=-1, keepdims=True) + EPS)
        xn = (xf * r1 * g_ref[...]).astype(BF16)
        cs, ss = cs_ref[0:1, :], ss_ref[0:1, :]
        c = cs * ci_ref[...] - ss * si_ref[...]
        sin = ss * ci_ref[...] + cs * si_ref[...]
        j = lax.broadcasted_iota(jnp.int32, (T_PROJ, 128), 1) % HEAD_DIM
        a = jnp.where(j < 8, -sin, 0.0)
        b = jnp.where(j >= 8, sin, 0.0)
        c_ref[...], a_ref[...], b_ref[...] = c, a, b
        proj = lambda lo_c, w: lax.dot_general(xn, w_ref[lo_c:lo_c + w, :], _NT, preferred_element_type=F32)
        q_ref[...] = (_rope(proj(0, ATTN_W), c, a, b) * SCALE).astype(BF16)
        kv = proj(ATTN_W, 2 * KV_W)
        k = _rope(kv[:, 0:KV_W], c, a, b)
        v = kv[:, KV_W:2 * KV_W]
        lo = _lane_lo(k.shape)
        for t, ref in ((k, kd_ref), (v, vd_ref)):
            sw = pltpu.roll(t, HEAD_DIM, 1)
            ref[:, 0:128] = jnp.where(lo, t, sw).astype(BF16)
            ref[:, 128:256] = jnp.where(lo, sw, t).astype(BF16)
        for n in range(REST_W // 512):
            rest_ref[:, 512 * n:512 * (n + 1)] = proj(ATTN_W + 2 * KV_W + 512 * n, 512)

        @pl.when(i == nt - 1)
        def _():
            sent = gather([me] * 3)
            for cp in gather([2 * chip[0] + chip[1] for chip in chips]):
                cp.wait_recv()
            for cp in sent:
                cp.wait_send()
            for cp in own:
                cp.wait()

    tile = lambda w: pl.BlockSpec((T_PROJ, w), lambda i: (i, 0))
    whole = lambda r, w: pl.BlockSpec((r, w), lambda i: (0, 0))
    vmem = pl.BlockSpec(memory_space=pltpu.VMEM)
    hbm = pl.BlockSpec(memory_space=pl.ANY)
    return pl.pallas_call(
        body, name="fwd_proj", grid=(nt,),
        out_shape=(jax.ShapeDtypeStruct((seq, ATTN_W), BF16), jax.ShapeDtypeStruct((seq, 2 * KV_W), BF16),
                   jax.ShapeDtypeStruct((seq, 2 * KV_W), BF16), jax.ShapeDtypeStruct((seq, REST_W), F32))
        + (jax.ShapeDtypeStruct((seq, 128), F32),) * 3
        + (jax.ShapeDtypeStruct((N_CHIPS, W_OUT_BLK, D_MODEL), BF16), jax.ShapeDtypeStruct((N_CHIPS, 8, 128), F32)),
        in_specs=[tile(D_MODEL), whole(1, D_MODEL), whole(IN_W, D_MODEL), pl.BlockSpec((8, 128), lambda i: (i, 0)),
                  pl.BlockSpec((8, 128), lambda i: (i, 0)), whole(T_PROJ, 128), whole(T_PROJ, 128), vmem, vmem],
        out_specs=(tile(ATTN_W), tile(2 * KV_W), tile(2 * KV_W), tile(REST_W), tile(128), tile(128), tile(128),
                   hbm, hbm),
        scratch_shapes=[pltpu.VMEM((W_OUT_BLK, D_MODEL), BF16), pltpu.SemaphoreType.DMA((6,)),
                        pltpu.SemaphoreType.DMA((6,)), pltpu.SemaphoreType.DMA((2,))],
        compiler_params=_params(("arbitrary",)),
    )(x, norm_g, w_in_t, cos_st, sin_st, cos_in, sin_in, w_out, conv_w8)


CONV_SPEC = pl.BlockSpec((N_CHIPS, 8, 128), lambda i: (0, 0, 0))


def _conv_rows(cw_ref):
    return jnp.concatenate([cw_ref[j] for j in range(N_CHIPS)], axis=1)


def _conv_parts(rest_ref, prev_ref, cw_ref, first):
    u = rest_ref[:, 1024:1536] * rest_ref[:, 1536:2048]
    up = prev_ref[:, 1024:1536] * prev_ref[:, 1536:2048]
    up = jnp.where(first, jnp.zeros_like(up), up)
    um1 = _shift_down(u, up, 1)
    um2 = _shift_down(u, up, 2)
    cw = _conv_rows(cw_ref)
    cv = cw[0:1, :] * um2 + cw[1:2, :] * um1 + cw[2:3, :] * u
    return u, um1, um2, cv


def _fwd_mix(x, q, kd, vd, rest, sinks, conv_w, w_out, final_g, target):
    seq = x.shape[0]
    nt = seq // T_FMIX
    nsub = T_FMIX // BLOCK

    def body(sinks_ref, x_ref, q_ref, kd_ref, vd_ref, kdp_ref, vdp_ref, rest_ref, restp_ref, cw_ref, wo_ref,
             fg_ref, tgt_ref, attn_ref, gate_ref, dh2_ref, gwo_ref, gwob_ref, small_ref, mix_ref, pmix_ref, pdh2_ref):
        i = pl.program_id(0)

        @pl.when(i == 0)
        def _():
            small_ref[...] = jnp.zeros_like(small_ref)
            gwo_ref[...] = jnp.zeros_like(gwo_ref)
            pmix_ref[...] = jnp.zeros_like(pmix_ref)
            pdh2_ref[...] = jnp.zeros_like(pdh2_ref)

        chains = []
        for rows, kk, vv, mask in _key_windows(i, nsub, kd_ref, vd_ref, kdp_ref, vdp_ref):
            qt = q_ref[rows, :]
            for g in range(2):
                kg = kk[:, 128 * g:128 * (g + 1)]
                chains.append(dict(g=g, rows=rows, mask=mask, vg=vv[:, 128 * g:128 * (g + 1)],
                                   s=lax.dot_general(_stack_heads(qt, g), kg, _NT, preferred_element_type=F32)))
        gwo_ref[...] += lax.dot_general(pmix_ref[...], pdh2_ref[...], _TN, preferred_element_type=F32)
        for ch in chains:
            ch["prob"], _ = _softmax(ch.pop("s"), _sink_col(sinks_ref, ch["g"]), ch["mask"])
        for ch in chains:
            o = jnp.dot(ch["prob"].astype(BF16), ch["vg"], preferred_element_type=F32)
            for pp in range(2):
                lanes = slice(256 * ch["g"] + 128 * pp, 256 * ch["g"] + 128 * (pp + 1))
                attn_ref[ch["rows"], lanes] = _unstack_pair(o, pp)

        def silu_parts(t, lo_c):
            sg = _sigmoid(t)
            silu = t * sg
            gate_ref[:, lo_c:lo_c + 512] = silu
            gate_ref[:, lo_c + 512:lo_c + 1024] = sg * (1.0 + t * (1.0 - sg))
            return silu

        mix_ref[:, 0:ATTN_W] = (attn_ref[...] * silu_parts(rest_ref[:, 0:512], 0)).astype(BF16)
        _, _, _, cv = _conv_parts(rest_ref, restp_ref, cw_ref, i == 0)
        mix_ref[:, ATTN_W:] = (rest_ref[:, 512:1024] * cv * silu_parts(rest_ref[:, 2048:2560], 1024)).astype(BF16)

        h2 = x_ref[...] + jnp.dot(mix_ref[...], wo_ref[...], preferred_element_type=F32)
        r2 = lax.rsqrt(jnp.mean(h2 * h2, axis=-1, keepdims=True) + EPS)
        n2 = h2 * r2
        err = n2 * fg_ref[...] - tgt_ref[...]
        dy = err * (1.0 / D_MODEL)
        small_ref[6:7, :] += jnp.sum(err * err, axis=0, keepdims=True) * (0.5 / D_MODEL)
        small_ref[1:2, :] += jnp.sum(dy * n2, axis=0, keepdims=True)
        dn = dy * fg_ref[...]
        dh2 = r2 * (dn - n2 * jnp.mean(dn * n2, axis=-1, keepdims=True))
        dh2_ref[...] = dh2
        pmix_ref[...] = mix_ref[...]
        pdh2_ref[...] = dh2.astype(BF16)

        @pl.when(i == nt - 1)
        def _():
            gwo_ref[...] += lax.dot_general(pmix_ref[...], pdh2_ref[...], _TN, preferred_element_type=F32)
            gwob_ref[...] = gwo_ref[...].astype(BF16)

    tile = lambda w: pl.BlockSpec((T_FMIX, w), lambda i: (i, 0))
    whole = lambda r, w: pl.BlockSpec((r, w), lambda i: (0, 0))
    prev_blk = pl.BlockSpec((BLOCK, 2 * KV_W), lambda i: (jnp.maximum(i * nsub - 1, 0), 0))
    prev8 = pl.BlockSpec((8, REST_W), lambda i: (jnp.maximum(i * (T_FMIX // 8) - 1, 0), 0))
    return pl.pallas_call(
        body, name="fwd_mix", grid=(nt,),
        out_shape=(jax.ShapeDtypeStruct((seq, ATTN_W), F32), jax.ShapeDtypeStruct((seq, 4 * 512), F32),
                   jax.ShapeDtypeStruct((seq, D_MODEL), F32),
                   jax.ShapeDtypeStruct((D_MODEL, D_MODEL), F32), jax.ShapeDtypeStruct((D_MODEL, D_MODEL), BF16),
                   jax.ShapeDtypeStruct((SMALL_ROWS, D_MODEL), F32)),
        in_specs=[pl.BlockSpec(memory_space=pltpu.SMEM), tile(D_MODEL), tile(ATTN_W), tile(2 * KV_W), tile(2 * KV_W),
                  prev_blk, prev_blk, tile(REST_W), prev8, CONV_SPEC, whole(D_MODEL, D_MODEL),
                  whole(1, D_MODEL), tile(D_MODEL)],
        out_specs=(tile(ATTN_W), tile(4 * 512), tile(D_MODEL), whole(D_MODEL, D_MODEL), whole(D_MODEL, D_MODEL),
                   whole(SMALL_ROWS, D_MODEL)),
        scratch_shapes=[pltpu.VMEM((T_FMIX, D_MODEL), BF16)] * 3,
        compiler_params=_params(("arbitrary",)),
    )(sinks, x, q, kd, vd, kd, vd, rest, rest, conv_w, w_out, final_g, target)


def _scatter_copies(g_hbm, gb_hbm, mine, land, send_sems, recv_sems, local_sem, half):
    x, y, c = lax.axis_index("x"), lax.axis_index("y"), lax.axis_index("c")
    cps = []
    for f in range(1, 8):
        to = (x ^ (f >> 2), y ^ ((f >> 1) & 1), c ^ (f & 1))
        src = gb_hbm.at[2 * to[0] + to[1], pl.ds(to[2] * half, half)]
        cps.append(pltpu.make_async_remote_copy(src_ref=src, dst_ref=land.at[f - 1], send_sem=send_sems.at[f - 1],
                                                recv_sem=recv_sems.at[f - 1], device_id=to, device_id_type=MESH))
    own = pltpu.make_async_copy(g_hbm.at[2 * x + y, pl.ds(c * half, half)], mine, local_sem)
    return cps, own


def _scatter_finish(cps, own, mine, land, out_hbm, send_sems, recv_sems, local_sem, half):
    x, y, c = lax.axis_index("x"), lax.axis_index("y"), lax.axis_index("c")
    own.wait()
    tot = mine[...]
    for f in range(1, 8):
        cps[f - 1].wait_recv()
        tot = tot + land[f - 1].astype(F32)
    mine[...] = tot

    def swap(rows_of):
        return pltpu.make_async_remote_copy(src_ref=mine, dst_ref=out_hbm.at[pl.ds(rows_of * half, half)],
                                            send_sem=send_sems.at[7], recv_sem=recv_sems.at[7],
                                            device_id=(x, y, 1 - c), device_id_type=MESH)

    keep = pltpu.make_async_copy(mine, out_hbm.at[pl.ds(c * half, half)], local_sem)
    keep.start()
    swap(c).start()
    swap(1 - c).wait_recv()
    keep.wait()
    for cp in cps:
        cp.wait_send()
    swap(c).wait_send()


def _bwd_mix(dh2, q, kd, vd, attn, gates, rest, sinks, conv_w, w_out, rope_c, rope_a, rope_b, g_out, g_out_b):
    seq = dh2.shape[0]
    nt = seq // T_MIX
    nsub = T_MIX // BLOCK
    ho = W_OUT_BLK // 2

    def body(sinks_ref, dh2_ref, q_ref, kd_ref, vd_ref, kdp_ref, vdp_ref, attn_ref, gate_ref, rest_ref, restp_ref,
             cw_ref, wo_ref, c_ref, a_ref, b_ref, go_hbm, gob_hbm,
             dq_ref, dk_ref, dv_ref, dkh_ref, dvh_ref, dga_ref, db_ref, dgc_ref, dcv_ref, small_ref, go_out,
             dmix_ref, dsink_ref, mine_o, land_o, send_sems, recv_sems, local_sems):
        i = pl.program_id(0)
        scatter = (mine_o, land_o, send_sems, recv_sems, local_sems.at[0], ho)

        @pl.when(i == 0)
        def _():
            small_ref[...] = jnp.zeros_like(small_ref)
            dsink_ref[...] = jnp.zeros_like(dsink_ref)
            cps, own = _scatter_copies(go_hbm, gob_hbm, *scatter)
            for cp in cps + [own]:
                cp.start()

        dmix_ref[...] = lax.dot_general(dh2_ref[...].astype(BF16), wo_ref[...], _NT, preferred_element_type=F32)

        dma = dmix_ref[:, 0:ATTN_W]
        dga_ref[...] = (dma * attn_ref[...] * gate_ref[:, 512:1024]).astype(BF16)
        dmix_ref[:, 0:ATTN_W] = dma * gate_ref[:, 0:512]

        u, um1, um2, cv = _conv_parts(rest_ref, restp_ref, cw_ref, i == 0)
        bg = rest_ref[:, 512:1024]
        dmc = dmix_ref[:, ATTN_W:]
        t1 = dmc * gate_ref[:, 1024:1536]
        db_ref[...] = (t1 * cv).astype(BF16)
        dcv = t1 * bg
        dcv_ref[...] = dcv
        dgc_ref[...] = (dmc * (bg * cv) * gate_ref[:, 1536:2048]).astype(BF16)
        small_ref[2:3, 0:CONV_W] += jnp.sum(dcv * um2, axis=0, keepdims=True)
        small_ref[3:4, 0:CONV_W] += jnp.sum(dcv * um1, axis=0, keepdims=True)
        small_ref[4:5, 0:CONV_W] += jnp.sum(dcv * u, axis=0, keepdims=True)

        lo = _lane_lo((2 * BLOCK, 128))
        dk_blocks = [None] * (nsub + 1)
        dv_blocks = [None] * (nsub + 1)

        def add(lst, n, val):
            lst[n] = val if lst[n] is None else lst[n] + val

        chains = []
        for rows, kk, vv, mask in _key_windows(i, nsub, kd_ref, vd_ref, kdp_ref, vdp_ref):
            qt = q_ref[rows, :]
            dot = dmix_ref[rows, 0:ATTN_W].astype(BF16)
            for g in range(2):
                chains.append(dict(g=g, rows=rows, mask=mask, qs=_stack_heads(qt, g), dos=_stack_heads(dot, g),
                                   kg=kk[:, 128 * g:128 * (g + 1)], vg=vv[:, 128 * g:128 * (g + 1)]))
        for ch in chains:
            ch["prob"], ch["psink"] = _probs(ch["qs"], ch["kg"], _sink_col(sinks_ref, ch["g"]), ch["mask"])
        for ch in chains:
            ch["dp"] = lax.dot_general(ch["dos"], ch["vg"], _NT, preferred_element_type=F32)
        for ch in chains:
            rs = jnp.sum(ch["prob"] * ch["dp"], axis=-1, keepdims=True)
            ch["ds"] = (ch["prob"] * (ch["dp"] - rs)).astype(BF16)
            dsink_ref[ch["g"]] += -ch["psink"] * rs
        for ch in chains:
            dqs = jnp.dot(ch["ds"], ch["kg"], preferred_element_type=F32) * SCALE
            c, a, b = c_ref[ch["rows"], :], a_ref[ch["rows"], :], b_ref[ch["rows"], :]
            for pp in range(2):
                lanes = slice(256 * ch["g"] + 128 * pp, 256 * ch["g"] + 128 * (pp + 1))
                dq_ref[ch["rows"], lanes] = _rope(_unstack_pair(dqs, pp), c, -a, -b).astype(BF16)
            dkd = lax.dot_general(ch["ds"], ch["qs"], _TN, preferred_element_type=F32)
            dvd = lax.dot_general(ch["prob"].astype(BF16), ch["dos"], _TN, preferred_element_type=F32)
            ch["dk"] = dkd + pltpu.roll(dkd, HEAD_DIM, 1)
            ch["dv"] = dvd + pltpu.roll(dvd, HEAD_DIM, 1)
        for sb in range(nsub):
            dk2 = jnp.where(lo, chains[2 * sb]["dk"], chains[2 * sb + 1]["dk"])
            dv2 = jnp.where(lo, chains[2 * sb]["dv"], chains[2 * sb + 1]["dv"])
            add(dk_blocks, sb, dk2[0:BLOCK])
            add(dk_blocks, sb + 1, dk2[BLOCK:])
            add(dv_blocks, sb, dv2[0:BLOCK])
            add(dv_blocks, sb + 1, dv2[BLOCK:])
        dkh_ref[0] = dk_blocks[0]
        dvh_ref[0] = dv_blocks[0]
        for sb in range(nsub):
            dk_ref[BLOCK * sb:BLOCK * (sb + 1), :] = dk_blocks[sb + 1]
            dv_ref[BLOCK * sb:BLOCK * (sb + 1), :] = dv_blocks[sb + 1]

        @pl.when(i == nt - 1)
        def _():
            for h in range(8):
                tot = jnp.sum(dsink_ref[h // 4, BLOCK * (h % 4):BLOCK * (h % 4 + 1), :], axis=0, keepdims=True)
                small_ref[5:6, h:h + 1] = tot
            cps, own = _scatter_copies(go_hbm, gob_hbm, *scatter)
            _scatter_finish(cps, own, mine_o, land_o, go_out, send_sems, recv_sems, local_sems.at[1], ho)

    tile = lambda w: pl.BlockSpec((T_MIX, w), lambda i: (i, 0))
    whole = lambda r, w: pl.BlockSpec((r, w), lambda i: (0, 0))
    prev_blk = pl.BlockSpec((BLOCK, 2 * KV_W), lambda i: (jnp.maximum(i * nsub - 1, 0), 0))
    prev8 = pl.BlockSpec((8, REST_W), lambda i: (jnp.maximum(i * (T_MIX // 8) - 1, 0), 0))
    halo = pl.BlockSpec((1, BLOCK, KV_W), lambda i: (i, 0, 0))
    hbm = pl.BlockSpec(memory_space=pl.ANY)
    bf = lambda w: jax.ShapeDtypeStruct((seq, w), BF16)
    f32 = lambda w: jax.ShapeDtypeStruct((seq, w), F32)
    return pl.pallas_call(
        body, name="bwd_mix", grid=(nt,),
        out_shape=(bf(ATTN_W), f32(KV_W), f32(KV_W), jax.ShapeDtypeStruct((nt, BLOCK, KV_W), F32),
                   jax.ShapeDtypeStruct((nt, BLOCK, KV_W), F32), bf(ATTN_W), bf(CONV_W), bf(CONV_W), f32(CONV_W),
                   jax.ShapeDtypeStruct((SMALL_ROWS, D_MODEL), F32), jax.ShapeDtypeStruct((W_OUT_BLK, D_MODEL), F32)),
        in_specs=[pl.BlockSpec(memory_space=pltpu.SMEM), tile(D_MODEL), tile(ATTN_W), tile(2 * KV_W), tile(2 * KV_W),
                  prev_blk, prev_blk, tile(ATTN_W), tile(4 * 512), tile(REST_W), prev8, CONV_SPEC,
                  whole(D_MODEL, D_MODEL), tile(128), tile(128), tile(128), hbm, hbm],
        out_specs=(tile(ATTN_W), tile(KV_W), tile(KV_W), halo, halo, tile(ATTN_W), tile(CONV_W), tile(CONV_W),
                   tile(CONV_W), whole(SMALL_ROWS, D_MODEL), hbm),
        scratch_shapes=[pltpu.VMEM((T_MIX, D_MODEL), F32), pltpu.VMEM((2, 4 * BLOCK, 1), F32),
                        pltpu.VMEM((ho, D_MODEL), F32), pltpu.VMEM((7, ho, D_MODEL), BF16),
                        pltpu.SemaphoreType.DMA((8,)), pltpu.SemaphoreType.DMA((8,)), pltpu.SemaphoreType.DMA((2,))],
        compiler_params=_params(("arbitrary",)),
    )(sinks, dh2, q, kd, vd, kd, vd, attn, gates, rest, rest, conv_w, w_out, rope_c, rope_a, rope_b, g_out, g_out_b)


def _bwd_proj(x, norm_g, dh2, dq, dk, dv, dkh, dvh, dga, db, dgc, dcv, rest, conv_w, w_in_t, rope_c, rope_a, rope_b):
    seq = x.shape[0]
    tb = T_PROJ
    per = tb // T_MIX
    nt = seq // tb

    def body(x_ref, g_ref, dh2_ref, dq_ref, dk_ref, dv_ref, dkh_ref, dvh_ref, dkn_ref, dvn_ref, dga_ref, db_ref,
             dgc_ref, dcv_ref, dcvn_ref, ch_ref, cw_ref, w_ref, c_ref, a_ref, b_ref, gx_ref, gw_hbm, small_ref,
             dp_ref, acc_ref):
        i = pl.program_id(0)

        @pl.when(i == 0)
        def _():
            small_ref[...] = jnp.zeros_like(small_ref)
            acc_ref[...] = jnp.zeros_like(acc_ref)

        last = i == nt - 1
        keep = jnp.where(last, 0.0, 1.0)
        pad = jnp.zeros((T_MIX - BLOCK, KV_W), F32)

        def with_halos(main_ref, halo_ref, next_ref):
            parts = []
            for m in range(1, per + 1):
                parts += [pad, halo_ref[m] if m < per else next_ref[0] * keep]
            return main_ref[...] + jnp.concatenate(parts, axis=0)

        dk = with_halos(dk_ref, dkh_ref, dkn_ref)
        dv = with_halos(dv_ref, dvh_ref, dvn_ref)
        dp_ref[:, 0:ATTN_W] = dq_ref[...]
        dp_ref[:, ATTN_W:ATTN_W + KV_W] = _rope(dk, c_ref[...], -a_ref[...], -b_ref[...]).astype(BF16)
        dp_ref[:, ATTN_W + KV_W:ATTN_W + 2 * KV_W] = dv.astype(BF16)
        base = ATTN_W + 2 * KV_W
        dp_ref[:, base:base + 512] = dga_ref[...]
        dp_ref[:, base + 512:base + 1024] = db_ref[...]
        dcv = dcv_ref[...]
        nxt = dcvn_ref[...] * keep
        cw = _conv_rows(cw_ref)
        du = cw[2:3, :] * dcv + cw[1:2, :] * _shift_up(dcv, nxt, 1) + cw[0:1, :] * _shift_up(dcv, nxt, 2)
        dp_ref[:, base + 1024:base + 1536] = (du * ch_ref[:, 512:1024]).astype(BF16)
        dp_ref[:, base + 1536:base + 2048] = (du * ch_ref[:, 0:512]).astype(BF16)
        dp_ref[:, base + 2048:base + 2560] = dgc_ref[...]

        xf = x_ref[...]
        r1 = lax.rsqrt(jnp.mean(xf * xf, axis=-1, keepdims=True) + EPS)
        n1 = xf * r1
        xn = (n1 * g_ref[...]).astype(BF16)
        for n in range(IN_W // 256):
            cols = slice(256 * n, 256 * (n + 1))
            acc_ref[cols, :] += lax.dot_general(dp_ref[:, cols], xn, _TN, preferred_element_type=F32)
        dxn = jnp.dot(dp_ref[...], w_ref[...], preferred_element_type=F32)
        small_ref[0:1, :] += jnp.sum(dxn * n1, axis=0, keepdims=True)
        dxg = dxn * g_ref[...]
        gx_ref[...] = r1 * (dxg - n1 * jnp.mean(dxg * n1, axis=-1, keepdims=True)) + dh2_ref[...]

        @pl.when(last)
        def _():
            pltpu.sync_copy(acc_ref, gw_hbm)

    tile = lambda w: pl.BlockSpec((tb, w), lambda i: (i, 0))
    whole = lambda r, w: pl.BlockSpec((r, w), lambda i: (0, 0))
    halo = pl.BlockSpec((per, BLOCK, KV_W), lambda i: (i, 0, 0))
    halo_next = pl.BlockSpec((1, BLOCK, KV_W), lambda i: (jnp.minimum((i + 1) * per, seq // T_MIX - 1), 0, 0))
    next8 = pl.BlockSpec((8, CONV_W), lambda i: (jnp.minimum((i + 1) * (tb // 8), seq // 8 - 1), 0))
    ch = pl.BlockSpec((tb, 1024), lambda i: (i, 1))
    return pl.pallas_call(
        body, name="bwd_proj", grid=(nt,),
        out_shape=(jax.ShapeDtypeStruct((seq, D_MODEL), F32), jax.ShapeDtypeStruct((IN_W, D_MODEL), F32),
                   jax.ShapeDtypeStruct((SMALL_ROWS, D_MODEL), F32)),
        in_specs=[tile(D_MODEL), whole(1, D_MODEL), tile(D_MODEL), tile(ATTN_W), tile(KV_W), tile(KV_W), halo, halo,
                  halo_next, halo_next,
                  tile(ATTN_W), tile(CONV_W), tile(CONV_W), tile(CONV_W), next8, ch, CONV_SPEC,
                  pl.BlockSpec((IN_W, D_MODEL), lambda i: (0, 0), pipeline_mode=pl.Buffered(1)),
                  tile(128), tile(128), tile(128)],
        out_specs=(tile(D_MODEL), pl.BlockSpec(memory_space=pl.ANY), whole(SMALL_ROWS, D_MODEL)),
        scratch_shapes=[pltpu.VMEM((tb, IN_W), BF16), pltpu.VMEM((IN_W, D_MODEL), F32)],
        compiler_params=_params(("arbitrary",)),
    )(x, norm_g, dh2, dq, dk, dv, dkh, dvh, dkh, dvh, dga, db, dgc, dcv, dcv, rest, conv_w, w_in_t,
      rope_c, rope_a, rope_b)


def _adamw_step(w, g, m, v):
    m2 = ADAM_B1 * m + (1.0 - ADAM_B1) * g
    v2 = ADAM_B2 * v + (1.0 - ADAM_B2) * jnp.square(g)
    m_hat = m2 / (1.0 - ADAM_B1 ** ADAM_STEP)
    v_hat = v2 / (1.0 - ADAM_B2 ** ADAM_STEP)
    return -ADAM_LR * (m_hat / (jnp.sqrt(v_hat) + ADAM_EPS) + ADAM_WD * w), m2, v2


def _adamw_weights(groups):
    steps = 4

    def body(*refs):
        ins, outs = refs[:4 * len(groups)], refs[4 * len(groups):]
        for k in range(len(groups)):
            res = _adamw_step(*(r[...] for r in ins[4 * k:4 * k + 4]))
            for o_ref, val in zip(outs[3 * k:3 * k + 3], res):
                o_ref[...] = val

    in_specs, out_specs, out_shape = [], [], []
    for w, _, _, _ in groups:
        rows, cols = w.shape
        spec = pl.BlockSpec((rows // steps, cols), lambda i: (i, 0))
        in_specs += [spec] * 4
        out_specs += [spec] * 3
        out_shape += [jax.ShapeDtypeStruct((rows, cols), F32)] * 3
    flat = pl.pallas_call(
        body, name="adamw_weights", grid=(steps,), out_shape=tuple(out_shape), in_specs=in_specs,
        out_specs=tuple(out_specs), compiler_params=_params(("arbitrary",)),
    )(*[a for grp in groups for a in grp])
    return [flat[3 * k:3 * k + 3] for k in range(len(groups))]


def _adamw_small(chip, small, params, m, v):
    def body(chip_ref, small_ref, conv_ref, *refs):
        ins, outs = refs[:12], refs[12:]
        outs[0][...] = jnp.sum(small_ref[6:7, :], axis=-1, keepdims=True)
        grads = (small_ref[0:1, :], small_ref[1:2, :], conv_ref[2:5, :], small_ref[5:6, 0:8])
        for k, g in enumerate(grads):
            outs[1 + k][...] = g
            res = _adamw_step(ins[k][...], g, ins[4 + k][...], ins[8 + k][...])
            for n, val in enumerate(res):
                outs[5 + 4 * n + k][...] = val

    full = lambda a: pl.BlockSpec(a.shape, lambda i, c: (0,) * len(a.shape))
    shapes = [jax.ShapeDtypeStruct(p.shape, F32) for p in params]
    outs = [jax.ShapeDtypeStruct((1, 1), F32)] + shapes * 4
    flat = pl.pallas_call(
        body, name="adamw_small",
        grid_spec=pltpu.PrefetchScalarGridSpec(
            num_scalar_prefetch=1, grid=(1,),
            in_specs=[full(small), pl.BlockSpec((SMALL_ROWS, 128), lambda i, c: (0, c[0]))]
            + [full(a) for a in (*params, *m, *v)],
            out_specs=tuple(full(s) for s in outs)),
        out_shape=tuple(outs), compiler_params=_params(("arbitrary",)),
    )(chip, small, small, *params, *m, *v)
    return flat[0], flat[1:5], [flat[5 + 4 * n:9 + 4 * n] for n in range(3)]


def kernel(x, norm_g, w_in, sinks, conv_w, w_out, final_g, loss_target, m_norm_g, m_w_in, m_sinks, m_conv_w, m_w_out, m_final_g, v_norm_g, v_w_in, v_sinks, v_conv_w, v_w_out, v_final_g):
    seq = x.shape[1]
    x2 = x.reshape(seq, D_MODEL)
    tgt = loss_target.reshape(seq, D_MODEL)
    ng = norm_g.reshape(1, D_MODEL)
    fg = final_g.reshape(1, D_MODEL)
    chip = 2 * lax.axis_index("x") + lax.axis_index("y")

    conv_w8 = jnp.zeros((8, 128), F32).at[0:3].set(conv_w)
    w_in_full = _gather_w_in(w_in.T).reshape(IN_W, D_MODEL)

    q, kd, vd, rest, rope_c, rope_a, rope_b, wo_all, cw_all = _fwd_proj(x2, ng, w_in_full, w_out, conv_w8)
    w_out_full = wo_all.reshape(D_MODEL, D_MODEL)
    attn, gates, dh2, g_wo, g_wo_b, small_f = _fwd_mix(x2, q, kd, vd, rest, sinks, cw_all, w_out_full, fg, tgt)
    out_blocks = lambda t: t.reshape(N_CHIPS, W_OUT_BLK, D_MODEL)
    dq, dk, dv, dkh, dvh, dga, db, dgc, dcv, small_m, grad_w_out = _bwd_mix(
        dh2, q, kd, vd, attn, gates, rest, sinks, cw_all, w_out_full, rope_c, rope_a, rope_b,
        out_blocks(g_wo), out_blocks(g_wo_b))
    grad_x, g_wi, small_p = _bwd_proj(x2, ng, dh2, dq, dk, dv, dkh, dvh, dga, db, dgc, dcv, rest, cw_all,
                                      w_in_full, rope_c, rope_a, rope_b)

    g_in_blocks = g_wi.reshape(N_CHIPS, W_IN_BLK, D_MODEL)
    grad_w_in_t, small = _reduce_grads(g_in_blocks, small_f, small_m, small_p)

    (upd_wi, upd_wo) = _adamw_weights([(w_in.T, grad_w_in_t, m_w_in.T, v_w_in.T),
                                       (w_out, grad_w_out, m_w_out, v_w_out)])
    row = lambda t: t.reshape(1, -1)
    loss, grads_s, upd_s = _adamw_small(
        chip.reshape(1), small, (ng, fg, conv_w, row(sinks)),
        (row(m_norm_g), row(m_final_g), m_conv_w, row(m_sinks)),
        (row(v_norm_g), row(v_final_g), v_conv_w, row(v_sinks)))

    def named(ng_, fg_, cw_, sk_, wi_t, wo_):
        return [ng_.reshape(D_MODEL), wi_t.T, sk_.reshape(8), cw_, wo_, fg_.reshape(D_MODEL)]

    g_named = named(*grads_s, grad_w_in_t, grad_w_out)
    out = [loss.reshape(()), grad_x.reshape(1, seq, D_MODEL)] + g_named
    for n in range(3):
        out += named(*upd_s[n], upd_wi[n], upd_wo[n])
    return tuple(out)
```

```python
import jax
import jax.numpy as jnp
from jax import lax
from jax.experimental import pallas as pl
from jax.experimental.pallas import tpu as pltpu

F32 = jnp.float32
BF16 = jnp.bfloat16

D_MODEL = 1024
HEAD_DIM = 64
ATTN_W = 512
KV_W = 128
CONV_W = 512
IN_W = 3328
REST_W = IN_W - ATTN_W - 2 * KV_W
BLOCK = 128
ROT_DIM = 16
ROPE_THETA = 500000.0
EPS = 1e-5
SCALE = 0.125
NEG = -1e30

N_CHIPS = 4
W_IN_BLK = IN_W // N_CHIPS
W_OUT_BLK = D_MODEL // N_CHIPS

ADAM_LR = 0.001
ADAM_B1 = 0.9
ADAM_B2 = 0.999
ADAM_EPS = 1e-08
ADAM_WD = 0.01
ADAM_STEP = 10

VMEM_LIMIT = 60 * 1024 * 1024
T_PROJ = 512
T_FMIX = 512
T_MIX = 512
SMALL_ROWS = 8
MESH = pl.DeviceIdType.MESH

_NT = (((1,), (1,)), ((), ()))
_TN = (((0,), (0,)), ((), ()))


def _params(sem=None):
    kw = dict(vmem_limit_bytes=VMEM_LIMIT)
    if sem is not None:
        kw["dimension_semantics"] = sem
    return pltpu.CompilerParams(**kw)


def _sigmoid(t):
    return 1.0 / (1.0 + jnp.exp(-t))


def _shift_down(t, prev8, k):
    rolled = pltpu.roll(t, k, 0)
    row = lax.broadcasted_iota(jnp.int32, t.shape, 0)
    for j in range(k):
        rolled = jnp.where(row == j, prev8[8 - k + j:8 - k + j + 1, :], rolled)
    return rolled


def _shift_up(t, next8, k):
    n = t.shape[0]
    rolled = pltpu.roll(t, n - k, 0)
    row = lax.broadcasted_iota(jnp.int32, t.shape, 0)
    for j in range(k):
        rolled = jnp.where(row == n - k + j, next8[j:j + 1, :], rolled)
    return rolled


def _rope(t, c, a, b):
    w = t.shape[1]
    reps = w // 128
    if reps > 1:
        c, a, b = (jnp.concatenate([z] * reps, axis=1) for z in (c, a, b))
    return t * c + pltpu.roll(t, w - 8, 1) * a + pltpu.roll(t, 8, 1) * b


def _lane_lo(shape):
    return lax.broadcasted_iota(jnp.int32, shape, 1) < HEAD_DIM


def _stack_heads(t, g):
    lo = _lane_lo((BLOCK, 128))
    parts = []
    for hh in range(4):
        pair = t[:, 256 * g + 128 * (hh // 2):256 * g + 128 * (hh // 2) + 128]
        keep = lo if hh % 2 == 0 else jnp.logical_not(lo)
        parts.append(jnp.where(keep, pair, jnp.zeros_like(pair)))
    return jnp.concatenate(parts, axis=0)


def _unstack_pair(o, pp):
    lo = _lane_lo((BLOCK, 128))
    return jnp.where(lo, o[256 * pp:256 * pp + 128], o[256 * pp + 128:256 * pp + 256])


def _band_masks(has_prev):
    r = lax.broadcasted_iota(jnp.int32, (4 * BLOCK, 2 * BLOCK), 0) % BLOCK
    kj = lax.broadcasted_iota(jnp.int32, (4 * BLOCK, 2 * BLOCK), 1)
    cur = (kj >= BLOCK) & (kj - BLOCK <= r)
    prev = (kj < BLOCK) & (kj > r)
    return cur | (prev & has_prev), cur | prev


def _sink_col(sinks_ref, g):
    r = lax.broadcasted_iota(jnp.int32, (4 * BLOCK, 1), 0) // BLOCK
    col = jnp.full((4 * BLOCK, 1), sinks_ref[4 * g + 3], F32)
    for hh in range(3):
        col = jnp.where(r == hh, sinks_ref[4 * g + hh], col)
    return col


def _probs(qs, kd, sink_col, mask):
    return _softmax(lax.dot_general(qs, kd, _NT, preferred_element_type=F32), sink_col, mask)


def _softmax(s, sink_col, mask):
    s = jnp.where(mask, s, NEG)
    m = jnp.maximum(jnp.max(s, axis=-1, keepdims=True), sink_col)
    p = jnp.exp(s - m)
    es = jnp.exp(sink_col - m)
    inv = 1.0 / (jnp.sum(p, axis=-1, keepdims=True) + es)
    return p * inv, es * inv


def _key_windows(i, nsub, kd_ref, vd_ref, kdp_ref, vdp_ref):
    mask_first, mask_rest = _band_masks(i > 0)
    out = []
    for sb in range(nsub):
        rows = slice(BLOCK * sb, BLOCK * (sb + 1))
        if sb == 0:
            kk = jnp.concatenate([kdp_ref[...], kd_ref[rows, :]], axis=0)
            vv = jnp.concatenate([vdp_ref[...], vd_ref[rows, :]], axis=0)
            out.append((rows, kk, vv, mask_first))
        else:
            both = slice(BLOCK * (sb - 1), BLOCK * (sb + 1))
            out.append((rows, kd_ref[both, :], vd_ref[both, :], mask_rest))
    return out


def _gather_w_in(w_in_t):
    hi = W_IN_BLK // 2
    qr = hi // 2

    def body(wi_ref, wi_all, send_sems, recv_sems):
        x, y, c = lax.axis_index("x"), lax.axis_index("y"), lax.axis_index("c")
        me, sibling = (x, y), (x, y, 1 - c)
        xnb, ynb, diag = (1 - x, y), (x, 1 - y), (1 - x, 1 - y)

        wi_all[2 * x + y] = wi_ref[...].astype(BF16)

        def copy(k, chip, half, quarter, to):
            r = wi_all.at[2 * chip[0] + chip[1], pl.ds(half * hi + quarter * qr, qr)]
            return pltpu.make_async_remote_copy(src_ref=r, dst_ref=r, send_sem=send_sems.at[k],
                                                recv_sem=recv_sems.at[k], device_id=to, device_id_type=MESH)

        plan = [(0, xnb, 0), (1, ynb, 1), (2, xnb, 1), (3, ynb, 0)]
        sent = [copy(k, me, c, quarter, (*nb, c)) for k, nb, quarter in plan]
        for cp in sent:
            cp.start()
        arrivals = [(0, xnb, 0), (1, ynb, 1), (2, xnb, 1), (3, ynb, 0), (4, diag, 0), (5, diag, 1)]
        relay = {0: (4, ynb), 1: (5, xnb)}
        for k, chip, quarter in arrivals:
            copy(k, chip, c, quarter, (x, y, c)).wait_recv()
            if k in relay:
                sent.append(copy(relay[k][0], chip, c, quarter, (*relay[k][1], c)))
                sent[-1].start()
            sent.append(copy(6 + k, chip, c, quarter, sibling))
            sent[-1].start()
        for k, chip, quarter in arrivals:
            copy(6 + k, chip, 1 - c, quarter, (x, y, c)).wait_recv()
        for cp in sent:
            cp.wait_send()

    vmem = pl.BlockSpec(memory_space=pltpu.VMEM)
    return pl.pallas_call(
        body, name="gather_w_in",
        out_shape=jax.ShapeDtypeStruct((N_CHIPS, W_IN_BLK, D_MODEL), BF16),
        in_specs=[vmem], out_specs=vmem,
        scratch_shapes=[pltpu.SemaphoreType.DMA((12,)), pltpu.SemaphoreType.DMA((12,))],
        compiler_params=_params(),
    )(w_in_t)


def _reduce_grads(g_in, *smalls):
    hi = W_IN_BLK // 2
    qr = hi // 2
    q0, q1 = slice(0, qr), slice(qr, hi)

    def body(gi_hbm, s0_ref, s1_ref, s2_ref, gi_out, small_out,
             mine_i, sib_i, out_i, ici_i, small_in, small_ref, send_sems, recv_sems, local_sems):
        x, y, c = lax.axis_index("x"), lax.axis_index("y"), lax.axis_index("c")
        my_dev = 4 * x + 2 * y + c
        sibling = (x, y, 1 - c)
        xnb, ynb = (1 - x, y, c), (x, 1 - y, c)
        order = [(1 - x, 1 - y), (1 - x, y), (x, 1 - y), (x, y)]

        def remote(k, src, dst, to):
            return pltpu.make_async_remote_copy(src_ref=src, dst_ref=dst, send_sem=send_sems.at[k],
                                                recv_sem=recv_sems.at[k], device_id=to, device_id_type=MESH)

        small_ref[...] = s0_ref[...] + s1_ref[...] + s2_ref[...]
        small_cps = []
        for f in range(1, 8):
            fx, fy, fc = f >> 2, (f >> 1) & 1, f & 1
            small_cps.append(remote(11 + f - 1, small_ref, small_in.at[f - 1], (x ^ fx, y ^ fy, c ^ fc)))
        for cp in small_cps:
            cp.start()

        own, to_sib = [], []
        for n, chip in enumerate(order):
            j = 2 * chip[0] + chip[1]
            own.append(pltpu.make_async_copy(gi_hbm.at[j, pl.ds(c * hi, hi)], mine_i.at[n], local_sems.at[n]))
            to_sib.append(remote(6 + n, gi_hbm.at[j, pl.ds((1 - c) * hi, hi)], sib_i.at[n], sibling))
            own[-1].start()
            to_sib[-1].start()

        def pair_sum(n):
            own[n].wait()
            to_sib[n].wait_recv()
            return mine_i[n] + sib_i[n]

        ici = [remote(k, out_i.at[k], ici_i.at[k], xnb if k % 2 == 0 else ynb) for k in range(6)]

        def send(k, rows_f32):
            out_i[k] = rows_f32.astype(BF16)
            ici[k].start()

        p_diag = pair_sum(0)
        send(0, p_diag[q0])
        send(1, p_diag[q1])
        p_x = pair_sum(1)
        send(2, p_x[q0])
        p_y = pair_sum(2)
        send(3, p_y[q1])
        ici[0].wait_recv()
        send(5, p_y[q0] + ici_i[0].astype(F32))
        ici[1].wait_recv()
        send(4, p_x[q1] + ici_i[1].astype(F32))
        p_mine = pair_sum(3)
        for k in range(2, 6):
            ici[k].wait_recv()
        gi_out[pl.ds(c * hi, qr), :] = p_mine[q0] + ici_i[2].astype(F32) + ici_i[5].astype(F32)
        gi_out[pl.ds(c * hi + qr, qr), :] = p_mine[q1] + ici_i[4].astype(F32) + ici_i[3].astype(F32)

        swap = [remote(10, gi_out.at[pl.ds(c * hi, hi)], gi_out.at[pl.ds(c * hi, hi)], sibling)]
        for cp in swap:
            cp.start()

        for cp in small_cps:
            cp.wait_recv()
        total = jnp.zeros((SMALL_ROWS, D_MODEL), F32)
        for d in range(8):
            slot = jnp.maximum((d ^ my_dev) - 1, 0)
            total = total + jnp.where(d == my_dev, small_ref[...], small_in[slot])
        small_out[...] = total

        remote(10, gi_out.at[pl.ds((1 - c) * hi, hi)], gi_out.at[pl.ds((1 - c) * hi, hi)], sibling).wait_recv()
        for cp in to_sib + ici + swap + small_cps:
            cp.wait_send()

    vmem = pl.BlockSpec(memory_space=pltpu.VMEM)
    anyspace = pl.BlockSpec(memory_space=pl.ANY)
    return pl.pallas_call(
        body, name="reduce_grads",
        out_shape=(jax.ShapeDtypeStruct((W_IN_BLK, D_MODEL), F32), jax.ShapeDtypeStruct((SMALL_ROWS, D_MODEL), F32)),
        in_specs=[anyspace, vmem, vmem, vmem], out_specs=(vmem, vmem),
        scratch_shapes=[pltpu.VMEM((N_CHIPS, hi, D_MODEL), F32), pltpu.VMEM((N_CHIPS, hi, D_MODEL), F32),
                        pltpu.VMEM((6, qr, D_MODEL), BF16), pltpu.VMEM((6, qr, D_MODEL), BF16),
                        pltpu.VMEM((7, SMALL_ROWS, D_MODEL), F32), pltpu.VMEM((SMALL_ROWS, D_MODEL), F32),
                        pltpu.SemaphoreType.DMA((18,)), pltpu.SemaphoreType.DMA((18,)),
                        pltpu.SemaphoreType.DMA((4,))],
        compiler_params=_params(),
    )(g_in, *smalls)


def _fwd_proj(x, norm_g, w_in_t, w_out, conv_w8):
    seq = x.shape[0]
    nt = seq // T_PROJ
    lane = jnp.arange(128, dtype=jnp.int32) % HEAD_DIM
    inv_freq = ROPE_THETA ** (-(2 * (lane % 8)).astype(F32) / ROT_DIM)
    inv_freq = jnp.where(lane < ROT_DIM, inv_freq, 0.0).reshape(1, 128)
    in_tile = jnp.arange(T_PROJ, dtype=jnp.int32).astype(F32)[:, None] * inv_freq
    cos_in, sin_in = jnp.cos(in_tile), jnp.sin(in_tile)
    start = jnp.repeat((jnp.arange(nt, dtype=jnp.int32) * T_PROJ).astype(F32), 8)[:, None] * inv_freq
    cos_st, sin_st = jnp.cos(start), jnp.sin(start)

    def body(x_ref, g_ref, w_ref, cs_ref, ss_ref, ci_ref, si_ref, wo_ref, cw_ref,
             q_ref, kd_ref, vd_ref, rest_ref, c_ref, a_ref, b_ref, wo_all, cw_all,
             wo_stage, send_sems, recv_sems, local_sems):
        i = pl.program_id(0)
        mx, my, mc = lax.axis_index("x"), lax.axis_index("y"), lax.axis_index("c")
        chips = [(1 - mx, my), (mx, 1 - my), (1 - mx, 1 - my)]

        def gather(blocks):
            cps = []
            for k, chip in enumerate(chips):
                for n, (src, dst) in enumerate(((wo_stage, wo_all), (cw_ref, cw_all))):
                    cps.append(pltpu.make_async_remote_copy(
                        src_ref=src, dst_ref=dst.at[blocks[k]], send_sem=send_sems.at[2 * k + n],
                        recv_sem=recv_sems.at[2 * k + n], device_id=(*chip, mc), device_id_type=MESH))
            return cps

        me = 2 * mx + my
        own = [pltpu.make_async_copy(wo_stage, wo_all.at[me], local_sems.at[0]),
               pltpu.make_async_copy(cw_ref, cw_all.at[me], local_sems.at[1])]

        @pl.when(i == 0)
        def _():
            wo_stage[...] = wo_ref[...].astype(BF16)
            for cp in own + gather([me] * 3):
                cp.start()

        xf = x_ref[...]
        r1 = lax.rsqrt(jnp.mean(xf * xf, axis=-1, keepdims=True) + EPS)
        xn = (xf * r1 * g_ref[...]).astype(BF16)
        cs, ss = cs_ref[0:1, :], ss_ref[0:1, :]
        c = cs * ci_ref[...] - ss * si_ref[...]
        sin = ss * ci_ref[...] + cs * si_ref[...]
        j = lax.broadcasted_iota(jnp.int32, (T_PROJ, 128), 1) % HEAD_DIM
        a = jnp.where(j < 8, -sin, 0.0)
        b = jnp.where(j >= 8, sin, 0.0)
        c_ref[...], a_ref[...], b_ref[...] = c, a, b
        proj = lambda lo_c, w: lax.dot_general(xn, w_ref[lo_c:lo_c + w, :], _NT, preferred_element_type=F32)
        q_ref[...] = (_rope(proj(0, ATTN_W), c, a, b) * SCALE).astype(BF16)
        kv = proj(ATTN_W, 2 * KV_W)
        k = _rope(kv[:, 0:KV_W], c, a, b)
        v = kv[:, KV_W:2 * KV_W]
        lo = _lane_lo(k.shape)
        for t, ref in ((k, kd_ref), (v, vd_ref)):
            sw = pltpu.roll(t, HEAD_DIM, 1)
            ref[:, 0:128] = jnp.where(lo, t, sw).astype(BF16)
            ref[:, 128:256] = jnp.where(lo, sw, t).astype(BF16)
        for n in range(REST_W // 512):
            rest_ref[:, 512 * n:512 * (n + 1)] = proj(ATTN_W + 2 * KV_W + 512 * n, 512)

        @pl.when(i == nt - 1)
        def _():
            sent = gather([me] * 3)
            for cp in gather([2 * chip[0] + chip[1] for chip in chips]):
                cp.wait_recv()
            for cp in sent:
                cp.wait_send()
            for cp in own:
                cp.wait()

    tile = lambda w: pl.BlockSpec((T_PROJ, w), lambda i: (i, 0))
    whole = lambda r, w: pl.BlockSpec((r, w), lambda i: (0, 0))
    vmem = pl.BlockSpec(memory_space=pltpu.VMEM)
    hbm = pl.BlockSpec(memory_space=pl.ANY)
    return pl.pallas_call(
        body, name="fwd_proj", grid=(nt,),
        out_shape=(jax.ShapeDtypeStruct((seq, ATTN_W), BF16), jax.ShapeDtypeStruct((seq, 2 * KV_W), BF16),
                   jax.ShapeDtypeStruct((seq, 2 * KV_W), BF16), jax.ShapeDtypeStruct((seq, REST_W), F32))
        + (jax.ShapeDtypeStruct((seq, 128), F32),) * 3
        + (jax.ShapeDtypeStruct((N_CHIPS, W_OUT_BLK, D_MODEL), BF16), jax.ShapeDtypeStruct((N_CHIPS, 8, 128), F32)),
        in_specs=[tile(D_MODEL), whole(1, D_MODEL), whole(IN_W, D_MODEL), pl.BlockSpec((8, 128), lambda i: (i, 0)),
                  pl.BlockSpec((8, 128), lambda i: (i, 0)), whole(T_PROJ, 128), whole(T_PROJ, 128), vmem, vmem],
        out_specs=(tile(ATTN_W), tile(2 * KV_W), tile(2 * KV_W), tile(REST_W), tile(128), tile(128), tile(128),
                   hbm, hbm),
        scratch_shapes=[pltpu.VMEM((W_OUT_BLK, D_MODEL), BF16), pltpu.SemaphoreType.DMA((6,)),
                        pltpu.SemaphoreType.DMA((6,)), pltpu.SemaphoreType.DMA((2,))],
        compiler_params=_params(("arbitrary",)),
    )(x, norm_g, w_in_t, cos_st, sin_st, cos_in, sin_in, w_out, conv_w8)


CONV_SPEC = pl.BlockSpec((N_CHIPS, 8, 128), lambda i: (0, 0, 0))


def _conv_rows(cw_ref):
    return jnp.concatenate([cw_ref[j] for j in range(N_CHIPS)], axis=1)


def _conv_parts(rest_ref, prev_ref, cw_ref, first):
    u = rest_ref[:, 1024:1536] * rest_ref[:, 1536:2048]
    up = prev_ref[:, 1024:1536] * prev_ref[:, 1536:2048]
    up = jnp.where(first, jnp.zeros_like(up), up)
    um1 = _shift_down(u, up, 1)
    um2 = _shift_down(u, up, 2)
    cw = _conv_rows(cw_ref)
    cv = cw[0:1, :] * um2 + cw[1:2, :] * um1 + cw[2:3, :] * u
    return u, um1, um2, cv


def _fwd_mix(x, q, kd, vd, rest, sinks, conv_w, w_out, final_g, target):
    seq = x.shape[0]
    nt = seq // T_FMIX
    nsub = T_FMIX // BLOCK

    def body(sinks_ref, x_ref, q_ref, kd_ref, vd_ref, kdp_ref, vdp_ref, rest_ref, restp_ref, cw_ref, wo_ref,
             fg_ref, tgt_ref, attn_ref, sinkw_ref, prob_ref, gate_ref, dh2_ref, gwo_hbm, gwob_hbm, small_ref,
             mix_ref, pmix_ref, pdh2_ref, gwo_ref):
        i = pl.program_id(0)

        @pl.when(i == 0)
        def _():
            small_ref[...] = jnp.zeros_like(small_ref)
            gwo_ref[...] = jnp.zeros_like(gwo_ref)
            pmix_ref[...] = jnp.zeros_like(pmix_ref)
            pdh2_ref[...] = jnp.zeros_like(pdh2_ref)

        chains = []
        for rows, kk, vv, mask in _key_windows(i, nsub, kd_ref, vd_ref, kdp_ref, vdp_ref):
            qt = q_ref[rows, :]
            for g in range(2):
                kg = kk[:, 128 * g:128 * (g + 1)]
                chains.append(dict(g=g, rows=rows, mask=mask, vg=vv[:, 128 * g:128 * (g + 1)],
                                   s=lax.dot_general(_stack_heads(qt, g), kg, _NT, preferred_element_type=F32)))
        gwo_ref[...] += lax.dot_general(pmix_ref[...], pdh2_ref[...], _TN, preferred_element_type=F32)
        for ch in chains:
            ch["prob"], ch["psink"] = _softmax(ch.pop("s"), _sink_col(sinks_ref, ch["g"]), ch["mask"])
        for k, ch in enumerate(chains):
            prob = ch["prob"].astype(BF16)
            prob_ref[k] = prob
            o = jnp.dot(prob, ch["vg"], preferred_element_type=F32)
            ow = o * ch["psink"]
            for pp in range(2):
                lanes = slice(256 * ch["g"] + 128 * pp, 256 * ch["g"] + 128 * (pp + 1))
                attn_ref[ch["rows"], lanes] = _unstack_pair(o, pp)
                sinkw_ref[ch["rows"], lanes] = _unstack_pair(ow, pp)

        def silu_parts(t, lo_c):
            sg = _sigmoid(t)
            silu = t * sg
            gate_ref[:, lo_c:lo_c + 512] = silu
            gate_ref[:, lo_c + 512:lo_c + 1024] = sg * (1.0 + t * (1.0 - sg))
            return silu

        mix_ref[:, 0:ATTN_W] = (attn_ref[...] * silu_parts(rest_ref[:, 0:512], 0)).astype(BF16)
        _, _, _, cv = _conv_parts(rest_ref, restp_ref, cw_ref, i == 0)
        mix_ref[:, ATTN_W:] = (rest_ref[:, 512:1024] * cv * silu_parts(rest_ref[:, 2048:2560], 1024)).astype(BF16)

        h2 = x_ref[...] + jnp.dot(mix_ref[...], wo_ref[...], preferred_element_type=F32)
        r2 = lax.rsqrt(jnp.mean(h2 * h2, axis=-1, keepdims=True) + EPS)
        n2 = h2 * r2
        err = n2 * fg_ref[...] - tgt_ref[...]
        dy = err * (1.0 / D_MODEL)
        small_ref[6:7, :] += jnp.sum(err * err, axis=0, keepdims=True) * (0.5 / D_MODEL)
        small_ref[1:2, :] += jnp.sum(dy * n2, axis=0, keepdims=True)
        dn = dy * fg_ref[...]
        dh2 = r2 * (dn - n2 * jnp.mean(dn * n2, axis=-1, keepdims=True))
        dh2_ref[...] = dh2
        pmix_ref[...] = mix_ref[...]
        pdh2_ref[...] = dh2.astype(BF16)

        @pl.when(i == nt - 1)
        def _():
            gwo_ref[...] += lax.dot_general(pmix_ref[...], pdh2_ref[...], _TN, preferred_element_type=F32)
            pltpu.sync_copy(gwo_ref, gwo_hbm)
            for n in range(D_MODEL // T_FMIX):
                slab = slice(T_FMIX * n, T_FMIX * (n + 1))
                pmix_ref[...] = gwo_ref[slab, :].astype(BF16)
                pltpu.sync_copy(pmix_ref, gwob_hbm.at[slab])

    tile = lambda w: pl.BlockSpec((T_FMIX, w), lambda i: (i, 0))
    whole = lambda r, w: pl.BlockSpec((r, w), lambda i: (0, 0))
    prev_blk = pl.BlockSpec((BLOCK, 2 * KV_W), lambda i: (jnp.maximum(i * nsub - 1, 0), 0))
    prev8 = pl.BlockSpec((8, REST_W), lambda i: (jnp.maximum(i * (T_FMIX // 8) - 1, 0), 0))
    return pl.pallas_call(
        body, name="fwd_mix", grid=(nt,),
        out_shape=(jax.ShapeDtypeStruct((seq, ATTN_W), F32), jax.ShapeDtypeStruct((seq, ATTN_W), F32),
                   jax.ShapeDtypeStruct((2 * seq // BLOCK, 4 * BLOCK, 2 * BLOCK), BF16),
                   jax.ShapeDtypeStruct((seq, 4 * 512), F32), jax.ShapeDtypeStruct((seq, D_MODEL), F32),
                   jax.ShapeDtypeStruct((D_MODEL, D_MODEL), F32), jax.ShapeDtypeStruct((D_MODEL, D_MODEL), BF16),
                   jax.ShapeDtypeStruct((SMALL_ROWS, D_MODEL), F32)),
        in_specs=[pl.BlockSpec(memory_space=pltpu.SMEM), tile(D_MODEL), tile(ATTN_W), tile(2 * KV_W), tile(2 * KV_W),
                  prev_blk, prev_blk, tile(REST_W), prev8, CONV_SPEC,
                  pl.BlockSpec((D_MODEL, D_MODEL), lambda i: (0, 0), pipeline_mode=pl.Buffered(1)),
                  whole(1, D_MODEL), tile(D_MODEL)],
        out_specs=(tile(ATTN_W), tile(ATTN_W), pl.BlockSpec((2 * nsub, 4 * BLOCK, 2 * BLOCK), lambda i: (i, 0, 0)),
                   tile(4 * 512), tile(D_MODEL), pl.BlockSpec(memory_space=pl.ANY), pl.BlockSpec(memory_space=pl.ANY),
                   whole(SMALL_ROWS, D_MODEL)),
        scratch_shapes=[pltpu.VMEM((T_FMIX, D_MODEL), BF16)] * 3 + [pltpu.VMEM((D_MODEL, D_MODEL), F32)],
        compiler_params=_params(("arbitrary",)),
    )(sinks, x, q, kd, vd, kd, vd, rest, rest, conv_w, w_out, final_g, target)


def _scatter_copies(g_hbm, gb_hbm, mine, land, send_sems, recv_sems, local_sem, half):
    x, y, c = lax.axis_index("x"), lax.axis_index("y"), lax.axis_index("c")
    cps = []
    for f in range(1, 8):
        to = (x ^ (f >> 2), y ^ ((f >> 1) & 1), c ^ (f & 1))
        src = gb_hbm.at[2 * to[0] + to[1], pl.ds(to[2] * half, half)]
        cps.append(pltpu.make_async_remote_copy(src_ref=src, dst_ref=land.at[f - 1], send_sem=send_sems.at[f - 1],
                                                recv_sem=recv_sems.at[f - 1], device_id=to, device_id_type=MESH))
    own = pltpu.make_async_copy(g_hbm.at[2 * x + y, pl.ds(c * half, half)], mine, local_sem)
    return cps, own


def _scatter_finish(cps, own, mine, land, out_hbm, send_sems, recv_sems, local_sem, half):
    x, y, c = lax.axis_index("x"), lax.axis_index("y"), lax.axis_index("c")
    own.wait()
    tot = mine[...]
    for f in range(1, 8):
        cps[f - 1].wait_recv()
        tot = tot + land[f - 1].astype(F32)
    mine[...] = tot

    def swap(rows_of):
        return pltpu.make_async_remote_copy(src_ref=mine, dst_ref=out_hbm.at[pl.ds(rows_of * half, half)],
                                            send_sem=send_sems.at[7], recv_sem=recv_sems.at[7],
                                            device_id=(x, y, 1 - c), device_id_type=MESH)

    keep = pltpu.make_async_copy(mine, out_hbm.at[pl.ds(c * half, half)], local_sem)
    keep.start()
    swap(c).start()
    swap(1 - c).wait_recv()
    keep.wait()
    for cp in cps:
        cp.wait_send()
    swap(c).wait_send()


def _bwd_mix(dh2, q, kd, vd, attn, sinkw, probs, gates, rest, conv_w, w_out, rope_c, rope_a, rope_b, g_out, g_out_b):
    seq = dh2.shape[0]
    nt = seq // T_MIX
    nsub = T_MIX // BLOCK
    ho = W_OUT_BLK // 2

    def body(dh2_ref, q_ref, kd_ref, vd_ref, kdp_ref, vdp_ref, attn_ref, sinkw_ref, prob_ref, gate_ref, rest_ref,
             restp_ref, cw_ref, wo_ref, c_ref, a_ref, b_ref, go_hbm, gob_hbm,
             dq_ref, dk_ref, dv_ref, dkh_ref, dvh_ref, dga_ref, db_ref, dgc_ref, dcv_ref, small_ref, go_out,
             dmix_ref, dsink_ref, mine_o, land_o, send_sems, recv_sems, local_sems):
        i = pl.program_id(0)
        scatter = (mine_o, land_o, send_sems, recv_sems, local_sems.at[0], ho)

        @pl.when(i == 0)
        def _():
            small_ref[...] = jnp.zeros_like(small_ref)
            dsink_ref[...] = jnp.zeros_like(dsink_ref)
            cps, own = _scatter_copies(go_hbm, gob_hbm, *scatter)
            for cp in cps + [own]:
                cp.start()

        dmix_ref[...] = lax.dot_general(dh2_ref[...].astype(BF16), wo_ref[...], _NT, preferred_element_type=F32)

        dma = dmix_ref[:, 0:ATTN_W]
        dga_ref[...] = (dma * attn_ref[...] * gate_ref[:, 512:1024]).astype(BF16)
        d_attn = dma * gate_ref[:, 0:512]
        dmix_ref[:, 0:ATTN_W] = d_attn
        dsink_ref[0:1, :] += jnp.sum(d_attn * sinkw_ref[...], axis=0, keepdims=True)

        u, um1, um2, cv = _conv_parts(rest_ref, restp_ref, cw_ref, i == 0)
        bg = rest_ref[:, 512:1024]
        dmc = dmix_ref[:, ATTN_W:]
        t1 = dmc * gate_ref[:, 1024:1536]
        db_ref[...] = (t1 * cv).astype(BF16)
        dcv = t1 * bg
        dcv_ref[...] = dcv
        dgc_ref[...] = (dmc * (bg * cv) * gate_ref[:, 1536:2048]).astype(BF16)
        small_ref[2:3, 0:CONV_W] += jnp.sum(dcv * um2, axis=0, keepdims=True)
        small_ref[3:4, 0:CONV_W] += jnp.sum(dcv * um1, axis=0, keepdims=True)
        small_ref[4:5, 0:CONV_W] += jnp.sum(dcv * u, axis=0, keepdims=True)

        lo = _lane_lo((2 * BLOCK, 128))
        dk_blocks = [None] * (nsub + 1)
        dv_blocks = [None] * (nsub + 1)

        def add(lst, n, val):
            lst[n] = val if lst[n] is None else lst[n] + val

        chains = []
        for rows, kk, vv, _ in _key_windows(i, nsub, kd_ref, vd_ref, kdp_ref, vdp_ref):
            qt = q_ref[rows, :]
            do = dmix_ref[rows, 0:ATTN_W]
            dot = do.astype(BF16)
            for g in range(2):
                rs = jnp.sum(_stack_heads(do, g) * _stack_heads(attn_ref[rows, :], g), axis=-1, keepdims=True)
                chains.append(dict(g=g, rows=rows, rs=rs, qs=_stack_heads(qt, g), dos=_stack_heads(dot, g),
                                   kg=kk[:, 128 * g:128 * (g + 1)], vg=vv[:, 128 * g:128 * (g + 1)]))
        for ch in chains:
            ch["dp"] = lax.dot_general(ch["dos"], ch["vg"], _NT, preferred_element_type=F32)
        for k, ch in enumerate(chains):
            ch["ds"] = (prob_ref[k].astype(F32) * (ch["dp"] - ch["rs"])).astype(BF16)
        for k, ch in enumerate(chains):
            dqs = jnp.dot(ch["ds"], ch["kg"], preferred_element_type=F32) * SCALE
            c, a, b = c_ref[ch["rows"], :], a_ref[ch["rows"], :], b_ref[ch["rows"], :]
            for pp in range(2):
                lanes = slice(256 * ch["g"] + 128 * pp, 256 * ch["g"] + 128 * (pp + 1))
                dq_ref[ch["rows"], lanes] = _rope(_unstack_pair(dqs, pp), c, -a, -b).astype(BF16)
            dkd = lax.dot_general(ch["ds"], ch["qs"], _TN, preferred_element_type=F32)
            dvd = lax.dot_general(prob_ref[k], ch["dos"], _TN, preferred_element_type=F32)
            ch["dk"] = dkd + pltpu.roll(dkd, HEAD_DIM, 1)
            ch["dv"] = dvd + pltpu.roll(dvd, HEAD_DIM, 1)
        for sb in range(nsub):
            dk2 = jnp.where(lo, chains[2 * sb]["dk"], chains[2 * sb + 1]["dk"])
            dv2 = jnp.where(lo, chains[2 * sb]["dv"], chains[2 * sb + 1]["dv"])
            add(dk_blocks, sb, dk2[0:BLOCK])
            add(dk_blocks, sb + 1, dk2[BLOCK:])
            add(dv_blocks, sb, dv2[0:BLOCK])
            add(dv_blocks, sb + 1, dv2[BLOCK:])
        dkh_ref[0] = dk_blocks[0]
        dvh_ref[0] = dv_blocks[0]
        for sb in range(nsub):
            dk_ref[BLOCK * sb:BLOCK * (sb + 1), :] = dk_blocks[sb + 1]
            dv_ref[BLOCK * sb:BLOCK * (sb + 1), :] = dv_blocks[sb + 1]

        @pl.when(i == nt - 1)
        def _():
            head = lax.broadcasted_iota(jnp.int32, (1, ATTN_W), 1) // HEAD_DIM
            for h in range(8):
                tot = jnp.sum(jnp.where(head == h, dsink_ref[0:1, :], 0.0), axis=-1, keepdims=True)
                small_ref[5:6, h:h + 1] = -tot
            cps, own = _scatter_copies(go_hbm, gob_hbm, *scatter)
            _scatter_finish(cps, own, mine_o, land_o, go_out, send_sems, recv_sems, local_sems.at[1], ho)

    tile = lambda w: pl.BlockSpec((T_MIX, w), lambda i: (i, 0))
    whole = lambda r, w: pl.BlockSpec((r, w), lambda i: (0, 0))
    prev_blk = pl.BlockSpec((BLOCK, 2 * KV_W), lambda i: (jnp.maximum(i * nsub - 1, 0), 0))
    prev8 = pl.BlockSpec((8, REST_W), lambda i: (jnp.maximum(i * (T_MIX // 8) - 1, 0), 0))
    halo = pl.BlockSpec((1, BLOCK, KV_W), lambda i: (i, 0, 0))
    hbm = pl.BlockSpec(memory_space=pl.ANY)
    bf = lambda w: jax.ShapeDtypeStruct((seq, w), BF16)
    f32 = lambda w: jax.ShapeDtypeStruct((seq, w), F32)
    return pl.pallas_call(
        body, name="bwd_mix", grid=(nt,),
        out_shape=(bf(ATTN_W), f32(KV_W), f32(KV_W), jax.ShapeDtypeStruct((nt, BLOCK, KV_W), F32),
                   jax.ShapeDtypeStruct((nt, BLOCK, KV_W), F32), bf(ATTN_W), bf(CONV_W), bf(CONV_W), f32(CONV_W),
                   jax.ShapeDtypeStruct((SMALL_ROWS, D_MODEL), F32), jax.ShapeDtypeStruct((W_OUT_BLK, D_MODEL), F32)),
        in_specs=[tile(D_MODEL), tile(ATTN_W), tile(2 * KV_W), tile(2 * KV_W),
                  prev_blk, prev_blk, tile(ATTN_W), tile(ATTN_W),
                  pl.BlockSpec((2 * nsub, 4 * BLOCK, 2 * BLOCK), lambda i: (i, 0, 0)), tile(4 * 512), tile(REST_W),
                  prev8, CONV_SPEC, pl.BlockSpec((D_MODEL, D_MODEL), lambda i: (0, 0), pipeline_mode=pl.Buffered(1)),
                  tile(128), tile(128), tile(128), hbm, hbm],
        out_specs=(tile(ATTN_W), tile(KV_W), tile(KV_W), halo, halo, tile(ATTN_W), tile(CONV_W), tile(CONV_W),
                   tile(CONV_W), whole(SMALL_ROWS, D_MODEL), hbm),
        scratch_shapes=[pltpu.VMEM((T_MIX, D_MODEL), F32), pltpu.VMEM((8, ATTN_W), F32),
                        pltpu.VMEM((ho, D_MODEL), F32), pltpu.VMEM((7, ho, D_MODEL), BF16),
                        pltpu.SemaphoreType.DMA((8,)), pltpu.SemaphoreType.DMA((8,)), pltpu.SemaphoreType.DMA((2,))],
        compiler_params=_params(("arbitrary",)),
    )(dh2, q, kd, vd, kd, vd, attn, sinkw, probs, gates, rest, rest, conv_w, w_out, rope_c, rope_a, rope_b,
      g_out, g_out_b)


def _bwd_proj(x, norm_g, dh2, dq, dk, dv, dkh, dvh, dga, db, dgc, dcv, rest, conv_w, w_in_t, rope_c, rope_a, rope_b):
    seq = x.shape[0]
    tb = T_PROJ
    per = tb // T_MIX
    nt = seq // tb

    def body(x_ref, g_ref, dh2_ref, dq_ref, dk_ref, dv_ref, dkh_ref, dvh_ref, dkn_ref, dvn_ref, dga_ref, db_ref,
             dgc_ref, dcv_ref, dcvn_ref, ch_ref, cw_ref, w_ref, c_ref, a_ref, b_ref, gx_ref, gw_hbm, small_ref,
             dp_ref, acc_ref):
        i = pl.program_id(0)

        @pl.when(i == 0)
        def _():
            small_ref[...] = jnp.zeros_like(small_ref)
            acc_ref[...] = jnp.zeros_like(acc_ref)

        last = i == nt - 1
        keep = jnp.where(last, 0.0, 1.0)
        pad = jnp.zeros((T_MIX - BLOCK, KV_W), F32)

        def with_halos(main_ref, halo_ref, next_ref):
            parts = []
            for m in range(1, per + 1):
                parts += [pad, halo_ref[m] if m < per else next_ref[0] * keep]
            return main_ref[...] + jnp.concatenate(parts, axis=0)

        dk = with_halos(dk_ref, dkh_ref, dkn_ref)
        dv = with_halos(dv_ref, dvh_ref, dvn_ref)
        dp_ref[:, 0:ATTN_W] = dq_ref[...]
        dp_ref[:, ATTN_W:ATTN_W + KV_W] = _rope(dk, c_ref[...], -a_ref[...], -b_ref[...]).astype(BF16)
        dp_ref[:, ATTN_W + KV_W:ATTN_W + 2 * KV_W] = dv.astype(BF16)
        base = ATTN_W + 2 * KV_W
        dp_ref[:, base:base + 512] = dga_ref[...]
        dp_ref[:, base + 512:base + 1024] = db_ref[...]
        dcv = dcv_ref[...]
        nxt = dcvn_ref[...] * keep
        cw = _conv_rows(cw_ref)
        du = cw[2:3, :] * dcv + cw[1:2, :] * _shift_up(dcv, nxt, 1) + cw[0:1, :] * _shift_up(dcv, nxt, 2)
        dp_ref[:, base + 1024:base + 1536] = (du * ch_ref[:, 512:1024]).astype(BF16)
        dp_ref[:, base + 1536:base + 2048] = (du * ch_ref[:, 0:512]).astype(BF16)
        dp_ref[:, base + 2048:base + 2560] = dgc_ref[...]

        xf = x_ref[...]
        r1 = lax.rsqrt(jnp.mean(xf * xf, axis=-1, keepdims=True) + EPS)
        n1 = xf * r1
        xn = (n1 * g_ref[...]).astype(BF16)
        for n in range(IN_W // 256):
            cols = slice(256 * n, 256 * (n + 1))
            acc_ref[cols, :] += lax.dot_general(dp_ref[:, cols], xn, _TN, preferred_element_type=F32)
        dxn = jnp.dot(dp_ref[...], w_ref[...], preferred_element_type=F32)
        small_ref[0:1, :] += jnp.sum(dxn * n1, axis=0, keepdims=True)
        dxg = dxn * g_ref[...]
        gx_ref[...] = r1 * (dxg - n1 * jnp.mean(dxg * n1, axis=-1, keepdims=True)) + dh2_ref[...]

        @pl.when(last)
        def _():
            pltpu.sync_copy(acc_ref, gw_hbm)

    tile = lambda w: pl.BlockSpec((tb, w), lambda i: (i, 0))
    whole = lambda r, w: pl.BlockSpec((r, w), lambda i: (0, 0))
    halo = pl.BlockSpec((per, BLOCK, KV_W), lambda i: (i, 0, 0))
    halo_next = pl.BlockSpec((1, BLOCK, KV_W), lambda i: (jnp.minimum((i + 1) * per, seq // T_MIX - 1), 0, 0))
    next8 = pl.BlockSpec((8, CONV_W), lambda i: (jnp.minimum((i + 1) * (tb // 8), seq // 8 - 1), 0))
    ch = pl.BlockSpec((tb, 1024), lambda i: (i, 1))
    return pl.pallas_call(
        body, name="bwd_proj", grid=(nt,),
        out_shape=(jax.ShapeDtypeStruct((seq, D_MODEL), F32), jax.ShapeDtypeStruct((IN_W, D_MODEL), F32),
                   jax.ShapeDtypeStruct((SMALL_ROWS, D_MODEL), F32)),
        in_specs=[tile(D_MODEL), whole(1, D_MODEL), tile(D_MODEL), tile(ATTN_W), tile(KV_W), tile(KV_W), halo, halo,
                  halo_next, halo_next,
                  tile(ATTN_W), tile(CONV_W), tile(CONV_W), tile(CONV_W), next8, ch, CONV_SPEC,
                  pl.BlockSpec((IN_W, D_MODEL), lambda i: (0, 0), pipeline_mode=pl.Buffered(1)),
                  tile(128), tile(128), tile(128)],
        out_specs=(tile(D_MODEL), pl.BlockSpec(memory_space=pl.ANY), whole(SMALL_ROWS, D_MODEL)),
        scratch_shapes=[pltpu.VMEM((tb, IN_W), BF16), pltpu.VMEM((IN_W, D_MODEL), F32)],
        compiler_params=_params(("arbitrary",)),
    )(x, norm_g, dh2, dq, dk, dv, dkh, dvh, dkh, dvh, dga, db, dgc, dcv, dcv, rest, conv_w, w_in_t,
      rope_c, rope_a, rope_b)


def _adamw_step(w, g, m, v):
    m2 = ADAM_B1 * m + (1.0 - ADAM_B1) * g
    v2 = ADAM_B2 * v + (1.0 - ADAM_B2) * jnp.square(g)
    m_hat = m2 / (1.0 - ADAM_B1 ** ADAM_STEP)
    v_hat = v2 / (1.0 - ADAM_B2 ** ADAM_STEP)
    return -ADAM_LR * (m_hat / (jnp.sqrt(v_hat) + ADAM_EPS) + ADAM_WD * w), m2, v2


def _adamw_weights(groups):
    steps = 4

    def body(*refs):
        ins, outs = refs[:4 * len(groups)], refs[4 * len(groups):]
        for k in range(len(groups)):
            res = _adamw_step(*(r[...] for r in ins[4 * k:4 * k + 4]))
            for o_ref, val in zip(outs[3 * k:3 * k + 3], res):
                o_ref[...] = val

    in_specs, out_specs, out_shape = [], [], []
    for w, _, _, _ in groups:
        rows, cols = w.shape
        spec = pl.BlockSpec((rows // steps, cols), lambda i: (i, 0))
        in_specs += [spec] * 4
        out_specs += [spec] * 3
        out_shape += [jax.ShapeDtypeStruct((rows, cols), F32)] * 3
    flat = pl.pallas_call(
        body, name="adamw_weights", grid=(steps,), out_shape=tuple(out_shape), in_specs=in_specs,
        out_specs=tuple(out_specs), compiler_params=_params(("arbitrary",)),
    )(*[a for grp in groups for a in grp])
    return [flat[3 * k:3 * k + 3] for k in range(len(groups))]


def _adamw_small(chip, small, params, m, v):
    def body(chip_ref, small_ref, conv_ref, *refs):
        ins, outs = refs[:12], refs[12:]
        outs[0][...] = jnp.sum(small_ref[6:7, :], axis=-1, keepdims=True)
        grads = (small_ref[0:1, :], small_ref[1:2, :], conv_ref[2:5, :], small_ref[5:6, 0:8])
        for k, g in enumerate(grads):
            outs[1 + k][...] = g
            res = _adamw_step(ins[k][...], g, ins[4 + k][...], ins[8 + k][...])
            for n, val in enumerate(res):
                outs[5 + 4 * n + k][...] = val

    full = lambda a: pl.BlockSpec(a.shape, lambda i, c: (0,) * len(a.shape))
    shapes = [jax.ShapeDtypeStruct(p.shape, F32) for p in params]
    outs = [jax.ShapeDtypeStruct((1, 1), F32)] + shapes * 4
    flat = pl.pallas_call(
        body, name="adamw_small",
        grid_spec=pltpu.PrefetchScalarGridSpec(
            num_scalar_prefetch=1, grid=(1,),
            in_specs=[full(small), pl.BlockSpec((SMALL_ROWS, 128), lambda i, c: (0, c[0]))]
            + [full(a) for a in (*params, *m, *v)],
            out_specs=tuple(full(s) for s in outs)),
        out_shape=tuple(outs), compiler_params=_params(("arbitrary",)),
    )(chip, small, small, *params, *m, *v)
    return flat[0], flat[1:5], [flat[5 + 4 * n:9 + 4 * n] for n in range(3)]


def kernel(x, norm_g, w_in, sinks, conv_w, w_out, final_g, loss_target, m_norm_g, m_w_in, m_sinks, m_conv_w, m_w_out, m_final_g, v_norm_g, v_w_in, v_sinks, v_conv_w, v_w_out, v_final_g):
    seq = x.shape[1]
    x2 = x.reshape(seq, D_MODEL)
    tgt = loss_target.reshape(seq, D_MODEL)
    ng = norm_g.reshape(1, D_MODEL)
    fg = final_g.reshape(1, D_MODEL)
    chip = 2 * lax.axis_index("x") + lax.axis_index("y")

    conv_w8 = jnp.zeros((8, 128), F32).at[0:3].set(conv_w)
    w_in_full = _gather_w_in(w_in.T).reshape(IN_W, D_MODEL)

    q, kd, vd, rest, rope_c, rope_a, rope_b, wo_all, cw_all = _fwd_proj(x2, ng, w_in_full, w_out, conv_w8)
    w_out_full = wo_all.reshape(D_MODEL, D_MODEL)
    attn, sinkw, probs, gates, dh2, g_wo, g_wo_b, small_f = _fwd_mix(
        x2, q, kd, vd, rest, sinks, cw_all, w_out_full, fg, tgt)
    out_blocks = lambda t: t.reshape(N_CHIPS, W_OUT_BLK, D_MODEL)
    dq, dk, dv, dkh, dvh, dga, db, dgc, dcv, small_m, grad_w_out = _bwd_mix(
        dh2, q, kd, vd, attn, sinkw, probs, gates, rest, cw_all, w_out_full, rope_c, rope_a, rope_b,
        out_blocks(g_wo), out_blocks(g_wo_b))
    grad_x, g_wi, small_p = _bwd_proj(x2, ng, dh2, dq, dk, dv, dkh, dvh, dga, db, dgc, dcv, rest, cw_all,
                                      w_in_full, rope_c, rope_a, rope_b)

    g_in_blocks = g_wi.reshape(N_CHIPS, W_IN_BLK, D_MODEL)
    grad_w_in_t, small = _reduce_grads(g_in_blocks, small_f, small_m, small_p)

    (upd_wi, upd_wo) = _adamw_weights([(w_in.T, grad_w_in_t, m_w_in.T, v_w_in.T),
                                       (w_out, grad_w_out, m_w_out, v_w_out)])
    row = lambda t: t.reshape(1, -1)
    loss, grads_s, upd_s = _adamw_small(
        chip.reshape(1), small, (ng, fg, conv_w, row(sinks)),
        (row(m_norm_g), row(m_final_g), m_conv_w, row(m_sinks)),
        (row(v_norm_g), row(v_final_g), v_conv_w, row(v_sinks)))

    def named(ng_, fg_, cw_, sk_, wi_t, wo_):
        return [ng_.reshape(D_MODEL), wi_t.T, sk_.reshape(8), cw_, wo_, fg_.reshape(D_MODEL)]

    g_named = named(*grads_s, grad_w_in_t, grad_w_out)
    out = [loss.reshape(()), grad_x.reshape(1, seq, D_MODEL)] + g_named
    for n in range(3):
        out += named(*upd_s[n], upd_wi[n], upd_wo[n])
    return tuple(out)
```

```python
import jax
import jax.numpy as jnp
from jax import lax
from jax.experimental import pallas as pl
from jax.experimental.pallas import tpu as pltpu

F32 = jnp.float32
BF16 = jnp.bfloat16

D_MODEL = 1024
HEAD_DIM = 64
ATTN_W = 512
KV_W = 128
CONV_W = 512
IN_W = 3328
REST_W = IN_W - ATTN_W - 2 * KV_W
BLOCK = 128
ROT_DIM = 16
ROPE_THETA = 500000.0
EPS = 1e-5
SCALE = 0.125
NEG = -1e30

N_CHIPS = 4
W_IN_BLK = IN_W // N_CHIPS
W_OUT_BLK = D_MODEL // N_CHIPS

ADAM_LR = 0.001
ADAM_B1 = 0.9
ADAM_B2 = 0.999
ADAM_EPS = 1e-08
ADAM_WD = 0.01
ADAM_STEP = 10

VMEM_LIMIT = 60 * 1024 * 1024
T_PROJ = 512
T_FMIX = 512
T_MIX = 512
SMALL_ROWS = 8
MESH = pl.DeviceIdType.MESH

_NT = (((1,), (1,)), ((), ()))
_TN = (((0,), (0,)), ((), ()))


def _params(sem=None):
    kw = dict(vmem_limit_bytes=VMEM_LIMIT)
    if sem is not None:
        kw["dimension_semantics"] = sem
    return pltpu.CompilerParams(**kw)


def _sigmoid(t):
    return 1.0 / (1.0 + jnp.exp(-t))


def _shift_down(t, prev8, k):
    rolled = pltpu.roll(t, k, 0)
    row = lax.broadcasted_iota(jnp.int32, t.shape, 0)
    for j in range(k):
        rolled = jnp.where(row == j, prev8[8 - k + j:8 - k + j + 1, :], rolled)
    return rolled


def _shift_up(t, next8, k):
    n = t.shape[0]
    rolled = pltpu.roll(t, n - k, 0)
    row = lax.broadcasted_iota(jnp.int32, t.shape, 0)
    for j in range(k):
        rolled = jnp.where(row == n - k + j, next8[j:j + 1, :], rolled)
    return rolled


def _rope(t, c, a, b):
    w = t.shape[1]
    reps = w // 128
    if reps > 1:
        c, a, b = (jnp.concatenate([z] * reps, axis=1) for z in (c, a, b))
    return t * c + pltpu.roll(t, w - 8, 1) * a + pltpu.roll(t, 8, 1) * b


def _lane_lo(shape):
    return lax.broadcasted_iota(jnp.int32, shape, 1) < HEAD_DIM


def _stack_heads(t, g):
    lo = _lane_lo((BLOCK, 128))
    parts = []
    for hh in range(4):
        pair = t[:, 256 * g + 128 * (hh // 2):256 * g + 128 * (hh // 2) + 128]
        keep = lo if hh % 2 == 0 else jnp.logical_not(lo)
        parts.append(jnp.where(keep, pair, jnp.zeros_like(pair)))
    return jnp.concatenate(parts, axis=0)


def _unstack_pair(o, pp):
    lo = _lane_lo((BLOCK, 128))
    return jnp.where(lo, o[256 * pp:256 * pp + 128], o[256 * pp + 128:256 * pp + 256])


def _band_masks(has_prev):
    r = lax.broadcasted_iota(jnp.int32, (4 * BLOCK, 2 * BLOCK), 0) % BLOCK
    kj = lax.broadcasted_iota(jnp.int32, (4 * BLOCK, 2 * BLOCK), 1)
    cur = (kj >= BLOCK) & (kj - BLOCK <= r)
    prev = (kj < BLOCK) & (kj > r)
    return cur | (prev & has_prev), cur | prev


def _sink_col(sinks_ref, g):
    r = lax.broadcasted_iota(jnp.int32, (4 * BLOCK, 1), 0) // BLOCK
    col = jnp.full((4 * BLOCK, 1), sinks_ref[4 * g + 3], F32)
    for hh in range(3):
        col = jnp.where(r == hh, sinks_ref[4 * g + hh], col)
    return col


def _probs(qs, kd, sink_col, mask):
    return _softmax(lax.dot_general(qs, kd, _NT, preferred_element_type=F32), sink_col, mask)


def _softmax(s, sink_col, mask):
    s = jnp.where(mask, s, NEG)
    m = jnp.maximum(jnp.max(s, axis=-1, keepdims=True), sink_col)
    p = jnp.exp(s - m)
    es = jnp.exp(sink_col - m)
    inv = 1.0 / (jnp.sum(p, axis=-1, keepdims=True) + es)
    return p * inv, es * inv


def _key_windows(i, nsub, kd_ref, vd_ref, kdp_ref, vdp_ref):
    mask_first, mask_rest = _band_masks(i > 0)
    out = []
    for sb in range(nsub):
        rows = slice(BLOCK * sb, BLOCK * (sb + 1))
        if sb == 0:
            kk = jnp.concatenate([kdp_ref[...], kd_ref[rows, :]], axis=0)
            vv = jnp.concatenate([vdp_ref[...], vd_ref[rows, :]], axis=0)
            out.append((rows, kk, vv, mask_first))
        else:
            both = slice(BLOCK * (sb - 1), BLOCK * (sb + 1))
            out.append((rows, kd_ref[both, :], vd_ref[both, :], mask_rest))
    return out


def _gather_w_in(w_in_t):
    hi = W_IN_BLK // 2
    qr = hi // 2

    def body(wi_ref, wi_all, send_sems, recv_sems):
        x, y, c = lax.axis_index("x"), lax.axis_index("y"), lax.axis_index("c")
        me, sibling = (x, y), (x, y, 1 - c)
        xnb, ynb, diag = (1 - x, y), (x, 1 - y), (1 - x, 1 - y)

        wi_all[2 * x + y] = wi_ref[...].astype(BF16)

        def copy(k, chip, half, quarter, to):
            r = wi_all.at[2 * chip[0] + chip[1], pl.ds(half * hi + quarter * qr, qr)]
            return pltpu.make_async_remote_copy(src_ref=r, dst_ref=r, send_sem=send_sems.at[k],
                                                recv_sem=recv_sems.at[k], device_id=to, device_id_type=MESH)

        plan = [(0, xnb, 0), (1, ynb, 1), (2, xnb, 1), (3, ynb, 0)]
        sent = [copy(k, me, c, quarter, (*nb, c)) for k, nb, quarter in plan]
        for cp in sent:
            cp.start()
        arrivals = [(0, xnb, 0), (1, ynb, 1), (2, xnb, 1), (3, ynb, 0), (4, diag, 0), (5, diag, 1)]
        relay = {0: (4, ynb), 1: (5, xnb)}
        for k, chip, quarter in arrivals:
            copy(k, chip, c, quarter, (x, y, c)).wait_recv()
            if k in relay:
                sent.append(copy(relay[k][0], chip, c, quarter, (*relay[k][1], c)))
                sent[-1].start()
            sent.append(copy(6 + k, chip, c, quarter, sibling))
            sent[-1].start()
        for k, chip, quarter in arrivals:
            copy(6 + k, chip, 1 - c, quarter, (x, y, c)).wait_recv()
        for cp in sent:
            cp.wait_send()

    vmem = pl.BlockSpec(memory_space=pltpu.VMEM)
    return pl.pallas_call(
        body, name="gather_w_in",
        out_shape=jax.ShapeDtypeStruct((N_CHIPS, W_IN_BLK, D_MODEL), BF16),
        in_specs=[vmem], out_specs=vmem,
        scratch_shapes=[pltpu.SemaphoreType.DMA((12,)), pltpu.SemaphoreType.DMA((12,))],
        compiler_params=_params(),
    )(w_in_t)


def _reduce_grads(g_in, *smalls):
    hi = W_IN_BLK // 2
    qr = hi // 2
    q0, q1 = slice(0, qr), slice(qr, hi)

    def body(gi_hbm, s0_ref, s1_ref, gi_out, small_out,
             mine_i, sib_i, out_i, ici_i, small_in, small_ref, send_sems, recv_sems, local_sems):
        x, y, c = lax.axis_index("x"), lax.axis_index("y"), lax.axis_index("c")
        my_dev = 4 * x + 2 * y + c
        sibling = (x, y, 1 - c)
        xnb, ynb = (1 - x, y, c), (x, 1 - y, c)
        order = [(1 - x, 1 - y), (1 - x, y), (x, 1 - y), (x, y)]

        def remote(k, src, dst, to):
            return pltpu.make_async_remote_copy(src_ref=src, dst_ref=dst, send_sem=send_sems.at[k],
                                                recv_sem=recv_sems.at[k], device_id=to, device_id_type=MESH)

        small_ref[...] = s0_ref[...] + s1_ref[...]
        small_cps = []
        for f in range(1, 8):
            fx, fy, fc = f >> 2, (f >> 1) & 1, f & 1
            small_cps.append(remote(11 + f - 1, small_ref, small_in.at[f - 1], (x ^ fx, y ^ fy, c ^ fc)))
        for cp in small_cps:
            cp.start()

        own, to_sib = [], []
        for n, chip in enumerate(order):
            j = 2 * chip[0] + chip[1]
            own.append(pltpu.make_async_copy(gi_hbm.at[j, pl.ds(c * hi, hi)], mine_i.at[n], local_sems.at[n]))
            to_sib.append(remote(6 + n, gi_hbm.at[j, pl.ds((1 - c) * hi, hi)], sib_i.at[n], sibling))
            own[-1].start()
            to_sib[-1].start()

        def pair_sum(n):
            own[n].wait()
            to_sib[n].wait_recv()
            return mine_i[n] + sib_i[n]

        ici = [remote(k, out_i.at[k], ici_i.at[k], xnb if k % 2 == 0 else ynb) for k in range(6)]

        def send(k, rows_f32):
            out_i[k] = rows_f32.astype(BF16)
            ici[k].start()

        p_diag = pair_sum(0)
        send(0, p_diag[q0])
        send(1, p_diag[q1])
        p_x = pair_sum(1)
        send(2, p_x[q0])
        p_y = pair_sum(2)
        send(3, p_y[q1])
        ici[0].wait_recv()
        send(5, p_y[q0] + ici_i[0].astype(F32))
        ici[1].wait_recv()
        send(4, p_x[q1] + ici_i[1].astype(F32))
        p_mine = pair_sum(3)
        for k in range(2, 6):
            ici[k].wait_recv()
        gi_out[pl.ds(c * hi, qr), :] = p_mine[q0] + ici_i[2].astype(F32) + ici_i[5].astype(F32)
        gi_out[pl.ds(c * hi + qr, qr), :] = p_mine[q1] + ici_i[4].astype(F32) + ici_i[3].astype(F32)

        swap = [remote(10, gi_out.at[pl.ds(c * hi, hi)], gi_out.at[pl.ds(c * hi, hi)], sibling)]
        for cp in swap:
            cp.start()

        for cp in small_cps:
            cp.wait_recv()
        total = jnp.zeros((SMALL_ROWS, D_MODEL), F32)
        for d in range(8):
            slot = jnp.maximum((d ^ my_dev) - 1, 0)
            total = total + jnp.where(d == my_dev, small_ref[...], small_in[slot])
        small_out[...] = total

        remote(10, gi_out.at[pl.ds((1 - c) * hi, hi)], gi_out.at[pl.ds((1 - c) * hi, hi)], sibling).wait_recv()
        for cp in to_sib + ici + swap + small_cps:
            cp.wait_send()

    vmem = pl.BlockSpec(memory_space=pltpu.VMEM)
    anyspace = pl.BlockSpec(memory_space=pl.ANY)
    return pl.pallas_call(
        body, name="reduce_grads",
        out_shape=(jax.ShapeDtypeStruct((W_IN_BLK, D_MODEL), F32), jax.ShapeDtypeStruct((SMALL_ROWS, D_MODEL), F32)),
        in_specs=[anyspace, vmem, vmem], out_specs=(vmem, vmem),
        scratch_shapes=[pltpu.VMEM((N_CHIPS, hi, D_MODEL), F32), pltpu.VMEM((N_CHIPS, hi, D_MODEL), F32),
                        pltpu.VMEM((6, qr, D_MODEL), BF16), pltpu.VMEM((6, qr, D_MODEL), BF16),
                        pltpu.VMEM((7, SMALL_ROWS, D_MODEL), F32), pltpu.VMEM((SMALL_ROWS, D_MODEL), F32),
                        pltpu.SemaphoreType.DMA((18,)), pltpu.SemaphoreType.DMA((18,)),
                        pltpu.SemaphoreType.DMA((4,))],
        compiler_params=_params(),
    )(g_in, *smalls)


def _fwd_proj(x, norm_g, w_in_t, w_out, conv_w8):
    seq = x.shape[0]
    nt = seq // T_PROJ
    lane = jnp.arange(128, dtype=jnp.int32) % HEAD_DIM
    inv_freq = ROPE_THETA ** (-(2 * (lane % 8)).astype(F32) / ROT_DIM)
    inv_freq = jnp.where(lane < ROT_DIM, inv_freq, 0.0).reshape(1, 128)
    in_tile = jnp.arange(T_PROJ, dtype=jnp.int32).astype(F32)[:, None] * inv_freq
    cos_in, sin_in = jnp.cos(in_tile), jnp.sin(in_tile)
    start = jnp.repeat((jnp.arange(nt, dtype=jnp.int32) * T_PROJ).astype(F32), 8)[:, None] * inv_freq
    cos_st, sin_st = jnp.cos(start), jnp.sin(start)

    def body(x_ref, g_ref, w_ref, cs_ref, ss_ref, ci_ref, si_ref, wo_ref, cw_ref,
             q_ref, kd_ref, vd_ref, rest_ref, c_ref, a_ref, b_ref, wo_all, cw_all,
             wo_stage, send_sems, recv_sems, local_sems):
        i = pl.program_id(0)
        mx, my, mc = lax.axis_index("x"), lax.axis_index("y"), lax.axis_index("c")
        chips = [(1 - mx, my), (mx, 1 - my), (1 - mx, 1 - my)]

        def gather(blocks):
            cps = []
            for k, chip in enumerate(chips):
                for n, (src, dst) in enumerate(((wo_stage, wo_all), (cw_ref, cw_all))):
                    cps.append(pltpu.make_async_remote_copy(
                        src_ref=src, dst_ref=dst.at[blocks[k]], send_sem=send_sems.at[2 * k + n],
                        recv_sem=recv_sems.at[2 * k + n], device_id=(*chip, mc), device_id_type=MESH))
            return cps

        me = 2 * mx + my
        own = [pltpu.make_async_copy(wo_stage, wo_all.at[me], local_sems.at[0]),
               pltpu.make_async_copy(cw_ref, cw_all.at[me], local_sems.at[1])]

        @pl.when(i == 0)
        def _():
            wo_stage[...] = wo_ref[...].astype(BF16)
            for cp in own + gather([me] * 3):
                cp.start()

        xf = x_ref[...]
        r1 = lax.rsqrt(jnp.mean(xf * xf, axis=-1, keepdims=True) + EPS)
        xn = (xf * r1 * g_ref[...]).astype(BF16)
        cs, ss = cs_ref[0:1, :], ss_ref[0:1, :]
        c = cs * ci_ref[...] - ss * si_ref[...]
        sin = ss * ci_ref[...] + cs * si_ref[...]
        j = lax.broadcasted_iota(jnp.int32, (T_PROJ, 128), 1) % HEAD_DIM
        a = jnp.where(j < 8, -sin, 0.0)
        b = jnp.where(j >= 8, sin, 0.0)
        c_ref[...], a_ref[...], b_ref[...] = c, a, b
        proj = lambda lo_c, w: lax.dot_general(xn, w_ref[lo_c:lo_c + w, :], _NT, preferred_element_type=F32)
        q_ref[...] = (_rope(proj(0, ATTN_W), c, a, b) * SCALE).astype(BF16)
        kv = proj(ATTN_W, 2 * KV_W)
        k = _rope(kv[:, 0:KV_W], c, a, b)
        v = kv[:, KV_W:2 * KV_W]
        lo = _lane_lo(k.shape)
        for t, ref in ((k, kd_ref), (v, vd_ref)):
            sw = pltpu.roll(t, HEAD_DIM, 1)
            ref[:, 0:128] = jnp.where(lo, t, sw).astype(BF16)
            ref[:, 128:256] = jnp.where(lo, sw, t).astype(BF16)
        for n in range(REST_W // 512):
            rest_ref[:, 512 * n:512 * (n + 1)] = proj(ATTN_W + 2 * KV_W + 512 * n, 512)

        @pl.when(i == nt - 1)
        def _():
            sent = gather([me] * 3)
            for cp in gather([2 * chip[0] + chip[1] for chip in chips]):
                cp.wait_recv()
            for cp in sent:
                cp.wait_send()
            for cp in own:
                cp.wait()

    tile = lambda w: pl.BlockSpec((T_PROJ, w), lambda i: (i, 0))
    whole = lambda r, w: pl.BlockSpec((r, w), lambda i: (0, 0))
    vmem = pl.BlockSpec(memory_space=pltpu.VMEM)
    hbm = pl.BlockSpec(memory_space=pl.ANY)
    return pl.pallas_call(
        body, name="fwd_proj", grid=(nt,),
        out_shape=(jax.ShapeDtypeStruct((seq, ATTN_W), BF16), jax.ShapeDtypeStruct((seq, 2 * KV_W), BF16),
                   jax.ShapeDtypeStruct((seq, 2 * KV_W), BF16), jax.ShapeDtypeStruct((seq, REST_W), F32))
        + (jax.ShapeDtypeStruct((seq, 128), F32),) * 3
        + (jax.ShapeDtypeStruct((N_CHIPS, W_OUT_BLK, D_MODEL), BF16), jax.ShapeDtypeStruct((N_CHIPS, 8, 128), F32)),
        in_specs=[tile(D_MODEL), whole(1, D_MODEL), whole(IN_W, D_MODEL), pl.BlockSpec((8, 128), lambda i: (i, 0)),
                  pl.BlockSpec((8, 128), lambda i: (i, 0)), whole(T_PROJ, 128), whole(T_PROJ, 128), vmem, vmem],
        out_specs=(tile(ATTN_W), tile(2 * KV_W), tile(2 * KV_W), tile(REST_W), tile(128), tile(128), tile(128),
                   hbm, hbm),
        scratch_shapes=[pltpu.VMEM((W_OUT_BLK, D_MODEL), BF16), pltpu.SemaphoreType.DMA((6,)),
                        pltpu.SemaphoreType.DMA((6,)), pltpu.SemaphoreType.DMA((2,))],
        compiler_params=_params(("arbitrary",)),
    )(x, norm_g, w_in_t, cos_st, sin_st, cos_in, sin_in, w_out, conv_w8)


CONV_SPEC = pl.BlockSpec((N_CHIPS, 8, 128), lambda i: (0, 0, 0))


def _conv_rows(cw_ref):
    return jnp.concatenate([cw_ref[j] for j in range(N_CHIPS)], axis=1)


def _conv_parts(rest_ref, prev_ref, cw_ref, first):
    u = rest_ref[:, 1024:1536] * rest_ref[:, 1536:2048]
    up = prev_ref[:, 1024:1536] * prev_ref[:, 1536:2048]
    up = jnp.where(first, jnp.zeros_like(up), up)
    um1 = _shift_down(u, up, 1)
    um2 = _shift_down(u, up, 2)
    cw = _conv_rows(cw_ref)
    cv = cw[0:1, :] * um2 + cw[1:2, :] * um1 + cw[2:3, :] * u
    return u, um1, um2, cv


def _fwd_mix(x, q, kd, vd, rest, sinks, conv_w, w_out, final_g, target):
    seq = x.shape[0]
    nt = seq // T_FMIX
    nsub = T_FMIX // BLOCK

    def body(sinks_ref, x_ref, q_ref, kd_ref, vd_ref, kdp_ref, vdp_ref, rest_ref, restp_ref, cw_ref, wo_ref,
             fg_ref, tgt_ref, prob_ref, do_ref, prod_ref, dga_ref, db_ref, dgc_ref, dcv_ref, dh2_ref,
             gwo_hbm, gwob_hbm, small_ref,
             pmix_ref, pdh2_ref, gwo_ref, gate_ref, dmix_ref, dsink_ref):
        i = pl.program_id(0)
        attn_ref, sinkw_ref = prod_ref, dcv_ref

        @pl.when(i == 0)
        def _():
            small_ref[...] = jnp.zeros_like(small_ref)
            dsink_ref[...] = jnp.zeros_like(dsink_ref)
            gwo_ref[...] = jnp.zeros_like(gwo_ref)
            pmix_ref[...] = jnp.zeros_like(pmix_ref)
            pdh2_ref[...] = jnp.zeros_like(pdh2_ref)

        chains = []
        for rows, kk, vv, mask in _key_windows(i, nsub, kd_ref, vd_ref, kdp_ref, vdp_ref):
            qt = q_ref[rows, :]
            for g in range(2):
                kg = kk[:, 128 * g:128 * (g + 1)]
                chains.append(dict(g=g, rows=rows, mask=mask, vg=vv[:, 128 * g:128 * (g + 1)],
                                   s=lax.dot_general(_stack_heads(qt, g), kg, _NT, preferred_element_type=F32)))
        gwo_ref[...] += lax.dot_general(pmix_ref[...], pdh2_ref[...], _TN, preferred_element_type=F32)
        for ch in chains:
            ch["prob"], ch["psink"] = _softmax(ch.pop("s"), _sink_col(sinks_ref, ch["g"]), ch["mask"])
        for k, ch in enumerate(chains):
            prob = ch["prob"].astype(BF16)
            prob_ref[k] = prob
            o = jnp.dot(prob, ch["vg"], preferred_element_type=F32)
            ow = o * ch["psink"]
            for pp in range(2):
                lanes = slice(256 * ch["g"] + 128 * pp, 256 * ch["g"] + 128 * (pp + 1))
                attn_ref[ch["rows"], lanes] = _unstack_pair(o, pp)
                sinkw_ref[ch["rows"], lanes] = _unstack_pair(ow, pp)

        def silu_parts(t, lo_c):
            sg = _sigmoid(t)
            silu = t * sg
            gate_ref[:, lo_c:lo_c + 512] = silu
            gate_ref[:, lo_c + 512:lo_c + 1024] = sg * (1.0 + t * (1.0 - sg))
            return silu

        pmix_ref[:, 0:ATTN_W] = (attn_ref[...] * silu_parts(rest_ref[:, 0:512], 0)).astype(BF16)
        u, um1, um2, cv = _conv_parts(rest_ref, restp_ref, cw_ref, i == 0)
        pmix_ref[:, ATTN_W:] = (rest_ref[:, 512:1024] * cv * silu_parts(rest_ref[:, 2048:2560], 1024)).astype(BF16)

        h2 = x_ref[...] + jnp.dot(pmix_ref[...], wo_ref[...], preferred_element_type=F32)
        r2 = lax.rsqrt(jnp.mean(h2 * h2, axis=-1, keepdims=True) + EPS)
        n2 = h2 * r2
        err = n2 * fg_ref[...] - tgt_ref[...]
        dy = err * (1.0 / D_MODEL)
        small_ref[6:7, :] += jnp.sum(err * err, axis=0, keepdims=True) * (0.5 / D_MODEL)
        small_ref[1:2, :] += jnp.sum(dy * n2, axis=0, keepdims=True)
        dn = dy * fg_ref[...]
        dh2 = r2 * (dn - n2 * jnp.mean(dn * n2, axis=-1, keepdims=True))
        dh2_ref[...] = dh2
        pdh2_ref[...] = dh2.astype(BF16)

        dmix_ref[...] = lax.dot_general(pdh2_ref[...], wo_ref[...], _NT, preferred_element_type=F32)
        dma = dmix_ref[:, 0:ATTN_W]
        dga_ref[...] = (dma * attn_ref[...] * gate_ref[:, 512:1024]).astype(BF16)
        d_attn = dma * gate_ref[:, 0:512]
        do_ref[...] = d_attn.astype(BF16)
        prod_ref[...] = d_attn * attn_ref[...]
        dsink_ref[0:1, :] += jnp.sum(d_attn * sinkw_ref[...], axis=0, keepdims=True)
        bg = rest_ref[:, 512:1024]
        dmc = dmix_ref[:, ATTN_W:]
        t1 = dmc * gate_ref[:, 1024:1536]
        db_ref[...] = (t1 * cv).astype(BF16)
        dcv = t1 * bg
        dcv_ref[...] = dcv
        dgc_ref[...] = (dmc * (bg * cv) * gate_ref[:, 1536:2048]).astype(BF16)
        small_ref[2:3, 0:CONV_W] += jnp.sum(dcv * um2, axis=0, keepdims=True)
        small_ref[3:4, 0:CONV_W] += jnp.sum(dcv * um1, axis=0, keepdims=True)
        small_ref[4:5, 0:CONV_W] += jnp.sum(dcv * u, axis=0, keepdims=True)

        @pl.when(i == nt - 1)
        def _():
            head = lax.broadcasted_iota(jnp.int32, (1, ATTN_W), 1) // HEAD_DIM
            for h in range(8):
                tot = jnp.sum(jnp.where(head == h, dsink_ref[0:1, :], 0.0), axis=-1, keepdims=True)
                small_ref[5:6, h:h + 1] = -tot
            gwo_ref[...] += lax.dot_general(pmix_ref[...], pdh2_ref[...], _TN, preferred_element_type=F32)
            pltpu.sync_copy(gwo_ref, gwo_hbm)
            for n in range(D_MODEL // T_FMIX):
                slab = slice(T_FMIX * n, T_FMIX * (n + 1))
                pmix_ref[...] = gwo_ref[slab, :].astype(BF16)
                pltpu.sync_copy(pmix_ref, gwob_hbm.at[slab])

    tile = lambda w: pl.BlockSpec((T_FMIX, w), lambda i: (i, 0))
    whole = lambda r, w: pl.BlockSpec((r, w), lambda i: (0, 0))
    prev_blk = pl.BlockSpec((BLOCK, 2 * KV_W), lambda i: (jnp.maximum(i * nsub - 1, 0), 0))
    prev8 = pl.BlockSpec((8, REST_W), lambda i: (jnp.maximum(i * (T_FMIX // 8) - 1, 0), 0))
    bf = lambda w: jax.ShapeDtypeStruct((seq, w), BF16)
    f32 = lambda w: jax.ShapeDtypeStruct((seq, w), F32)
    return pl.pallas_call(
        body, name="fwd_mix", grid=(nt,),
        out_shape=(jax.ShapeDtypeStruct((2 * seq // BLOCK, 4 * BLOCK, 2 * BLOCK), BF16),
                   bf(ATTN_W), f32(ATTN_W), bf(ATTN_W), bf(CONV_W), bf(CONV_W), f32(CONV_W), f32(D_MODEL),
                   jax.ShapeDtypeStruct((D_MODEL, D_MODEL), F32), jax.ShapeDtypeStruct((D_MODEL, D_MODEL), BF16),
                   jax.ShapeDtypeStruct((SMALL_ROWS, D_MODEL), F32)),
        in_specs=[pl.BlockSpec(memory_space=pltpu.SMEM), tile(D_MODEL), tile(ATTN_W), tile(2 * KV_W), tile(2 * KV_W),
                  prev_blk, prev_blk, tile(REST_W), prev8, CONV_SPEC,
                  pl.BlockSpec((D_MODEL, D_MODEL), lambda i: (0, 0), pipeline_mode=pl.Buffered(1)),
                  whole(1, D_MODEL), tile(D_MODEL)],
        out_specs=(pl.BlockSpec((2 * nsub, 4 * BLOCK, 2 * BLOCK), lambda i: (i, 0, 0)),
                   tile(ATTN_W), tile(ATTN_W), tile(ATTN_W), tile(CONV_W), tile(CONV_W), tile(CONV_W), tile(D_MODEL),
                   pl.BlockSpec(memory_space=pl.ANY), pl.BlockSpec(memory_space=pl.ANY), whole(SMALL_ROWS, D_MODEL)),
        scratch_shapes=[pltpu.VMEM((T_FMIX, D_MODEL), BF16)] * 2 + [
            pltpu.VMEM((D_MODEL, D_MODEL), F32), pltpu.VMEM((T_FMIX, 4 * 512), F32), pltpu.VMEM((T_FMIX, D_MODEL), F32), pltpu.VMEM((8, ATTN_W), F32)],
        compiler_params=_params(("arbitrary",)),
    )(sinks, x, q, kd, vd, kd, vd, rest, rest, conv_w, w_out, final_g, target)


def _scatter_copies(g_hbm, gb_hbm, mine, land, send_sems, recv_sems, local_sem, half):
    x, y, c = lax.axis_index("x"), lax.axis_index("y"), lax.axis_index("c")
    cps = []
    for f in range(1, 8):
        to = (x ^ (f >> 2), y ^ ((f >> 1) & 1), c ^ (f & 1))
        src = gb_hbm.at[2 * to[0] + to[1], pl.ds(to[2] * half, half)]
        cps.append(pltpu.make_async_remote_copy(src_ref=src, dst_ref=land.at[f - 1], send_sem=send_sems.at[f - 1],
                                                recv_sem=recv_sems.at[f - 1], device_id=to, device_id_type=MESH))
    own = pltpu.make_async_copy(g_hbm.at[2 * x + y, pl.ds(c * half, half)], mine, local_sem)
    return cps, own


def _scatter_finish(cps, own, mine, land, out_hbm, send_sems, recv_sems, local_sem, half):
    x, y, c = lax.axis_index("x"), lax.axis_index("y"), lax.axis_index("c")
    own.wait()
    tot = mine[...]
    for f in range(1, 8):
        cps[f - 1].wait_recv()
        tot = tot + land[f - 1].astype(F32)
    mine[...] = tot

    def swap(rows_of):
        return pltpu.make_async_remote_copy(src_ref=mine, dst_ref=out_hbm.at[pl.ds(rows_of * half, half)],
                                            send_sem=send_sems.at[7], recv_sem=recv_sems.at[7],
                                            device_id=(x, y, 1 - c), device_id_type=MESH)

    keep = pltpu.make_async_copy(mine, out_hbm.at[pl.ds(c * half, half)], local_sem)
    keep.start()
    swap(c).start()
    swap(1 - c).wait_recv()
    keep.wait()
    for cp in cps:
        cp.wait_send()
    swap(c).wait_send()


def _bwd_mix(q, kd, vd, probs, d_attn, prod, rope_c, rope_a, rope_b, g_out, g_out_b):
    seq = q.shape[0]
    nt = seq // T_MIX
    nsub = T_MIX // BLOCK
    ho = W_OUT_BLK // 2

    def body(q_ref, kd_ref, vd_ref, kdp_ref, vdp_ref, prob_ref, do_ref, prod_ref, c_ref, a_ref, b_ref, go_hbm, gob_hbm,
             dq_ref, dk_ref, dv_ref, dkh_ref, dvh_ref, go_out,
             mine_o, land_o, send_sems, recv_sems, local_sems):
        i = pl.program_id(0)
        scatter = (mine_o, land_o, send_sems, recv_sems, local_sems.at[0], ho)

        @pl.when(i == 0)
        def _():
            cps, own = _scatter_copies(go_hbm, gob_hbm, *scatter)
            for cp in cps + [own]:
                cp.start()

        lo = _lane_lo((2 * BLOCK, 128))
        dk_blocks = [None] * (nsub + 1)
        dv_blocks = [None] * (nsub + 1)

        def add(lst, n, val):
            lst[n] = val if lst[n] is None else lst[n] + val

        chains = []
        for rows, kk, vv, _ in _key_windows(i, nsub, kd_ref, vd_ref, kdp_ref, vdp_ref):
            qt = q_ref[rows, :]
            dot = do_ref[rows, :]
            for g in range(2):
                rs = jnp.sum(_stack_heads(prod_ref[rows, :], g), axis=-1, keepdims=True)
                chains.append(dict(g=g, rows=rows, rs=rs, qs=_stack_heads(qt, g), dos=_stack_heads(dot, g),
                                   kg=kk[:, 128 * g:128 * (g + 1)], vg=vv[:, 128 * g:128 * (g + 1)]))
        for ch in chains:
            ch["dp"] = lax.dot_general(ch["dos"], ch["vg"], _NT, preferred_element_type=F32)
        for k, ch in enumerate(chains):
            ch["ds"] = (prob_ref[k].astype(F32) * (ch["dp"] - ch["rs"])).astype(BF16)
        for k, ch in enumerate(chains):
            dqs = jnp.dot(ch["ds"], ch["kg"], preferred_element_type=F32) * SCALE
            c, a, b = c_ref[ch["rows"], :], a_ref[ch["rows"], :], b_ref[ch["rows"], :]
            for pp in range(2):
                lanes = slice(256 * ch["g"] + 128 * pp, 256 * ch["g"] + 128 * (pp + 1))
                dq_ref[ch["rows"], lanes] = _rope(_unstack_pair(dqs, pp), c, -a, -b).astype(BF16)
            dkd = lax.dot_general(ch["ds"], ch["qs"], _TN, preferred_element_type=F32)
            dvd = lax.dot_general(prob_ref[k], ch["dos"], _TN, preferred_element_type=F32)
            ch["dk"] = dkd + pltpu.roll(dkd, HEAD_DIM, 1)
            ch["dv"] = dvd + pltpu.roll(dvd, HEAD_DIM, 1)
        for sb in range(nsub):
            dk2 = jnp.where(lo, chains[2 * sb]["dk"], chains[2 * sb + 1]["dk"])
            dv2 = jnp.where(lo, chains[2 * sb]["dv"], chains[2 * sb + 1]["dv"])
            add(dk_blocks, sb, dk2[0:BLOCK])
            add(dk_blocks, sb + 1, dk2[BLOCK:])
            add(dv_blocks, sb, dv2[0:BLOCK])
            add(dv_blocks, sb + 1, dv2[BLOCK:])
        dkh_ref[0] = dk_blocks[0]
        dvh_ref[0] = dv_blocks[0]
        for sb in range(nsub):
            dk_ref[BLOCK * sb:BLOCK * (sb + 1), :] = dk_blocks[sb + 1]
            dv_ref[BLOCK * sb:BLOCK * (sb + 1), :] = dv_blocks[sb + 1]

        @pl.when(i == nt - 1)
        def _():
            cps, own = _scatter_copies(go_hbm, gob_hbm, *scatter)
            _scatter_finish(cps, own, mine_o, land_o, go_out, send_sems, recv_sems, local_sems.at[1], ho)

    tile = lambda w: pl.BlockSpec((T_MIX, w), lambda i: (i, 0))
    prev_blk = pl.BlockSpec((BLOCK, 2 * KV_W), lambda i: (jnp.maximum(i * nsub - 1, 0), 0))
    halo = pl.BlockSpec((1, BLOCK, KV_W), lambda i: (i, 0, 0))
    hbm = pl.BlockSpec(memory_space=pl.ANY)
    f32 = lambda w: jax.ShapeDtypeStruct((seq, w), F32)
    return pl.pallas_call(
        body, name="bwd_mix", grid=(nt,),
        out_shape=(jax.ShapeDtypeStruct((seq, ATTN_W), BF16), f32(KV_W), f32(KV_W),
                   jax.ShapeDtypeStruct((nt, BLOCK, KV_W), F32), jax.ShapeDtypeStruct((nt, BLOCK, KV_W), F32),
                   jax.ShapeDtypeStruct((W_OUT_BLK, D_MODEL), F32)),
        in_specs=[tile(ATTN_W), tile(2 * KV_W), tile(2 * KV_W), prev_blk, prev_blk,
                  pl.BlockSpec((2 * nsub, 4 * BLOCK, 2 * BLOCK), lambda i: (i, 0, 0)), tile(ATTN_W), tile(ATTN_W),
                  tile(128), tile(128), tile(128), hbm, hbm],
        out_specs=(tile(ATTN_W), tile(KV_W), tile(KV_W), halo, halo, hbm),
        scratch_shapes=[pltpu.VMEM((ho, D_MODEL), F32), pltpu.VMEM((7, ho, D_MODEL), BF16),
                        pltpu.SemaphoreType.DMA((8,)), pltpu.SemaphoreType.DMA((8,)), pltpu.SemaphoreType.DMA((2,))],
        compiler_params=_params(("arbitrary",)),
    )(q, kd, vd, kd, vd, probs, d_attn, prod, rope_c, rope_a, rope_b, g_out, g_out_b)


def _bwd_proj(x, norm_g, dh2, dq, dk, dv, dkh, dvh, dga, db, dgc, dcv, rest, conv_w, w_in_t, rope_c, rope_a, rope_b):
    seq = x.shape[0]
    tb = T_PROJ
    per = tb // T_MIX
    nt = seq // tb

    def body(x_ref, g_ref, dh2_ref, dq_ref, dk_ref, dv_ref, dkh_ref, dvh_ref, dkn_ref, dvn_ref, dga_ref, db_ref,
             dgc_ref, dcv_ref, dcvn_ref, ch_ref, cw_ref, w_ref, c_ref, a_ref, b_ref, gx_ref, gw_hbm, small_ref,
             dp_ref, acc_ref):
        i = pl.program_id(0)

        @pl.when(i == 0)
        def _():
            small_ref[...] = jnp.zeros_like(small_ref)
            acc_ref[...] = jnp.zeros_like(acc_ref)

        last = i == nt - 1
        keep = jnp.where(last, 0.0, 1.0)
        pad = jnp.zeros((T_MIX - BLOCK, KV_W), F32)

        def with_halos(main_ref, halo_ref, next_ref):
            parts = []
            for m in range(1, per + 1):
                parts += [pad, halo_ref[m] if m < per else next_ref[0] * keep]
            return main_ref[...] + jnp.concatenate(parts, axis=0)

        dk = with_halos(dk_ref, dkh_ref, dkn_ref)
        dv = with_halos(dv_ref, dvh_ref, dvn_ref)
        dp_ref[:, 0:ATTN_W] = dq_ref[...]
        dp_ref[:, ATTN_W:ATTN_W + KV_W] = _rope(dk, c_ref[...], -a_ref[...], -b_ref[...]).astype(BF16)
        dp_ref[:, ATTN_W + KV_W:ATTN_W + 2 * KV_W] = dv.astype(BF16)
        base = ATTN_W + 2 * KV_W
        dp_ref[:, base:base + 512] = dga_ref[...]
        dp_ref[:, base + 512:base + 1024] = db_ref[...]
        dcv = dcv_ref[...]
        nxt = dcvn_ref[...] * keep
        cw = _conv_rows(cw_ref)
        du = cw[2:3, :] * dcv + cw[1:2, :] * _shift_up(dcv, nxt, 1) + cw[0:1, :] * _shift_up(dcv, nxt, 2)
        dp_ref[:, base + 1024:base + 1536] = (du * ch_ref[:, 512:1024]).astype(BF16)
        dp_ref[:, base + 1536:base + 2048] = (du * ch_ref[:, 0:512]).astype(BF16)
        dp_ref[:, base + 2048:base + 2560] = dgc_ref[...]

        xf = x_ref[...]
        r1 = lax.rsqrt(jnp.mean(xf * xf, axis=-1, keepdims=True) + EPS)
        n1 = xf * r1
        xn = (n1 * g_ref[...]).astype(BF16)
        for n in range(IN_W // 256):
            cols = slice(256 * n, 256 * (n + 1))
            acc_ref[cols, :] += lax.dot_general(dp_ref[:, cols], xn, _TN, preferred_element_type=F32)
        dxn = jnp.dot(dp_ref[...], w_ref[...], preferred_element_type=F32)
        small_ref[0:1, :] += jnp.sum(dxn * n1, axis=0, keepdims=True)
        dxg = dxn * g_ref[...]
        gx_ref[...] = r1 * (dxg - n1 * jnp.mean(dxg * n1, axis=-1, keepdims=True)) + dh2_ref[...]

        @pl.when(last)
        def _():
            pltpu.sync_copy(acc_ref, gw_hbm)

    tile = lambda w: pl.BlockSpec((tb, w), lambda i: (i, 0))
    whole = lambda r, w: pl.BlockSpec((r, w), lambda i: (0, 0))
    halo = pl.BlockSpec((per, BLOCK, KV_W), lambda i: (i, 0, 0))
    halo_next = pl.BlockSpec((1, BLOCK, KV_W), lambda i: (jnp.minimum((i + 1) * per, seq // T_MIX - 1), 0, 0))
    next8 = pl.BlockSpec((8, CONV_W), lambda i: (jnp.minimum((i + 1) * (tb // 8), seq // 8 - 1), 0))
    ch = pl.BlockSpec((tb, 1024), lambda i: (i, 1))
    return pl.pallas_call(
        body, name="bwd_proj", grid=(nt,),
        out_shape=(jax.ShapeDtypeStruct((seq, D_MODEL), F32), jax.ShapeDtypeStruct((IN_W, D_MODEL), F32),
                   jax.ShapeDtypeStruct((SMALL_ROWS, D_MODEL), F32)),
        in_specs=[tile(D_MODEL), whole(1, D_MODEL), tile(D_MODEL), tile(ATTN_W), tile(KV_W), tile(KV_W), halo, halo,
                  halo_next, halo_next,
                  tile(ATTN_W), tile(CONV_W), tile(CONV_W), tile(CONV_W), next8, ch, CONV_SPEC,
                  pl.BlockSpec((IN_W, D_MODEL), lambda i: (0, 0), pipeline_mode=pl.Buffered(1)),
                  tile(128), tile(128), tile(128)],
        out_specs=(tile(D_MODEL), pl.BlockSpec(memory_space=pl.ANY), whole(SMALL_ROWS, D_MODEL)),
        scratch_shapes=[pltpu.VMEM((tb, IN_W), BF16), pltpu.VMEM((IN_W, D_MODEL), F32)],
        compiler_params=_params(("arbitrary",)),
    )(x, norm_g, dh2, dq, dk, dv, dkh, dvh, dkh, dvh, dga, db, dgc, dcv, dcv, rest, conv_w, w_in_t,
      rope_c, rope_a, rope_b)


def _adamw_step(w, g, m, v):
    m2 = ADAM_B1 * m + (1.0 - ADAM_B1) * g
    v2 = ADAM_B2 * v + (1.0 - ADAM_B2) * jnp.square(g)
    m_hat = m2 / (1.0 - ADAM_B1 ** ADAM_STEP)
    v_hat = v2 / (1.0 - ADAM_B2 ** ADAM_STEP)
    return -ADAM_LR * (m_hat / (jnp.sqrt(v_hat) + ADAM_EPS) + ADAM_WD * w), m2, v2


def _adamw_weights(groups):
    steps = 4

    def body(*refs):
        ins, outs = refs[:4 * len(groups)], refs[4 * len(groups):]
        for k in range(len(groups)):
            res = _adamw_step(*(r[...] for r in ins[4 * k:4 * k + 4]))
            for o_ref, val in zip(outs[3 * k:3 * k + 3], res):
                o_ref[...] = val

    in_specs, out_specs, out_shape = [], [], []
    for w, _, _, _ in groups:
        rows, cols = w.shape
        spec = pl.BlockSpec((rows // steps, cols), lambda i: (i, 0))
        in_specs += [spec] * 4
        out_specs += [spec] * 3
        out_shape += [jax.ShapeDtypeStruct((rows, cols), F32)] * 3
    flat = pl.pallas_call(
        body, name="adamw_weights", grid=(steps,), out_shape=tuple(out_shape), in_specs=in_specs,
        out_specs=tuple(out_specs), compiler_params=_params(("arbitrary",)),
    )(*[a for grp in groups for a in grp])
    return [flat[3 * k:3 * k + 3] for k in range(len(groups))]


def _adamw_small(chip, small, params, m, v):
    def body(chip_ref, small_ref, conv_ref, *refs):
        ins, outs = refs[:12], refs[12:]
        outs[0][...] = jnp.sum(small_ref[6:7, :], axis=-1, keepdims=True)
        grads = (small_ref[0:1, :], small_ref[1:2, :], conv_ref[2:5, :], small_ref[5:6, 0:8])
        for k, g in enumerate(grads):
            outs[1 + k][...] = g
            res = _adamw_step(ins[k][...], g, ins[4 + k][...], ins[8 + k][...])
            for n, val in enumerate(res):
                outs[5 + 4 * n + k][...] = val

    full = lambda a: pl.BlockSpec(a.shape, lambda i, c: (0,) * len(a.shape))
    shapes = [jax.ShapeDtypeStruct(p.shape, F32) for p in params]
    outs = [jax.ShapeDtypeStruct((1, 1), F32)] + shapes * 4
    flat = pl.pallas_call(
        body, name="adamw_small",
        grid_spec=pltpu.PrefetchScalarGridSpec(
            num_scalar_prefetch=1, grid=(1,),
            in_specs=[full(small), pl.BlockSpec((SMALL_ROWS, 128), lambda i, c: (0, c[0]))]
            + [full(a) for a in (*params, *m, *v)],
            out_specs=tuple(full(s) for s in outs)),
        out_shape=tuple(outs), compiler_params=_params(("arbitrary",)),
    )(chip, small, small, *params, *m, *v)
    return flat[0], flat[1:5], [flat[5 + 4 * n:9 + 4 * n] for n in range(3)]


def kernel(x, norm_g, w_in, sinks, conv_w, w_out, final_g, loss_target, m_norm_g, m_w_in, m_sinks, m_conv_w, m_w_out, m_final_g, v_norm_g, v_w_in, v_sinks, v_conv_w, v_w_out, v_final_g):
    seq = x.shape[1]
    x2 = x.reshape(seq, D_MODEL)
    tgt = loss_target.reshape(seq, D_MODEL)
    ng = norm_g.reshape(1, D_MODEL)
    fg = final_g.reshape(1, D_MODEL)
    chip = 2 * lax.axis_index("x") + lax.axis_index("y")

    conv_w8 = jnp.zeros((8, 128), F32).at[0:3].set(conv_w)
    w_in_full = _gather_w_in(w_in.T).reshape(IN_W, D_MODEL)

    q, kd, vd, rest, rope_c, rope_a, rope_b, wo_all, cw_all = _fwd_proj(x2, ng, w_in_full, w_out, conv_w8)
    w_out_full = wo_all.reshape(D_MODEL, D_MODEL)
    probs, d_attn, prod, dga, db, dgc, dcv, dh2, g_wo, g_wo_b, small_m = _fwd_mix(
        x2, q, kd, vd, rest, sinks, cw_all, w_out_full, fg, tgt)
    out_blocks = lambda t: t.reshape(N_CHIPS, W_OUT_BLK, D_MODEL)
    dq, dk, dv, dkh, dvh, grad_w_out = _bwd_mix(
        q, kd, vd, probs, d_attn, prod, rope_c, rope_a, rope_b, out_blocks(g_wo), out_blocks(g_wo_b))
    grad_x, g_wi, small_p = _bwd_proj(x2, ng, dh2, dq, dk, dv, dkh, dvh, dga, db, dgc, dcv, rest, cw_all,
                                      w_in_full, rope_c, rope_a, rope_b)

    g_in_blocks = g_wi.reshape(N_CHIPS, W_IN_BLK, D_MODEL)
    grad_w_in_t, small = _reduce_grads(g_in_blocks, small_m, small_p)

    (upd_wi, upd_wo) = _adamw_weights([(w_in.T, grad_w_in_t, m_w_in.T, v_w_in.T),
                                       (w_out, grad_w_out, m_w_out, v_w_out)])
    row = lambda t: t.reshape(1, -1)
    loss, grads_s, upd_s = _adamw_small(
        chip.reshape(1), small, (ng, fg, conv_w, row(sinks)),
        (row(m_norm_g), row(m_final_g), m_conv_w, row(m_sinks)),
        (row(v_norm_g), row(v_final_g), v_conv_w, row(v_sinks)))

    def named(ng_, fg_, cw_, sk_, wi_t, wo_):
        return [ng_.reshape(D_MODEL), wi_t.T, sk_.reshape(8), cw_, wo_, fg_.reshape(D_MODEL)]

    g_named = named(*grads_s, grad_w_in_t, grad_w_out)
    out = [loss.reshape(()), grad_x.reshape(1, seq, D_MODEL)] + g_named
    for n in range(3):
        out += named(*upd_s[n], upd_wi[n], upd_wo[n])
    return tuple(out)
```

```python
import jax
import jax.numpy as jnp
from jax import lax
from jax.experimental import pallas as pl
from jax.experimental.pallas import tpu as pltpu

F32 = jnp.float32
BF16 = jnp.bfloat16

D_MODEL = 1024
HEAD_DIM = 64
ATTN_W = 512
KV_W = 128
CONV_W = 512
IN_W = 3328
REST_W = IN_W - ATTN_W - 2 * KV_W
BLOCK = 128
ROT_DIM = 16
ROPE_THETA = 500000.0
EPS = 1e-5
SCALE = 0.125
NEG = -1e30

N_CHIPS = 4
W_IN_BLK = IN_W // N_CHIPS
W_OUT_BLK = D_MODEL // N_CHIPS

ADAM_LR = 0.001
ADAM_B1 = 0.9
ADAM_B2 = 0.999
ADAM_EPS = 1e-08
ADAM_WD = 0.01
ADAM_STEP = 10

VMEM_LIMIT = 60 * 1024 * 1024
T_PROJ = 512
T_FMIX = 512
T_MIX = 512
SMALL_ROWS = 8
MESH = pl.DeviceIdType.MESH

_NT = (((1,), (1,)), ((), ()))
_TN = (((0,), (0,)), ((), ()))


def _params(sem=None):
    kw = dict(vmem_limit_bytes=VMEM_LIMIT)
    if sem is not None:
        kw["dimension_semantics"] = sem
    return pltpu.CompilerParams(**kw)


def _sigmoid(t):
    return 1.0 / (1.0 + jnp.exp(-t))


def _shift_down(t, prev8, k):
    rolled = pltpu.roll(t, k, 0)
    row = lax.broadcasted_iota(jnp.int32, t.shape, 0)
    for j in range(k):
        rolled = jnp.where(row == j, prev8[8 - k + j:8 - k + j + 1, :], rolled)
    return rolled


def _shift_up(t, next8, k):
    n = t.shape[0]
    rolled = pltpu.roll(t, n - k, 0)
    row = lax.broadcasted_iota(jnp.int32, t.shape, 0)
    for j in range(k):
        rolled = jnp.where(row == n - k + j, next8[j:j + 1, :], rolled)
    return rolled


def _rope(t, c, a, b):
    w = t.shape[1]
    reps = w // 128
    if reps > 1:
        c, a, b = (jnp.concatenate([z] * reps, axis=1) for z in (c, a, b))
    return t * c + pltpu.roll(t, w - 8, 1) * a + pltpu.roll(t, 8, 1) * b


def _lane_lo(shape):
    return lax.broadcasted_iota(jnp.int32, shape, 1) < HEAD_DIM


def _stack_heads(t, g):
    lo = _lane_lo((BLOCK, 128))
    parts = []
    for hh in range(4):
        pair = t[:, 256 * g + 128 * (hh // 2):256 * g + 128 * (hh // 2) + 128]
        keep = lo if hh % 2 == 0 else jnp.logical_not(lo)
        parts.append(jnp.where(keep, pair, jnp.zeros_like(pair)))
    return jnp.concatenate(parts, axis=0)


def _unstack_pair(o, pp):
    lo = _lane_lo((BLOCK, 128))
    return jnp.where(lo, o[256 * pp:256 * pp + 128], o[256 * pp + 128:256 * pp + 256])


def _band_masks(has_prev):
    r = lax.broadcasted_iota(jnp.int32, (4 * BLOCK, 2 * BLOCK), 0) % BLOCK
    kj = lax.broadcasted_iota(jnp.int32, (4 * BLOCK, 2 * BLOCK), 1)
    cur = (kj >= BLOCK) & (kj - BLOCK <= r)
    prev = (kj < BLOCK) & (kj > r)
    return cur | (prev & has_prev), cur | prev


def _sink_col(sinks_ref, g):
    r = lax.broadcasted_iota(jnp.int32, (4 * BLOCK, 1), 0) // BLOCK
    col = jnp.full((4 * BLOCK, 1), sinks_ref[4 * g + 3], F32)
    for hh in range(3):
        col = jnp.where(r == hh, sinks_ref[4 * g + hh], col)
    return col


def _softmax(s, sink_col, mask):
    s = jnp.where(mask, s, NEG)
    m = jnp.maximum(jnp.max(s, axis=-1, keepdims=True), sink_col)
    p = jnp.exp(s - m)
    es = jnp.exp(sink_col - m)
    inv = 1.0 / (jnp.sum(p, axis=-1, keepdims=True) + es)
    return p * inv, es * inv


def _key_windows(i, nsub, kd_ref, vd_ref, kdp_ref, vdp_ref):
    mask_first, mask_rest = _band_masks(i > 0)
    out = []
    for sb in range(nsub):
        rows = slice(BLOCK * sb, BLOCK * (sb + 1))
        if sb == 0:
            kk = jnp.concatenate([kdp_ref[...], kd_ref[rows, :]], axis=0)
            vv = jnp.concatenate([vdp_ref[...], vd_ref[rows, :]], axis=0)
            out.append((rows, kk, vv, mask_first))
        else:
            both = slice(BLOCK * (sb - 1), BLOCK * (sb + 1))
            out.append((rows, kd_ref[both, :], vd_ref[both, :], mask_rest))
    return out


def _gather_w_in(w_in_t):
    hi = W_IN_BLK // 2
    qr = hi // 2

    def body(wi_ref, wi_all, send_sems, recv_sems):
        x, y, c = lax.axis_index("x"), lax.axis_index("y"), lax.axis_index("c")
        me, sibling = (x, y), (x, y, 1 - c)
        xnb, ynb, diag = (1 - x, y), (x, 1 - y), (1 - x, 1 - y)

        wi_all[2 * x + y] = wi_ref[...].astype(BF16)

        def copy(k, chip, half, quarter, to):
            r = wi_all.at[2 * chip[0] + chip[1], pl.ds(half * hi + quarter * qr, qr)]
            return pltpu.make_async_remote_copy(src_ref=r, dst_ref=r, send_sem=send_sems.at[k],
                                                recv_sem=recv_sems.at[k], device_id=to, device_id_type=MESH)

        plan = [(0, xnb, 0), (1, ynb, 1), (2, xnb, 1), (3, ynb, 0)]
        sent = [copy(k, me, c, quarter, (*nb, c)) for k, nb, quarter in plan]
        for cp in sent:
            cp.start()
        arrivals = [(0, xnb, 0), (1, ynb, 1), (2, xnb, 1), (3, ynb, 0), (4, diag, 0), (5, diag, 1)]
        relay = {0: (4, ynb), 1: (5, xnb)}
        for k, chip, quarter in arrivals:
            copy(k, chip, c, quarter, (x, y, c)).wait_recv()
            if k in relay:
                sent.append(copy(relay[k][0], chip, c, quarter, (*relay[k][1], c)))
                sent[-1].start()
            sent.append(copy(6 + k, chip, c, quarter, sibling))
            sent[-1].start()
        for k, chip, quarter in arrivals:
            copy(6 + k, chip, 1 - c, quarter, (x, y, c)).wait_recv()
        for cp in sent:
            cp.wait_send()

    vmem = pl.BlockSpec(memory_space=pltpu.VMEM)
    return pl.pallas_call(
        body, name="gather_w_in",
        out_shape=jax.ShapeDtypeStruct((N_CHIPS, W_IN_BLK, D_MODEL), BF16),
        in_specs=[vmem], out_specs=vmem,
        scratch_shapes=[pltpu.SemaphoreType.DMA((12,)), pltpu.SemaphoreType.DMA((12,))],
        compiler_params=_params(),
    )(w_in_t)


def _reduce_grads(g_in, *smalls):
    hi = W_IN_BLK // 2
    qr = hi // 2
    q0, q1 = slice(0, qr), slice(qr, hi)

    def body(gi_hbm, s0_ref, s1_ref, gi_out, small_out,
             mine_i, sib_i, out_i, ici_i, small_in, small_ref, send_sems, recv_sems, local_sems):
        x, y, c = lax.axis_index("x"), lax.axis_index("y"), lax.axis_index("c")
        my_dev = 4 * x + 2 * y + c
        sibling = (x, y, 1 - c)
        xnb, ynb = (1 - x, y, c), (x, 1 - y, c)
        order = [(1 - x, 1 - y), (1 - x, y), (x, 1 - y), (x, y)]

        def remote(k, src, dst, to):
            return pltpu.make_async_remote_copy(src_ref=src, dst_ref=dst, send_sem=send_sems.at[k],
                                                recv_sem=recv_sems.at[k], device_id=to, device_id_type=MESH)

        small_ref[...] = s0_ref[...] + s1_ref[...]
        small_cps = []
        for f in range(1, 8):
            fx, fy, fc = f >> 2, (f >> 1) & 1, f & 1
            small_cps.append(remote(11 + f - 1, small_ref, small_in.at[f - 1], (x ^ fx, y ^ fy, c ^ fc)))
        for cp in small_cps:
            cp.start()

        own, to_sib = [], []
        for n, chip in enumerate(order):
            j = 2 * chip[0] + chip[1]
            own.append(pltpu.make_async_copy(gi_hbm.at[j, pl.ds(c * hi, hi)], mine_i.at[n], local_sems.at[n]))
            to_sib.append(remote(6 + n, gi_hbm.at[j, pl.ds((1 - c) * hi, hi)], sib_i.at[n], sibling))
            own[-1].start()
            to_sib[-1].start()

        def pair_sum(n):
            own[n].wait()
            to_sib[n].wait_recv()
            return mine_i[n] + sib_i[n]

        ici = [remote(k, out_i.at[k], ici_i.at[k], xnb if k % 2 == 0 else ynb) for k in range(6)]

        def send(k, rows_f32):
            out_i[k] = rows_f32.astype(BF16)
            ici[k].start()

        p_diag = pair_sum(0)
        send(0, p_diag[q0])
        send(1, p_diag[q1])
        p_x = pair_sum(1)
        send(2, p_x[q0])
        p_y = pair_sum(2)
        send(3, p_y[q1])
        ici[0].wait_recv()
        send(5, p_y[q0] + ici_i[0].astype(F32))
        ici[1].wait_recv()
        send(4, p_x[q1] + ici_i[1].astype(F32))
        p_mine = pair_sum(3)
        for k in range(2, 6):
            ici[k].wait_recv()
        gi_out[pl.ds(c * hi, qr), :] = p_mine[q0] + ici_i[2].astype(F32) + ici_i[5].astype(F32)
        gi_out[pl.ds(c * hi + qr, qr), :] = p_mine[q1] + ici_i[4].astype(F32) + ici_i[3].astype(F32)

        swap = [remote(10, gi_out.at[pl.ds(c * hi, hi)], gi_out.at[pl.ds(c * hi, hi)], sibling)]
        for cp in swap:
            cp.start()

        for cp in small_cps:
            cp.wait_recv()
        total = jnp.zeros((SMALL_ROWS, D_MODEL), F32)
        for d in range(8):
            slot = jnp.maximum((d ^ my_dev) - 1, 0)
            total = total + jnp.where(d == my_dev, small_ref[...], small_in[slot])
        small_out[...] = total

        remote(10, gi_out.at[pl.ds((1 - c) * hi, hi)], gi_out.at[pl.ds((1 - c) * hi, hi)], sibling).wait_recv()
        for cp in to_sib + ici + swap + small_cps:
            cp.wait_send()

    vmem = pl.BlockSpec(memory_space=pltpu.VMEM)
    anyspace = pl.BlockSpec(memory_space=pl.ANY)
    return pl.pallas_call(
        body, name="reduce_grads",
        out_shape=(jax.ShapeDtypeStruct((W_IN_BLK, D_MODEL), F32), jax.ShapeDtypeStruct((SMALL_ROWS, D_MODEL), F32)),
        in_specs=[anyspace, vmem, vmem], out_specs=(vmem, vmem),
        scratch_shapes=[pltpu.VMEM((N_CHIPS, hi, D_MODEL), F32), pltpu.VMEM((N_CHIPS, hi, D_MODEL), F32),
                        pltpu.VMEM((6, qr, D_MODEL), BF16), pltpu.VMEM((6, qr, D_MODEL), BF16),
                        pltpu.VMEM((7, SMALL_ROWS, D_MODEL), F32), pltpu.VMEM((SMALL_ROWS, D_MODEL), F32),
                        pltpu.SemaphoreType.DMA((18,)), pltpu.SemaphoreType.DMA((18,)),
                        pltpu.SemaphoreType.DMA((4,))],
        compiler_params=_params(),
    )(g_in, *smalls)


def _fwd_proj(x, norm_g, w_in_t, w_out, conv_w8):
    seq = x.shape[0]
    nt = seq // T_PROJ
    lane = jnp.arange(128, dtype=jnp.int32) % HEAD_DIM
    inv_freq = ROPE_THETA ** (-(2 * (lane % 8)).astype(F32) / ROT_DIM)
    inv_freq = jnp.where(lane < ROT_DIM, inv_freq, 0.0).reshape(1, 128)
    in_tile = jnp.arange(T_PROJ, dtype=jnp.int32).astype(F32)[:, None] * inv_freq
    cos_in, sin_in = jnp.cos(in_tile), jnp.sin(in_tile)
    start = jnp.repeat((jnp.arange(nt, dtype=jnp.int32) * T_PROJ).astype(F32), 8)[:, None] * inv_freq
    cos_st, sin_st = jnp.cos(start), jnp.sin(start)

    def body(x_ref, g_ref, w_ref, cs_ref, ss_ref, ci_ref, si_ref, wo_ref, cw_ref,
             q_ref, kd_ref, vd_ref, rest_ref, c_ref, a_ref, b_ref, wo_all, cw_all,
             wo_stage, send_sems, recv_sems, local_sems):
        i = pl.program_id(0)
        mx, my, mc = lax.axis_index("x"), lax.axis_index("y"), lax.axis_index("c")
        chips = [(1 - mx, my), (mx, 1 - my), (1 - mx, 1 - my)]

        def gather(blocks):
            cps = []
            for k, chip in enumerate(chips):
                for n, (src, dst) in enumerate(((wo_stage, wo_all), (cw_ref, cw_all))):
                    cps.append(pltpu.make_async_remote_copy(
                        src_ref=src, dst_ref=dst.at[blocks[k]], send_sem=send_sems.at[2 * k + n],
                        recv_sem=recv_sems.at[2 * k + n], device_id=(*chip, mc), device_id_type=MESH))
            return cps

        me = 2 * mx + my
        own = [pltpu.make_async_copy(wo_stage, wo_all.at[me], local_sems.at[0]),
               pltpu.make_async_copy(cw_ref, cw_all.at[me], local_sems.at[1])]

        @pl.when(i == 0)
        def _():
            wo_stage[...] = wo_ref[...].astype(BF16)
            for cp in own + gather([me] * 3):
                cp.start()

        xf = x_ref[...]
        r1 = lax.rsqrt(jnp.mean(xf * xf, axis=-1, keepdims=True) + EPS)
        xn = (xf * r1 * g_ref[...]).astype(BF16)
        cs, ss = cs_ref[0:1, :], ss_ref[0:1, :]
        c = cs * ci_ref[...] - ss * si_ref[...]
        sin = ss * ci_ref[...] + cs * si_ref[...]
        j = lax.broadcasted_iota(jnp.int32, (T_PROJ, 128), 1) % HEAD_DIM
        a = jnp.where(j < 8, -sin, 0.0)
        b = jnp.where(j >= 8, sin, 0.0)
        c_ref[...], a_ref[...], b_ref[...] = c, a, b
        proj = lambda lo_c, w: lax.dot_general(xn, w_ref[lo_c:lo_c + w, :], _NT, preferred_element_type=F32)
        q_ref[...] = (_rope(proj(0, ATTN_W), c, a, b) * SCALE).astype(BF16)
        kv = proj(ATTN_W, 2 * KV_W)
        k = _rope(kv[:, 0:KV_W], c, a, b)
        v = kv[:, KV_W:2 * KV_W]
        lo = _lane_lo(k.shape)
        for t, ref in ((k, kd_ref), (v, vd_ref)):
            sw = pltpu.roll(t, HEAD_DIM, 1)
            ref[:, 0:128] = jnp.where(lo, t, sw).astype(BF16)
            ref[:, 128:256] = jnp.where(lo, sw, t).astype(BF16)
        for n in range(REST_W // 512):
            rest_ref[:, 512 * n:512 * (n + 1)] = proj(ATTN_W + 2 * KV_W + 512 * n, 512)

        @pl.when(i == nt - 1)
        def _():
            sent = gather([me] * 3)
            for cp in gather([2 * chip[0] + chip[1] for chip in chips]):
                cp.wait_recv()
            for cp in sent:
                cp.wait_send()
            for cp in own:
                cp.wait()

    tile = lambda w: pl.BlockSpec((T_PROJ, w), lambda i: (i, 0))
    whole = lambda r, w: pl.BlockSpec((r, w), lambda i: (0, 0))
    vmem = pl.BlockSpec(memory_space=pltpu.VMEM)
    hbm = pl.BlockSpec(memory_space=pl.ANY)
    return pl.pallas_call(
        body, name="fwd_proj", grid=(nt,),
        out_shape=(jax.ShapeDtypeStruct((seq, ATTN_W), BF16), jax.ShapeDtypeStruct((seq, 2 * KV_W), BF16),
                   jax.ShapeDtypeStruct((seq, 2 * KV_W), BF16), jax.ShapeDtypeStruct((seq, REST_W), F32))
        + (jax.ShapeDtypeStruct((seq, 128), F32),) * 3
        + (jax.ShapeDtypeStruct((N_CHIPS, W_OUT_BLK, D_MODEL), BF16), jax.ShapeDtypeStruct((N_CHIPS, 8, 128), F32)),
        in_specs=[tile(D_MODEL), whole(1, D_MODEL), whole(IN_W, D_MODEL), pl.BlockSpec((8, 128), lambda i: (i, 0)),
                  pl.BlockSpec((8, 128), lambda i: (i, 0)), whole(T_PROJ, 128), whole(T_PROJ, 128), vmem, vmem],
        out_specs=(tile(ATTN_W), tile(2 * KV_W), tile(2 * KV_W), tile(REST_W), tile(128), tile(128), tile(128),
                   hbm, hbm),
        scratch_shapes=[pltpu.VMEM((W_OUT_BLK, D_MODEL), BF16), pltpu.SemaphoreType.DMA((6,)),
                        pltpu.SemaphoreType.DMA((6,)), pltpu.SemaphoreType.DMA((2,))],
        compiler_params=_params(("arbitrary",)),
    )(x, norm_g, w_in_t, cos_st, sin_st, cos_in, sin_in, w_out, conv_w8)


CONV_SPEC = pl.BlockSpec((N_CHIPS, 8, 128), lambda i: (0, 0, 0))


def _conv_rows(cw_ref):
    return jnp.concatenate([cw_ref[j] for j in range(N_CHIPS)], axis=1)


def _conv_parts(rest_ref, prev_ref, cw_ref, first):
    u = rest_ref[:, 1024:1536] * rest_ref[:, 1536:2048]
    up = prev_ref[:, 1024:1536] * prev_ref[:, 1536:2048]
    up = jnp.where(first, jnp.zeros_like(up), up)
    um1 = _shift_down(u, up, 1)
    um2 = _shift_down(u, up, 2)
    cw = _conv_rows(cw_ref)
    cv = cw[0:1, :] * um2 + cw[1:2, :] * um1 + cw[2:3, :] * u
    return u, um1, um2, cv


def _fwd_mix(x, q, kd, vd, rest, sinks, conv_w, w_out, final_g, target):
    seq = x.shape[0]
    nt = seq // T_FMIX
    nsub = T_FMIX // BLOCK

    def body(sinks_ref, x_ref, q_ref, kd_ref, vd_ref, kdp_ref, vdp_ref, rest_ref, restp_ref, cw_ref, wo_ref,
             fg_ref, tgt_ref, prob_ref, do_ref, prod_ref, dga_ref, db_ref, dgc_ref, dcv_ref, dh2_ref,
             gwo_hbm, gwob_hbm, small_ref,
             pmix_ref, pdh2_ref, gwo_ref, gate_ref, dsink_ref):
        i = pl.program_id(0)
        attn_ref, sinkw_ref = prod_ref, dcv_ref

        @pl.when(i == 0)
        def _():
            small_ref[...] = jnp.zeros_like(small_ref)
            dsink_ref[...] = jnp.zeros_like(dsink_ref)
            gwo_ref[...] = jnp.zeros_like(gwo_ref)
            pmix_ref[...] = jnp.zeros_like(pmix_ref)
            pdh2_ref[...] = jnp.zeros_like(pdh2_ref)

        chains = []
        for rows, kk, vv, mask in _key_windows(i, nsub, kd_ref, vd_ref, kdp_ref, vdp_ref):
            qt = q_ref[rows, :]
            for g in range(2):
                kg = kk[:, 128 * g:128 * (g + 1)]
                chains.append(dict(g=g, rows=rows, mask=mask, vg=vv[:, 128 * g:128 * (g + 1)],
                                   s=lax.dot_general(_stack_heads(qt, g), kg, _NT, preferred_element_type=F32)))
        gwo_ref[...] += lax.dot_general(pmix_ref[...], pdh2_ref[...], _TN, preferred_element_type=F32)
        for ch in chains:
            ch["prob"], ch["psink"] = _softmax(ch.pop("s"), _sink_col(sinks_ref, ch["g"]), ch["mask"])
        for k, ch in enumerate(chains):
            prob = ch["prob"].astype(BF16)
            prob_ref[k] = prob
            o = jnp.dot(prob, ch["vg"], preferred_element_type=F32)
            ow = o * ch["psink"]
            for pp in range(2):
                lanes = slice(256 * ch["g"] + 128 * pp, 256 * ch["g"] + 128 * (pp + 1))
                attn_ref[ch["rows"], lanes] = _unstack_pair(o, pp)
                sinkw_ref[ch["rows"], lanes] = _unstack_pair(ow, pp)

        def silu_parts(t, lo_c):
            sg = _sigmoid(t)
            silu = t * sg
            gate_ref[:, lo_c:lo_c + 512] = silu
            gate_ref[:, lo_c + 512:lo_c + 1024] = sg * (1.0 + t * (1.0 - sg))
            return silu

        pmix_ref[:, 0:ATTN_W] = (attn_ref[...] * silu_parts(rest_ref[:, 0:512], 0)).astype(BF16)
        u, um1, um2, cv = _conv_parts(rest_ref, restp_ref, cw_ref, i == 0)
        pmix_ref[:, ATTN_W:] = (rest_ref[:, 512:1024] * cv * silu_parts(rest_ref[:, 2048:2560], 1024)).astype(BF16)

        h2 = x_ref[...] + jnp.dot(pmix_ref[...], wo_ref[...], preferred_element_type=F32)
        r2 = lax.rsqrt(jnp.mean(h2 * h2, axis=-1, keepdims=True) + EPS)
        n2 = h2 * r2
        err = n2 * fg_ref[...] - tgt_ref[...]
        dy = err * (1.0 / D_MODEL)
        small_ref[6:7, :] += jnp.sum(err * err, axis=0, keepdims=True) * (0.5 / D_MODEL)
        small_ref[1:2, :] += jnp.sum(dy * n2, axis=0, keepdims=True)
        dn = dy * fg_ref[...]
        dh2 = r2 * (dn - n2 * jnp.mean(dn * n2, axis=-1, keepdims=True))
        dh2_ref[...] = dh2
        pdh2_ref[...] = dh2.astype(BF16)

        d_mix = lambda lo_r: lax.dot_general(pdh2_ref[...], wo_ref[lo_r:lo_r + 512, :], _NT, preferred_element_type=F32)
        dma = d_mix(0)
        dga_ref[...] = (dma * attn_ref[...] * gate_ref[:, 512:1024]).astype(BF16)
        d_attn = dma * gate_ref[:, 0:512]
        do_ref[...] = d_attn.astype(BF16)
        prod_ref[...] = d_attn * attn_ref[...]
        dsink_ref[0:1, :] += jnp.sum(d_attn * sinkw_ref[...], axis=0, keepdims=True)
        bg = rest_ref[:, 512:1024]
        dmc = d_mix(ATTN_W)
        t1 = dmc * gate_ref[:, 1024:1536]
        db_ref[...] = (t1 * cv).astype(BF16)
        dcv = t1 * bg
        dcv_ref[...] = dcv
        dgc_ref[...] = (dmc * (bg * cv) * gate_ref[:, 1536:2048]).astype(BF16)
        small_ref[2:3, 0:CONV_W] += jnp.sum(dcv * um2, axis=0, keepdims=True)
        small_ref[3:4, 0:CONV_W] += jnp.sum(dcv * um1, axis=0, keepdims=True)
        small_ref[4:5, 0:CONV_W] += jnp.sum(dcv * u, axis=0, keepdims=True)

        @pl.when(i == nt - 1)
        def _():
            head = lax.broadcasted_iota(jnp.int32, (1, ATTN_W), 1) // HEAD_DIM
            for h in range(8):
                tot = jnp.sum(jnp.where(head == h, dsink_ref[0:1, :], 0.0), axis=-1, keepdims=True)
                small_ref[5:6, h:h + 1] = -tot
            gwo_ref[...] += lax.dot_general(pmix_ref[...], pdh2_ref[...], _TN, preferred_element_type=F32)
            pltpu.sync_copy(gwo_ref, gwo_hbm)
            for n in range(D_MODEL // T_FMIX):
                slab = slice(T_FMIX * n, T_FMIX * (n + 1))
                pmix_ref[...] = gwo_ref[slab, :].astype(BF16)
                pltpu.sync_copy(pmix_ref, gwob_hbm.at[slab])

    tile = lambda w: pl.BlockSpec((T_FMIX, w), lambda i: (i, 0))
    whole = lambda r, w: pl.BlockSpec((r, w), lambda i: (0, 0))
    prev_blk = pl.BlockSpec((BLOCK, 2 * KV_W), lambda i: (jnp.maximum(i * nsub - 1, 0), 0))
    prev8 = pl.BlockSpec((8, REST_W), lambda i: (jnp.maximum(i * (T_FMIX // 8) - 1, 0), 0))
    bf = lambda w: jax.ShapeDtypeStruct((seq, w), BF16)
    f32 = lambda w: jax.ShapeDtypeStruct((seq, w), F32)
    return pl.pallas_call(
        body, name="fwd_mix", grid=(nt,),
        out_shape=(jax.ShapeDtypeStruct((2 * seq // BLOCK, 4 * BLOCK, 2 * BLOCK), BF16),
                   bf(ATTN_W), f32(ATTN_W), bf(ATTN_W), bf(CONV_W), bf(CONV_W), f32(CONV_W), f32(D_MODEL),
                   jax.ShapeDtypeStruct((D_MODEL, D_MODEL), F32), jax.ShapeDtypeStruct((D_MODEL, D_MODEL), BF16),
                   jax.ShapeDtypeStruct((SMALL_ROWS, D_MODEL), F32)),
        in_specs=[pl.BlockSpec(memory_space=pltpu.SMEM), tile(D_MODEL), tile(ATTN_W), tile(2 * KV_W), tile(2 * KV_W),
                  prev_blk, prev_blk, tile(REST_W), prev8, CONV_SPEC,
                  pl.BlockSpec((D_MODEL, D_MODEL), lambda i: (0, 0), pipeline_mode=pl.Buffered(1)),
                  whole(1, D_MODEL), tile(D_MODEL)],
        out_specs=(pl.BlockSpec((2 * nsub, 4 * BLOCK, 2 * BLOCK), lambda i: (i, 0, 0)),
                   tile(ATTN_W), tile(ATTN_W), tile(ATTN_W), tile(CONV_W), tile(CONV_W), tile(CONV_W), tile(D_MODEL),
                   pl.BlockSpec(memory_space=pl.ANY), pl.BlockSpec(memory_space=pl.ANY), whole(SMALL_ROWS, D_MODEL)),
        scratch_shapes=[pltpu.VMEM((T_FMIX, D_MODEL), BF16)] * 2 + [
            pltpu.VMEM((D_MODEL, D_MODEL), F32), pltpu.VMEM((T_FMIX, 4 * 512), F32), pltpu.VMEM((8, ATTN_W), F32)],
        compiler_params=_params(("arbitrary",)),
    )(sinks, x, q, kd, vd, kd, vd, rest, rest, conv_w, w_out, final_g, target)


def _scatter_copies(g_hbm, gb_hbm, mine, land, send_sems, recv_sems, local_sem, half):
    x, y, c = lax.axis_index("x"), lax.axis_index("y"), lax.axis_index("c")
    cps = []
    for f in range(1, 8):
        to = (x ^ (f >> 2), y ^ ((f >> 1) & 1), c ^ (f & 1))
        src = gb_hbm.at[2 * to[0] + to[1], pl.ds(to[2] * half, half)]
        cps.append(pltpu.make_async_remote_copy(src_ref=src, dst_ref=land.at[f - 1], send_sem=send_sems.at[f - 1],
                                                recv_sem=recv_sems.at[f - 1], device_id=to, device_id_type=MESH))
    own = pltpu.make_async_copy(g_hbm.at[2 * x + y, pl.ds(c * half, half)], mine, local_sem)
    return cps, own


def _scatter_finish(cps, own, mine, land, out_hbm, send_sems, recv_sems, local_sem, half):
    x, y, c = lax.axis_index("x"), lax.axis_index("y"), lax.axis_index("c")
    own.wait()
    tot = mine[...]
    for f in range(1, 8):
        cps[f - 1].wait_recv()
        tot = tot + land[f - 1].astype(F32)
    mine[...] = tot

    def swap(rows_of):
        return pltpu.make_async_remote_copy(src_ref=mine, dst_ref=out_hbm.at[pl.ds(rows_of * half, half)],
                                            send_sem=send_sems.at[7], recv_sem=recv_sems.at[7],
                                            device_id=(x, y, 1 - c), device_id_type=MESH)

    keep = pltpu.make_async_copy(mine, out_hbm.at[pl.ds(c * half, half)], local_sem)
    keep.start()
    swap(c).start()
    swap(1 - c).wait_recv()
    keep.wait()
    for cp in cps:
        cp.wait_send()
    swap(c).wait_send()


def _bwd_mix(q, kd, vd, probs, d_attn, prod, rope_c, rope_a, rope_b, g_out, g_out_b):
    seq = q.shape[0]
    nt = seq // T_MIX
    nsub = T_MIX // BLOCK
    ho = W_OUT_BLK // 2

    def body(q_ref, kd_ref, vd_ref, kdp_ref, vdp_ref, prob_ref, do_ref, prod_ref, c_ref, a_ref, b_ref, go_hbm, gob_hbm,
             dq_ref, dk_ref, dv_ref, dkh_ref, dvh_ref, go_out,
             mine_o, land_o, send_sems, recv_sems, local_sems):
        i = pl.program_id(0)
        scatter = (mine_o, land_o, send_sems, recv_sems, local_sems.at[0], ho)

        @pl.when(i == 0)
        def _():
            cps, own = _scatter_copies(go_hbm, gob_hbm, *scatter)
            for cp in cps + [own]:
                cp.start()

        lo = _lane_lo((2 * BLOCK, 128))
        dk_blocks = [None] * (nsub + 1)
        dv_blocks = [None] * (nsub + 1)

        def add(lst, n, val):
            lst[n] = val if lst[n] is None else lst[n] + val

        chains = []
        for rows, kk, vv, _ in _key_windows(i, nsub, kd_ref, vd_ref, kdp_ref, vdp_ref):
            qt = q_ref[rows, :]
            dot = do_ref[rows, :]
            for g in range(2):
                rs = jnp.sum(_stack_heads(prod_ref[rows, :], g), axis=-1, keepdims=True)
                chains.append(dict(g=g, rows=rows, rs=rs, qs=_stack_heads(qt, g), dos=_stack_heads(dot, g),
                                   kg=kk[:, 128 * g:128 * (g + 1)], vg=vv[:, 128 * g:128 * (g + 1)]))
        for ch in chains:
            ch["dp"] = lax.dot_general(ch["dos"], ch["vg"], _NT, preferred_element_type=F32)
        for k, ch in enumerate(chains):
            ch["ds"] = (prob_ref[k].astype(F32) * (ch["dp"] - ch["rs"])).astype(BF16)
        for k, ch in enumerate(chains):
            dqs = jnp.dot(ch["ds"], ch["kg"], preferred_element_type=F32) * SCALE
            c, a, b = c_ref[ch["rows"], :], a_ref[ch["rows"], :], b_ref[ch["rows"], :]
            for pp in range(2):
                lanes = slice(256 * ch["g"] + 128 * pp, 256 * ch["g"] + 128 * (pp + 1))
                dq_ref[ch["rows"], lanes] = _rope(_unstack_pair(dqs, pp), c, -a, -b).astype(BF16)
            dkd = lax.dot_general(ch["ds"], ch["qs"], _TN, preferred_element_type=F32)
            dvd = lax.dot_general(prob_ref[k], ch["dos"], _TN, preferred_element_type=F32)
            ch["dk"] = dkd + pltpu.roll(dkd, HEAD_DIM, 1)
            ch["dv"] = dvd + pltpu.roll(dvd, HEAD_DIM, 1)
        for sb in range(nsub):
            dk2 = jnp.where(lo, chains[2 * sb]["dk"], chains[2 * sb + 1]["dk"])
            dv2 = jnp.where(lo, chains[2 * sb]["dv"], chains[2 * sb + 1]["dv"])
            add(dk_blocks, sb, dk2[0:BLOCK])
            add(dk_blocks, sb + 1, dk2[BLOCK:])
            add(dv_blocks, sb, dv2[0:BLOCK])
            add(dv_blocks, sb + 1, dv2[BLOCK:])
        dkh_ref[0] = dk_blocks[0]
        dvh_ref[0] = dv_blocks[0]
        for sb in range(nsub):
            dk_ref[BLOCK * sb:BLOCK * (sb + 1), :] = dk_blocks[sb + 1]
            dv_ref[BLOCK * sb:BLOCK * (sb + 1), :] = dv_blocks[sb + 1]

        @pl.when(i == nt - 1)
        def _():
            cps, own = _scatter_copies(go_hbm, gob_hbm, *scatter)
            _scatter_finish(cps, own, mine_o, land_o, go_out, send_sems, recv_sems, local_sems.at[1], ho)

    tile = lambda w: pl.BlockSpec((T_MIX, w), lambda i: (i, 0))
    prev_blk = pl.BlockSpec((BLOCK, 2 * KV_W), lambda i: (jnp.maximum(i * nsub - 1, 0), 0))
    halo = pl.BlockSpec((1, BLOCK, KV_W), lambda i: (i, 0, 0))
    hbm = pl.BlockSpec(memory_space=pl.ANY)
    f32 = lambda w: jax.ShapeDtypeStruct((seq, w), F32)
    return pl.pallas_call(
        body, name="bwd_mix", grid=(nt,),
        out_shape=(jax.ShapeDtypeStruct((seq, ATTN_W), BF16), f32(KV_W), f32(KV_W),
                   jax.ShapeDtypeStruct((nt, BLOCK, KV_W), F32), jax.ShapeDtypeStruct((nt, BLOCK, KV_W), F32),
                   jax.ShapeDtypeStruct((W_OUT_BLK, D_MODEL), F32)),
        in_specs=[tile(ATTN_W), tile(2 * KV_W), tile(2 * KV_W), prev_blk, prev_blk,
                  pl.BlockSpec((2 * nsub, 4 * BLOCK, 2 * BLOCK), lambda i: (i, 0, 0)), tile(ATTN_W), tile(ATTN_W),
                  tile(128), tile(128), tile(128), hbm, hbm],
        out_specs=(tile(ATTN_W), tile(KV_W), tile(KV_W), halo, halo, hbm),
        scratch_shapes=[pltpu.VMEM((ho, D_MODEL), F32), pltpu.VMEM((7, ho, D_MODEL), BF16),
                        pltpu.SemaphoreType.DMA((8,)), pltpu.SemaphoreType.DMA((8,)), pltpu.SemaphoreType.DMA((2,))],
        compiler_params=_params(("arbitrary",)),
    )(q, kd, vd, kd, vd, probs, d_attn, prod, rope_c, rope_a, rope_b, g_out, g_out_b)


def _bwd_proj(x, norm_g, dh2, dq, dk, dv, dkh, dvh, dga, db, dgc, dcv, rest, conv_w, w_in_t, rope_c, rope_a, rope_b):
    seq = x.shape[0]
    tb = T_PROJ
    per = tb // T_MIX
    nt = seq // tb

    def body(x_ref, g_ref, dh2_ref, dq_ref, dk_ref, dv_ref, dkh_ref, dvh_ref, dkn_ref, dvn_ref, dga_ref, db_ref,
             dgc_ref, dcv_ref, dcvn_ref, ch_ref, cw_ref, w_ref, c_ref, a_ref, b_ref, gx_ref, gw_hbm, small_ref,
             dp_ref, acc_ref):
        i = pl.program_id(0)

        @pl.when(i == 0)
        def _():
            small_ref[...] = jnp.zeros_like(small_ref)
            acc_ref[...] = jnp.zeros_like(acc_ref)

        last = i == nt - 1
        keep = jnp.where(last, 0.0, 1.0)
        pad = jnp.zeros((T_MIX - BLOCK, KV_W), F32)

        def with_halos(main_ref, halo_ref, next_ref):
            parts = []
            for m in range(1, per + 1):
                parts += [pad, halo_ref[m] if m < per else next_ref[0] * keep]
            return main_ref[...] + jnp.concatenate(parts, axis=0)

        dk = with_halos(dk_ref, dkh_ref, dkn_ref)
        dv = with_halos(dv_ref, dvh_ref, dvn_ref)
        dp_ref[:, 0:ATTN_W] = dq_ref[...]
        dp_ref[:, ATTN_W:ATTN_W + KV_W] = _rope(dk, c_ref[...], -a_ref[...], -b_ref[...]).astype(BF16)
        dp_ref[:, ATTN_W + KV_W:ATTN_W + 2 * KV_W] = dv.astype(BF16)
        base = ATTN_W + 2 * KV_W
        dp_ref[:, base:base + 512] = dga_ref[...]
        dp_ref[:, base + 512:base + 1024] = db_ref[...]
        dcv = dcv_ref[...]
        nxt = dcvn_ref[...] * keep
        cw = _conv_rows(cw_ref)
        du = cw[2:3, :] * dcv + cw[1:2, :] * _shift_up(dcv, nxt, 1) + cw[0:1, :] * _shift_up(dcv, nxt, 2)
        dp_ref[:, base + 1024:base + 1536] = (du * ch_ref[:, 512:1024]).astype(BF16)
        dp_ref[:, base + 1536:base + 2048] = (du * ch_ref[:, 0:512]).astype(BF16)
        dp_ref[:, base + 2048:base + 2560] = dgc_ref[...]

        xf = x_ref[...]
        r1 = lax.rsqrt(jnp.mean(xf * xf, axis=-1, keepdims=True) + EPS)
        n1 = xf * r1
        xn = (n1 * g_ref[...]).astype(BF16)
        for n in range(IN_W // 256):
            cols = slice(256 * n, 256 * (n + 1))
            acc_ref[cols, :] += lax.dot_general(dp_ref[:, cols], xn, _TN, preferred_element_type=F32)
        dxn = jnp.dot(dp_ref[...], w_ref[...], preferred_element_type=F32)
        small_ref[0:1, :] += jnp.sum(dxn * n1, axis=0, keepdims=True)
        dxg = dxn * g_ref[...]
        gx_ref[...] = r1 * (dxg - n1 * jnp.mean(dxg * n1, axis=-1, keepdims=True)) + dh2_ref[...]

        @pl.when(last)
        def _():
            pltpu.sync_copy(acc_ref, gw_hbm)

    tile = lambda w: pl.BlockSpec((tb, w), lambda i: (i, 0))
    whole = lambda r, w: pl.BlockSpec((r, w), lambda i: (0, 0))
    halo = pl.BlockSpec((per, BLOCK, KV_W), lambda i: (i, 0, 0))
    halo_next = pl.BlockSpec((1, BLOCK, KV_W), lambda i: (jnp.minimum((i + 1) * per, seq // T_MIX - 1), 0, 0))
    next8 = pl.BlockSpec((8, CONV_W), lambda i: (jnp.minimum((i + 1) * (tb // 8), seq // 8 - 1), 0))
    ch = pl.BlockSpec((tb, 1024), lambda i: (i, 1))
    return pl.pallas_call(
        body, name="bwd_proj", grid=(nt,),
        out_shape=(jax.ShapeDtypeStruct((seq, D_MODEL), F32), jax.ShapeDtypeStruct((IN_W, D_MODEL), F32),
                   jax.ShapeDtypeStruct((SMALL_ROWS, D_MODEL), F32)),
        in_specs=[tile(D_MODEL), whole(1, D_MODEL), tile(D_MODEL), tile(ATTN_W), tile(KV_W), tile(KV_W), halo, halo,
                  halo_next, halo_next,
                  tile(ATTN_W), tile(CONV_W), tile(CONV_W), tile(CONV_W), next8, ch, CONV_SPEC,
                  pl.BlockSpec((IN_W, D_MODEL), lambda i: (0, 0), pipeline_mode=pl.Buffered(1)),
                  tile(128), tile(128), tile(128)],
        out_specs=(tile(D_MODEL), pl.BlockSpec(memory_space=pl.ANY), whole(SMALL_ROWS, D_MODEL)),
        scratch_shapes=[pltpu.VMEM((tb, IN_W), BF16), pltpu.VMEM((IN_W, D_MODEL), F32)],
        compiler_params=_params(("arbitrary",)),
    )(x, norm_g, dh2, dq, dk, dv, dkh, dvh, dkh, dvh, dga, db, dgc, dcv, dcv, rest, conv_w, w_in_t,
      rope_c, rope_a, rope_b)


def _adamw_step(w, g, m, v):
    m2 = ADAM_B1 * m + (1.0 - ADAM_B1) * g
    v2 = ADAM_B2 * v + (1.0 - ADAM_B2) * jnp.square(g)
    m_hat = m2 / (1.0 - ADAM_B1 ** ADAM_STEP)
    v_hat = v2 / (1.0 - ADAM_B2 ** ADAM_STEP)
    return -ADAM_LR * (m_hat / (jnp.sqrt(v_hat) + ADAM_EPS) + ADAM_WD * w), m2, v2


def _adamw_weights(groups):
    steps = 4

    def body(*refs):
        ins, outs = refs[:4 * len(groups)], refs[4 * len(groups):]
        for k in range(len(groups)):
            res = _adamw_step(*(r[...] for r in ins[4 * k:4 * k + 4]))
            for o_ref, val in zip(outs[3 * k:3 * k + 3], res):
                o_ref[...] = val

    in_specs, out_specs, out_shape = [], [], []
    for w, _, _, _ in groups:
        rows, cols = w.shape
        spec = pl.BlockSpec((rows // steps, cols), lambda i: (i, 0))
        in_specs += [spec] * 4
        out_specs += [spec] * 3
        out_shape += [jax.ShapeDtypeStruct((rows, cols), F32)] * 3
    flat = pl.pallas_call(
        body, name="adamw_weights", grid=(steps,), out_shape=tuple(out_shape), in_specs=in_specs,
        out_specs=tuple(out_specs), compiler_params=_params(("arbitrary",)),
    )(*[a for grp in groups for a in grp])
    return [flat[3 * k:3 * k + 3] for k in range(len(groups))]


def _adamw_small(chip, small, params, m, v):
    def body(chip_ref, small_ref, conv_ref, *refs):
        ins, outs = refs[:12], refs[12:]
        outs[0][...] = jnp.sum(small_ref[6:7, :], axis=-1, keepdims=True)
        grads = (small_ref[0:1, :], small_ref[1:2, :], conv_ref[2:5, :], small_ref[5:6, 0:8])
        for k, g in enumerate(grads):
            outs[1 + k][...] = g
            res = _adamw_step(ins[k][...], g, ins[4 + k][...], ins[8 + k][...])
            for n, val in enumerate(res):
                outs[5 + 4 * n + k][...] = val

    full = lambda a: pl.BlockSpec(a.shape, lambda i, c: (0,) * len(a.shape))
    shapes = [jax.ShapeDtypeStruct(p.shape, F32) for p in params]
    outs = [jax.ShapeDtypeStruct((1, 1), F32)] + shapes * 4
    flat = pl.pallas_call(
        body, name="adamw_small",
        grid_spec=pltpu.PrefetchScalarGridSpec(
            num_scalar_prefetch=1, grid=(1,),
            in_specs=[full(small), pl.BlockSpec((SMALL_ROWS, 128), lambda i, c: (0, c[0]))]
            + [full(a) for a in (*params, *m, *v)],
            out_specs=tuple(full(s) for s in outs)),
        out_shape=tuple(outs), compiler_params=_params(("arbitrary",)),
    )(chip, small, small, *params, *m, *v)
    return flat[0], flat[1:5], [flat[5 + 4 * n:9 + 4 * n] for n in range(3)]


def kernel(x, norm_g, w_in, sinks, conv_w, w_out, final_g, loss_target, m_norm_g, m_w_in, m_sinks, m_conv_w, m_w_out, m_final_g, v_norm_g, v_w_in, v_sinks, v_conv_w, v_w_out, v_final_g):
    seq = x.shape[1]
    x2 = x.reshape(seq, D_MODEL)
    tgt = loss_target.reshape(seq, D_MODEL)
    ng = norm_g.reshape(1, D_MODEL)
    fg = final_g.reshape(1, D_MODEL)
    chip = 2 * lax.axis_index("x") + lax.axis_index("y")

    conv_w8 = jnp.zeros((8, 128), F32).at[0:3].set(conv_w)
    w_in_full = _gather_w_in(w_in.T).reshape(IN_W, D_MODEL)

    q, kd, vd, rest, rope_c, rope_a, rope_b, wo_all, cw_all = _fwd_proj(x2, ng, w_in_full, w_out, conv_w8)
    w_out_full = wo_all.reshape(D_MODEL, D_MODEL)
    probs, d_attn, prod, dga, db, dgc, dcv, dh2, g_wo, g_wo_b, small_m = _fwd_mix(
        x2, q, kd, vd, rest, sinks, cw_all, w_out_full, fg, tgt)
    out_blocks = lambda t: t.reshape(N_CHIPS, W_OUT_BLK, D_MODEL)
    dq, dk, dv, dkh, dvh, grad_w_out = _bwd_mix(
        q, kd, vd, probs, d_attn, prod, rope_c, rope_a, rope_b, out_blocks(g_wo), out_blocks(g_wo_b))
    grad_x, g_wi, small_p = _bwd_proj(x2, ng, dh2, dq, dk, dv, dkh, dvh, dga, db, dgc, dcv, rest, cw_all,
                                      w_in_full, rope_c, rope_a, rope_b)

    g_in_blocks = g_wi.reshape(N_CHIPS, W_IN_BLK, D_MODEL)
    grad_w_in_t, small = _reduce_grads(g_in_blocks, small_m, small_p)

    (upd_wi, upd_wo) = _adamw_weights([(w_in.T, grad_w_in_t, m_w_in.T, v_w_in.T),
                                       (w_out, grad_w_out, m_w_out, v_w_out)])
    row = lambda t: t.reshape(1, -1)
    loss, grads_s, upd_s = _adamw_small(
        chip.reshape(1), small, (ng, fg, conv_w, row(sinks)),
        (row(m_norm_g), row(m_final_g), m_conv_w, row(m_sinks)),
        (row(v_norm_g), row(v_final_g), v_conv_w, row(v_sinks)))

    def named(ng_, fg_, cw_, sk_, wi_t, wo_):
        return [ng_.reshape(D_MODEL), wi_t.T, sk_.reshape(8), cw_, wo_, fg_.reshape(D_MODEL)]

    g_named = named(*grads_s, grad_w_in_t, grad_w_out)
    out = [loss.reshape(()), grad_x.reshape(1, seq, D_MODEL)] + g_named
    for n in range(3):
        out += named(*upd_s[n], upd_wi[n], upd_wo[n])
    return tuple(out)
```

```python
import jax
import jax.numpy as jnp
from jax import lax
from jax.experimental import pallas as pl
from jax.experimental.pallas import tpu as pltpu

F32 = jnp.float32
BF16 = jnp.bfloat16

D_MODEL = 1024
HEAD_DIM = 64
ATTN_W = 512
KV_W = 128
CONV_W = 512
IN_W = 3328
REST_W = IN_W - ATTN_W - 2 * KV_W
BLOCK = 128
ROT_DIM = 16
ROPE_THETA = 500000.0
EPS = 1e-5
SCALE = 0.125
NEG = -1e30

N_CHIPS = 4
W_IN_BLK = IN_W // N_CHIPS
W_OUT_BLK = D_MODEL // N_CHIPS

ADAM_LR = 0.001
ADAM_B1 = 0.9
ADAM_B2 = 0.999
ADAM_EPS = 1e-08
ADAM_WD = 0.01
ADAM_STEP = 10

VMEM_LIMIT = 60 * 1024 * 1024
T_PROJ = 512
T_FMIX = 512
T_MIX = 512
SMALL_ROWS = 8
MESH = pl.DeviceIdType.MESH

_NT = (((1,), (1,)), ((), ()))
_TN = (((0,), (0,)), ((), ()))


def _params(sem=None):
    kw = dict(vmem_limit_bytes=VMEM_LIMIT)
    if sem is not None:
        kw["dimension_semantics"] = sem
    return pltpu.CompilerParams(**kw)


def _sigmoid(t):
    return 1.0 / (1.0 + jnp.exp(-t))


def _shift_down(t, prev8, k):
    rolled = pltpu.roll(t, k, 0)
    row = lax.broadcasted_iota(jnp.int32, t.shape, 0)
    for j in range(k):
        rolled = jnp.where(row == j, prev8[8 - k + j:8 - k + j + 1, :], rolled)
    return rolled


def _shift_up(t, next8, k):
    n = t.shape[0]
    rolled = pltpu.roll(t, n - k, 0)
    row = lax.broadcasted_iota(jnp.int32, t.shape, 0)
    for j in range(k):
        rolled = jnp.where(row == n - k + j, next8[j:j + 1, :], rolled)
    return rolled


def _rope(t, c, a, b):
    w = t.shape[1]
    reps = w // 128
    if reps > 1:
        c, a, b = (jnp.concatenate([z] * reps, axis=1) for z in (c, a, b))
    return t * c + pltpu.roll(t, w - 8, 1) * a + pltpu.roll(t, 8, 1) * b


def _lane_lo(shape):
    return lax.broadcasted_iota(jnp.int32, shape, 1) < HEAD_DIM


def _stack_heads(t, g):
    lo = _lane_lo((BLOCK, 128))
    parts = []
    for hh in range(4):
        pair = t[:, 256 * g + 128 * (hh // 2):256 * g + 128 * (hh // 2) + 128]
        keep = lo if hh % 2 == 0 else jnp.logical_not(lo)
        parts.append(jnp.where(keep, pair, jnp.zeros_like(pair)))
    return jnp.concatenate(parts, axis=0)


def _unstack_pair(o, pp):
    lo = _lane_lo((BLOCK, 128))
    return jnp.where(lo, o[256 * pp:256 * pp + 128], o[256 * pp + 128:256 * pp + 256])


def _sink_col(sinks_ref, g):
    r = lax.broadcasted_iota(jnp.int32, (4 * BLOCK, 1), 0) // BLOCK
    col = jnp.full((4 * BLOCK, 1), sinks_ref[4 * g + 3], F32)
    for hh in range(3):
        col = jnp.where(r == hh, sinks_ref[4 * g + hh], col)
    return col


def _upper():
    r = lax.broadcasted_iota(jnp.int32, (4 * BLOCK, BLOCK), 0) % BLOCK
    return lax.broadcasted_iota(jnp.int32, (4 * BLOCK, BLOCK), 1) > r


def _fold(t):
    return jnp.where(_upper(), t[:, 0:BLOCK], t[:, BLOCK:])


def _unfold(t):
    zero = jnp.zeros_like(t)
    return jnp.concatenate([jnp.where(_upper(), t, zero), jnp.where(_upper(), zero, t)], axis=1)


def _softmax(s, sink_col, has_prev):
    if has_prev is not True:
        s = jnp.concatenate([jnp.where(has_prev, s[:, 0:BLOCK], NEG), s[:, BLOCK:]], axis=1)
    f = _fold(s)
    m = jnp.maximum(jnp.max(f, axis=-1, keepdims=True), sink_col)
    p = jnp.exp(f - m)
    es = jnp.exp(sink_col - m)
    inv = 1.0 / (jnp.sum(p, axis=-1, keepdims=True) + es)
    return (p * inv).astype(BF16), es * inv


def _key_windows(i, nsub, kd_ref, vd_ref, kdp_ref, vdp_ref):
    out = []
    for sb in range(nsub):
        rows = slice(BLOCK * sb, BLOCK * (sb + 1))
        if sb == 0:
            kk = jnp.concatenate([kdp_ref[...], kd_ref[rows, :]], axis=0)
            vv = jnp.concatenate([vdp_ref[...], vd_ref[rows, :]], axis=0)
            out.append((rows, kk, vv, i > 0))
        else:
            both = slice(BLOCK * (sb - 1), BLOCK * (sb + 1))
            out.append((rows, kd_ref[both, :], vd_ref[both, :], True))
    return out


def _gather_w_in(w_in_t):
    hi = W_IN_BLK // 2
    qr = hi // 2

    def body(wi_ref, wi_all, send_sems, recv_sems):
        x, y, c = lax.axis_index("x"), lax.axis_index("y"), lax.axis_index("c")
        me, sibling = (x, y), (x, y, 1 - c)
        xnb, ynb, diag = (1 - x, y), (x, 1 - y), (1 - x, 1 - y)

        wi_all[2 * x + y] = wi_ref[...].astype(BF16)

        def copy(k, chip, half, quarter, to):
            r = wi_all.at[2 * chip[0] + chip[1], pl.ds(half * hi + quarter * qr, qr)]
            return pltpu.make_async_remote_copy(src_ref=r, dst_ref=r, send_sem=send_sems.at[k],
                                                recv_sem=recv_sems.at[k], device_id=to, device_id_type=MESH)

        plan = [(0, xnb, 0), (1, ynb, 1), (2, xnb, 1), (3, ynb, 0)]
        sent = [copy(k, me, c, quarter, (*nb, c)) for k, nb, quarter in plan]
        for cp in sent:
            cp.start()
        arrivals = [(0, xnb, 0), (1, ynb, 1), (2, xnb, 1), (3, ynb, 0), (4, diag, 0), (5, diag, 1)]
        relay = {0: (4, ynb), 1: (5, xnb)}
        for k, chip, quarter in arrivals:
            copy(k, chip, c, quarter, (x, y, c)).wait_recv()
            if k in relay:
                sent.append(copy(relay[k][0], chip, c, quarter, (*relay[k][1], c)))
                sent[-1].start()
            sent.append(copy(6 + k, chip, c, quarter, sibling))
            sent[-1].start()
        for k, chip, quarter in arrivals:
            copy(6 + k, chip, 1 - c, quarter, (x, y, c)).wait_recv()
        for cp in sent:
            cp.wait_send()

    vmem = pl.BlockSpec(memory_space=pltpu.VMEM)
    return pl.pallas_call(
        body, name="gather_w_in",
        out_shape=jax.ShapeDtypeStruct((N_CHIPS, W_IN_BLK, D_MODEL), BF16),
        in_specs=[vmem], out_specs=vmem,
        scratch_shapes=[pltpu.SemaphoreType.DMA((12,)), pltpu.SemaphoreType.DMA((12,))],
        compiler_params=_params(),
    )(w_in_t)


def _reduce_grads(g_in, *smalls):
    hi = W_IN_BLK // 2
    qr = hi // 2
    q0, q1 = slice(0, qr), slice(qr, hi)

    def body(gi_hbm, s0_ref, s1_ref, gi_out, small_out,
             mine_i, sib_i, out_i, ici_i, small_in, small_ref, send_sems, recv_sems, local_sems):
        x, y, c = lax.axis_index("x"), lax.axis_index("y"), lax.axis_index("c")
        my_dev = 4 * x + 2 * y + c
        sibling = (x, y, 1 - c)
        xnb, ynb = (1 - x, y, c), (x, 1 - y, c)
        order = [(1 - x, 1 - y), (1 - x, y), (x, 1 - y), (x, y)]

        def remote(k, src, dst, to):
            return pltpu.make_async_remote_copy(src_ref=src, dst_ref=dst, send_sem=send_sems.at[k],
                                                recv_sem=recv_sems.at[k], device_id=to, device_id_type=MESH)

        small_ref[...] = s0_ref[...] + s1_ref[...]
        small_cps = []
        for f in range(1, 8):
            fx, fy, fc = f >> 2, (f >> 1) & 1, f & 1
            small_cps.append(remote(11 + f - 1, small_ref, small_in.at[f - 1], (x ^ fx, y ^ fy, c ^ fc)))
        for cp in small_cps:
            cp.start()

        own, to_sib = [], []
        for n, chip in enumerate(order):
            j = 2 * chip[0] + chip[1]
            own.append(pltpu.make_async_copy(gi_hbm.at[j, pl.ds(c * hi, hi)], mine_i.at[n], local_sems.at[n]))
            to_sib.append(remote(6 + n, gi_hbm.at[j, pl.ds((1 - c) * hi, hi)], sib_i.at[n], sibling))
            own[-1].start()
            to_sib[-1].start()

        def pair_sum(n):
            own[n].wait()
            to_sib[n].wait_recv()
            return mine_i[n] + sib_i[n]

        ici = [remote(k, out_i.at[k], ici_i.at[k], xnb if k % 2 == 0 else ynb) for k in range(6)]

        def send(k, rows_f32):
            out_i[k] = rows_f32.astype(BF16)
            ici[k].start()

        p_diag = pair_sum(0)
        send(0, p_diag[q0])
        send(1, p_diag[q1])
        p_x = pair_sum(1)
        send(2, p_x[q0])
        p_y = pair_sum(2)
        send(3, p_y[q1])
        ici[0].wait_recv()
        send(5, p_y[q0] + ici_i[0].astype(F32))
        ici[1].wait_recv()
        send(4, p_x[q1] + ici_i[1].astype(F32))
        p_mine = pair_sum(3)
        for k in range(2, 6):
            ici[k].wait_recv()
        gi_out[pl.ds(c * hi, qr), :] = p_mine[q0] + ici_i[2].astype(F32) + ici_i[5].astype(F32)
        gi_out[pl.ds(c * hi + qr, qr), :] = p_mine[q1] + ici_i[4].astype(F32) + ici_i[3].astype(F32)

        swap = [remote(10, gi_out.at[pl.ds(c * hi, hi)], gi_out.at[pl.ds(c * hi, hi)], sibling)]
        for cp in swap:
            cp.start()

        for cp in small_cps:
            cp.wait_recv()
        total = jnp.zeros((SMALL_ROWS, D_MODEL), F32)
        for d in range(8):
            slot = jnp.maximum((d ^ my_dev) - 1, 0)
            total = total + jnp.where(d == my_dev, small_ref[...], small_in[slot])
        small_out[...] = total

        remote(10, gi_out.at[pl.ds((1 - c) * hi, hi)], gi_out.at[pl.ds((1 - c) * hi, hi)], sibling).wait_recv()
        for cp in to_sib + ici + swap + small_cps:
            cp.wait_send()

    vmem = pl.BlockSpec(memory_space=pltpu.VMEM)
    anyspace = pl.BlockSpec(memory_space=pl.ANY)
    return pl.pallas_call(
        body, name="reduce_grads",
        out_shape=(jax.ShapeDtypeStruct((W_IN_BLK, D_MODEL), F32), jax.ShapeDtypeStruct((SMALL_ROWS, D_MODEL), F32)),
        in_specs=[anyspace, vmem, vmem], out_specs=(vmem, vmem),
        scratch_shapes=[pltpu.VMEM((N_CHIPS, hi, D_MODEL), F32), pltpu.VMEM((N_CHIPS, hi, D_MODEL), F32),
                        pltpu.VMEM((6, qr, D_MODEL), BF16), pltpu.VMEM((6, qr, D_MODEL), BF16),
                        pltpu.VMEM((7, SMALL_ROWS, D_MODEL), F32), pltpu.VMEM((SMALL_ROWS, D_MODEL), F32),
                        pltpu.SemaphoreType.DMA((18,)), pltpu.SemaphoreType.DMA((18,)),
                        pltpu.SemaphoreType.DMA((4,))],
        compiler_params=_params(),
    )(g_in, *smalls)


def _fwd_proj(x, norm_g, w_in_t, w_out, conv_w8):
    seq = x.shape[0]
    nt = seq // T_PROJ
    lane = jnp.arange(128, dtype=jnp.int32) % HEAD_DIM
    inv_freq = ROPE_THETA ** (-(2 * (lane % 8)).astype(F32) / ROT_DIM)
    inv_freq = jnp.where(lane < ROT_DIM, inv_freq, 0.0).reshape(1, 128)
    in_tile = jnp.arange(T_PROJ, dtype=jnp.int32).astype(F32)[:, None] * inv_freq
    cos_in, sin_in = jnp.cos(in_tile), jnp.sin(in_tile)
    start = jnp.repeat((jnp.arange(nt, dtype=jnp.int32) * T_PROJ).astype(F32), 8)[:, None] * inv_freq
    cos_st, sin_st = jnp.cos(start), jnp.sin(start)

    def body(x_ref, g_ref, w_ref, cs_ref, ss_ref, ci_ref, si_ref, wo_ref, cw_ref,
             q_ref, kd_ref, vd_ref, rest_ref, c_ref, a_ref, b_ref, wo_all, cw_all,
             wo_stage, send_sems, recv_sems, local_sems):
        i = pl.program_id(0)
        mx, my, mc = lax.axis_index("x"), lax.axis_index("y"), lax.axis_index("c")
        chips = [(1 - mx, my), (mx, 1 - my), (1 - mx, 1 - my)]

        def gather(blocks):
            cps = []
            for k, chip in enumerate(chips):
                for n, (src, dst) in enumerate(((wo_stage, wo_all), (cw_ref, cw_all))):
                    cps.append(pltpu.make_async_remote_copy(
                        src_ref=src, dst_ref=dst.at[blocks[k]], send_sem=send_sems.at[2 * k + n],
                        recv_sem=recv_sems.at[2 * k + n], device_id=(*chip, mc), device_id_type=MESH))
            return cps

        me = 2 * mx + my
        own = [pltpu.make_async_copy(wo_stage, wo_all.at[me], local_sems.at[0]),
               pltpu.make_async_copy(cw_ref, cw_all.at[me], local_sems.at[1])]

        @pl.when(i == 0)
        def _():
            wo_stage[...] = wo_ref[...].astype(BF16)
            for cp in own + gather([me] * 3):
                cp.start()

        xf = x_ref[...]
        r1 = lax.rsqrt(jnp.mean(xf * xf, axis=-1, keepdims=True) + EPS)
        xn = (xf * r1 * g_ref[...]).astype(BF16)
        cs, ss = cs_ref[0:1, :], ss_ref[0:1, :]
        c = cs * ci_ref[...] - ss * si_ref[...]
        sin = ss * ci_ref[...] + cs * si_ref[...]
        j = lax.broadcasted_iota(jnp.int32, (T_PROJ, 128), 1) % HEAD_DIM
        a = jnp.where(j < 8, -sin, 0.0)
        b = jnp.where(j >= 8, sin, 0.0)
        c_ref[...], a_ref[...], b_ref[...] = c, a, b
        proj = lambda lo_c, w: lax.dot_general(xn, w_ref[lo_c:lo_c + w, :], _NT, preferred_element_type=F32)
        q_ref[...] = (_rope(proj(0, ATTN_W), c, a, b) * SCALE).astype(BF16)
        kv = proj(ATTN_W, 2 * KV_W)
        k = _rope(kv[:, 0:KV_W], c, a, b)
        v = kv[:, KV_W:2 * KV_W]
        lo = _lane_lo(k.shape)
        for t, ref in ((k, kd_ref), (v, vd_ref)):
            sw = pltpu.roll(t, HEAD_DIM, 1)
            ref[:, 0:128] = jnp.where(lo, t, sw).astype(BF16)
            ref[:, 128:256] = jnp.where(lo, sw, t).astype(BF16)
        for n in range(REST_W // 512):
            rest_ref[:, 512 * n:512 * (n + 1)] = proj(ATTN_W + 2 * KV_W + 512 * n, 512)

        @pl.when(i == nt - 1)
        def _():
            sent = gather([me] * 3)
            for cp in gather([2 * chip[0] + chip[1] for chip in chips]):
                cp.wait_recv()
            for cp in sent:
                cp.wait_send()
            for cp in own:
                cp.wait()

    tile = lambda w: pl.BlockSpec((T_PROJ, w), lambda i: (i, 0))
    whole = lambda r, w: pl.BlockSpec((r, w), lambda i: (0, 0))
    vmem = pl.BlockSpec(memory_space=pltpu.VMEM)
    hbm = pl.BlockSpec(memory_space=pl.ANY)
    return pl.pallas_call(
        body, name="fwd_proj", grid=(nt,),
        out_shape=(jax.ShapeDtypeStruct((seq, ATTN_W), BF16), jax.ShapeDtypeStruct((seq, 2 * KV_W), BF16),
                   jax.ShapeDtypeStruct((seq, 2 * KV_W), BF16), jax.ShapeDtypeStruct((seq, REST_W), F32))
        + (jax.ShapeDtypeStruct((seq, 128), F32),) * 3
        + (jax.ShapeDtypeStruct((N_CHIPS, W_OUT_BLK, D_MODEL), BF16), jax.ShapeDtypeStruct((N_CHIPS, 8, 128), F32)),
        in_specs=[tile(D_MODEL), whole(1, D_MODEL), whole(IN_W, D_MODEL), pl.BlockSpec((8, 128), lambda i: (i, 0)),
                  pl.BlockSpec((8, 128), lambda i: (i, 0)), whole(T_PROJ, 128), whole(T_PROJ, 128), vmem, vmem],
        out_specs=(tile(ATTN_W), tile(2 * KV_W), tile(2 * KV_W), tile(REST_W), tile(128), tile(128), tile(128),
                   hbm, hbm),
        scratch_shapes=[pltpu.VMEM((W_OUT_BLK, D_MODEL), BF16), pltpu.SemaphoreType.DMA((6,)),
                        pltpu.SemaphoreType.DMA((6,)), pltpu.SemaphoreType.DMA((2,))],
        compiler_params=_params(("arbitrary",)),
    )(x, norm_g, w_in_t, cos_st, sin_st, cos_in, sin_in, w_out, conv_w8)


CONV_SPEC = pl.BlockSpec((N_CHIPS, 8, 128), lambda i: (0, 0, 0))


def _conv_rows(cw_ref):
    return jnp.concatenate([cw_ref[j] for j in range(N_CHIPS)], axis=1)


def _conv_parts(rest_ref, prev_ref, cw_ref, first):
    u = rest_ref[:, 1024:1536] * rest_ref[:, 1536:2048]
    up = prev_ref[:, 1024:1536] * prev_ref[:, 1536:2048]
    up = jnp.where(first, jnp.zeros_like(up), up)
    um1 = _shift_down(u, up, 1)
    um2 = _shift_down(u, up, 2)
    cw = _conv_rows(cw_ref)
    cv = cw[0:1, :] * um2 + cw[1:2, :] * um1 + cw[2:3, :] * u
    return u, um1, um2, cv


def _fwd_mix(x, q, kd, vd, rest, sinks, conv_w, w_out, final_g, target):
    seq = x.shape[0]
    nt = seq // T_FMIX
    nsub = T_FMIX // BLOCK

    def body(sinks_ref, x_ref, q_ref, kd_ref, vd_ref, kdp_ref, vdp_ref, rest_ref, restp_ref, cw_ref, wo_ref,
             fg_ref, tgt_ref, prob_ref, do_ref, prod_ref, dga_ref, db_ref, dgc_ref, dcv_ref, dh2_ref,
             gwo_hbm, gwob_hbm, small_ref,
             pmix_ref, pdh2_ref, gwo_ref, gate_ref, dsink_ref):
        i = pl.program_id(0)
        attn_ref, sinkw_ref = prod_ref, dcv_ref

        @pl.when(i == 0)
        def _():
            small_ref[...] = jnp.zeros_like(small_ref)
            dsink_ref[...] = jnp.zeros_like(dsink_ref)
            gwo_ref[...] = jnp.zeros_like(gwo_ref)
            pmix_ref[...] = jnp.zeros_like(pmix_ref)
            pdh2_ref[...] = jnp.zeros_like(pdh2_ref)

        chains = []
        for rows, kk, vv, has_prev in _key_windows(i, nsub, kd_ref, vd_ref, kdp_ref, vdp_ref):
            qt = q_ref[rows, :]
            for g in range(2):
                kg = kk[:, 128 * g:128 * (g + 1)]
                chains.append(dict(g=g, rows=rows, has_prev=has_prev, vg=vv[:, 128 * g:128 * (g + 1)],
                                   s=lax.dot_general(_stack_heads(qt, g), kg, _NT, preferred_element_type=F32)))
        gwo_ref[...] += lax.dot_general(pmix_ref[...], pdh2_ref[...], _TN, preferred_element_type=F32)
        for ch in chains:
            ch["prob"], ch["psink"] = _softmax(ch.pop("s"), _sink_col(sinks_ref, ch["g"]), ch["has_prev"])
        for k, ch in enumerate(chains):
            prob_ref[k] = ch["prob"]
            o = jnp.dot(_unfold(ch["prob"]), ch["vg"], preferred_element_type=F32)
            ow = o * ch["psink"]
            for pp in range(2):
                lanes = slice(256 * ch["g"] + 128 * pp, 256 * ch["g"] + 128 * (pp + 1))
                attn_ref[ch["rows"], lanes] = _unstack_pair(o, pp)
                sinkw_ref[ch["rows"], lanes] = _unstack_pair(ow, pp)

        def silu_parts(t, lo_c):
            sg = _sigmoid(t)
            silu = t * sg
            gate_ref[:, lo_c:lo_c + 512] = silu
            gate_ref[:, lo_c + 512:lo_c + 1024] = sg * (1.0 + t * (1.0 - sg))
            return silu

        pmix_ref[:, 0:ATTN_W] = (attn_ref[...] * silu_parts(rest_ref[:, 0:512], 0)).astype(BF16)
        u, um1, um2, cv = _conv_parts(rest_ref, restp_ref, cw_ref, i == 0)
        pmix_ref[:, ATTN_W:] = (rest_ref[:, 512:1024] * cv * silu_parts(rest_ref[:, 2048:2560], 1024)).astype(BF16)

        h2 = x_ref[...] + jnp.dot(pmix_ref[...], wo_ref[...], preferred_element_type=F32)
        r2 = lax.rsqrt(jnp.mean(h2 * h2, axis=-1, keepdims=True) + EPS)
        n2 = h2 * r2
        err = n2 * fg_ref[...] - tgt_ref[...]
        dy = err * (1.0 / D_MODEL)
        small_ref[6:7, :] += jnp.sum(err * err, axis=0, keepdims=True) * (0.5 / D_MODEL)
        small_ref[1:2, :] += jnp.sum(dy * n2, axis=0, keepdims=True)
        dn = dy * fg_ref[...]
        dh2 = r2 * (dn - n2 * jnp.mean(dn * n2, axis=-1, keepdims=True))
        dh2_ref[...] = dh2
        pdh2_ref[...] = dh2.astype(BF16)

        d_mix = lambda lo_r: lax.dot_general(pdh2_ref[...], wo_ref[lo_r:lo_r + 512, :], _NT, preferred_element_type=F32)
        dma = d_mix(0)
        dga_ref[...] = (dma * attn_ref[...] * gate_ref[:, 512:1024]).astype(BF16)
        d_attn = dma * gate_ref[:, 0:512]
        do_ref[...] = d_attn.astype(BF16)
        prod_ref[...] = d_attn * attn_ref[...]
        dsink_ref[0:1, :] += jnp.sum(d_attn * sinkw_ref[...], axis=0, keepdims=True)
        bg = rest_ref[:, 512:1024]
        dmc = d_mix(ATTN_W)
        t1 = dmc * gate_ref[:, 1024:1536]
        db_ref[...] = (t1 * cv).astype(BF16)
        dcv = t1 * bg
        dcv_ref[...] = dcv
        dgc_ref[...] = (dmc * (bg * cv) * gate_ref[:, 1536:2048]).astype(BF16)
        small_ref[2:3, 0:CONV_W] += jnp.sum(dcv * um2, axis=0, keepdims=True)
        small_ref[3:4, 0:CONV_W] += jnp.sum(dcv * um1, axis=0, keepdims=True)
        small_ref[4:5, 0:CONV_W] += jnp.sum(dcv * u, axis=0, keepdims=True)

        @pl.when(i == nt - 1)
        def _():
            head = lax.broadcasted_iota(jnp.int32, (1, ATTN_W), 1) // HEAD_DIM
            for h in range(8):
                tot = jnp.sum(jnp.where(head == h, dsink_ref[0:1, :], 0.0), axis=-1, keepdims=True)
                small_ref[5:6, h:h + 1] = -tot
            gwo_ref[...] += lax.dot_general(pmix_ref[...], pdh2_ref[...], _TN, preferred_element_type=F32)
            pltpu.sync_copy(gwo_ref, gwo_hbm)
            for n in range(D_MODEL // T_FMIX):
                slab = slice(T_FMIX * n, T_FMIX * (n + 1))
                pmix_ref[...] = gwo_ref[slab, :].astype(BF16)
                pltpu.sync_copy(pmix_ref, gwob_hbm.at[slab])

    tile = lambda w: pl.BlockSpec((T_FMIX, w), lambda i: (i, 0))
    whole = lambda r, w: pl.BlockSpec((r, w), lambda i: (0, 0))
    prev_blk = pl.BlockSpec((BLOCK, 2 * KV_W), lambda i: (jnp.maximum(i * nsub - 1, 0), 0))
    prev8 = pl.BlockSpec((8, REST_W), lambda i: (jnp.maximum(i * (T_FMIX // 8) - 1, 0), 0))
    bf = lambda w: jax.ShapeDtypeStruct((seq, w), BF16)
    f32 = lambda w: jax.ShapeDtypeStruct((seq, w), F32)
    return pl.pallas_call(
        body, name="fwd_mix", grid=(nt,),
        out_shape=(jax.ShapeDtypeStruct((2 * seq // BLOCK, 4 * BLOCK, BLOCK), BF16),
                   bf(ATTN_W), f32(ATTN_W), bf(ATTN_W), bf(CONV_W), bf(CONV_W), f32(CONV_W), f32(D_MODEL),
                   jax.ShapeDtypeStruct((D_MODEL, D_MODEL), F32), jax.ShapeDtypeStruct((D_MODEL, D_MODEL), BF16),
                   jax.ShapeDtypeStruct((SMALL_ROWS, D_MODEL), F32)),
        in_specs=[pl.BlockSpec(memory_space=pltpu.SMEM), tile(D_MODEL), tile(ATTN_W), tile(2 * KV_W), tile(2 * KV_W),
                  prev_blk, prev_blk, tile(REST_W), prev8, CONV_SPEC,
                  pl.BlockSpec((D_MODEL, D_MODEL), lambda i: (0, 0), pipeline_mode=pl.Buffered(1)),
                  whole(1, D_MODEL), tile(D_MODEL)],
        out_specs=(pl.BlockSpec((2 * nsub, 4 * BLOCK, BLOCK), lambda i: (i, 0, 0)),
                   tile(ATTN_W), tile(ATTN_W), tile(ATTN_W), tile(CONV_W), tile(CONV_W), tile(CONV_W), tile(D_MODEL),
                   pl.BlockSpec(memory_space=pl.ANY), pl.BlockSpec(memory_space=pl.ANY), whole(SMALL_ROWS, D_MODEL)),
        scratch_shapes=[pltpu.VMEM((T_FMIX, D_MODEL), BF16)] * 2 + [
            pltpu.VMEM((D_MODEL, D_MODEL), F32), pltpu.VMEM((T_FMIX, 4 * 512), F32), pltpu.VMEM((8, ATTN_W), F32)],
        compiler_params=_params(("arbitrary",)),
    )(sinks, x, q, kd, vd, kd, vd, rest, rest, conv_w, w_out, final_g, target)


def _scatter_copies(g_hbm, gb_hbm, mine, land, send_sems, recv_sems, local_sem, half):
    x, y, c = lax.axis_index("x"), lax.axis_index("y"), lax.axis_index("c")
    cps = []
    for f in range(1, 8):
        to = (x ^ (f >> 2), y ^ ((f >> 1) & 1), c ^ (f & 1))
        src = gb_hbm.at[2 * to[0] + to[1], pl.ds(to[2] * half, half)]
        cps.append(pltpu.make_async_remote_copy(src_ref=src, dst_ref=land.at[f - 1], send_sem=send_sems.at[f - 1],
                                                recv_sem=recv_sems.at[f - 1], device_id=to, device_id_type=MESH))
    own = pltpu.make_async_copy(g_hbm.at[2 * x + y, pl.ds(c * half, half)], mine, local_sem)
    return cps, own


def _scatter_finish(cps, own, mine, land, out_hbm, send_sems, recv_sems, local_sem, half):
    x, y, c = lax.axis_index("x"), lax.axis_index("y"), lax.axis_index("c")
    own.wait()
    tot = mine[...]
    for f in range(1, 8):
        cps[f - 1].wait_recv()
        tot = tot + land[f - 1].astype(F32)
    mine[...] = tot

    def swap(rows_of):
        return pltpu.make_async_remote_copy(src_ref=mine, dst_ref=out_hbm.at[pl.ds(rows_of * half, half)],
                                            send_sem=send_sems.at[7], recv_sem=recv_sems.at[7],
                                            device_id=(x, y, 1 - c), device_id_type=MESH)

    keep = pltpu.make_async_copy(mine, out_hbm.at[pl.ds(c * half, half)], local_sem)
    keep.start()
    swap(c).start()
    swap(1 - c).wait_recv()
    keep.wait()
    for cp in cps:
        cp.wait_send()
    swap(c).wait_send()


def _bwd_mix(q, kd, vd, probs, d_attn, prod, rope_c, rope_a, rope_b, g_out, g_out_b):
    seq = q.shape[0]
    nt = seq // T_MIX
    nsub = T_MIX // BLOCK
    ho = W_OUT_BLK // 2

    def body(q_ref, kd_ref, vd_ref, kdp_ref, vdp_ref, prob_ref, do_ref, prod_ref, c_ref, a_ref, b_ref, go_hbm, gob_hbm,
             dq_ref, dk_ref, dv_ref, dkh_ref, dvh_ref, go_out,
             mine_o, land_o, send_sems, recv_sems, local_sems):
        i = pl.program_id(0)
        scatter = (mine_o, land_o, send_sems, recv_sems, local_sems.at[0], ho)

        @pl.when(i == 0)
        def _():
            cps, own = _scatter_copies(go_hbm, gob_hbm, *scatter)
            for cp in cps + [own]:
                cp.start()

        lo = _lane_lo((2 * BLOCK, 128))
        dk_blocks = [None] * (nsub + 1)
        dv_blocks = [None] * (nsub + 1)

        def add(lst, n, val):
            lst[n] = val if lst[n] is None else lst[n] + val

        chains = []
        for rows, kk, vv, _ in _key_windows(i, nsub, kd_ref, vd_ref, kdp_ref, vdp_ref):
            qt = q_ref[rows, :]
            dot = do_ref[rows, :]
            for g in range(2):
                rs = jnp.sum(_stack_heads(prod_ref[rows, :], g), axis=-1, keepdims=True)
                chains.append(dict(g=g, rows=rows, rs=rs, qs=_stack_heads(qt, g), dos=_stack_heads(dot, g),
                                   kg=kk[:, 128 * g:128 * (g + 1)], vg=vv[:, 128 * g:128 * (g + 1)]))
        for ch in chains:
            ch["dp"] = lax.dot_general(ch["dos"], ch["vg"], _NT, preferred_element_type=F32)
        for k, ch in enumerate(chains):
            ch["ds"] = _unfold((prob_ref[k].astype(F32) * (_fold(ch["dp"]) - ch["rs"])).astype(BF16))
        for k, ch in enumerate(chains):
            dqs = jnp.dot(ch["ds"], ch["kg"], preferred_element_type=F32) * SCALE
            c, a, b = c_ref[ch["rows"], :], a_ref[ch["rows"], :], b_ref[ch["rows"], :]
            for pp in range(2):
                lanes = slice(256 * ch["g"] + 128 * pp, 256 * ch["g"] + 128 * (pp + 1))
                dq_ref[ch["rows"], lanes] = _rope(_unstack_pair(dqs, pp), c, -a, -b).astype(BF16)
            dkd = lax.dot_general(ch["ds"], ch["qs"], _TN, preferred_element_type=F32)
            dvd = lax.dot_general(_unfold(prob_ref[k]), ch["dos"], _TN, preferred_element_type=F32)
            ch["dk"] = dkd + pltpu.roll(dkd, HEAD_DIM, 1)
            ch["dv"] = dvd + pltpu.roll(dvd, HEAD_DIM, 1)
        for sb in range(nsub):
            dk2 = jnp.where(lo, chains[2 * sb]["dk"], chains[2 * sb + 1]["dk"])
            dv2 = jnp.where(lo, chains[2 * sb]["dv"], chains[2 * sb + 1]["dv"])
            add(dk_blocks, sb, dk2[0:BLOCK])
            add(dk_blocks, sb + 1, dk2[BLOCK:])
            add(dv_blocks, sb, dv2[0:BLOCK])
            add(dv_blocks, sb + 1, dv2[BLOCK:])
        dkh_ref[0] = dk_blocks[0]
        dvh_ref[0] = dv_blocks[0]
        for sb in range(nsub):
            dk_ref[BLOCK * sb:BLOCK * (sb + 1), :] = dk_blocks[sb + 1]
            dv_ref[BLOCK * sb:BLOCK * (sb + 1), :] = dv_blocks[sb + 1]

        @pl.when(i == nt - 1)
        def _():
            cps, own = _scatter_copies(go_hbm, gob_hbm, *scatter)
            _scatter_finish(cps, own, mine_o, land_o, go_out, send_sems, recv_sems, local_sems.at[1], ho)

    tile = lambda w: pl.BlockSpec((T_MIX, w), lambda i: (i, 0))
    prev_blk = pl.BlockSpec((BLOCK, 2 * KV_W), lambda i: (jnp.maximum(i * nsub - 1, 0), 0))
    halo = pl.BlockSpec((1, BLOCK, KV_W), lambda i: (i, 0, 0))
    hbm = pl.BlockSpec(memory_space=pl.ANY)
    f32 = lambda w: jax.ShapeDtypeStruct((seq, w), F32)
    return pl.pallas_call(
        body, name="bwd_mix", grid=(nt,),
        out_shape=(jax.ShapeDtypeStruct((seq, ATTN_W), BF16), f32(KV_W), f32(KV_W),
                   jax.ShapeDtypeStruct((nt, BLOCK, KV_W), F32), jax.ShapeDtypeStruct((nt, BLOCK, KV_W), F32),
                   jax.ShapeDtypeStruct((W_OUT_BLK, D_MODEL), F32)),
        in_specs=[tile(ATTN_W), tile(2 * KV_W), tile(2 * KV_W), prev_blk, prev_blk,
                  pl.BlockSpec((2 * nsub, 4 * BLOCK, BLOCK), lambda i: (i, 0, 0)), tile(ATTN_W), tile(ATTN_W),
                  tile(128), tile(128), tile(128), hbm, hbm],
        out_specs=(tile(ATTN_W), tile(KV_W), tile(KV_W), halo, halo, hbm),
        scratch_shapes=[pltpu.VMEM((ho, D_MODEL), F32), pltpu.VMEM((7, ho, D_MODEL), BF16),
                        pltpu.SemaphoreType.DMA((8,)), pltpu.SemaphoreType.DMA((8,)), pltpu.SemaphoreType.DMA((2,))],
        compiler_params=_params(("arbitrary",)),
    )(q, kd, vd, kd, vd, probs, d_attn, prod, rope_c, rope_a, rope_b, g_out, g_out_b)


def _bwd_proj(x, norm_g, dh2, dq, dk, dv, dkh, dvh, dga, db, dgc, dcv, rest, conv_w, w_in_t, rope_c, rope_a, rope_b):
    seq = x.shape[0]
    tb = T_PROJ
    per = tb // T_MIX
    nt = seq // tb

    def body(x_ref, g_ref, dh2_ref, dq_ref, dk_ref, dv_ref, dkh_ref, dvh_ref, dkn_ref, dvn_ref, dga_ref, db_ref,
             dgc_ref, dcv_ref, dcvn_ref, ch_ref, cw_ref, w_ref, c_ref, a_ref, b_ref, gx_ref, gw_hbm, small_ref,
             dp_ref, acc_ref):
        i = pl.program_id(0)

        @pl.when(i == 0)
        def _():
            small_ref[...] = jnp.zeros_like(small_ref)
            acc_ref[...] = jnp.zeros_like(acc_ref)

        last = i == nt - 1
        keep = jnp.where(last, 0.0, 1.0)
        pad = jnp.zeros((T_MIX - BLOCK, KV_W), F32)

        def with_halos(main_ref, halo_ref, next_ref):
            parts = []
            for m in range(1, per + 1):
                parts += [pad, halo_ref[m] if m < per else next_ref[0] * keep]
            return main_ref[...] + jnp.concatenate(parts, axis=0)

        dk = with_halos(dk_ref, dkh_ref, dkn_ref)
        dv = with_halos(dv_ref, dvh_ref, dvn_ref)
        dp_ref[:, 0:ATTN_W] = dq_ref[...]
        dp_ref[:, ATTN_W:ATTN_W + KV_W] = _rope(dk, c_ref[...], -a_ref[...], -b_ref[...]).astype(BF16)
        dp_ref[:, ATTN_W + KV_W:ATTN_W + 2 * KV_W] = dv.astype(BF16)
        base = ATTN_W + 2 * KV_W
        dp_ref[:, base:base + 512] = dga_ref[...]
        dp_ref[:, base + 512:base + 1024] = db_ref[...]
        dcv = dcv_ref[...]
        nxt = dcvn_ref[...] * keep
        cw = _conv_rows(cw_ref)
        du = cw[2:3, :] * dcv + cw[1:2, :] * _shift_up(dcv, nxt, 1) + cw[0:1, :] * _shift_up(dcv, nxt, 2)
        dp_ref[:, base + 1024:base + 1536] = (du * ch_ref[:, 512:1024]).astype(BF16)
        dp_ref[:, base + 1536:base + 2048] = (du * ch_ref[:, 0:512]).astype(BF16)
        dp_ref[:, base + 2048:base + 2560] = dgc_ref[...]

        xf = x_ref[...]
        r1 = lax.rsqrt(jnp.mean(xf * xf, axis=-1, keepdims=True) + EPS)
        n1 = xf * r1
        xn = (n1 * g_ref[...]).astype(BF16)
        for n in range(IN_W // 256):
            cols = slice(256 * n, 256 * (n + 1))
            acc_ref[cols, :] += lax.dot_general(dp_ref[:, cols], xn, _TN, preferred_element_type=F32)
        dxn = jnp.dot(dp_ref[...], w_ref[...], preferred_element_type=F32)
        small_ref[0:1, :] += jnp.sum(dxn * n1, axis=0, keepdims=True)
        dxg = dxn * g_ref[...]
        gx_ref[...] = r1 * (dxg - n1 * jnp.mean(dxg * n1, axis=-1, keepdims=True)) + dh2_ref[...]

        @pl.when(last)
        def _():
            pltpu.sync_copy(acc_ref, gw_hbm)

    tile = lambda w: pl.BlockSpec((tb, w), lambda i: (i, 0))
    whole = lambda r, w: pl.BlockSpec((r, w), lambda i: (0, 0))
    halo = pl.BlockSpec((per, BLOCK, KV_W), lambda i: (i, 0, 0))
    halo_next = pl.BlockSpec((1, BLOCK, KV_W), lambda i: (jnp.minimum((i + 1) * per, seq // T_MIX - 1), 0, 0))
    next8 = pl.BlockSpec((8, CONV_W), lambda i: (jnp.minimum((i + 1) * (tb // 8), seq // 8 - 1), 0))
    ch = pl.BlockSpec((tb, 1024), lambda i: (i, 1))
    return pl.pallas_call(
        body, name="bwd_proj", grid=(nt,),
        out_shape=(jax.ShapeDtypeStruct((seq, D_MODEL), F32), jax.ShapeDtypeStruct((IN_W, D_MODEL), F32),
                   jax.ShapeDtypeStruct((SMALL_ROWS, D_MODEL), F32)),
        in_specs=[tile(D_MODEL), whole(1, D_MODEL), tile(D_MODEL), tile(ATTN_W), tile(KV_W), tile(KV_W), halo, halo,
                  halo_next, halo_next,
                  tile(ATTN_W), tile(CONV_W), tile(CONV_W), tile(CONV_W), next8, ch, CONV_SPEC,
                  pl.BlockSpec((IN_W, D_MODEL), lambda i: (0, 0), pipeline_mode=pl.Buffered(1)),
                  tile(128), tile(128), tile(128)],
        out_specs=(tile(D_MODEL), pl.BlockSpec(memory_space=pl.ANY), whole(SMALL_ROWS, D_MODEL)),
        scratch_shapes=[pltpu.VMEM((tb, IN_W), BF16), pltpu.VMEM((IN_W, D_MODEL), F32)],
        compiler_params=_params(("arbitrary",)),
    )(x, norm_g, dh2, dq, dk, dv, dkh, dvh, dkh, dvh, dga, db, dgc, dcv, dcv, rest, conv_w, w_in_t,
      rope_c, rope_a, rope_b)


def _adamw_step(w, g, m, v):
    m2 = ADAM_B1 * m + (1.0 - ADAM_B1) * g
    v2 = ADAM_B2 * v + (1.0 - ADAM_B2) * jnp.square(g)
    m_hat = m2 / (1.0 - ADAM_B1 ** ADAM_STEP)
    v_hat = v2 / (1.0 - ADAM_B2 ** ADAM_STEP)
    return -ADAM_LR * (m_hat / (jnp.sqrt(v_hat) + ADAM_EPS) + ADAM_WD * w), m2, v2


def _adamw_weights(groups):
    steps = 4

    def body(*refs):
        ins, outs = refs[:4 * len(groups)], refs[4 * len(groups):]
        for k in range(len(groups)):
            res = _adamw_step(*(r[...] for r in ins[4 * k:4 * k + 4]))
            for o_ref, val in zip(outs[3 * k:3 * k + 3], res):
                o_ref[...] = val

    in_specs, out_specs, out_shape = [], [], []
    for w, _, _, _ in groups:
        rows, cols = w.shape
        spec = pl.BlockSpec((rows // steps, cols), lambda i: (i, 0))
        in_specs += [spec] * 4
        out_specs += [spec] * 3
        out_shape += [jax.ShapeDtypeStruct((rows, cols), F32)] * 3
    flat = pl.pallas_call(
        body, name="adamw_weights", grid=(steps,), out_shape=tuple(out_shape), in_specs=in_specs,
        out_specs=tuple(out_specs), compiler_params=_params(("arbitrary",)),
    )(*[a for grp in groups for a in grp])
    return [flat[3 * k:3 * k + 3] for k in range(len(groups))]


def _adamw_small(chip, small, params, m, v):
    def body(chip_ref, small_ref, conv_ref, *refs):
        ins, outs = refs[:12], refs[12:]
        outs[0][...] = jnp.sum(small_ref[6:7, :], axis=-1, keepdims=True)
        grads = (small_ref[0:1, :], small_ref[1:2, :], conv_ref[2:5, :], small_ref[5:6, 0:8])
        for k, g in enumerate(grads):
            outs[1 + k][...] = g
            res = _adamw_step(ins[k][...], g, ins[4 + k][...], ins[8 + k][...])
            for n, val in enumerate(res):
                outs[5 + 4 * n + k][...] = val

    full = lambda a: pl.BlockSpec(a.shape, lambda i, c: (0,) * len(a.shape))
    shapes = [jax.ShapeDtypeStruct(p.shape, F32) for p in params]
    outs = [jax.ShapeDtypeStruct((1, 1), F32)] + shapes * 4
    flat = pl.pallas_call(
        body, name="adamw_small",
        grid_spec=pltpu.PrefetchScalarGridSpec(
            num_scalar_prefetch=1, grid=(1,),
            in_specs=[full(small), pl.BlockSpec((SMALL_ROWS, 128), lambda i, c: (0, c[0]))]
            + [full(a) for a in (*params, *m, *v)],
            out_specs=tuple(full(s) for s in outs)),
        out_shape=tuple(outs), compiler_params=_params(("arbitrary",)),
    )(chip, small, small, *params, *m, *v)
    return flat[0], flat[1:5], [flat[5 + 4 * n:9 + 4 * n] for n in range(3)]


def kernel(x, norm_g, w_in, sinks, conv_w, w_out, final_g, loss_target, m_norm_g, m_w_in, m_sinks, m_conv_w, m_w_out, m_final_g, v_norm_g, v_w_in, v_sinks, v_conv_w, v_w_out, v_final_g):
    seq = x.shape[1]
    x2 = x.reshape(seq, D_MODEL)
    tgt = loss_target.reshape(seq, D_MODEL)
    ng = norm_g.reshape(1, D_MODEL)
    fg = final_g.reshape(1, D_MODEL)
    chip = 2 * lax.axis_index("x") + lax.axis_index("y")

    conv_w8 = jnp.zeros((8, 128), F32).at[0:3].set(conv_w)
    w_in_full = _gather_w_in(w_in.T).reshape(IN_W, D_MODEL)

    q, kd, vd, rest, rope_c, rope_a, rope_b, wo_all, cw_all = _fwd_proj(x2, ng, w_in_full, w_out, conv_w8)
    w_out_full = wo_all.reshape(D_MODEL, D_MODEL)
    probs, d_attn, prod, dga, db, dgc, dcv, dh2, g_wo, g_wo_b, small_m = _fwd_mix(
        x2, q, kd, vd, rest, sinks, cw_all, w_out_full, fg, tgt)
    out_blocks = lambda t: t.reshape(N_CHIPS, W_OUT_BLK, D_MODEL)
    dq, dk, dv, dkh, dvh, grad_w_out = _bwd_mix(
        q, kd, vd, probs, d_attn, prod, rope_c, rope_a, rope_b, out_blocks(g_wo), out_blocks(g_wo_b))
    grad_x, g_wi, small_p = _bwd_proj(x2, ng, dh2, dq, dk, dv, dkh, dvh, dga, db, dgc, dcv, rest, cw_all,
                                      w_in_full, rope_c, rope_a, rope_b)

    g_in_blocks = g_wi.reshape(N_CHIPS, W_IN_BLK, D_MODEL)
    grad_w_in_t, small = _reduce_grads(g_in_blocks, small_m, small_p)

    (upd_wi, upd_wo) = _adamw_weights([(w_in.T, grad_w_in_t, m_w_in.T, v_w_in.T),
                                       (w_out, grad_w_out, m_w_out, v_w_out)])
    row = lambda t: t.reshape(1, -1)
    loss, grads_s, upd_s = _adamw_small(
        chip.reshape(1), small, (ng, fg, conv_w, row(sinks)),
        (row(m_norm_g), row(m_final_g), m_conv_w, row(m_sinks)),
        (row(v_norm_g), row(v_final_g), v_conv_w, row(v_sinks)))

    def named(ng_, fg_, cw_, sk_, wi_t, wo_):
        return [ng_.reshape(D_MODEL), wi_t.T, sk_.reshape(8), cw_, wo_, fg_.reshape(D_MODEL)]

    g_named = named(*grads_s, grad_w_in_t, grad_w_out)
    out = [loss.reshape(()), grad_x.reshape(1, seq, D_MODEL)] + g_named
    for n in range(3):
        out += named(*upd_s[n], upd_wi[n], upd_wo[n])
    return tuple(out)
```

```python
import jax
import jax.numpy as jnp
from jax import lax
from jax.experimental import pallas as pl
from jax.experimental.pallas import tpu as pltpu

F32 = jnp.float32
BF16 = jnp.bfloat16

D_MODEL = 1024
HEAD_DIM = 64
ATTN_W = 512
KV_W = 128
CONV_W = 512
IN_W = 3328
REST_W = IN_W - ATTN_W - 2 * KV_W
BLOCK = 128
ROT_DIM = 16
ROPE_THETA = 500000.0
EPS = 1e-5
SCALE = 0.125
NEG = -1e30

N_CHIPS = 4
W_IN_BLK = IN_W // N_CHIPS
W_OUT_BLK = D_MODEL // N_CHIPS

ADAM_LR = 0.001
ADAM_B1 = 0.9
ADAM_B2 = 0.999
ADAM_EPS = 1e-08
ADAM_WD = 0.01
ADAM_STEP = 10

VMEM_LIMIT = 60 * 1024 * 1024
T_PROJ = 512
T_FMIX = 512
T_MIX = 512
SMALL_ROWS = 8
MESH = pl.DeviceIdType.MESH

_NT = (((1,), (1,)), ((), ()))
_TN = (((0,), (0,)), ((), ()))


def _params(sem=None):
    kw = dict(vmem_limit_bytes=VMEM_LIMIT)
    if sem is not None:
        kw["dimension_semantics"] = sem
    return pltpu.CompilerParams(**kw)


def _sigmoid(t):
    return 1.0 / (1.0 + jnp.exp(-t))


def _shift_down(t, prev8, k):
    rolled = pltpu.roll(t, k, 0)
    row = lax.broadcasted_iota(jnp.int32, t.shape, 0)
    for j in range(k):
        rolled = jnp.where(row == j, prev8[8 - k + j:8 - k + j + 1, :], rolled)
    return rolled


def _shift_up(t, next8, k):
    n = t.shape[0]
    rolled = pltpu.roll(t, n - k, 0)
    row = lax.broadcasted_iota(jnp.int32, t.shape, 0)
    for j in range(k):
        rolled = jnp.where(row == n - k + j, next8[j:j + 1, :], rolled)
    return rolled


def _rope(t, c, a, b):
    w = t.shape[1]
    reps = w // 128
    if reps > 1:
        c, a, b = (jnp.concatenate([z] * reps, axis=1) for z in (c, a, b))
    return t * c + pltpu.roll(t, w - 8, 1) * a + pltpu.roll(t, 8, 1) * b


def _lane_lo(shape):
    return lax.broadcasted_iota(jnp.int32, shape, 1) < HEAD_DIM


def _stack_heads(t, g):
    lo = _lane_lo((BLOCK, 128))
    parts = []
    for hh in range(4):
        pair = t[:, 256 * g + 128 * (hh // 2):256 * g + 128 * (hh // 2) + 128]
        keep = lo if hh % 2 == 0 else jnp.logical_not(lo)
        parts.append(jnp.where(keep, pair, jnp.zeros_like(pair)))
    return jnp.concatenate(parts, axis=0)


def _unstack_pair(o, pp):
    lo = _lane_lo((BLOCK, 128))
    return jnp.where(lo, o[256 * pp:256 * pp + 128], o[256 * pp + 128:256 * pp + 256])


def _sink_col(sinks_ref, g):
    r = lax.broadcasted_iota(jnp.int32, (4 * BLOCK, 1), 0) // BLOCK
    col = jnp.full((4 * BLOCK, 1), sinks_ref[4 * g + 3], F32)
    for hh in range(3):
        col = jnp.where(r == hh, sinks_ref[4 * g + hh], col)
    return col


def _upper():
    r = lax.broadcasted_iota(jnp.int32, (4 * BLOCK, BLOCK), 0) % BLOCK
    return lax.broadcasted_iota(jnp.int32, (4 * BLOCK, BLOCK), 1) > r


def _fold(t):
    return jnp.where(_upper(), t[:, 0:BLOCK], t[:, BLOCK:])


def _unfold(t):
    zero = jnp.zeros_like(t)
    return jnp.concatenate([jnp.where(_upper(), t, zero), jnp.where(_upper(), zero, t)], axis=1)


def _softmax(s, sink_col, has_prev):
    if has_prev is not True:
        s = jnp.concatenate([jnp.where(has_prev, s[:, 0:BLOCK], NEG), s[:, BLOCK:]], axis=1)
    f = _fold(s)
    m = jnp.maximum(jnp.max(f, axis=-1, keepdims=True), sink_col)
    p = jnp.exp(f - m)
    es = jnp.exp(sink_col - m)
    inv = 1.0 / (jnp.sum(p, axis=-1, keepdims=True) + es)
    return (p * inv).astype(BF16), es * inv


def _key_windows(i, nsub, kd_ref, vd_ref, kdp_ref, vdp_ref):
    out = []
    for sb in range(nsub):
        rows = slice(BLOCK * sb, BLOCK * (sb + 1))
        if sb == 0:
            kk = jnp.concatenate([kdp_ref[...], kd_ref[rows, :]], axis=0)
            vv = jnp.concatenate([vdp_ref[...], vd_ref[rows, :]], axis=0)
            out.append((rows, kk, vv, i > 0))
        else:
            both = slice(BLOCK * (sb - 1), BLOCK * (sb + 1))
            out.append((rows, kd_ref[both, :], vd_ref[both, :], True))
    return out


def _gather_w_in(w_in_t):
    hi = W_IN_BLK // 2
    qr = hi // 2

    def body(wi_hbm, wi_all, f_mine, f_other, b_mine, b_other, send_sems, recv_sems, local_sems):
        x, y, c = lax.axis_index("x"), lax.axis_index("y"), lax.axis_index("c")
        me, sibling = (x, y), (x, y, 1 - c)
        xnb, ynb, diag = (1 - x, y), (x, 1 - y), (1 - x, 1 - y)

        loads = [pltpu.make_async_copy(wi_hbm.at[pl.ds(c * hi, hi)], f_mine, local_sems.at[0]),
                 pltpu.make_async_copy(wi_hbm.at[pl.ds((1 - c) * hi, hi)], f_other, local_sems.at[1])]
        for cp in loads:
            cp.start()

        def copy(k, chip, half, quarter, to, src=None):
            r = wi_all.at[2 * chip[0] + chip[1], pl.ds(half * hi + quarter * qr, qr)]
            return pltpu.make_async_remote_copy(src_ref=r if src is None else src, dst_ref=r, send_sem=send_sems.at[k],
                                                recv_sem=recv_sems.at[k], device_id=to, device_id_type=MESH)

        plan = [(0, xnb, 0), (1, ynb, 1), (2, xnb, 1), (3, ynb, 0)]
        loads[0].wait()
        b_mine[...] = f_mine[...].astype(BF16)
        sent = [copy(k, me, c, quarter, (*nb, c), src=b_mine.at[pl.ds(quarter * qr, qr)]) for k, nb, quarter in plan]
        for cp in sent:
            cp.start()
        loads[1].wait()
        b_other[...] = f_other[...].astype(BF16)
        own_slot = wi_all.at[2 * x + y]
        keeps = [pltpu.make_async_copy(b_mine, own_slot.at[pl.ds(c * hi, hi)], local_sems.at[2]),
                 pltpu.make_async_copy(b_other, own_slot.at[pl.ds((1 - c) * hi, hi)], local_sems.at[3])]
        for cp in keeps:
            cp.start()
        arrivals = [(0, xnb, 0), (1, ynb, 1), (2, xnb, 1), (3, ynb, 0), (4, diag, 0), (5, diag, 1)]
        relay = {0: (4, ynb), 1: (5, xnb)}
        for k, chip, quarter in arrivals:
            copy(k, chip, c, quarter, (x, y, c)).wait_recv()
            if k in relay:
                sent.append(copy(relay[k][0], chip, c, quarter, (*relay[k][1], c)))
                sent[-1].start()
            sent.append(copy(6 + k, chip, c, quarter, sibling))
            sent[-1].start()
        for k, chip, quarter in arrivals:
            copy(6 + k, chip, 1 - c, quarter, (x, y, c)).wait_recv()
        for cp in sent:
            cp.wait_send()
        for cp in keeps:
            cp.wait()

    hbm = pl.BlockSpec(memory_space=pl.ANY)
    return pl.pallas_call(
        body, name="gather_w_in",
        out_shape=jax.ShapeDtypeStruct((N_CHIPS, W_IN_BLK, D_MODEL), BF16),
        in_specs=[hbm], out_specs=hbm,
        scratch_shapes=[pltpu.VMEM((hi, D_MODEL), F32), pltpu.VMEM((hi, D_MODEL), F32),
                        pltpu.VMEM((hi, D_MODEL), BF16), pltpu.VMEM((hi, D_MODEL), BF16),
                        pltpu.SemaphoreType.DMA((12,)), pltpu.SemaphoreType.DMA((12,)), pltpu.SemaphoreType.DMA((4,))],
        compiler_params=_params(),
    )(w_in_t)


def _reduce_grads(g_in, *smalls):
    hi = W_IN_BLK // 2
    qr = hi // 2
    q0, q1 = slice(0, qr), slice(qr, hi)

    def body(gi_hbm, s0_ref, s1_ref, gi_out, small_out,
             mine_i, sib_i, out_i, ici_i, small_in, small_ref, send_sems, recv_sems, local_sems):
        x, y, c = lax.axis_index("x"), lax.axis_index("y"), lax.axis_index("c")
        my_dev = 4 * x + 2 * y + c
        sibling = (x, y, 1 - c)
        xnb, ynb = (1 - x, y, c), (x, 1 - y, c)
        order = [(1 - x, 1 - y), (1 - x, y), (x, 1 - y), (x, y)]

        def remote(k, src, dst, to):
            return pltpu.make_async_remote_copy(src_ref=src, dst_ref=dst, send_sem=send_sems.at[k],
                                                recv_sem=recv_sems.at[k], device_id=to, device_id_type=MESH)

        small_ref[...] = s0_ref[...] + s1_ref[...]
        small_cps = []
        for f in range(1, 8):
            fx, fy, fc = f >> 2, (f >> 1) & 1, f & 1
            small_cps.append(remote(11 + f - 1, small_ref, small_in.at[f - 1], (x ^ fx, y ^ fy, c ^ fc)))
        for cp in small_cps:
            cp.start()

        own, to_sib = [], []
        for n, chip in enumerate(order):
            j = 2 * chip[0] + chip[1]
            own.append(pltpu.make_async_copy(gi_hbm.at[j, pl.ds(c * hi, hi)], mine_i.at[n], local_sems.at[n]))
            to_sib.append(remote(6 + n, gi_hbm.at[j, pl.ds((1 - c) * hi, hi)], sib_i.at[n], sibling))
            own[-1].start()
            to_sib[-1].start()

        def pair_sum(n):
            own[n].wait()
            to_sib[n].wait_recv()
            return mine_i[n] + sib_i[n]

        ici = [remote(k, out_i.at[k], ici_i.at[k], xnb if k % 2 == 0 else ynb) for k in range(6)]

        def send(k, rows_f32):
            out_i[k] = rows_f32.astype(BF16)
            ici[k].start()

        p_diag = pair_sum(0)
        send(0, p_diag[q0])
        send(1, p_diag[q1])
        p_x = pair_sum(1)
        send(2, p_x[q0])
        p_y = pair_sum(2)
        send(3, p_y[q1])
        ici[0].wait_recv()
        send(5, p_y[q0] + ici_i[0].astype(F32))
        ici[1].wait_recv()
        send(4, p_x[q1] + ici_i[1].astype(F32))
        p_mine = pair_sum(3)
        for k in range(2, 6):
            ici[k].wait_recv()
        gi_out[pl.ds(c * hi, qr), :] = p_mine[q0] + ici_i[2].astype(F32) + ici_i[5].astype(F32)
        gi_out[pl.ds(c * hi + qr, qr), :] = p_mine[q1] + ici_i[4].astype(F32) + ici_i[3].astype(F32)

        swap = [remote(10, gi_out.at[pl.ds(c * hi, hi)], gi_out.at[pl.ds(c * hi, hi)], sibling)]
        for cp in swap:
            cp.start()

        for cp in small_cps:
            cp.wait_recv()
        total = jnp.zeros((SMALL_ROWS, D_MODEL), F32)
        for d in range(8):
            slot = jnp.maximum((d ^ my_dev) - 1, 0)
            total = total + jnp.where(d == my_dev, small_ref[...], small_in[slot])
        small_out[...] = total

        remote(10, gi_out.at[pl.ds((1 - c) * hi, hi)], gi_out.at[pl.ds((1 - c) * hi, hi)], sibling).wait_recv()
        for cp in to_sib + ici + swap + small_cps:
            cp.wait_send()

    vmem = pl.BlockSpec(memory_space=pltpu.VMEM)
    anyspace = pl.BlockSpec(memory_space=pl.ANY)
    return pl.pallas_call(
        body, name="reduce_grads",
        out_shape=(jax.ShapeDtypeStruct((W_IN_BLK, D_MODEL), F32), jax.ShapeDtypeStruct((SMALL_ROWS, D_MODEL), F32)),
        in_specs=[anyspace, vmem, vmem], out_specs=(vmem, vmem),
        scratch_shapes=[pltpu.VMEM((N_CHIPS, hi, D_MODEL), F32), pltpu.VMEM((N_CHIPS, hi, D_MODEL), F32),
                        pltpu.VMEM((6, qr, D_MODEL), BF16), pltpu.VMEM((6, qr, D_MODEL), BF16),
                        pltpu.VMEM((7, SMALL_ROWS, D_MODEL), F32), pltpu.VMEM((SMALL_ROWS, D_MODEL), F32),
                        pltpu.SemaphoreType.DMA((18,)), pltpu.SemaphoreType.DMA((18,)),
                        pltpu.SemaphoreType.DMA((4,))],
        compiler_params=_params(),
    )(g_in, *smalls)


def _fwd_proj(x, norm_g, w_in_t, w_out, conv_w8):
    seq = x.shape[0]
    nt = seq // T_PROJ
    lane = jnp.arange(128, dtype=jnp.int32) % HEAD_DIM
    inv_freq = ROPE_THETA ** (-(2 * (lane % 8)).astype(F32) / ROT_DIM)
    inv_freq = jnp.where(lane < ROT_DIM, inv_freq, 0.0).reshape(1, 128)
    in_tile = jnp.arange(T_PROJ, dtype=jnp.int32).astype(F32)[:, None] * inv_freq
    cos_in, sin_in = jnp.cos(in_tile), jnp.sin(in_tile)
    start = jnp.repeat((jnp.arange(nt, dtype=jnp.int32) * T_PROJ).astype(F32), 8)[:, None] * inv_freq
    cos_st, sin_st = jnp.cos(start), jnp.sin(start)

    def body(x_ref, g_ref, w_ref, cs_ref, ss_ref, ci_ref, si_ref, wo_ref, cw_ref,
             q_ref, kd_ref, vd_ref, rest_ref, c_ref, a_ref, b_ref, wo_all, cw_all,
             wo_stage, send_sems, recv_sems, local_sems):
        i = pl.program_id(0)
        mx, my, mc = lax.axis_index("x"), lax.axis_index("y"), lax.axis_index("c")
        chips = [(1 - mx, my), (mx, 1 - my), (1 - mx, 1 - my)]

        def gather(blocks):
            cps = []
            for k, chip in enumerate(chips):
                for n, (src, dst) in enumerate(((wo_stage, wo_all), (cw_ref, cw_all))):
                    cps.append(pltpu.make_async_remote_copy(
                        src_ref=src, dst_ref=dst.at[blocks[k]], send_sem=send_sems.at[2 * k + n],
                        recv_sem=recv_sems.at[2 * k + n], device_id=(*chip, mc), device_id_type=MESH))
            return cps

        me = 2 * mx + my
        own = [pltpu.make_async_copy(wo_stage, wo_all.at[me], local_sems.at[0]),
               pltpu.make_async_copy(cw_ref, cw_all.at[me], local_sems.at[1])]

        @pl.when(i == 0)
        def _():
            wo_stage[...] = wo_ref[...].astype(BF16)
            for cp in own + gather([me] * 3):
                cp.start()

        xf = x_ref[...]
        r1 = lax.rsqrt(jnp.mean(xf * xf, axis=-1, keepdims=True) + EPS)
        xn = (xf * r1 * g_ref[...]).astype(BF16)
        cs, ss = cs_ref[0:1, :], ss_ref[0:1, :]
        c = cs * ci_ref[...] - ss * si_ref[...]
        sin = ss * ci_ref[...] + cs * si_ref[...]
        j = lax.broadcasted_iota(jnp.int32, (T_PROJ, 128), 1) % HEAD_DIM
        a = jnp.where(j < 8, -sin, 0.0)
        b = jnp.where(j >= 8, sin, 0.0)
        c_ref[...], a_ref[...], b_ref[...] = c, a, b
        proj = lambda lo_c, w: lax.dot_general(xn, w_ref[lo_c:lo_c + w, :], _NT, preferred_element_type=F32)
        q_ref[...] = (_rope(proj(0, ATTN_W), c, a, b) * SCALE).astype(BF16)
        kv = proj(ATTN_W, 2 * KV_W)
        k = _rope(kv[:, 0:KV_W], c, a, b)
        v = kv[:, KV_W:2 * KV_W]
        lo = _lane_lo(k.shape)
        for t, ref in ((k, kd_ref), (v, vd_ref)):
            sw = pltpu.roll(t, HEAD_DIM, 1)
            ref[:, 0:128] = jnp.where(lo, t, sw).astype(BF16)
            ref[:, 128:256] = jnp.where(lo, sw, t).astype(BF16)
        for n in range(REST_W // 512):
            rest_ref[:, 512 * n:512 * (n + 1)] = proj(ATTN_W + 2 * KV_W + 512 * n, 512)

        @pl.when(i == nt - 1)
        def _():
            sent = gather([me] * 3)
            for cp in gather([2 * chip[0] + chip[1] for chip in chips]):
                cp.wait_recv()
            for cp in sent:
                cp.wait_send()
            for cp in own:
                cp.wait()

    tile = lambda w: pl.BlockSpec((T_PROJ, w), lambda i: (i, 0))
    whole = lambda r, w: pl.BlockSpec((r, w), lambda i: (0, 0))
    vmem = pl.BlockSpec(memory_space=pltpu.VMEM)
    hbm = pl.BlockSpec(memory_space=pl.ANY)
    return pl.pallas_call(
        body, name="fwd_proj", grid=(nt,),
        out_shape=(jax.ShapeDtypeStruct((seq, ATTN_W), BF16), jax.ShapeDtypeStruct((seq, 2 * KV_W), BF16),
                   jax.ShapeDtypeStruct((seq, 2 * KV_W), BF16), jax.ShapeDtypeStruct((seq, REST_W), F32))
        + (jax.ShapeDtypeStruct((seq, 128), F32),) * 3
        + (jax.ShapeDtypeStruct((N_CHIPS, W_OUT_BLK, D_MODEL), BF16), jax.ShapeDtypeStruct((N_CHIPS, 8, 128), F32)),
        in_specs=[tile(D_MODEL), whole(1, D_MODEL), whole(IN_W, D_MODEL), pl.BlockSpec((8, 128), lambda i: (i, 0)),
                  pl.BlockSpec((8, 128), lambda i: (i, 0)), whole(T_PROJ, 128), whole(T_PROJ, 128), vmem, vmem],
        out_specs=(tile(ATTN_W), tile(2 * KV_W), tile(2 * KV_W), tile(REST_W), tile(128), tile(128), tile(128),
                   hbm, hbm),
        scratch_shapes=[pltpu.VMEM((W_OUT_BLK, D_MODEL), BF16), pltpu.SemaphoreType.DMA((6,)),
                        pltpu.SemaphoreType.DMA((6,)), pltpu.SemaphoreType.DMA((2,))],
        compiler_params=_params(("arbitrary",)),
    )(x, norm_g, w_in_t, cos_st, sin_st, cos_in, sin_in, w_out, conv_w8)


CONV_SPEC = pl.BlockSpec((N_CHIPS, 8, 128), lambda i: (0, 0, 0))


def _conv_rows(cw_ref):
    return jnp.concatenate([cw_ref[j] for j in range(N_CHIPS)], axis=1)


def _conv_parts(rest_ref, prev_ref, cw_ref, first):
    u = rest_ref[:, 1024:1536] * rest_ref[:, 1536:2048]
    up = prev_ref[:, 1024:1536] * prev_ref[:, 1536:2048]
    up = jnp.where(first, jnp.zeros_like(up), up)
    um1 = _shift_down(u, up, 1)
    um2 = _shift_down(u, up, 2)
    cw = _conv_rows(cw_ref)
    cv = cw[0:1, :] * um2 + cw[1:2, :] * um1 + cw[2:3, :] * u
    return u, um1, um2, cv


def _fwd_mix(x, q, kd, vd, rest, sinks, conv_w, w_out, final_g, target):
    seq = x.shape[0]
    nt = seq // T_FMIX
    nsub = T_FMIX // BLOCK

    def body(sinks_ref, x_ref, q_ref, kd_ref, vd_ref, kdp_ref, vdp_ref, rest_ref, restp_ref, cw_ref, wo_ref,
             fg_ref, tgt_ref, prob_ref, do_ref, prod_ref, dga_ref, db_ref, dgc_ref, dcv_ref, dh2_ref,
             gwo_hbm, gwob_hbm, small_ref,
             pmix_ref, pdh2_ref, gwo_ref, gate_ref, dsink_ref):
        i = pl.program_id(0)
        attn_ref, sinkw_ref = prod_ref, dcv_ref

        @pl.when(i == 0)
        def _():
            small_ref[...] = jnp.zeros_like(small_ref)
            dsink_ref[...] = jnp.zeros_like(dsink_ref)
            gwo_ref[...] = jnp.zeros_like(gwo_ref)
            pmix_ref[...] = jnp.zeros_like(pmix_ref)
            pdh2_ref[...] = jnp.zeros_like(pdh2_ref)

        chains = []
        for rows, kk, vv, has_prev in _key_windows(i, nsub, kd_ref, vd_ref, kdp_ref, vdp_ref):
            qt = q_ref[rows, :]
            for g in range(2):
                kg = kk[:, 128 * g:128 * (g + 1)]
                chains.append(dict(g=g, rows=rows, has_prev=has_prev, vg=vv[:, 128 * g:128 * (g + 1)],
                                   s=lax.dot_general(_stack_heads(qt, g), kg, _NT, preferred_element_type=F32)))
        gwo_ref[...] += lax.dot_general(pmix_ref[...], pdh2_ref[...], _TN, preferred_element_type=F32)
        for ch in chains:
            ch["prob"], ch["psink"] = _softmax(ch.pop("s"), _sink_col(sinks_ref, ch["g"]), ch["has_prev"])
        for k, ch in enumerate(chains):
            prob_ref[k] = ch["prob"]
            o = jnp.dot(_unfold(ch["prob"]), ch["vg"], preferred_element_type=F32)
            ow = o * ch["psink"]
            for pp in range(2):
                lanes = slice(256 * ch["g"] + 128 * pp, 256 * ch["g"] + 128 * (pp + 1))
                attn_ref[ch["rows"], lanes] = _unstack_pair(o, pp)
                sinkw_ref[ch["rows"], lanes] = _unstack_pair(ow, pp)

        def silu_parts(t, lo_c):
            sg = _sigmoid(t)
            silu = t * sg
            gate_ref[:, lo_c:lo_c + 512] = silu
            gate_ref[:, lo_c + 512:lo_c + 1024] = sg * (1.0 + t * (1.0 - sg))
            return silu

        pmix_ref[:, 0:ATTN_W] = (attn_ref[...] * silu_parts(rest_ref[:, 0:512], 0)).astype(BF16)
        u, um1, um2, cv = _conv_parts(rest_ref, restp_ref, cw_ref, i == 0)
        pmix_ref[:, ATTN_W:] = (rest_ref[:, 512:1024] * cv * silu_parts(rest_ref[:, 2048:2560], 1024)).astype(BF16)

        h2 = x_ref[...] + jnp.dot(pmix_ref[...], wo_ref[...], preferred_element_type=F32)
        r2 = lax.rsqrt(jnp.mean(h2 * h2, axis=-1, keepdims=True) + EPS)
        n2 = h2 * r2
        err = n2 * fg_ref[...] - tgt_ref[...]
        dy = err * (1.0 / D_MODEL)
        small_ref[6:7, :] += jnp.sum(err * err, axis=0, keepdims=True) * (0.5 / D_MODEL)
        small_ref[1:2, :] += jnp.sum(dy * n2, axis=0, keepdims=True)
        dn = dy * fg_ref[...]
        dh2 = r2 * (dn - n2 * jnp.mean(dn * n2, axis=-1, keepdims=True))
        dh2_ref[...] = dh2
        pdh2_ref[...] = dh2.astype(BF16)

        d_mix = lambda lo_r: lax.dot_general(pdh2_ref[...], wo_ref[lo_r:lo_r + 512, :], _NT, preferred_element_type=F32)
        dma = d_mix(0)
        dga_ref[...] = (dma * attn_ref[...] * gate_ref[:, 512:1024]).astype(BF16)
        d_attn = dma * gate_ref[:, 0:512]
        do_ref[...] = d_attn.astype(BF16)
        prod_ref[...] = d_attn * attn_ref[...]
        dsink_ref[0:1, :] += jnp.sum(d_attn * sinkw_ref[...], axis=0, keepdims=True)
        bg = rest_ref[:, 512:1024]
        dmc = d_mix(ATTN_W)
        t1 = dmc * gate_ref[:, 1024:1536]
        db_ref[...] = (t1 * cv).astype(BF16)
        dcv = t1 * bg
        dcv_ref[...] = dcv
        dgc_ref[...] = (dmc * (bg * cv) * gate_ref[:, 1536:2048]).astype(BF16)
        small_ref[2:3, 0:CONV_W] += jnp.sum(dcv * um2, axis=0, keepdims=True)
        small_ref[3:4, 0:CONV_W] += jnp.sum(dcv * um1, axis=0, keepdims=True)
        small_ref[4:5, 0:CONV_W] += jnp.sum(dcv * u, axis=0, keepdims=True)

        @pl.when(i == nt - 1)
        def _():
            head = lax.broadcasted_iota(jnp.int32, (1, ATTN_W), 1) // HEAD_DIM
            for h in range(8):
                tot = jnp.sum(jnp.where(head == h, dsink_ref[0:1, :], 0.0), axis=-1, keepdims=True)
                small_ref[5:6, h:h + 1] = -tot
            gwo_ref[...] += lax.dot_general(pmix_ref[...], pdh2_ref[...], _TN, preferred_element_type=F32)
            pltpu.sync_copy(gwo_ref, gwo_hbm)
            for n in range(D_MODEL // T_FMIX):
                slab = slice(T_FMIX * n, T_FMIX * (n + 1))
                pmix_ref[...] = gwo_ref[slab, :].astype(BF16)
                pltpu.sync_copy(pmix_ref, gwob_hbm.at[slab])

    tile = lambda w: pl.BlockSpec((T_FMIX, w), lambda i: (i, 0))
    whole = lambda r, w: pl.BlockSpec((r, w), lambda i: (0, 0))
    prev_blk = pl.BlockSpec((BLOCK, 2 * KV_W), lambda i: (jnp.maximum(i * nsub - 1, 0), 0))
    prev8 = pl.BlockSpec((8, REST_W), lambda i: (jnp.maximum(i * (T_FMIX // 8) - 1, 0), 0))
    bf = lambda w: jax.ShapeDtypeStruct((seq, w), BF16)
    f32 = lambda w: jax.ShapeDtypeStruct((seq, w), F32)
    return pl.pallas_call(
        body, name="fwd_mix", grid=(nt,),
        out_shape=(jax.ShapeDtypeStruct((2 * seq // BLOCK, 4 * BLOCK, BLOCK), BF16),
                   bf(ATTN_W), f32(ATTN_W), bf(ATTN_W), bf(CONV_W), bf(CONV_W), f32(CONV_W), f32(D_MODEL),
                   jax.ShapeDtypeStruct((D_MODEL, D_MODEL), F32), jax.ShapeDtypeStruct((D_MODEL, D_MODEL), BF16),
                   jax.ShapeDtypeStruct((SMALL_ROWS, D_MODEL), F32)),
        in_specs=[pl.BlockSpec(memory_space=pltpu.SMEM), tile(D_MODEL), tile(ATTN_W), tile(2 * KV_W), tile(2 * KV_W),
                  prev_blk, prev_blk, tile(REST_W), prev8, CONV_SPEC,
                  pl.BlockSpec((D_MODEL, D_MODEL), lambda i: (0, 0), pipeline_mode=pl.Buffered(1)),
                  whole(1, D_MODEL), tile(D_MODEL)],
        out_specs=(pl.BlockSpec((2 * nsub, 4 * BLOCK, BLOCK), lambda i: (i, 0, 0)),
                   tile(ATTN_W), tile(ATTN_W), tile(ATTN_W), tile(CONV_W), tile(CONV_W), tile(CONV_W), tile(D_MODEL),
                   pl.BlockSpec(memory_space=pl.ANY), pl.BlockSpec(memory_space=pl.ANY), whole(SMALL_ROWS, D_MODEL)),
        scratch_shapes=[pltpu.VMEM((T_FMIX, D_MODEL), BF16)] * 2 + [
            pltpu.VMEM((D_MODEL, D_MODEL), F32), pltpu.VMEM((T_FMIX, 4 * 512), F32), pltpu.VMEM((8, ATTN_W), F32)],
        compiler_params=_params(("arbitrary",)),
    )(sinks, x, q, kd, vd, kd, vd, rest, rest, conv_w, w_out, final_g, target)


def _scatter_copies(g_hbm, gb_hbm, mine, land, send_sems, recv_sems, local_sem, half):
    x, y, c = lax.axis_index("x"), lax.axis_index("y"), lax.axis_index("c")
    cps = []
    for f in range(1, 8):
        to = (x ^ (f >> 2), y ^ ((f >> 1) & 1), c ^ (f & 1))
        src = gb_hbm.at[2 * to[0] + to[1], pl.ds(to[2] * half, half)]
        cps.append(pltpu.make_async_remote_copy(src_ref=src, dst_ref=land.at[f - 1], send_sem=send_sems.at[f - 1],
                                                recv_sem=recv_sems.at[f - 1], device_id=to, device_id_type=MESH))
    own = pltpu.make_async_copy(g_hbm.at[2 * x + y, pl.ds(c * half, half)], mine, local_sem)
    return cps, own


def _scatter_finish(cps, own, mine, land, out_hbm, send_sems, recv_sems, local_sem, half):
    x, y, c = lax.axis_index("x"), lax.axis_index("y"), lax.axis_index("c")
    own.wait()
    tot = mine[...]
    for f in range(1, 8):
        cps[f - 1].wait_recv()
        tot = tot + land[f - 1].astype(F32)
    mine[...] = tot

    def swap(rows_of):
        return pltpu.make_async_remote_copy(src_ref=mine, dst_ref=out_hbm.at[pl.ds(rows_of * half, half)],
                                            send_sem=send_sems.at[7], recv_sem=recv_sems.at[7],
                                            device_id=(x, y, 1 - c), device_id_type=MESH)

    keep = pltpu.make_async_copy(mine, out_hbm.at[pl.ds(c * half, half)], local_sem)
    keep.start()
    swap(c).start()
    swap(1 - c).wait_recv()
    keep.wait()
    for cp in cps:
        cp.wait_send()
    swap(c).wait_send()


def _bwd_mix(q, kd, vd, probs, d_attn, prod, rope_c, rope_a, rope_b, g_out, g_out_b):
    seq = q.shape[0]
    nt = seq // T_MIX
    nsub = T_MIX // BLOCK
    ho = W_OUT_BLK // 2

    def body(q_ref, kd_ref, vd_ref, kdp_ref, vdp_ref, prob_ref, do_ref, prod_ref, c_ref, a_ref, b_ref, go_hbm, gob_hbm,
             dq_ref, dk_ref, dv_ref, dkh_ref, dvh_ref, go_out,
             mine_o, land_o, send_sems, recv_sems, local_sems):
        i = pl.program_id(0)
        scatter = (mine_o, land_o, send_sems, recv_sems, local_sems.at[0], ho)

        @pl.when(i == 0)
        def _():
            cps, own = _scatter_copies(go_hbm, gob_hbm, *scatter)
            for cp in cps + [own]:
                cp.start()

        lo = _lane_lo((2 * BLOCK, 128))
        dk_blocks = [None] * (nsub + 1)
        dv_blocks = [None] * (nsub + 1)

        def add(lst, n, val):
            lst[n] = val if lst[n] is None else lst[n] + val

        chains = []
        for rows, kk, vv, _ in _key_windows(i, nsub, kd_ref, vd_ref, kdp_ref, vdp_ref):
            qt = q_ref[rows, :]
            dot = do_ref[rows, :]
            for g in range(2):
                rs = jnp.sum(_stack_heads(prod_ref[rows, :], g), axis=-1, keepdims=True)
                chains.append(dict(g=g, rows=rows, rs=rs, qs=_stack_heads(qt, g), dos=_stack_heads(dot, g),
                                   kg=kk[:, 128 * g:128 * (g + 1)], vg=vv[:, 128 * g:128 * (g + 1)]))
        for ch in chains:
            ch["dp"] = lax.dot_general(ch["dos"], ch["vg"], _NT, preferred_element_type=F32)
        for k, ch in enumerate(chains):
            ch["ds"] = _unfold((prob_ref[k].astype(F32) * (_fold(ch["dp"]) - ch["rs"])).astype(BF16))
        for k, ch in enumerate(chains):
            dqs = jnp.dot(ch["ds"], ch["kg"], preferred_element_type=F32) * SCALE
            c, a, b = c_ref[ch["rows"], :], a_ref[ch["rows"], :], b_ref[ch["rows"], :]
            for pp in range(2):
                lanes = slice(256 * ch["g"] + 128 * pp, 256 * ch["g"] + 128 * (pp + 1))
                dq_ref[ch["rows"], lanes] = _rope(_unstack_pair(dqs, pp), c, -a, -b).astype(BF16)
            dkd = lax.dot_general(ch["ds"], ch["qs"], _TN, preferred_element_type=F32)
            dvd = lax.dot_general(_unfold(prob_ref[k]), ch["dos"], _TN, preferred_element_type=F32)
            ch["dk"] = dkd + pltpu.roll(dkd, HEAD_DIM, 1)
            ch["dv"] = dvd + pltpu.roll(dvd, HEAD_DIM, 1)
        for sb in range(nsub):
            dk2 = jnp.where(lo, chains[2 * sb]["dk"], chains[2 * sb + 1]["dk"])
            dv2 = jnp.where(lo, chains[2 * sb]["dv"], chains[2 * sb + 1]["dv"])
            add(dk_blocks, sb, dk2[0:BLOCK])
            add(dk_blocks, sb + 1, dk2[BLOCK:])
            add(dv_blocks, sb, dv2[0:BLOCK])
            add(dv_blocks, sb + 1, dv2[BLOCK:])
        dkh_ref[0] = dk_blocks[0]
        dvh_ref[0] = dv_blocks[0]
        for sb in range(nsub):
            dk_ref[BLOCK * sb:BLOCK * (sb + 1), :] = dk_blocks[sb + 1]
            dv_ref[BLOCK * sb:BLOCK * (sb + 1), :] = dv_blocks[sb + 1]

        @pl.when(i == nt - 1)
        def _():
            cps, own = _scatter_copies(go_hbm, gob_hbm, *scatter)
            _scatter_finish(cps, own, mine_o, land_o, go_out, send_sems, recv_sems, local_sems.at[1], ho)

    tile = lambda w: pl.BlockSpec((T_MIX, w), lambda i: (i, 0))
    prev_blk = pl.BlockSpec((BLOCK, 2 * KV_W), lambda i: (jnp.maximum(i * nsub - 1, 0), 0))
    halo = pl.BlockSpec((1, BLOCK, KV_W), lambda i: (i, 0, 0))
    hbm = pl.BlockSpec(memory_space=pl.ANY)
    f32 = lambda w: jax.ShapeDtypeStruct((seq, w), F32)
    return pl.pallas_call(
        body, name="bwd_mix", grid=(nt,),
        out_shape=(jax.ShapeDtypeStruct((seq, ATTN_W), BF16), f32(KV_W), f32(KV_W),
                   jax.ShapeDtypeStruct((nt, BLOCK, KV_W), F32), jax.ShapeDtypeStruct((nt, BLOCK, KV_W), F32),
                   jax.ShapeDtypeStruct((W_OUT_BLK, D_MODEL), F32)),
        in_specs=[tile(ATTN_W), tile(2 * KV_W), tile(2 * KV_W), prev_blk, prev_blk,
                  pl.BlockSpec((2 * nsub, 4 * BLOCK, BLOCK), lambda i: (i, 0, 0)), tile(ATTN_W), tile(ATTN_W),
                  tile(128), tile(128), tile(128), hbm, hbm],
        out_specs=(tile(ATTN_W), tile(KV_W), tile(KV_W), halo, halo, hbm),
        scratch_shapes=[pltpu.VMEM((ho, D_MODEL), F32), pltpu.VMEM((7, ho, D_MODEL), BF16),
                        pltpu.SemaphoreType.DMA((8,)), pltpu.SemaphoreType.DMA((8,)), pltpu.SemaphoreType.DMA((2,))],
        compiler_params=_params(("arbitrary",)),
    )(q, kd, vd, kd, vd, probs, d_attn, prod, rope_c, rope_a, rope_b, g_out, g_out_b)


def _bwd_proj(x, norm_g, dh2, dq, dk, dv, dkh, dvh, dga, db, dgc, dcv, rest, conv_w, w_in_t, rope_c, rope_a, rope_b):
    seq = x.shape[0]
    tb = T_PROJ
    per = tb // T_MIX
    nt = seq // tb

    def body(x_ref, g_ref, dh2_ref, dq_ref, dk_ref, dv_ref, dkh_ref, dvh_ref, dkn_ref, dvn_ref, dga_ref, db_ref,
             dgc_ref, dcv_ref, dcvn_ref, ch_ref, cw_ref, w_ref, c_ref, a_ref, b_ref, gx_ref, gw_hbm, small_ref,
             dp_ref, acc_ref):
        i = pl.program_id(0)

        @pl.when(i == 0)
        def _():
            small_ref[...] = jnp.zeros_like(small_ref)
            acc_ref[...] = jnp.zeros_like(acc_ref)

        last = i == nt - 1
        keep = jnp.where(last, 0.0, 1.0)
        pad = jnp.zeros((T_MIX - BLOCK, KV_W), F32)

        def with_halos(main_ref, halo_ref, next_ref):
            parts = []
            for m in range(1, per + 1):
                parts += [pad, halo_ref[m] if m < per else next_ref[0] * keep]
            return main_ref[...] + jnp.concatenate(parts, axis=0)

        dk = with_halos(dk_ref, dkh_ref, dkn_ref)
        dv = with_halos(dv_ref, dvh_ref, dvn_ref)
        dp_ref[:, 0:ATTN_W] = dq_ref[...]
        dp_ref[:, ATTN_W:ATTN_W + KV_W] = _rope(dk, c_ref[...], -a_ref[...], -b_ref[...]).astype(BF16)
        dp_ref[:, ATTN_W + KV_W:ATTN_W + 2 * KV_W] = dv.astype(BF16)
        base = ATTN_W + 2 * KV_W
        dp_ref[:, base:base + 512] = dga_ref[...]
        dp_ref[:, base + 512:base + 1024] = db_ref[...]
        dcv = dcv_ref[...]
        nxt = dcvn_ref[...] * keep
        cw = _conv_rows(cw_ref)
        du = cw[2:3, :] * dcv + cw[1:2, :] * _shift_up(dcv, nxt, 1) + cw[0:1, :] * _shift_up(dcv, nxt, 2)
        dp_ref[:, base + 1024:base + 1536] = (du * ch_ref[:, 512:1024]).astype(BF16)
        dp_ref[:, base + 1536:base + 2048] = (du * ch_ref[:, 0:512]).astype(BF16)
        dp_ref[:, base + 2048:base + 2560] = dgc_ref[...]

        xf = x_ref[...]
        r1 = lax.rsqrt(jnp.mean(xf * xf, axis=-1, keepdims=True) + EPS)
        n1 = xf * r1
        xn = (n1 * g_ref[...]).astype(BF16)
        for n in range(IN_W // 256):
            cols = slice(256 * n, 256 * (n + 1))
            acc_ref[cols, :] += lax.dot_general(dp_ref[:, cols], xn, _TN, preferred_element_type=F32)
        dxn = jnp.dot(dp_ref[...], w_ref[...], preferred_element_type=F32)
        small_ref[0:1, :] += jnp.sum(dxn * n1, axis=0, keepdims=True)
        dxg = dxn * g_ref[...]
        gx_ref[...] = r1 * (dxg - n1 * jnp.mean(dxg * n1, axis=-1, keepdims=True)) + dh2_ref[...]

        @pl.when(last)
        def _():
            pltpu.sync_copy(acc_ref, gw_hbm)

    tile = lambda w: pl.BlockSpec((tb, w), lambda i: (i, 0))
    whole = lambda r, w: pl.BlockSpec((r, w), lambda i: (0, 0))
    halo = pl.BlockSpec((per, BLOCK, KV_W), lambda i: (i, 0, 0))
    halo_next = pl.BlockSpec((1, BLOCK, KV_W), lambda i: (jnp.minimum((i + 1) * per, seq // T_MIX - 1), 0, 0))
    next8 = pl.BlockSpec((8, CONV_W), lambda i: (jnp.minimum((i + 1) * (tb // 8), seq // 8 - 1), 0))
    ch = pl.BlockSpec((tb, 1024), lambda i: (i, 1))
    return pl.pallas_call(
        body, name="bwd_proj", grid=(nt,),
        out_shape=(jax.ShapeDtypeStruct((seq, D_MODEL), F32), jax.ShapeDtypeStruct((IN_W, D_MODEL), F32),
                   jax.ShapeDtypeStruct((SMALL_ROWS, D_MODEL), F32)),
        in_specs=[tile(D_MODEL), whole(1, D_MODEL), tile(D_MODEL), tile(ATTN_W), tile(KV_W), tile(KV_W), halo, halo,
                  halo_next, halo_next,
                  tile(ATTN_W), tile(CONV_W), tile(CONV_W), tile(CONV_W), next8, ch, CONV_SPEC,
                  pl.BlockSpec((IN_W, D_MODEL), lambda i: (0, 0), pipeline_mode=pl.Buffered(1)),
                  tile(128), tile(128), tile(128)],
        out_specs=(tile(D_MODEL), pl.BlockSpec(memory_space=pl.ANY), whole(SMALL_ROWS, D_MODEL)),
        scratch_shapes=[pltpu.VMEM((tb, IN_W), BF16), pltpu.VMEM((IN_W, D_MODEL), F32)],
        compiler_params=_params(("arbitrary",)),
    )(x, norm_g, dh2, dq, dk, dv, dkh, dvh, dkh, dvh, dga, db, dgc, dcv, dcv, rest, conv_w, w_in_t,
      rope_c, rope_a, rope_b)


def _adamw_step(w, g, m, v):
    m2 = ADAM_B1 * m + (1.0 - ADAM_B1) * g
    v2 = ADAM_B2 * v + (1.0 - ADAM_B2) * jnp.square(g)
    m_hat = m2 / (1.0 - ADAM_B1 ** ADAM_STEP)
    v_hat = v2 / (1.0 - ADAM_B2 ** ADAM_STEP)
    return -ADAM_LR * (m_hat / (jnp.sqrt(v_hat) + ADAM_EPS) + ADAM_WD * w), m2, v2


def _adamw_weights(groups):
    steps = 4

    def body(*refs):
        ins, outs = refs[:4 * len(groups)], refs[4 * len(groups):]
        for k in range(len(groups)):
            res = _adamw_step(*(r[...] for r in ins[4 * k:4 * k + 4]))
            for o_ref, val in zip(outs[3 * k:3 * k + 3], res):
                o_ref[...] = val

    in_specs, out_specs, out_shape = [], [], []
    for w, _, _, _ in groups:
        rows, cols = w.shape
        spec = pl.BlockSpec((rows // steps, cols), lambda i: (i, 0))
        in_specs += [spec] * 4
        out_specs += [spec] * 3
        out_shape += [jax.ShapeDtypeStruct((rows, cols), F32)] * 3
    flat = pl.pallas_call(
        body, name="adamw_weights", grid=(steps,), out_shape=tuple(out_shape), in_specs=in_specs,
        out_specs=tuple(out_specs), compiler_params=_params(("arbitrary",)),
    )(*[a for grp in groups for a in grp])
    return [flat[3 * k:3 * k + 3] for k in range(len(groups))]


def _adamw_small(chip, small, params, m, v):
    def body(chip_ref, small_ref, conv_ref, *refs):
        ins, outs = refs[:12], refs[12:]
        outs[0][...] = jnp.sum(small_ref[6:7, :], axis=-1, keepdims=True)
        grads = (small_ref[0:1, :], small_ref[1:2, :], conv_ref[2:5, :], small_ref[5:6, 0:8])
        for k, g in enumerate(grads):
            outs[1 + k][...] = g
            res = _adamw_step(ins[k][...], g, ins[4 + k][...], ins[8 + k][...])
            for n, val in enumerate(res):
                outs[5 + 4 * n + k][...] = val

    full = lambda a: pl.BlockSpec(a.shape, lambda i, c: (0,) * len(a.shape))
    shapes = [jax.ShapeDtypeStruct(p.shape, F32) for p in params]
    outs = [jax.ShapeDtypeStruct((1, 1), F32)] + shapes * 4
    flat = pl.pallas_call(
        body, name="adamw_small",
        grid_spec=pltpu.PrefetchScalarGridSpec(
            num_scalar_prefetch=1, grid=(1,),
            in_specs=[full(small), pl.BlockSpec((SMALL_ROWS, 128), lambda i, c: (0, c[0]))]
            + [full(a) for a in (*params, *m, *v)],
            out_specs=tuple(full(s) for s in outs)),
        out_shape=tuple(outs), compiler_params=_params(("arbitrary",)),
    )(chip, small, small, *params, *m, *v)
    return flat[0], flat[1:5], [flat[5 + 4 * n:9 + 4 * n] for n in range(3)]


def kernel(x, norm_g, w_in, sinks, conv_w, w_out, final_g, loss_target, m_norm_g, m_w_in, m_sinks, m_conv_w, m_w_out, m_final_g, v_norm_g, v_w_in, v_sinks, v_conv_w, v_w_out, v_final_g):
    seq = x.shape[1]
    x2 = x.reshape(seq, D_MODEL)
    tgt = loss_target.reshape(seq, D_MODEL)
    ng = norm_g.reshape(1, D_MODEL)
    fg = final_g.reshape(1, D_MODEL)
    chip = 2 * lax.axis_index("x") + lax.axis_index("y")

    conv_w8 = jnp.zeros((8, 128), F32).at[0:3].set(conv_w)
    w_in_full = _gather_w_in(w_in.T).reshape(IN_W, D_MODEL)

    q, kd, vd, rest, rope_c, rope_a, rope_b, wo_all, cw_all = _fwd_proj(x2, ng, w_in_full, w_out, conv_w8)
    w_out_full = wo_all.reshape(D_MODEL, D_MODEL)
    probs, d_attn, prod, dga, db, dgc, dcv, dh2, g_wo, g_wo_b, small_m = _fwd_mix(
        x2, q, kd, vd, rest, sinks, cw_all, w_out_full, fg, tgt)
    out_blocks = lambda t: t.reshape(N_CHIPS, W_OUT_BLK, D_MODEL)
    dq, dk, dv, dkh, dvh, grad_w_out = _bwd_mix(
        q, kd, vd, probs, d_attn, prod, rope_c, rope_a, rope_b, out_blocks(g_wo), out_blocks(g_wo_b))
    grad_x, g_wi, small_p = _bwd_proj(x2, ng, dh2, dq, dk, dv, dkh, dvh, dga, db, dgc, dcv, rest, cw_all,
                                      w_in_full, rope_c, rope_a, rope_b)

    g_in_blocks = g_wi.reshape(N_CHIPS, W_IN_BLK, D_MODEL)
    grad_w_in_t, small = _reduce_grads(g_in_blocks, small_m, small_p)

    (upd_wi, upd_wo) = _adamw_weights([(w_in.T, grad_w_in_t, m_w_in.T, v_w_in.T),
                                       (w_out, grad_w_out, m_w_out, v_w_out)])
    row = lambda t: t.reshape(1, -1)
    loss, grads_s, upd_s = _adamw_small(
        chip.reshape(1), small, (ng, fg, conv_w, row(sinks)),
        (row(m_norm_g), row(m_final_g), m_conv_w, row(m_sinks)),
        (row(v_norm_g), row(v_final_g), v_conv_w, row(v_sinks)))

    def named(ng_, fg_, cw_, sk_, wi_t, wo_):
        return [ng_.reshape(D_MODEL), wi_t.T, sk_.reshape(8), cw_, wo_, fg_.reshape(D_MODEL)]

    g_named = named(*grads_s, grad_w_in_t, grad_w_out)
    out = [loss.reshape(()), grad_x.reshape(1, seq, D_MODEL)] + g_named
    for n in range(3):
        out += named(*upd_s[n], upd_wi[n], upd_wo[n])
    return tuple(out)
```

```python
import jax
import jax.numpy as jnp
from jax import lax
from jax.experimental import pallas as pl
from jax.experimental.pallas import tpu as pltpu

F32 = jnp.float32
BF16 = jnp.bfloat16

D_MODEL = 1024
HEAD_DIM = 64
ATTN_W = 512
KV_W = 128
CONV_W = 512
IN_W = 3328
REST_W = IN_W - ATTN_W - 2 * KV_W
BLOCK = 128
ROT_DIM = 16
ROPE_THETA = 500000.0
EPS = 1e-5
SCALE = 0.125
NEG = -1e30

N_CHIPS = 4
W_IN_BLK = IN_W // N_CHIPS
W_OUT_BLK = D_MODEL // N_CHIPS

ADAM_LR = 0.001
ADAM_B1 = 0.9
ADAM_B2 = 0.999
ADAM_EPS = 1e-08
ADAM_WD = 0.01
ADAM_STEP = 10

VMEM_LIMIT = 60 * 1024 * 1024
T_PROJ = 512
T_FMIX = 512
T_MIX = 512
SMALL_ROWS = 8
MESH = pl.DeviceIdType.MESH

_NT = (((1,), (1,)), ((), ()))
_TN = (((0,), (0,)), ((), ()))


def _params(sem=None):
    kw = dict(vmem_limit_bytes=VMEM_LIMIT)
    if sem is not None:
        kw["dimension_semantics"] = sem
    return pltpu.CompilerParams(**kw)


def _sigmoid(t):
    return 1.0 / (1.0 + jnp.exp(-t))


def _shift_down(t, prev8, k):
    rolled = pltpu.roll(t, k, 0)
    row = lax.broadcasted_iota(jnp.int32, t.shape, 0)
    for j in range(k):
        rolled = jnp.where(row == j, prev8[8 - k + j:8 - k + j + 1, :], rolled)
    return rolled


def _shift_up(t, next8, k):
    n = t.shape[0]
    rolled = pltpu.roll(t, n - k, 0)
    row = lax.broadcasted_iota(jnp.int32, t.shape, 0)
    for j in range(k):
        rolled = jnp.where(row == n - k + j, next8[j:j + 1, :], rolled)
    return rolled


def _rope(t, c, a, b):
    w = t.shape[1]
    reps = w // 128
    if reps > 1:
        c, a, b = (jnp.concatenate([z] * reps, axis=1) for z in (c, a, b))
    return t * c + pltpu.roll(t, w - 8, 1) * a + pltpu.roll(t, 8, 1) * b


def _lane_lo(shape):
    return lax.broadcasted_iota(jnp.int32, shape, 1) < HEAD_DIM


def _stack_heads(t, g):
    lo = _lane_lo((BLOCK, 128))
    parts = []
    for hh in range(4):
        pair = t[:, 256 * g + 128 * (hh // 2):256 * g + 128 * (hh // 2) + 128]
        keep = lo if hh % 2 == 0 else jnp.logical_not(lo)
        parts.append(jnp.where(keep, pair, jnp.zeros_like(pair)))
    return jnp.concatenate(parts, axis=0)


def _unstack_pair(o, pp):
    lo = _lane_lo((BLOCK, 128))
    return jnp.where(lo, o[256 * pp:256 * pp + 128], o[256 * pp + 128:256 * pp + 256])


def _sink_col(sinks_ref, g):
    r = lax.broadcasted_iota(jnp.int32, (4 * BLOCK, 1), 0) // BLOCK
    col = jnp.full((4 * BLOCK, 1), sinks_ref[4 * g + 3], F32)
    for hh in range(3):
        col = jnp.where(r == hh, sinks_ref[4 * g + hh], col)
    return col


def _upper():
    r = lax.broadcasted_iota(jnp.int32, (4 * BLOCK, BLOCK), 0) % BLOCK
    return lax.broadcasted_iota(jnp.int32, (4 * BLOCK, BLOCK), 1) > r


def _fold(t):
    return jnp.where(_upper(), t[:, 0:BLOCK], t[:, BLOCK:])


def _unfold(t):
    zero = jnp.zeros_like(t)
    return jnp.concatenate([jnp.where(_upper(), t, zero), jnp.where(_upper(), zero, t)], axis=1)


def _softmax(s, sink_col, has_prev):
    if has_prev is not True:
        s = jnp.concatenate([jnp.where(has_prev, s[:, 0:BLOCK], NEG), s[:, BLOCK:]], axis=1)
    f = _fold(s)
    m = jnp.maximum(jnp.max(f, axis=-1, keepdims=True), sink_col)
    p = jnp.exp(f - m)
    es = jnp.exp(sink_col - m)
    inv = 1.0 / (jnp.sum(p, axis=-1, keepdims=True) + es)
    return (p * inv).astype(BF16), es * inv


def _key_windows(i, nsub, kd_ref, vd_ref, kdp_ref, vdp_ref):
    out = []
    for sb in range(nsub):
        rows = slice(BLOCK * sb, BLOCK * (sb + 1))
        if sb == 0:
            kk = jnp.concatenate([kdp_ref[...], kd_ref[rows, :]], axis=0)
            vv = jnp.concatenate([vdp_ref[...], vd_ref[rows, :]], axis=0)
            out.append((rows, kk, vv, i > 0))
        else:
            both = slice(BLOCK * (sb - 1), BLOCK * (sb + 1))
            out.append((rows, kd_ref[both, :], vd_ref[both, :], True))
    return out


def _gather_w_in(w_in_t):
    hi = W_IN_BLK // 2
    qr = hi // 2

    def body(wi_hbm, wi_all, f_mine, f_other, b_mine, b_other, send_sems, recv_sems, local_sems):
        x, y, c = lax.axis_index("x"), lax.axis_index("y"), lax.axis_index("c")
        me, sibling = (x, y), (x, y, 1 - c)
        xnb, ynb, diag = (1 - x, y), (x, 1 - y), (1 - x, 1 - y)

        loads = [pltpu.make_async_copy(wi_hbm.at[pl.ds(c * hi, hi)], f_mine, local_sems.at[0]),
                 pltpu.make_async_copy(wi_hbm.at[pl.ds((1 - c) * hi, hi)], f_other, local_sems.at[1])]
        for cp in loads:
            cp.start()

        def copy(k, chip, half, quarter, to, src=None):
            r = wi_all.at[2 * chip[0] + chip[1], pl.ds(half * hi + quarter * qr, qr)]
            return pltpu.make_async_remote_copy(src_ref=r if src is None else src, dst_ref=r, send_sem=send_sems.at[k],
                                                recv_sem=recv_sems.at[k], device_id=to, device_id_type=MESH)

        plan = [(0, xnb, 0), (1, ynb, 1), (2, xnb, 1), (3, ynb, 0)]
        loads[0].wait()
        b_mine[...] = f_mine[...].astype(BF16)
        sent = [copy(k, me, c, quarter, (*nb, c), src=b_mine.at[pl.ds(quarter * qr, qr)]) for k, nb, quarter in plan]
        for cp in sent:
            cp.start()
        loads[1].wait()
        b_other[...] = f_other[...].astype(BF16)
        own_slot = wi_all.at[2 * x + y]
        keeps = [pltpu.make_async_copy(b_mine, own_slot.at[pl.ds(c * hi, hi)], local_sems.at[2]),
                 pltpu.make_async_copy(b_other, own_slot.at[pl.ds((1 - c) * hi, hi)], local_sems.at[3])]
        for cp in keeps:
            cp.start()
        arrivals = [(0, xnb, 0), (1, ynb, 1), (2, xnb, 1), (3, ynb, 0), (4, diag, 0), (5, diag, 1)]
        relay = {0: (4, ynb), 1: (5, xnb)}
        for k, chip, quarter in arrivals:
            copy(k, chip, c, quarter, (x, y, c)).wait_recv()
            if k in relay:
                sent.append(copy(relay[k][0], chip, c, quarter, (*relay[k][1], c)))
                sent[-1].start()
            sent.append(copy(6 + k, chip, c, quarter, sibling))
            sent[-1].start()
        for k, chip, quarter in arrivals:
            copy(6 + k, chip, 1 - c, quarter, (x, y, c)).wait_recv()
        for cp in sent:
            cp.wait_send()
        for cp in keeps:
            cp.wait()

    hbm = pl.BlockSpec(memory_space=pl.ANY)
    return pl.pallas_call(
        body, name="gather_w_in",
        out_shape=jax.ShapeDtypeStruct((N_CHIPS, W_IN_BLK, D_MODEL), BF16),
        in_specs=[hbm], out_specs=hbm,
        scratch_shapes=[pltpu.VMEM((hi, D_MODEL), F32), pltpu.VMEM((hi, D_MODEL), F32),
                        pltpu.VMEM((hi, D_MODEL), BF16), pltpu.VMEM((hi, D_MODEL), BF16),
                        pltpu.SemaphoreType.DMA((12,)), pltpu.SemaphoreType.DMA((12,)), pltpu.SemaphoreType.DMA((4,))],
        compiler_params=_params(),
    )(w_in_t)


def _reduce_grads(g_in, *smalls):
    hi = W_IN_BLK // 2
    qr = hi // 2
    q0, q1 = slice(0, qr), slice(qr, hi)

    def body(gi_hbm, s0_ref, s1_ref, gi_out, small_out,
             mine_i, sib_i, out_i, ici_i, small_in, small_ref, send_sems, recv_sems, local_sems):
        x, y, c = lax.axis_index("x"), lax.axis_index("y"), lax.axis_index("c")
        my_dev = 4 * x + 2 * y + c
        sibling = (x, y, 1 - c)
        xnb, ynb = (1 - x, y, c), (x, 1 - y, c)
        order = [(1 - x, 1 - y), (1 - x, y), (x, 1 - y), (x, y)]

        def remote(k, src, dst, to):
            return pltpu.make_async_remote_copy(src_ref=src, dst_ref=dst, send_sem=send_sems.at[k],
                                                recv_sem=recv_sems.at[k], device_id=to, device_id_type=MESH)

        small_ref[...] = s0_ref[...] + s1_ref[...]
        small_cps = []
        for f in range(1, 8):
            fx, fy, fc = f >> 2, (f >> 1) & 1, f & 1
            small_cps.append(remote(11 + f - 1, small_ref, small_in.at[f - 1], (x ^ fx, y ^ fy, c ^ fc)))
        for cp in small_cps:
            cp.start()

        own, to_sib = [], []
        for n, chip in enumerate(order):
            j = 2 * chip[0] + chip[1]
            own.append(pltpu.make_async_copy(gi_hbm.at[j, pl.ds(c * hi, hi)], mine_i.at[n], local_sems.at[n]))
            to_sib.append(remote(6 + n, gi_hbm.at[j, pl.ds((1 - c) * hi, hi)], sib_i.at[n], sibling))
            own[-1].start()
            to_sib[-1].start()

        def pair_sum(n):
            own[n].wait()
            to_sib[n].wait_recv()
            return mine_i[n] + sib_i[n]

        ici = [remote(k, out_i.at[k], ici_i.at[k], xnb if k % 2 == 0 else ynb) for k in range(6)]

        def send(k, rows_f32):
            out_i[k] = rows_f32.astype(BF16)
            ici[k].start()

        p_diag = pair_sum(0)
        send(0, p_diag[q0])
        send(1, p_diag[q1])
        p_x = pair_sum(1)
        send(2, p_x[q0])
        p_y = pair_sum(2)
        send(3, p_y[q1])
        ici[0].wait_recv()
        send(5, p_y[q0] + ici_i[0].astype(F32))
        ici[1].wait_recv()
        send(4, p_x[q1] + ici_i[1].astype(F32))
        p_mine = pair_sum(3)
        swap, keep = [], []
        for quarter, (rows, a, b) in enumerate(((q0, 2, 5), (q1, 4, 3))):
            ici[a].wait_recv()
            ici[b].wait_recv()
            mine_i[3, rows, :] = p_mine[rows] + ici_i[a].astype(F32) + ici_i[b].astype(F32)
            dst = gi_out.at[pl.ds(c * hi + quarter * qr, qr)]
            swap.append(remote(10 + 8 * quarter, mine_i.at[3, rows], dst, sibling))
            keep.append(pltpu.make_async_copy(mine_i.at[3, rows], dst, local_sems.at[4 + quarter]))
            swap[-1].start()
            keep[-1].start()

        for cp in small_cps:
            cp.wait_recv()
        total = jnp.zeros((SMALL_ROWS, D_MODEL), F32)
        for d in range(8):
            slot = jnp.maximum((d ^ my_dev) - 1, 0)
            total = total + jnp.where(d == my_dev, small_ref[...], small_in[slot])
        small_out[...] = total

        for quarter in range(2):
            theirs = gi_out.at[pl.ds((1 - c) * hi + quarter * qr, qr)]
            remote(10 + 8 * quarter, theirs, theirs, sibling).wait_recv()
        for cp in keep:
            cp.wait()
        for cp in to_sib + ici + swap + small_cps:
            cp.wait_send()

    vmem = pl.BlockSpec(memory_space=pltpu.VMEM)
    anyspace = pl.BlockSpec(memory_space=pl.ANY)
    return pl.pallas_call(
        body, name="reduce_grads",
        out_shape=(jax.ShapeDtypeStruct((W_IN_BLK, D_MODEL), F32), jax.ShapeDtypeStruct((SMALL_ROWS, D_MODEL), F32)),
        in_specs=[anyspace, vmem, vmem], out_specs=(anyspace, vmem),
        scratch_shapes=[pltpu.VMEM((N_CHIPS, hi, D_MODEL), F32), pltpu.VMEM((N_CHIPS, hi, D_MODEL), F32),
                        pltpu.VMEM((6, qr, D_MODEL), BF16), pltpu.VMEM((6, qr, D_MODEL), BF16),
                        pltpu.VMEM((7, SMALL_ROWS, D_MODEL), F32), pltpu.VMEM((SMALL_ROWS, D_MODEL), F32),
                        pltpu.SemaphoreType.DMA((19,)), pltpu.SemaphoreType.DMA((19,)),
                        pltpu.SemaphoreType.DMA((6,))],
        compiler_params=_params(),
    )(g_in, *smalls)


def _fwd_proj(x, norm_g, w_in_t, w_out, conv_w8):
    seq = x.shape[0]
    nt = seq // T_PROJ
    lane = jnp.arange(128, dtype=jnp.int32) % HEAD_DIM
    inv_freq = ROPE_THETA ** (-(2 * (lane % 8)).astype(F32) / ROT_DIM)
    inv_freq = jnp.where(lane < ROT_DIM, inv_freq, 0.0).reshape(1, 128)
    in_tile = jnp.arange(T_PROJ, dtype=jnp.int32).astype(F32)[:, None] * inv_freq
    cos_in, sin_in = jnp.cos(in_tile), jnp.sin(in_tile)
    start = jnp.repeat((jnp.arange(nt, dtype=jnp.int32) * T_PROJ).astype(F32), 8)[:, None] * inv_freq
    cos_st, sin_st = jnp.cos(start), jnp.sin(start)

    def body(x_ref, g_ref, w_ref, cs_ref, ss_ref, ci_ref, si_ref, wo_ref, cw_ref,
             q_ref, kd_ref, vd_ref, rest_ref, c_ref, a_ref, b_ref, wo_all, cw_all,
             wo_stage, send_sems, recv_sems, local_sems):
        i = pl.program_id(0)
        mx, my, mc = lax.axis_index("x"), lax.axis_index("y"), lax.axis_index("c")
        chips = [(1 - mx, my), (mx, 1 - my), (1 - mx, 1 - my)]

        def gather(blocks):
            cps = []
            for k, chip in enumerate(chips):
                for n, (src, dst) in enumerate(((wo_stage, wo_all), (cw_ref, cw_all))):
                    cps.append(pltpu.make_async_remote_copy(
                        src_ref=src, dst_ref=dst.at[blocks[k]], send_sem=send_sems.at[2 * k + n],
                        recv_sem=recv_sems.at[2 * k + n], device_id=(*chip, mc), device_id_type=MESH))
            return cps

        me = 2 * mx + my
        own = [pltpu.make_async_copy(wo_stage, wo_all.at[me], local_sems.at[0]),
               pltpu.make_async_copy(cw_ref, cw_all.at[me], local_sems.at[1])]

        @pl.when(i == 0)
        def _():
            wo_stage[...] = wo_ref[...].astype(BF16)
            for cp in own + gather([me] * 3):
                cp.start()

        xf = x_ref[...]
        r1 = lax.rsqrt(jnp.mean(xf * xf, axis=-1, keepdims=True) + EPS)
        xn = (xf * r1 * g_ref[...]).astype(BF16)
        cs, ss = cs_ref[0:1, :], ss_ref[0:1, :]
        c = cs * ci_ref[...] - ss * si_ref[...]
        sin = ss * ci_ref[...] + cs * si_ref[...]
        j = lax.broadcasted_iota(jnp.int32, (T_PROJ, 128), 1) % HEAD_DIM
        a = jnp.where(j < 8, -sin, 0.0)
        b = jnp.where(j >= 8, sin, 0.0)
        c_ref[...], a_ref[...], b_ref[...] = c, a, b
        proj = lambda lo_c, w: lax.dot_general(xn, w_ref[lo_c:lo_c + w, :], _NT, preferred_element_type=F32)
        q_ref[...] = (_rope(proj(0, ATTN_W), c, a, b) * SCALE).astype(BF16)
        kv = proj(ATTN_W, 2 * KV_W)
        k = _rope(kv[:, 0:KV_W], c, a, b)
        v = kv[:, KV_W:2 * KV_W]
        lo = _lane_lo(k.shape)
        for t, ref in ((k, kd_ref), (v, vd_ref)):
            sw = pltpu.roll(t, HEAD_DIM, 1)
            ref[:, 0:128] = jnp.where(lo, t, sw).astype(BF16)
            ref[:, 128:256] = jnp.where(lo, sw, t).astype(BF16)
        for n in range(REST_W // 512):
            rest_ref[:, 512 * n:512 * (n + 1)] = proj(ATTN_W + 2 * KV_W + 512 * n, 512)

        @pl.when(i == nt - 1)
        def _():
            sent = gather([me] * 3)
            for cp in gather([2 * chip[0] + chip[1] for chip in chips]):
                cp.wait_recv()
            for cp in sent:
                cp.wait_send()
            for cp in own:
                cp.wait()

    tile = lambda w: pl.BlockSpec((T_PROJ, w), lambda i: (i, 0))
    whole = lambda r, w: pl.BlockSpec((r, w), lambda i: (0, 0))
    vmem = pl.BlockSpec(memory_space=pltpu.VMEM)
    hbm = pl.BlockSpec(memory_space=pl.ANY)
    return pl.pallas_call(
        body, name="fwd_proj", grid=(nt,),
        out_shape=(jax.ShapeDtypeStruct((seq, ATTN_W), BF16), jax.ShapeDtypeStruct((seq, 2 * KV_W), BF16),
                   jax.ShapeDtypeStruct((seq, 2 * KV_W), BF16), jax.ShapeDtypeStruct((seq, REST_W), F32))
        + (jax.ShapeDtypeStruct((seq, 128), F32),) * 3
        + (jax.ShapeDtypeStruct((N_CHIPS, W_OUT_BLK, D_MODEL), BF16), jax.ShapeDtypeStruct((N_CHIPS, 8, 128), F32)),
        in_specs=[tile(D_MODEL), whole(1, D_MODEL), whole(IN_W, D_MODEL), pl.BlockSpec((8, 128), lambda i: (i, 0)),
                  pl.BlockSpec((8, 128), lambda i: (i, 0)), whole(T_PROJ, 128), whole(T_PROJ, 128), vmem, vmem],
        out_specs=(tile(ATTN_W), tile(2 * KV_W), tile(2 * KV_W), tile(REST_W), tile(128), tile(128), tile(128),
                   hbm, hbm),
        scratch_shapes=[pltpu.VMEM((W_OUT_BLK, D_MODEL), BF16), pltpu.SemaphoreType.DMA((6,)),
                        pltpu.SemaphoreType.DMA((6,)), pltpu.SemaphoreType.DMA((2,))],
        compiler_params=_params(("arbitrary",)),
    )(x, norm_g, w_in_t, cos_st, sin_st, cos_in, sin_in, w_out, conv_w8)


CONV_SPEC = pl.BlockSpec((N_CHIPS, 8, 128), lambda i: (0, 0, 0))


def _conv_rows(cw_ref):
    return jnp.concatenate([cw_ref[j] for j in range(N_CHIPS)], axis=1)


def _conv_parts(rest_ref, prev_ref, cw_ref, first):
    u = rest_ref[:, 1024:1536] * rest_ref[:, 1536:2048]
    up = prev_ref[:, 1024:1536] * prev_ref[:, 1536:2048]
    up = jnp.where(first, jnp.zeros_like(up), up)
    um1 = _shift_down(u, up, 1)
    um2 = _shift_down(u, up, 2)
    cw = _conv_rows(cw_ref)
    cv = cw[0:1, :] * um2 + cw[1:2, :] * um1 + cw[2:3, :] * u
    return u, um1, um2, cv


def _fwd_mix(x, q, kd, vd, rest, sinks, conv_w, w_out, final_g, target):
    seq = x.shape[0]
    nt = seq // T_FMIX
    nsub = T_FMIX // BLOCK
    assert D_MODEL == 2 * T_FMIX

    def body(sinks_ref, x_ref, q_ref, kd_ref, vd_ref, kdp_ref, vdp_ref, rest_ref, restp_ref, cw_ref, wo_ref,
             fg_ref, tgt_ref, prob_ref, do_ref, prod_ref, dga_ref, db_ref, dgc_ref, dcv_ref, dh2_ref,
             gwo_hbm, gwob_hbm, small_ref,
             pmix_ref, pdh2_ref, gwo_ref, gate_ref, dsink_ref, out_sems):
        i = pl.program_id(0)
        attn_ref, sinkw_ref = prod_ref, dcv_ref

        @pl.when(i == 0)
        def _():
            small_ref[...] = jnp.zeros_like(small_ref)
            dsink_ref[...] = jnp.zeros_like(dsink_ref)
            gwo_ref[...] = jnp.zeros_like(gwo_ref)
            pmix_ref[...] = jnp.zeros_like(pmix_ref)
            pdh2_ref[...] = jnp.zeros_like(pdh2_ref)

        chains = []
        for rows, kk, vv, has_prev in _key_windows(i, nsub, kd_ref, vd_ref, kdp_ref, vdp_ref):
            qt = q_ref[rows, :]
            for g in range(2):
                kg = kk[:, 128 * g:128 * (g + 1)]
                chains.append(dict(g=g, rows=rows, has_prev=has_prev, vg=vv[:, 128 * g:128 * (g + 1)],
                                   s=lax.dot_general(_stack_heads(qt, g), kg, _NT, preferred_element_type=F32)))
        gwo_ref[...] += lax.dot_general(pmix_ref[...], pdh2_ref[...], _TN, preferred_element_type=F32)
        for ch in chains:
            ch["prob"], ch["psink"] = _softmax(ch.pop("s"), _sink_col(sinks_ref, ch["g"]), ch["has_prev"])
        for k, ch in enumerate(chains):
            prob_ref[k] = ch["prob"]
            o = jnp.dot(_unfold(ch["prob"]), ch["vg"], preferred_element_type=F32)
            ow = o * ch["psink"]
            for pp in range(2):
                lanes = slice(256 * ch["g"] + 128 * pp, 256 * ch["g"] + 128 * (pp + 1))
                attn_ref[ch["rows"], lanes] = _unstack_pair(o, pp)
                sinkw_ref[ch["rows"], lanes] = _unstack_pair(ow, pp)

        def silu_parts(t, lo_c):
            sg = _sigmoid(t)
            silu = t * sg
            gate_ref[:, lo_c:lo_c + 512] = silu
            gate_ref[:, lo_c + 512:lo_c + 1024] = sg * (1.0 + t * (1.0 - sg))
            return silu

        pmix_ref[:, 0:ATTN_W] = (attn_ref[...] * silu_parts(rest_ref[:, 0:512], 0)).astype(BF16)
        u, um1, um2, cv = _conv_parts(rest_ref, restp_ref, cw_ref, i == 0)
        pmix_ref[:, ATTN_W:] = (rest_ref[:, 512:1024] * cv * silu_parts(rest_ref[:, 2048:2560], 1024)).astype(BF16)

        h2 = x_ref[...] + jnp.dot(pmix_ref[...], wo_ref[...], preferred_element_type=F32)
        r2 = lax.rsqrt(jnp.mean(h2 * h2, axis=-1, keepdims=True) + EPS)
        n2 = h2 * r2
        err = n2 * fg_ref[...] - tgt_ref[...]
        dy = err * (1.0 / D_MODEL)
        small_ref[6:7, :] += jnp.sum(err * err, axis=0, keepdims=True) * (0.5 / D_MODEL)
        small_ref[1:2, :] += jnp.sum(dy * n2, axis=0, keepdims=True)
        dn = dy * fg_ref[...]
        dh2 = r2 * (dn - n2 * jnp.mean(dn * n2, axis=-1, keepdims=True))
        dh2_ref[...] = dh2
        pdh2_ref[...] = dh2.astype(BF16)

        d_mix = lambda lo_r: lax.dot_general(pdh2_ref[...], wo_ref[lo_r:lo_r + 512, :], _NT, preferred_element_type=F32)
        dma = d_mix(0)
        dga_ref[...] = (dma * attn_ref[...] * gate_ref[:, 512:1024]).astype(BF16)
        d_attn = dma * gate_ref[:, 0:512]
        do_ref[...] = d_attn.astype(BF16)
        prod_ref[...] = d_attn * attn_ref[...]
        dsink_ref[0:1, :] += jnp.sum(d_attn * sinkw_ref[...], axis=0, keepdims=True)
        bg = rest_ref[:, 512:1024]
        dmc = d_mix(ATTN_W)
        t1 = dmc * gate_ref[:, 1024:1536]
        db_ref[...] = (t1 * cv).astype(BF16)
        dcv = t1 * bg
        dcv_ref[...] = dcv
        dgc_ref[...] = (dmc * (bg * cv) * gate_ref[:, 1536:2048]).astype(BF16)
        small_ref[2:3, 0:CONV_W] += jnp.sum(dcv * um2, axis=0, keepdims=True)
        small_ref[3:4, 0:CONV_W] += jnp.sum(dcv * um1, axis=0, keepdims=True)
        small_ref[4:5, 0:CONV_W] += jnp.sum(dcv * u, axis=0, keepdims=True)

        @pl.when(i == nt - 1)
        def _():
            head = lax.broadcasted_iota(jnp.int32, (1, ATTN_W), 1) // HEAD_DIM
            for h in range(8):
                tot = jnp.sum(jnp.where(head == h, dsink_ref[0:1, :], 0.0), axis=-1, keepdims=True)
                small_ref[5:6, h:h + 1] = -tot
            gwo_ref[...] += lax.dot_general(pmix_ref[...], pdh2_ref[...], _TN, preferred_element_type=F32)
            outs = [pltpu.make_async_copy(gwo_ref, gwo_hbm, out_sems.at[0])]
            outs[0].start()
            for n, stage in enumerate((pmix_ref, pdh2_ref)):
                slab = slice(T_FMIX * n, T_FMIX * (n + 1))
                stage[...] = gwo_ref[slab, :].astype(BF16)
                outs.append(pltpu.make_async_copy(stage, gwob_hbm.at[slab], out_sems.at[1 + n]))
                outs[-1].start()
            for cp in outs:
                cp.wait()

    tile = lambda w: pl.BlockSpec((T_FMIX, w), lambda i: (i, 0))
    whole = lambda r, w: pl.BlockSpec((r, w), lambda i: (0, 0))
    prev_blk = pl.BlockSpec((BLOCK, 2 * KV_W), lambda i: (jnp.maximum(i * nsub - 1, 0), 0))
    prev8 = pl.BlockSpec((8, REST_W), lambda i: (jnp.maximum(i * (T_FMIX // 8) - 1, 0), 0))
    bf = lambda w: jax.ShapeDtypeStruct((seq, w), BF16)
    f32 = lambda w: jax.ShapeDtypeStruct((seq, w), F32)
    return pl.pallas_call(
        body, name="fwd_mix", grid=(nt,),
        out_shape=(jax.ShapeDtypeStruct((2 * seq // BLOCK, 4 * BLOCK, BLOCK), BF16),
                   bf(ATTN_W), f32(ATTN_W), bf(ATTN_W), bf(CONV_W), bf(CONV_W), f32(CONV_W), f32(D_MODEL),
                   jax.ShapeDtypeStruct((D_MODEL, D_MODEL), F32), jax.ShapeDtypeStruct((D_MODEL, D_MODEL), BF16),
                   jax.ShapeDtypeStruct((SMALL_ROWS, D_MODEL), F32)),
        in_specs=[pl.BlockSpec(memory_space=pltpu.SMEM), tile(D_MODEL), tile(ATTN_W), tile(2 * KV_W), tile(2 * KV_W),
                  prev_blk, prev_blk, tile(REST_W), prev8, CONV_SPEC,
                  pl.BlockSpec((D_MODEL, D_MODEL), lambda i: (0, 0), pipeline_mode=pl.Buffered(1)),
                  whole(1, D_MODEL), tile(D_MODEL)],
        out_specs=(pl.BlockSpec((2 * nsub, 4 * BLOCK, BLOCK), lambda i: (i, 0, 0)),
                   tile(ATTN_W), tile(ATTN_W), tile(ATTN_W), tile(CONV_W), tile(CONV_W), tile(CONV_W), tile(D_MODEL),
                   pl.BlockSpec(memory_space=pl.ANY), pl.BlockSpec(memory_space=pl.ANY), whole(SMALL_ROWS, D_MODEL)),
        scratch_shapes=[pltpu.VMEM((T_FMIX, D_MODEL), BF16)] * 2 + [
            pltpu.VMEM((D_MODEL, D_MODEL), F32), pltpu.VMEM((T_FMIX, 4 * 512), F32), pltpu.VMEM((8, ATTN_W), F32),
            pltpu.SemaphoreType.DMA((3,))],
        compiler_params=_params(("arbitrary",)),
    )(sinks, x, q, kd, vd, kd, vd, rest, rest, conv_w, w_out, final_g, target)


def _scatter_copies(g_hbm, gb_hbm, mine, land, send_sems, recv_sems, local_sem, half):
    x, y, c = lax.axis_index("x"), lax.axis_index("y"), lax.axis_index("c")
    cps = []
    for f in range(1, 8):
        to = (x ^ (f >> 2), y ^ ((f >> 1) & 1), c ^ (f & 1))
        src = gb_hbm.at[2 * to[0] + to[1], pl.ds(to[2] * half, half)]
        cps.append(pltpu.make_async_remote_copy(src_ref=src, dst_ref=land.at[f - 1], send_sem=send_sems.at[f - 1],
                                                recv_sem=recv_sems.at[f - 1], device_id=to, device_id_type=MESH))
    own = pltpu.make_async_copy(g_hbm.at[2 * x + y, pl.ds(c * half, half)], mine, local_sem)
    return cps, own


def _scatter_finish(cps, own, mine, land, out_hbm, send_sems, recv_sems, local_sem, half):
    x, y, c = lax.axis_index("x"), lax.axis_index("y"), lax.axis_index("c")
    own.wait()
    tot = mine[...]
    for f in range(1, 8):
        cps[f - 1].wait_recv()
        tot = tot + land[f - 1].astype(F32)
    mine[...] = tot

    def swap(rows_of):
        return pltpu.make_async_remote_copy(src_ref=mine, dst_ref=out_hbm.at[pl.ds(rows_of * half, half)],
                                            send_sem=send_sems.at[7], recv_sem=recv_sems.at[7],
                                            device_id=(x, y, 1 - c), device_id_type=MESH)

    keep = pltpu.make_async_copy(mine, out_hbm.at[pl.ds(c * half, half)], local_sem)
    keep.start()
    swap(c).start()
    swap(1 - c).wait_recv()
    keep.wait()
    for cp in cps:
        cp.wait_send()
    swap(c).wait_send()


def _bwd_mix(q, kd, vd, probs, d_attn, prod, rope_c, rope_a, rope_b, g_out, g_out_b):
    seq = q.shape[0]
    nt = seq // T_MIX
    nsub = T_MIX // BLOCK
    ho = W_OUT_BLK // 2

    def body(q_ref, kd_ref, vd_ref, kdp_ref, vdp_ref, prob_ref, do_ref, prod_ref, c_ref, a_ref, b_ref, go_hbm, gob_hbm,
             dq_ref, dk_ref, dv_ref, dkh_ref, dvh_ref, go_out,
             mine_o, land_o, send_sems, recv_sems, local_sems):
        i = pl.program_id(0)
        scatter = (mine_o, land_o, send_sems, recv_sems, local_sems.at[0], ho)

        @pl.when(i == 0)
        def _():
            cps, own = _scatter_copies(go_hbm, gob_hbm, *scatter)
            for cp in cps + [own]:
                cp.start()

        lo = _lane_lo((2 * BLOCK, 128))
        dk_blocks = [None] * (nsub + 1)
        dv_blocks = [None] * (nsub + 1)

        def add(lst, n, val):
            lst[n] = val if lst[n] is None else lst[n] + val

        chains = []
        for rows, kk, vv, _ in _key_windows(i, nsub, kd_ref, vd_ref, kdp_ref, vdp_ref):
            qt = q_ref[rows, :]
            dot = do_ref[rows, :]
            for g in range(2):
                rs = jnp.sum(_stack_heads(prod_ref[rows, :], g), axis=-1, keepdims=True)
                chains.append(dict(g=g, rows=rows, rs=rs, qs=_stack_heads(qt, g), dos=_stack_heads(dot, g),
                                   kg=kk[:, 128 * g:128 * (g + 1)], vg=vv[:, 128 * g:128 * (g + 1)]))
        for ch in chains:
            ch["dp"] = lax.dot_general(ch["dos"], ch["vg"], _NT, preferred_element_type=F32)
        for k, ch in enumerate(chains):
            ch["ds"] = _unfold((prob_ref[k].astype(F32) * (_fold(ch["dp"]) - ch["rs"])).astype(BF16))
        for k, ch in enumerate(chains):
            dqs = jnp.dot(ch["ds"], ch["kg"], preferred_element_type=F32) * SCALE
            c, a, b = c_ref[ch["rows"], :], a_ref[ch["rows"], :], b_ref[ch["rows"], :]
            for pp in range(2):
                lanes = slice(256 * ch["g"] + 128 * pp, 256 * ch["g"] + 128 * (pp + 1))
                dq_ref[ch["rows"], lanes] = _rope(_unstack_pair(dqs, pp), c, -a, -b).astype(BF16)
            dkd = lax.dot_general(ch["ds"], ch["qs"], _TN, preferred_element_type=F32)
            dvd = lax.dot_general(_unfold(prob_ref[k]), ch["dos"], _TN, preferred_element_type=F32)
            ch["dk"] = dkd + pltpu.roll(dkd, HEAD_DIM, 1)
            ch["dv"] = dvd + pltpu.roll(dvd, HEAD_DIM, 1)
        for sb in range(nsub):
            dk2 = jnp.where(lo, chains[2 * sb]["dk"], chains[2 * sb + 1]["dk"])
            dv2 = jnp.where(lo, chains[2 * sb]["dv"], chains[2 * sb + 1]["dv"])
            add(dk_blocks, sb, dk2[0:BLOCK])
            add(dk_blocks, sb + 1, dk2[BLOCK:])
            add(dv_blocks, sb, dv2[0:BLOCK])
            add(dv_blocks, sb + 1, dv2[BLOCK:])
        dkh_ref[0] = dk_blocks[0]
        dvh_ref[0] = dv_blocks[0]
        for sb in range(nsub):
            dk_ref[BLOCK * sb:BLOCK * (sb + 1), :] = dk_blocks[sb + 1]
            dv_ref[BLOCK * sb:BLOCK * (sb + 1), :] = dv_blocks[sb + 1]

        @pl.when(i == nt - 1)
        def _():
            cps, own = _scatter_copies(go_hbm, gob_hbm, *scatter)
            _scatter_finish(cps, own, mine_o, land_o, go_out, send_sems, recv_sems, local_sems.at[1], ho)

    tile = lambda w: pl.BlockSpec((T_MIX, w), lambda i: (i, 0))
    prev_blk = pl.BlockSpec((BLOCK, 2 * KV_W), lambda i: (jnp.maximum(i * nsub - 1, 0), 0))
    halo = pl.BlockSpec((1, BLOCK, KV_W), lambda i: (i, 0, 0))
    hbm = pl.BlockSpec(memory_space=pl.ANY)
    f32 = lambda w: jax.ShapeDtypeStruct((seq, w), F32)
    return pl.pallas_call(
        body, name="bwd_mix", grid=(nt,),
        out_shape=(jax.ShapeDtypeStruct((seq, ATTN_W), BF16), f32(KV_W), f32(KV_W),
                   jax.ShapeDtypeStruct((nt, BLOCK, KV_W), F32), jax.ShapeDtypeStruct((nt, BLOCK, KV_W), F32),
                   jax.ShapeDtypeStruct((W_OUT_BLK, D_MODEL), F32)),
        in_specs=[tile(ATTN_W), tile(2 * KV_W), tile(2 * KV_W), prev_blk, prev_blk,
                  pl.BlockSpec((2 * nsub, 4 * BLOCK, BLOCK), lambda i: (i, 0, 0)), tile(ATTN_W), tile(ATTN_W),
                  tile(128), tile(128), tile(128), hbm, hbm],
        out_specs=(tile(ATTN_W), tile(KV_W), tile(KV_W), halo, halo, hbm),
        scratch_shapes=[pltpu.VMEM((ho, D_MODEL), F32), pltpu.VMEM((7, ho, D_MODEL), BF16),
                        pltpu.SemaphoreType.DMA((8,)), pltpu.SemaphoreType.DMA((8,)), pltpu.SemaphoreType.DMA((2,))],
        compiler_params=_params(("arbitrary",)),
    )(q, kd, vd, kd, vd, probs, d_attn, prod, rope_c, rope_a, rope_b, g_out, g_out_b)


def _bwd_proj(x, norm_g, dh2, dq, dk, dv, dkh, dvh, dga, db, dgc, dcv, rest, conv_w, w_in_t, rope_c, rope_a, rope_b):
    seq = x.shape[0]
    tb = T_PROJ
    per = tb // T_MIX
    nt = seq // tb

    def body(x_ref, g_ref, dh2_ref, dq_ref, dk_ref, dv_ref, dkh_ref, dvh_ref, dkn_ref, dvn_ref, dga_ref, db_ref,
             dgc_ref, dcv_ref, dcvn_ref, ch_ref, cw_ref, w_ref, c_ref, a_ref, b_ref, gx_ref, gw_hbm, small_ref,
             dp_ref, acc_ref, out_sem):
        i = pl.program_id(0)

        @pl.when(i == 0)
        def _():
            small_ref[...] = jnp.zeros_like(small_ref)
            acc_ref[...] = jnp.zeros_like(acc_ref)

        last = i == nt - 1
        keep = jnp.where(last, 0.0, 1.0)
        pad = jnp.zeros((T_MIX - BLOCK, KV_W), F32)

        def with_halos(main_ref, halo_ref, next_ref):
            parts = []
            for m in range(1, per + 1):
                parts += [pad, halo_ref[m] if m < per else next_ref[0] * keep]
            return main_ref[...] + jnp.concatenate(parts, axis=0)

        dk = with_halos(dk_ref, dkh_ref, dkn_ref)
        dv = with_halos(dv_ref, dvh_ref, dvn_ref)
        dp_ref[:, 0:ATTN_W] = dq_ref[...]
        dp_ref[:, ATTN_W:ATTN_W + KV_W] = _rope(dk, c_ref[...], -a_ref[...], -b_ref[...]).astype(BF16)
        dp_ref[:, ATTN_W + KV_W:ATTN_W + 2 * KV_W] = dv.astype(BF16)
        base = ATTN_W + 2 * KV_W
        dp_ref[:, base:base + 512] = dga_ref[...]
        dp_ref[:, base + 512:base + 1024] = db_ref[...]
        dcv = dcv_ref[...]
        nxt = dcvn_ref[...] * keep
        cw = _conv_rows(cw_ref)
        du = cw[2:3, :] * dcv + cw[1:2, :] * _shift_up(dcv, nxt, 1) + cw[0:1, :] * _shift_up(dcv, nxt, 2)
        dp_ref[:, base + 1024:base + 1536] = (du * ch_ref[:, 512:1024]).astype(BF16)
        dp_ref[:, base + 1536:base + 2048] = (du * ch_ref[:, 0:512]).astype(BF16)
        dp_ref[:, base + 2048:base + 2560] = dgc_ref[...]

        xf = x_ref[...]
        r1 = lax.rsqrt(jnp.mean(xf * xf, axis=-1, keepdims=True) + EPS)
        n1 = xf * r1
        xn = (n1 * g_ref[...]).astype(BF16)
        for n in range(IN_W // 256):
            cols = slice(256 * n, 256 * (n + 1))
            acc_ref[cols, :] += lax.dot_general(dp_ref[:, cols], xn, _TN, preferred_element_type=F32)
        acc_out = pltpu.make_async_copy(acc_ref, gw_hbm, out_sem.at[0])

        @pl.when(last)
        def _():
            acc_out.start()

        dxn = jnp.dot(dp_ref[...], w_ref[...], preferred_element_type=F32)
        small_ref[0:1, :] += jnp.sum(dxn * n1, axis=0, keepdims=True)
        dxg = dxn * g_ref[...]
        gx_ref[...] = r1 * (dxg - n1 * jnp.mean(dxg * n1, axis=-1, keepdims=True)) + dh2_ref[...]

        @pl.when(last)
        def _():
            acc_out.wait()

    tile = lambda w: pl.BlockSpec((tb, w), lambda i: (i, 0))
    whole = lambda r, w: pl.BlockSpec((r, w), lambda i: (0, 0))
    halo = pl.BlockSpec((per, BLOCK, KV_W), lambda i: (i, 0, 0))
    halo_next = pl.BlockSpec((1, BLOCK, KV_W), lambda i: (jnp.minimum((i + 1) * per, seq // T_MIX - 1), 0, 0))
    next8 = pl.BlockSpec((8, CONV_W), lambda i: (jnp.minimum((i + 1) * (tb // 8), seq // 8 - 1), 0))
    ch = pl.BlockSpec((tb, 1024), lambda i: (i, 1))
    return pl.pallas_call(
        body, name="bwd_proj", grid=(nt,),
        out_shape=(jax.ShapeDtypeStruct((seq, D_MODEL), F32), jax.ShapeDtypeStruct((IN_W, D_MODEL), F32),
                   jax.ShapeDtypeStruct((SMALL_ROWS, D_MODEL), F32)),
        in_specs=[tile(D_MODEL), whole(1, D_MODEL), tile(D_MODEL), tile(ATTN_W), tile(KV_W), tile(KV_W), halo, halo,
                  halo_next, halo_next,
                  tile(ATTN_W), tile(CONV_W), tile(CONV_W), tile(CONV_W), next8, ch, CONV_SPEC,
                  pl.BlockSpec((IN_W, D_MODEL), lambda i: (0, 0), pipeline_mode=pl.Buffered(1)),
                  tile(128), tile(128), tile(128)],
        out_specs=(tile(D_MODEL), pl.BlockSpec(memory_space=pl.ANY), whole(SMALL_ROWS, D_MODEL)),
        scratch_shapes=[pltpu.VMEM((tb, IN_W), BF16), pltpu.VMEM((IN_W, D_MODEL), F32), pltpu.SemaphoreType.DMA((1,))],
        compiler_params=_params(("arbitrary",)),
    )(x, norm_g, dh2, dq, dk, dv, dkh, dvh, dkh, dvh, dga, db, dgc, dcv, dcv, rest, conv_w, w_in_t,
      rope_c, rope_a, rope_b)


def _adamw_step(w, g, m, v):
    m2 = ADAM_B1 * m + (1.0 - ADAM_B1) * g
    v2 = ADAM_B2 * v + (1.0 - ADAM_B2) * jnp.square(g)
    m_hat = m2 / (1.0 - ADAM_B1 ** ADAM_STEP)
    v_hat = v2 / (1.0 - ADAM_B2 ** ADAM_STEP)
    return -ADAM_LR * (m_hat / (jnp.sqrt(v_hat) + ADAM_EPS) + ADAM_WD * w), m2, v2


def _adamw_weights(groups):
    steps = 4

    def body(*refs):
        ins, outs = refs[:4 * len(groups)], refs[4 * len(groups):]
        for k in range(len(groups)):
            res = _adamw_step(*(r[...] for r in ins[4 * k:4 * k + 4]))
            for o_ref, val in zip(outs[3 * k:3 * k + 3], res):
                o_ref[...] = val

    in_specs, out_specs, out_shape = [], [], []
    for w, _, _, _ in groups:
        rows, cols = w.shape
        spec = pl.BlockSpec((rows // steps, cols), lambda i: (i, 0))
        in_specs += [spec] * 4
        out_specs += [spec] * 3
        out_shape += [jax.ShapeDtypeStruct((rows, cols), F32)] * 3
    flat = pl.pallas_call(
        body, name="adamw_weights", grid=(steps,), out_shape=tuple(out_shape), in_specs=in_specs,
        out_specs=tuple(out_specs), compiler_params=_params(("arbitrary",)),
    )(*[a for grp in groups for a in grp])
    return [flat[3 * k:3 * k + 3] for k in range(len(groups))]


def _adamw_small(chip, small, params, m, v):
    def body(chip_ref, small_ref, conv_ref, *refs):
        ins, outs = refs[:12], refs[12:]
        outs[0][...] = jnp.sum(small_ref[6:7, :], axis=-1, keepdims=True)
        grads = (small_ref[0:1, :], small_ref[1:2, :], conv_ref[2:5, :], small_ref[5:6, 0:8])
        for k, g in enumerate(grads):
            outs[1 + k][...] = g
            res = _adamw_step(ins[k][...], g, ins[4 + k][...], ins[8 + k][...])
            for n, val in enumerate(res):
                outs[5 + 4 * n + k][...] = val

    full = lambda a: pl.BlockSpec(a.shape, lambda i, c: (0,) * len(a.shape))
    shapes = [jax.ShapeDtypeStruct(p.shape, F32) for p in params]
    outs = [jax.ShapeDtypeStruct((1, 1), F32)] + shapes * 4
    flat = pl.pallas_call(
        body, name="adamw_small",
        grid_spec=pltpu.PrefetchScalarGridSpec(
            num_scalar_prefetch=1, grid=(1,),
            in_specs=[full(small), pl.BlockSpec((SMALL_ROWS, 128), lambda i, c: (0, c[0]))]
            + [full(a) for a in (*params, *m, *v)],
            out_specs=tuple(full(s) for s in outs)),
        out_shape=tuple(outs), compiler_params=_params(("arbitrary",)),
    )(chip, small, small, *params, *m, *v)
    return flat[0], flat[1:5], [flat[5 + 4 * n:9 + 4 * n] for n in range(3)]


def kernel(x, norm_g, w_in, sinks, conv_w, w_out, final_g, loss_target, m_norm_g, m_w_in, m_sinks, m_conv_w, m_w_out, m_final_g, v_norm_g, v_w_in, v_sinks, v_conv_w, v_w_out, v_final_g):
    seq = x.shape[1]
    x2 = x.reshape(seq, D_MODEL)
    tgt = loss_target.reshape(seq, D_MODEL)
    ng = norm_g.reshape(1, D_MODEL)
    fg = final_g.reshape(1, D_MODEL)
    chip = 2 * lax.axis_index("x") + lax.axis_index("y")

    conv_w8 = jnp.zeros((8, 128), F32).at[0:3].set(conv_w)
    w_in_full = _gather_w_in(w_in.T).reshape(IN_W, D_MODEL)

    q, kd, vd, rest, rope_c, rope_a, rope_b, wo_all, cw_all = _fwd_proj(x2, ng, w_in_full, w_out, conv_w8)
    w_out_full = wo_all.reshape(D_MODEL, D_MODEL)
    probs, d_attn, prod, dga, db, dgc, dcv, dh2, g_wo, g_wo_b, small_m = _fwd_mix(
        x2, q, kd, vd, rest, sinks, cw_all, w_out_full, fg, tgt)
    out_blocks = lambda t: t.reshape(N_CHIPS, W_OUT_BLK, D_MODEL)
    dq, dk, dv, dkh, dvh, grad_w_out = _bwd_mix(
        q, kd, vd, probs, d_attn, prod, rope_c, rope_a, rope_b, out_blocks(g_wo), out_blocks(g_wo_b))
    grad_x, g_wi, small_p = _bwd_proj(x2, ng, dh2, dq, dk, dv, dkh, dvh, dga, db, dgc, dcv, rest, cw_all,
                                      w_in_full, rope_c, rope_a, rope_b)

    g_in_blocks = g_wi.reshape(N_CHIPS, W_IN_BLK, D_MODEL)
    grad_w_in_t, small = _reduce_grads(g_in_blocks, small_m, small_p)

    (upd_wi, upd_wo) = _adamw_weights([(w_in.T, grad_w_in_t, m_w_in.T, v_w_in.T),
                                       (w_out, grad_w_out, m_w_out, v_w_out)])
    row = lambda t: t.reshape(1, -1)
    loss, grads_s, upd_s = _adamw_small(
        chip.reshape(1), small, (ng, fg, conv_w, row(sinks)),
        (row(m_norm_g), row(m_final_g), m_conv_w, row(m_sinks)),
        (row(v_norm_g), row(v_final_g), v_conv_w, row(v_sinks)))

    def named(ng_, fg_, cw_, sk_, wi_t, wo_):
        return [ng_.reshape(D_MODEL), wi_t.T, sk_.reshape(8), cw_, wo_, fg_.reshape(D_MODEL)]

    g_named = named(*grads_s, grad_w_in_t, grad_w_out)
    out = [loss.reshape(()), grad_x.reshape(1, seq, D_MODEL)] + g_named
    for n in range(3):
        out += named(*upd_s[n], upd_wi[n], upd_wo[n])
    return tuple(out)
```

```python
import jax
import jax.numpy as jnp
from jax import lax
from jax.experimental import pallas as pl
from jax.experimental.pallas import tpu as pltpu

F32 = jnp.float32
BF16 = jnp.bfloat16

D_MODEL = 1024
HEAD_DIM = 64
ATTN_W = 512
KV_W = 128
CONV_W = 512
IN_W = 3328
REST_W = IN_W - ATTN_W - 2 * KV_W
BLOCK = 128
ROT_DIM = 16
ROPE_THETA = 500000.0
EPS = 1e-5
SCALE = 0.125
NEG = -1e30

N_CHIPS = 4
W_IN_BLK = IN_W // N_CHIPS
W_OUT_BLK = D_MODEL // N_CHIPS

ADAM_LR = 0.001
ADAM_B1 = 0.9
ADAM_B2 = 0.999
ADAM_EPS = 1e-08
ADAM_WD = 0.01
ADAM_STEP = 10

VMEM_LIMIT = 60 * 1024 * 1024
T_PROJ = 512
T_FMIX = 512
T_MIX = 512
SMALL_ROWS = 8
MESH = pl.DeviceIdType.MESH

_NT = (((1,), (1,)), ((), ()))
_TN = (((0,), (0,)), ((), ()))


def _params(sem=None):
    kw = dict(vmem_limit_bytes=VMEM_LIMIT)
    if sem is not None:
        kw["dimension_semantics"] = sem
    return pltpu.CompilerParams(**kw)


def _sigmoid(t):
    return 1.0 / (1.0 + jnp.exp(-t))


def _shift_down(t, prev8, k):
    rolled = pltpu.roll(t, k, 0)
    row = lax.broadcasted_iota(jnp.int32, t.shape, 0)
    for j in range(k):
        rolled = jnp.where(row == j, prev8[8 - k + j:8 - k + j + 1, :], rolled)
    return rolled


def _shift_up(t, next8, k):
    n = t.shape[0]
    rolled = pltpu.roll(t, n - k, 0)
    row = lax.broadcasted_iota(jnp.int32, t.shape, 0)
    for j in range(k):
        rolled = jnp.where(row == n - k + j, next8[j:j + 1, :], rolled)
    return rolled


def _rope(t, c, a, b):
    w = t.shape[1]
    reps = w // 128
    if reps > 1:
        c, a, b = (jnp.concatenate([z] * reps, axis=1) for z in (c, a, b))
    return t * c + pltpu.roll(t, w - 8, 1) * a + pltpu.roll(t, 8, 1) * b


def _lane_lo(shape):
    return lax.broadcasted_iota(jnp.int32, shape, 1) < HEAD_DIM


def _stack_heads(t, g):
    lo = _lane_lo((BLOCK, 128))
    parts = []
    for hh in range(4):
        pair = t[:, 256 * g + 128 * (hh // 2):256 * g + 128 * (hh // 2) + 128]
        keep = lo if hh % 2 == 0 else jnp.logical_not(lo)
        parts.append(jnp.where(keep, pair, jnp.zeros_like(pair)))
    return jnp.concatenate(parts, axis=0)


def _unstack_pair(o, pp):
    lo = _lane_lo((BLOCK, 128))
    return jnp.where(lo, o[256 * pp:256 * pp + 128], o[256 * pp + 128:256 * pp + 256])


def _sink_col(sinks_ref, g):
    r = lax.broadcasted_iota(jnp.int32, (4 * BLOCK, 1), 0) // BLOCK
    col = jnp.full((4 * BLOCK, 1), sinks_ref[4 * g + 3], F32)
    for hh in range(3):
        col = jnp.where(r == hh, sinks_ref[4 * g + hh], col)
    return col


def _upper():
    r = lax.broadcasted_iota(jnp.int32, (4 * BLOCK, BLOCK), 0) % BLOCK
    return lax.broadcasted_iota(jnp.int32, (4 * BLOCK, BLOCK), 1) > r


def _fold(t):
    return jnp.where(_upper(), t[:, 0:BLOCK], t[:, BLOCK:])


def _unfold(t):
    zero = jnp.zeros_like(t)
    return jnp.concatenate([jnp.where(_upper(), t, zero), jnp.where(_upper(), zero, t)], axis=1)


def _softmax(s, sink_col, has_prev):
    if has_prev is not True:
        s = jnp.concatenate([jnp.where(has_prev, s[:, 0:BLOCK], NEG), s[:, BLOCK:]], axis=1)
    f = _fold(s)
    m = jnp.maximum(jnp.max(f, axis=-1, keepdims=True), sink_col)
    p = jnp.exp(f - m)
    es = jnp.exp(sink_col - m)
    inv = 1.0 / (jnp.sum(p, axis=-1, keepdims=True) + es)
    return (p * inv).astype(BF16), es * inv


def _key_windows(i, nsub, kd_ref, vd_ref, kdp_ref, vdp_ref):
    out = []
    for sb in range(nsub):
        rows = slice(BLOCK * sb, BLOCK * (sb + 1))
        if sb == 0:
            kk = jnp.concatenate([kdp_ref[...], kd_ref[rows, :]], axis=0)
            vv = jnp.concatenate([vdp_ref[...], vd_ref[rows, :]], axis=0)
            out.append((rows, kk, vv, i > 0))
        else:
            both = slice(BLOCK * (sb - 1), BLOCK * (sb + 1))
            out.append((rows, kd_ref[both, :], vd_ref[both, :], True))
    return out


def _gather_w_in(w_in_t):
    hi = W_IN_BLK // 2
    qr = hi // 2

    def body(wi_hbm, wi_all, f_mine, f_other, b_mine, b_other, send_sems, recv_sems, local_sems):
        x, y, c = lax.axis_index("x"), lax.axis_index("y"), lax.axis_index("c")
        me, sibling = (x, y), (x, y, 1 - c)
        xnb, ynb, diag = (1 - x, y), (x, 1 - y), (1 - x, 1 - y)

        loads = [pltpu.make_async_copy(wi_hbm.at[pl.ds(c * hi, hi)], f_mine, local_sems.at[0]),
                 pltpu.make_async_copy(wi_hbm.at[pl.ds((1 - c) * hi, hi)], f_other, local_sems.at[1])]
        for cp in loads:
            cp.start()

        def copy(k, chip, half, quarter, to, src=None):
            r = wi_all.at[2 * chip[0] + chip[1], pl.ds(half * hi + quarter * qr, qr)]
            return pltpu.make_async_remote_copy(src_ref=r if src is None else src, dst_ref=r, send_sem=send_sems.at[k],
                                                recv_sem=recv_sems.at[k], device_id=to, device_id_type=MESH)

        plan = [(0, xnb, 0), (1, ynb, 1), (2, xnb, 1), (3, ynb, 0)]
        loads[0].wait()
        b_mine[...] = f_mine[...].astype(BF16)
        sent = [copy(k, me, c, quarter, (*nb, c), src=b_mine.at[pl.ds(quarter * qr, qr)]) for k, nb, quarter in plan]
        for cp in sent:
            cp.start()
        loads[1].wait()
        b_other[...] = f_other[...].astype(BF16)
        own_slot = wi_all.at[2 * x + y]
        keeps = [pltpu.make_async_copy(b_mine, own_slot.at[pl.ds(c * hi, hi)], local_sems.at[2]),
                 pltpu.make_async_copy(b_other, own_slot.at[pl.ds((1 - c) * hi, hi)], local_sems.at[3])]
        for cp in keeps:
            cp.start()
        arrivals = [(0, xnb, 0), (1, ynb, 1), (2, xnb, 1), (3, ynb, 0), (4, diag, 0), (5, diag, 1)]
        relay = {0: (4, ynb), 1: (5, xnb)}
        for k, chip, quarter in arrivals:
            copy(k, chip, c, quarter, (x, y, c)).wait_recv()
            if k in relay:
                sent.append(copy(relay[k][0], chip, c, quarter, (*relay[k][1], c)))
                sent[-1].start()
            sent.append(copy(6 + k, chip, c, quarter, sibling))
            sent[-1].start()
        for k, chip, quarter in arrivals:
            copy(6 + k, chip, 1 - c, quarter, (x, y, c)).wait_recv()
        for cp in sent:
            cp.wait_send()
        for cp in keeps:
            cp.wait()

    hbm = pl.BlockSpec(memory_space=pl.ANY)
    return pl.pallas_call(
        body, name="gather_w_in",
        out_shape=jax.ShapeDtypeStruct((N_CHIPS, W_IN_BLK, D_MODEL), BF16),
        in_specs=[hbm], out_specs=hbm,
        scratch_shapes=[pltpu.VMEM((hi, D_MODEL), F32), pltpu.VMEM((hi, D_MODEL), F32),
                        pltpu.VMEM((hi, D_MODEL), BF16), pltpu.VMEM((hi, D_MODEL), BF16),
                        pltpu.SemaphoreType.DMA((12,)), pltpu.SemaphoreType.DMA((12,)), pltpu.SemaphoreType.DMA((4,))],
        compiler_params=_params(),
    )(w_in_t)


def _reduce_grads(g_in, *smalls):
    hi = W_IN_BLK // 2
    qr = hi // 2
    q0, q1 = slice(0, qr), slice(qr, hi)

    def body(gi_hbm, s0_ref, s1_ref, gi_out, small_out,
             mine_i, sib_i, out_i, ici_i, small_in, small_ref, send_sems, recv_sems, local_sems):
        x, y, c = lax.axis_index("x"), lax.axis_index("y"), lax.axis_index("c")
        my_dev = 4 * x + 2 * y + c
        sibling = (x, y, 1 - c)
        xnb, ynb = (1 - x, y, c), (x, 1 - y, c)
        order = [(1 - x, 1 - y), (1 - x, y), (x, 1 - y), (x, y)]

        def remote(k, src, dst, to):
            return pltpu.make_async_remote_copy(src_ref=src, dst_ref=dst, send_sem=send_sems.at[k],
                                                recv_sem=recv_sems.at[k], device_id=to, device_id_type=MESH)

        small_ref[...] = s0_ref[...] + s1_ref[...]
        small_cps = []
        for f in range(1, 8):
            fx, fy, fc = f >> 2, (f >> 1) & 1, f & 1
            small_cps.append(remote(11 + f - 1, small_ref, small_in.at[f - 1], (x ^ fx, y ^ fy, c ^ fc)))
        for cp in small_cps:
            cp.start()

        own, to_sib = [], []
        for n, chip in enumerate(order):
            j = 2 * chip[0] + chip[1]
            own.append(pltpu.make_async_copy(gi_hbm.at[j, pl.ds(c * hi, hi)], mine_i.at[n], local_sems.at[n]))
            to_sib.append(remote(6 + n, gi_hbm.at[j, pl.ds((1 - c) * hi, hi)], sib_i.at[n], sibling))
            own[-1].start()
            to_sib[-1].start()

        def pair_sum(n):
            own[n].wait()
            to_sib[n].wait_recv()
            return mine_i[n] + sib_i[n]

        ici = [remote(k, out_i.at[k], ici_i.at[k], xnb if k % 2 == 0 else ynb) for k in range(6)]

        def send(k, rows_f32):
            out_i[k] = rows_f32.astype(BF16)
            ici[k].start()

        p_diag = pair_sum(0)
        send(0, p_diag[q0])
        send(1, p_diag[q1])
        p_x = pair_sum(1)
        send(2, p_x[q0])
        p_y = pair_sum(2)
        send(3, p_y[q1])
        ici[0].wait_recv()
        send(5, p_y[q0] + ici_i[0].astype(F32))
        ici[1].wait_recv()
        send(4, p_x[q1] + ici_i[1].astype(F32))
        p_mine = pair_sum(3)
        swap, keep = [], []
        for quarter, (rows, a, b) in enumerate(((q0, 2, 5), (q1, 4, 3))):
            ici[a].wait_recv()
            ici[b].wait_recv()
            mine_i[3, rows, :] = p_mine[rows] + ici_i[a].astype(F32) + ici_i[b].astype(F32)
            dst = gi_out.at[pl.ds(c * hi + quarter * qr, qr)]
            swap.append(remote(10 + 8 * quarter, mine_i.at[3, rows], dst, sibling))
            keep.append(pltpu.make_async_copy(mine_i.at[3, rows], dst, local_sems.at[4 + quarter]))
            swap[-1].start()
            keep[-1].start()

        for cp in small_cps:
            cp.wait_recv()
        total = jnp.zeros((SMALL_ROWS, D_MODEL), F32)
        for d in range(8):
            slot = jnp.maximum((d ^ my_dev) - 1, 0)
            total = total + jnp.where(d == my_dev, small_ref[...], small_in[slot])
        small_out[...] = total

        for quarter in range(2):
            theirs = gi_out.at[pl.ds((1 - c) * hi + quarter * qr, qr)]
            remote(10 + 8 * quarter, theirs, theirs, sibling).wait_recv()
        for cp in keep:
            cp.wait()
        for cp in to_sib + ici + swap + small_cps:
            cp.wait_send()

    vmem = pl.BlockSpec(memory_space=pltpu.VMEM)
    anyspace = pl.BlockSpec(memory_space=pl.ANY)
    return pl.pallas_call(
        body, name="reduce_grads",
        out_shape=(jax.ShapeDtypeStruct((W_IN_BLK, D_MODEL), F32), jax.ShapeDtypeStruct((SMALL_ROWS, D_MODEL), F32)),
        in_specs=[anyspace, vmem, vmem], out_specs=(anyspace, vmem),
        scratch_shapes=[pltpu.VMEM((N_CHIPS, hi, D_MODEL), F32), pltpu.VMEM((N_CHIPS, hi, D_MODEL), F32),
                        pltpu.VMEM((6, qr, D_MODEL), BF16), pltpu.VMEM((6, qr, D_MODEL), BF16),
                        pltpu.VMEM((7, SMALL_ROWS, D_MODEL), F32), pltpu.VMEM((SMALL_ROWS, D_MODEL), F32),
                        pltpu.SemaphoreType.DMA((19,)), pltpu.SemaphoreType.DMA((19,)),
                        pltpu.SemaphoreType.DMA((6,))],
        compiler_params=_params(),
    )(g_in, *smalls)


def _fwd_proj(x, norm_g, w_in_t, w_out, conv_w8):
    seq = x.shape[0]
    nt = seq // T_PROJ
    lane = jnp.arange(128, dtype=jnp.int32) % HEAD_DIM
    inv_freq = ROPE_THETA ** (-(2 * (lane % 8)).astype(F32) / ROT_DIM)
    inv_freq = jnp.where(lane < ROT_DIM, inv_freq, 0.0).reshape(1, 128)
    in_tile = jnp.arange(T_PROJ, dtype=jnp.int32).astype(F32)[:, None] * inv_freq
    cos_in, sin_in = jnp.cos(in_tile), jnp.sin(in_tile)
    start = jnp.repeat((jnp.arange(nt, dtype=jnp.int32) * T_PROJ).astype(F32), 8)[:, None] * inv_freq
    cos_st, sin_st = jnp.cos(start), jnp.sin(start)

    def body(x_ref, g_ref, w_ref, cs_ref, ss_ref, ci_ref, si_ref, wo_ref, cw_ref,
             q_ref, kd_ref, vd_ref, rest_ref, c_ref, a_ref, b_ref, wo_all, cw_all,
             wo_stage, send_sems, recv_sems, local_sems):
        i = pl.program_id(0)
        mx, my, mc = lax.axis_index("x"), lax.axis_index("y"), lax.axis_index("c")
        chips = [(1 - mx, my), (mx, 1 - my), (1 - mx, 1 - my)]

        def gather(blocks):
            cps = []
            for k, chip in enumerate(chips):
                for n, (src, dst) in enumerate(((wo_stage, wo_all), (cw_ref, cw_all))):
                    cps.append(pltpu.make_async_remote_copy(
                        src_ref=src, dst_ref=dst.at[blocks[k]], send_sem=send_sems.at[2 * k + n],
                        recv_sem=recv_sems.at[2 * k + n], device_id=(*chip, mc), device_id_type=MESH))
            return cps

        me = 2 * mx + my
        own = [pltpu.make_async_copy(wo_stage, wo_all.at[me], local_sems.at[0]),
               pltpu.make_async_copy(cw_ref, cw_all.at[me], local_sems.at[1])]

        @pl.when(i == 0)
        def _():
            wo_stage[...] = wo_ref[...].astype(BF16)
            for cp in own + gather([me] * 3):
                cp.start()

        xf = x_ref[...]
        r1 = lax.rsqrt(jnp.mean(xf * xf, axis=-1, keepdims=True) + EPS)
        xn = (xf * r1 * g_ref[...]).astype(BF16)
        cs, ss = cs_ref[0:1, :], ss_ref[0:1, :]
        c = cs * ci_ref[...] - ss * si_ref[...]
        sin = ss * ci_ref[...] + cs * si_ref[...]
        j = lax.broadcasted_iota(jnp.int32, (T_PROJ, 128), 1) % HEAD_DIM
        a = jnp.where(j < 8, -sin, 0.0)
        b = jnp.where(j >= 8, sin, 0.0)
        c_ref[...], a_ref[...], b_ref[...] = c, a, b
        proj = lambda lo_c, w: lax.dot_general(xn, w_ref[lo_c:lo_c + w, :], _NT, preferred_element_type=F32)
        q_ref[...] = (_rope(proj(0, ATTN_W), c, a, b) * SCALE).astype(BF16)
        kv = proj(ATTN_W, 2 * KV_W)
        k = _rope(kv[:, 0:KV_W], c, a, b)
        v = kv[:, KV_W:2 * KV_W]
        lo = _lane_lo(k.shape)
        for t, ref in ((k, kd_ref), (v, vd_ref)):
            sw = pltpu.roll(t, HEAD_DIM, 1)
            ref[:, 0:128] = jnp.where(lo, t, sw).astype(BF16)
            ref[:, 128:256] = jnp.where(lo, sw, t).astype(BF16)
        for n in range(REST_W // 512):
            rest_ref[:, 512 * n:512 * (n + 1)] = proj(ATTN_W + 2 * KV_W + 512 * n, 512)

        @pl.when(i == nt - 1)
        def _():
            sent = gather([me] * 3)
            for cp in gather([2 * chip[0] + chip[1] for chip in chips]):
                cp.wait_recv()
            for cp in sent:
                cp.wait_send()
            for cp in own:
                cp.wait()

    tile = lambda w: pl.BlockSpec((T_PROJ, w), lambda i: (i, 0))
    whole = lambda r, w: pl.BlockSpec((r, w), lambda i: (0, 0))
    vmem = pl.BlockSpec(memory_space=pltpu.VMEM)
    hbm = pl.BlockSpec(memory_space=pl.ANY)
    return pl.pallas_call(
        body, name="fwd_proj", grid=(nt,),
        out_shape=(jax.ShapeDtypeStruct((seq, ATTN_W), BF16), jax.ShapeDtypeStruct((seq, 2 * KV_W), BF16),
                   jax.ShapeDtypeStruct((seq, 2 * KV_W), BF16), jax.ShapeDtypeStruct((seq, REST_W), F32))
        + (jax.ShapeDtypeStruct((seq, 128), F32),) * 3
        + (jax.ShapeDtypeStruct((N_CHIPS, W_OUT_BLK, D_MODEL), BF16), jax.ShapeDtypeStruct((N_CHIPS, 8, 128), F32)),
        in_specs=[tile(D_MODEL), whole(1, D_MODEL), whole(IN_W, D_MODEL), pl.BlockSpec((8, 128), lambda i: (i, 0)),
                  pl.BlockSpec((8, 128), lambda i: (i, 0)), whole(T_PROJ, 128), whole(T_PROJ, 128), vmem, vmem],
        out_specs=(tile(ATTN_W), tile(2 * KV_W), tile(2 * KV_W), tile(REST_W), tile(128), tile(128), tile(128),
                   hbm, hbm),
        scratch_shapes=[pltpu.VMEM((W_OUT_BLK, D_MODEL), BF16), pltpu.SemaphoreType.DMA((6,)),
                        pltpu.SemaphoreType.DMA((6,)), pltpu.SemaphoreType.DMA((2,))],
        compiler_params=_params(("arbitrary",)),
    )(x, norm_g, w_in_t, cos_st, sin_st, cos_in, sin_in, w_out, conv_w8)


CONV_SPEC = pl.BlockSpec((N_CHIPS, 8, 128), lambda i: (0, 0, 0))


def _conv_rows(cw_ref):
    return jnp.concatenate([cw_ref[j] for j in range(N_CHIPS)], axis=1)


def _conv_parts(rest_ref, prev_ref, cw_ref, first):
    u = rest_ref[:, 1024:1536] * rest_ref[:, 1536:2048]
    up = prev_ref[:, 1024:1536] * prev_ref[:, 1536:2048]
    up = jnp.where(first, jnp.zeros_like(up), up)
    um1 = _shift_down(u, up, 1)
    um2 = _shift_down(u, up, 2)
    cw = _conv_rows(cw_ref)
    cv = cw[0:1, :] * um2 + cw[1:2, :] * um1 + cw[2:3, :] * u
    return u, um1, um2, cv


def _fwd_mix(x, q, kd, vd, rest, sinks, conv_w, w_out, final_g, target):
    seq = x.shape[0]
    nt = seq // T_FMIX
    nsub = T_FMIX // BLOCK
    assert D_MODEL == 2 * T_FMIX

    def body(sinks_ref, x_ref, q_ref, kd_ref, vd_ref, kdp_ref, vdp_ref, rest_ref, restp_ref, cw_ref, wo_ref,
             fg_ref, tgt_ref, prob_ref, do_ref, prod_ref, dga_ref, db_ref, dgc_ref, dcv_ref, dh2_ref,
             gwo_hbm, gwob_hbm, small_ref,
             mix_ref, dh2b_ref, gwo_ref, gate_ref, dsink_ref, out_sems):
        i = pl.program_id(0)
        attn_ref, sinkw_ref = prod_ref, dcv_ref

        @pl.when(i == 0)
        def _():
            small_ref[...] = jnp.zeros_like(small_ref)
            dsink_ref[...] = jnp.zeros_like(dsink_ref)
            gwo_ref[...] = jnp.zeros_like(gwo_ref)

        chains = []
        for rows, kk, vv, has_prev in _key_windows(i, nsub, kd_ref, vd_ref, kdp_ref, vdp_ref):
            qt = q_ref[rows, :]
            for g in range(2):
                kg = kk[:, 128 * g:128 * (g + 1)]
                chains.append(dict(g=g, rows=rows, has_prev=has_prev, vg=vv[:, 128 * g:128 * (g + 1)],
                                   s=lax.dot_general(_stack_heads(qt, g), kg, _NT, preferred_element_type=F32)))
        for ch in chains:
            ch["prob"], ch["psink"] = _softmax(ch.pop("s"), _sink_col(sinks_ref, ch["g"]), ch["has_prev"])
        for k, ch in enumerate(chains):
            prob_ref[k] = ch["prob"]
            o = jnp.dot(_unfold(ch["prob"]), ch["vg"], preferred_element_type=F32)
            ow = o * ch["psink"]
            for pp in range(2):
                lanes = slice(256 * ch["g"] + 128 * pp, 256 * ch["g"] + 128 * (pp + 1))
                attn_ref[ch["rows"], lanes] = _unstack_pair(o, pp)
                sinkw_ref[ch["rows"], lanes] = _unstack_pair(ow, pp)

        def silu_parts(t, lo_c):
            sg = _sigmoid(t)
            silu = t * sg
            gate_ref[:, lo_c:lo_c + 512] = silu
            gate_ref[:, lo_c + 512:lo_c + 1024] = sg * (1.0 + t * (1.0 - sg))
            return silu

        mix_ref[:, 0:ATTN_W] = (attn_ref[...] * silu_parts(rest_ref[:, 0:512], 0)).astype(BF16)
        u, um1, um2, cv = _conv_parts(rest_ref, restp_ref, cw_ref, i == 0)
        mix_ref[:, ATTN_W:] = (rest_ref[:, 512:1024] * cv * silu_parts(rest_ref[:, 2048:2560], 1024)).astype(BF16)

        h2 = x_ref[...] + jnp.dot(mix_ref[...], wo_ref[...], preferred_element_type=F32)
        r2 = lax.rsqrt(jnp.mean(h2 * h2, axis=-1, keepdims=True) + EPS)
        n2 = h2 * r2
        err = n2 * fg_ref[...] - tgt_ref[...]
        dy = err * (1.0 / D_MODEL)
        small_ref[6:7, :] += jnp.sum(err * err, axis=0, keepdims=True) * (0.5 / D_MODEL)
        small_ref[1:2, :] += jnp.sum(dy * n2, axis=0, keepdims=True)
        dn = dy * fg_ref[...]
        dh2 = r2 * (dn - n2 * jnp.mean(dn * n2, axis=-1, keepdims=True))
        dh2_ref[...] = dh2
        dh2b_ref[...] = dh2.astype(BF16)

        d_mix = lambda lo_r: lax.dot_general(dh2b_ref[...], wo_ref[lo_r:lo_r + 512, :], _NT, preferred_element_type=F32)
        dma = d_mix(0)
        gwo_ref[...] += lax.dot_general(mix_ref[...], dh2b_ref[...], _TN, preferred_element_type=F32)
        dga_ref[...] = (dma * attn_ref[...] * gate_ref[:, 512:1024]).astype(BF16)
        d_attn = dma * gate_ref[:, 0:512]
        do_ref[...] = d_attn.astype(BF16)
        prod_ref[...] = d_attn * attn_ref[...]
        dsink_ref[0:1, :] += jnp.sum(d_attn * sinkw_ref[...], axis=0, keepdims=True)
        bg = rest_ref[:, 512:1024]
        dmc = d_mix(ATTN_W)
        t1 = dmc * gate_ref[:, 1024:1536]
        db_ref[...] = (t1 * cv).astype(BF16)
        dcv = t1 * bg
        dcv_ref[...] = dcv
        dgc_ref[...] = (dmc * (bg * cv) * gate_ref[:, 1536:2048]).astype(BF16)
        small_ref[2:3, 0:CONV_W] += jnp.sum(dcv * um2, axis=0, keepdims=True)
        small_ref[3:4, 0:CONV_W] += jnp.sum(dcv * um1, axis=0, keepdims=True)
        small_ref[4:5, 0:CONV_W] += jnp.sum(dcv * u, axis=0, keepdims=True)

        @pl.when(i == nt - 1)
        def _():
            head = lax.broadcasted_iota(jnp.int32, (1, ATTN_W), 1) // HEAD_DIM
            for h in range(8):
                tot = jnp.sum(jnp.where(head == h, dsink_ref[0:1, :], 0.0), axis=-1, keepdims=True)
                small_ref[5:6, h:h + 1] = -tot
            outs = [pltpu.make_async_copy(gwo_ref, gwo_hbm, out_sems.at[0])]
            outs[0].start()
            for n, stage in enumerate((mix_ref, dh2b_ref)):
                slab = slice(T_FMIX * n, T_FMIX * (n + 1))
                stage[...] = gwo_ref[slab, :].astype(BF16)
                outs.append(pltpu.make_async_copy(stage, gwob_hbm.at[slab], out_sems.at[1 + n]))
                outs[-1].start()
            for cp in outs:
                cp.wait()

    tile = lambda w: pl.BlockSpec((T_FMIX, w), lambda i: (i, 0))
    whole = lambda r, w: pl.BlockSpec((r, w), lambda i: (0, 0))
    prev_blk = pl.BlockSpec((BLOCK, 2 * KV_W), lambda i: (jnp.maximum(i * nsub - 1, 0), 0))
    prev8 = pl.BlockSpec((8, REST_W), lambda i: (jnp.maximum(i * (T_FMIX // 8) - 1, 0), 0))
    bf = lambda w: jax.ShapeDtypeStruct((seq, w), BF16)
    f32 = lambda w: jax.ShapeDtypeStruct((seq, w), F32)
    return pl.pallas_call(
        body, name="fwd_mix", grid=(nt,),
        out_shape=(jax.ShapeDtypeStruct((2 * seq // BLOCK, 4 * BLOCK, BLOCK), BF16),
                   bf(ATTN_W), f32(ATTN_W), bf(ATTN_W), bf(CONV_W), bf(CONV_W), f32(CONV_W), f32(D_MODEL),
                   jax.ShapeDtypeStruct((D_MODEL, D_MODEL), F32), jax.ShapeDtypeStruct((D_MODEL, D_MODEL), BF16),
                   jax.ShapeDtypeStruct((SMALL_ROWS, D_MODEL), F32)),
        in_specs=[pl.BlockSpec(memory_space=pltpu.SMEM), tile(D_MODEL), tile(ATTN_W), tile(2 * KV_W), tile(2 * KV_W),
                  prev_blk, prev_blk, tile(REST_W), prev8, CONV_SPEC,
                  pl.BlockSpec((D_MODEL, D_MODEL), lambda i: (0, 0), pipeline_mode=pl.Buffered(1)),
                  whole(1, D_MODEL), tile(D_MODEL)],
        out_specs=(pl.BlockSpec((2 * nsub, 4 * BLOCK, BLOCK), lambda i: (i, 0, 0)),
                   tile(ATTN_W), tile(ATTN_W), tile(ATTN_W), tile(CONV_W), tile(CONV_W), tile(CONV_W), tile(D_MODEL),
                   pl.BlockSpec(memory_space=pl.ANY), pl.BlockSpec(memory_space=pl.ANY), whole(SMALL_ROWS, D_MODEL)),
        scratch_shapes=[pltpu.VMEM((T_FMIX, D_MODEL), BF16)] * 2 + [
            pltpu.VMEM((D_MODEL, D_MODEL), F32), pltpu.VMEM((T_FMIX, 4 * 512), F32), pltpu.VMEM((8, ATTN_W), F32),
            pltpu.SemaphoreType.DMA((3,))],
        compiler_params=_params(("arbitrary",)),
    )(sinks, x, q, kd, vd, kd, vd, rest, rest, conv_w, w_out, final_g, target)


def _scatter_copies(g_hbm, gb_hbm, mine, land, send_sems, recv_sems, local_sem, half):
    x, y, c = lax.axis_index("x"), lax.axis_index("y"), lax.axis_index("c")
    cps = []
    for f in range(1, 8):
        to = (x ^ (f >> 2), y ^ ((f >> 1) & 1), c ^ (f & 1))
        src = gb_hbm.at[2 * to[0] + to[1], pl.ds(to[2] * half, half)]
        cps.append(pltpu.make_async_remote_copy(src_ref=src, dst_ref=land.at[f - 1], send_sem=send_sems.at[f - 1],
                                                recv_sem=recv_sems.at[f - 1], device_id=to, device_id_type=MESH))
    own = pltpu.make_async_copy(g_hbm.at[2 * x + y, pl.ds(c * half, half)], mine, local_sem)
    return cps, own


def _scatter_finish(cps, own, mine, land, out_hbm, send_sems, recv_sems, local_sem, half):
    x, y, c = lax.axis_index("x"), lax.axis_index("y"), lax.axis_index("c")
    own.wait()
    tot = mine[...]
    for f in range(1, 8):
        cps[f - 1].wait_recv()
        tot = tot + land[f - 1].astype(F32)
    mine[...] = tot

    def swap(rows_of):
        return pltpu.make_async_remote_copy(src_ref=mine, dst_ref=out_hbm.at[pl.ds(rows_of * half, half)],
                                            send_sem=send_sems.at[7], recv_sem=recv_sems.at[7],
                                            device_id=(x, y, 1 - c), device_id_type=MESH)

    keep = pltpu.make_async_copy(mine, out_hbm.at[pl.ds(c * half, half)], local_sem)
    keep.start()
    swap(c).start()
    swap(1 - c).wait_recv()
    keep.wait()
    for cp in cps:
        cp.wait_send()
    swap(c).wait_send()


def _bwd_mix(q, kd, vd, probs, d_attn, prod, rope_c, rope_a, rope_b, g_out, g_out_b):
    seq = q.shape[0]
    nt = seq // T_MIX
    nsub = T_MIX // BLOCK
    ho = W_OUT_BLK // 2

    def body(q_ref, kd_ref, vd_ref, kdp_ref, vdp_ref, prob_ref, do_ref, prod_ref, c_ref, a_ref, b_ref, go_hbm, gob_hbm,
             dq_ref, dk_ref, dv_ref, dkh_ref, dvh_ref, go_out,
             mine_o, land_o, send_sems, recv_sems, local_sems):
        i = pl.program_id(0)
        scatter = (mine_o, land_o, send_sems, recv_sems, local_sems.at[0], ho)

        @pl.when(i == 0)
        def _():
            cps, own = _scatter_copies(go_hbm, gob_hbm, *scatter)
            for cp in cps + [own]:
                cp.start()

        lo = _lane_lo((2 * BLOCK, 128))
        dk_blocks = [None] * (nsub + 1)
        dv_blocks = [None] * (nsub + 1)

        def add(lst, n, val):
            lst[n] = val if lst[n] is None else lst[n] + val

        chains = []
        for rows, kk, vv, _ in _key_windows(i, nsub, kd_ref, vd_ref, kdp_ref, vdp_ref):
            qt = q_ref[rows, :]
            dot = do_ref[rows, :]
            for g in range(2):
                rs = jnp.sum(_stack_heads(prod_ref[rows, :], g), axis=-1, keepdims=True)
                chains.append(dict(g=g, rows=rows, rs=rs, qs=_stack_heads(qt, g), dos=_stack_heads(dot, g),
                                   kg=kk[:, 128 * g:128 * (g + 1)], vg=vv[:, 128 * g:128 * (g + 1)]))
        for ch in chains:
            ch["dp"] = lax.dot_general(ch["dos"], ch["vg"], _NT, preferred_element_type=F32)
        for k, ch in enumerate(chains):
            ch["ds"] = _unfold((prob_ref[k].astype(F32) * (_fold(ch["dp"]) - ch["rs"])).astype(BF16))
        for k, ch in enumerate(chains):
            dqs = jnp.dot(ch["ds"], ch["kg"], preferred_element_type=F32) * SCALE
            c, a, b = c_ref[ch["rows"], :], a_ref[ch["rows"], :], b_ref[ch["rows"], :]
            for pp in range(2):
                lanes = slice(256 * ch["g"] + 128 * pp, 256 * ch["g"] + 128 * (pp + 1))
                dq_ref[ch["rows"], lanes] = _rope(_unstack_pair(dqs, pp), c, -a, -b).astype(BF16)
            dkd = lax.dot_general(ch["ds"], ch["qs"], _TN, preferred_element_type=F32)
            dvd = lax.dot_general(_unfold(prob_ref[k]), ch["dos"], _TN, preferred_element_type=F32)
            ch["dk"] = dkd + pltpu.roll(dkd, HEAD_DIM, 1)
            ch["dv"] = dvd + pltpu.roll(dvd, HEAD_DIM, 1)
        for sb in range(nsub):
            dk2 = jnp.where(lo, chains[2 * sb]["dk"], chains[2 * sb + 1]["dk"])
            dv2 = jnp.where(lo, chains[2 * sb]["dv"], chains[2 * sb + 1]["dv"])
            add(dk_blocks, sb, dk2[0:BLOCK])
            add(dk_blocks, sb + 1, dk2[BLOCK:])
            add(dv_blocks, sb, dv2[0:BLOCK])
            add(dv_blocks, sb + 1, dv2[BLOCK:])
        dkh_ref[0] = dk_blocks[0]
        dvh_ref[0] = dv_blocks[0]
        for sb in range(nsub):
            dk_ref[BLOCK * sb:BLOCK * (sb + 1), :] = dk_blocks[sb + 1]
            dv_ref[BLOCK * sb:BLOCK * (sb + 1), :] = dv_blocks[sb + 1]

        @pl.when(i == nt - 1)
        def _():
            cps, own = _scatter_copies(go_hbm, gob_hbm, *scatter)
            _scatter_finish(cps, own, mine_o, land_o, go_out, send_sems, recv_sems, local_sems.at[1], ho)

    tile = lambda w: pl.BlockSpec((T_MIX, w), lambda i: (i, 0))
    prev_blk = pl.BlockSpec((BLOCK, 2 * KV_W), lambda i: (jnp.maximum(i * nsub - 1, 0), 0))
    halo = pl.BlockSpec((1, BLOCK, KV_W), lambda i: (i, 0, 0))
    hbm = pl.BlockSpec(memory_space=pl.ANY)
    f32 = lambda w: jax.ShapeDtypeStruct((seq, w), F32)
    return pl.pallas_call(
        body, name="bwd_mix", grid=(nt,),
        out_shape=(jax.ShapeDtypeStruct((seq, ATTN_W), BF16), f32(KV_W), f32(KV_W),
                   jax.ShapeDtypeStruct((nt, BLOCK, KV_W), F32), jax.ShapeDtypeStruct((nt, BLOCK, KV_W), F32),
                   jax.ShapeDtypeStruct((W_OUT_BLK, D_MODEL), F32)),
        in_specs=[tile(ATTN_W), tile(2 * KV_W), tile(2 * KV_W), prev_blk, prev_blk,
                  pl.BlockSpec((2 * nsub, 4 * BLOCK, BLOCK), lambda i: (i, 0, 0)), tile(ATTN_W), tile(ATTN_W),
                  tile(128), tile(128), tile(128), hbm, hbm],
        out_specs=(tile(ATTN_W), tile(KV_W), tile(KV_W), halo, halo, hbm),
        scratch_shapes=[pltpu.VMEM((ho, D_MODEL), F32), pltpu.VMEM((7, ho, D_MODEL), BF16),
                        pltpu.SemaphoreType.DMA((8,)), pltpu.SemaphoreType.DMA((8,)), pltpu.SemaphoreType.DMA((2,))],
        compiler_params=_params(("arbitrary",)),
    )(q, kd, vd, kd, vd, probs, d_attn, prod, rope_c, rope_a, rope_b, g_out, g_out_b)


def _bwd_proj(x, norm_g, dh2, dq, dk, dv, dkh, dvh, dga, db, dgc, dcv, rest, conv_w, w_in_t, rope_c, rope_a, rope_b):
    seq = x.shape[0]
    tb = T_PROJ
    per = tb // T_MIX
    nt = seq // tb

    def body(x_ref, g_ref, dh2_ref, dq_ref, dk_ref, dv_ref, dkh_ref, dvh_ref, dkn_ref, dvn_ref, dga_ref, db_ref,
             dgc_ref, dcv_ref, dcvn_ref, ch_ref, cw_ref, w_ref, c_ref, a_ref, b_ref, gx_ref, gw_hbm, small_ref,
             dp_ref, acc_ref, out_sem):
        i = pl.program_id(0)

        @pl.when(i == 0)
        def _():
            small_ref[...] = jnp.zeros_like(small_ref)
            acc_ref[...] = jnp.zeros_like(acc_ref)

        last = i == nt - 1
        keep = jnp.where(last, 0.0, 1.0)
        pad = jnp.zeros((T_MIX - BLOCK, KV_W), F32)

        def with_halos(main_ref, halo_ref, next_ref):
            parts = []
            for m in range(1, per + 1):
                parts += [pad, halo_ref[m] if m < per else next_ref[0] * keep]
            return main_ref[...] + jnp.concatenate(parts, axis=0)

        dk = with_halos(dk_ref, dkh_ref, dkn_ref)
        dv = with_halos(dv_ref, dvh_ref, dvn_ref)
        dp_ref[:, 0:ATTN_W] = dq_ref[...]
        dp_ref[:, ATTN_W:ATTN_W + KV_W] = _rope(dk, c_ref[...], -a_ref[...], -b_ref[...]).astype(BF16)
        dp_ref[:, ATTN_W + KV_W:ATTN_W + 2 * KV_W] = dv.astype(BF16)
        base = ATTN_W + 2 * KV_W
        dp_ref[:, base:base + 512] = dga_ref[...]
        dp_ref[:, base + 512:base + 1024] = db_ref[...]
        dcv = dcv_ref[...]
        nxt = dcvn_ref[...] * keep
        cw = _conv_rows(cw_ref)
        du = cw[2:3, :] * dcv + cw[1:2, :] * _shift_up(dcv, nxt, 1) + cw[0:1, :] * _shift_up(dcv, nxt, 2)
        dp_ref[:, base + 1024:base + 1536] = (du * ch_ref[:, 512:1024]).astype(BF16)
        dp_ref[:, base + 1536:base + 2048] = (du * ch_ref[:, 0:512]).astype(BF16)
        dp_ref[:, base + 2048:base + 2560] = dgc_ref[...]

        xf = x_ref[...]
        r1 = lax.rsqrt(jnp.mean(xf * xf, axis=-1, keepdims=True) + EPS)
        n1 = xf * r1
        xn = (n1 * g_ref[...]).astype(BF16)
        for n in range(IN_W // 256):
            cols = slice(256 * n, 256 * (n + 1))
            acc_ref[cols, :] += lax.dot_general(dp_ref[:, cols], xn, _TN, preferred_element_type=F32)
        acc_out = pltpu.make_async_copy(acc_ref, gw_hbm, out_sem.at[0])

        @pl.when(last)
        def _():
            acc_out.start()

        dxn = jnp.dot(dp_ref[...], w_ref[...], preferred_element_type=F32)
        small_ref[0:1, :] += jnp.sum(dxn * n1, axis=0, keepdims=True)
        dxg = dxn * g_ref[...]
        gx_ref[...] = r1 * (dxg - n1 * jnp.mean(dxg * n1, axis=-1, keepdims=True)) + dh2_ref[...]

        @pl.when(last)
        def _():
            acc_out.wait()

    tile = lambda w: pl.BlockSpec((tb, w), lambda i: (i, 0))
    whole = lambda r, w: pl.BlockSpec((r, w), lambda i: (0, 0))
    halo = pl.BlockSpec((per, BLOCK, KV_W), lambda i: (i, 0, 0))
    halo_next = pl.BlockSpec((1, BLOCK, KV_W), lambda i: (jnp.minimum((i + 1) * per, seq // T_MIX - 1), 0, 0))
    next8 = pl.BlockSpec((8, CONV_W), lambda i: (jnp.minimum((i + 1) * (tb // 8), seq // 8 - 1), 0))
    ch = pl.BlockSpec((tb, 1024), lambda i: (i, 1))
    return pl.pallas_call(
        body, name="bwd_proj", grid=(nt,),
        out_shape=(jax.ShapeDtypeStruct((seq, D_MODEL), F32), jax.ShapeDtypeStruct((IN_W, D_MODEL), F32),
                   jax.ShapeDtypeStruct((SMALL_ROWS, D_MODEL), F32)),
        in_specs=[tile(D_MODEL), whole(1, D_MODEL), tile(D_MODEL), tile(ATTN_W), tile(KV_W), tile(KV_W), halo, halo,
                  halo_next, halo_next,
                  tile(ATTN_W), tile(CONV_W), tile(CONV_W), tile(CONV_W), next8, ch, CONV_SPEC,
                  pl.BlockSpec((IN_W, D_MODEL), lambda i: (0, 0), pipeline_mode=pl.Buffered(1)),
                  tile(128), tile(128), tile(128)],
        out_specs=(tile(D_MODEL), pl.BlockSpec(memory_space=pl.ANY), whole(SMALL_ROWS, D_MODEL)),
        scratch_shapes=[pltpu.VMEM((tb, IN_W), BF16), pltpu.VMEM((IN_W, D_MODEL), F32), pltpu.SemaphoreType.DMA((1,))],
        compiler_params=_params(("arbitrary",)),
    )(x, norm_g, dh2, dq, dk, dv, dkh, dvh, dkh, dvh, dga, db, dgc, dcv, dcv, rest, conv_w, w_in_t,
      rope_c, rope_a, rope_b)


def _adamw_step(w, g, m, v):
    m2 = ADAM_B1 * m + (1.0 - ADAM_B1) * g
    v2 = ADAM_B2 * v + (1.0 - ADAM_B2) * jnp.square(g)
    m_hat = m2 / (1.0 - ADAM_B1 ** ADAM_STEP)
    v_hat = v2 / (1.0 - ADAM_B2 ** ADAM_STEP)
    return -ADAM_LR * (m_hat / (jnp.sqrt(v_hat) + ADAM_EPS) + ADAM_WD * w), m2, v2


def _adamw_weights(groups):
    steps = 4

    def body(*refs):
        ins, outs = refs[:4 * len(groups)], refs[4 * len(groups):]
        for k in range(len(groups)):
            res = _adamw_step(*(r[...] for r in ins[4 * k:4 * k + 4]))
            for o_ref, val in zip(outs[3 * k:3 * k + 3], res):
                o_ref[...] = val

    in_specs, out_specs, out_shape = [], [], []
    for w, _, _, _ in groups:
        rows, cols = w.shape
        spec = pl.BlockSpec((rows // steps, cols), lambda i: (i, 0))
        in_specs += [spec] * 4
        out_specs += [spec] * 3
        out_shape += [jax.ShapeDtypeStruct((rows, cols), F32)] * 3
    flat = pl.pallas_call(
        body, name="adamw_weights", grid=(steps,), out_shape=tuple(out_shape), in_specs=in_specs,
        out_specs=tuple(out_specs), compiler_params=_params(("arbitrary",)),
    )(*[a for grp in groups for a in grp])
    return [flat[3 * k:3 * k + 3] for k in range(len(groups))]


def _adamw_small(chip, small, params, m, v):
    def body(chip_ref, small_ref, conv_ref, *refs):
        ins, outs = refs[:12], refs[12:]
        outs[0][...] = jnp.sum(small_ref[6:7, :], axis=-1, keepdims=True)
        grads = (small_ref[0:1, :], small_ref[1:2, :], conv_ref[2:5, :], small_ref[5:6, 0:8])
        for k, g in enumerate(grads):
            outs[1 + k][...] = g
            res = _adamw_step(ins[k][...], g, ins[4 + k][...], ins[8 + k][...])
            for n, val in enumerate(res):
                outs[5 + 4 * n + k][...] = val

    full = lambda a: pl.BlockSpec(a.shape, lambda i, c: (0,) * len(a.shape))
    shapes = [jax.ShapeDtypeStruct(p.shape, F32) for p in params]
    outs = [jax.ShapeDtypeStruct((1, 1), F32)] + shapes * 4
    flat = pl.pallas_call(
        body, name="adamw_small",
        grid_spec=pltpu.PrefetchScalarGridSpec(
            num_scalar_prefetch=1, grid=(1,),
            in_specs=[full(small), pl.BlockSpec((SMALL_ROWS, 128), lambda i, c: (0, c[0]))]
            + [full(a) for a in (*params, *m, *v)],
            out_specs=tuple(full(s) for s in outs)),
        out_shape=tuple(outs), compiler_params=_params(("arbitrary",)),
    )(chip, small, small, *params, *m, *v)
    return flat[0], flat[1:5], [flat[5 + 4 * n:9 + 4 * n] for n in range(3)]


def kernel(x, norm_g, w_in, sinks, conv_w, w_out, final_g, loss_target, m_norm_g, m_w_in, m_sinks, m_conv_w, m_w_out, m_final_g, v_norm_g, v_w_in, v_sinks, v_conv_w, v_w_out, v_final_g):
    seq = x.shape[1]
    x2 = x.reshape(seq, D_MODEL)
    tgt = loss_target.reshape(seq, D_MODEL)
    ng = norm_g.reshape(1, D_MODEL)
    fg = final_g.reshape(1, D_MODEL)
    chip = 2 * lax.axis_index("x") + lax.axis_index("y")

    conv_w8 = jnp.zeros((8, 128), F32).at[0:3].set(conv_w)
    w_in_full = _gather_w_in(w_in.T).reshape(IN_W, D_MODEL)

    q, kd, vd, rest, rope_c, rope_a, rope_b, wo_all, cw_all = _fwd_proj(x2, ng, w_in_full, w_out, conv_w8)
    w_out_full = wo_all.reshape(D_MODEL, D_MODEL)
    probs, d_attn, prod, dga, db, dgc, dcv, dh2, g_wo, g_wo_b, small_m = _fwd_mix(
        x2, q, kd, vd, rest, sinks, cw_all, w_out_full, fg, tgt)
    out_blocks = lambda t: t.reshape(N_CHIPS, W_OUT_BLK, D_MODEL)
    dq, dk, dv, dkh, dvh, grad_w_out = _bwd_mix(
        q, kd, vd, probs, d_attn, prod, rope_c, rope_a, rope_b, out_blocks(g_wo), out_blocks(g_wo_b))
    grad_x, g_wi, small_p = _bwd_proj(x2, ng, dh2, dq, dk, dv, dkh, dvh, dga, db, dgc, dcv, rest, cw_all,
                                      w_in_full, rope_c, rope_a, rope_b)

    g_in_blocks = g_wi.reshape(N_CHIPS, W_IN_BLK, D_MODEL)
    grad_w_in_t, small = _reduce_grads(g_in_blocks, small_m, small_p)

    (upd_wi, upd_wo) = _adamw_weights([(w_in.T, grad_w_in_t, m_w_in.T, v_w_in.T),
                                       (w_out, grad_w_out, m_w_out, v_w_out)])
    row = lambda t: t.reshape(1, -1)
    loss, grads_s, upd_s = _adamw_small(
        chip.reshape(1), small, (ng, fg, conv_w, row(sinks)),
        (row(m_norm_g), row(m_final_g), m_conv_w, row(m_sinks)),
        (row(v_norm_g), row(v_final_g), v_conv_w, row(v_sinks)))

    def named(ng_, fg_, cw_, sk_, wi_t, wo_):
        return [ng_.reshape(D_MODEL), wi_t.T, sk_.reshape(8), cw_, wo_, fg_.reshape(D_MODEL)]

    g_named = named(*grads_s, grad_w_in_t, grad_w_out)
    out = [loss.reshape(()), grad_x.reshape(1, seq, D_MODEL)] + g_named
    for n in range(3):
        out += named(*upd_s[n], upd_wi[n], upd_wo[n])
    return tuple(out)
```

```python
import jax
import jax.numpy as jnp
from jax import lax
from jax.experimental import pallas as pl
from jax.experimental.pallas import tpu as pltpu

F32 = jnp.float32
BF16 = jnp.bfloat16

D_MODEL = 1024
HEAD_DIM = 64
ATTN_W = 512
KV_W = 128
CONV_W = 512
IN_W = 3328
REST_W = IN_W - ATTN_W - 2 * KV_W
BLOCK = 128
ROT_DIM = 16
ROPE_THETA = 500000.0
EPS = 1e-5
SCALE = 0.125
NEG = -1e30

N_CHIPS = 4
W_IN_BLK = IN_W // N_CHIPS
W_OUT_BLK = D_MODEL // N_CHIPS

ADAM_LR = 0.001
ADAM_B1 = 0.9
ADAM_B2 = 0.999
ADAM_EPS = 1e-08
ADAM_WD = 0.01
ADAM_STEP = 10

VMEM_LIMIT = 60 * 1024 * 1024
T_PROJ = 512
T_FMIX = 512
T_MIX = 512
SMALL_ROWS = 8
MESH = pl.DeviceIdType.MESH

_NT = (((1,), (1,)), ((), ()))
_TN = (((0,), (0,)), ((), ()))


def _params(sem=None):
    kw = dict(vmem_limit_bytes=VMEM_LIMIT)
    if sem is not None:
        kw["dimension_semantics"] = sem
    return pltpu.CompilerParams(**kw)


def _sigmoid(t):
    return 1.0 / (1.0 + jnp.exp(-t))


def _shift_down(t, prev8, k):
    rolled = pltpu.roll(t, k, 0)
    row = lax.broadcasted_iota(jnp.int32, t.shape, 0)
    for j in range(k):
        rolled = jnp.where(row == j, prev8[8 - k + j:8 - k + j + 1, :], rolled)
    return rolled


def _shift_up(t, next8, k):
    n = t.shape[0]
    rolled = pltpu.roll(t, n - k, 0)
    row = lax.broadcasted_iota(jnp.int32, t.shape, 0)
    for j in range(k):
        rolled = jnp.where(row == n - k + j, next8[j:j + 1, :], rolled)
    return rolled


def _rope(t, c, a, b):
    w = t.shape[1]
    reps = w // 128
    if reps > 1:
        c, a, b = (jnp.concatenate([z] * reps, axis=1) for z in (c, a, b))
    return t * c + pltpu.roll(t, w - 8, 1) * a + pltpu.roll(t, 8, 1) * b


def _lane_lo(shape):
    return lax.broadcasted_iota(jnp.int32, shape, 1) < HEAD_DIM


def _stack_heads(t, g):
    lo = _lane_lo((BLOCK, 128))
    parts = []
    for hh in range(4):
        pair = t[:, 256 * g + 128 * (hh // 2):256 * g + 128 * (hh // 2) + 128]
        keep = lo if hh % 2 == 0 else jnp.logical_not(lo)
        parts.append(jnp.where(keep, pair, jnp.zeros_like(pair)))
    return jnp.concatenate(parts, axis=0)


def _unstack_pair(o, pp):
    lo = _lane_lo((BLOCK, 128))
    return jnp.where(lo, o[256 * pp:256 * pp + 128], o[256 * pp + 128:256 * pp + 256])


def _sink_col(sinks_ref, g):
    r = lax.broadcasted_iota(jnp.int32, (4 * BLOCK, 1), 0) // BLOCK
    col = jnp.full((4 * BLOCK, 1), sinks_ref[4 * g + 3], F32)
    for hh in range(3):
        col = jnp.where(r == hh, sinks_ref[4 * g + hh], col)
    return col


def _upper():
    r = lax.broadcasted_iota(jnp.int32, (4 * BLOCK, BLOCK), 0) % BLOCK
    return lax.broadcasted_iota(jnp.int32, (4 * BLOCK, BLOCK), 1) > r


def _fold(t):
    return jnp.where(_upper(), t[:, 0:BLOCK], t[:, BLOCK:])


def _unfold(t):
    zero = jnp.zeros_like(t)
    return jnp.concatenate([jnp.where(_upper(), t, zero), jnp.where(_upper(), zero, t)], axis=1)


def _softmax(s, sink_col, has_prev):
    if has_prev is not True:
        s = jnp.concatenate([jnp.where(has_prev, s[:, 0:BLOCK], NEG), s[:, BLOCK:]], axis=1)
    f = _fold(s)
    m = jnp.maximum(jnp.max(f, axis=-1, keepdims=True), sink_col)
    p = jnp.exp(f - m)
    es = jnp.exp(sink_col - m)
    inv = 1.0 / (jnp.sum(p, axis=-1, keepdims=True) + es)
    return (p * inv).astype(BF16), es * inv


def _key_windows(i, nsub, kd_ref, vd_ref, kdp_ref, vdp_ref):
    out = []
    for sb in range(nsub):
        rows = slice(BLOCK * sb, BLOCK * (sb + 1))
        if sb == 0:
            kk = jnp.concatenate([kdp_ref[...], kd_ref[rows, :]], axis=0)
            vv = jnp.concatenate([vdp_ref[...], vd_ref[rows, :]], axis=0)
            out.append((rows, kk, vv, i > 0))
        else:
            both = slice(BLOCK * (sb - 1), BLOCK * (sb + 1))
            out.append((rows, kd_ref[both, :], vd_ref[both, :], True))
    return out


def _gather_w_in(w_in_t):
    hi = W_IN_BLK // 2
    qr = hi // 2

    def body(wi_hbm, wi_all, f_mine, f_other, b_mine, b_other, send_sems, recv_sems, local_sems):
        x, y, c = lax.axis_index("x"), lax.axis_index("y"), lax.axis_index("c")
        me, sibling = (x, y), (x, y, 1 - c)
        xnb, ynb, diag = (1 - x, y), (x, 1 - y), (1 - x, 1 - y)

        loads = [pltpu.make_async_copy(wi_hbm.at[pl.ds(c * hi, hi)], f_mine, local_sems.at[0]),
                 pltpu.make_async_copy(wi_hbm.at[pl.ds((1 - c) * hi, hi)], f_other, local_sems.at[1])]
        for cp in loads:
            cp.start()

        def copy(k, chip, half, quarter, to, src=None):
            r = wi_all.at[2 * chip[0] + chip[1], pl.ds(half * hi + quarter * qr, qr)]
            return pltpu.make_async_remote_copy(src_ref=r if src is None else src, dst_ref=r, send_sem=send_sems.at[k],
                                                recv_sem=recv_sems.at[k], device_id=to, device_id_type=MESH)

        plan = [(0, xnb, 0), (1, ynb, 1), (2, xnb, 1), (3, ynb, 0)]
        loads[0].wait()
        b_mine[...] = f_mine[...].astype(BF16)
        sent = [copy(k, me, c, quarter, (*nb, c), src=b_mine.at[pl.ds(quarter * qr, qr)]) for k, nb, quarter in plan]
        for cp in sent:
            cp.start()
        loads[1].wait()
        b_other[...] = f_other[...].astype(BF16)
        own_slot = wi_all.at[2 * x + y]
        keeps = [pltpu.make_async_copy(b_mine, own_slot.at[pl.ds(c * hi, hi)], local_sems.at[2]),
                 pltpu.make_async_copy(b_other, own_slot.at[pl.ds((1 - c) * hi, hi)], local_sems.at[3])]
        for cp in keeps:
            cp.start()
        arrivals = [(0, xnb, 0), (1, ynb, 1), (2, xnb, 1), (3, ynb, 0), (4, diag, 0), (5, diag, 1)]
        relay = {0: (4, ynb), 1: (5, xnb)}
        for k, chip, quarter in arrivals:
            copy(k, chip, c, quarter, (x, y, c)).wait_recv()
            if k in relay:
                sent.append(copy(relay[k][0], chip, c, quarter, (*relay[k][1], c)))
                sent[-1].start()
            sent.append(copy(6 + k, chip, c, quarter, sibling))
            sent[-1].start()
        for k, chip, quarter in arrivals:
            copy(6 + k, chip, 1 - c, quarter, (x, y, c)).wait_recv()
        for cp in sent:
            cp.wait_send()
        for cp in keeps:
            cp.wait()

    hbm = pl.BlockSpec(memory_space=pl.ANY)
    return pl.pallas_call(
        body, name="gather_w_in",
        out_shape=jax.ShapeDtypeStruct((N_CHIPS, W_IN_BLK, D_MODEL), BF16),
        in_specs=[hbm], out_specs=hbm,
        scratch_shapes=[pltpu.VMEM((hi, D_MODEL), F32), pltpu.VMEM((hi, D_MODEL), F32),
                        pltpu.VMEM((hi, D_MODEL), BF16), pltpu.VMEM((hi, D_MODEL), BF16),
                        pltpu.SemaphoreType.DMA((12,)), pltpu.SemaphoreType.DMA((12,)), pltpu.SemaphoreType.DMA((4,))],
        compiler_params=_params(),
    )(w_in_t)


def _reduce_grads(g_in, *smalls):
    hi = W_IN_BLK // 2
    qr = hi // 2
    q0, q1 = slice(0, qr), slice(qr, hi)

    def body(gi_hbm, s0_ref, s1_ref, gi_out, small_out,
             mine_i, sib_i, out_i, ici_i, small_in, small_ref, send_sems, recv_sems, local_sems):
        x, y, c = lax.axis_index("x"), lax.axis_index("y"), lax.axis_index("c")
        my_dev = 4 * x + 2 * y + c
        sibling = (x, y, 1 - c)
        xnb, ynb = (1 - x, y, c), (x, 1 - y, c)
        order = [(1 - x, 1 - y), (1 - x, y), (x, 1 - y), (x, y)]

        def remote(k, src, dst, to):
            return pltpu.make_async_remote_copy(src_ref=src, dst_ref=dst, send_sem=send_sems.at[k],
                                                recv_sem=recv_sems.at[k], device_id=to, device_id_type=MESH)

        small_ref[...] = s0_ref[...] + s1_ref[...]
        small_cps = []
        for f in range(1, 8):
            fx, fy, fc = f >> 2, (f >> 1) & 1, f & 1
            small_cps.append(remote(11 + f - 1, small_ref, small_in.at[f - 1], (x ^ fx, y ^ fy, c ^ fc)))
        for cp in small_cps:
            cp.start()

        own, to_sib = [], []
        for n, chip in enumerate(order):
            j = 2 * chip[0] + chip[1]
            own.append(pltpu.make_async_copy(gi_hbm.at[j, pl.ds(c * hi, hi)], mine_i.at[n], local_sems.at[n]))
            to_sib.append(remote(6 + n, gi_hbm.at[j, pl.ds((1 - c) * hi, hi)], sib_i.at[n], sibling))
            own[-1].start()
            to_sib[-1].start()

        def pair_sum(n):
            own[n].wait()
            to_sib[n].wait_recv()
            return mine_i[n] + sib_i[n]

        ici = [remote(k, out_i.at[k], ici_i.at[k], xnb if k % 2 == 0 else ynb) for k in range(6)]

        def send(k, rows_f32):
            out_i[k] = rows_f32.astype(BF16)
            ici[k].start()

        p_diag = pair_sum(0)
        send(0, p_diag[q0])
        send(1, p_diag[q1])
        p_x = pair_sum(1)
        send(2, p_x[q0])
        p_y = pair_sum(2)
        send(3, p_y[q1])
        ici[0].wait_recv()
        send(5, p_y[q0] + ici_i[0].astype(F32))
        ici[1].wait_recv()
        send(4, p_x[q1] + ici_i[1].astype(F32))
        p_mine = pair_sum(3)
        swap, keep = [], []
        for quarter, (rows, a, b) in enumerate(((q0, 2, 5), (q1, 4, 3))):
            ici[a].wait_recv()
            ici[b].wait_recv()
            mine_i[3, rows, :] = p_mine[rows] + ici_i[a].astype(F32) + ici_i[b].astype(F32)
            dst = gi_out.at[pl.ds(c * hi + quarter * qr, qr)]
            swap.append(remote(10 + 8 * quarter, mine_i.at[3, rows], dst, sibling))
            keep.append(pltpu.make_async_copy(mine_i.at[3, rows], dst, local_sems.at[4 + quarter]))
            swap[-1].start()
            keep[-1].start()

        for cp in small_cps:
            cp.wait_recv()
        total = jnp.zeros((SMALL_ROWS, D_MODEL), F32)
        for d in range(8):
            slot = jnp.maximum((d ^ my_dev) - 1, 0)
            total = total + jnp.where(d == my_dev, small_ref[...], small_in[slot])
        small_out[...] = total

        for quarter in range(2):
            theirs = gi_out.at[pl.ds((1 - c) * hi + quarter * qr, qr)]
            remote(10 + 8 * quarter, theirs, theirs, sibling).wait_recv()
        for cp in keep:
            cp.wait()
        for cp in to_sib + ici + swap + small_cps:
            cp.wait_send()

    vmem = pl.BlockSpec(memory_space=pltpu.VMEM)
    anyspace = pl.BlockSpec(memory_space=pl.ANY)
    return pl.pallas_call(
        body, name="reduce_grads",
        out_shape=(jax.ShapeDtypeStruct((W_IN_BLK, D_MODEL), F32), jax.ShapeDtypeStruct((SMALL_ROWS, D_MODEL), F32)),
        in_specs=[anyspace, vmem, vmem], out_specs=(anyspace, vmem),
        scratch_shapes=[pltpu.VMEM((N_CHIPS, hi, D_MODEL), F32), pltpu.VMEM((N_CHIPS, hi, D_MODEL), F32),
                        pltpu.VMEM((6, qr, D_MODEL), BF16), pltpu.VMEM((6, qr, D_MODEL), BF16),
                        pltpu.VMEM((7, SMALL_ROWS, D_MODEL), F32), pltpu.VMEM((SMALL_ROWS, D_MODEL), F32),
                        pltpu.SemaphoreType.DMA((19,)), pltpu.SemaphoreType.DMA((19,)),
                        pltpu.SemaphoreType.DMA((6,))],
        compiler_params=_params(),
    )(g_in, *smalls)


def _fwd_proj(x, norm_g, w_in_t, w_out, conv_w8):
    seq = x.shape[0]
    nt = seq // T_PROJ
    lane = jnp.arange(128, dtype=jnp.int32) % HEAD_DIM
    inv_freq = ROPE_THETA ** (-(2 * (lane % 8)).astype(F32) / ROT_DIM)
    inv_freq = jnp.where(lane < ROT_DIM, inv_freq, 0.0).reshape(1, 128)
    in_tile = jnp.arange(T_PROJ, dtype=jnp.int32).astype(F32)[:, None] * inv_freq
    cos_in, sin_in = jnp.cos(in_tile), jnp.sin(in_tile)
    start = jnp.repeat((jnp.arange(nt, dtype=jnp.int32) * T_PROJ).astype(F32), 8)[:, None] * inv_freq
    cos_st, sin_st = jnp.cos(start), jnp.sin(start)

    def body(x_ref, g_ref, w_ref, cs_ref, ss_ref, ci_ref, si_ref, wo_ref, cw_ref,
             q_ref, kd_ref, vd_ref, rest_ref, c_ref, a_ref, b_ref, wo_all, cw_all,
             wo_stage, send_sems, recv_sems, local_sems):
        i = pl.program_id(0)
        mx, my, mc = lax.axis_index("x"), lax.axis_index("y"), lax.axis_index("c")
        chips = [(1 - mx, my), (mx, 1 - my), (1 - mx, 1 - my)]

        def gather(blocks):
            cps = []
            for k, chip in enumerate(chips):
                for n, (src, dst) in enumerate(((wo_stage, wo_all), (cw_ref, cw_all))):
                    cps.append(pltpu.make_async_remote_copy(
                        src_ref=src, dst_ref=dst.at[blocks[k]], send_sem=send_sems.at[2 * k + n],
                        recv_sem=recv_sems.at[2 * k + n], device_id=(*chip, mc), device_id_type=MESH))
            return cps

        me = 2 * mx + my
        own = [pltpu.make_async_copy(wo_stage, wo_all.at[me], local_sems.at[0]),
               pltpu.make_async_copy(cw_ref, cw_all.at[me], local_sems.at[1])]

        @pl.when(i == 0)
        def _():
            wo_stage[...] = wo_ref[...].astype(BF16)
            for cp in own + gather([me] * 3):
                cp.start()

        xf = x_ref[...]
        r1 = lax.rsqrt(jnp.mean(xf * xf, axis=-1, keepdims=True) + EPS)
        xn = (xf * r1 * g_ref[...]).astype(BF16)
        cs, ss = cs_ref[0:1, :], ss_ref[0:1, :]
        c = cs * ci_ref[...] - ss * si_ref[...]
        sin = ss * ci_ref[...] + cs * si_ref[...]
        j = lax.broadcasted_iota(jnp.int32, (T_PROJ, 128), 1) % HEAD_DIM
        a = jnp.where(j < 8, -sin, 0.0)
        b = jnp.where(j >= 8, sin, 0.0)
        c_ref[...], a_ref[...], b_ref[...] = c, a, b
        proj = lambda lo_c, w: lax.dot_general(xn, w_ref[lo_c:lo_c + w, :], _NT, preferred_element_type=F32)
        q_ref[...] = (_rope(proj(0, ATTN_W), c, a, b) * SCALE).astype(BF16)
        kv = proj(ATTN_W, 2 * KV_W)
        k = _rope(kv[:, 0:KV_W], c, a, b)
        v = kv[:, KV_W:2 * KV_W]
        lo = _lane_lo(k.shape)
        for t, ref in ((k, kd_ref), (v, vd_ref)):
            sw = pltpu.roll(t, HEAD_DIM, 1)
            ref[:, 0:128] = jnp.where(lo, t, sw).astype(BF16)
            ref[:, 128:256] = jnp.where(lo, sw, t).astype(BF16)
        for n in range(REST_W // 512):
            rest_ref[:, 512 * n:512 * (n + 1)] = proj(ATTN_W + 2 * KV_W + 512 * n, 512)

        @pl.when(i == nt - 1)
        def _():
            sent = gather([me] * 3)
            for cp in gather([2 * chip[0] + chip[1] for chip in chips]):
                cp.wait_recv()
            for cp in sent:
                cp.wait_send()
            for cp in own:
                cp.wait()

    tile = lambda w: pl.BlockSpec((T_PROJ, w), lambda i: (i, 0))
    whole = lambda r, w: pl.BlockSpec((r, w), lambda i: (0, 0))
    vmem = pl.BlockSpec(memory_space=pltpu.VMEM)
    hbm = pl.BlockSpec(memory_space=pl.ANY)
    return pl.pallas_call(
        body, name="fwd_proj", grid=(nt,),
        out_shape=(jax.ShapeDtypeStruct((seq, ATTN_W), BF16), jax.ShapeDtypeStruct((seq, 2 * KV_W), BF16),
                   jax.ShapeDtypeStruct((seq, 2 * KV_W), BF16), jax.ShapeDtypeStruct((seq, REST_W), F32))
        + (jax.ShapeDtypeStruct((seq, 128), F32),) * 3
        + (jax.ShapeDtypeStruct((N_CHIPS, W_OUT_BLK, D_MODEL), BF16), jax.ShapeDtypeStruct((N_CHIPS, 8, 128), F32)),
        in_specs=[tile(D_MODEL), whole(1, D_MODEL), whole(IN_W, D_MODEL), pl.BlockSpec((8, 128), lambda i: (i, 0)),
                  pl.BlockSpec((8, 128), lambda i: (i, 0)), whole(T_PROJ, 128), whole(T_PROJ, 128), vmem, vmem],
        out_specs=(tile(ATTN_W), tile(2 * KV_W), tile(2 * KV_W), tile(REST_W), tile(128), tile(128), tile(128),
                   hbm, hbm),
        scratch_shapes=[pltpu.VMEM((W_OUT_BLK, D_MODEL), BF16), pltpu.SemaphoreType.DMA((6,)),
                        pltpu.SemaphoreType.DMA((6,)), pltpu.SemaphoreType.DMA((2,))],
        compiler_params=_params(("arbitrary",)),
    )(x, norm_g, w_in_t, cos_st, sin_st, cos_in, sin_in, w_out, conv_w8)


CONV_SPEC = pl.BlockSpec((N_CHIPS, 8, 128), lambda i: (0, 0, 0))


def _conv_rows(cw_ref):
    return jnp.concatenate([cw_ref[j] for j in range(N_CHIPS)], axis=1)


def _conv_parts(rest_ref, prev_ref, cw_ref, first):
    u = rest_ref[:, 1024:1536] * rest_ref[:, 1536:2048]
    up = prev_ref[:, 1024:1536] * prev_ref[:, 1536:2048]
    up = jnp.where(first, jnp.zeros_like(up), up)
    um1 = _shift_down(u, up, 1)
    um2 = _shift_down(u, up, 2)
    cw = _conv_rows(cw_ref)
    cv = cw[0:1, :] * um2 + cw[1:2, :] * um1 + cw[2:3, :] * u
    return u, um1, um2, cv


def _fwd_mix(x, q, kd, vd, rest, sinks, conv_w, w_out, final_g, target):
    seq = x.shape[0]
    nt = seq // T_FMIX
    nsub = T_FMIX // BLOCK
    assert D_MODEL == 2 * T_FMIX

    def body(sinks_ref, x_ref, q_ref, kd_ref, vd_ref, kdp_ref, vdp_ref, rest_ref, restp_ref, cw_ref, wo_ref,
             fg_ref, tgt_ref, prob_ref, do_ref, prod_ref, dga_ref, db_ref, dgc_ref, dcv_ref, dh2_ref,
             gwo_hbm, gwob_hbm, small_ref,
             mix_ref, dh2b_ref, gwo_ref, gate_ref, dsink_ref, out_sems):
        i = pl.program_id(0)
        attn_ref, sinkw_ref = prod_ref, dcv_ref

        @pl.when(i == 0)
        def _():
            small_ref[...] = jnp.zeros_like(small_ref)
            dsink_ref[...] = jnp.zeros_like(dsink_ref)
            gwo_ref[...] = jnp.zeros_like(gwo_ref)

        chains = []
        for rows, kk, vv, has_prev in _key_windows(i, nsub, kd_ref, vd_ref, kdp_ref, vdp_ref):
            qt = q_ref[rows, :]
            for g in range(2):
                kg = kk[:, 128 * g:128 * (g + 1)]
                chains.append(dict(g=g, rows=rows, has_prev=has_prev, vg=vv[:, 128 * g:128 * (g + 1)],
                                   s=lax.dot_general(_stack_heads(qt, g), kg, _NT, preferred_element_type=F32)))
        for ch in chains:
            ch["prob"], ch["psink"] = _softmax(ch.pop("s"), _sink_col(sinks_ref, ch["g"]), ch["has_prev"])
        for k, ch in enumerate(chains):
            prob_ref[k] = ch["prob"]
            o = jnp.dot(_unfold(ch["prob"]), ch["vg"], preferred_element_type=F32)
            ow = o * ch["psink"]
            for pp in range(2):
                lanes = slice(256 * ch["g"] + 128 * pp, 256 * ch["g"] + 128 * (pp + 1))
                attn_ref[ch["rows"], lanes] = _unstack_pair(o, pp)
                sinkw_ref[ch["rows"], lanes] = _unstack_pair(ow, pp)

        def silu_parts(t, lo_c):
            sg = _sigmoid(t)
            silu = t * sg
            gate_ref[:, lo_c:lo_c + 512] = silu
            gate_ref[:, lo_c + 512:lo_c + 1024] = sg * (1.0 + t * (1.0 - sg))
            return silu

        mix_ref[:, 0:ATTN_W] = (attn_ref[...] * silu_parts(rest_ref[:, 0:512], 0)).astype(BF16)
        u, um1, um2, cv = _conv_parts(rest_ref, restp_ref, cw_ref, i == 0)
        mix_ref[:, ATTN_W:] = (rest_ref[:, 512:1024] * cv * silu_parts(rest_ref[:, 2048:2560], 1024)).astype(BF16)

        h2 = x_ref[...] + jnp.dot(mix_ref[...], wo_ref[...], preferred_element_type=F32)
        r2 = lax.rsqrt(jnp.mean(h2 * h2, axis=-1, keepdims=True) + EPS)
        n2 = h2 * r2
        err = n2 * fg_ref[...] - tgt_ref[...]
        dy = err * (1.0 / D_MODEL)
        small_ref[6:7, :] += jnp.sum(err * err, axis=0, keepdims=True) * (0.5 / D_MODEL)
        small_ref[1:2, :] += jnp.sum(dy * n2, axis=0, keepdims=True)
        dn = dy * fg_ref[...]
        dh2 = r2 * (dn - n2 * jnp.mean(dn * n2, axis=-1, keepdims=True))
        dh2_ref[...] = dh2
        dh2b_ref[...] = dh2.astype(BF16)

        d_mix = lambda lo_r: lax.dot_general(dh2b_ref[...], wo_ref[lo_r:lo_r + 512, :], _NT, preferred_element_type=F32)
        dma = d_mix(0)
        dga_ref[...] = (dma * attn_ref[...] * gate_ref[:, 512:1024]).astype(BF16)
        d_attn = dma * gate_ref[:, 0:512]
        do_ref[...] = d_attn.astype(BF16)
        prod_ref[...] = d_attn * attn_ref[...]
        dsink_ref[0:1, :] += jnp.sum(d_attn * sinkw_ref[...], axis=0, keepdims=True)
        bg = rest_ref[:, 512:1024]
        dmc = d_mix(ATTN_W)
        t1 = dmc * gate_ref[:, 1024:1536]
        db_ref[...] = (t1 * cv).astype(BF16)
        dcv = t1 * bg
        dcv_ref[...] = dcv
        dgc_ref[...] = (dmc * (bg * cv) * gate_ref[:, 1536:2048]).astype(BF16)
        small_ref[2:3, 0:CONV_W] += jnp.sum(dcv * um2, axis=0, keepdims=True)
        small_ref[3:4, 0:CONV_W] += jnp.sum(dcv * um1, axis=0, keepdims=True)
        small_ref[4:5, 0:CONV_W] += jnp.sum(dcv * u, axis=0, keepdims=True)
        gwo_ref[...] += lax.dot_general(mix_ref[...], dh2b_ref[...], _TN, preferred_element_type=F32)

        @pl.when(i == nt - 1)
        def _():
            head = lax.broadcasted_iota(jnp.int32, (1, ATTN_W), 1) // HEAD_DIM
            for h in range(8):
                tot = jnp.sum(jnp.where(head == h, dsink_ref[0:1, :], 0.0), axis=-1, keepdims=True)
                small_ref[5:6, h:h + 1] = -tot
            outs = [pltpu.make_async_copy(gwo_ref, gwo_hbm, out_sems.at[0])]
            outs[0].start()
            for n, stage in enumerate((mix_ref, dh2b_ref)):
                slab = slice(T_FMIX * n, T_FMIX * (n + 1))
                stage[...] = gwo_ref[slab, :].astype(BF16)
                outs.append(pltpu.make_async_copy(stage, gwob_hbm.at[slab], out_sems.at[1 + n]))
                outs[-1].start()
            for cp in outs:
                cp.wait()

    tile = lambda w: pl.BlockSpec((T_FMIX, w), lambda i: (i, 0))
    whole = lambda r, w: pl.BlockSpec((r, w), lambda i: (0, 0))
    prev_blk = pl.BlockSpec((BLOCK, 2 * KV_W), lambda i: (jnp.maximum(i * nsub - 1, 0), 0))
    prev8 = pl.BlockSpec((8, REST_W), lambda i: (jnp.maximum(i * (T_FMIX // 8) - 1, 0), 0))
    bf = lambda w: jax.ShapeDtypeStruct((seq, w), BF16)
    f32 = lambda w: jax.ShapeDtypeStruct((seq, w), F32)
    return pl.pallas_call(
        body, name="fwd_mix", grid=(nt,),
        out_shape=(jax.ShapeDtypeStruct((2 * seq // BLOCK, 4 * BLOCK, BLOCK), BF16),
                   bf(ATTN_W), f32(ATTN_W), bf(ATTN_W), bf(CONV_W), bf(CONV_W), f32(CONV_W), f32(D_MODEL),
                   jax.ShapeDtypeStruct((D_MODEL, D_MODEL), F32), jax.ShapeDtypeStruct((D_MODEL, D_MODEL), BF16),
                   jax.ShapeDtypeStruct((SMALL_ROWS, D_MODEL), F32)),
        in_specs=[pl.BlockSpec(memory_space=pltpu.SMEM), tile(D_MODEL), tile(ATTN_W), tile(2 * KV_W), tile(2 * KV_W),
                  prev_blk, prev_blk, tile(REST_W), prev8, CONV_SPEC,
                  pl.BlockSpec((D_MODEL, D_MODEL), lambda i: (0, 0), pipeline_mode=pl.Buffered(1)),
                  whole(1, D_MODEL), tile(D_MODEL)],
        out_specs=(pl.BlockSpec((2 * nsub, 4 * BLOCK, BLOCK), lambda i: (i, 0, 0)),
                   tile(ATTN_W), tile(ATTN_W), tile(ATTN_W), tile(CONV_W), tile(CONV_W), tile(CONV_W), tile(D_MODEL),
                   pl.BlockSpec(memory_space=pl.ANY), pl.BlockSpec(memory_space=pl.ANY), whole(SMALL_ROWS, D_MODEL)),
        scratch_shapes=[pltpu.VMEM((T_FMIX, D_MODEL), BF16)] * 2 + [
            pltpu.VMEM((D_MODEL, D_MODEL), F32), pltpu.VMEM((T_FMIX, 4 * 512), F32), pltpu.VMEM((8, ATTN_W), F32),
            pltpu.SemaphoreType.DMA((3,))],
        compiler_params=_params(("arbitrary",)),
    )(sinks, x, q, kd, vd, kd, vd, rest, rest, conv_w, w_out, final_g, target)


def _scatter_copies(g_hbm, gb_hbm, mine, land, send_sems, recv_sems, local_sem, half):
    x, y, c = lax.axis_index("x"), lax.axis_index("y"), lax.axis_index("c")
    cps = []
    for f in range(1, 8):
        to = (x ^ (f >> 2), y ^ ((f >> 1) & 1), c ^ (f & 1))
        src = gb_hbm.at[2 * to[0] + to[1], pl.ds(to[2] * half, half)]
        cps.append(pltpu.make_async_remote_copy(src_ref=src, dst_ref=land.at[f - 1], send_sem=send_sems.at[f - 1],
                                                recv_sem=recv_sems.at[f - 1], device_id=to, device_id_type=MESH))
    own = pltpu.make_async_copy(g_hbm.at[2 * x + y, pl.ds(c * half, half)], mine, local_sem)
    return cps, own


def _scatter_finish(cps, own, mine, land, out_hbm, send_sems, recv_sems, local_sem, half):
    x, y, c = lax.axis_index("x"), lax.axis_index("y"), lax.axis_index("c")
    own.wait()
    tot = mine[...]
    for f in range(1, 8):
        cps[f - 1].wait_recv()
        tot = tot + land[f - 1].astype(F32)
    mine[...] = tot

    def swap(rows_of):
        return pltpu.make_async_remote_copy(src_ref=mine, dst_ref=out_hbm.at[pl.ds(rows_of * half, half)],
                                            send_sem=send_sems.at[7], recv_sem=recv_sems.at[7],
                                            device_id=(x, y, 1 - c), device_id_type=MESH)

    keep = pltpu.make_async_copy(mine, out_hbm.at[pl.ds(c * half, half)], local_sem)
    keep.start()
    swap(c).start()
    swap(1 - c).wait_recv()
    keep.wait()
    for cp in cps:
        cp.wait_send()
    swap(c).wait_send()


def _bwd_mix(q, kd, vd, probs, d_attn, prod, rope_c, rope_a, rope_b, g_out, g_out_b):
    seq = q.shape[0]
    nt = seq // T_MIX
    nsub = T_MIX // BLOCK
    ho = W_OUT_BLK // 2

    def body(q_ref, kd_ref, vd_ref, kdp_ref, vdp_ref, prob_ref, do_ref, prod_ref, c_ref, a_ref, b_ref, go_hbm, gob_hbm,
             dq_ref, dk_ref, dv_ref, dkh_ref, dvh_ref, go_out,
             mine_o, land_o, send_sems, recv_sems, local_sems):
        i = pl.program_id(0)
        scatter = (mine_o, land_o, send_sems, recv_sems, local_sems.at[0], ho)

        @pl.when(i == 0)
        def _():
            cps, own = _scatter_copies(go_hbm, gob_hbm, *scatter)
            for cp in cps + [own]:
                cp.start()

        lo = _lane_lo((2 * BLOCK, 128))
        dk_blocks = [None] * (nsub + 1)
        dv_blocks = [None] * (nsub + 1)

        def add(lst, n, val):
            lst[n] = val if lst[n] is None else lst[n] + val

        chains = []
        for rows, kk, vv, _ in _key_windows(i, nsub, kd_ref, vd_ref, kdp_ref, vdp_ref):
            qt = q_ref[rows, :]
            dot = do_ref[rows, :]
            for g in range(2):
                rs = jnp.sum(_stack_heads(prod_ref[rows, :], g), axis=-1, keepdims=True)
                chains.append(dict(g=g, rows=rows, rs=rs, qs=_stack_heads(qt, g), dos=_stack_heads(dot, g),
                                   kg=kk[:, 128 * g:128 * (g + 1)], vg=vv[:, 128 * g:128 * (g + 1)]))
        for k, ch in enumerate(chains):
            ch["dp"] = lax.dot_general(ch["dos"], ch["vg"], _NT, preferred_element_type=F32)
            dvd = lax.dot_general(_unfold(prob_ref[k]), ch["dos"], _TN, preferred_element_type=F32)
            ch["dv"] = dvd + pltpu.roll(dvd, HEAD_DIM, 1)
        for k, ch in enumerate(chains):
            ch["ds"] = _unfold((prob_ref[k].astype(F32) * (_fold(ch["dp"]) - ch["rs"])).astype(BF16))
        for k, ch in enumerate(chains):
            dqs = jnp.dot(ch["ds"], ch["kg"], preferred_element_type=F32) * SCALE
            c, a, b = c_ref[ch["rows"], :], a_ref[ch["rows"], :], b_ref[ch["rows"], :]
            for pp in range(2):
                lanes = slice(256 * ch["g"] + 128 * pp, 256 * ch["g"] + 128 * (pp + 1))
                dq_ref[ch["rows"], lanes] = _rope(_unstack_pair(dqs, pp), c, -a, -b).astype(BF16)
            dkd = lax.dot_general(ch["ds"], ch["qs"], _TN, preferred_element_type=F32)
            ch["dk"] = dkd + pltpu.roll(dkd, HEAD_DIM, 1)
        for sb in range(nsub):
            dk2 = jnp.where(lo, chains[2 * sb]["dk"], chains[2 * sb + 1]["dk"])
            dv2 = jnp.where(lo, chains[2 * sb]["dv"], chains[2 * sb + 1]["dv"])
            add(dk_blocks, sb, dk2[0:BLOCK])
            add(dk_blocks, sb + 1, dk2[BLOCK:])
            add(dv_blocks, sb, dv2[0:BLOCK])
            add(dv_blocks, sb + 1, dv2[BLOCK:])
        dkh_ref[0] = dk_blocks[0]
        dvh_ref[0] = dv_blocks[0]
        for sb in range(nsub):
            dk_ref[BLOCK * sb:BLOCK * (sb + 1), :] = dk_blocks[sb + 1]
            dv_ref[BLOCK * sb:BLOCK * (sb + 1), :] = dv_blocks[sb + 1]

        @pl.when(i == nt - 1)
        def _():
            cps, own = _scatter_copies(go_hbm, gob_hbm, *scatter)
            _scatter_finish(cps, own, mine_o, land_o, go_out, send_sems, recv_sems, local_sems.at[1], ho)

    tile = lambda w: pl.BlockSpec((T_MIX, w), lambda i: (i, 0))
    prev_blk = pl.BlockSpec((BLOCK, 2 * KV_W), lambda i: (jnp.maximum(i * nsub - 1, 0), 0))
    halo = pl.BlockSpec((1, BLOCK, KV_W), lambda i: (i, 0, 0))
    hbm = pl.BlockSpec(memory_space=pl.ANY)
    f32 = lambda w: jax.ShapeDtypeStruct((seq, w), F32)
    return pl.pallas_call(
        body, name="bwd_mix", grid=(nt,),
        out_shape=(jax.ShapeDtypeStruct((seq, ATTN_W), BF16), f32(KV_W), f32(KV_W),
                   jax.ShapeDtypeStruct((nt, BLOCK, KV_W), F32), jax.ShapeDtypeStruct((nt, BLOCK, KV_W), F32),
                   jax.ShapeDtypeStruct((W_OUT_BLK, D_MODEL), F32)),
        in_specs=[tile(ATTN_W), tile(2 * KV_W), tile(2 * KV_W), prev_blk, prev_blk,
                  pl.BlockSpec((2 * nsub, 4 * BLOCK, BLOCK), lambda i: (i, 0, 0)), tile(ATTN_W), tile(ATTN_W),
                  tile(128), tile(128), tile(128), hbm, hbm],
        out_specs=(tile(ATTN_W), tile(KV_W), tile(KV_W), halo, halo, hbm),
        scratch_shapes=[pltpu.VMEM((ho, D_MODEL), F32), pltpu.VMEM((7, ho, D_MODEL), BF16),
                        pltpu.SemaphoreType.DMA((8,)), pltpu.SemaphoreType.DMA((8,)), pltpu.SemaphoreType.DMA((2,))],
        compiler_params=_params(("arbitrary",)),
    )(q, kd, vd, kd, vd, probs, d_attn, prod, rope_c, rope_a, rope_b, g_out, g_out_b)


def _bwd_proj(x, norm_g, dh2, dq, dk, dv, dkh, dvh, dga, db, dgc, dcv, rest, conv_w, w_in_t, rope_c, rope_a, rope_b):
    seq = x.shape[0]
    tb = T_PROJ
    per = tb // T_MIX
    nt = seq // tb

    def body(x_ref, g_ref, dh2_ref, dq_ref, dk_ref, dv_ref, dkh_ref, dvh_ref, dkn_ref, dvn_ref, dga_ref, db_ref,
             dgc_ref, dcv_ref, dcvn_ref, ch_ref, cw_ref, w_ref, c_ref, a_ref, b_ref, gx_ref, gw_hbm, small_ref,
             dp_ref, acc_ref, out_sem):
        i = pl.program_id(0)

        @pl.when(i == 0)
        def _():
            small_ref[...] = jnp.zeros_like(small_ref)
            acc_ref[...] = jnp.zeros_like(acc_ref)

        last = i == nt - 1
        keep = jnp.where(last, 0.0, 1.0)
        pad = jnp.zeros((T_MIX - BLOCK, KV_W), F32)

        def with_halos(main_ref, halo_ref, next_ref):
            parts = []
            for m in range(1, per + 1):
                parts += [pad, halo_ref[m] if m < per else next_ref[0] * keep]
            return main_ref[...] + jnp.concatenate(parts, axis=0)

        dk = with_halos(dk_ref, dkh_ref, dkn_ref)
        dv = with_halos(dv_ref, dvh_ref, dvn_ref)
        dp_ref[:, 0:ATTN_W] = dq_ref[...]
        dp_ref[:, ATTN_W:ATTN_W + KV_W] = _rope(dk, c_ref[...], -a_ref[...], -b_ref[...]).astype(BF16)
        dp_ref[:, ATTN_W + KV_W:ATTN_W + 2 * KV_W] = dv.astype(BF16)
        base = ATTN_W + 2 * KV_W
        dp_ref[:, base:base + 512] = dga_ref[...]
        dp_ref[:, base + 512:base + 1024] = db_ref[...]
        dcv = dcv_ref[...]
        nxt = dcvn_ref[...] * keep
        cw = _conv_rows(cw_ref)
        du = cw[2:3, :] * dcv + cw[1:2, :] * _shift_up(dcv, nxt, 1) + cw[0:1, :] * _shift_up(dcv, nxt, 2)
        dp_ref[:, base + 1024:base + 1536] = (du * ch_ref[:, 512:1024]).astype(BF16)
        dp_ref[:, base + 1536:base + 2048] = (du * ch_ref[:, 0:512]).astype(BF16)
        dp_ref[:, base + 2048:base + 2560] = dgc_ref[...]

        xf = x_ref[...]
        r1 = lax.rsqrt(jnp.mean(xf * xf, axis=-1, keepdims=True) + EPS)
        n1 = xf * r1
        xn = (n1 * g_ref[...]).astype(BF16)
        for n in range(IN_W // 256):
            cols = slice(256 * n, 256 * (n + 1))
            acc_ref[cols, :] += lax.dot_general(dp_ref[:, cols], xn, _TN, preferred_element_type=F32)
        acc_out = pltpu.make_async_copy(acc_ref, gw_hbm, out_sem.at[0])

        @pl.when(last)
        def _():
            acc_out.start()

        dxn = jnp.dot(dp_ref[...], w_ref[...], preferred_element_type=F32)
        small_ref[0:1, :] += jnp.sum(dxn * n1, axis=0, keepdims=True)
        dxg = dxn * g_ref[...]
        gx_ref[...] = r1 * (dxg - n1 * jnp.mean(dxg * n1, axis=-1, keepdims=True)) + dh2_ref[...]

        @pl.when(last)
        def _():
            acc_out.wait()

    tile = lambda w: pl.BlockSpec((tb, w), lambda i: (i, 0))
    whole = lambda r, w: pl.BlockSpec((r, w), lambda i: (0, 0))
    halo = pl.BlockSpec((per, BLOCK, KV_W), lambda i: (i, 0, 0))
    halo_next = pl.BlockSpec((1, BLOCK, KV_W), lambda i: (jnp.minimum((i + 1) * per, seq // T_MIX - 1), 0, 0))
    next8 = pl.BlockSpec((8, CONV_W), lambda i: (jnp.minimum((i + 1) * (tb // 8), seq // 8 - 1), 0))
    ch = pl.BlockSpec((tb, 1024), lambda i: (i, 1))
    return pl.pallas_call(
        body, name="bwd_proj", grid=(nt,),
        out_shape=(jax.ShapeDtypeStruct((seq, D_MODEL), F32), jax.ShapeDtypeStruct((IN_W, D_MODEL), F32),
                   jax.ShapeDtypeStruct((SMALL_ROWS, D_MODEL), F32)),
        in_specs=[tile(D_MODEL), whole(1, D_MODEL), tile(D_MODEL), tile(ATTN_W), tile(KV_W), tile(KV_W), halo, halo,
                  halo_next, halo_next,
                  tile(ATTN_W), tile(CONV_W), tile(CONV_W), tile(CONV_W), next8, ch, CONV_SPEC,
                  pl.BlockSpec((IN_W, D_MODEL), lambda i: (0, 0), pipeline_mode=pl.Buffered(1)),
                  tile(128), tile(128), tile(128)],
        out_specs=(tile(D_MODEL), pl.BlockSpec(memory_space=pl.ANY), whole(SMALL_ROWS, D_MODEL)),
        scratch_shapes=[pltpu.VMEM((tb, IN_W), BF16), pltpu.VMEM((IN_W, D_MODEL), F32), pltpu.SemaphoreType.DMA((1,))],
        compiler_params=_params(("arbitrary",)),
    )(x, norm_g, dh2, dq, dk, dv, dkh, dvh, dkh, dvh, dga, db, dgc, dcv, dcv, rest, conv_w, w_in_t,
      rope_c, rope_a, rope_b)


def _adamw_step(w, g, m, v):
    m2 = ADAM_B1 * m + (1.0 - ADAM_B1) * g
    v2 = ADAM_B2 * v + (1.0 - ADAM_B2) * jnp.square(g)
    m_hat = m2 / (1.0 - ADAM_B1 ** ADAM_STEP)
    v_hat = v2 / (1.0 - ADAM_B2 ** ADAM_STEP)
    return -ADAM_LR * (m_hat / (jnp.sqrt(v_hat) + ADAM_EPS) + ADAM_WD * w), m2, v2


def _adamw_weights(groups):
    steps = 4

    def body(*refs):
        ins, outs = refs[:4 * len(groups)], refs[4 * len(groups):]
        for k in range(len(groups)):
            res = _adamw_step(*(r[...] for r in ins[4 * k:4 * k + 4]))
            for o_ref, val in zip(outs[3 * k:3 * k + 3], res):
                o_ref[...] = val

    in_specs, out_specs, out_shape = [], [], []
    for w, _, _, _ in groups:
        rows, cols = w.shape
        spec = pl.BlockSpec((rows // steps, cols), lambda i: (i, 0))
        in_specs += [spec] * 4
        out_specs += [spec] * 3
        out_shape += [jax.ShapeDtypeStruct((rows, cols), F32)] * 3
    flat = pl.pallas_call(
        body, name="adamw_weights", grid=(steps,), out_shape=tuple(out_shape), in_specs=in_specs,
        out_specs=tuple(out_specs), compiler_params=_params(("arbitrary",)),
    )(*[a for grp in groups for a in grp])
    return [flat[3 * k:3 * k + 3] for k in range(len(groups))]


def _adamw_small(chip, small, params, m, v):
    def body(chip_ref, small_ref, conv_ref, *refs):
        ins, outs = refs[:12], refs[12:]
        outs[0][...] = jnp.sum(small_ref[6:7, :], axis=-1, keepdims=True)
        grads = (small_ref[0:1, :], small_ref[1:2, :], conv_ref[2:5, :], small_ref[5:6, 0:8])
        for k, g in enumerate(grads):
            outs[1 + k][...] = g
            res = _adamw_step(ins[k][...], g, ins[4 + k][...], ins[8 + k][...])
            for n, val in enumerate(res):
                outs[5 + 4 * n + k][...] = val

    full = lambda a: pl.BlockSpec(a.shape, lambda i, c: (0,) * len(a.shape))
    shapes = [jax.ShapeDtypeStruct(p.shape, F32) for p in params]
    outs = [jax.ShapeDtypeStruct((1, 1), F32)] + shapes * 4
    flat = pl.pallas_call(
        body, name="adamw_small",
        grid_spec=pltpu.PrefetchScalarGridSpec(
            num_scalar_prefetch=1, grid=(1,),
            in_specs=[full(small), pl.BlockSpec((SMALL_ROWS, 128), lambda i, c: (0, c[0]))]
            + [full(a) for a in (*params, *m, *v)],
            out_specs=tuple(full(s) for s in outs)),
        out_shape=tuple(outs), compiler_params=_params(("arbitrary",)),
    )(chip, small, small, *params, *m, *v)
    return flat[0], flat[1:5], [flat[5 + 4 * n:9 + 4 * n] for n in range(3)]


def kernel(x, norm_g, w_in, sinks, conv_w, w_out, final_g, loss_target, m_norm_g, m_w_in, m_sinks, m_conv_w, m_w_out, m_final_g, v_norm_g, v_w_in, v_sinks, v_conv_w, v_w_out, v_final_g):
    seq = x.shape[1]
    x2 = x.reshape(seq, D_MODEL)
    tgt = loss_target.reshape(seq, D_MODEL)
    ng = norm_g.reshape(1, D_MODEL)
    fg = final_g.reshape(1, D_MODEL)
    chip = 2 * lax.axis_index("x") + lax.axis_index("y")

    conv_w8 = jnp.zeros((8, 128), F32).at[0:3].set(conv_w)
    w_in_full = _gather_w_in(w_in.T).reshape(IN_W, D_MODEL)

    q, kd, vd, rest, rope_c, rope_a, rope_b, wo_all, cw_all = _fwd_proj(x2, ng, w_in_full, w_out, conv_w8)
    w_out_full = wo_all.reshape(D_MODEL, D_MODEL)
    probs, d_attn, prod, dga, db, dgc, dcv, dh2, g_wo, g_wo_b, small_m = _fwd_mix(
        x2, q, kd, vd, rest, sinks, cw_all, w_out_full, fg, tgt)
    out_blocks = lambda t: t.reshape(N_CHIPS, W_OUT_BLK, D_MODEL)
    dq, dk, dv, dkh, dvh, grad_w_out = _bwd_mix(
        q, kd, vd, probs, d_attn, prod, rope_c, rope_a, rope_b, out_blocks(g_wo), out_blocks(g_wo_b))
    grad_x, g_wi, small_p = _bwd_proj(x2, ng, dh2, dq, dk, dv, dkh, dvh, dga, db, dgc, dcv, rest, cw_all,
                                      w_in_full, rope_c, rope_a, rope_b)

    g_in_blocks = g_wi.reshape(N_CHIPS, W_IN_BLK, D_MODEL)
    grad_w_in_t, small = _reduce_grads(g_in_blocks, small_m, small_p)

    (upd_wi, upd_wo) = _adamw_weights([(w_in.T, grad_w_in_t, m_w_in.T, v_w_in.T),
                                       (w_out, grad_w_out, m_w_out, v_w_out)])
    row = lambda t: t.reshape(1, -1)
    loss, grads_s, upd_s = _adamw_small(
        chip.reshape(1), small, (ng, fg, conv_w, row(sinks)),
        (row(m_norm_g), row(m_final_g), m_conv_w, row(m_sinks)),
        (row(v_norm_g), row(v_final_g), v_conv_w, row(v_sinks)))

    def named(ng_, fg_, cw_, sk_, wi_t, wo_):
        return [ng_.reshape(D_MODEL), wi_t.T, sk_.reshape(8), cw_, wo_, fg_.reshape(D_MODEL)]

    g_named = named(*grads_s, grad_w_in_t, grad_w_out)
    out = [loss.reshape(()), grad_x.reshape(1, seq, D_MODEL)] + g_named
    for n in range(3):
        out += named(*upd_s[n], upd_wi[n], upd_wo[n])
    return tuple(out)
```

```python
import jax
import jax.numpy as jnp
from jax import lax
from jax.experimental import pallas as pl
from jax.experimental.pallas import tpu as pltpu

F32 = jnp.float32
BF16 = jnp.bfloat16

D_MODEL = 1024
HEAD_DIM = 64
ATTN_W = 512
KV_W = 128
CONV_W = 512
IN_W = 3328
REST_W = IN_W - ATTN_W - 2 * KV_W
BLOCK = 128
ROT_DIM = 16
ROPE_THETA = 500000.0
EPS = 1e-5
SCALE = 0.125
NEG = -1e30

N_CHIPS = 4
W_IN_BLK = IN_W // N_CHIPS
W_OUT_BLK = D_MODEL // N_CHIPS

ADAM_LR = 0.001
ADAM_B1 = 0.9
ADAM_B2 = 0.999
ADAM_EPS = 1e-08
ADAM_WD = 0.01
ADAM_STEP = 10

VMEM_LIMIT = 60 * 1024 * 1024
T_PROJ = 512
T_FMIX = 512
T_MIX = 512
SMALL_ROWS = 8
MESH = pl.DeviceIdType.MESH

_NT = (((1,), (1,)), ((), ()))
_TN = (((0,), (0,)), ((), ()))


def _params(sem=None):
    kw = dict(vmem_limit_bytes=VMEM_LIMIT)
    if sem is not None:
        kw["dimension_semantics"] = sem
    return pltpu.CompilerParams(**kw)


def _sigmoid(t):
    return 1.0 / (1.0 + jnp.exp(-t))


def _shift_down(t, prev8, k):
    rolled = pltpu.roll(t, k, 0)
    row = lax.broadcasted_iota(jnp.int32, t.shape, 0)
    for j in range(k):
        rolled = jnp.where(row == j, prev8[8 - k + j:8 - k + j + 1, :], rolled)
    return rolled


def _shift_up(t, next8, k):
    n = t.shape[0]
    rolled = pltpu.roll(t, n - k, 0)
    row = lax.broadcasted_iota(jnp.int32, t.shape, 0)
    for j in range(k):
        rolled = jnp.where(row == n - k + j, next8[j:j + 1, :], rolled)
    return rolled


def _rope(t, c, a, b):
    w = t.shape[1]
    reps = w // 128
    if reps > 1:
        c, a, b = (jnp.concatenate([z] * reps, axis=1) for z in (c, a, b))
    return t * c + pltpu.roll(t, w - 8, 1) * a + pltpu.roll(t, 8, 1) * b


def _lane_lo(shape):
    return lax.broadcasted_iota(jnp.int32, shape, 1) < HEAD_DIM


def _stack_heads(t, g):
    lo = _lane_lo((BLOCK, 128))
    parts = []
    for hh in range(4):
        pair = t[:, 256 * g + 128 * (hh // 2):256 * g + 128 * (hh // 2) + 128]
        keep = lo if hh % 2 == 0 else jnp.logical_not(lo)
        parts.append(jnp.where(keep, pair, jnp.zeros_like(pair)))
    return jnp.concatenate(parts, axis=0)


def _unstack_pair(o, pp):
    lo = _lane_lo((BLOCK, 128))
    return jnp.where(lo, o[256 * pp:256 * pp + 128], o[256 * pp + 128:256 * pp + 256])


def _sink_col(sinks_ref, g):
    r = lax.broadcasted_iota(jnp.int32, (4 * BLOCK, 1), 0) // BLOCK
    col = jnp.full((4 * BLOCK, 1), sinks_ref[4 * g + 3], F32)
    for hh in range(3):
        col = jnp.where(r == hh, sinks_ref[4 * g + hh], col)
    return col


def _upper():
    r = lax.broadcasted_iota(jnp.int32, (4 * BLOCK, BLOCK), 0) % BLOCK
    return lax.broadcasted_iota(jnp.int32, (4 * BLOCK, BLOCK), 1) > r


def _fold(t):
    return jnp.where(_upper(), t[:, 0:BLOCK], t[:, BLOCK:])


def _unfold(t):
    zero = jnp.zeros_like(t)
    return jnp.concatenate([jnp.where(_upper(), t, zero), jnp.where(_upper(), zero, t)], axis=1)


def _softmax(s, sink_col, has_prev):
    if has_prev is not True:
        s = jnp.concatenate([jnp.where(has_prev, s[:, 0:BLOCK], NEG), s[:, BLOCK:]], axis=1)
    f = _fold(s)
    m = jnp.maximum(jnp.max(f, axis=-1, keepdims=True), sink_col)
    p = jnp.exp(f - m)
    es = jnp.exp(sink_col - m)
    inv = 1.0 / (jnp.sum(p, axis=-1, keepdims=True) + es)
    return (p * inv).astype(BF16), es * inv


def _key_windows(i, nsub, kd_ref, vd_ref, kdp_ref, vdp_ref):
    out = []
    for sb in range(nsub):
        rows = slice(BLOCK * sb, BLOCK * (sb + 1))
        if sb == 0:
            kk = jnp.concatenate([kdp_ref[...], kd_ref[rows, :]], axis=0)
            vv = jnp.concatenate([vdp_ref[...], vd_ref[rows, :]], axis=0)
            out.append((rows, kk, vv, i > 0))
        else:
            both = slice(BLOCK * (sb - 1), BLOCK * (sb + 1))
            out.append((rows, kd_ref[both, :], vd_ref[both, :], True))
    return out


def _gather_w_in(w_in_t):
    hi = W_IN_BLK // 2
    qr = hi // 2

    def body(wi_hbm, wi_all, f_mine, f_other, b_mine, b_other, send_sems, recv_sems, local_sems):
        x, y, c = lax.axis_index("x"), lax.axis_index("y"), lax.axis_index("c")
        me, sibling = (x, y), (x, y, 1 - c)
        xnb, ynb, diag = (1 - x, y), (x, 1 - y), (1 - x, 1 - y)

        loads = [pltpu.make_async_copy(wi_hbm.at[pl.ds(c * hi, hi)], f_mine, local_sems.at[0]),
                 pltpu.make_async_copy(wi_hbm.at[pl.ds((1 - c) * hi, hi)], f_other, local_sems.at[1])]
        for cp in loads:
            cp.start()

        def copy(k, chip, half, quarter, to, src=None):
            r = wi_all.at[2 * chip[0] + chip[1], pl.ds(half * hi + quarter * qr, qr)]
            return pltpu.make_async_remote_copy(src_ref=r if src is None else src, dst_ref=r, send_sem=send_sems.at[k],
                                                recv_sem=recv_sems.at[k], device_id=to, device_id_type=MESH)

        plan = [(0, xnb, 0), (1, ynb, 1), (2, xnb, 1), (3, ynb, 0)]
        loads[0].wait()
        b_mine[...] = f_mine[...].astype(BF16)
        sent = [copy(k, me, c, quarter, (*nb, c), src=b_mine.at[pl.ds(quarter * qr, qr)]) for k, nb, quarter in plan]
        for cp in sent:
            cp.start()
        loads[1].wait()
        b_other[...] = f_other[...].astype(BF16)
        own_slot = wi_all.at[2 * x + y]
        keeps = [pltpu.make_async_copy(b_mine, own_slot.at[pl.ds(c * hi, hi)], local_sems.at[2]),
                 pltpu.make_async_copy(b_other, own_slot.at[pl.ds((1 - c) * hi, hi)], local_sems.at[3])]
        for cp in keeps:
            cp.start()
        arrivals = [(0, xnb, 0), (1, ynb, 1), (2, xnb, 1), (3, ynb, 0), (4, diag, 0), (5, diag, 1)]
        relay = {0: (4, ynb), 1: (5, xnb)}
        for k, chip, quarter in arrivals:
            copy(k, chip, c, quarter, (x, y, c)).wait_recv()
            if k in relay:
                sent.append(copy(relay[k][0], chip, c, quarter, (*relay[k][1], c)))
                sent[-1].start()
            sent.append(copy(6 + k, chip, c, quarter, sibling))
            sent[-1].start()
        for k, chip, quarter in arrivals:
            copy(6 + k, chip, 1 - c, quarter, (x, y, c)).wait_recv()
        for cp in sent:
            cp.wait_send()
        for cp in keeps:
            cp.wait()

    hbm = pl.BlockSpec(memory_space=pl.ANY)
    return pl.pallas_call(
        body, name="gather_w_in",
        out_shape=jax.ShapeDtypeStruct((N_CHIPS, W_IN_BLK, D_MODEL), BF16),
        in_specs=[hbm], out_specs=hbm,
        scratch_shapes=[pltpu.VMEM((hi, D_MODEL), F32), pltpu.VMEM((hi, D_MODEL), F32),
                        pltpu.VMEM((hi, D_MODEL), BF16), pltpu.VMEM((hi, D_MODEL), BF16),
                        pltpu.SemaphoreType.DMA((12,)), pltpu.SemaphoreType.DMA((12,)), pltpu.SemaphoreType.DMA((4,))],
        compiler_params=_params(),
    )(w_in_t)


def _reduce_grads(g_in, *smalls):
    hi = W_IN_BLK // 2
    qr = hi // 2
    q0, q1 = slice(0, qr), slice(qr, hi)

    def body(gi_hbm, s0_ref, s1_ref, gi_out, small_out,
             mine_i, sib_i, out_i, ici_i, small_in, small_ref, send_sems, recv_sems, local_sems):
        x, y, c = lax.axis_index("x"), lax.axis_index("y"), lax.axis_index("c")
        my_dev = 4 * x + 2 * y + c
        sibling = (x, y, 1 - c)
        xnb, ynb = (1 - x, y, c), (x, 1 - y, c)
        order = [(1 - x, 1 - y), (1 - x, y), (x, 1 - y), (x, y)]

        def remote(k, src, dst, to):
            return pltpu.make_async_remote_copy(src_ref=src, dst_ref=dst, send_sem=send_sems.at[k],
                                                recv_sem=recv_sems.at[k], device_id=to, device_id_type=MESH)

        small_ref[...] = s0_ref[...] + s1_ref[...]
        small_cps = []
        for f in range(1, 8):
            fx, fy, fc = f >> 2, (f >> 1) & 1, f & 1
            small_cps.append(remote(11 + f - 1, small_ref, small_in.at[f - 1], (x ^ fx, y ^ fy, c ^ fc)))
        for cp in small_cps:
            cp.start()

        own, to_sib = [], []
        for n, chip in enumerate(order):
            j = 2 * chip[0] + chip[1]
            own.append(pltpu.make_async_copy(gi_hbm.at[j, pl.ds(c * hi, hi)], mine_i.at[n], local_sems.at[n]))
            to_sib.append(remote(6 + n, gi_hbm.at[j, pl.ds((1 - c) * hi, hi)], sib_i.at[n], sibling))
            own[-1].start()
            to_sib[-1].start()

        def pair_sum(n):
            own[n].wait()
            to_sib[n].wait_recv()
            return mine_i[n] + sib_i[n]

        ici = [remote(k, out_i.at[k], ici_i.at[k], xnb if k % 2 == 0 else ynb) for k in range(6)]

        def send(k, rows_f32):
            out_i[k] = rows_f32.astype(BF16)
            ici[k].start()

        p_diag = pair_sum(0)
        send(0, p_diag[q0])
        send(1, p_diag[q1])
        p_x = pair_sum(1)
        send(2, p_x[q0])
        p_y = pair_sum(2)
        send(3, p_y[q1])
        ici[0].wait_recv()
        send(5, p_y[q0] + ici_i[0].astype(F32))
        ici[1].wait_recv()
        send(4, p_x[q1] + ici_i[1].astype(F32))
        p_mine = pair_sum(3)
        swap, keep = [], []
        for quarter, (rows, a, b) in enumerate(((q0, 2, 5), (q1, 4, 3))):
            ici[a].wait_recv()
            ici[b].wait_recv()
            mine_i[3, rows, :] = p_mine[rows] + ici_i[a].astype(F32) + ici_i[b].astype(F32)
            dst = gi_out.at[pl.ds(c * hi + quarter * qr, qr)]
            swap.append(remote(10 + 8 * quarter, mine_i.at[3, rows], dst, sibling))
            keep.append(pltpu.make_async_copy(mine_i.at[3, rows], dst, local_sems.at[4 + quarter]))
            swap[-1].start()
            keep[-1].start()

        for cp in small_cps:
            cp.wait_recv()
        total = jnp.zeros((SMALL_ROWS, D_MODEL), F32)
        for d in range(8):
            slot = jnp.maximum((d ^ my_dev) - 1, 0)
            total = total + jnp.where(d == my_dev, small_ref[...], small_in[slot])
        small_out[...] = total

        for quarter in range(2):
            theirs = gi_out.at[pl.ds((1 - c) * hi + quarter * qr, qr)]
            remote(10 + 8 * quarter, theirs, theirs, sibling).wait_recv()
        for cp in keep:
            cp.wait()
        for cp in to_sib + ici + swap + small_cps:
            cp.wait_send()

    vmem = pl.BlockSpec(memory_space=pltpu.VMEM)
    anyspace = pl.BlockSpec(memory_space=pl.ANY)
    return pl.pallas_call(
        body, name="reduce_grads",
        out_shape=(jax.ShapeDtypeStruct((W_IN_BLK, D_MODEL), F32), jax.ShapeDtypeStruct((SMALL_ROWS, D_MODEL), F32)),
        in_specs=[anyspace, vmem, vmem], out_specs=(anyspace, vmem),
        scratch_shapes=[pltpu.VMEM((N_CHIPS, hi, D_MODEL), F32), pltpu.VMEM((N_CHIPS, hi, D_MODEL), F32),
                        pltpu.VMEM((6, qr, D_MODEL), BF16), pltpu.VMEM((6, qr, D_MODEL), BF16),
                        pltpu.VMEM((7, SMALL_ROWS, D_MODEL), F32), pltpu.VMEM((SMALL_ROWS, D_MODEL), F32),
                        pltpu.SemaphoreType.DMA((19,)), pltpu.SemaphoreType.DMA((19,)),
                        pltpu.SemaphoreType.DMA((6,))],
        compiler_params=_params(),
    )(g_in, *smalls)


def _fwd_proj(x, norm_g, w_in_t, w_out, conv_w8):
    seq = x.shape[0]
    nt = seq // T_PROJ
    lane = jnp.arange(128, dtype=jnp.int32) % HEAD_DIM
    inv_freq = ROPE_THETA ** (-(2 * (lane % 8)).astype(F32) / ROT_DIM)
    inv_freq = jnp.where(lane < ROT_DIM, inv_freq, 0.0).reshape(1, 128)
    in_tile = jnp.arange(T_PROJ, dtype=jnp.int32).astype(F32)[:, None] * inv_freq
    cos_in, sin_in = jnp.cos(in_tile), jnp.sin(in_tile)
    start = jnp.repeat((jnp.arange(nt, dtype=jnp.int32) * T_PROJ).astype(F32), 8)[:, None] * inv_freq
    cos_st, sin_st = jnp.cos(start), jnp.sin(start)

    def body(x_ref, g_ref, w_ref, cs_ref, ss_ref, ci_ref, si_ref, wo_ref, cw_ref,
             q_ref, kd_ref, vd_ref, rest_ref, c_ref, a_ref, b_ref, wo_all, cw_all,
             wo_stage, send_sems, recv_sems, local_sems):
        i = pl.program_id(0)
        mx, my, mc = lax.axis_index("x"), lax.axis_index("y"), lax.axis_index("c")
        chips = [(1 - mx, my), (mx, 1 - my), (1 - mx, 1 - my)]

        def gather(blocks):
            cps = []
            for k, chip in enumerate(chips):
                for n, (src, dst) in enumerate(((wo_stage, wo_all), (cw_ref, cw_all))):
                    cps.append(pltpu.make_async_remote_copy(
                        src_ref=src, dst_ref=dst.at[blocks[k]], send_sem=send_sems.at[2 * k + n],
                        recv_sem=recv_sems.at[2 * k + n], device_id=(*chip, mc), device_id_type=MESH))
            return cps

        me = 2 * mx + my
        own = [pltpu.make_async_copy(wo_stage, wo_all.at[me], local_sems.at[0]),
               pltpu.make_async_copy(cw_ref, cw_all.at[me], local_sems.at[1])]

        @pl.when(i == 0)
        def _():
            wo_stage[...] = wo_ref[...].astype(BF16)
            for cp in own + gather([me] * 3):
                cp.start()

        xf = x_ref[...]
        r1 = lax.rsqrt(jnp.mean(xf * xf, axis=-1, keepdims=True) + EPS)
        xn = (xf * r1 * g_ref[...]).astype(BF16)
        cs, ss = cs_ref[0:1, :], ss_ref[0:1, :]
        c = cs * ci_ref[...] - ss * si_ref[...]
        sin = ss * ci_ref[...] + cs * si_ref[...]
        j = lax.broadcasted_iota(jnp.int32, (T_PROJ, 128), 1) % HEAD_DIM
        a = jnp.where(j < 8, -sin, 0.0)
        b = jnp.where(j >= 8, sin, 0.0)
        c_ref[...], a_ref[...], b_ref[...] = c, a, b
        proj = lambda lo_c, w: lax.dot_general(xn, w_ref[lo_c:lo_c + w, :], _NT, preferred_element_type=F32)
        q_ref[...] = (_rope(proj(0, ATTN_W), c, a, b) * SCALE).astype(BF16)
        kv = proj(ATTN_W, 2 * KV_W)
        k = _rope(kv[:, 0:KV_W], c, a, b)
        v = kv[:, KV_W:2 * KV_W]
        lo = _lane_lo(k.shape)
        for t, ref in ((k, kd_ref), (v, vd_ref)):
            sw = pltpu.roll(t, HEAD_DIM, 1)
            ref[:, 0:128] = jnp.where(lo, t, sw).astype(BF16)
            ref[:, 128:256] = jnp.where(lo, sw, t).astype(BF16)
        for n in range(REST_W // 512):
            rest_ref[:, 512 * n:512 * (n + 1)] = proj(ATTN_W + 2 * KV_W + 512 * n, 512)

        @pl.when(i == nt - 1)
        def _():
            sent = gather([me] * 3)
            for cp in gather([2 * chip[0] + chip[1] for chip in chips]):
                cp.wait_recv()
            for cp in sent:
                cp.wait_send()
            for cp in own:
                cp.wait()

    tile = lambda w: pl.BlockSpec((T_PROJ, w), lambda i: (i, 0))
    whole = lambda r, w: pl.BlockSpec((r, w), lambda i: (0, 0))
    vmem = pl.BlockSpec(memory_space=pltpu.VMEM)
    hbm = pl.BlockSpec(memory_space=pl.ANY)
    return pl.pallas_call(
        body, name="fwd_proj", grid=(nt,),
        out_shape=(jax.ShapeDtypeStruct((seq, ATTN_W), BF16), jax.ShapeDtypeStruct((seq, 2 * KV_W), BF16),
                   jax.ShapeDtypeStruct((seq, 2 * KV_W), BF16), jax.ShapeDtypeStruct((seq, REST_W), F32))
        + (jax.ShapeDtypeStruct((seq, 128), F32),) * 3
        + (jax.ShapeDtypeStruct((N_CHIPS, W_OUT_BLK, D_MODEL), BF16), jax.ShapeDtypeStruct((N_CHIPS, 8, 128), F32)),
        in_specs=[tile(D_MODEL), whole(1, D_MODEL), whole(IN_W, D_MODEL), pl.BlockSpec((8, 128), lambda i: (i, 0)),
                  pl.BlockSpec((8, 128), lambda i: (i, 0)), whole(T_PROJ, 128), whole(T_PROJ, 128), vmem, vmem],
        out_specs=(tile(ATTN_W), tile(2 * KV_W), tile(2 * KV_W), tile(REST_W), tile(128), tile(128), tile(128),
                   hbm, hbm),
        scratch_shapes=[pltpu.VMEM((W_OUT_BLK, D_MODEL), BF16), pltpu.SemaphoreType.DMA((6,)),
                        pltpu.SemaphoreType.DMA((6,)), pltpu.SemaphoreType.DMA((2,))],
        compiler_params=_params(("arbitrary",)),
    )(x, norm_g, w_in_t, cos_st, sin_st, cos_in, sin_in, w_out, conv_w8)


CONV_SPEC = pl.BlockSpec((N_CHIPS, 8, 128), lambda i: (0, 0, 0))


def _conv_rows(cw_ref):
    return jnp.concatenate([cw_ref[j] for j in range(N_CHIPS)], axis=1)


def _conv_parts(rest_ref, prev_ref, cw_ref, first):
    u = rest_ref[:, 1024:1536] * rest_ref[:, 1536:2048]
    up = prev_ref[:, 1024:1536] * prev_ref[:, 1536:2048]
    up = jnp.where(first, jnp.zeros_like(up), up)
    um1 = _shift_down(u, up, 1)
    um2 = _shift_down(u, up, 2)
    cw = _conv_rows(cw_ref)
    cv = cw[0:1, :] * um2 + cw[1:2, :] * um1 + cw[2:3, :] * u
    return u, um1, um2, cv


def _fwd_mix(x, q, kd, vd, rest, sinks, conv_w, w_out, final_g, target):
    seq = x.shape[0]
    nt = seq // T_FMIX
    nsub = T_FMIX // BLOCK
    assert D_MODEL == 2 * T_FMIX

    def body(sinks_ref, x_ref, q_ref, kd_ref, vd_ref, kdp_ref, vdp_ref, rest_ref, restp_ref, cw_ref, wo_ref,
             fg_ref, tgt_ref, prob_ref, do_ref, prod_ref, dga_ref, db_ref, dgc_ref, dcv_ref, dh2_ref,
             gwo_hbm, gwob_hbm, small_ref,
             mix_ref, dh2b_ref, gwo_ref, gate_ref, dsink_ref, out_sems):
        i = pl.program_id(0)
        attn_ref, sinkw_ref = prod_ref, dcv_ref

        @pl.when(i == 0)
        def _():
            small_ref[...] = jnp.zeros_like(small_ref)
            dsink_ref[...] = jnp.zeros_like(dsink_ref)
            gwo_ref[...] = jnp.zeros_like(gwo_ref)

        chains = []
        for rows, kk, vv, has_prev in _key_windows(i, nsub, kd_ref, vd_ref, kdp_ref, vdp_ref):
            qt = q_ref[rows, :]
            for g in range(2):
                kg = kk[:, 128 * g:128 * (g + 1)]
                chains.append(dict(g=g, rows=rows, has_prev=has_prev, vg=vv[:, 128 * g:128 * (g + 1)],
                                   s=lax.dot_general(_stack_heads(qt, g), kg, _NT, preferred_element_type=F32)))
        for ch in chains:
            ch["prob"], ch["psink"] = _softmax(ch.pop("s"), _sink_col(sinks_ref, ch["g"]), ch["has_prev"])
        for k, ch in enumerate(chains):
            prob_ref[k] = ch["prob"]
            o = jnp.dot(_unfold(ch["prob"]), ch["vg"], preferred_element_type=F32)
            ow = o * ch["psink"]
            for pp in range(2):
                lanes = slice(256 * ch["g"] + 128 * pp, 256 * ch["g"] + 128 * (pp + 1))
                attn_ref[ch["rows"], lanes] = _unstack_pair(o, pp)
                sinkw_ref[ch["rows"], lanes] = _unstack_pair(ow, pp)

        def silu_parts(t, lo_c):
            sg = _sigmoid(t)
            silu = t * sg
            gate_ref[:, lo_c:lo_c + 512] = silu
            gate_ref[:, lo_c + 512:lo_c + 1024] = sg * (1.0 + t * (1.0 - sg))
            return silu

        mix_ref[:, 0:ATTN_W] = (attn_ref[...] * silu_parts(rest_ref[:, 0:512], 0)).astype(BF16)
        u, um1, um2, cv = _conv_parts(rest_ref, restp_ref, cw_ref, i == 0)
        mix_ref[:, ATTN_W:] = (rest_ref[:, 512:1024] * cv * silu_parts(rest_ref[:, 2048:2560], 1024)).astype(BF16)

        h2 = x_ref[...] + jnp.dot(mix_ref[...], wo_ref[...], preferred_element_type=F32)
        r2 = lax.rsqrt(jnp.mean(h2 * h2, axis=-1, keepdims=True) + EPS)
        n2 = h2 * r2
        err = n2 * fg_ref[...] - tgt_ref[...]
        dy = err * (1.0 / D_MODEL)
        small_ref[6:7, :] += jnp.sum(err * err, axis=0, keepdims=True) * (0.5 / D_MODEL)
        small_ref[1:2, :] += jnp.sum(dy * n2, axis=0, keepdims=True)
        dn = dy * fg_ref[...]
        dh2 = r2 * (dn - n2 * jnp.mean(dn * n2, axis=-1, keepdims=True))
        dh2_ref[...] = dh2
        dh2b_ref[...] = dh2.astype(BF16)

        d_mix = lambda lo_r: lax.dot_general(dh2b_ref[...], wo_ref[lo_r:lo_r + 512, :], _NT, preferred_element_type=F32)
        dma = d_mix(0)
        gwo_ref[...] += lax.dot_general(mix_ref[...], dh2b_ref[...], _TN, preferred_element_type=F32)
        dga_ref[...] = (dma * attn_ref[...] * gate_ref[:, 512:1024]).astype(BF16)
        d_attn = dma * gate_ref[:, 0:512]
        do_ref[...] = d_attn.astype(BF16)
        prod_ref[...] = d_attn * attn_ref[...]
        dsink_ref[0:1, :] += jnp.sum(d_attn * sinkw_ref[...], axis=0, keepdims=True)
        bg = rest_ref[:, 512:1024]
        dmc = d_mix(ATTN_W)
        t1 = dmc * gate_ref[:, 1024:1536]
        db_ref[...] = (t1 * cv).astype(BF16)
        dcv = t1 * bg
        dcv_ref[...] = dcv
        dgc_ref[...] = (dmc * (bg * cv) * gate_ref[:, 1536:2048]).astype(BF16)
        small_ref[2:3, 0:CONV_W] += jnp.sum(dcv * um2, axis=0, keepdims=True)
        small_ref[3:4, 0:CONV_W] += jnp.sum(dcv * um1, axis=0, keepdims=True)
        small_ref[4:5, 0:CONV_W] += jnp.sum(dcv * u, axis=0, keepdims=True)

        @pl.when(i == nt - 1)
        def _():
            head = lax.broadcasted_iota(jnp.int32, (1, ATTN_W), 1) // HEAD_DIM
            for h in range(8):
                tot = jnp.sum(jnp.where(head == h, dsink_ref[0:1, :], 0.0), axis=-1, keepdims=True)
                small_ref[5:6, h:h + 1] = -tot
            outs = [pltpu.make_async_copy(gwo_ref, gwo_hbm, out_sems.at[0])]
            outs[0].start()
            for n, stage in enumerate((mix_ref, dh2b_ref)):
                slab = slice(T_FMIX * n, T_FMIX * (n + 1))
                stage[...] = gwo_ref[slab, :].astype(BF16)
                outs.append(pltpu.make_async_copy(stage, gwob_hbm.at[slab], out_sems.at[1 + n]))
                outs[-1].start()
            for cp in outs:
                cp.wait()

    tile = lambda w: pl.BlockSpec((T_FMIX, w), lambda i: (i, 0))
    whole = lambda r, w: pl.BlockSpec((r, w), lambda i: (0, 0))
    prev_blk = pl.BlockSpec((BLOCK, 2 * KV_W), lambda i: (jnp.maximum(i * nsub - 1, 0), 0))
    prev8 = pl.BlockSpec((8, REST_W), lambda i: (jnp.maximum(i * (T_FMIX // 8) - 1, 0), 0))
    bf = lambda w: jax.ShapeDtypeStruct((seq, w), BF16)
    f32 = lambda w: jax.ShapeDtypeStruct((seq, w), F32)
    return pl.pallas_call(
        body, name="fwd_mix", grid=(nt,),
        out_shape=(jax.ShapeDtypeStruct((2 * seq // BLOCK, 4 * BLOCK, BLOCK), BF16),
                   bf(ATTN_W), f32(ATTN_W), bf(ATTN_W), bf(CONV_W), bf(CONV_W), f32(CONV_W), f32(D_MODEL),
                   jax.ShapeDtypeStruct((D_MODEL, D_MODEL), F32), jax.ShapeDtypeStruct((D_MODEL, D_MODEL), BF16),
                   jax.ShapeDtypeStruct((SMALL_ROWS, D_MODEL), F32)),
        in_specs=[pl.BlockSpec(memory_space=pltpu.SMEM), tile(D_MODEL), tile(ATTN_W), tile(2 * KV_W), tile(2 * KV_W),
                  prev_blk, prev_blk, tile(REST_W), prev8, CONV_SPEC,
                  pl.BlockSpec((D_MODEL, D_MODEL), lambda i: (0, 0), pipeline_mode=pl.Buffered(1)),
                  whole(1, D_MODEL), tile(D_MODEL)],
        out_specs=(pl.BlockSpec((2 * nsub, 4 * BLOCK, BLOCK), lambda i: (i, 0, 0)),
                   tile(ATTN_W), tile(ATTN_W), tile(ATTN_W), tile(CONV_W), tile(CONV_W), tile(CONV_W), tile(D_MODEL),
                   pl.BlockSpec(memory_space=pl.ANY), pl.BlockSpec(memory_space=pl.ANY), whole(SMALL_ROWS, D_MODEL)),
        scratch_shapes=[pltpu.VMEM((T_FMIX, D_MODEL), BF16)] * 2 + [
            pltpu.VMEM((D_MODEL, D_MODEL), F32), pltpu.VMEM((T_FMIX, 4 * 512), F32), pltpu.VMEM((8, ATTN_W), F32),
            pltpu.SemaphoreType.DMA((3,))],
        compiler_params=_params(("arbitrary",)),
    )(sinks, x, q, kd, vd, kd, vd, rest, rest, conv_w, w_out, final_g, target)


def _scatter_copies(g_hbm, gb_hbm, mine, land, send_sems, recv_sems, local_sem, half):
    x, y, c = lax.axis_index("x"), lax.axis_index("y"), lax.axis_index("c")
    cps = []
    for f in range(1, 8):
        to = (x ^ (f >> 2), y ^ ((f >> 1) & 1), c ^ (f & 1))
        src = gb_hbm.at[2 * to[0] + to[1], pl.ds(to[2] * half, half)]
        cps.append(pltpu.make_async_remote_copy(src_ref=src, dst_ref=land.at[f - 1], send_sem=send_sems.at[f - 1],
                                                recv_sem=recv_sems.at[f - 1], device_id=to, device_id_type=MESH))
    own = pltpu.make_async_copy(g_hbm.at[2 * x + y, pl.ds(c * half, half)], mine, local_sem)
    return cps, own


def _scatter_finish(cps, own, mine, land, out_hbm, send_sems, recv_sems, local_sem, half):
    x, y, c = lax.axis_index("x"), lax.axis_index("y"), lax.axis_index("c")
    own.wait()
    tot = mine[...]
    for f in range(1, 8):
        cps[f - 1].wait_recv()
        tot = tot + land[f - 1].astype(F32)
    mine[...] = tot

    def swap(rows_of):
        return pltpu.make_async_remote_copy(src_ref=mine, dst_ref=out_hbm.at[pl.ds(rows_of * half, half)],
                                            send_sem=send_sems.at[7], recv_sem=recv_sems.at[7],
                                            device_id=(x, y, 1 - c), device_id_type=MESH)

    keep = pltpu.make_async_copy(mine, out_hbm.at[pl.ds(c * half, half)], local_sem)
    keep.start()
    swap(c).start()
    swap(1 - c).wait_recv()
    keep.wait()
    for cp in cps:
        cp.wait_send()
    swap(c).wait_send()


def _bwd_mix(q, kd, vd, probs, d_attn, prod, rope_c, rope_a, rope_b, g_out, g_out_b):
    seq = q.shape[0]
    nt = seq // T_MIX
    nsub = T_MIX // BLOCK
    ho = W_OUT_BLK // 2

    def body(q_ref, kd_ref, vd_ref, kdp_ref, vdp_ref, prob_ref, do_ref, prod_ref, c_ref, a_ref, b_ref, go_hbm, gob_hbm,
             dq_ref, dk_ref, dv_ref, dkh_ref, dvh_ref, go_out,
             mine_o, land_o, send_sems, recv_sems, local_sems):
        i = pl.program_id(0)
        scatter = (mine_o, land_o, send_sems, recv_sems, local_sems.at[0], ho)

        @pl.when(i == 0)
        def _():
            cps, own = _scatter_copies(go_hbm, gob_hbm, *scatter)
            for cp in cps + [own]:
                cp.start()

        lo = _lane_lo((2 * BLOCK, 128))
        dk_blocks = [None] * (nsub + 1)
        dv_blocks = [None] * (nsub + 1)

        def add(lst, n, val):
            lst[n] = val if lst[n] is None else lst[n] + val

        chains = []
        for rows, kk, vv, _ in _key_windows(i, nsub, kd_ref, vd_ref, kdp_ref, vdp_ref):
            qt = q_ref[rows, :]
            dot = do_ref[rows, :]
            for g in range(2):
                rs = jnp.sum(_stack_heads(prod_ref[rows, :], g), axis=-1, keepdims=True)
                chains.append(dict(g=g, rows=rows, rs=rs, qs=_stack_heads(qt, g), dos=_stack_heads(dot, g),
                                   kg=kk[:, 128 * g:128 * (g + 1)], vg=vv[:, 128 * g:128 * (g + 1)]))
        for k, ch in enumerate(chains):
            ch["dp"] = lax.dot_general(ch["dos"], ch["vg"], _NT, preferred_element_type=F32)
            dvd = lax.dot_general(_unfold(prob_ref[k]), ch["dos"], _TN, preferred_element_type=F32)
            ch["dv"] = dvd + pltpu.roll(dvd, HEAD_DIM, 1)
        for k, ch in enumerate(chains):
            ch["ds"] = _unfold((prob_ref[k].astype(F32) * (_fold(ch["dp"]) - ch["rs"])).astype(BF16))
        for k, ch in enumerate(chains):
            dqs = jnp.dot(ch["ds"], ch["kg"], preferred_element_type=F32) * SCALE
            c, a, b = c_ref[ch["rows"], :], a_ref[ch["rows"], :], b_ref[ch["rows"], :]
            for pp in range(2):
                lanes = slice(256 * ch["g"] + 128 * pp, 256 * ch["g"] + 128 * (pp + 1))
                dq_ref[ch["rows"], lanes] = _rope(_unstack_pair(dqs, pp), c, -a, -b).astype(BF16)
            dkd = lax.dot_general(ch["ds"], ch["qs"], _TN, preferred_element_type=F32)
            ch["dk"] = dkd + pltpu.roll(dkd, HEAD_DIM, 1)
        for sb in range(nsub):
            dk2 = jnp.where(lo, chains[2 * sb]["dk"], chains[2 * sb + 1]["dk"])
            dv2 = jnp.where(lo, chains[2 * sb]["dv"], chains[2 * sb + 1]["dv"])
            add(dk_blocks, sb, dk2[0:BLOCK])
            add(dk_blocks, sb + 1, dk2[BLOCK:])
            add(dv_blocks, sb, dv2[0:BLOCK])
            add(dv_blocks, sb + 1, dv2[BLOCK:])
        dkh_ref[0] = dk_blocks[0]
        dvh_ref[0] = dv_blocks[0]
        for sb in range(nsub):
            dk_ref[BLOCK * sb:BLOCK * (sb + 1), :] = dk_blocks[sb + 1]
            dv_ref[BLOCK * sb:BLOCK * (sb + 1), :] = dv_blocks[sb + 1]

        @pl.when(i == nt - 1)
        def _():
            cps, own = _scatter_copies(go_hbm, gob_hbm, *scatter)
            _scatter_finish(cps, own, mine_o, land_o, go_out, send_sems, recv_sems, local_sems.at[1], ho)

    tile = lambda w: pl.BlockSpec((T_MIX, w), lambda i: (i, 0))
    prev_blk = pl.BlockSpec((BLOCK, 2 * KV_W), lambda i: (jnp.maximum(i * nsub - 1, 0), 0))
    halo = pl.BlockSpec((1, BLOCK, KV_W), lambda i: (i, 0, 0))
    hbm = pl.BlockSpec(memory_space=pl.ANY)
    f32 = lambda w: jax.ShapeDtypeStruct((seq, w), F32)
    return pl.pallas_call(
        body, name="bwd_mix", grid=(nt,),
        out_shape=(jax.ShapeDtypeStruct((seq, ATTN_W), BF16), f32(KV_W), f32(KV_W),
                   jax.ShapeDtypeStruct((nt, BLOCK, KV_W), F32), jax.ShapeDtypeStruct((nt, BLOCK, KV_W), F32),
                   jax.ShapeDtypeStruct((W_OUT_BLK, D_MODEL), F32)),
        in_specs=[tile(ATTN_W), tile(2 * KV_W), tile(2 * KV_W), prev_blk, prev_blk,
                  pl.BlockSpec((2 * nsub, 4 * BLOCK, BLOCK), lambda i: (i, 0, 0)), tile(ATTN_W), tile(ATTN_W),
                  tile(128), tile(128), tile(128), hbm, hbm],
        out_specs=(tile(ATTN_W), tile(KV_W), tile(KV_W), halo, halo, hbm),
        scratch_shapes=[pltpu.VMEM((ho, D_MODEL), F32), pltpu.VMEM((7, ho, D_MODEL), BF16),
                        pltpu.SemaphoreType.DMA((8,)), pltpu.SemaphoreType.DMA((8,)), pltpu.SemaphoreType.DMA((2,))],
        compiler_params=_params(("arbitrary",)),
    )(q, kd, vd, kd, vd, probs, d_attn, prod, rope_c, rope_a, rope_b, g_out, g_out_b)


def _bwd_proj(x, norm_g, dh2, dq, dk, dv, dkh, dvh, dga, db, dgc, dcv, rest, conv_w, w_in_t, rope_c, rope_a, rope_b):
    seq = x.shape[0]
    tb = T_PROJ
    per = tb // T_MIX
    nt = seq // tb

    def body(x_ref, g_ref, dh2_ref, dq_ref, dk_ref, dv_ref, dkh_ref, dvh_ref, dkn_ref, dvn_ref, dga_ref, db_ref,
             dgc_ref, dcv_ref, dcvn_ref, ch_ref, cw_ref, w_ref, c_ref, a_ref, b_ref, gx_ref, gw_hbm, small_ref,
             dp_ref, acc_ref, out_sem):
        i = pl.program_id(0)

        @pl.when(i == 0)
        def _():
            small_ref[...] = jnp.zeros_like(small_ref)
            acc_ref[...] = jnp.zeros_like(acc_ref)

        last = i == nt - 1
        keep = jnp.where(last, 0.0, 1.0)
        pad = jnp.zeros((T_MIX - BLOCK, KV_W), F32)

        def with_halos(main_ref, halo_ref, next_ref):
            parts = []
            for m in range(1, per + 1):
                parts += [pad, halo_ref[m] if m < per else next_ref[0] * keep]
            return main_ref[...] + jnp.concatenate(parts, axis=0)

        dk = with_halos(dk_ref, dkh_ref, dkn_ref)
        dv = with_halos(dv_ref, dvh_ref, dvn_ref)
        dp_ref[:, 0:ATTN_W] = dq_ref[...]
        dp_ref[:, ATTN_W:ATTN_W + KV_W] = _rope(dk, c_ref[...], -a_ref[...], -b_ref[...]).astype(BF16)
        dp_ref[:, ATTN_W + KV_W:ATTN_W + 2 * KV_W] = dv.astype(BF16)
        base = ATTN_W + 2 * KV_W
        dp_ref[:, base:base + 512] = dga_ref[...]
        dp_ref[:, base + 512:base + 1024] = db_ref[...]
        dcv = dcv_ref[...]
        nxt = dcvn_ref[...] * keep
        cw = _conv_rows(cw_ref)
        du = cw[2:3, :] * dcv + cw[1:2, :] * _shift_up(dcv, nxt, 1) + cw[0:1, :] * _shift_up(dcv, nxt, 2)
        dp_ref[:, base + 1024:base + 1536] = (du * ch_ref[:, 512:1024]).astype(BF16)
        dp_ref[:, base + 1536:base + 2048] = (du * ch_ref[:, 0:512]).astype(BF16)
        dp_ref[:, base + 2048:base + 2560] = dgc_ref[...]

        xf = x_ref[...]
        r1 = lax.rsqrt(jnp.mean(xf * xf, axis=-1, keepdims=True) + EPS)
        n1 = xf * r1
        xn = (n1 * g_ref[...]).astype(BF16)
        for n in range(IN_W // 256):
            cols = slice(256 * n, 256 * (n + 1))
            acc_ref[cols, :] += lax.dot_general(dp_ref[:, cols], xn, _TN, preferred_element_type=F32)
        acc_out = pltpu.make_async_copy(acc_ref, gw_hbm, out_sem.at[0])

        @pl.when(last)
        def _():
            acc_out.start()

        dxn = jnp.dot(dp_ref[...], w_ref[...], preferred_element_type=F32)
        small_ref[0:1, :] += jnp.sum(dxn * n1, axis=0, keepdims=True)
        dxg = dxn * g_ref[...]
        gx_ref[...] = r1 * (dxg - n1 * jnp.mean(dxg * n1, axis=-1, keepdims=True)) + dh2_ref[...]

        @pl.when(last)
        def _():
            acc_out.wait()

    tile = lambda w: pl.BlockSpec((tb, w), lambda i: (i, 0))
    whole = lambda r, w: pl.BlockSpec((r, w), lambda i: (0, 0))
    halo = pl.BlockSpec((per, BLOCK, KV_W), lambda i: (i, 0, 0))
    halo_next = pl.BlockSpec((1, BLOCK, KV_W), lambda i: (jnp.minimum((i + 1) * per, seq // T_MIX - 1), 0, 0))
    next8 = pl.BlockSpec((8, CONV_W), lambda i: (jnp.minimum((i + 1) * (tb // 8), seq // 8 - 1), 0))
    ch = pl.BlockSpec((tb, 1024), lambda i: (i, 1))
    return pl.pallas_call(
        body, name="bwd_proj", grid=(nt,),
        out_shape=(jax.ShapeDtypeStruct((seq, D_MODEL), F32), jax.ShapeDtypeStruct((IN_W, D_MODEL), F32),
                   jax.ShapeDtypeStruct((SMALL_ROWS, D_MODEL), F32)),
        in_specs=[tile(D_MODEL), whole(1, D_MODEL), tile(D_MODEL), tile(ATTN_W), tile(KV_W), tile(KV_W), halo, halo,
                  halo_next, halo_next,
                  tile(ATTN_W), tile(CONV_W), tile(CONV_W), tile(CONV_W), next8, ch, CONV_SPEC,
                  pl.BlockSpec((IN_W, D_MODEL), lambda i: (0, 0), pipeline_mode=pl.Buffered(1)),
                  tile(128), tile(128), tile(128)],
        out_specs=(tile(D_MODEL), pl.BlockSpec(memory_space=pl.ANY), whole(SMALL_ROWS, D_MODEL)),
        scratch_shapes=[pltpu.VMEM((tb, IN_W), BF16), pltpu.VMEM((IN_W, D_MODEL), F32), pltpu.SemaphoreType.DMA((1,))],
        compiler_params=_params(("arbitrary",)),
    )(x, norm_g, dh2, dq, dk, dv, dkh, dvh, dkh, dvh, dga, db, dgc, dcv, dcv, rest, conv_w, w_in_t,
      rope_c, rope_a, rope_b)


def _adamw_step(w, g, m, v):
    m2 = ADAM_B1 * m + (1.0 - ADAM_B1) * g
    v2 = ADAM_B2 * v + (1.0 - ADAM_B2) * jnp.square(g)
    m_hat = m2 / (1.0 - ADAM_B1 ** ADAM_STEP)
    v_hat = v2 / (1.0 - ADAM_B2 ** ADAM_STEP)
    return -ADAM_LR * (m_hat / (jnp.sqrt(v_hat) + ADAM_EPS) + ADAM_WD * w), m2, v2


def _adamw_weights(groups):
    steps = 4

    def body(*refs):
        ins, outs = refs[:4 * len(groups)], refs[4 * len(groups):]
        for k in range(len(groups)):
            res = _adamw_step(*(r[...] for r in ins[4 * k:4 * k + 4]))
            for o_ref, val in zip(outs[3 * k:3 * k + 3], res):
                o_ref[...] = val

    in_specs, out_specs, out_shape = [], [], []
    for w, _, _, _ in groups:
        rows, cols = w.shape
        spec = pl.BlockSpec((rows // steps, cols), lambda i: (i, 0))
        in_specs += [spec] * 4
        out_specs += [spec] * 3
        out_shape += [jax.ShapeDtypeStruct((rows, cols), F32)] * 3
    flat = pl.pallas_call(
        body, name="adamw_weights", grid=(steps,), out_shape=tuple(out_shape), in_specs=in_specs,
        out_specs=tuple(out_specs), compiler_params=_params(("arbitrary",)),
    )(*[a for grp in groups for a in grp])
    return [flat[3 * k:3 * k + 3] for k in range(len(groups))]


def _adamw_small(chip, small, params, m, v):
    def body(chip_ref, small_ref, conv_ref, *refs):
        ins, outs = refs[:12], refs[12:]
        outs[0][...] = jnp.sum(small_ref[6:7, :], axis=-1, keepdims=True)
        grads = (small_ref[0:1, :], small_ref[1:2, :], conv_ref[2:5, :], small_ref[5:6, 0:8])
        for k, g in enumerate(grads):
            outs[1 + k][...] = g
            res = _adamw_step(ins[k][...], g, ins[4 + k][...], ins[8 + k][...])
            for n, val in enumerate(res):
                outs[5 + 4 * n + k][...] = val

    full = lambda a: pl.BlockSpec(a.shape, lambda i, c: (0,) * len(a.shape))
    shapes = [jax.ShapeDtypeStruct(p.shape, F32) for p in params]
    outs = [jax.ShapeDtypeStruct((1, 1), F32)] + shapes * 4
    flat = pl.pallas_call(
        body, name="adamw_small",
        grid_spec=pltpu.PrefetchScalarGridSpec(
            num_scalar_prefetch=1, grid=(1,),
            in_specs=[full(small), pl.BlockSpec((SMALL_ROWS, 128), lambda i, c: (0, c[0]))]
            + [full(a) for a in (*params, *m, *v)],
            out_specs=tuple(full(s) for s in outs)),
        out_shape=tuple(outs), compiler_params=_params(("arbitrary",)),
    )(chip, small, small, *params, *m, *v)
    return flat[0], flat[1:5], [flat[5 + 4 * n:9 + 4 * n] for n in range(3)]


def kernel(x, norm_g, w_in, sinks, conv_w, w_out, final_g, loss_target, m_norm_g, m_w_in, m_sinks, m_conv_w, m_w_out, m_final_g, v_norm_g, v_w_in, v_sinks, v_conv_w, v_w_out, v_final_g):
    seq = x.shape[1]
    x2 = x.reshape(seq, D_MODEL)
    tgt = loss_target.reshape(seq, D_MODEL)
    ng = norm_g.reshape(1, D_MODEL)
    fg = final_g.reshape(1, D_MODEL)
    chip = 2 * lax.axis_index("x") + lax.axis_index("y")

    conv_w8 = jnp.zeros((8, 128), F32).at[0:3].set(conv_w)
    w_in_full = _gather_w_in(w_in.T).reshape(IN_W, D_MODEL)

    q, kd, vd, rest, rope_c, rope_a, rope_b, wo_all, cw_all = _fwd_proj(x2, ng, w_in_full, w_out, conv_w8)
    w_out_full = wo_all.reshape(D_MODEL, D_MODEL)
    probs, d_attn, prod, dga, db, dgc, dcv, dh2, g_wo, g_wo_b, small_m = _fwd_mix(
        x2, q, kd, vd, rest, sinks, cw_all, w_out_full, fg, tgt)
    out_blocks = lambda t: t.reshape(N_CHIPS, W_OUT_BLK, D_MODEL)
    dq, dk, dv, dkh, dvh, grad_w_out = _bwd_mix(
        q, kd, vd, probs, d_attn, prod, rope_c, rope_a, rope_b, out_blocks(g_wo), out_blocks(g_wo_b))
    grad_x, g_wi, small_p = _bwd_proj(x2, ng, dh2, dq, dk, dv, dkh, dvh, dga, db, dgc, dcv, rest, cw_all,
                                      w_in_full, rope_c, rope_a, rope_b)

    g_in_blocks = g_wi.reshape(N_CHIPS, W_IN_BLK, D_MODEL)
    grad_w_in_t, small = _reduce_grads(g_in_blocks, small_m, small_p)

    (upd_wi, upd_wo) = _adamw_weights([(w_in.T, grad_w_in_t, m_w_in.T, v_w_in.T),
                                       (w_out, grad_w_out, m_w_out, v_w_out)])
    row = lambda t: t.reshape(1, -1)
    loss, grads_s, upd_s = _adamw_small(
        chip.reshape(1), small, (ng, fg, conv_w, row(sinks)),
        (row(m_norm_g), row(m_final_g), m_conv_w, row(m_sinks)),
        (row(v_norm_g), row(v_final_g), v_conv_w, row(v_sinks)))

    def named(ng_, fg_, cw_, sk_, wi_t, wo_):
        return [ng_.reshape(D_MODEL), wi_t.T, sk_.reshape(8), cw_, wo_, fg_.reshape(D_MODEL)]

    g_named = named(*grads_s, grad_w_in_t, grad_w_out)
    out = [loss.reshape(()), grad_x.reshape(1, seq, D_MODEL)] + g_named
    for n in range(3):
        out += named(*upd_s[n], upd_wi[n], upd_wo[n])
    return tuple(out)
```

```python
import jax
import jax.numpy as jnp
from jax import lax
from jax.experimental import pallas as pl
from jax.experimental.pallas import tpu as pltpu

F32 = jnp.float32
BF16 = jnp.bfloat16

D_MODEL = 1024
HEAD_DIM = 64
ATTN_W = 512
KV_W = 128
CONV_W = 512
IN_W = 3328
REST_W = IN_W - ATTN_W - 2 * KV_W
BLOCK = 128
ROT_DIM = 16
ROPE_THETA = 500000.0
EPS = 1e-5
SCALE = 0.125
NEG = -1e30

N_CHIPS = 4
W_IN_BLK = IN_W // N_CHIPS
W_OUT_BLK = D_MODEL // N_CHIPS

ADAM_LR = 0.001
ADAM_B1 = 0.9
ADAM_B2 = 0.999
ADAM_EPS = 1e-08
ADAM_WD = 0.01
ADAM_STEP = 10

VMEM_LIMIT = 60 * 1024 * 1024
T_PROJ = 512
T_FMIX = 512
T_MIX = 512
SMALL_ROWS = 8
MESH = pl.DeviceIdType.MESH

_NT = (((1,), (1,)), ((), ()))
_TN = (((0,), (0,)), ((), ()))


def _params(sem=None):
    kw = dict(vmem_limit_bytes=VMEM_LIMIT)
    if sem is not None:
        kw["dimension_semantics"] = sem
    return pltpu.CompilerParams(**kw)


def _sigmoid(t):
    return 1.0 / (1.0 + jnp.exp(-t))


def _shift_down(t, prev8, k):
    rolled = pltpu.roll(t, k, 0)
    row = lax.broadcasted_iota(jnp.int32, t.shape, 0)
    for j in range(k):
        rolled = jnp.where(row == j, prev8[8 - k + j:8 - k + j + 1, :], rolled)
    return rolled


def _shift_up(t, next8, k):
    n = t.shape[0]
    rolled = pltpu.roll(t, n - k, 0)
    row = lax.broadcasted_iota(jnp.int32, t.shape, 0)
    for j in range(k):
        rolled = jnp.where(row == n - k + j, next8[j:j + 1, :], rolled)
    return rolled


def _rope(t, c, a, b):
    w = t.shape[1]
    reps = w // 128
    if reps > 1:
        c, a, b = (jnp.concatenate([z] * reps, axis=1) for z in (c, a, b))
    return t * c + pltpu.roll(t, w - 8, 1) * a + pltpu.roll(t, 8, 1) * b


def _lane_lo(shape):
    return lax.broadcasted_iota(jnp.int32, shape, 1) < HEAD_DIM


def _stack_heads(t, g):
    lo = _lane_lo((BLOCK, 128))
    parts = []
    for hh in range(4):
        pair = t[:, 256 * g + 128 * (hh // 2):256 * g + 128 * (hh // 2) + 128]
        keep = lo if hh % 2 == 0 else jnp.logical_not(lo)
        parts.append(jnp.where(keep, pair, jnp.zeros_like(pair)))
    return jnp.concatenate(parts, axis=0)


def _unstack_pair(o, pp):
    lo = _lane_lo((BLOCK, 128))
    return jnp.where(lo, o[256 * pp:256 * pp + 128], o[256 * pp + 128:256 * pp + 256])


def _sink_col(sinks_ref, g):
    r = lax.broadcasted_iota(jnp.int32, (4 * BLOCK, 1), 0) // BLOCK
    col = jnp.full((4 * BLOCK, 1), sinks_ref[4 * g + 3], F32)
    for hh in range(3):
        col = jnp.where(r == hh, sinks_ref[4 * g + hh], col)
    return col


def _upper():
    r = lax.broadcasted_iota(jnp.int32, (4 * BLOCK, BLOCK), 0) % BLOCK
    return lax.broadcasted_iota(jnp.int32, (4 * BLOCK, BLOCK), 1) > r


def _fold(t):
    return jnp.where(_upper(), t[:, 0:BLOCK], t[:, BLOCK:])


def _unfold(t):
    zero = jnp.zeros_like(t)
    return jnp.concatenate([jnp.where(_upper(), t, zero), jnp.where(_upper(), zero, t)], axis=1)


def _softmax(s, sink_col, has_prev):
    if has_prev is not True:
        s = jnp.concatenate([jnp.where(has_prev, s[:, 0:BLOCK], NEG), s[:, BLOCK:]], axis=1)
    f = _fold(s)
    m = jnp.maximum(jnp.max(f, axis=-1, keepdims=True), sink_col)
    p = jnp.exp(f - m)
    es = jnp.exp(sink_col - m)
    inv = 1.0 / (jnp.sum(p, axis=-1, keepdims=True) + es)
    return (p * inv).astype(BF16), es * inv


def _key_windows(i, nsub, kd_ref, vd_ref, kdp_ref, vdp_ref):
    out = []
    for sb in range(nsub):
        rows = slice(BLOCK * sb, BLOCK * (sb + 1))
        if sb == 0:
            kk = jnp.concatenate([kdp_ref[...], kd_ref[rows, :]], axis=0)
            vv = jnp.concatenate([vdp_ref[...], vd_ref[rows, :]], axis=0)
            out.append((rows, kk, vv, i > 0))
        else:
            both = slice(BLOCK * (sb - 1), BLOCK * (sb + 1))
            out.append((rows, kd_ref[both, :], vd_ref[both, :], True))
    return out


def _gather_w_in(w_in_t):
    hi = W_IN_BLK // 2
    qr = hi // 2

    def body(wi_hbm, wi_all, f_mine, f_other, b_mine, b_other, send_sems, recv_sems, local_sems):
        x, y, c = lax.axis_index("x"), lax.axis_index("y"), lax.axis_index("c")
        me, sibling = (x, y), (x, y, 1 - c)
        xnb, ynb, diag = (1 - x, y), (x, 1 - y), (1 - x, 1 - y)

        loads = [pltpu.make_async_copy(wi_hbm.at[pl.ds(c * hi, hi)], f_mine, local_sems.at[0]),
                 pltpu.make_async_copy(wi_hbm.at[pl.ds((1 - c) * hi, hi)], f_other, local_sems.at[1])]
        for cp in loads:
            cp.start()

        def copy(k, chip, half, quarter, to, src=None):
            r = wi_all.at[2 * chip[0] + chip[1], pl.ds(half * hi + quarter * qr, qr)]
            return pltpu.make_async_remote_copy(src_ref=r if src is None else src, dst_ref=r, send_sem=send_sems.at[k],
                                                recv_sem=recv_sems.at[k], device_id=to, device_id_type=MESH)

        plan = [(0, xnb, 0), (1, ynb, 1), (2, xnb, 1), (3, ynb, 0)]
        loads[0].wait()
        b_mine[...] = f_mine[...].astype(BF16)
        sent = [copy(k, me, c, quarter, (*nb, c), src=b_mine.at[pl.ds(quarter * qr, qr)]) for k, nb, quarter in plan]
        for cp in sent:
            cp.start()
        loads[1].wait()
        b_other[...] = f_other[...].astype(BF16)
        own_slot = wi_all.at[2 * x + y]
        keeps = [pltpu.make_async_copy(b_mine, own_slot.at[pl.ds(c * hi, hi)], local_sems.at[2]),
                 pltpu.make_async_copy(b_other, own_slot.at[pl.ds((1 - c) * hi, hi)], local_sems.at[3])]
        for cp in keeps:
            cp.start()
        arrivals = [(0, xnb, 0), (1, ynb, 1), (2, xnb, 1), (3, ynb, 0), (4, diag, 0), (5, diag, 1)]
        relay = {0: (4, ynb), 1: (5, xnb)}
        for k, chip, quarter in arrivals:
            copy(k, chip, c, quarter, (x, y, c)).wait_recv()
            if k in relay:
                sent.append(copy(relay[k][0], chip, c, quarter, (*relay[k][1], c)))
                sent[-1].start()
            sent.append(copy(6 + k, chip, c, quarter, sibling))
            sent[-1].start()
        for k, chip, quarter in arrivals:
            copy(6 + k, chip, 1 - c, quarter, (x, y, c)).wait_recv()
        for cp in sent:
            cp.wait_send()
        for cp in keeps:
            cp.wait()

    hbm = pl.BlockSpec(memory_space=pl.ANY)
    return pl.pallas_call(
        body, name="gather_w_in",
        out_shape=jax.ShapeDtypeStruct((N_CHIPS, W_IN_BLK, D_MODEL), BF16),
        in_specs=[hbm], out_specs=hbm,
        scratch_shapes=[pltpu.VMEM((hi, D_MODEL), F32), pltpu.VMEM((hi, D_MODEL), F32),
                        pltpu.VMEM((hi, D_MODEL), BF16), pltpu.VMEM((hi, D_MODEL), BF16),
                        pltpu.SemaphoreType.DMA((12,)), pltpu.SemaphoreType.DMA((12,)), pltpu.SemaphoreType.DMA((4,))],
        compiler_params=_params(),
    )(w_in_t)


def _reduce_grads(g_in, *smalls):
    hi = W_IN_BLK // 2
    qr = hi // 2
    q0, q1 = slice(0, qr), slice(qr, hi)

    def body(gi_hbm, s0_ref, s1_ref, gi_out, small_out,
             mine_i, sib_i, out_i, ici_i, small_in, small_ref, send_sems, recv_sems, local_sems):
        x, y, c = lax.axis_index("x"), lax.axis_index("y"), lax.axis_index("c")
        my_dev = 4 * x + 2 * y + c
        sibling = (x, y, 1 - c)
        xnb, ynb = (1 - x, y, c), (x, 1 - y, c)
        order = [(1 - x, 1 - y), (1 - x, y), (x, 1 - y), (x, y)]

        def remote(k, src, dst, to):
            return pltpu.make_async_remote_copy(src_ref=src, dst_ref=dst, send_sem=send_sems.at[k],
                                                recv_sem=recv_sems.at[k], device_id=to, device_id_type=MESH)

        small_ref[...] = s0_ref[...] + s1_ref[...]
        small_cps = []
        for f in range(1, 8):
            fx, fy, fc = f >> 2, (f >> 1) & 1, f & 1
            small_cps.append(remote(11 + f - 1, small_ref, small_in.at[f - 1], (x ^ fx, y ^ fy, c ^ fc)))
        for cp in small_cps:
            cp.start()

        own, to_sib = [], []
        for n, chip in enumerate(order):
            j = 2 * chip[0] + chip[1]
            own.append(pltpu.make_async_copy(gi_hbm.at[j, pl.ds(c * hi, hi)], mine_i.at[n], local_sems.at[n]))
            to_sib.append(remote(6 + n, gi_hbm.at[j, pl.ds((1 - c) * hi, hi)], sib_i.at[n], sibling))
            own[-1].start()
            to_sib[-1].start()

        def pair_sum(n):
            own[n].wait()
            to_sib[n].wait_recv()
            return mine_i[n] + sib_i[n]

        ici = [remote(k, out_i.at[k], ici_i.at[k], xnb if k % 2 == 0 else ynb) for k in range(6)]

        def send(k, rows_f32):
            out_i[k] = rows_f32.astype(BF16)
            ici[k].start()

        p_diag = pair_sum(0)
        send(0, p_diag[q0])
        send(1, p_diag[q1])
        p_x = pair_sum(1)
        send(2, p_x[q0])
        p_y = pair_sum(2)
        send(3, p_y[q1])
        ici[0].wait_recv()
        send(5, p_y[q0] + ici_i[0].astype(F32))
        ici[1].wait_recv()
        send(4, p_x[q1] + ici_i[1].astype(F32))
        p_mine = pair_sum(3)
        swap, keep = [], []
        for quarter, (rows, a, b) in enumerate(((q0, 2, 5), (q1, 4, 3))):
            ici[a].wait_recv()
            ici[b].wait_recv()
            mine_i[3, rows, :] = p_mine[rows] + ici_i[a].astype(F32) + ici_i[b].astype(F32)
            dst = gi_out.at[pl.ds(c * hi + quarter * qr, qr)]
            swap.append(remote(10 + 8 * quarter, mine_i.at[3, rows], dst, sibling))
            keep.append(pltpu.make_async_copy(mine_i.at[3, rows], dst, local_sems.at[4 + quarter]))
            swap[-1].start()
            keep[-1].start()

        for cp in small_cps:
            cp.wait_recv()
        total = jnp.zeros((SMALL_ROWS, D_MODEL), F32)
        for d in range(8):
            slot = jnp.maximum((d ^ my_dev) - 1, 0)
            total = total + jnp.where(d == my_dev, small_ref[...], small_in[slot])
        small_out[...] = total

        for quarter in range(2):
            theirs = gi_out.at[pl.ds((1 - c) * hi + quarter * qr, qr)]
            remote(10 + 8 * quarter, theirs, theirs, sibling).wait_recv()
        for cp in keep:
            cp.wait()
        for cp in to_sib + ici + swap + small_cps:
            cp.wait_send()

    vmem = pl.BlockSpec(memory_space=pltpu.VMEM)
    anyspace = pl.BlockSpec(memory_space=pl.ANY)
    return pl.pallas_call(
        body, name="reduce_grads",
        out_shape=(jax.ShapeDtypeStruct((W_IN_BLK, D_MODEL), F32), jax.ShapeDtypeStruct((SMALL_ROWS, D_MODEL), F32)),
        in_specs=[anyspace, vmem, vmem], out_specs=(anyspace, vmem),
        scratch_shapes=[pltpu.VMEM((N_CHIPS, hi, D_MODEL), F32), pltpu.VMEM((N_CHIPS, hi, D_MODEL), F32),
                        pltpu.VMEM((6, qr, D_MODEL), BF16), pltpu.VMEM((6, qr, D_MODEL), BF16),
                        pltpu.VMEM((7, SMALL_ROWS, D_MODEL), F32), pltpu.VMEM((SMALL_ROWS, D_MODEL), F32),
                        pltpu.SemaphoreType.DMA((19,)), pltpu.SemaphoreType.DMA((19,)),
                        pltpu.SemaphoreType.DMA((6,))],
        compiler_params=_params(),
    )(g_in, *smalls)


def _fwd_proj(x, norm_g, w_in_t, w_out, conv_w8):
    seq = x.shape[0]
    nt = seq // T_PROJ
    lane = jnp.arange(128, dtype=jnp.int32) % HEAD_DIM
    inv_freq = ROPE_THETA ** (-(2 * (lane % 8)).astype(F32) / ROT_DIM)
    inv_freq = jnp.where(lane < ROT_DIM, inv_freq, 0.0).reshape(1, 128)
    in_tile = jnp.arange(T_PROJ, dtype=jnp.int32).astype(F32)[:, None] * inv_freq
    cos_in, sin_in = jnp.cos(in_tile), jnp.sin(in_tile)
    start = jnp.repeat((jnp.arange(nt, dtype=jnp.int32) * T_PROJ).astype(F32), 8)[:, None] * inv_freq
    cos_st, sin_st = jnp.cos(start), jnp.sin(start)

    def body(x_ref, g_ref, w_ref, cs_ref, ss_ref, ci_ref, si_ref, wo_ref, cw_ref,
             q_ref, kd_ref, vd_ref, rest_ref, c_ref, a_ref, b_ref, wo_all, cw_all,
             wo_stage, send_sems, recv_sems, local_sems):
        i = pl.program_id(0)
        mx, my, mc = lax.axis_index("x"), lax.axis_index("y"), lax.axis_index("c")
        chips = [(1 - mx, my), (mx, 1 - my), (1 - mx, 1 - my)]

        def gather(blocks):
            cps = []
            for k, chip in enumerate(chips):
                for n, (src, dst) in enumerate(((wo_stage, wo_all), (cw_ref, cw_all))):
                    cps.append(pltpu.make_async_remote_copy(
                        src_ref=src, dst_ref=dst.at[blocks[k]], send_sem=send_sems.at[2 * k + n],
                        recv_sem=recv_sems.at[2 * k + n], device_id=(*chip, mc), device_id_type=MESH))
            return cps

        me = 2 * mx + my
        own = [pltpu.make_async_copy(wo_stage, wo_all.at[me], local_sems.at[0]),
               pltpu.make_async_copy(cw_ref, cw_all.at[me], local_sems.at[1])]

        @pl.when(i == 0)
        def _():
            wo_stage[...] = wo_ref[...].astype(BF16)
            for cp in own + gather([me] * 3):
                cp.start()

        xf = x_ref[...]
        r1 = lax.rsqrt(jnp.mean(xf * xf, axis=-1, keepdims=True) + EPS)
        xn = (xf * r1 * g_ref[...]).astype(BF16)
        cs, ss = cs_ref[0:1, :], ss_ref[0:1, :]
        c = cs * ci_ref[...] - ss * si_ref[...]
        sin = ss * ci_ref[...] + cs * si_ref[...]
        j = lax.broadcasted_iota(jnp.int32, (T_PROJ, 128), 1) % HEAD_DIM
        a = jnp.where(j < 8, -sin, 0.0)
        b = jnp.where(j >= 8, sin, 0.0)
        c_ref[...], a_ref[...], b_ref[...] = c, a, b
        proj = lambda lo_c, w: lax.dot_general(xn, w_ref[lo_c:lo_c + w, :], _NT, preferred_element_type=F32)
        q_ref[...] = (_rope(proj(0, ATTN_W), c, a, b) * SCALE).astype(BF16)
        kv = proj(ATTN_W, 2 * KV_W)
        k = _rope(kv[:, 0:KV_W], c, a, b)
        v = kv[:, KV_W:2 * KV_W]
        lo = _lane_lo(k.shape)
        for t, ref in ((k, kd_ref), (v, vd_ref)):
            sw = pltpu.roll(t, HEAD_DIM, 1)
            ref[:, 0:128] = jnp.where(lo, t, sw).astype(BF16)
            ref[:, 128:256] = jnp.where(lo, sw, t).astype(BF16)
        for n in range(REST_W // 512):
            rest_ref[:, 512 * n:512 * (n + 1)] = proj(ATTN_W + 2 * KV_W + 512 * n, 512)

        @pl.when(i == nt - 1)
        def _():
            sent = gather([me] * 3)
            for cp in gather([2 * chip[0] + chip[1] for chip in chips]):
                cp.wait_recv()
            for cp in sent:
                cp.wait_send()
            for cp in own:
                cp.wait()

    tile = lambda w: pl.BlockSpec((T_PROJ, w), lambda i: (i, 0))
    whole = lambda r, w: pl.BlockSpec((r, w), lambda i: (0, 0))
    vmem = pl.BlockSpec(memory_space=pltpu.VMEM)
    hbm = pl.BlockSpec(memory_space=pl.ANY)
    return pl.pallas_call(
        body, name="fwd_proj", grid=(nt,),
        out_shape=(jax.ShapeDtypeStruct((seq, ATTN_W), BF16), jax.ShapeDtypeStruct((seq, 2 * KV_W), BF16),
                   jax.ShapeDtypeStruct((seq, 2 * KV_W), BF16), jax.ShapeDtypeStruct((seq, REST_W), F32))
        + (jax.ShapeDtypeStruct((seq, 128), F32),) * 3
        + (jax.ShapeDtypeStruct((N_CHIPS, W_OUT_BLK, D_MODEL), BF16), jax.ShapeDtypeStruct((N_CHIPS, 8, 128), F32)),
        in_specs=[tile(D_MODEL), whole(1, D_MODEL), whole(IN_W, D_MODEL), pl.BlockSpec((8, 128), lambda i: (i, 0)),
                  pl.BlockSpec((8, 128), lambda i: (i, 0)), whole(T_PROJ, 128), whole(T_PROJ, 128), vmem, vmem],
        out_specs=(tile(ATTN_W), tile(2 * KV_W), tile(2 * KV_W), tile(REST_W), tile(128), tile(128), tile(128),
                   hbm, hbm),
        scratch_shapes=[pltpu.VMEM((W_OUT_BLK, D_MODEL), BF16), pltpu.SemaphoreType.DMA((6,)),
                        pltpu.SemaphoreType.DMA((6,)), pltpu.SemaphoreType.DMA((2,))],
        compiler_params=_params(("arbitrary",)),
    )(x, norm_g, w_in_t, cos_st, sin_st, cos_in, sin_in, w_out, conv_w8)


CONV_SPEC = pl.BlockSpec((N_CHIPS, 8, 128), lambda i: (0, 0, 0))


def _conv_rows(cw_ref):
    return jnp.concatenate([cw_ref[j] for j in range(N_CHIPS)], axis=1)


def _conv_parts(rest_ref, prev_ref, cw_ref, first):
    u = rest_ref[:, 1024:1536] * rest_ref[:, 1536:2048]
    up = prev_ref[:, 1024:1536] * prev_ref[:, 1536:2048]
    up = jnp.where(first, jnp.zeros_like(up), up)
    um1 = _shift_down(u, up, 1)
    um2 = _shift_down(u, up, 2)
    cw = _conv_rows(cw_ref)
    cv = cw[0:1, :] * um2 + cw[1:2, :] * um1 + cw[2:3, :] * u
    return u, um1, um2, cv


def _fwd_mix(x, q, kd, vd, rest, sinks, conv_w, w_out, final_g, target):
    seq = x.shape[0]
    nt = seq // T_FMIX
    nsub = T_FMIX // BLOCK
    assert D_MODEL == 2 * T_FMIX

    def body(sinks_ref, x_ref, q_ref, kd_ref, vd_ref, kdp_ref, vdp_ref, rest_ref, restp_ref, cw_ref, wo_ref,
             fg_ref, tgt_ref, prob_ref, do_ref, prod_ref, dga_ref, db_ref, dgc_ref, dcv_ref, dh2_ref,
             gwo_hbm, gwob_hbm, small_ref,
             mix_ref, dh2b_ref, gwo_ref, gate_ref, dsink_ref, out_sems):
        i = pl.program_id(0)
        attn_ref, sinkw_ref = prod_ref, dcv_ref

        @pl.when(i == 0)
        def _():
            small_ref[...] = jnp.zeros_like(small_ref)
            dsink_ref[...] = jnp.zeros_like(dsink_ref)
            gwo_ref[...] = jnp.zeros_like(gwo_ref)

        chains = []
        for rows, kk, vv, has_prev in _key_windows(i, nsub, kd_ref, vd_ref, kdp_ref, vdp_ref):
            qt = q_ref[rows, :]
            for g in range(2):
                kg = kk[:, 128 * g:128 * (g + 1)]
                chains.append(dict(g=g, rows=rows, has_prev=has_prev, vg=vv[:, 128 * g:128 * (g + 1)],
                                   s=lax.dot_general(_stack_heads(qt, g), kg, _NT, preferred_element_type=F32)))
        for ch in chains:
            ch["prob"], ch["psink"] = _softmax(ch.pop("s"), _sink_col(sinks_ref, ch["g"]), ch["has_prev"])
        for k, ch in enumerate(chains):
            prob_ref[k] = ch["prob"]
            o = jnp.dot(_unfold(ch["prob"]), ch["vg"], preferred_element_type=F32)
            ow = o * ch["psink"]
            for pp in range(2):
                lanes = slice(256 * ch["g"] + 128 * pp, 256 * ch["g"] + 128 * (pp + 1))
                attn_ref[ch["rows"], lanes] = _unstack_pair(o, pp)
                sinkw_ref[ch["rows"], lanes] = _unstack_pair(ow, pp)

        def silu_parts(t, lo_c):
            sg = _sigmoid(t)
            silu = t * sg
            gate_ref[:, lo_c:lo_c + 512] = silu
            gate_ref[:, lo_c + 512:lo_c + 1024] = sg * (1.0 + t * (1.0 - sg))
            return silu

        mix_ref[:, 0:ATTN_W] = (attn_ref[...] * silu_parts(rest_ref[:, 0:512], 0)).astype(BF16)
        u, um1, um2, cv = _conv_parts(rest_ref, restp_ref, cw_ref, i == 0)
        mix_ref[:, ATTN_W:] = (rest_ref[:, 512:1024] * cv * silu_parts(rest_ref[:, 2048:2560], 1024)).astype(BF16)

        h2 = x_ref[...] + jnp.dot(mix_ref[...], wo_ref[...], preferred_element_type=F32)
        r2 = lax.rsqrt(jnp.mean(h2 * h2, axis=-1, keepdims=True) + EPS)
        n2 = h2 * r2
        err = n2 * fg_ref[...] - tgt_ref[...]
        dy = err * (1.0 / D_MODEL)
        small_ref[6:7, :] += jnp.sum(err * err, axis=0, keepdims=True) * (0.5 / D_MODEL)
        small_ref[1:2, :] += jnp.sum(dy * n2, axis=0, keepdims=True)
        dn = dy * fg_ref[...]
        dh2 = r2 * (dn - n2 * jnp.mean(dn * n2, axis=-1, keepdims=True))
        dh2_ref[...] = dh2
        dh2b_ref[...] = dh2.astype(BF16)

        d_mix = lambda lo_r: lax.dot_general(dh2b_ref[...], wo_ref[lo_r:lo_r + 512, :], _NT, preferred_element_type=F32)
        dma = d_mix(0)
        gwo_ref[...] += lax.dot_general(mix_ref[...], dh2b_ref[...], _TN, preferred_element_type=F32)
        dga_ref[...] = (dma * attn_ref[...] * gate_ref[:, 512:1024]).astype(BF16)
        d_attn = dma * gate_ref[:, 0:512]
        do_ref[...] = d_attn.astype(BF16)
        prod_ref[...] = d_attn * attn_ref[...]
        dsink_ref[0:1, :] += jnp.sum(d_attn * sinkw_ref[...], axis=0, keepdims=True)
        bg = rest_ref[:, 512:1024]
        dmc = d_mix(ATTN_W)
        t1 = dmc * gate_ref[:, 1024:1536]
        db_ref[...] = (t1 * cv).astype(BF16)
        dcv = t1 * bg
        dcv_ref[...] = dcv
        dgc_ref[...] = (dmc * (bg * cv) * gate_ref[:, 1536:2048]).astype(BF16)
        small_ref[2:3, 0:CONV_W] += jnp.sum(dcv * um2, axis=0, keepdims=True)
        small_ref[3:4, 0:CONV_W] += jnp.sum(dcv * um1, axis=0, keepdims=True)
        small_ref[4:5, 0:CONV_W] += jnp.sum(dcv * u, axis=0, keepdims=True)

        @pl.when(i == nt - 1)
        def _():
            head = lax.broadcasted_iota(jnp.int32, (1, ATTN_W), 1) // HEAD_DIM
            for h in range(8):
                tot = jnp.sum(jnp.where(head == h, dsink_ref[0:1, :], 0.0), axis=-1, keepdims=True)
                small_ref[5:6, h:h + 1] = -tot
            outs = [pltpu.make_async_copy(gwo_ref, gwo_hbm, out_sems.at[0])]
            outs[0].start()
            for n, stage in enumerate((mix_ref, dh2b_ref)):
                slab = slice(T_FMIX * n, T_FMIX * (n + 1))
                stage[...] = gwo_ref[slab, :].astype(BF16)
                outs.append(pltpu.make_async_copy(stage, gwob_hbm.at[slab], out_sems.at[1 + n]))
                outs[-1].start()
            for cp in outs:
                cp.wait()

    tile = lambda w: pl.BlockSpec((T_FMIX, w), lambda i: (i, 0))
    whole = lambda r, w: pl.BlockSpec((r, w), lambda i: (0, 0))
    prev_blk = pl.BlockSpec((BLOCK, 2 * KV_W), lambda i: (jnp.maximum(i * nsub - 1, 0), 0))
    prev8 = pl.BlockSpec((8, REST_W), lambda i: (jnp.maximum(i * (T_FMIX // 8) - 1, 0), 0))
    bf = lambda w: jax.ShapeDtypeStruct((seq, w), BF16)
    f32 = lambda w: jax.ShapeDtypeStruct((seq, w), F32)
    return pl.pallas_call(
        body, name="fwd_mix", grid=(nt,),
        out_shape=(jax.ShapeDtypeStruct((2 * seq // BLOCK, 4 * BLOCK, BLOCK), BF16),
                   bf(ATTN_W), f32(ATTN_W), bf(ATTN_W), bf(CONV_W), bf(CONV_W), f32(CONV_W), f32(D_MODEL),
                   jax.ShapeDtypeStruct((D_MODEL, D_MODEL), F32), jax.ShapeDtypeStruct((D_MODEL, D_MODEL), BF16),
                   jax.ShapeDtypeStruct((SMALL_ROWS, D_MODEL), F32)),
        in_specs=[pl.BlockSpec(memory_space=pltpu.SMEM), tile(D_MODEL), tile(ATTN_W), tile(2 * KV_W), tile(2 * KV_W),
                  prev_blk, prev_blk, tile(REST_W), prev8, CONV_SPEC,
                  pl.BlockSpec((D_MODEL, D_MODEL), lambda i: (0, 0), pipeline_mode=pl.Buffered(1)),
                  whole(1, D_MODEL), tile(D_MODEL)],
        out_specs=(pl.BlockSpec((2 * nsub, 4 * BLOCK, BLOCK), lambda i: (i, 0, 0)),
                   tile(ATTN_W), tile(ATTN_W), tile(ATTN_W), tile(CONV_W), tile(CONV_W), tile(CONV_W), tile(D_MODEL),
                   pl.BlockSpec(memory_space=pl.ANY), pl.BlockSpec(memory_space=pl.ANY), whole(SMALL_ROWS, D_MODEL)),
        scratch_shapes=[pltpu.VMEM((T_FMIX, D_MODEL), BF16)] * 2 + [
            pltpu.VMEM((D_MODEL, D_MODEL), F32), pltpu.VMEM((T_FMIX, 4 * 512), F32), pltpu.VMEM((8, ATTN_W), F32),
            pltpu.SemaphoreType.DMA((3,))],
        compiler_params=_params(("arbitrary",)),
    )(sinks, x, q, kd, vd, kd, vd, rest, rest, conv_w, w_out, final_g, target)


def _scatter_copies(g_hbm, gb_hbm, mine, land, send_sems, recv_sems, local_sem, half):
    x, y, c = lax.axis_index("x"), lax.axis_index("y"), lax.axis_index("c")
    cps = []
    for f in range(1, 8):
        to = (x ^ (f >> 2), y ^ ((f >> 1) & 1), c ^ (f & 1))
        src = gb_hbm.at[2 * to[0] + to[1], pl.ds(to[2] * half, half)]
        cps.append(pltpu.make_async_remote_copy(src_ref=src, dst_ref=land.at[f - 1], send_sem=send_sems.at[f - 1],
                                                recv_sem=recv_sems.at[f - 1], device_id=to, device_id_type=MESH))
    own = pltpu.make_async_copy(g_hbm.at[2 * x + y, pl.ds(c * half, half)], mine, local_sem)
    return cps, own


def _scatter_finish(cps, own, mine, land, out_hbm, send_sems, recv_sems, local_sem, half):
    x, y, c = lax.axis_index("x"), lax.axis_index("y"), lax.axis_index("c")
    own.wait()
    tot = mine[...]
    for f in range(1, 8):
        cps[f - 1].wait_recv()
        tot = tot + land[f - 1].astype(F32)
    mine[...] = tot

    def swap(rows_of):
        return pltpu.make_async_remote_copy(src_ref=mine, dst_ref=out_hbm.at[pl.ds(rows_of * half, half)],
                                            send_sem=send_sems.at[7], recv_sem=recv_sems.at[7],
                                            device_id=(x, y, 1 - c), device_id_type=MESH)

    keep = pltpu.make_async_copy(mine, out_hbm.at[pl.ds(c * half, half)], local_sem)
    keep.start()
    swap(c).start()
    swap(1 - c).wait_recv()
    keep.wait()
    for cp in cps:
        cp.wait_send()
    swap(c).wait_send()


def _bwd_mix(q, kd, vd, probs, d_attn, prod, rope_c, rope_a, rope_b, g_out, g_out_b):
    seq = q.shape[0]
    nt = seq // T_MIX
    nsub = T_MIX // BLOCK
    ho = W_OUT_BLK // 2

    def body(q_ref, kd_ref, vd_ref, kdp_ref, vdp_ref, prob_ref, do_ref, prod_ref, c_ref, a_ref, b_ref, go_hbm, gob_hbm,
             dq_ref, dk_ref, dv_ref, dkh_ref, dvh_ref, go_out,
             mine_o, land_o, send_sems, recv_sems, local_sems):
        i = pl.program_id(0)
        scatter = (mine_o, land_o, send_sems, recv_sems, local_sems.at[0], ho)

        @pl.when(i == 0)
        def _():
            cps, own = _scatter_copies(go_hbm, gob_hbm, *scatter)
            for cp in cps + [own]:
                cp.start()

        lo = _lane_lo((2 * BLOCK, 128))
        dk_blocks = [None] * (nsub + 1)
        dv_blocks = [None] * (nsub + 1)

        def add(lst, n, val):
            lst[n] = val if lst[n] is None else lst[n] + val

        chains = []
        for rows, kk, vv, _ in _key_windows(i, nsub, kd_ref, vd_ref, kdp_ref, vdp_ref):
            qt = q_ref[rows, :]
            dot = do_ref[rows, :]
            for g in range(2):
                rs = jnp.sum(_stack_heads(prod_ref[rows, :], g), axis=-1, keepdims=True)
                chains.append(dict(g=g, rows=rows, rs=rs, qs=_stack_heads(qt, g), dos=_stack_heads(dot, g),
                                   kg=kk[:, 128 * g:128 * (g + 1)], vg=vv[:, 128 * g:128 * (g + 1)]))
        for k, ch in enumerate(chains):
            ch["dp"] = lax.dot_general(ch["dos"], ch["vg"], _NT, preferred_element_type=F32)
            dvd = lax.dot_general(_unfold(prob_ref[k]), ch["dos"], _TN, preferred_element_type=F32)
            ch["dv"] = dvd + pltpu.roll(dvd, HEAD_DIM, 1)
        for k, ch in enumerate(chains):
            ch["ds"] = _unfold((prob_ref[k].astype(F32) * (_fold(ch["dp"]) - ch["rs"])).astype(BF16))
        for k, ch in enumerate(chains):
            dqs = jnp.dot(ch["ds"], ch["kg"], preferred_element_type=F32) * SCALE
            c, a, b = c_ref[ch["rows"], :], a_ref[ch["rows"], :], b_ref[ch["rows"], :]
            for pp in range(2):
                lanes = slice(256 * ch["g"] + 128 * pp, 256 * ch["g"] + 128 * (pp + 1))
                dq_ref[ch["rows"], lanes] = _rope(_unstack_pair(dqs, pp), c, -a, -b).astype(BF16)
            dkd = lax.dot_general(ch["ds"], ch["qs"], _TN, preferred_element_type=F32)
            ch["dk"] = dkd + pltpu.roll(dkd, HEAD_DIM, 1)
        for sb in range(nsub):
            dk2 = jnp.where(lo, chains[2 * sb]["dk"], chains[2 * sb + 1]["dk"])
            dv2 = jnp.where(lo, chains[2 * sb]["dv"], chains[2 * sb + 1]["dv"])
            add(dk_blocks, sb, dk2[0:BLOCK])
            add(dk_blocks, sb + 1, dk2[BLOCK:])
            add(dv_blocks, sb, dv2[0:BLOCK])
            add(dv_blocks, sb + 1, dv2[BLOCK:])
        dkh_ref[0] = dk_blocks[0]
        dvh_ref[0] = dv_blocks[0]
        for sb in range(nsub):
            dk_ref[BLOCK * sb:BLOCK * (sb + 1), :] = dk_blocks[sb + 1]
            dv_ref[BLOCK * sb:BLOCK * (sb + 1), :] = dv_blocks[sb + 1]

        @pl.when(i == nt - 1)
        def _():
            cps, own = _scatter_copies(go_hbm, gob_hbm, *scatter)
            _scatter_finish(cps, own, mine_o, land_o, go_out, send_sems, recv_sems, local_sems.at[1], ho)

    tile = lambda w: pl.BlockSpec((T_MIX, w), lambda i: (i, 0))
    prev_blk = pl.BlockSpec((BLOCK, 2 * KV_W), lambda i: (jnp.maximum(i * nsub - 1, 0), 0))
    halo = pl.BlockSpec((1, BLOCK, KV_W), lambda i: (i, 0, 0))
    hbm = pl.BlockSpec(memory_space=pl.ANY)
    f32 = lambda w: jax.ShapeDtypeStruct((seq, w), F32)
    return pl.pallas_call(
        body, name="bwd_mix", grid=(nt,),
        out_shape=(jax.ShapeDtypeStruct((seq, ATTN_W), BF16), f32(KV_W), f32(KV_W),
                   jax.ShapeDtypeStruct((nt, BLOCK, KV_W), F32), jax.ShapeDtypeStruct((nt, BLOCK, KV_W), F32),
                   jax.ShapeDtypeStruct((W_OUT_BLK, D_MODEL), F32)),
        in_specs=[tile(ATTN_W), tile(2 * KV_W), tile(2 * KV_W), prev_blk, prev_blk,
                  pl.BlockSpec((2 * nsub, 4 * BLOCK, BLOCK), lambda i: (i, 0, 0)), tile(ATTN_W), tile(ATTN_W),
                  tile(128), tile(128), tile(128), hbm, hbm],
        out_specs=(tile(ATTN_W), tile(KV_W), tile(KV_W), halo, halo, hbm),
        scratch_shapes=[pltpu.VMEM((ho, D_MODEL), F32), pltpu.VMEM((7, ho, D_MODEL), BF16),
                        pltpu.SemaphoreType.DMA((8,)), pltpu.SemaphoreType.DMA((8,)), pltpu.SemaphoreType.DMA((2,))],
        compiler_params=_params(("arbitrary",)),
    )(q, kd, vd, kd, vd, probs, d_attn, prod, rope_c, rope_a, rope_b, g_out, g_out_b)


def _bwd_proj(x, norm_g, dh2, dq, dk, dv, dkh, dvh, dga, db, dgc, dcv, rest, conv_w, w_in_t, rope_c, rope_a, rope_b):
    seq = x.shape[0]
    tb = T_PROJ
    per = tb // T_MIX
    nt = seq // tb

    def body(x_ref, g_ref, dh2_ref, dq_ref, dk_ref, dv_ref, dkh_ref, dvh_ref, dkn_ref, dvn_ref, dga_ref, db_ref,
             dgc_ref, dcv_ref, dcvn_ref, ch_ref, cw_ref, w_ref, c_ref, a_ref, b_ref, gx_ref, gw_hbm, small_ref,
             dp_ref, acc_ref, out_sem):
        i = pl.program_id(0)

        @pl.when(i == 0)
        def _():
            small_ref[...] = jnp.zeros_like(small_ref)
            acc_ref[...] = jnp.zeros_like(acc_ref)

        last = i == nt - 1
        keep = jnp.where(last, 0.0, 1.0)
        pad = jnp.zeros((T_MIX - BLOCK, KV_W), F32)

        def with_halos(main_ref, halo_ref, next_ref):
            parts = []
            for m in range(1, per + 1):
                parts += [pad, halo_ref[m] if m < per else next_ref[0] * keep]
            return main_ref[...] + jnp.concatenate(parts, axis=0)

        dk = with_halos(dk_ref, dkh_ref, dkn_ref)
        dv = with_halos(dv_ref, dvh_ref, dvn_ref)
        dp_ref[:, 0:ATTN_W] = dq_ref[...]
        dp_ref[:, ATTN_W:ATTN_W + KV_W] = _rope(dk, c_ref[...], -a_ref[...], -b_ref[...]).astype(BF16)
        dp_ref[:, ATTN_W + KV_W:ATTN_W + 2 * KV_W] = dv.astype(BF16)
        base = ATTN_W + 2 * KV_W
        dp_ref[:, base:base + 512] = dga_ref[...]
        dp_ref[:, base + 512:base + 1024] = db_ref[...]
        dcv = dcv_ref[...]
        nxt = dcvn_ref[...] * keep
        cw = _conv_rows(cw_ref)
        du = cw[2:3, :] * dcv + cw[1:2, :] * _shift_up(dcv, nxt, 1) + cw[0:1, :] * _shift_up(dcv, nxt, 2)
        dp_ref[:, base + 1024:base + 1536] = (du * ch_ref[:, 512:1024]).astype(BF16)
        dp_ref[:, base + 1536:base + 2048] = (du * ch_ref[:, 0:512]).astype(BF16)
        dp_ref[:, base + 2048:base + 2560] = dgc_ref[...]

        xf = x_ref[...]
        r1 = lax.rsqrt(jnp.mean(xf * xf, axis=-1, keepdims=True) + EPS)
        n1 = xf * r1
        xn = (n1 * g_ref[...]).astype(BF16)
        dxn = jnp.dot(dp_ref[...], w_ref[...], preferred_element_type=F32)
        small_ref[0:1, :] += jnp.sum(dxn * n1, axis=0, keepdims=True)
        dxg = dxn * g_ref[...]
        gx_ref[...] = r1 * (dxg - n1 * jnp.mean(dxg * n1, axis=-1, keepdims=True)) + dh2_ref[...]
        half = 256 * (IN_W // 512)
        parts = [pltpu.make_async_copy(acc_ref.at[rows], gw_hbm.at[rows], out_sem.at[n])
                 for n, rows in enumerate((pl.ds(0, half), pl.ds(half, IN_W - half)))]
        for n in range(IN_W // 256):
            cols = slice(256 * n, 256 * (n + 1))
            acc_ref[cols, :] += lax.dot_general(dp_ref[:, cols], xn, _TN, preferred_element_type=F32)
            if 256 * (n + 1) == half:
                @pl.when(last)
                def _():
                    parts[0].start()

        @pl.when(last)
        def _():
            parts[1].start()
            for cp in parts:
                cp.wait()

    tile = lambda w: pl.BlockSpec((tb, w), lambda i: (i, 0))
    whole = lambda r, w: pl.BlockSpec((r, w), lambda i: (0, 0))
    halo = pl.BlockSpec((per, BLOCK, KV_W), lambda i: (i, 0, 0))
    halo_next = pl.BlockSpec((1, BLOCK, KV_W), lambda i: (jnp.minimum((i + 1) * per, seq // T_MIX - 1), 0, 0))
    next8 = pl.BlockSpec((8, CONV_W), lambda i: (jnp.minimum((i + 1) * (tb // 8), seq // 8 - 1), 0))
    ch = pl.BlockSpec((tb, 1024), lambda i: (i, 1))
    return pl.pallas_call(
        body, name="bwd_proj", grid=(nt,),
        out_shape=(jax.ShapeDtypeStruct((seq, D_MODEL), F32), jax.ShapeDtypeStruct((IN_W, D_MODEL), F32),
                   jax.ShapeDtypeStruct((SMALL_ROWS, D_MODEL), F32)),
        in_specs=[tile(D_MODEL), whole(1, D_MODEL), tile(D_MODEL), tile(ATTN_W), tile(KV_W), tile(KV_W), halo, halo,
                  halo_next, halo_next,
                  tile(ATTN_W), tile(CONV_W), tile(CONV_W), tile(CONV_W), next8, ch, CONV_SPEC,
                  pl.BlockSpec((IN_W, D_MODEL), lambda i: (0, 0), pipeline_mode=pl.Buffered(1)),
                  tile(128), tile(128), tile(128)],
        out_specs=(tile(D_MODEL), pl.BlockSpec(memory_space=pl.ANY), whole(SMALL_ROWS, D_MODEL)),
        scratch_shapes=[pltpu.VMEM((tb, IN_W), BF16), pltpu.VMEM((IN_W, D_MODEL), F32), pltpu.SemaphoreType.DMA((2,))],
        compiler_params=_params(("arbitrary",)),
    )(x, norm_g, dh2, dq, dk, dv, dkh, dvh, dkh, dvh, dga, db, dgc, dcv, dcv, rest, conv_w, w_in_t,
      rope_c, rope_a, rope_b)


def _adamw_step(w, g, m, v):
    m2 = ADAM_B1 * m + (1.0 - ADAM_B1) * g
    v2 = ADAM_B2 * v + (1.0 - ADAM_B2) * jnp.square(g)
    m_hat = m2 / (1.0 - ADAM_B1 ** ADAM_STEP)
    v_hat = v2 / (1.0 - ADAM_B2 ** ADAM_STEP)
    return -ADAM_LR * (m_hat / (jnp.sqrt(v_hat) + ADAM_EPS) + ADAM_WD * w), m2, v2


def _adamw_weights(groups):
    steps = 4

    def body(*refs):
        ins, outs = refs[:4 * len(groups)], refs[4 * len(groups):]
        for k in range(len(groups)):
            res = _adamw_step(*(r[...] for r in ins[4 * k:4 * k + 4]))
            for o_ref, val in zip(outs[3 * k:3 * k + 3], res):
                o_ref[...] = val

    in_specs, out_specs, out_shape = [], [], []
    for w, _, _, _ in groups:
        rows, cols = w.shape
        spec = pl.BlockSpec((rows // steps, cols), lambda i: (i, 0))
        in_specs += [spec] * 4
        out_specs += [spec] * 3
        out_shape += [jax.ShapeDtypeStruct((rows, cols), F32)] * 3
    flat = pl.pallas_call(
        body, name="adamw_weights", grid=(steps,), out_shape=tuple(out_shape), in_specs=in_specs,
        out_specs=tuple(out_specs), compiler_params=_params(("arbitrary",)),
    )(*[a for grp in groups for a in grp])
    return [flat[3 * k:3 * k + 3] for k in range(len(groups))]


def _adamw_small(chip, small, params, m, v):
    def body(chip_ref, small_ref, conv_ref, *refs):
        ins, outs = refs[:12], refs[12:]
        outs[0][...] = jnp.sum(small_ref[6:7, :], axis=-1, keepdims=True)
        grads = (small_ref[0:1, :], small_ref[1:2, :], conv_ref[2:5, :], small_ref[5:6, 0:8])
        for k, g in enumerate(grads):
            outs[1 + k][...] = g
            res = _adamw_step(ins[k][...], g, ins[4 + k][...], ins[8 + k][...])
            for n, val in enumerate(res):
                outs[5 + 4 * n + k][...] = val

    full = lambda a: pl.BlockSpec(a.shape, lambda i, c: (0,) * len(a.shape))
    shapes = [jax.ShapeDtypeStruct(p.shape, F32) for p in params]
    outs = [jax.ShapeDtypeStruct((1, 1), F32)] + shapes * 4
    flat = pl.pallas_call(
        body, name="adamw_small",
        grid_spec=pltpu.PrefetchScalarGridSpec(
            num_scalar_prefetch=1, grid=(1,),
            in_specs=[full(small), pl.BlockSpec((SMALL_ROWS, 128), lambda i, c: (0, c[0]))]
            + [full(a) for a in (*params, *m, *v)],
            out_specs=tuple(full(s) for s in outs)),
        out_shape=tuple(outs), compiler_params=_params(("arbitrary",)),
    )(chip, small, small, *params, *m, *v)
    return flat[0], flat[1:5], [flat[5 + 4 * n:9 + 4 * n] for n in range(3)]


def kernel(x, norm_g, w_in, sinks, conv_w, w_out, final_g, loss_target, m_norm_g, m_w_in, m_sinks, m_conv_w, m_w_out, m_final_g, v_norm_g, v_w_in, v_sinks, v_conv_w, v_w_out, v_final_g):
    seq = x.shape[1]
    x2 = x.reshape(seq, D_MODEL)
    tgt = loss_target.reshape(seq, D_MODEL)
    ng = norm_g.reshape(1, D_MODEL)
    fg = final_g.reshape(1, D_MODEL)
    chip = 2 * lax.axis_index("x") + lax.axis_index("y")

    conv_w8 = jnp.zeros((8, 128), F32).at[0:3].set(conv_w)
    w_in_full = _gather_w_in(w_in.T).reshape(IN_W, D_MODEL)

    q, kd, vd, rest, rope_c, rope_a, rope_b, wo_all, cw_all = _fwd_proj(x2, ng, w_in_full, w_out, conv_w8)
    w_out_full = wo_all.reshape(D_MODEL, D_MODEL)
    probs, d_attn, prod, dga, db, dgc, dcv, dh2, g_wo, g_wo_b, small_m = _fwd_mix(
        x2, q, kd, vd, rest, sinks, cw_all, w_out_full, fg, tgt)
    out_blocks = lambda t: t.reshape(N_CHIPS, W_OUT_BLK, D_MODEL)
    dq, dk, dv, dkh, dvh, grad_w_out = _bwd_mix(
        q, kd, vd, probs, d_attn, prod, rope_c, rope_a, rope_b, out_blocks(g_wo), out_blocks(g_wo_b))
    grad_x, g_wi, small_p = _bwd_proj(x2, ng, dh2, dq, dk, dv, dkh, dvh, dga, db, dgc, dcv, rest, cw_all,
                                      w_in_full, rope_c, rope_a, rope_b)

    g_in_blocks = g_wi.reshape(N_CHIPS, W_IN_BLK, D_MODEL)
    grad_w_in_t, small = _reduce_grads(g_in_blocks, small_m, small_p)

    (upd_wi, upd_wo) = _adamw_weights([(w_in.T, grad_w_in_t, m_w_in.T, v_w_in.T),
                                       (w_out, grad_w_out, m_w_out, v_w_out)])
    row = lambda t: t.reshape(1, -1)
    loss, grads_s, upd_s = _adamw_small(
        chip.reshape(1), small, (ng, fg, conv_w, row(sinks)),
        (row(m_norm_g), row(m_final_g), m_conv_w, row(m_sinks)),
        (row(v_norm_g), row(v_final_g), v_conv_w, row(v_sinks)))

    def named(ng_, fg_, cw_, sk_, wi_t, wo_):
        return [ng_.reshape(D_MODEL), wi_t.T, sk_.reshape(8), cw_, wo_, fg_.reshape(D_MODEL)]

    g_named = named(*grads_s, grad_w_in_t, grad_w_out)
    out = [loss.reshape(()), grad_x.reshape(1, seq, D_MODEL)] + g_named
    for n in range(3):
        out += named(*upd_s[n], upd_wi[n], upd_wo[n])
    return tuple(out)
```
